```python
import math
import jax, jax.numpy as jnp
from jax import lax
import numpy as np

D_MODEL = 1024
BATCH = 8
SEQ = 2048
DEPTH = 2
DEC_BATCH = 128
DEC_SEQ = 1
PAST_LEN = 16384
PAGE_SIZE = 128

GROUP_W = D_MODEL // 4
D_MIX = 4 * GROUP_W
RET_HEADS = 4
RET_DH = GROUP_W // RET_HEADS
ML_HEADS = 4
ML_DH = GROUP_W // ML_HEADS
S5_CH = 16
S5_GROUPS = GROUP_W // S5_CH
S5_P = 64
MEM_LEN = 256
XA_HEADS = 4
XA_DH = GROUP_W // XA_HEADS
CHUNK = 128
ROPE_BASE = 10000.0
EPS = 1e-6
NEG_INF = -1e30
DT_MIN = 1e-3
DT_MAX = 1e-1
SPLIT_SIZES = (GROUP_W, GROUP_W, GROUP_W, GROUP_W,
               GROUP_W, GROUP_W, GROUP_W, GROUP_W, GROUP_W, ML_HEADS, ML_HEADS,
               GROUP_W, GROUP_W,
               GROUP_W, GROUP_W)
D_IN = sum(SPLIT_SIZES)
SPLIT_IDX = tuple(int(i) for i in np.cumsum(SPLIT_SIZES)[:-1])

kernel_name = "hybrid_ret_mlstm_s5_memxattn_step"


def _rms_norm(x, w):
    xf = x.astype(jnp.float32)
    y = xf * lax.rsqrt(jnp.mean(xf * xf, axis=-1, keepdims=True) + EPS)
    return (y * w.astype(jnp.float32)).astype(x.dtype)


def _head_norm(h, gain):
    mu = jnp.mean(h, axis=-1, keepdims=True)
    var = jnp.mean(jnp.square(h - mu), axis=-1, keepdims=True)
    y = (h - mu) * lax.rsqrt(var + EPS)
    return y.reshape(h.shape[0], h.shape[1], -1) * gain.astype(jnp.float32)


def _rope(x, pos):
    half = x.shape[-1] // 2
    inv = ROPE_BASE ** (-jnp.arange(half, dtype=jnp.float32) / half)
    ang = pos.astype(jnp.float32)[:, None] * inv[None, :]
    cos = jnp.cos(ang)[None, :, None, :]
    sin = jnp.sin(ang)[None, :, None, :]
    x1, x2 = x[..., :half], x[..., half:]
    return jnp.concatenate([x1 * cos - x2 * sin, x1 * sin + x2 * cos], axis=-1)


def _chunk_len(T):
    return CHUNK if T % CHUNK == 0 else T


def _to_chunks(t, nc, L):
    B, T, H, d = t.shape
    return t.reshape(B, nc, L, H, d).transpose(1, 0, 3, 2, 4)


def _from_chunks(h, B, T):
    nc, _, H, L, d = h.shape
    return h.transpose(1, 0, 3, 2, 4).reshape(B, T, H, d)


def _retention(q, k, v, s0):
    B, T, H, d = q.shape
    L = _chunk_len(T)
    nc = T // L
    lg = jnp.log1p(-jnp.power(2.0, -5.0 - jnp.arange(H, dtype=jnp.float32)))[:, None]
    idx = jnp.arange(L, dtype=jnp.float32)
    diff = idx[:, None] - idx[None, :]
    decay = jnp.where(diff >= 0, jnp.exp(lg[:, :, None] * jnp.maximum(diff, 0.0)), 0.0)
    q_decay = jnp.exp(lg * (idx + 1.0))[..., None]
    k_decay = jnp.exp(lg * (L - 1.0 - idx))[..., None]
    chunk_decay = jnp.exp(lg * L)[..., None]
    qc, kc, vc = (_to_chunks(t, nc, L) for t in (q, k, v))

    def step(s, inp):
        qi, ki, vi = inp
        inner = jnp.einsum('bhld,bhmd->bhlm', qi, ki) * decay
        o = (jnp.einsum('bhlm,bhme->bhle', inner, vi)
             + jnp.einsum('bhld,bhde->bhle', qi * q_decay, s))
        s = s * chunk_decay + jnp.einsum('bhld,bhle->bhde', ki * k_decay, vi)
        return s, o

    s, o = lax.scan(step, s0, (qc, kc, vc))
    return _from_chunks(o, B, T), s


def _mlstm(q, k, v, ig, lf, c0, n0, m0):
    B, T, H, d = q.shape
    L = _chunk_len(T)
    nc = T // L
    qc, kc, vc = (_to_chunks(t, nc, L) for t in (q, k, v))
    ic = ig.reshape(B, nc, L, H).transpose(1, 0, 3, 2)
    fc = lf.reshape(B, nc, L, H).transpose(1, 0, 3, 2)
    causal = jnp.tril(jnp.ones((L, L), dtype=bool))

    def step(carry, inp):
        c, n, m = carry
        qi, ki, vi, ii, fi = inp
        b = jnp.cumsum(fi, axis=-1)
        a = b + m[..., None]
        dlog = jnp.where(causal, b[..., :, None] - b[..., None, :] + ii[..., None, :], NEG_INF)
        mt = jnp.maximum(a, jnp.max(dlog, axis=-1))
        w_intra = jnp.exp(dlog - mt[..., None])
        w_state = jnp.exp(a - mt)
        s = jnp.einsum('bhld,bhmd->bhlm', qi, ki) * w_intra
        num = (jnp.einsum('bhlm,bhme->bhle', s, vi)
               + w_state[..., None] * jnp.einsum('bhed,bhld->bhle', c, qi))
        den = jnp.sum(s, axis=-1) + w_state * jnp.einsum('bhd,bhld->bhl', n, qi)
        h = num / jnp.maximum(jnp.abs(den), jnp.exp(-mt))[..., None]
        wl, wsl = w_intra[..., -1, :], w_state[..., -1]
        c = wsl[..., None, None] * c + jnp.einsum('bhm,bhme,bhmd->bhed', wl, vi, ki)
        n = wsl[..., None] * n + jnp.einsum('bhm,bhmd->bhd', wl, ki)
        return (c, n, mt[..., -1]), h

    (c, n, m), h = lax.scan(step, (c0, n0, m0), (qc, kc, vc, ic, fc))
    return _from_chunks(h, B, T), c, n, m


def _s5(u, x0_re, x0_im, a_re, a_im, log_dt, b_re, b_im, c_re, c_im, d_skip):
    f32 = jnp.float32
    a_re, a_im, b_re, b_im, c_re, c_im = (t.astype(f32) for t in (a_re, a_im, b_re, b_im, c_re, c_im))
    B, T, _ = u.shape
    ug = u.reshape(B, T, S5_GROUPS, S5_CH)
    dt = jnp.exp(log_dt.astype(f32))[:, None]
    mag = jnp.exp(a_re * dt)
    ab_re, ab_im = mag * jnp.cos(a_im * dt), mag * jnp.sin(a_im * dt)
    den = a_re * a_re + a_im * a_im
    nr, ni = ab_re - 1.0, ab_im
    f_re = (nr * a_re + ni * a_im) / den
    f_im = (ni * a_re - nr * a_im) / den
    bb_re = f_re[..., None] * b_re - f_im[..., None] * b_im
    bb_im = f_re[..., None] * b_im + f_im[..., None] * b_re
    bu_re = jnp.einsum('gpc,btgc->btgp', bb_re, ug)
    bu_im = jnp.einsum('gpc,btgc->btgp', bb_im, ug)
    x0_re, x0_im = x0_re.astype(f32), x0_im.astype(f32)
    bu_re = bu_re.at[:, 0].add(ab_re * x0_re - ab_im * x0_im)
    bu_im = bu_im.at[:, 0].add(ab_re * x0_im + ab_im * x0_re)
    A_re = jnp.broadcast_to(ab_re, bu_re.shape)
    A_im = jnp.broadcast_to(ab_im, bu_im.shape)

    def combine(e1, e2):
        a1r, a1i, b1r, b1i = e1
        a2r, a2i, b2r, b2i = e2
        return (a2r * a1r - a2i * a1i, a2r * a1i + a2i * a1r,
                a2r * b1r - a2i * b1i + b2r, a2r * b1i + a2i * b1r + b2i)

    _, _, xr, xi = lax.associative_scan(combine, (A_re, A_im, bu_re, bu_im), axis=1)
    y = jnp.einsum('gcp,btgp->btgc', c_re, xr) - jnp.einsum('gcp,btgp->btgc', c_im, xi)
    y = y.reshape(B, T, -1) + d_skip.astype(f32) * u
    return y, xr[:, -1], xi[:, -1]


def _mem_attend(q, mk, mv):
    s = jnp.einsum('bthd,bmhd->bhtm', q, mk) * (q.shape[-1] ** -0.5)
    p = jax.nn.softmax(s, axis=-1)
    return jnp.einsum('bhtm,bmhd->bthd', p, mv)


def _mixer_layer(x, pos, ret_s, ml_c, ml_n, ml_m, s5_re, s5_im, mem_k, mem_v,
                 norm_w, w_in, ret_gn, ml_b_i, ml_b_f, ml_gn,
                 s5_a_re, s5_a_im, s5_log_dt, s5_b_re, s5_b_im, s5_c_re, s5_c_im, s5_d, s5_w_glu,
                 w_out):
    f32 = jnp.float32
    B, T, _ = x.shape
    hn = _rms_norm(x, norm_w)
    proj = jnp.einsum('btd,de->bte', hn, w_in)
    (r_q, r_k, r_v, r_g, m_q, m_k, m_v, m_o, m_g, m_i, m_f,
     s_u, s_g, a_q, a_g) = jnp.split(proj, SPLIT_IDX, axis=-1)

    def heads(t, H):
        return t.astype(f32).reshape(B, T, H, -1)

    rq = _rope(heads(r_q, RET_HEADS), pos)
    rk = _rope(heads(r_k, RET_HEADS), pos) * (RET_DH ** -0.5)
    ro, ret_new = _retention(rq, rk, heads(r_v, RET_HEADS), ret_s.astype(f32))
    ret_out = _head_norm(ro, ret_gn) * jax.nn.silu(r_g.astype(f32))

    ig = m_i.astype(f32) + ml_b_i.astype(f32)
    lf = jax.nn.log_sigmoid(m_f.astype(f32) + ml_b_f.astype(f32))
    mh, c_new, n_new, m_new = _mlstm(heads(m_q, ML_HEADS), heads(m_k, ML_HEADS) * (ML_DH ** -0.5),
                                     heads(m_v, ML_HEADS), ig, lf,
                                     ml_c.astype(f32), ml_n.astype(f32), ml_m.astype(f32))
    mh = mh * jax.nn.sigmoid(heads(m_o, ML_HEADS))
    ml_out = _head_norm(mh, ml_gn) * jax.nn.silu(m_g.astype(f32))

    sy, s5r_new, s5i_new = _s5(s_u.astype(f32), s5_re, s5_im, s5_a_re, s5_a_im, s5_log_dt,
                               s5_b_re, s5_b_im, s5_c_re, s5_c_im, s5_d)
    sy = jax.nn.gelu(sy)
    sy = sy * jax.nn.sigmoid(jnp.einsum('btc,ce->bte', sy, s5_w_glu.astype(f32)))
    s5_out = sy * jax.nn.silu(s_g.astype(f32))

    xa = _mem_attend(heads(a_q, XA_HEADS), mem_k.astype(f32), mem_v.astype(f32)).reshape(B, T, -1)
    xa_out = xa * jax.nn.silu(a_g.astype(f32))

    mix = jnp.concatenate([ret_out, ml_out, s5_out, xa_out], axis=-1).astype(x.dtype)
    y = x + jnp.einsum('bte,ed->btd', mix, w_out)
    return y, ret_new, c_new, n_new, m_new, s5r_new, s5i_new


def setup_inputs(seed: int = 0) -> dict:
    key = jax.random.key(seed)
    ks = iter(jax.random.split(key, 40))
    f32 = jnp.float32
    G, P = S5_GROUPS, S5_P

    def nrm(shape, scale=1.0):
        return scale * jax.random.normal(next(ks), shape, f32)

    n_idx = jnp.arange(P, dtype=f32)
    inp = {}
    inp["x_prompt"] = nrm((BATCH, SEQ, D_MODEL))
    inp["x_sample"] = nrm((DEC_BATCH, DEC_SEQ, D_MODEL))
    inp["mem_prompt"] = nrm((BATCH, MEM_LEN, D_MODEL))
    inp["state_ret"] = nrm((DEPTH, DEC_BATCH, RET_HEADS, RET_DH, RET_DH), 0.3)
    inp["state_mlstm_c"] = nrm((DEPTH, DEC_BATCH, ML_HEADS, ML_DH, ML_DH), 0.3)
    inp["state_mlstm_n"] = nrm((DEPTH, DEC_BATCH, ML_HEADS, ML_DH), 0.3)
    inp["state_mlstm_m"] = jax.random.uniform(next(ks), (DEPTH, DEC_BATCH, ML_HEADS), f32, 0.0, 3.0)
    inp["state_s5_re"] = nrm((DEPTH, DEC_BATCH, G, P), 0.1)
    inp["state_s5_im"] = nrm((DEPTH, DEC_BATCH, G, P), 0.1)
    inp["cache_mem_k"] = nrm((DEPTH, DEC_BATCH, MEM_LEN, XA_HEADS, XA_DH))
    inp["cache_mem_v"] = nrm((DEPTH, DEC_BATCH, MEM_LEN, XA_HEADS, XA_DH))
    inp["norm_w"] = 1.0 + nrm((DEPTH, D_MODEL), 0.02)
    inp["w_in"] = nrm((DEPTH, D_MODEL, D_IN), D_MODEL ** -0.5)
    inp["ret_gn"] = 1.0 + nrm((DEPTH, GROUP_W), 0.02)
    inp["ml_b_i"] = nrm((DEPTH, ML_HEADS), 0.1)
    inp["ml_b_f"] = jnp.linspace(3.0, 6.0, ML_HEADS, dtype=f32) + nrm((DEPTH, ML_HEADS), 0.1)
    inp["ml_gn"] = 1.0 + nrm((DEPTH, GROUP_W), 0.02)
    inp["s5_a_re"] = -0.5 * jnp.exp(nrm((DEPTH, G, P), 0.01))
    inp["s5_a_im"] = jnp.pi * n_idx + nrm((DEPTH, G, P), 0.01)
    inp["s5_log_dt"] = jax.random.uniform(next(ks), (DEPTH, G), f32, math.log(DT_MIN), math.log(DT_MAX))
    inp["s5_b_re"] = nrm((DEPTH, G, P, S5_CH), (2 * S5_CH) ** -0.5)
    inp["s5_b_im"] = nrm((DEPTH, G, P, S5_CH), (2 * S5_CH) ** -0.5)
    inp["s5_c_re"] = nrm((DEPTH, G, S5_CH, P), P ** -0.5)
    inp["s5_c_im"] = nrm((DEPTH, G, S5_CH, P), P ** -0.5)
    inp["s5_d"] = nrm((DEPTH, GROUP_W))
    inp["s5_w_glu"] = nrm((DEPTH, GROUP_W, GROUP_W), GROUP_W ** -0.5)
    inp["w_mem_k"] = nrm((DEPTH, D_MODEL, GROUP_W), D_MODEL ** -0.5)
    inp["w_mem_v"] = nrm((DEPTH, D_MODEL, GROUP_W), D_MODEL ** -0.5)
    inp["w_out"] = nrm((DEPTH, D_MIX, D_MODEL), D_MIX ** -0.5)
    inp["final_norm_w"] = 1.0 + nrm((D_MODEL,), 0.02)
    return inp


def reference(x_prompt, x_sample, mem_prompt, state_ret, state_mlstm_c, state_mlstm_n, state_mlstm_m,
              state_s5_re, state_s5_im, cache_mem_k, cache_mem_v,
              norm_w, w_in, ret_gn, ml_b_i, ml_b_f, ml_gn,
              s5_a_re, s5_a_im, s5_log_dt, s5_b_re, s5_b_im, s5_c_re, s5_c_im, s5_d, s5_w_glu,
              w_mem_k, w_mem_v, w_out, final_norm_w):
    f32 = jnp.float32
    Bp, Tp, _ = x_prompt.shape
    Bs, Ts, _ = x_sample.shape
    pos_p = jnp.arange(Tp, dtype=jnp.int32)
    pos_s = PAST_LEN + jnp.arange(Ts, dtype=jnp.int32)
    z_ret = jnp.zeros((Bp,) + state_ret.shape[2:], f32)
    z_c = jnp.zeros((Bp,) + state_mlstm_c.shape[2:], f32)
    z_n = jnp.zeros((Bp,) + state_mlstm_n.shape[2:], f32)
    z_m = jnp.zeros((Bp,) + state_mlstm_m.shape[2:], f32)
    z_sr = jnp.zeros((Bp,) + state_s5_re.shape[2:], f32)
    z_si = jnp.zeros((Bp,) + state_s5_im.shape[2:], f32)

    hp, hs = x_prompt, x_sample
    st_p = [[] for _ in range(6)]
    st_s = [[] for _ in range(6)]
    mk_list, mv_list = [], []
    for l in range(DEPTH):
        w = (norm_w[l], w_in[l], ret_gn[l], ml_b_i[l], ml_b_f[l], ml_gn[l],
             s5_a_re[l], s5_a_im[l], s5_log_dt[l], s5_b_re[l], s5_b_im[l], s5_c_re[l], s5_c_im[l],
             s5_d[l], s5_w_glu[l], w_out[l])
        mk_p = jnp.einsum('bmd,de->bme', mem_prompt, w_mem_k[l]).reshape(Bp, -1, XA_HEADS, XA_DH)
        mv_p = jnp.einsum('bmd,de->bme', mem_prompt, w_mem_v[l]).reshape(Bp, -1, XA_HEADS, XA_DH)
        mk_list.append(mk_p)
        mv_list.append(mv_p)
        hp, *sp = _mixer_layer(hp, pos_p, z_ret, z_c, z_n, z_m, z_sr, z_si, mk_p, mv_p, *w)
        hs, *ss = _mixer_layer(hs, pos_s, state_ret[l], state_mlstm_c[l], state_mlstm_n[l],
                               state_mlstm_m[l], state_s5_re[l], state_s5_im[l],
                               cache_mem_k[l], cache_mem_v[l], *w)
        for i in range(6):
            st_p[i].append(sp[i])
            st_s[i].append(ss[i])

    y_prompt = _rms_norm(hp, final_norm_w)
    y_sample = _rms_norm(hs, final_norm_w)
    dts = (state_ret.dtype, state_mlstm_c.dtype, state_mlstm_n.dtype, state_mlstm_m.dtype,
           state_s5_re.dtype, state_s5_im.dtype)
    P_ = [jnp.stack(st_p[i]).astype(dts[i]) for i in range(6)]
    S_ = [jnp.stack(st_s[i]).astype(dts[i]) for i in range(6)]
    memk_p = jnp.stack(mk_list)
    memv_p = jnp.stack(mv_list)
    return (y_prompt, y_sample, P_[0], S_[0], P_[1], S_[1], P_[2], S_[2], P_[3], S_[3],
            P_[4], S_[4], P_[5], S_[5], memk_p, memv_p)
```

```python
import functools
import math

import numpy as np
import jax
import jax.numpy as jnp
from jax import lax
from jax.experimental import pallas as pl
from jax.experimental.pallas import tpu as pltpu

F32 = jnp.float32
BF16 = jnp.bfloat16

D = 1024
GW = 256
NH = 4
DH = 64
L = 128
S5G = 16
S5P = 64
S5C = 16
NS = S5G * S5P
MEM = 256
EPS = 1e-6
NEG_INF = -1e30
ROPE_BASE = 10000.0
PAST_LEN = 16384

TB = 512
SB = 8
NBLK = 14
DP = NBLK * GW
(RQ, RK, RV, RG, MQ, MK, MV, MO, MG, SU, SG, AQ, AG, GT) = range(NBLK)

VMEM_LIMIT = 52 * 1024 * 1024


def _dot(a, b):
    return jnp.dot(a, b, preferred_element_type=F32)


def _dot_nt(a, b):
    return lax.dot_general(a, b, (((1,), (1,)), ((), ())), preferred_element_type=F32)


def _dot_tn(a, b):
    return lax.dot_general(a, b, (((0,), (0,)), ((), ())), preferred_element_type=F32)


def _split2(x):
    hi = x.astype(BF16)
    lo = (x - hi.astype(F32)).astype(BF16)
    return hi, lo


def _dot_x2(x, w):
    hi, lo = _split2(x)
    return _dot(hi, w) + _dot(lo, w)


def _dot_x3(w, x):
    hi = x.astype(BF16)
    r = x - hi.astype(F32)
    mid = r.astype(BF16)
    lo = (r - mid.astype(F32)).astype(BF16)
    return _dot(w, hi) + _dot(w, mid) + _dot(w, lo)


def _sigmoid(x):
    return 1.0 / (1.0 + jnp.exp(-x))


def _silu(x):
    return x * _sigmoid(x)


def _log_sigmoid(x):
    return jnp.minimum(x, 0.0) - jnp.log1p(jnp.exp(-jnp.abs(x)))


def _gelu_tanh(x):
    c = math.sqrt(2.0 / math.pi)
    return x * (0.5 * (1.0 + jnp.tanh(c * (x + 0.044715 * (x * x * x)))))


def _lane_head(n):
    return lax.broadcasted_iota(jnp.int32, (1, n), 1) // DH


def _head_masks():
    lh = _lane_head(GW)
    return [lh == h for h in range(NH)]


def _block_diag_mask():
    r = lax.broadcasted_iota(jnp.int32, (GW, GW), 0) // DH
    c = lax.broadcasted_iota(jnp.int32, (GW, GW), 1) // DH
    return r == c


def _avg_matrix():
    return jnp.where(_block_diag_mask(), 1.0 / DH, 0.0).astype(BF16)


def _ones_matrix():
    return jnp.where(_block_diag_mask(), 1.0, 0.0).astype(BF16)


def _rope(x, cos, s1, s2):
    outs = []
    for j in range(2):
        sl = slice(j * 128, (j + 1) * 128)
        xs = x[:, sl]
        outs.append(xs * cos[:, sl] + pltpu.roll(xs, 32, 1) * s1[:, sl]
                    + pltpu.roll(xs, 96, 1) * s2[:, sl])
    return jnp.concatenate(outs, axis=1)


def _head_norm(x, gain, avg):
    mu = _dot_x2(x, avg)
    d = x - mu
    var = _dot_x2(d * d, avg)
    return d * lax.rsqrt(var + EPS) * gain


def _rms_norm(x, w):
    ms = jnp.mean(x * x, axis=-1, keepdims=True)
    return x * lax.rsqrt(ms + EPS) * w


def _stack_heads(x, hm):
    return jnp.concatenate([jnp.where(hm[h], x, 0.0) for h in range(NH)], axis=0)


def _fold_heads(r, hm, rows):
    out = jnp.where(hm[0], r[0:rows], 0.0)
    for h in range(1, NH):
        out = out + jnp.where(hm[h], r[h * rows:(h + 1) * rows], 0.0)
    return out


def _s5_disc_kernel(are_ref, aim_ref, ldt_ref, bre_ref, bim_ref,
                    abre_ref, abim_ref, bbre_ref, bbim_ref):
    a_re = are_ref[...]
    a_im = aim_ref[...]
    dt = jnp.exp(ldt_ref[...])
    mag = jnp.exp(a_re * dt)
    ab_re = mag * jnp.cos(a_im * dt)
    ab_im = mag * jnp.sin(a_im * dt)
    den = a_re * a_re + a_im * a_im
    nr = ab_re - 1.0
    ni = ab_im
    f_re = (nr * a_re + ni * a_im) / den
    f_im = (ni * a_re - nr * a_im) / den
    b_re = bre_ref[...]
    b_im = bim_ref[...]
    abre_ref[...] = ab_re
    abim_ref[...] = ab_im
    bbre_ref[...] = f_re * b_re - f_im * b_im
    bbim_ref[...] = f_re * b_im + f_im * b_re


def _s5_discretise(a_re, a_im, log_dt, b_re, b_im):
    depth = a_re.shape[0]
    n = depth * S5G
    are = a_re.reshape(n, 1, S5P)
    aim = a_im.reshape(n, 1, S5P)
    ldt = log_dt.reshape(n, 1, 1)
    bre = jnp.swapaxes(b_re, -1, -2).reshape(n, S5C, S5P)
    bim = jnp.swapaxes(b_im, -1, -2).reshape(n, S5C, S5P)
    out_shape = (jax.ShapeDtypeStruct((n, 1, S5P), F32), jax.ShapeDtypeStruct((n, 1, S5P), F32),
                 jax.ShapeDtypeStruct((n, S5C, S5P), F32), jax.ShapeDtypeStruct((n, S5C, S5P), F32))
    ab_re, ab_im, bb_re, bb_im = pl.pallas_call(
        _s5_disc_kernel, out_shape=out_shape, name="s5_discretise")(are, aim, ldt, bre, bim)
    ab_re = ab_re.reshape(depth, 1, NS)
    ab_im = ab_im.reshape(depth, 1, NS)
    eye = jnp.eye(S5G, dtype=F32)
    bb_re = bb_re.reshape(depth, S5G, S5C, S5P)
    bb_im = bb_im.reshape(depth, S5G, S5C, S5P)
    bt_re = jnp.einsum('lgcp,gh->lgchp', bb_re, eye).reshape(depth, GW, NS)
    bt_im = jnp.einsum('lgcp,gh->lgchp', bb_im, eye).reshape(depth, GW, NS)
    bt = jnp.concatenate([bt_re, bt_im], axis=-1).astype(BF16)
    return ab_re, ab_im, bt


def _s5_out_matrix(c_re, c_im):
    depth = c_re.shape[0]
    eye = jnp.eye(S5G, dtype=F32)
    ct_re = jnp.einsum('lgcp,gh->lgphc', c_re, eye).reshape(depth, NS, GW)
    ct_im = jnp.einsum('lgcp,gh->lgphc', c_im, eye).reshape(depth, NS, GW)
    return jnp.concatenate([ct_re, -ct_im], axis=1).astype(BF16)


def _prompt_kernel(x_ref, mem_ref, normw_ref, win_ref, wout_ref, wmk_ref, wmv_ref,
                   cos_ref, s1_ref, s2_ref, dec_ref, qdec_ref, kdec_ref, cdec_ref,
                   retgn_ref, mlgn_ref, bi_ref, bf_ref,
                   are_ref, aim_ref, bt_ref, ct_ref, dsk_ref, wglu_ref, fnw_ref,
                   y_ref, rets_ref, mlc_ref, mln_ref, mlm_ref, s5re_ref, s5im_ref,
                   memk_ref, memv_ref,
                   proj_ref, mix_ref, s_ref, c_ref, n_ref, m_ref, xre_ref, xim_ref,
                   mk_ref, mv_ref, bu_ref, xcat_ref, *, last_layer, n_tblocks):
    t = pl.program_id(1)
    hm = _head_masks()
    bd = _block_diag_mask()
    avg = _avg_matrix()
    lane128 = lax.broadcasted_iota(jnp.int32, (1, 128), 1)
    row_i = lax.broadcasted_iota(jnp.int32, (L, 128), 0)
    col_i = lax.broadcasted_iota(jnp.int32, (L, 128), 1)
    causal = row_i >= col_i
    tri = jnp.where(causal, 1.0, 0.0).astype(BF16)

    @pl.when(t == 0)
    def _init():
        s_ref[...] = jnp.zeros_like(s_ref)
        c_ref[...] = jnp.zeros_like(c_ref)
        n_ref[...] = jnp.zeros_like(n_ref)
        m_ref[...] = jnp.zeros_like(m_ref)
        xre_ref[...] = jnp.zeros_like(xre_ref)
        xim_ref[...] = jnp.zeros_like(xim_ref)
        memb = mem_ref[0].astype(BF16)
        mk = _dot(memb, wmk_ref[...])
        mv = _dot(memb, wmv_ref[...])
        memk_ref[0] = mk
        memv_ref[0] = mv
        mk_ref[...] = mk.astype(BF16)
        mv_ref[...] = mv.astype(BF16)

    x = x_ref[0]
    hn = _rms_norm(x, normw_ref[...]).astype(BF16)
    proj_ref[...] = _dot(hn, win_ref[...])

    def chunk(ci, carry):
        r0 = pl.multiple_of(ci * L, L)
        rows = pl.ds(r0, L)

        def P(blk):
            return proj_ref[rows, blk * GW:(blk + 1) * GW]

        cosr = cos_ref[rows, :]
        s1 = s1_ref[rows, :]
        s2 = s2_ref[rows, :]
        rq = _rope(P(RQ), cosr, s1, s2)
        rk = _rope(P(RK), cosr, s1, s2)
        rv = P(RV).astype(BF16)
        inner = _dot_nt(_stack_heads(rq, hm).astype(BF16), rk.astype(BF16)) * dec_ref[...]
        ro = _fold_heads(_dot(inner.astype(BF16), rv), hm, L)
        s_old = s_ref[...]
        ro = ro + _dot((rq * qdec_ref[...]).astype(BF16), s_old.astype(BF16))
        upd = _dot_tn((rk * kdec_ref[...]).astype(BF16), rv)
        s_ref[...] = s_old * cdec_ref[...] + jnp.where(bd, upd, 0.0)
        ret_out = _head_norm(ro, retgn_ref[...], avg) * _silu(P(RG))

        g = P(GT)
        ig = g[:, :128] + bi_ref[...]
        lf = _log_sigmoid(g[:, 128:] + bf_ref[...])
        b = _dot_x3(tri, lf)
        a = b + m_ref[0:1, :]
        gd = ig - b
        gdt = gd.T
        mq = P(MQ).astype(BF16)
        mk_ = P(MK)
        mv_ = P(MV).astype(BF16)
        sraw = _dot_nt(_stack_heads(P(MQ), hm).astype(BF16), mk_.astype(BF16))
        qc = _dot(mq, c_ref[...].astype(BF16))
        qn = _dot(mq, n_ref[...].astype(BF16))
        s_parts, mts, wss, dens = [], [], [], []
        for h in range(NH):
            bcol = b[:, h:h + 1]
            acol = a[:, h:h + 1]
            dlog = jnp.where(causal, bcol + gdt[h:h + 1, :], NEG_INF)
            mt = jnp.maximum(acol, jnp.max(dlog, axis=-1, keepdims=True))
            wi = jnp.exp(dlog - mt)
            ws = jnp.exp(acol - mt)
            s_h = sraw[h * L:(h + 1) * L] * wi
            dens.append(jnp.sum(s_h, axis=-1, keepdims=True) + ws * qn[:, h:h + 1])
            s_parts.append(s_h)
            mts.append(mt)
            wss.append(ws)
        r = _dot(jnp.concatenate(s_parts, axis=0).astype(BF16), mv_)
        mh = jnp.zeros((L, GW), F32)
        kw = jnp.zeros((L, GW), F32)
        wsl256 = jnp.zeros((1, GW), F32)
        wsl128 = jnp.zeros((1, 128), F32)
        m_new = jnp.zeros((1, 128), F32)
        for h in range(NH):
            dn = jnp.maximum(jnp.abs(dens[h]), jnp.exp(-mts[h]))
            hh = (r[h * L:(h + 1) * L] + wss[h] * qc) / dn
            mh = mh + jnp.where(hm[h], hh, 0.0)
            mt_last = mts[h][L - 1:L, :]
            wl = jnp.exp(gd[:, h:h + 1] + (b[L - 1:L, h:h + 1] - mt_last))
            kw = kw + jnp.where(hm[h], mk_ * wl, 0.0)
            wsl = wss[h][L - 1:L, :]
            wsl256 = wsl256 + jnp.where(hm[h], wsl, 0.0)
            wsl128 = wsl128 + jnp.where(lane128 == h, wsl, 0.0)
            m_new = m_new + jnp.where(lane128 == h, mt_last, 0.0)
        vaug = jnp.concatenate([mv_, jnp.ones((L, 128), BF16)], axis=1)
        u = _dot_tn(kw.astype(BF16), vaug)
        c_ref[...] = c_ref[...] * wsl256 + jnp.where(bd, u[:, :GW], 0.0)
        nmask = (lax.broadcasted_iota(jnp.int32, (GW, 128), 0) // DH
                 == lax.broadcasted_iota(jnp.int32, (GW, 128), 1))
        n_ref[...] = n_ref[...] * wsl128 + jnp.where(nmask, u[:, GW:], 0.0)
        m_ref[...] = jnp.broadcast_to(m_new, m_ref.shape)
        mh = mh * _sigmoid(P(MO))
        ml_out = _head_norm(mh, mlgn_ref[...], avg) * _silu(P(MG))

        su = P(SU)
        bu_ref[...] = _dot(su.astype(BF16), bt_ref[...])
        for j in range(NS // 128):
            lre = slice(j * 128, (j + 1) * 128)
            lim = slice(NS + j * 128, NS + (j + 1) * 128)
            pr = are_ref[:, lre]
            pi = aim_ref[:, lre]
            cr = xre_ref[0:1, lre]
            ci_ = xim_ref[0:1, lre]
            first = row_i == 0
            xr = bu_ref[:, lre] + jnp.where(first, pr * cr - pi * ci_, 0.0)
            xi = bu_ref[:, lim] + jnp.where(first, pr * ci_ + pi * cr, 0.0)
            for k in range(7):
                d = 1 << k
                keep = row_i >= d
                sr = jnp.where(keep, pltpu.roll(xr, d, 0), 0.0)
                si = jnp.where(keep, pltpu.roll(xi, d, 0), 0.0)
                xr, xi = xr + pr * sr - pi * si, xi + pr * si + pi * sr
                pr, pi = pr * pr - pi * pi, 2.0 * (pr * pi)
            xre_ref[:, lre] = jnp.broadcast_to(xr[L - 1:L, :], (8, 128))
            xim_ref[:, lre] = jnp.broadcast_to(xi[L - 1:L, :], (8, 128))
            xcat_ref[:, lre] = xr.astype(BF16)
            xcat_ref[:, lim] = xi.astype(BF16)
        sy = _dot(xcat_ref[...], ct_ref[...]) + dsk_ref[...] * su
        sy = _gelu_tanh(sy)
        sy = sy * _sigmoid(_dot(sy.astype(BF16), wglu_ref[...]))
        s5_out = sy * _silu(P(SG))

        sc = _dot_nt(_stack_heads(P(AQ), hm).astype(BF16), mk_ref[...])
        e = jnp.exp(sc - jnp.max(sc, axis=-1, keepdims=True))
        p = e / jnp.sum(e, axis=-1, keepdims=True)
        xa = _fold_heads(_dot(p.astype(BF16), mv_ref[...]), hm, L)
        xa_out = xa * _silu(P(AG))

        mix_ref[rows, :] = jnp.concatenate([ret_out, ml_out, s5_out, xa_out], axis=1).astype(BF16)
        return carry

    lax.fori_loop(0, TB // L, chunk, 0)

    y = x + _dot(mix_ref[...], wout_ref[...])
    if last_layer:
        y = _rms_norm(y, fnw_ref[...])
    y_ref[0] = y

    @pl.when(t == n_tblocks - 1)
    def _final():
        rets_ref[0] = s_ref[...]
        mlc_ref[0] = c_ref[...]
        mln_ref[0] = n_ref[...]
        mlm_ref[0] = m_ref[...]
        s5re_ref[0] = xre_ref[...]
        s5im_ref[0] = xim_ref[...]


def _prompt_layer(x, mem, w, consts, last_layer):
    bsz, seq, _ = x.shape
    nt = seq // TB
    full = lambda shape: pl.BlockSpec(shape, lambda b, t: (0,) * len(shape),
                                      pipeline_mode=pl.Buffered(1))
    tok = lambda width: pl.BlockSpec((TB, width), lambda b, t: (t, 0))
    per_b = lambda r, c: pl.BlockSpec((1, r, c), lambda b, t: (b, 0, 0))
    in_specs = [
        pl.BlockSpec((1, TB, D), lambda b, t: (b, t, 0)),
        per_b(MEM, D),
        full((1, D)), full((D, DP)), full((D, D)), full((D, GW)), full((D, GW)),
        tok(GW), tok(GW), tok(GW),
        full((NH * L, L)), full((L, GW)), full((L, GW)), full((1, GW)),
        full((1, GW)), full((1, GW)), full((1, 128)), full((1, 128)),
        full((1, NS)), full((1, NS)), full((GW, 2 * NS)), full((2 * NS, GW)),
        full((1, GW)), full((GW, GW)), full((1, D)),
    ]
    out_shape = (
        jax.ShapeDtypeStruct((bsz, seq, D), F32),
        jax.ShapeDtypeStruct((bsz, GW, GW), F32),
        jax.ShapeDtypeStruct((bsz, GW, GW), F32),
        jax.ShapeDtypeStruct((bsz, GW, 128), F32),
        jax.ShapeDtypeStruct((bsz, 8, 128), F32),
        jax.ShapeDtypeStruct((bsz, 8, NS), F32),
        jax.ShapeDtypeStruct((bsz, 8, NS), F32),
        jax.ShapeDtypeStruct((bsz, MEM, GW), F32),
        jax.ShapeDtypeStruct((bsz, MEM, GW), F32),
    )
    out_specs = (
        pl.BlockSpec((1, TB, D), lambda b, t: (b, t, 0)),
        per_b(GW, GW), per_b(GW, GW), per_b(GW, 128), per_b(8, 128),
        per_b(8, NS), per_b(8, NS), per_b(MEM, GW), per_b(MEM, GW),
    )
    scratch = [
        pltpu.VMEM((TB, DP), F32),
        pltpu.VMEM((TB, D), BF16),
        pltpu.VMEM((GW, GW), F32),
        pltpu.VMEM((GW, GW), F32),
        pltpu.VMEM((GW, 128), F32),
        pltpu.VMEM((8, 128), F32),
        pltpu.VMEM((8, NS), F32),
        pltpu.VMEM((8, NS), F32),
        pltpu.VMEM((MEM, GW), BF16),
        pltpu.VMEM((MEM, GW), BF16),
        pltpu.VMEM((L, 2 * NS), F32),
        pltpu.VMEM((L, 2 * NS), BF16),
    ]
    kern = functools.partial(_prompt_kernel, last_layer=last_layer, n_tblocks=nt)
    return pl.pallas_call(
        kern, grid=(bsz, nt), in_specs=in_specs, out_specs=out_specs, out_shape=out_shape,
        scratch_shapes=scratch, name="prompt_layer",
        compiler_params=pltpu.CompilerParams(
            dimension_semantics=("arbitrary", "arbitrary"), vmem_limit_bytes=VMEM_LIMIT),
    )(x, mem, w["norm_w"], w["w_in"], w["w_out"], w["w_mem_k"], w["w_mem_v"],
      consts["cos_p"], consts["s1_p"], consts["s2_p"],
      consts["dec"], consts["qdec"], consts["kdec"], consts["cdec"],
      w["ret_gn"], w["ml_gn"], w["b_i"], w["b_f"],
      w["ab_re"], w["ab_im"], w["bt"], w["ct"], w["s5_d"], w["w_glu"], w["final_norm_w"])


def _sample_kernel(x_ref, normw_ref, win_ref, wout_ref, cos_ref, s1_ref, s2_ref, gam_ref, gcol_ref,
                   retgn_ref, mlgn_ref, bi_ref, bf_ref,
                   are_ref, aim_ref, bt_ref, ct_ref, dsk_ref, wglu_ref, fnw_ref,
                   m0_ref, n0_ref, x0re_ref, x0im_ref,
                   rets_ref, mlc_ref, ck_ref, cv_ref,
                   y_ref, retn_ref, mlcn_ref, mlnn_ref, mlmn_ref, s5re_ref, s5im_ref,
                   proj_ref, qr8_ref, kr8_ref, vrh_ref, qmh_ref, vw8_ref, kmh_ref, a8_ref, qa8_ref,
                   ost_ref, cq_ref, xa_ref, wi_ref, ws_ref, emt_ref,
                   *, last_layer, n_blocks):
    g = pl.program_id(0)
    nsamp = x_ref.shape[0]
    hm = _head_masks()
    ones_bd = _ones_matrix()
    avg = _avg_matrix()
    mask8 = (lax.broadcasted_iota(jnp.int32, (8, GW), 0)
             == lax.broadcasted_iota(jnp.int32, (8, GW), 1) // DH)
    expand = (lax.broadcasted_iota(jnp.int32, (128, GW), 0)
              == lax.broadcasted_iota(jnp.int32, (128, GW), 1) // DH).astype(BF16)
    fold = (lax.broadcasted_iota(jnp.int32, (GW, DH), 0) % DH
            == lax.broadcasted_iota(jnp.int32, (GW, DH), 1)).astype(BF16)
    tile4 = (lax.broadcasted_iota(jnp.int32, (DH, GW), 0)
             == lax.broadcasted_iota(jnp.int32, (DH, GW), 1) % DH).astype(BF16)

    def rows8(v):
        n = v.shape[0]
        return jnp.where(mask8[None], v[:, None, :], 0.0).reshape(8 * n, GW)

    def P(blk):
        return proj_ref[:, blk * GW:(blk + 1) * GW]

    def gates():
        gt = P(GT)
        ig = gt[:, :128] + bi_ref[...]
        lf = _log_sigmoid(gt[:, 128:] + bf_ref[...])
        a = lf + m0_ref[...]
        mt = jnp.maximum(a, ig)
        return ig, a, mt

    @pl.when(g == 0)
    def _pre():
        hn = _rms_norm(x_ref[...], normw_ref[...]).astype(BF16)
        proj_ref[...] = _dot(hn, win_ref[...])
        cosr, s1, s2 = cos_ref[...], s1_ref[...], s2_ref[...]
        rq = _rope(P(RQ), cosr, s1, s2)
        rk = _rope(P(RK), cosr, s1, s2)
        qr8_ref[...] = rows8(rq * gam_ref[...])
        kr8_ref[...] = rows8(rk)
        vrh_ref[...] = _dot(rows8(P(RV)).astype(BF16), fold)
        ig, a, mt = gates()
        wi = _dot_x2(jnp.exp(ig - mt), expand)
        ws = _dot_x2(jnp.exp(a - mt), expand)
        wi_ref[...] = wi
        ws_ref[...] = ws
        emt_ref[...] = _dot_x2(jnp.exp(-mt), expand)
        qmh_ref[...] = _dot(rows8(P(MQ)).astype(BF16), fold)
        kmh_ref[...] = _dot(rows8(P(MK)).astype(BF16), fold)
        vw8_ref[...] = rows8(wi * P(MV))
        hi, lo = _split2(ws)
        rid = lax.broadcasted_iota(jnp.int32, (1, 8, GW), 1)
        a8 = jnp.where(rid == 0, hi.astype(F32)[:, None, :],
                       jnp.where(rid == 1, lo.astype(F32)[:, None, :], 0.0))
        a8_ref[...] = a8.reshape(8 * nsamp, GW)
        qa8_ref[...] = rows8(P(AQ))
        mlmn_ref[...] = mt
        su = P(SU)
        bu = _dot(su.astype(BF16), bt_ref[...])
        are, aim = are_ref[...], aim_ref[...]
        x0r, x0i = x0re_ref[...], x0im_ref[...]
        s5re_ref[...] = are * x0r - aim * x0i + bu[:, :NS]
        s5im_ref[...] = are * x0i + aim * x0r + bu[:, NS:]

    ones8 = jnp.ones((8, DH), BF16)

    def sample(i, carry):
        bg = g * SB + i
        r8 = pl.ds(pl.multiple_of(bg * 8, 8), 8)
        s_b = rets_ref[i]
        t_o = _dot(qr8_ref[r8, :].astype(BF16), s_b.astype(BF16))
        o_row = jnp.sum(jnp.where(mask8, _dot_x2(t_o, tile4), 0.0), axis=0, keepdims=True)
        ost_ref[pl.ds(bg, 1), :] = o_row
        upd = _dot_tn(kr8_ref[r8, :].astype(BF16), vrh_ref[r8, :].astype(BF16))
        retn_ref[i] = s_b * gcol_ref[...] + upd
        c_b = mlc_ref[i]
        t_c = _dot_nt(qmh_ref[r8, :].astype(BF16), c_b.astype(BF16))
        cq_ref[pl.ds(bg, 1), :] = jnp.sum(jnp.where(mask8, t_c, 0.0), axis=0, keepdims=True)
        wscol = _dot_tn(a8_ref[r8, :].astype(BF16), ones8)
        updc = _dot_tn(vw8_ref[r8, :].astype(BF16), kmh_ref[r8, :].astype(BF16))
        mlcn_ref[i] = c_b * wscol + updc
        k_b = ck_ref[i].astype(BF16)
        v_b = cv_ref[i].astype(BF16)
        sc = _dot_nt(qa8_ref[r8, :].astype(BF16), k_b)
        e = jnp.exp(sc - jnp.max(sc, axis=-1, keepdims=True))
        p = e / jnp.sum(e, axis=-1, keepdims=True)
        t_a = _dot(p.astype(BF16), v_b)
        xa_ref[pl.ds(bg, 1), :] = jnp.sum(jnp.where(mask8, t_a, 0.0), axis=0, keepdims=True)
        return carry

    lax.fori_loop(0, SB, sample, 0)

    @pl.when(g == n_blocks - 1)
    def _post():
        cosr, s1, s2 = cos_ref[...], s1_ref[...], s2_ref[...]
        rq = _rope(P(RQ), cosr, s1, s2)
        rk = _rope(P(RK), cosr, s1, s2)
        ro = _dot_x2(rq * rk, ones_bd) * P(RV) + ost_ref[...]
        ret_out = _head_norm(ro, retgn_ref[...], avg) * _silu(P(RG))
        mq, mk_, mv_ = P(MQ), P(MK), P(MV)
        wi, ws, emt = wi_ref[...], ws_ref[...], emt_ref[...]
        n0 = n0_ref[...]
        s = _dot_x2(mq * mk_, ones_bd) * wi
        num = s * mv_ + ws * cq_ref[...]
        den = s + ws * _dot_x2(n0 * mq, ones_bd)
        mh = num / jnp.maximum(jnp.abs(den), emt)
        mlnn_ref[...] = ws * n0 + wi * mk_
        mh = mh * _sigmoid(P(MO))
        ml_out = _head_norm(mh, mlgn_ref[...], avg) * _silu(P(MG))
        su = P(SU)
        xcat = jnp.concatenate([s5re_ref[...], s5im_ref[...]], axis=1).astype(BF16)
        sy = _dot(xcat, ct_ref[...]) + dsk_ref[...] * su
        sy = _gelu_tanh(sy)
        sy = sy * _sigmoid(_dot(sy.astype(BF16), wglu_ref[...]))
        s5_out = sy * _silu(P(SG))
        xa_out = xa_ref[...] * _silu(P(AG))
        mix = jnp.concatenate([ret_out, ml_out, s5_out, xa_out], axis=1).astype(BF16)
        y = x_ref[...] + _dot(mix, wout_ref[...])
        if last_layer:
            y = _rms_norm(y, fnw_ref[...])
        y_ref[...] = y


def _sample_layer(x, st, w, consts, last_layer):
    nsamp = x.shape[0]
    nb = nsamp // SB
    full = lambda shape: pl.BlockSpec(shape, lambda g: (0,) * len(shape))
    once = lambda shape: pl.BlockSpec(shape, lambda g: (0,) * len(shape),
                                      pipeline_mode=pl.Buffered(1))
    blk = lambda r, c: pl.BlockSpec((SB, r, c), lambda g: (g, 0, 0))
    in_specs = [
        once((nsamp, D)), once((1, D)), once((D, DP)), once((D, D)),
        once((1, GW)), once((1, GW)), once((1, GW)), once((1, GW)), once((GW, DH)),
        once((1, GW)), once((1, GW)), once((1, 128)), once((1, 128)),
        once((1, NS)), once((1, NS)), once((GW, 2 * NS)), once((2 * NS, GW)),
        once((1, GW)), once((GW, GW)), once((1, D)),
        once((nsamp, 128)), once((nsamp, GW)), once((nsamp, NS)), once((nsamp, NS)),
        blk(GW, DH), blk(GW, DH), blk(MEM, GW), blk(MEM, GW),
    ]
    out_shape = (
        jax.ShapeDtypeStruct((nsamp, D), F32),
        jax.ShapeDtypeStruct((nsamp, GW, DH), F32),
        jax.ShapeDtypeStruct((nsamp, GW, DH), F32),
        jax.ShapeDtypeStruct((nsamp, GW), F32),
        jax.ShapeDtypeStruct((nsamp, 128), F32),
        jax.ShapeDtypeStruct((nsamp, NS), F32),
        jax.ShapeDtypeStruct((nsamp, NS), F32),
    )
    out_specs = (
        full((nsamp, D)), blk(GW, DH), blk(GW, DH), full((nsamp, GW)), full((nsamp, 128)),
        full((nsamp, NS)), full((nsamp, NS)),
    )
    r8 = 8 * nsamp
    scratch = [
        pltpu.VMEM((nsamp, DP), F32),
        pltpu.VMEM((r8, GW), F32), pltpu.VMEM((r8, GW), F32), pltpu.VMEM((r8, DH), F32),
        pltpu.VMEM((r8, DH), F32), pltpu.VMEM((r8, GW), F32), pltpu.VMEM((r8, DH), F32),
        pltpu.VMEM((r8, GW), F32), pltpu.VMEM((r8, GW), F32),
        pltpu.VMEM((nsamp, GW), F32), pltpu.VMEM((nsamp, GW), F32), pltpu.VMEM((nsamp, GW), F32),
        pltpu.VMEM((nsamp, GW), F32), pltpu.VMEM((nsamp, GW), F32), pltpu.VMEM((nsamp, GW), F32),
    ]
    kern = functools.partial(_sample_kernel, last_layer=last_layer, n_blocks=nb)
    return pl.pallas_call(
        kern, grid=(nb,), in_specs=in_specs, out_specs=out_specs, out_shape=out_shape,
        scratch_shapes=scratch, name="sample_layer",
        compiler_params=pltpu.CompilerParams(
            dimension_semantics=("arbitrary",), vmem_limit_bytes=VMEM_LIMIT),
    )(x, w["norm_w"], w["w_in"], w["w_out"],
      consts["cos_s"], consts["s1_s"], consts["s2_s"], consts["gam"], consts["gcol"],
      w["ret_gn"], w["ml_gn"], w["b_i"], w["b_f"],
      w["ab_re"], w["ab_im"], w["bt"], w["ct"], w["s5_d"], w["w_glu"], w["final_norm_w"],
      st["m"], st["n"], st["s5_re"], st["s5_im"], st["ret"], st["c"], st["mem_k"], st["mem_v"])


def _rope_tables(pos):
    half = DH // 2
    inv = ROPE_BASE ** (-jnp.arange(half, dtype=F32) / half)
    ang = pos.astype(F32)[:, None] * inv[None, :]
    cos, sin = jnp.cos(ang), jnp.sin(ang)
    zero = jnp.zeros_like(sin)
    c = jnp.tile(jnp.concatenate([cos, cos], axis=-1), (1, NH))
    s1 = jnp.tile(jnp.concatenate([zero, sin], axis=-1), (1, NH))
    s2 = jnp.tile(jnp.concatenate([-sin, zero], axis=-1), (1, NH))
    return c, s1, s2


def _constants(seq):
    lg = jnp.log1p(-jnp.power(2.0, -5.0 - jnp.arange(NH, dtype=F32)))[:, None]
    idx = jnp.arange(L, dtype=F32)
    diff = idx[:, None] - idx[None, :]
    decay = jnp.where(diff >= 0, jnp.exp(lg[:, :, None] * jnp.maximum(diff, 0.0)), 0.0)
    rep = lambda t: jnp.repeat(t, DH, axis=0).T
    consts = {
        "dec": decay.reshape(NH * L, L),
        "qdec": rep(jnp.exp(lg * (idx + 1.0))),
        "kdec": rep(jnp.exp(lg * (L - 1.0 - idx))),
        "cdec": rep(jnp.exp(lg * L)),
        "gam": rep(jnp.exp(lg * 1.0)),
    }
    consts["gcol"] = jnp.broadcast_to(consts["gam"].T, (GW, DH))
    consts["cos_p"], consts["s1_p"], consts["s2_p"] = _rope_tables(jnp.arange(seq, dtype=jnp.int32))
    consts["cos_s"], consts["s1_s"], consts["s2_s"] = _rope_tables(
        PAST_LEN + jnp.arange(1, dtype=jnp.int32))
    return consts


def _pack_w_in(w_in):
    sizes = (GW,) * 9 + (NH, NH) + (GW,) * 4
    offs = np.concatenate([[0], np.cumsum(sizes)])
    seg = [w_in[:, int(offs[i]):int(offs[i + 1])] for i in range(len(sizes))]
    scale = DH ** -0.5
    pad = lambda t: jnp.pad(t, ((0, 0), (0, 128 - t.shape[1])))
    blocks = [seg[0], seg[1] * scale, seg[2], seg[3],
              seg[4], seg[5] * scale, seg[6], seg[7], seg[8],
              seg[11], seg[12], seg[13] * scale, seg[14],
              pad(seg[9]), pad(seg[10])]
    return jnp.concatenate(blocks, axis=1).astype(BF16)


def kernel(x_prompt, x_sample, mem_prompt, state_ret, state_mlstm_c, state_mlstm_n, state_mlstm_m,
           state_s5_re, state_s5_im, cache_mem_k, cache_mem_v,
           norm_w, w_in, ret_gn, ml_b_i, ml_b_f, ml_gn,
           s5_a_re, s5_a_im, s5_log_dt, s5_b_re, s5_b_im, s5_c_re, s5_c_im, s5_d, s5_w_glu,
           w_mem_k, w_mem_v, w_out, final_norm_w):
    depth = norm_w.shape[0]
    bp, seq, _ = x_prompt.shape
    bs = x_sample.shape[0]
    consts = _constants(seq)
    ab_re, ab_im, bt = _s5_discretise(s5_a_re, s5_a_im, s5_log_dt, s5_b_re, s5_b_im)
    ct = _s5_out_matrix(s5_c_re, s5_c_im)
    pad4 = lambda t: jnp.pad(t, ((0, 0), (0, 128 - t.shape[-1])))

    hp = x_prompt
    hs = x_sample.reshape(bs, D)
    outs_p = [[] for _ in range(8)]
    outs_s = [[] for _ in range(6)]
    for l in range(depth):
        w = {
            "norm_w": norm_w[l][None], "w_in": _pack_w_in(w_in[l]), "w_out": w_out[l].astype(BF16),
            "w_mem_k": w_mem_k[l].astype(BF16), "w_mem_v": w_mem_v[l].astype(BF16),
            "ret_gn": ret_gn[l][None], "ml_gn": ml_gn[l][None],
            "b_i": pad4(ml_b_i[l][None]), "b_f": pad4(ml_b_f[l][None]),
            "ab_re": ab_re[l], "ab_im": ab_im[l], "bt": bt[l], "ct": ct[l],
            "s5_d": s5_d[l][None], "w_glu": s5_w_glu[l].astype(BF16),
            "final_norm_w": final_norm_w[None],
        }
        last = l == depth - 1
        hp, rs, cs, ns, ms, xr, xi, mk, mv = _prompt_layer(hp, mem_prompt, w, consts, last)
        diag = lambda t: jnp.stack([t[:, h * DH:(h + 1) * DH, h * DH:(h + 1) * DH]
                                    for h in range(NH)], axis=1)
        outs_p[0].append(diag(rs))
        outs_p[1].append(jnp.swapaxes(diag(cs), -1, -2))
        outs_p[2].append(jnp.stack([ns[:, h * DH:(h + 1) * DH, h] for h in range(NH)], axis=1))
        outs_p[3].append(ms[:, 0, :NH])
        outs_p[4].append(xr[:, 0].reshape(bp, S5G, S5P))
        outs_p[5].append(xi[:, 0].reshape(bp, S5G, S5P))
        outs_p[6].append(mk.reshape(bp, MEM, NH, DH))
        outs_p[7].append(mv.reshape(bp, MEM, NH, DH))

        st = {
            "m": pad4(state_mlstm_m[l]), "n": state_mlstm_n[l].reshape(bs, GW),
            "s5_re": state_s5_re[l].reshape(bs, NS), "s5_im": state_s5_im[l].reshape(bs, NS),
            "ret": state_ret[l].reshape(bs, GW, DH), "c": state_mlstm_c[l].reshape(bs, GW, DH),
            "mem_k": cache_mem_k[l].reshape(bs, MEM, GW), "mem_v": cache_mem_v[l].reshape(bs, MEM, GW),
        }
        hs, rn, cn, nn, mn, sr, si = _sample_layer(hs, st, w, consts, last)
        outs_s[0].append(rn.reshape(bs, NH, DH, DH))
        outs_s[1].append(cn.reshape(bs, NH, DH, DH))
        outs_s[2].append(nn.reshape(bs, NH, DH))
        outs_s[3].append(mn[:, :NH])
        outs_s[4].append(sr.reshape(bs, S5G, S5P))
        outs_s[5].append(si.reshape(bs, S5G, S5P))

    stk = jnp.stack
    return (hp, hs.reshape(bs, 1, D),
            stk(outs_p[0]), stk(outs_s[0]), stk(outs_p[1]), stk(outs_s[1]),
            stk(outs_p[2]), stk(outs_s[2]), stk(outs_p[3]), stk(outs_s[3]),
            stk(outs_p[4]), stk(outs_s[4]), stk(outs_p[5]), stk(outs_s[5]),
            stk(outs_p[6]), stk(outs_p[7]))
```

```python
import functools
import math

import numpy as np
import jax
import jax.numpy as jnp
from jax import lax
from jax.experimental import pallas as pl
from jax.experimental.pallas import tpu as pltpu

F32 = jnp.float32
BF16 = jnp.bfloat16

D = 1024
GW = 256
NH = 4
DH = 64
L = 128
LS = 32
S5G = 16
S5P = 64
S5C = 16
NS = S5G * S5P
MEM = 256
EPS = 1e-6
NEG_INF = -1e30
ROPE_BASE = 10000.0
PAST_LEN = 16384

TB = 512
SB = 8
XQ = 256
NBLK = 14
DP = NBLK * GW
(RQ, RK, RV, RG, MQ, MK, MV, MO, MG, SU, SG, AQ, AG, GT) = range(NBLK)

VMEM_LIMIT = 52 * 1024 * 1024


def _dot(a, b):
    return jnp.dot(a, b, preferred_element_type=F32)


def _dot_nt(a, b):
    return lax.dot_general(a, b, (((1,), (1,)), ((), ())), preferred_element_type=F32)


def _dot_tn(a, b):
    return lax.dot_general(a, b, (((0,), (0,)), ((), ())), preferred_element_type=F32)


def _split2(x):
    hi = x.astype(BF16)
    lo = (x - hi.astype(F32)).astype(BF16)
    return hi, lo


def _dot_x2(x, w):
    hi, lo = _split2(x)
    return _dot(hi, w) + _dot(lo, w)


def _dot_x3(x, w):
    hi = x.astype(BF16)
    r = x - hi.astype(F32)
    mid = r.astype(BF16)
    lo = (r - mid.astype(F32)).astype(BF16)
    return _dot(hi, w) + _dot(mid, w) + _dot(lo, w)


def _sigmoid(x):
    return 1.0 / (1.0 + jnp.exp(-x))


def _silu(x):
    return x * _sigmoid(x)


def _log_sigmoid(x):
    return jnp.minimum(x, 0.0) - jnp.log1p(jnp.exp(-jnp.abs(x)))


def _gelu_tanh(x):
    c = math.sqrt(2.0 / math.pi)
    return x * (0.5 * (1.0 + jnp.tanh(c * (x + 0.044715 * (x * x * x)))))


def _lane_head(n):
    return lax.broadcasted_iota(jnp.int32, (1, n), 1) // DH


def _head_masks():
    lh = _lane_head(GW)
    return [lh == h for h in range(NH)]


def _block_diag_mask():
    r = lax.broadcasted_iota(jnp.int32, (GW, GW), 0) // DH
    c = lax.broadcasted_iota(jnp.int32, (GW, GW), 1) // DH
    return r == c


def _avg_matrix():
    return jnp.where(_block_diag_mask(), 1.0 / DH, 0.0).astype(BF16)


def _ones_matrix():
    return jnp.where(_block_diag_mask(), 1.0, 0.0).astype(BF16)


def _rope(x, cos, s1, s2):
    outs = []
    for j in range(2):
        sl = slice(j * 128, (j + 1) * 128)
        xs = x[:, sl]
        outs.append(xs * cos[:, sl] + pltpu.roll(xs, 32, 1) * s1[:, sl]
                    + pltpu.roll(xs, 96, 1) * s2[:, sl])
    return jnp.concatenate(outs, axis=1)


def _head_norm(x, gain, avg):
    mu = _dot_x2(x, avg)
    d = x - mu
    var = _dot_x2(d * d, avg)
    return d * lax.rsqrt(var + EPS) * gain


def _rms_norm(x, w):
    ms = jnp.mean(x * x, axis=-1, keepdims=True)
    return x * lax.rsqrt(ms + EPS) * w


def _stack_heads(x, hm):
    return jnp.concatenate([jnp.where(hm[h], x, 0.0) for h in range(NH)], axis=0)


def _fold_heads(r, hm, rows):
    out = jnp.where(hm[0], r[0:rows], 0.0)
    for h in range(1, NH):
        out = out + jnp.where(hm[h], r[h * rows:(h + 1) * rows], 0.0)
    return out


def _s5_prep_kernel(are_ref, aim_ref, ldt_ref, bre_ref, bim_ref, cre_ref, cim_ref,
                    ab_ref, bt_ref, ct_ref, tab_ref):
    a_re = are_ref[0]
    a_im = aim_ref[0]
    dt = jnp.exp(ldt_ref[0])
    lam_re = a_re * dt
    lam_im = a_im * dt
    mag = jnp.exp(lam_re)
    ab_re = mag * jnp.cos(lam_im)
    ab_im = mag * jnp.sin(lam_im)
    den = a_re * a_re + a_im * a_im
    nr = ab_re - 1.0
    ni = ab_im
    f_re = (nr * a_re + ni * a_im) / den
    f_im = (ni * a_re - nr * a_im) / den
    ab_ref[0, 0:1, :] = ab_re
    ab_ref[0, 1:2, :] = ab_im
    b_re = bre_ref[0]
    b_im = bim_ref[0]
    bb_re = (f_re * b_re - f_im * b_im).astype(BF16)
    bb_im = (f_re * b_im + f_im * b_re).astype(BF16)
    rep_r = (lax.broadcasted_iota(jnp.int32, (GW, S5C), 0) % S5C
             == lax.broadcasted_iota(jnp.int32, (GW, S5C), 1)).astype(BF16)
    in_blk = (lax.broadcasted_iota(jnp.int32, (GW, NS), 0) // S5C
              == lax.broadcasted_iota(jnp.int32, (GW, NS), 1) // S5P)
    bt_ref[0, :, :NS] = jnp.where(in_blk, _dot(rep_r, bb_re), 0.0).astype(BF16)
    bt_ref[0, :, NS:] = jnp.where(in_blk, _dot(rep_r, bb_im), 0.0).astype(BF16)
    rep_c = (lax.broadcasted_iota(jnp.int32, (S5C, GW), 0)
             == lax.broadcasted_iota(jnp.int32, (S5C, GW), 1) % S5C).astype(BF16)
    out_blk = (lax.broadcasted_iota(jnp.int32, (NS, GW), 0) // S5P
               == lax.broadcasted_iota(jnp.int32, (NS, GW), 1) // S5C)
    ct_ref[0, :NS, :] = jnp.where(out_blk, _dot(cre_ref[0].astype(BF16), rep_c), 0.0).astype(BF16)
    ct_ref[0, NS:, :] = jnp.where(out_blk, -_dot(cim_ref[0].astype(BF16), rep_c), 0.0).astype(BF16)
    i = (lax.broadcasted_iota(jnp.int32, (L, NS), 0) % LS).astype(F32)
    for slot, k in ((0, -i), (2, i), (4, i + 1.0)):
        pmag = jnp.exp(k * lam_re)
        tab_ref[0, slot] = pmag * jnp.cos(k * lam_im)
        tab_ref[0, slot + 1] = pmag * jnp.sin(k * lam_im)


def _s5_prepare(a_re, a_im, log_dt, b_re, b_im, c_re, c_im):
    depth = a_re.shape[0]
    are = a_re.reshape(depth, 1, NS)
    aim = a_im.reshape(depth, 1, NS)
    ldt = jnp.repeat(log_dt, S5P, axis=-1).reshape(depth, 1, NS)
    bre = jnp.transpose(b_re, (0, 3, 1, 2)).reshape(depth, S5C, NS)
    bim = jnp.transpose(b_im, (0, 3, 1, 2)).reshape(depth, S5C, NS)
    cre = jnp.transpose(c_re, (0, 1, 3, 2)).reshape(depth, NS, S5C)
    cim = jnp.transpose(c_im, (0, 1, 3, 2)).reshape(depth, NS, S5C)
    per_layer = lambda *shape: pl.BlockSpec((1,) + shape, lambda l: (l,) + (0,) * len(shape))
    out_shape = (jax.ShapeDtypeStruct((depth, 2, NS), F32),
                 jax.ShapeDtypeStruct((depth, GW, 2 * NS), BF16),
                 jax.ShapeDtypeStruct((depth, 2 * NS, GW), BF16),
                 jax.ShapeDtypeStruct((depth, 6, L, NS), F32))
    return pl.pallas_call(
        _s5_prep_kernel, grid=(depth,),
        in_specs=[per_layer(1, NS), per_layer(1, NS), per_layer(1, NS),
                  per_layer(S5C, NS), per_layer(S5C, NS), per_layer(NS, S5C), per_layer(NS, S5C)],
        out_specs=(per_layer(2, NS), per_layer(GW, 2 * NS), per_layer(2 * NS, GW),
                   per_layer(6, L, NS)),
        out_shape=out_shape, name="s5_prepare",
        compiler_params=pltpu.CompilerParams(dimension_semantics=("arbitrary",)),
    )(are, aim, ldt, bre, bim, cre, cim)


def _prompt_kernel(x_ref, mem_ref, normw_ref, win_ref, wout_ref, wmk_ref, wmv_ref,
                   cos_ref, s1_ref, s2_ref, dec_ref, qdec_ref, kdec_ref, cdec_ref,
                   retgn_ref, mlgn_ref, bi_ref, bf_ref,
                   tab_ref, bt_ref, ct_ref, dsk_ref, wglu_ref, fnw_ref,
                   y_ref, rets_ref, mlc_ref, mln_ref, mlm_ref, s5re_ref, s5im_ref,
                   memk_ref, memv_ref,
                   proj_ref, mix_ref, s_ref, c_ref, n_ref, m_ref, xre_ref, xim_ref,
                   mk_ref, mv_ref, bu_ref, xcat_ref, *, last_layer, n_tblocks):
    t = pl.program_id(1)
    hm = _head_masks()
    bd = _block_diag_mask()
    avg = _avg_matrix()
    lane128 = lax.broadcasted_iota(jnp.int32, (1, 128), 1)
    row_i = lax.broadcasted_iota(jnp.int32, (L, 128), 0)
    col_i = lax.broadcasted_iota(jnp.int32, (L, 128), 1)
    causal = row_i >= col_i
    tri_u = jnp.where(row_i <= col_i, 1.0, 0.0).astype(BF16)
    tri_sub = jnp.where(causal & (row_i // LS == col_i // LS), 1.0, 0.0).astype(BF16)

    @pl.when(t == 0)
    def _init():
        s_ref[...] = jnp.zeros_like(s_ref)
        c_ref[...] = jnp.zeros_like(c_ref)
        n_ref[...] = jnp.zeros_like(n_ref)
        m_ref[...] = jnp.zeros_like(m_ref)
        xre_ref[...] = jnp.zeros_like(xre_ref)
        xim_ref[...] = jnp.zeros_like(xim_ref)
        memb = mem_ref[0].astype(BF16)
        mk = _dot(memb, wmk_ref[...])
        mv = _dot(memb, wmv_ref[...])
        memk_ref[0] = mk
        memv_ref[0] = mv
        mk_ref[...] = mk.astype(BF16)
        mv_ref[...] = mv.astype(BF16)

    x = x_ref[0]
    hn = _rms_norm(x, normw_ref[...]).astype(BF16)
    proj_ref[...] = _dot(hn, win_ref[...])

    nc = TB // L
    crow = [slice(c * L, (c + 1) * L) for c in range(nc)]

    def PB(blk, rows=slice(None)):
        return proj_ref[rows, blk * GW:(blk + 1) * GW]

    for piece in range(TB // XQ):
        rs = slice(piece * XQ, (piece + 1) * XQ)
        sc = _dot_nt(_stack_heads(PB(AQ, rs), hm).astype(BF16), mk_ref[...])
        e = jnp.exp(sc - jnp.max(sc, axis=-1, keepdims=True))
        p = e / jnp.sum(e, axis=-1, keepdims=True)
        xa = _fold_heads(_dot(p.astype(BF16), mv_ref[...]), hm, XQ)
        mix_ref[rs, 3 * GW:4 * GW] = (xa * _silu(PB(AG, rs))).astype(BF16)

    su = PB(SU)
    bu_ref[...] = _dot(su.astype(BF16), bt_ref[...])
    lanes = [(slice(j * 128, (j + 1) * 128), slice(NS + j * 128, NS + (j + 1) * 128))
             for j in range(NS // 128)]
    for c in range(nc):
        for lre, lim in lanes:
            br = bu_ref[crow[c], lre]
            bi = bu_ref[crow[c], lim]
            wr = tab_ref[0, :, lre]
            wi = tab_ref[1, :, lre]
            xcat_ref[crow[c], lre] = (wr * br - wi * bi).astype(BF16)
            xcat_ref[crow[c], lim] = (wr * bi + wi * br).astype(BF16)
    for c in range(nc):
        bu_ref[crow[c], :] = _dot(tri_sub, xcat_ref[crow[c], :])
    for c in range(nc):
        for lre, lim in lanes:
            zr = bu_ref[crow[c], lre]
            zi = bu_ref[crow[c], lim]
            pr = tab_ref[2, :, lre]
            pi = tab_ref[3, :, lre]
            bu_ref[crow[c], lre] = pr * zr - pi * zi
            bu_ref[crow[c], lim] = pr * zi + pi * zr
    ar = tab_ref[4, LS - 1:LS, :]
    ai = tab_ref[5, LS - 1:LS, :]
    cr = xre_ref[0:1, :]
    ci_ = xim_ref[0:1, :]
    carries = []
    for sb in range(TB // LS):
        carries.append((cr, ci_))
        er = bu_ref[sb * LS + LS - 1:sb * LS + LS, 0:NS]
        ei = bu_ref[sb * LS + LS - 1:sb * LS + LS, NS:2 * NS]
        cr, ci_ = er + ar * cr - ai * ci_, ei + ar * ci_ + ai * cr
    xre_ref[...] = jnp.broadcast_to(cr, xre_ref.shape)
    xim_ref[...] = jnp.broadcast_to(ci_, xim_ref.shape)
    qr = tab_ref[4, 0:LS, :]
    qi = tab_ref[5, 0:LS, :]
    for sb in range(TB // LS):
        rs = slice(sb * LS, (sb + 1) * LS)
        pcr, pci = carries[sb]
        xcat_ref[rs, 0:NS] = (bu_ref[rs, 0:NS] + qr * pcr - qi * pci).astype(BF16)
        xcat_ref[rs, NS:2 * NS] = (bu_ref[rs, NS:2 * NS] + qr * pci + qi * pcr).astype(BF16)
    sy = _dot(xcat_ref[...], ct_ref[...]) + dsk_ref[...] * su
    sy = _gelu_tanh(sy)
    sy = sy * _sigmoid(_dot(sy.astype(BF16), wglu_ref[...]))
    mix_ref[:, 2 * GW:3 * GW] = (sy * _silu(PB(SG))).astype(BF16)

    rq = _rope(PB(RQ), cos_ref[...], s1_ref[...], s2_ref[...])
    rk = _rope(PB(RK), cos_ref[...], s1_ref[...], s2_ref[...])
    rv = PB(RV).astype(BF16)
    rkb = rk.astype(BF16)
    inner = [_dot_nt(_stack_heads(rq[crow[c]], hm).astype(BF16), rkb[crow[c]]) for c in range(nc)]
    pmat = [(inner[c] * dec_ref[...]).astype(BF16) for c in range(nc)]
    rloc = [_dot(pmat[c], rv[crow[c]]) for c in range(nc)]
    upd = [_dot_tn((rk[crow[c]] * kdec_ref[...]).astype(BF16), rv[crow[c]]) for c in range(nc)]
    st = [s_ref[...]]
    for c in range(nc):
        st.append(st[c] * cdec_ref[...] + jnp.where(bd, upd[c], 0.0))
    s_ref[...] = st[nc]
    ro = jnp.concatenate(
        [_fold_heads(rloc[c], hm, L)
         + _dot((rq[crow[c]] * qdec_ref[...]).astype(BF16), st[c].astype(BF16)) for c in range(nc)],
        axis=0)
    mix_ref[:, 0:GW] = (_head_norm(ro, retgn_ref[...], avg) * _silu(PB(RG))).astype(BF16)

    g = PB(GT)
    ig_r = jnp.concatenate([g[crow[c], 0:128].T[0:8] for c in range(nc)], axis=1) + bi_ref[...]
    lf_r = _log_sigmoid(
        jnp.concatenate([g[crow[c], 128:256].T[0:8] for c in range(nc)], axis=1) + bf_ref[...])
    b_r = [_dot_x3(lf_r[:, crow[c]], tri_u) for c in range(nc)]
    gd_r = [ig_r[:, crow[c]] - b_r[c] for c in range(nc)]
    lane_r = lax.broadcasted_iota(jnp.int32, (8, L), 1)
    cm_r = []
    for c in range(nc):
        v = gd_r[c]
        for k in range(7):
            d = 1 << k
            v = jnp.maximum(v, jnp.where(lane_r >= d, pltpu.roll(v, d, 1), NEG_INF))
        cm_r.append(v)
    m_prev = [m_ref[...]]
    mt_r = []
    for c in range(nc):
        mt = jnp.maximum(b_r[c] + m_prev[c], b_r[c] + cm_r[c])
        mt_r.append(mt)
        m_prev.append(jnp.broadcast_to(mt[:, L - 1:L], (8, L)))
    m_ref[...] = m_prev[nc]
    cols = []
    for c in range(nc):
        bm = b_r[c] - mt_r[c]
        ws = jnp.exp(b_r[c] + m_prev[c] - mt_r[c])
        wl = jnp.exp(gd_r[c] + jnp.broadcast_to(bm[:, L - 1:L], (8, L)))
        emt = jnp.exp(-mt_r[c])
        cols.append(jnp.concatenate([bm, ws, wl, emt, jnp.zeros((L - 32, L), F32)], axis=0).T)
    mqf = PB(MQ)
    mq = mqf.astype(BF16)
    mkf = PB(MK)
    mkb = mkf.astype(BF16)
    mv_ = PB(MV).astype(BF16)
    sraw = [_dot_nt(_stack_heads(mqf[crow[c]], hm).astype(BF16), mkb[crow[c]]) for c in range(nc)]
    smat, den_i = [], []
    for c in range(nc):
        parts = []
        for h in range(NH):
            arg = jnp.where(causal, cols[c][:, h:h + 1] + gd_r[c][h:h + 1, :], NEG_INF)
            parts.append(sraw[c][h * L:(h + 1) * L] * jnp.exp(arg))
        s_c = jnp.concatenate(parts, axis=0)
        den_i.append(jnp.sum(s_c, axis=-1, keepdims=True))
        smat.append(s_c.astype(BF16))
    rloc = [_dot(smat[c], mv_[crow[c]]) for c in range(nc)]
    ones_blk = jnp.ones((L, 128), BF16)
    nmask = (lax.broadcasted_iota(jnp.int32, (GW, 128), 0) // DH
             == lax.broadcasted_iota(jnp.int32, (GW, 128), 1))
    u = []
    for c in range(nc):
        kw = jnp.zeros((L, GW), F32)
        for h in range(NH):
            kw = kw + jnp.where(hm[h], mkf[crow[c]] * cols[c][:, 16 + h:17 + h], 0.0)
        vaug = jnp.concatenate([mv_[crow[c]], ones_blk], axis=1)
        u.append(_dot_tn(kw.astype(BF16), vaug))
    cst = [c_ref[...]]
    nst = [n_ref[...]]
    for c in range(nc):
        wsl256 = jnp.zeros((1, GW), F32)
        wsl128 = jnp.zeros((1, 128), F32)
        for h in range(NH):
            wsl = cols[c][L - 1:L, 8 + h:9 + h]
            wsl256 = wsl256 + jnp.where(hm[h], wsl, 0.0)
            wsl128 = wsl128 + jnp.where(lane128 == h, wsl, 0.0)
        cst.append(cst[c] * wsl256 + jnp.where(bd, u[c][:, :GW], 0.0))
        nst.append(nst[c] * wsl128 + jnp.where(nmask, u[c][:, GW:], 0.0))
    c_ref[...] = cst[nc]
    n_ref[...] = nst[nc]
    mhs = []
    for c in range(nc):
        qc = _dot(mq[crow[c]], cst[c].astype(BF16))
        qn = _dot(mq[crow[c]], nst[c].astype(BF16))
        mh = jnp.zeros((L, GW), F32)
        for h in range(NH):
            ws = cols[c][:, 8 + h:9 + h]
            den = den_i[c][h * L:(h + 1) * L] + ws * qn[:, h:h + 1]
            dn = jnp.maximum(jnp.abs(den), cols[c][:, 24 + h:25 + h])
            mh = mh + jnp.where(hm[h], (rloc[c][h * L:(h + 1) * L] + ws * qc) / dn, 0.0)
        mhs.append(mh)
    mh = jnp.concatenate(mhs, axis=0) * _sigmoid(PB(MO))
    mix_ref[:, GW:2 * GW] = (_head_norm(mh, mlgn_ref[...], avg) * _silu(PB(MG))).astype(BF16)

    y = x + _dot(mix_ref[...], wout_ref[...])
    if last_layer:
        y = _rms_norm(y, fnw_ref[...])
    y_ref[0] = y

    @pl.when(t == n_tblocks - 1)
    def _final():
        rets_ref[0] = s_ref[...]
        mlc_ref[0] = c_ref[...]
        mln_ref[0] = n_ref[...]
        mlm_ref[0] = m_ref[...]
        s5re_ref[0] = xre_ref[...]
        s5im_ref[0] = xim_ref[...]


def _prompt_layer(x, mem, w, consts, last_layer):
    bsz, seq, _ = x.shape
    nt = seq // TB
    full = lambda shape: pl.BlockSpec(shape, lambda b, t: (0,) * len(shape),
                                      pipeline_mode=pl.Buffered(1))
    tok = lambda width: pl.BlockSpec((TB, width), lambda b, t: (t, 0))
    per_b = lambda r, c: pl.BlockSpec((1, r, c), lambda b, t: (b, 0, 0))
    in_specs = [
        pl.BlockSpec((1, TB, D), lambda b, t: (b, t, 0)),
        per_b(MEM, D),
        full((1, D)), full((D, DP)), full((D, D)), full((D, GW)), full((D, GW)),
        tok(GW), tok(GW), tok(GW),
        full((NH * L, L)), full((L, GW)), full((L, GW)), full((1, GW)),
        full((1, GW)), full((1, GW)), full((8, TB)), full((8, TB)),
        full((6, L, NS)), full((GW, 2 * NS)), full((2 * NS, GW)),
        full((1, GW)), full((GW, GW)), full((1, D)),
    ]
    out_shape = (
        jax.ShapeDtypeStruct((bsz, seq, D), F32),
        jax.ShapeDtypeStruct((bsz, GW, GW), F32),
        jax.ShapeDtypeStruct((bsz, GW, GW), F32),
        jax.ShapeDtypeStruct((bsz, GW, 128), F32),
        jax.ShapeDtypeStruct((bsz, 8, 128), F32),
        jax.ShapeDtypeStruct((bsz, 8, NS), F32),
        jax.ShapeDtypeStruct((bsz, 8, NS), F32),
        jax.ShapeDtypeStruct((bsz, MEM, GW), F32),
        jax.ShapeDtypeStruct((bsz, MEM, GW), F32),
    )
    out_specs = (
        pl.BlockSpec((1, TB, D), lambda b, t: (b, t, 0)),
        per_b(GW, GW), per_b(GW, GW), per_b(GW, 128), per_b(8, 128),
        per_b(8, NS), per_b(8, NS), per_b(MEM, GW), per_b(MEM, GW),
    )
    scratch = [
        pltpu.VMEM((TB, DP), F32),
        pltpu.VMEM((TB, D), BF16),
        pltpu.VMEM((GW, GW), F32),
        pltpu.VMEM((GW, GW), F32),
        pltpu.VMEM((GW, 128), F32),
        pltpu.VMEM((8, 128), F32),
        pltpu.VMEM((8, NS), F32),
        pltpu.VMEM((8, NS), F32),
        pltpu.VMEM((MEM, GW), BF16),
        pltpu.VMEM((MEM, GW), BF16),
        pltpu.VMEM((TB, 2 * NS), F32),
        pltpu.VMEM((TB, 2 * NS), BF16),
    ]
    kern = functools.partial(_prompt_kernel, last_layer=last_layer, n_tblocks=nt)
    return pl.pallas_call(
        kern, grid=(bsz, nt), in_specs=in_specs, out_specs=out_specs, out_shape=out_shape,
        scratch_shapes=scratch, name="prompt_layer",
        compiler_params=pltpu.CompilerParams(
            dimension_semantics=("arbitrary", "arbitrary"), vmem_limit_bytes=VMEM_LIMIT),
    )(x, mem, w["norm_w"], w["w_in"], w["w_out"], w["w_mem_k"], w["w_mem_v"],
      consts["cos_p"], consts["s1_p"], consts["s2_p"],
      consts["dec"], consts["qdec"], consts["kdec"], consts["cdec"],
      w["ret_gn"], w["ml_gn"], w["b_i8"], w["b_f8"],
      w["tab"], w["bt"], w["ct"], w["s5_d"], w["w_glu"], w["final_norm_w"])


def _sample_kernel(x_ref, normw_ref, win_ref, wout_ref, cos_ref, s1_ref, s2_ref, gam_ref, gcol_ref,
                   retgn_ref, mlgn_ref, bi_ref, bf_ref,
                   ab_ref, bt_ref, ct_ref, dsk_ref, wglu_ref, fnw_ref,
                   m0_ref, n0_ref, x0re_ref, x0im_ref,
                   rets_ref, mlc_ref, ck_ref, cv_ref,
                   y_ref, retn_ref, mlcn_ref, mlnn_ref, mlmn_ref, s5re_ref, s5im_ref,
                   proj_ref, qr8_ref, kr8_ref, vrh_ref, qmh_ref, vw8_ref, kmh_ref, a8_ref, qa8_ref,
                   ost_ref, cq_ref, xa_ref, wi_ref, ws_ref, emt_ref,
                   *, last_layer, n_blocks):
    g = pl.program_id(0)
    nsamp = x_ref.shape[0]
    hm = _head_masks()
    ones_bd = _ones_matrix()
    avg = _avg_matrix()
    mask8 = (lax.broadcasted_iota(jnp.int32, (8, GW), 0)
             == lax.broadcasted_iota(jnp.int32, (8, GW), 1) // DH)
    expand = (lax.broadcasted_iota(jnp.int32, (128, GW), 0)
              == lax.broadcasted_iota(jnp.int32, (128, GW), 1) // DH).astype(BF16)
    fold = (lax.broadcasted_iota(jnp.int32, (GW, DH), 0) % DH
            == lax.broadcasted_iota(jnp.int32, (GW, DH), 1)).astype(BF16)
    tile4 = (lax.broadcasted_iota(jnp.int32, (DH, GW), 0)
             == lax.broadcasted_iota(jnp.int32, (DH, GW), 1) % DH).astype(BF16)

    def rows8(v):
        n = v.shape[0]
        return jnp.where(mask8[None], v[:, None, :], 0.0).reshape(8 * n, GW)

    def P(blk):
        return proj_ref[:, blk * GW:(blk + 1) * GW]

    def gates():
        gt = P(GT)
        ig = gt[:, :128] + bi_ref[...]
        lf = _log_sigmoid(gt[:, 128:] + bf_ref[...])
        a = lf + m0_ref[...]
        mt = jnp.maximum(a, ig)
        return ig, a, mt

    @pl.when(g == 0)
    def _pre():
        hn = _rms_norm(x_ref[...], normw_ref[...]).astype(BF16)
        proj_ref[...] = _dot(hn, win_ref[...])
        cosr, s1, s2 = cos_ref[...], s1_ref[...], s2_ref[...]
        rq = _rope(P(RQ), cosr, s1, s2)
        rk = _rope(P(RK), cosr, s1, s2)
        qr8_ref[...] = rows8(rq * gam_ref[...])
        kr8_ref[...] = rows8(rk)
        vrh_ref[...] = _dot(rows8(P(RV)).astype(BF16), fold)
        ig, a, mt = gates()
        wi = _dot_x2(jnp.exp(ig - mt), expand)
        ws = _dot_x2(jnp.exp(a - mt), expand)
        wi_ref[...] = wi
        ws_ref[...] = ws
        emt_ref[...] = _dot_x2(jnp.exp(-mt), expand)
        qmh_ref[...] = _dot(rows8(P(MQ)).astype(BF16), fold)
        kmh_ref[...] = _dot(rows8(P(MK)).astype(BF16), fold)
        vw8_ref[...] = rows8(wi * P(MV))
        hi, lo = _split2(ws)
        rid = lax.broadcasted_iota(jnp.int32, (1, 8, GW), 1)
        a8 = jnp.where(rid == 0, hi.astype(F32)[:, None, :],
                       jnp.where(rid == 1, lo.astype(F32)[:, None, :], 0.0))
        a8_ref[...] = a8.reshape(8 * nsamp, GW)
        qa8_ref[...] = rows8(P(AQ))
        mlmn_ref[...] = mt
        su = P(SU)
        bu = _dot(su.astype(BF16), bt_ref[...])
        are, aim = ab_ref[0:1, :], ab_ref[1:2, :]
        x0r, x0i = x0re_ref[...], x0im_ref[...]
        s5re_ref[...] = are * x0r - aim * x0i + bu[:, :NS]
        s5im_ref[...] = are * x0i + aim * x0r + bu[:, NS:]

    ones8 = jnp.ones((8, DH), BF16)

    def sample(i, carry):
        bg = g * SB + i
        r8 = pl.ds(pl.multiple_of(bg * 8, 8), 8)
        s_b = rets_ref[i]
        t_o = _dot(qr8_ref[r8, :].astype(BF16), s_b.astype(BF16))
        o_row = jnp.sum(jnp.where(mask8, _dot_x2(t_o, tile4), 0.0), axis=0, keepdims=True)
        ost_ref[pl.ds(bg, 1), :] = o_row
        upd = _dot_tn(kr8_ref[r8, :].astype(BF16), vrh_ref[r8, :].astype(BF16))
        retn_ref[i] = s_b * gcol_ref[...] + upd
        c_b = mlc_ref[i]
        t_c = _dot_nt(qmh_ref[r8, :].astype(BF16), c_b.astype(BF16))
        cq_ref[pl.ds(bg, 1), :] = jnp.sum(jnp.where(mask8, t_c, 0.0), axis=0, keepdims=True)
        wscol = _dot_tn(a8_ref[r8, :].astype(BF16), ones8)
        updc = _dot_tn(vw8_ref[r8, :].astype(BF16), kmh_ref[r8, :].astype(BF16))
        mlcn_ref[i] = c_b * wscol + updc
        k_b = ck_ref[i].astype(BF16)
        v_b = cv_ref[i].astype(BF16)
        sc = _dot_nt(qa8_ref[r8, :].astype(BF16), k_b)
        e = jnp.exp(sc - jnp.max(sc, axis=-1, keepdims=True))
        p = e / jnp.sum(e, axis=-1, keepdims=True)
        t_a = _dot(p.astype(BF16), v_b)
        xa_ref[pl.ds(bg, 1), :] = jnp.sum(jnp.where(mask8, t_a, 0.0), axis=0, keepdims=True)
        return carry

    lax.fori_loop(0, SB, sample, 0)

    @pl.when(g == n_blocks - 1)
    def _post():
        cosr, s1, s2 = cos_ref[...], s1_ref[...], s2_ref[...]
        rq = _rope(P(RQ), cosr, s1, s2)
        rk = _rope(P(RK), cosr, s1, s2)
        ro = _dot_x2(rq * rk, ones_bd) * P(RV) + ost_ref[...]
        ret_out = _head_norm(ro, retgn_ref[...], avg) * _silu(P(RG))
        mq, mk_, mv_ = P(MQ), P(MK), P(MV)
        wi, ws, emt = wi_ref[...], ws_ref[...], emt_ref[...]
        n0 = n0_ref[...]
        s = _dot_x2(mq * mk_, ones_bd) * wi
        num = s * mv_ + ws * cq_ref[...]
        den = s + ws * _dot_x2(n0 * mq, ones_bd)
        mh = num / jnp.maximum(jnp.abs(den), emt)
        mlnn_ref[...] = ws * n0 + wi * mk_
        mh = mh * _sigmoid(P(MO))
        ml_out = _head_norm(mh, mlgn_ref[...], avg) * _silu(P(MG))
        su = P(SU)
        xcat = jnp.concatenate([s5re_ref[...], s5im_ref[...]], axis=1).astype(BF16)
        sy = _dot(xcat, ct_ref[...]) + dsk_ref[...] * su
        sy = _gelu_tanh(sy)
        sy = sy * _sigmoid(_dot(sy.astype(BF16), wglu_ref[...]))
        s5_out = sy * _silu(P(SG))
        xa_out = xa_ref[...] * _silu(P(AG))
        mix = jnp.concatenate([ret_out, ml_out, s5_out, xa_out], axis=1).astype(BF16)
        y = x_ref[...] + _dot(mix, wout_ref[...])
        if last_layer:
            y = _rms_norm(y, fnw_ref[...])
        y_ref[...] = y


def _sample_layer(x, st, w, consts, last_layer):
    nsamp = x.shape[0]
    nb = nsamp // SB
    full = lambda shape: pl.BlockSpec(shape, lambda g: (0,) * len(shape))
    once = lambda shape: pl.BlockSpec(shape, lambda g: (0,) * len(shape),
                                      pipeline_mode=pl.Buffered(1))
    blk = lambda r, c: pl.BlockSpec((SB, r, c), lambda g: (g, 0, 0))
    in_specs = [
        once((nsamp, D)), once((1, D)), once((D, DP)), once((D, D)),
        once((1, GW)), once((1, GW)), once((1, GW)), once((1, GW)), once((GW, DH)),
        once((1, GW)), once((1, GW)), once((1, 128)), once((1, 128)),
        once((2, NS)), once((GW, 2 * NS)), once((2 * NS, GW)),
        once((1, GW)), once((GW, GW)), once((1, D)),
        once((nsamp, 128)), once((nsamp, GW)), once((nsamp, NS)), once((nsamp, NS)),
        blk(GW, DH), blk(GW, DH), blk(MEM, GW), blk(MEM, GW),
    ]
    out_shape = (
        jax.ShapeDtypeStruct((nsamp, D), F32),
        jax.ShapeDtypeStruct((nsamp, GW, DH), F32),
        jax.ShapeDtypeStruct((nsamp, GW, DH), F32),
        jax.ShapeDtypeStruct((nsamp, GW), F32),
        jax.ShapeDtypeStruct((nsamp, 128), F32),
        jax.ShapeDtypeStruct((nsamp, NS), F32),
        jax.ShapeDtypeStruct((nsamp, NS), F32),
    )
    out_specs = (
        full((nsamp, D)), blk(GW, DH), blk(GW, DH), full((nsamp, GW)), full((nsamp, 128)),
        full((nsamp, NS)), full((nsamp, NS)),
    )
    r8 = 8 * nsamp
    scratch = [
        pltpu.VMEM((nsamp, DP), F32),
        pltpu.VMEM((r8, GW), F32), pltpu.VMEM((r8, GW), F32), pltpu.VMEM((r8, DH), F32),
        pltpu.VMEM((r8, DH), F32), pltpu.VMEM((r8, GW), F32), pltpu.VMEM((r8, DH), F32),
        pltpu.VMEM((r8, GW), F32), pltpu.VMEM((r8, GW), F32),
        pltpu.VMEM((nsamp, GW), F32), pltpu.VMEM((nsamp, GW), F32), pltpu.VMEM((nsamp, GW), F32),
        pltpu.VMEM((nsamp, GW), F32), pltpu.VMEM((nsamp, GW), F32), pltpu.VMEM((nsamp, GW), F32),
    ]
    kern = functools.partial(_sample_kernel, last_layer=last_layer, n_blocks=nb)
    return pl.pallas_call(
        kern, grid=(nb,), in_specs=in_specs, out_specs=out_specs, out_shape=out_shape,
        scratch_shapes=scratch, name="sample_layer",
        compiler_params=pltpu.CompilerParams(
            dimension_semantics=("arbitrary",), vmem_limit_bytes=VMEM_LIMIT),
    )(x, w["norm_w"], w["w_in"], w["w_out"],
      consts["cos_s"], consts["s1_s"], consts["s2_s"], consts["gam"], consts["gcol"],
      w["ret_gn"], w["ml_gn"], w["b_i"], w["b_f"],
      w["ab"], w["bt"], w["ct"], w["s5_d"], w["w_glu"], w["final_norm_w"],
      st["m"], st["n"], st["s5_re"], st["s5_im"], st["ret"], st["c"], st["mem_k"], st["mem_v"])


def _rope_tables(pos):
    half = DH // 2
    inv = ROPE_BASE ** (-jnp.arange(half, dtype=F32) / half)
    ang = pos.astype(F32)[:, None] * inv[None, :]
    cos, sin = jnp.cos(ang), jnp.sin(ang)
    zero = jnp.zeros_like(sin)
    c = jnp.tile(jnp.concatenate([cos, cos], axis=-1), (1, NH))
    s1 = jnp.tile(jnp.concatenate([zero, sin], axis=-1), (1, NH))
    s2 = jnp.tile(jnp.concatenate([-sin, zero], axis=-1), (1, NH))
    return c, s1, s2


def _constants(seq):
    lg = jnp.log1p(-jnp.power(2.0, -5.0 - jnp.arange(NH, dtype=F32)))[:, None]
    idx = jnp.arange(L, dtype=F32)
    diff = idx[:, None] - idx[None, :]
    decay = jnp.where(diff >= 0, jnp.exp(lg[:, :, None] * jnp.maximum(diff, 0.0)), 0.0)
    rep = lambda t: jnp.repeat(t, DH, axis=0).T
    consts = {
        "dec": decay.reshape(NH * L, L),
        "qdec": rep(jnp.exp(lg * (idx + 1.0))),
        "kdec": rep(jnp.exp(lg * (L - 1.0 - idx))),
        "cdec": rep(jnp.exp(lg * L)),
        "gam": rep(jnp.exp(lg * 1.0)),
    }
    consts["gcol"] = jnp.broadcast_to(consts["gam"].T, (GW, DH))
    consts["cos_p"], consts["s1_p"], consts["s2_p"] = _rope_tables(jnp.arange(seq, dtype=jnp.int32))
    consts["cos_s"], consts["s1_s"], consts["s2_s"] = _rope_tables(
        PAST_LEN + jnp.arange(1, dtype=jnp.int32))
    return consts


def _pack_w_in(w_in):
    sizes = (GW,) * 9 + (NH, NH) + (GW,) * 4
    offs = np.concatenate([[0], np.cumsum(sizes)])
    seg = [w_in[:, int(offs[i]):int(offs[i + 1])] for i in range(len(sizes))]
    scale = DH ** -0.5
    pad = lambda t: jnp.pad(t, ((0, 0), (0, 128 - t.shape[1])))
    blocks = [seg[0], seg[1] * scale, seg[2], seg[3],
              seg[4], seg[5] * scale, seg[6], seg[7], seg[8],
              seg[11], seg[12], seg[13] * scale, seg[14],
              pad(seg[9]), pad(seg[10])]
    return jnp.concatenate(blocks, axis=1).astype(BF16)


def kernel(x_prompt, x_sample, mem_prompt, state_ret, state_mlstm_c, state_mlstm_n, state_mlstm_m,
           state_s5_re, state_s5_im, cache_mem_k, cache_mem_v,
           norm_w, w_in, ret_gn, ml_b_i, ml_b_f, ml_gn,
           s5_a_re, s5_a_im, s5_log_dt, s5_b_re, s5_b_im, s5_c_re, s5_c_im, s5_d, s5_w_glu,
           w_mem_k, w_mem_v, w_out, final_norm_w):
    depth = norm_w.shape[0]
    bp, seq, _ = x_prompt.shape
    bs = x_sample.shape[0]
    consts = _constants(seq)
    ab, bt, ct, tab = _s5_prepare(s5_a_re, s5_a_im, s5_log_dt, s5_b_re, s5_b_im, s5_c_re, s5_c_im)
    pad4 = lambda t: jnp.pad(t, ((0, 0), (0, 128 - t.shape[-1])))
    rows8 = lambda t: jnp.pad(jnp.broadcast_to(t[:, None], (NH, TB)), ((0, 8 - NH), (0, 0)))

    hp = x_prompt
    hs = x_sample.reshape(bs, D)
    outs_p = [[] for _ in range(8)]
    outs_s = [[] for _ in range(6)]
    for l in range(depth):
        w = {
            "norm_w": norm_w[l][None], "w_in": _pack_w_in(w_in[l]), "w_out": w_out[l].astype(BF16),
            "w_mem_k": w_mem_k[l].astype(BF16), "w_mem_v": w_mem_v[l].astype(BF16),
            "ret_gn": ret_gn[l][None], "ml_gn": ml_gn[l][None],
            "b_i": pad4(ml_b_i[l][None]), "b_f": pad4(ml_b_f[l][None]),
            "b_i8": rows8(ml_b_i[l]), "b_f8": rows8(ml_b_f[l]),
            "ab": ab[l], "tab": tab[l], "bt": bt[l], "ct": ct[l],
            "s5_d": s5_d[l][None], "w_glu": s5_w_glu[l].astype(BF16),
            "final_norm_w": final_norm_w[None],
        }
        last = l == depth - 1
        hp, rs, cs, ns, ms, xr, xi, mk, mv = _prompt_layer(hp, mem_prompt, w, consts, last)
        diag = lambda t: jnp.stack([t[:, h * DH:(h + 1) * DH, h * DH:(h + 1) * DH]
                                    for h in range(NH)], axis=1)
        outs_p[0].append(diag(rs))
        outs_p[1].append(jnp.swapaxes(diag(cs), -1, -2))
        outs_p[2].append(jnp.stack([ns[:, h * DH:(h + 1) * DH, h] for h in range(NH)], axis=1))
        outs_p[3].append(ms[:, :NH, 0])
        outs_p[4].append(xr[:, 0].reshape(bp, S5G, S5P))
        outs_p[5].append(xi[:, 0].reshape(bp, S5G, S5P))
        outs_p[6].append(mk.reshape(bp, MEM, NH, DH))
        outs_p[7].append(mv.reshape(bp, MEM, NH, DH))

        st = {
            "m": pad4(state_mlstm_m[l]), "n": state_mlstm_n[l].reshape(bs, GW),
            "s5_re": state_s5_re[l].reshape(bs, NS), "s5_im": state_s5_im[l].reshape(bs, NS),
            "ret": state_ret[l].reshape(bs, GW, DH), "c": state_mlstm_c[l].reshape(bs, GW, DH),
            "mem_k": cache_mem_k[l].reshape(bs, MEM, GW), "mem_v": cache_mem_v[l].reshape(bs, MEM, GW),
        }
        hs, rn, cn, nn, mn, sr, si = _sample_layer(hs, st, w, consts, last)
        outs_s[0].append(rn.reshape(bs, NH, DH, DH))
        outs_s[1].append(cn.reshape(bs, NH, DH, DH))
        outs_s[2].append(nn.reshape(bs, NH, DH))
        outs_s[3].append(mn[:, :NH])
        outs_s[4].append(sr.reshape(bs, S5G, S5P))
        outs_s[5].append(si.reshape(bs, S5G, S5P))

    stk = jnp.stack
    return (hp, hs.reshape(bs, 1, D),
            stk(outs_p[0]), stk(outs_s[0]), stk(outs_p[1]), stk(outs_s[1]),
            stk(outs_p[2]), stk(outs_s[2]), stk(outs_p[3]), stk(outs_s[3]),
            stk(outs_p[4]), stk(outs_s[4]), stk(outs_p[5]), stk(outs_s[5]),
            stk(outs_p[6]), stk(outs_p[7]))
```

```python
import functools
import math

import numpy as np
import jax
import jax.numpy as jnp
from jax import lax
from jax.experimental import pallas as pl
from jax.experimental.pallas import tpu as pltpu

F32 = jnp.float32
BF16 = jnp.bfloat16

D = 1024
GW = 256
NH = 4
DH = 64
L = 128
LS = 8
S5G = 16
S5P = 64
S5C = 16
NS = S5G * S5P
MEM = 256
EPS = 1e-6
NEG_INF = -1e30
ROPE_BASE = 10000.0
PAST_LEN = 16384

TB = 512
SB = 8
XQ = 256
NBLK = 14
DP = NBLK * GW
(RQ, RK, RV, RG, MQ, MK, MV, MO, MG, SU, SG, AQ, AG, GT) = range(NBLK)

VMEM_LIMIT = 56 * 1024 * 1024


def _dot(a, b):
    return jnp.dot(a, b, preferred_element_type=F32)


def _dot_nt(a, b):
    return lax.dot_general(a, b, (((1,), (1,)), ((), ())), preferred_element_type=F32)


def _dot_tn(a, b):
    return lax.dot_general(a, b, (((0,), (0,)), ((), ())), preferred_element_type=F32)


def _split2(x):
    hi = x.astype(BF16)
    lo = (x - hi.astype(F32)).astype(BF16)
    return hi, lo


def _dot_x2(x, w):
    hi, lo = _split2(x)
    return _dot(hi, w) + _dot(lo, w)


def _dot_x3(x, w):
    hi = x.astype(BF16)
    r = x - hi.astype(F32)
    mid = r.astype(BF16)
    lo = (r - mid.astype(F32)).astype(BF16)
    return _dot(hi, w) + _dot(mid, w) + _dot(lo, w)


def _sigmoid(x):
    return 1.0 / (1.0 + jnp.exp(-x))


def _silu(x):
    return x * _sigmoid(x)


def _log_sigmoid(x):
    return jnp.minimum(x, 0.0) - jnp.log1p(jnp.exp(-jnp.abs(x)))


def _gelu_tanh(x):
    c = math.sqrt(2.0 / math.pi)
    return x * (0.5 * (1.0 + jnp.tanh(c * (x + 0.044715 * (x * x * x)))))


def _lane_head(n):
    return lax.broadcasted_iota(jnp.int32, (1, n), 1) // DH


def _head_masks():
    lh = _lane_head(GW)
    return [lh == h for h in range(NH)]


def _block_diag_mask():
    r = lax.broadcasted_iota(jnp.int32, (GW, GW), 0) // DH
    c = lax.broadcasted_iota(jnp.int32, (GW, GW), 1) // DH
    return r == c


def _avg_matrix():
    return jnp.where(_block_diag_mask(), 1.0 / DH, 0.0).astype(BF16)


def _ones_matrix():
    return jnp.where(_block_diag_mask(), 1.0, 0.0).astype(BF16)


def _rope(x, cos, s1, s2):
    outs = []
    for j in range(2):
        xs = x[:, j * 128:(j + 1) * 128]
        outs.append(xs * cos + pltpu.roll(xs, 32, 1) * s1 + pltpu.roll(xs, 96, 1) * s2)
    return jnp.concatenate(outs, axis=1)


def _head_norm(x, gain, avg):
    mu = _dot_x2(x, avg)
    d = x - mu
    var = _dot_x2(d * d, avg)
    return d * lax.rsqrt(var + EPS) * gain


def _rms_norm(x, w):
    ms = jnp.mean(x * x, axis=-1, keepdims=True)
    return x * lax.rsqrt(ms + EPS) * w


def _stack_heads(x, hm):
    return jnp.concatenate([jnp.where(hm[h], x, 0.0) for h in range(NH)], axis=0)


def _fold_heads(r, hm, rows):
    out = jnp.where(hm[0], r[0:rows], 0.0)
    for h in range(1, NH):
        out = out + jnp.where(hm[h], r[h * rows:(h + 1) * rows], 0.0)
    return out


def _s5_prep_kernel(are_ref, aim_ref, ldt_ref, bre_ref, bim_ref, cre_ref, cim_ref,
                    ab_ref, bt_ref, ct_ref, tab_ref):
    a_re = are_ref[0]
    a_im = aim_ref[0]
    dt = jnp.exp(ldt_ref[0])
    lam_re = a_re * dt
    lam_im = a_im * dt
    mag = jnp.exp(lam_re)
    ab_re = mag * jnp.cos(lam_im)
    ab_im = mag * jnp.sin(lam_im)
    den = a_re * a_re + a_im * a_im
    nr = ab_re - 1.0
    ni = ab_im
    f_re = (nr * a_re + ni * a_im) / den
    f_im = (ni * a_re - nr * a_im) / den
    ab_ref[0, 0:1, :] = ab_re
    ab_ref[0, 1:2, :] = ab_im
    b_re = bre_ref[0]
    b_im = bim_ref[0]
    bb_re = (f_re * b_re - f_im * b_im).astype(BF16)
    bb_im = (f_re * b_im + f_im * b_re).astype(BF16)
    rep_r = (lax.broadcasted_iota(jnp.int32, (GW, S5C), 0) % S5C
             == lax.broadcasted_iota(jnp.int32, (GW, S5C), 1)).astype(BF16)
    in_blk = (lax.broadcasted_iota(jnp.int32, (GW, NS), 0) // S5C
              == lax.broadcasted_iota(jnp.int32, (GW, NS), 1) // S5P)
    bt_ref[0, :, :NS] = jnp.where(in_blk, _dot(rep_r, bb_re), 0.0).astype(BF16)
    bt_ref[0, :, NS:] = jnp.where(in_blk, _dot(rep_r, bb_im), 0.0).astype(BF16)
    rep_c = (lax.broadcasted_iota(jnp.int32, (S5C, GW), 0)
             == lax.broadcasted_iota(jnp.int32, (S5C, GW), 1) % S5C).astype(BF16)
    out_blk = (lax.broadcasted_iota(jnp.int32, (NS, GW), 0) // S5P
               == lax.broadcasted_iota(jnp.int32, (NS, GW), 1) // S5C)
    ct_ref[0, :NS, :] = jnp.where(out_blk, _dot(cre_ref[0].astype(BF16), rep_c), 0.0).astype(BF16)
    ct_ref[0, NS:, :] = jnp.where(out_blk, -_dot(cim_ref[0].astype(BF16), rep_c), 0.0).astype(BF16)
    i = (lax.broadcasted_iota(jnp.int32, (L, NS), 0) % LS).astype(F32)
    for slot, k in ((0, -i), (2, i), (4, i + 1.0)):
        pmag = jnp.exp(k * lam_re)
        tab_ref[0, slot] = pmag * jnp.cos(k * lam_im)
        tab_ref[0, slot + 1] = pmag * jnp.sin(k * lam_im)


def _s5_prepare(a_re, a_im, log_dt, b_re, b_im, c_re, c_im):
    depth = a_re.shape[0]
    are = a_re.reshape(depth, 1, NS)
    aim = a_im.reshape(depth, 1, NS)
    ldt = jnp.repeat(log_dt, S5P, axis=-1).reshape(depth, 1, NS)
    bre = jnp.transpose(b_re, (0, 3, 1, 2)).reshape(depth, S5C, NS)
    bim = jnp.transpose(b_im, (0, 3, 1, 2)).reshape(depth, S5C, NS)
    cre = jnp.transpose(c_re, (0, 1, 3, 2)).reshape(depth, NS, S5C)
    cim = jnp.transpose(c_im, (0, 1, 3, 2)).reshape(depth, NS, S5C)
    per_layer = lambda *shape: pl.BlockSpec((1,) + shape, lambda l: (l,) + (0,) * len(shape))
    out_shape = (jax.ShapeDtypeStruct((depth, 2, NS), F32),
                 jax.ShapeDtypeStruct((depth, GW, 2 * NS), BF16),
                 jax.ShapeDtypeStruct((depth, 2 * NS, GW), BF16),
                 jax.ShapeDtypeStruct((depth, 6, L, NS), F32))
    return pl.pallas_call(
        _s5_prep_kernel, grid=(depth,),
        in_specs=[per_layer(1, NS), per_layer(1, NS), per_layer(1, NS),
                  per_layer(S5C, NS), per_layer(S5C, NS), per_layer(NS, S5C), per_layer(NS, S5C)],
        out_specs=(per_layer(2, NS), per_layer(GW, 2 * NS), per_layer(2 * NS, GW),
                   per_layer(6, L, NS)),
        out_shape=out_shape, name="s5_prepare",
        compiler_params=pltpu.CompilerParams(dimension_semantics=("arbitrary",)),
    )(are, aim, ldt, bre, bim, cre, cim)


def _prompt_kernel(x_ref, mem_ref, normw_ref, win_ref, wout_ref, wmkv_ref,
                   cos_ref, s1_ref, s2_ref, dec_ref, qdec_ref, kdec_ref, cdec_ref,
                   retgn_ref, mlgn_ref, bi_ref, bf_ref,
                   tab_ref, bt_ref, ct_ref, dsk_ref, wglu_ref, fnw_ref,
                   y_ref, rets_ref, mlc_ref, mln_ref, mlm_ref, s5re_ref, s5im_ref,
                   memk_ref, memv_ref,
                   proj_ref, mix_ref, s_ref, c_ref, n_ref, m_ref, xre_ref, xim_ref,
                   mk_ref, mv_ref, bu_ref, xcat_ref, car_ref, *, last_layer, n_tblocks):
    t = pl.program_id(1)
    hm = _head_masks()
    bd = _block_diag_mask()
    avg = _avg_matrix()
    lane128 = lax.broadcasted_iota(jnp.int32, (1, 128), 1)
    row_i = lax.broadcasted_iota(jnp.int32, (L, 128), 0)
    col_i = lax.broadcasted_iota(jnp.int32, (L, 128), 1)
    causal = row_i >= col_i
    tri_u = jnp.where(row_i <= col_i, 1.0, 0.0).astype(BF16)
    tri_sub = jnp.where(causal & (row_i // LS == col_i // LS), 1.0, 0.0)
    blk_sum = jnp.where(lax.broadcasted_iota(jnp.int32, (L // LS, 128), 0)
                        == lax.broadcasted_iota(jnp.int32, (L // LS, 128), 1) // LS, 1.0, 0.0)
    tri_ext = jnp.concatenate([tri_sub, blk_sum], axis=0).astype(BF16)

    @pl.when(t == 0)
    def _init():
        s_ref[...] = jnp.zeros_like(s_ref)
        c_ref[...] = jnp.zeros_like(c_ref)
        n_ref[...] = jnp.zeros_like(n_ref)
        m_ref[...] = jnp.zeros_like(m_ref)
        xre_ref[...] = jnp.zeros_like(xre_ref)
        xim_ref[...] = jnp.zeros_like(xim_ref)
        mkv = _dot(mem_ref[0].astype(BF16), wmkv_ref[...])
        mk = mkv[:, :GW]
        mv = mkv[:, GW:]
        memk_ref[0] = mk
        memv_ref[0] = mv
        mk_ref[...] = mk.astype(BF16)
        mv_ref[...] = mv.astype(BF16)

    x = x_ref[0]
    hn = _rms_norm(x, normw_ref[...]).astype(BF16)
    proj_ref[...] = _dot(hn, win_ref[...])

    nc = TB // L
    crow = [slice(c * L, (c + 1) * L) for c in range(nc)]

    def PB(blk, rows=slice(None)):
        return proj_ref[rows, blk * GW:(blk + 1) * GW]

    for piece in range(TB // XQ):
        rs = slice(piece * XQ, (piece + 1) * XQ)
        sc = _dot_nt(_stack_heads(PB(AQ, rs), hm).astype(BF16), mk_ref[...])
        e = jnp.exp(sc - jnp.max(sc, axis=-1, keepdims=True))
        p = e / jnp.sum(e, axis=-1, keepdims=True)
        xa = _fold_heads(_dot(p.astype(BF16), mv_ref[...]), hm, XQ)
        mix_ref[rs, 3 * GW:4 * GW] = (xa * _silu(PB(AG, rs))).astype(BF16)

    su = PB(SU)
    bu_ref[...] = _dot(su.astype(BF16), bt_ref[...])
    lanes = [(slice(j * 128, (j + 1) * 128), slice(NS + j * 128, NS + (j + 1) * 128))
             for j in range(NS // 128)]
    for c in range(nc):
        for lre, lim in lanes:
            br = bu_ref[crow[c], lre]
            bi = bu_ref[crow[c], lim]
            wr = tab_ref[0, :, lre]
            wi = tab_ref[1, :, lre]
            xcat_ref[crow[c], lre] = (wr * br - wi * bi).astype(BF16)
            xcat_ref[crow[c], lim] = (wr * bi + wi * br).astype(BF16)
    for c in range(nc):
        z = _dot(tri_ext, xcat_ref[crow[c], :])
        bu_ref[crow[c], :] = z[0:L]
        car_ref[c * (L // LS):(c + 1) * (L // LS), :] = z[L:L + L // LS]
    for c in range(nc):
        for lre, lim in lanes:
            zr = bu_ref[crow[c], lre]
            zi = bu_ref[crow[c], lim]
            pr = tab_ref[2, :, lre]
            pi = tab_ref[3, :, lre]
            bu_ref[crow[c], lre] = pr * zr - pi * zi
            bu_ref[crow[c], lim] = pr * zi + pi * zr
    nblk = TB // LS
    rowb = lax.broadcasted_iota(jnp.int32, (nblk, 128), 0)
    for lre, lim in lanes:
        zr = car_ref[:, lre]
        zi = car_ref[:, lim]
        pr = tab_ref[2, LS - 1:LS, lre]
        pi = tab_ref[3, LS - 1:LS, lre]
        er = pr * zr - pi * zi
        ei = pr * zi + pi * zr
        pr = tab_ref[4, LS - 1:LS, lre]
        pi = tab_ref[5, LS - 1:LS, lre]
        c0r = xre_ref[0:1, lre]
        c0i = xim_ref[0:1, lre]
        er = er + jnp.where(rowb == 0, pr * c0r - pi * c0i, 0.0)
        ei = ei + jnp.where(rowb == 0, pr * c0i + pi * c0r, 0.0)
        for k in range(nblk.bit_length() - 1):
            d = 1 << k
            sr = jnp.where(rowb >= d, pltpu.roll(er, d, 0), 0.0)
            si = jnp.where(rowb >= d, pltpu.roll(ei, d, 0), 0.0)
            er, ei = er + pr * sr - pi * si, ei + pr * si + pi * sr
            pr, pi = pr * pr - pi * pi, 2.0 * (pr * pi)
        xre_ref[:, lre] = jnp.broadcast_to(er[nblk - 1:nblk, :], (8, 128))
        xim_ref[:, lre] = jnp.broadcast_to(ei[nblk - 1:nblk, :], (8, 128))
        car_ref[:, lre] = jnp.where(rowb == 0, c0r, pltpu.roll(er, 1, 0))
        car_ref[:, lim] = jnp.where(rowb == 0, c0i, pltpu.roll(ei, 1, 0))
    qr = tab_ref[4, 0:16, :]
    qi = tab_ref[5, 0:16, :]
    per_tile = 16 // LS

    def add_carry(sp, carry):
        rs = pl.ds(pl.multiple_of(sp * 16, 16), 16)
        cin = jnp.concatenate(
            [jnp.broadcast_to(car_ref[pl.ds(sp * per_tile + k, 1), :], (LS, 2 * NS))
             for k in range(per_tile)], axis=0)
        cr, ci_ = cin[:, :NS], cin[:, NS:]
        xcat_ref[rs, 0:NS] = (bu_ref[rs, 0:NS] + qr * cr - qi * ci_).astype(BF16)
        xcat_ref[rs, NS:2 * NS] = (bu_ref[rs, NS:2 * NS] + qr * ci_ + qi * cr).astype(BF16)
        return carry

    lax.fori_loop(0, TB // 16, add_carry, 0, unroll=2)
    sy = _dot(xcat_ref[...], ct_ref[...]) + dsk_ref[...] * su
    sy = _gelu_tanh(sy)
    sy = sy * _sigmoid(_dot(sy.astype(BF16), wglu_ref[...]))
    mix_ref[:, 2 * GW:3 * GW] = (sy * _silu(PB(SG))).astype(BF16)

    rq = _rope(PB(RQ), cos_ref[...], s1_ref[...], s2_ref[...])
    rk = _rope(PB(RK), cos_ref[...], s1_ref[...], s2_ref[...])
    rv = PB(RV).astype(BF16)
    rkb = rk.astype(BF16)
    inner = [_dot_nt(_stack_heads(rq[crow[c]], hm).astype(BF16), rkb[crow[c]]) for c in range(nc)]
    pmat = [(inner[c] * dec_ref[...]).astype(BF16) for c in range(nc)]
    rloc = [_dot(pmat[c], rv[crow[c]]) for c in range(nc)]
    upd = [_dot_tn((rk[crow[c]] * kdec_ref[...]).astype(BF16), rv[crow[c]]) for c in range(nc)]
    st = [s_ref[...]]
    for c in range(nc):
        st.append(st[c] * cdec_ref[...] + jnp.where(bd, upd[c], 0.0))
    s_ref[...] = st[nc]
    ro = jnp.concatenate(
        [_fold_heads(rloc[c], hm, L)
         + _dot((rq[crow[c]] * qdec_ref[...]).astype(BF16), st[c].astype(BF16)) for c in range(nc)],
        axis=0)
    mix_ref[:, 0:GW] = (_head_norm(ro, retgn_ref[...], avg) * _silu(PB(RG))).astype(BF16)

    g = PB(GT)
    ig_r = jnp.concatenate([g[crow[c], 0:128].T[0:8] for c in range(nc)], axis=1) + bi_ref[...]
    lf_r = _log_sigmoid(
        jnp.concatenate([g[crow[c], 128:256].T[0:8] for c in range(nc)], axis=1) + bf_ref[...])
    b_r = [_dot_x3(lf_r[:, crow[c]], tri_u) for c in range(nc)]
    gd_r = [ig_r[:, crow[c]] - b_r[c] for c in range(nc)]
    lane_r = lax.broadcasted_iota(jnp.int32, (8, L), 1)
    cm_r = []
    for c in range(nc):
        v = gd_r[c]
        for k in range(7):
            d = 1 << k
            v = jnp.maximum(v, jnp.where(lane_r >= d, pltpu.roll(v, d, 1), NEG_INF))
        cm_r.append(v)
    m_prev = [m_ref[...]]
    mt_r = []
    for c in range(nc):
        mt = jnp.maximum(b_r[c] + m_prev[c], b_r[c] + cm_r[c])
        mt_r.append(mt)
        m_prev.append(jnp.broadcast_to(mt[:, L - 1:L], (8, L)))
    m_ref[...] = m_prev[nc]
    cols = []
    for c in range(nc):
        bm = b_r[c] - mt_r[c]
        ws = jnp.exp(b_r[c] + m_prev[c] - mt_r[c])
        wl = jnp.exp(gd_r[c] + jnp.broadcast_to(bm[:, L - 1:L], (8, L)))
        emt = jnp.exp(-mt_r[c])
        cols.append(jnp.concatenate([bm, ws, wl, emt, jnp.zeros((L - 32, L), F32)], axis=0).T)
    mqf = PB(MQ)
    mq = mqf.astype(BF16)
    mkf = PB(MK)
    mkb = mkf.astype(BF16)
    mv_ = PB(MV).astype(BF16)
    sraw = [_dot_nt(_stack_heads(mqf[crow[c]], hm).astype(BF16), mkb[crow[c]]) for c in range(nc)]
    smat, den_i = [], []
    for c in range(nc):
        parts = []
        for h in range(NH):
            arg = jnp.where(causal, cols[c][:, h:h + 1] + gd_r[c][h:h + 1, :], NEG_INF)
            parts.append(sraw[c][h * L:(h + 1) * L] * jnp.exp(arg))
        s_c = jnp.concatenate(parts, axis=0)
        den_i.append(jnp.sum(s_c, axis=-1, keepdims=True))
        smat.append(s_c.astype(BF16))
    rloc = [_dot(smat[c], mv_[crow[c]]) for c in range(nc)]
    ones_blk = jnp.ones((L, 128), BF16)
    nmask = (lax.broadcasted_iota(jnp.int32, (GW, 128), 0) // DH
             == lax.broadcasted_iota(jnp.int32, (GW, 128), 1))
    u = []
    for c in range(nc):
        kw = jnp.zeros((L, GW), F32)
        for h in range(NH):
            kw = kw + jnp.where(hm[h], mkf[crow[c]] * cols[c][:, 16 + h:17 + h], 0.0)
        vaug = jnp.concatenate([mv_[crow[c]], ones_blk], axis=1)
        u.append(_dot_tn(kw.astype(BF16), vaug))
    cst = [c_ref[...]]
    nst = [n_ref[...]]
    for c in range(nc):
        wsl256 = jnp.zeros((1, GW), F32)
        wsl128 = jnp.zeros((1, 128), F32)
        for h in range(NH):
            wsl = cols[c][L - 1:L, 8 + h:9 + h]
            wsl256 = wsl256 + jnp.where(hm[h], wsl, 0.0)
            wsl128 = wsl128 + jnp.where(lane128 == h, wsl, 0.0)
        cst.append(cst[c] * wsl256 + jnp.where(bd, u[c][:, :GW], 0.0))
        nst.append(nst[c] * wsl128 + jnp.where(nmask, u[c][:, GW:], 0.0))
    c_ref[...] = cst[nc]
    n_ref[...] = nst[nc]
    mhs = []
    for c in range(nc):
        qc = _dot(mq[crow[c]], cst[c].astype(BF16))
        qn = _dot(mq[crow[c]], nst[c].astype(BF16))
        mh = jnp.zeros((L, GW), F32)
        for h in range(NH):
            ws = cols[c][:, 8 + h:9 + h]
            den = den_i[c][h * L:(h + 1) * L] + ws * qn[:, h:h + 1]
            dn = jnp.maximum(jnp.abs(den), cols[c][:, 24 + h:25 + h])
            mh = mh + jnp.where(hm[h], (rloc[c][h * L:(h + 1) * L] + ws * qc) / dn, 0.0)
        mhs.append(mh)
    mh = jnp.concatenate(mhs, axis=0) * _sigmoid(PB(MO))
    mix_ref[:, GW:2 * GW] = (_head_norm(mh, mlgn_ref[...], avg) * _silu(PB(MG))).astype(BF16)

    y = x + _dot(mix_ref[...], wout_ref[...])
    if last_layer:
        y = _rms_norm(y, fnw_ref[...])
    y_ref[0] = y

    @pl.when(t == n_tblocks - 1)
    def _final():
        rets_ref[0] = s_ref[...]
        mlc_ref[0] = c_ref[...]
        mln_ref[0] = n_ref[...]
        mlm_ref[0] = m_ref[...]
        s5re_ref[0] = xre_ref[...]
        s5im_ref[0] = xim_ref[...]


def _prompt_layer(layer, x, mem, w, consts, last_layer):
    bsz, seq, _ = x.shape
    nt = seq // TB
    full = lambda shape: pl.BlockSpec(shape, lambda b, t: (0,) * len(shape),
                                      pipeline_mode=pl.Buffered(1))
    lyr = lambda shape: pl.BlockSpec((None,) + shape, lambda b, t: (layer,) + (0,) * len(shape),
                                     pipeline_mode=pl.Buffered(1))
    tok = lambda width: pl.BlockSpec((TB, width), lambda b, t: (t, 0))
    per_b = lambda r, c: pl.BlockSpec((1, r, c), lambda b, t: (b, 0, 0))
    in_specs = [
        pl.BlockSpec((1, TB, D), lambda b, t: (b, t, 0)),
        per_b(MEM, D),
        lyr((1, D)), lyr((D, DP)), lyr((D, D)), lyr((D, 2 * GW)),
        tok(128), tok(128), tok(128),
        full((NH * L, L)), full((L, GW)), full((L, GW)), full((1, GW)),
        lyr((1, GW)), lyr((1, GW)), lyr((8, TB)), lyr((8, TB)),
        lyr((6, L, NS)), lyr((GW, 2 * NS)), lyr((2 * NS, GW)),
        lyr((1, GW)), lyr((GW, GW)), full((1, D)),
    ]
    out_shape = (
        jax.ShapeDtypeStruct((bsz, seq, D), F32),
        jax.ShapeDtypeStruct((bsz, GW, GW), F32),
        jax.ShapeDtypeStruct((bsz, GW, GW), F32),
        jax.ShapeDtypeStruct((bsz, GW, 128), F32),
        jax.ShapeDtypeStruct((bsz, 8, 128), F32),
        jax.ShapeDtypeStruct((bsz, 8, NS), F32),
        jax.ShapeDtypeStruct((bsz, 8, NS), F32),
        jax.ShapeDtypeStruct((bsz, MEM, GW), F32),
        jax.ShapeDtypeStruct((bsz, MEM, GW), F32),
    )
    out_specs = (
        pl.BlockSpec((1, TB, D), lambda b, t: (b, t, 0)),
        per_b(GW, GW), per_b(GW, GW), per_b(GW, 128), per_b(8, 128),
        per_b(8, NS), per_b(8, NS), per_b(MEM, GW), per_b(MEM, GW),
    )
    scratch = [
        pltpu.VMEM((TB, DP), F32),
        pltpu.VMEM((TB, D), BF16),
        pltpu.VMEM((GW, GW), F32),
        pltpu.VMEM((GW, GW), F32),
        pltpu.VMEM((GW, 128), F32),
        pltpu.VMEM((8, 128), F32),
        pltpu.VMEM((8, NS), F32),
        pltpu.VMEM((8, NS), F32),
        pltpu.VMEM((MEM, GW), BF16),
        pltpu.VMEM((MEM, GW), BF16),
        pltpu.VMEM((TB, 2 * NS), F32),
        pltpu.VMEM((TB, 2 * NS), BF16),
        pltpu.VMEM((TB // LS, 2 * NS), F32),
    ]
    kern = functools.partial(_prompt_kernel, last_layer=last_layer, n_tblocks=nt)
    return pl.pallas_call(
        kern, grid=(bsz, nt), in_specs=in_specs, out_specs=out_specs, out_shape=out_shape,
        scratch_shapes=scratch, name="prompt_layer",
        compiler_params=pltpu.CompilerParams(
            dimension_semantics=("arbitrary", "arbitrary"), vmem_limit_bytes=VMEM_LIMIT),
    )(x, mem, w["norm_w"], w["w_in"], w["w_out"], w["w_mem_kv"],
      consts["cos_p"], consts["s1_p"], consts["s2_p"],
      consts["dec"], consts["qdec"], consts["kdec"], consts["cdec"],
      w["ret_gn"], w["ml_gn"], w["b_i8"], w["b_f8"],
      w["tab"], w["bt"], w["ct"], w["s5_d"], w["w_glu"], w["final_norm_w"])


def _sample_kernel(x_ref, normw_ref, win_ref, wout_ref, cos_ref, s1_ref, s2_ref, gam_ref, gcol_ref,
                   retgn_ref, mlgn_ref, bi_ref, bf_ref,
                   ab_ref, bt_ref, ct_ref, dsk_ref, wglu_ref, fnw_ref,
                   m0_ref, n0_ref, x0re_ref, x0im_ref,
                   rets_ref, mlc_ref, ck_ref, cv_ref,
                   y_ref, retn_ref, mlcn_ref, mlnn_ref, mlmn_ref, s5re_ref, s5im_ref,
                   hs_ref, proj_ref, qr8_ref, kr8_ref, vrh_ref, qmh_ref, vw8_ref, kmh_ref, a8_ref,
                   qa8_ref, ost_ref, cq_ref, xa_ref, wi_ref, ws_ref, emt_ref,
                   *, n_layers, n_blocks):
    layer = pl.program_id(0)
    g = pl.program_id(1)
    nsamp = x_ref.shape[0]

    @pl.when((layer == 0) & (g == 0))
    def _load_x():
        hs_ref[...] = x_ref[...]
    hm = _head_masks()
    ones_bd = _ones_matrix()
    avg = _avg_matrix()
    mask8 = (lax.broadcasted_iota(jnp.int32, (8, GW), 0)
             == lax.broadcasted_iota(jnp.int32, (8, GW), 1) // DH)
    expand = (lax.broadcasted_iota(jnp.int32, (128, GW), 0)
              == lax.broadcasted_iota(jnp.int32, (128, GW), 1) // DH).astype(BF16)
    fold = (lax.broadcasted_iota(jnp.int32, (GW, DH), 0) % DH
            == lax.broadcasted_iota(jnp.int32, (GW, DH), 1)).astype(BF16)
    tile4 = (lax.broadcasted_iota(jnp.int32, (DH, GW), 0)
             == lax.broadcasted_iota(jnp.int32, (DH, GW), 1) % DH).astype(BF16)

    def rows8(v):
        n = v.shape[0]
        return jnp.where(mask8[None], v[:, None, :], 0.0).reshape(8 * n, GW)

    def P(blk):
        return proj_ref[:, blk * GW:(blk + 1) * GW]

    def gates():
        gt = P(GT)
        ig = gt[:, :128] + bi_ref[...]
        lf = _log_sigmoid(gt[:, 128:] + bf_ref[...])
        a = lf + m0_ref[...]
        mt = jnp.maximum(a, ig)
        return ig, a, mt

    @pl.when(g == 0)
    def _pre():
        hn = _rms_norm(hs_ref[...], normw_ref[...]).astype(BF16)
        proj_ref[...] = _dot(hn, win_ref[...])
        cosr, s1, s2 = cos_ref[...], s1_ref[...], s2_ref[...]
        rq = _rope(P(RQ), cosr, s1, s2)
        rk = _rope(P(RK), cosr, s1, s2)
        qr8_ref[...] = rows8(rq * gam_ref[...])
        kr8_ref[...] = rows8(rk)
        vrh_ref[...] = _dot(rows8(P(RV)).astype(BF16), fold)
        ig, a, mt = gates()
        wi = _dot_x2(jnp.exp(ig - mt), expand)
        ws = _dot_x2(jnp.exp(a - mt), expand)
        wi_ref[...] = wi
        ws_ref[...] = ws
        emt_ref[...] = _dot_x2(jnp.exp(-mt), expand)
        qmh_ref[...] = _dot(rows8(P(MQ)).astype(BF16), fold)
        kmh_ref[...] = _dot(rows8(P(MK)).astype(BF16), fold)
        vw8_ref[...] = rows8(wi * P(MV))
        hi, lo = _split2(ws)
        rid = lax.broadcasted_iota(jnp.int32, (1, 8, GW), 1)
        a8 = jnp.where(rid == 0, hi.astype(F32)[:, None, :],
                       jnp.where(rid == 1, lo.astype(F32)[:, None, :], 0.0))
        a8_ref[...] = a8.reshape(8 * nsamp, GW)
        qa8_ref[...] = rows8(P(AQ))
        mlmn_ref[...] = mt
        su = P(SU)
        bu = _dot(su.astype(BF16), bt_ref[...])
        are, aim = ab_ref[0:1, :], ab_ref[1:2, :]
        x0r, x0i = x0re_ref[...], x0im_ref[...]
        s5re_ref[...] = are * x0r - aim * x0i + bu[:, :NS]
        s5im_ref[...] = are * x0i + aim * x0r + bu[:, NS:]

    ones8 = jnp.ones((8, DH), BF16)

    def sample(i, carry):
        bg = g * SB + i
        r8 = pl.ds(pl.multiple_of(bg * 8, 8), 8)
        s_b = rets_ref[i]
        t_o = _dot(qr8_ref[r8, :].astype(BF16), s_b.astype(BF16))
        o_row = jnp.sum(jnp.where(mask8, _dot_x2(t_o, tile4), 0.0), axis=0, keepdims=True)
        ost_ref[pl.ds(bg, 1), :] = o_row
        upd = _dot_tn(kr8_ref[r8, :].astype(BF16), vrh_ref[r8, :].astype(BF16))
        retn_ref[i] = s_b * gcol_ref[...] + upd
        c_b = mlc_ref[i]
        t_c = _dot_nt(qmh_ref[r8, :].astype(BF16), c_b.astype(BF16))
        cq_ref[pl.ds(bg, 1), :] = jnp.sum(jnp.where(mask8, t_c, 0.0), axis=0, keepdims=True)
        wscol = _dot_tn(a8_ref[r8, :].astype(BF16), ones8)
        updc = _dot_tn(vw8_ref[r8, :].astype(BF16), kmh_ref[r8, :].astype(BF16))
        mlcn_ref[i] = c_b * wscol + updc
        k_b = ck_ref[i].astype(BF16)
        v_b = cv_ref[i].astype(BF16)
        sc = _dot(qa8_ref[r8, :].astype(BF16), k_b)
        e = jnp.exp(sc - jnp.max(sc, axis=-1, keepdims=True))
        p = e / jnp.sum(e, axis=-1, keepdims=True)
        t_a = _dot_nt(p.astype(BF16), v_b)
        xa_ref[pl.ds(bg, 1), :] = jnp.sum(jnp.where(mask8, t_a, 0.0), axis=0, keepdims=True)
        return carry

    lax.fori_loop(0, SB, sample, 0)

    @pl.when(g == n_blocks - 1)
    def _post():
        cosr, s1, s2 = cos_ref[...], s1_ref[...], s2_ref[...]
        rq = _rope(P(RQ), cosr, s1, s2)
        rk = _rope(P(RK), cosr, s1, s2)
        ro = _dot_x2(rq * rk, ones_bd) * P(RV) + ost_ref[...]
        ret_out = _head_norm(ro, retgn_ref[...], avg) * _silu(P(RG))
        mq, mk_, mv_ = P(MQ), P(MK), P(MV)
        wi, ws, emt = wi_ref[...], ws_ref[...], emt_ref[...]
        n0 = n0_ref[...]
        s = _dot_x2(mq * mk_, ones_bd) * wi
        num = s * mv_ + ws * cq_ref[...]
        den = s + ws * _dot_x2(n0 * mq, ones_bd)
        mh = num / jnp.maximum(jnp.abs(den), emt)
        mlnn_ref[...] = ws * n0 + wi * mk_
        mh = mh * _sigmoid(P(MO))
        ml_out = _head_norm(mh, mlgn_ref[...], avg) * _silu(P(MG))
        su = P(SU)
        xcat = jnp.concatenate([s5re_ref[...], s5im_ref[...]], axis=1).astype(BF16)
        sy = _dot(xcat, ct_ref[...]) + dsk_ref[...] * su
        sy = _gelu_tanh(sy)
        sy = sy * _sigmoid(_dot(sy.astype(BF16), wglu_ref[...]))
        s5_out = sy * _silu(P(SG))
        xa_out = xa_ref[...] * _silu(P(AG))
        mix = jnp.concatenate([ret_out, ml_out, s5_out, xa_out], axis=1).astype(BF16)
        y = hs_ref[...] + _dot(mix, wout_ref[...])
        hs_ref[...] = y

        @pl.when(layer == n_layers - 1)
        def _emit():
            y_ref[...] = _rms_norm(y, fnw_ref[...])


def _sample_layers(x, st, w, consts):
    nsamp = x.shape[0]
    depth = w["w_in"].shape[0]
    nb = nsamp // SB
    once = lambda shape: pl.BlockSpec(shape, lambda l, g: (0,) * len(shape),
                                      pipeline_mode=pl.Buffered(1))
    lyr = lambda shape: pl.BlockSpec((None,) + shape, lambda l, g: (l,) + (0,) * len(shape),
                                     pipeline_mode=pl.Buffered(1))
    lyr_out = lambda shape: pl.BlockSpec((None,) + shape, lambda l, g: (l,) + (0,) * len(shape))
    blk = lambda r, c: pl.BlockSpec((None, SB, r, c), lambda l, g: (l, g, 0, 0))
    in_specs = [
        once((nsamp, D)), lyr((1, D)), lyr((D, DP)), lyr((D, D)),
        once((1, 128)), once((1, 128)), once((1, 128)), once((1, GW)), once((GW, DH)),
        lyr((1, GW)), lyr((1, GW)), lyr((1, 128)), lyr((1, 128)),
        lyr((2, NS)), lyr((GW, 2 * NS)), lyr((2 * NS, GW)),
        lyr((1, GW)), lyr((GW, GW)), once((1, D)),
        lyr((nsamp, 128)), lyr((nsamp, GW)), lyr((nsamp, NS)), lyr((nsamp, NS)),
        blk(GW, DH), blk(GW, DH), blk(GW, MEM), blk(GW, MEM),
    ]
    out_shape = (
        jax.ShapeDtypeStruct((nsamp, D), F32),
        jax.ShapeDtypeStruct((depth, nsamp, GW, DH), F32),
        jax.ShapeDtypeStruct((depth, nsamp, GW, DH), F32),
        jax.ShapeDtypeStruct((depth, nsamp, GW), F32),
        jax.ShapeDtypeStruct((depth, nsamp, 128), F32),
        jax.ShapeDtypeStruct((depth, nsamp, NS), F32),
        jax.ShapeDtypeStruct((depth, nsamp, NS), F32),
    )
    out_specs = (
        pl.BlockSpec((nsamp, D), lambda l, g: (0, 0)), blk(GW, DH), blk(GW, DH),
        lyr_out((nsamp, GW)), lyr_out((nsamp, 128)), lyr_out((nsamp, NS)), lyr_out((nsamp, NS)),
    )
    r8 = 8 * nsamp
    scratch = [
        pltpu.VMEM((nsamp, D), F32),
        pltpu.VMEM((nsamp, DP), F32),
        pltpu.VMEM((r8, GW), F32), pltpu.VMEM((r8, GW), F32), pltpu.VMEM((r8, DH), F32),
        pltpu.VMEM((r8, DH), F32), pltpu.VMEM((r8, GW), F32), pltpu.VMEM((r8, DH), F32),
        pltpu.VMEM((r8, GW), F32), pltpu.VMEM((r8, GW), F32),
        pltpu.VMEM((nsamp, GW), F32), pltpu.VMEM((nsamp, GW), F32), pltpu.VMEM((nsamp, GW), F32),
        pltpu.VMEM((nsamp, GW), F32), pltpu.VMEM((nsamp, GW), F32), pltpu.VMEM((nsamp, GW), F32),
    ]
    kern = functools.partial(_sample_kernel, n_layers=depth, n_blocks=nb)
    return pl.pallas_call(
        kern, grid=(depth, nb), in_specs=in_specs, out_specs=out_specs, out_shape=out_shape,
        scratch_shapes=scratch, name="sample_layers",
        compiler_params=pltpu.CompilerParams(
            dimension_semantics=("arbitrary", "arbitrary"), vmem_limit_bytes=VMEM_LIMIT),
    )(x, w["norm_w"], w["w_in"], w["w_out"],
      consts["cos_s"], consts["s1_s"], consts["s2_s"], consts["gam"], consts["gcol"],
      w["ret_gn"], w["ml_gn"], w["b_i"], w["b_f"],
      w["ab"], w["bt"], w["ct"], w["s5_d"], w["w_glu"], w["final_norm_w"],
      st["m"], st["n"], st["s5_re"], st["s5_im"], st["ret"], st["c"], st["mem_k"], st["mem_v"])


def _rope_tables(pos):
    half = DH // 2
    inv = ROPE_BASE ** (-jnp.arange(half, dtype=F32) / half)
    ang = pos.astype(F32)[:, None] * inv[None, :]
    cos, sin = jnp.cos(ang), jnp.sin(ang)
    zero = jnp.zeros_like(sin)
    c = jnp.tile(jnp.concatenate([cos, cos], axis=-1), (1, 2))
    s1 = jnp.tile(jnp.concatenate([zero, sin], axis=-1), (1, 2))
    s2 = jnp.tile(jnp.concatenate([-sin, zero], axis=-1), (1, 2))
    return c, s1, s2


def _constants(seq):
    lg = jnp.log1p(-jnp.power(2.0, -5.0 - jnp.arange(NH, dtype=F32)))[:, None]
    idx = jnp.arange(L, dtype=F32)
    diff = idx[:, None] - idx[None, :]
    decay = jnp.where(diff >= 0, jnp.exp(lg[:, :, None] * jnp.maximum(diff, 0.0)), 0.0)
    rep = lambda t: jnp.repeat(t, DH, axis=0).T
    consts = {
        "dec": decay.reshape(NH * L, L),
        "qdec": rep(jnp.exp(lg * (idx + 1.0))),
        "kdec": rep(jnp.exp(lg * (L - 1.0 - idx))),
        "cdec": rep(jnp.exp(lg * L)),
        "gam": rep(jnp.exp(lg * 1.0)),
    }
    consts["gcol"] = jnp.broadcast_to(consts["gam"].T, (GW, DH))
    consts["cos_p"], consts["s1_p"], consts["s2_p"] = _rope_tables(jnp.arange(seq, dtype=jnp.int32))
    consts["cos_s"], consts["s1_s"], consts["s2_s"] = _rope_tables(
        PAST_LEN + jnp.arange(1, dtype=jnp.int32))
    return consts


def _pack_w_in(w_in):
    sizes = (GW,) * 9 + (NH, NH) + (GW,) * 4
    offs = np.concatenate([[0], np.cumsum(sizes)])
    seg = [w_in[:, :, int(offs[i]):int(offs[i + 1])] for i in range(len(sizes))]
    scale = DH ** -0.5
    pad = lambda t: jnp.pad(t, ((0, 0), (0, 0), (0, 128 - t.shape[-1])))
    blocks = [seg[0], seg[1] * scale, seg[2], seg[3],
              seg[4], seg[5] * scale, seg[6], seg[7], seg[8],
              seg[11], seg[12], seg[13] * scale, seg[14],
              pad(seg[9]), pad(seg[10])]
    return jnp.concatenate(blocks, axis=-1).astype(BF16)


def kernel(x_prompt, x_sample, mem_prompt, state_ret, state_mlstm_c, state_mlstm_n, state_mlstm_m,
           state_s5_re, state_s5_im, cache_mem_k, cache_mem_v,
           norm_w, w_in, ret_gn, ml_b_i, ml_b_f, ml_gn,
           s5_a_re, s5_a_im, s5_log_dt, s5_b_re, s5_b_im, s5_c_re, s5_c_im, s5_d, s5_w_glu,
           w_mem_k, w_mem_v, w_out, final_norm_w):
    depth = norm_w.shape[0]
    bp, seq, _ = x_prompt.shape
    bs = x_sample.shape[0]
    consts = _constants(seq)
    ab, bt, ct, tab = _s5_prepare(s5_a_re, s5_a_im, s5_log_dt, s5_b_re, s5_b_im, s5_c_re, s5_c_im)
    pad_lanes = lambda t: jnp.pad(t, ((0, 0),) * (t.ndim - 1) + ((0, 128 - t.shape[-1]),))
    rows8 = lambda t: jnp.pad(jnp.broadcast_to(t[:, :, None], (depth, NH, TB)),
                              ((0, 0), (0, 8 - NH), (0, 0)))
    w = {
        "norm_w": norm_w[:, None], "w_in": _pack_w_in(w_in), "w_out": w_out.astype(BF16),
        "w_mem_kv": jnp.concatenate([w_mem_k, w_mem_v], axis=-1).astype(BF16),
        "ret_gn": ret_gn[:, None], "ml_gn": ml_gn[:, None],
        "b_i": pad_lanes(ml_b_i)[:, None], "b_f": pad_lanes(ml_b_f)[:, None],
        "b_i8": rows8(ml_b_i), "b_f8": rows8(ml_b_f),
        "ab": ab, "tab": tab, "bt": bt, "ct": ct,
        "s5_d": s5_d[:, None], "w_glu": s5_w_glu.astype(BF16),
        "final_norm_w": final_norm_w[None],
    }

    hp = x_prompt
    outs_p = [[] for _ in range(8)]
    for l in range(depth):
        hp, *sp = _prompt_layer(l, hp, mem_prompt, w, consts, l == depth - 1)
        for i in range(8):
            outs_p[i].append(sp[i])
    rs, cs, ns, ms, xr, xi, mk, mv = (jnp.stack(o) for o in outs_p)
    diag = lambda t: jnp.stack([t[:, :, h * DH:(h + 1) * DH, h * DH:(h + 1) * DH]
                                for h in range(NH)], axis=2)
    ret_p = diag(rs)
    mlc_p = jnp.swapaxes(diag(cs), -1, -2)
    mln_p = jnp.stack([ns[:, :, h * DH:(h + 1) * DH, h] for h in range(NH)], axis=2)
    mlm_p = ms[:, :, :NH, 0]
    s5re_p = xr[:, :, 0].reshape(depth, bp, S5G, S5P)
    s5im_p = xi[:, :, 0].reshape(depth, bp, S5G, S5P)
    memk_p = mk.reshape(depth, bp, MEM, NH, DH)
    memv_p = mv.reshape(depth, bp, MEM, NH, DH)

    st = {
        "m": pad_lanes(state_mlstm_m), "n": state_mlstm_n.reshape(depth, bs, GW),
        "s5_re": state_s5_re.reshape(depth, bs, NS), "s5_im": state_s5_im.reshape(depth, bs, NS),
        "ret": state_ret.reshape(depth, bs, GW, DH), "c": state_mlstm_c.reshape(depth, bs, GW, DH),
        "mem_k": jnp.transpose(cache_mem_k, (0, 1, 3, 4, 2)).reshape(depth, bs, GW, MEM),
        "mem_v": jnp.transpose(cache_mem_v, (0, 1, 3, 4, 2)).reshape(depth, bs, GW, MEM),
    }
    hs, rn, cn, nn, mn, sr, si = _sample_layers(x_sample.reshape(bs, D), st, w, consts)
    return (hp, hs.reshape(bs, 1, D),
            ret_p, rn.reshape(depth, bs, NH, DH, DH), mlc_p, cn.reshape(depth, bs, NH, DH, DH),
            mln_p, nn.reshape(depth, bs, NH, DH), mlm_p, mn[:, :, :NH],
            s5re_p, sr.reshape(depth, bs, S5G, S5P), s5im_p, si.reshape(depth, bs, S5G, S5P),
            memk_p, memv_p)
```

```python
import functools
import math

import numpy as np
import jax
import jax.numpy as jnp
from jax import lax
from jax.experimental import pallas as pl
from jax.experimental.pallas import tpu as pltpu

F32 = jnp.float32
BF16 = jnp.bfloat16

D = 1024
GW = 256
NH = 4
DH = 64
L = 128
LS = 8
S5G = 16
S5P = 64
S5C = 16
NS = S5G * S5P
MEM = 256
EPS = 1e-6
NEG_INF = -1e30
ROPE_BASE = 10000.0
PAST_LEN = 16384

TB = 512
SB = 8
XQ = 256
NBLK = 14
DP = NBLK * GW
(RQ, RK, RV, RG, MQ, MK, MV, MO, MG, SU, SG, AQ, AG, GT) = range(NBLK)

VMEM_LIMIT = 56 * 1024 * 1024


_DONE = object()


def _dot(a, b):
    return jnp.dot(a, b, preferred_element_type=F32)


def _dot_nt(a, b):
    return lax.dot_general(a, b, (((1,), (1,)), ((), ())), preferred_element_type=F32)


def _dot_tn(a, b):
    return lax.dot_general(a, b, (((0,), (0,)), ((), ())), preferred_element_type=F32)


def _split2(x):
    hi = x.astype(BF16)
    lo = (x - hi.astype(F32)).astype(BF16)
    return hi, lo


def _dot_x2(x, w):
    hi, lo = _split2(x)
    return _dot(hi, w) + _dot(lo, w)


def _dot_x3(x, w):
    hi = x.astype(BF16)
    r = x - hi.astype(F32)
    mid = r.astype(BF16)
    lo = (r - mid.astype(F32)).astype(BF16)
    return _dot(hi, w) + _dot(mid, w) + _dot(lo, w)


def _sigmoid(x):
    return 1.0 / (1.0 + jnp.exp(-x))


def _silu(x):
    return x * _sigmoid(x)


def _log_sigmoid(x):
    return jnp.minimum(x, 0.0) - jnp.log1p(jnp.exp(-jnp.abs(x)))


def _gelu_tanh(x):
    c = math.sqrt(2.0 / math.pi)
    return x * (0.5 * (1.0 + jnp.tanh(c * (x + 0.044715 * (x * x * x)))))


def _lane_head(n):
    return lax.broadcasted_iota(jnp.int32, (1, n), 1) // DH


def _head_masks():
    lh = _lane_head(GW)
    return [lh == h for h in range(NH)]


def _block_diag_mask():
    r = lax.broadcasted_iota(jnp.int32, (GW, GW), 0) // DH
    c = lax.broadcasted_iota(jnp.int32, (GW, GW), 1) // DH
    return r == c


def _avg_matrix():
    return jnp.where(_block_diag_mask(), 1.0 / DH, 0.0).astype(BF16)


def _ones_matrix():
    return jnp.where(_block_diag_mask(), 1.0, 0.0).astype(BF16)


def _rope(x, cos, s1, s2):
    outs = []
    for j in range(2):
        xs = x[:, j * 128:(j + 1) * 128]
        outs.append(xs * cos + pltpu.roll(xs, 32, 1) * s1 + pltpu.roll(xs, 96, 1) * s2)
    return jnp.concatenate(outs, axis=1)


def _head_norm(x, gain, avg):
    mu = _dot_x2(x, avg)
    d = x - mu
    var = _dot_x2(d * d, avg)
    return d * lax.rsqrt(var + EPS) * gain


def _rms_norm(x, w):
    ms = jnp.mean(x * x, axis=-1, keepdims=True)
    return x * lax.rsqrt(ms + EPS) * w


def _stack_heads(x, hm):
    return jnp.concatenate([jnp.where(hm[h], x, 0.0) for h in range(NH)], axis=0)


def _fold_heads(r, hm, rows):
    out = jnp.where(hm[0], r[0:rows], 0.0)
    for h in range(1, NH):
        out = out + jnp.where(hm[h], r[h * rows:(h + 1) * rows], 0.0)
    return out


def _s5_prep_kernel(are_ref, aim_ref, ldt_ref, bre_ref, bim_ref, cre_ref, cim_ref,
                    ab_ref, bt_ref, ct_ref, tab_ref):
    a_re = are_ref[0]
    a_im = aim_ref[0]
    dt = jnp.exp(ldt_ref[0])
    lam_re = a_re * dt
    lam_im = a_im * dt
    mag = jnp.exp(lam_re)
    ab_re = mag * jnp.cos(lam_im)
    ab_im = mag * jnp.sin(lam_im)
    den = a_re * a_re + a_im * a_im
    nr = ab_re - 1.0
    ni = ab_im
    f_re = (nr * a_re + ni * a_im) / den
    f_im = (ni * a_re - nr * a_im) / den
    ab_ref[0, 0:1, :] = ab_re
    ab_ref[0, 1:2, :] = ab_im
    b_re = bre_ref[0]
    b_im = bim_ref[0]
    bb_re = (f_re * b_re - f_im * b_im).astype(BF16)
    bb_im = (f_re * b_im + f_im * b_re).astype(BF16)
    rep_r = (lax.broadcasted_iota(jnp.int32, (GW, S5C), 0) % S5C
             == lax.broadcasted_iota(jnp.int32, (GW, S5C), 1)).astype(BF16)
    in_blk = (lax.broadcasted_iota(jnp.int32, (GW, NS), 0) // S5C
              == lax.broadcasted_iota(jnp.int32, (GW, NS), 1) // S5P)
    bt_ref[0, :, :NS] = jnp.where(in_blk, _dot(rep_r, bb_re), 0.0).astype(BF16)
    bt_ref[0, :, NS:] = jnp.where(in_blk, _dot(rep_r, bb_im), 0.0).astype(BF16)
    rep_c = (lax.broadcasted_iota(jnp.int32, (S5C, GW), 0)
             == lax.broadcasted_iota(jnp.int32, (S5C, GW), 1) % S5C).astype(BF16)
    out_blk = (lax.broadcasted_iota(jnp.int32, (NS, GW), 0) // S5P
               == lax.broadcasted_iota(jnp.int32, (NS, GW), 1) // S5C)
    ct_ref[0, :NS, :] = jnp.where(out_blk, _dot(cre_ref[0].astype(BF16), rep_c), 0.0).astype(BF16)
    ct_ref[0, NS:, :] = jnp.where(out_blk, -_dot(cim_ref[0].astype(BF16), rep_c), 0.0).astype(BF16)
    i = (lax.broadcasted_iota(jnp.int32, (L, NS), 0) % LS).astype(F32)
    for slot, k in ((0, -i), (2, i), (4, i + 1.0)):
        pmag = jnp.exp(k * lam_re)
        tab_ref[0, slot] = pmag * jnp.cos(k * lam_im)
        tab_ref[0, slot + 1] = pmag * jnp.sin(k * lam_im)


def _s5_prepare(a_re, a_im, log_dt, b_re, b_im, c_re, c_im):
    depth = a_re.shape[0]
    are = a_re.reshape(depth, 1, NS)
    aim = a_im.reshape(depth, 1, NS)
    ldt = jnp.repeat(log_dt, S5P, axis=-1).reshape(depth, 1, NS)
    bre = jnp.transpose(b_re, (0, 3, 1, 2)).reshape(depth, S5C, NS)
    bim = jnp.transpose(b_im, (0, 3, 1, 2)).reshape(depth, S5C, NS)
    cre = jnp.transpose(c_re, (0, 1, 3, 2)).reshape(depth, NS, S5C)
    cim = jnp.transpose(c_im, (0, 1, 3, 2)).reshape(depth, NS, S5C)
    per_layer = lambda *shape: pl.BlockSpec((1,) + shape, lambda l: (l,) + (0,) * len(shape))
    out_shape = (jax.ShapeDtypeStruct((depth, 2, NS), F32),
                 jax.ShapeDtypeStruct((depth, GW, 2 * NS), BF16),
                 jax.ShapeDtypeStruct((depth, 2 * NS, GW), BF16),
                 jax.ShapeDtypeStruct((depth, 6, L, NS), F32))
    return pl.pallas_call(
        _s5_prep_kernel, grid=(depth,),
        in_specs=[per_layer(1, NS), per_layer(1, NS), per_layer(1, NS),
                  per_layer(S5C, NS), per_layer(S5C, NS), per_layer(NS, S5C), per_layer(NS, S5C)],
        out_specs=(per_layer(2, NS), per_layer(GW, 2 * NS), per_layer(2 * NS, GW),
                   per_layer(6, L, NS)),
        out_shape=out_shape, name="s5_prepare",
        compiler_params=pltpu.CompilerParams(dimension_semantics=("arbitrary",)),
    )(are, aim, ldt, bre, bim, cre, cim)


def _prompt_kernel(x_ref, mem_ref, normw_ref, win_ref, wout_ref, wmkv_ref,
                   cos_ref, s1_ref, s2_ref, dec_ref, qdec_ref, kdec_ref, cdec_ref,
                   retgn_ref, mlgn_ref, bi_ref, bf_ref,
                   tab_ref, bt_ref, ct_ref, dsk_ref, wglu_ref, fnw_ref,
                   y_ref, rets_ref, mlc_ref, mln_ref, mlm_ref, s5re_ref, s5im_ref,
                   memk_ref, memv_ref,
                   proj_ref, mix_ref, s_ref, c_ref, n_ref, m_ref, xre_ref, xim_ref,
                   mk_ref, mv_ref, bu_ref, xcat_ref, car_ref, *, last_layer, n_tblocks):
    t = pl.program_id(1)
    hm = _head_masks()
    bd = _block_diag_mask()
    avg = _avg_matrix()
    lane128 = lax.broadcasted_iota(jnp.int32, (1, 128), 1)
    row_i = lax.broadcasted_iota(jnp.int32, (L, 128), 0)
    col_i = lax.broadcasted_iota(jnp.int32, (L, 128), 1)
    causal = row_i >= col_i
    tri_u = jnp.where(row_i <= col_i, 1.0, 0.0).astype(BF16)
    tri_sub = jnp.where(causal & (row_i // LS == col_i // LS), 1.0, 0.0)
    blk_sum = jnp.where(lax.broadcasted_iota(jnp.int32, (L // LS, 128), 0)
                        == lax.broadcasted_iota(jnp.int32, (L // LS, 128), 1) // LS, 1.0, 0.0)
    tri_ext = jnp.concatenate([tri_sub, blk_sum], axis=0).astype(BF16)

    @pl.when(t == 0)
    def _init():
        s_ref[...] = jnp.zeros_like(s_ref)
        c_ref[...] = jnp.zeros_like(c_ref)
        n_ref[...] = jnp.zeros_like(n_ref)
        m_ref[...] = jnp.zeros_like(m_ref)
        xre_ref[...] = jnp.zeros_like(xre_ref)
        xim_ref[...] = jnp.zeros_like(xim_ref)
        mkv = _dot(mem_ref[0].astype(BF16), wmkv_ref[...])
        mk = mkv[:, :GW]
        mv = mkv[:, GW:]
        memk_ref[0] = mk
        memv_ref[0] = mv
        mk_ref[...] = mk.astype(BF16)
        mv_ref[...] = mv.astype(BF16)

    x = x_ref[0]
    hn = _rms_norm(x, normw_ref[...]).astype(BF16)
    proj_ref[...] = _dot(hn, win_ref[...])

    nc = TB // L
    crow = [slice(c * L, (c + 1) * L) for c in range(nc)]

    def PB(blk, rows=slice(None)):
        return proj_ref[rows, blk * GW:(blk + 1) * GW]


    def xattn_stages():
        for piece in range(TB // XQ):
            rs = slice(piece * XQ, (piece + 1) * XQ)
            sc = _dot_nt(_stack_heads(PB(AQ, rs), hm).astype(BF16), mk_ref[...])
            yield
            e = jnp.exp(sc - jnp.max(sc, axis=-1, keepdims=True))
            p = (e / jnp.sum(e, axis=-1, keepdims=True)).astype(BF16)
            yield
            r = _dot(p, mv_ref[...])
            yield
            mix_ref[rs, 3 * GW:4 * GW] = (_fold_heads(r, hm, XQ) * _silu(PB(AG, rs))).astype(BF16)
            yield

    def s5_stages():
        lanes = [(slice(j * 128, (j + 1) * 128), slice(NS + j * 128, NS + (j + 1) * 128))
                 for j in range(NS // 128)]
        bpc = L // LS
        nblk = TB // LS
        su = PB(SU)
        bu_ref[...] = _dot(su.astype(BF16), bt_ref[...])
        yield
        for c in range(nc):
            for lre, lim in lanes:
                br = bu_ref[crow[c], lre]
                bi = bu_ref[crow[c], lim]
                wr = tab_ref[0, :, lre]
                wi = tab_ref[1, :, lre]
                xcat_ref[crow[c], lre] = (wr * br - wi * bi).astype(BF16)
                xcat_ref[crow[c], lim] = (wr * bi + wi * br).astype(BF16)
            yield
        for c in range(nc):
            z = _dot(tri_ext, xcat_ref[crow[c], :])
            bu_ref[crow[c], :] = z[0:L]
            car_ref[c * bpc:(c + 1) * bpc, :] = z[L:L + bpc]
            yield
        rowb = lax.broadcasted_iota(jnp.int32, (nblk, 128), 0)
        ers, eis, prs, pis, c0s = [], [], [], [], []
        for lre, lim in lanes:
            zr = car_ref[:, lre]
            zi = car_ref[:, lim]
            pr = tab_ref[2, LS - 1:LS, lre]
            pi = tab_ref[3, LS - 1:LS, lre]
            er = pr * zr - pi * zi
            ei = pr * zi + pi * zr
            pr = tab_ref[4, LS - 1:LS, lre]
            pi = tab_ref[5, LS - 1:LS, lre]
            c0r = xre_ref[0:1, lre]
            c0i = xim_ref[0:1, lre]
            ers.append(er + jnp.where(rowb == 0, pr * c0r - pi * c0i, 0.0))
            eis.append(ei + jnp.where(rowb == 0, pr * c0i + pi * c0r, 0.0))
            prs.append(pr)
            pis.append(pi)
            c0s.append((c0r, c0i))
        yield
        for k in range(nblk.bit_length() - 1):
            d = 1 << k
            for j in range(len(lanes)):
                er, ei, pr, pi = ers[j], eis[j], prs[j], pis[j]
                sr = jnp.where(rowb >= d, pltpu.roll(er, d, 0), 0.0)
                si = jnp.where(rowb >= d, pltpu.roll(ei, d, 0), 0.0)
                ers[j], eis[j] = er + pr * sr - pi * si, ei + pr * si + pi * sr
                prs[j], pis[j] = pr * pr - pi * pi, 2.0 * (pr * pi)
            yield
        for j, (lre, lim) in enumerate(lanes):
            xre_ref[:, lre] = jnp.broadcast_to(ers[j][nblk - 1:nblk, :], (8, 128))
            xim_ref[:, lre] = jnp.broadcast_to(eis[j][nblk - 1:nblk, :], (8, 128))
            cr = jnp.where(rowb == 0, c0s[j][0], pltpu.roll(ers[j], 1, 0))
            ci_ = jnp.where(rowb == 0, c0s[j][1], pltpu.roll(eis[j], 1, 0))
            ar = tab_ref[4, 0:1, lre]
            ai = tab_ref[5, 0:1, lre]
            car_ref[:, lre] = ar * cr - ai * ci_
            car_ref[:, lim] = ar * ci_ + ai * cr
        yield
        for c in range(nc):
            for lre, lim in lanes:
                cr = jnp.broadcast_to(car_ref[c * bpc:(c + 1) * bpc, lre][:, None, :],
                                      (bpc, LS, 128)).reshape(L, 128)
                ci_ = jnp.broadcast_to(car_ref[c * bpc:(c + 1) * bpc, lim][:, None, :],
                                       (bpc, LS, 128)).reshape(L, 128)
                zr = bu_ref[crow[c], lre] + cr
                zi = bu_ref[crow[c], lim] + ci_
                pr = tab_ref[2, :, lre]
                pi = tab_ref[3, :, lre]
                xcat_ref[crow[c], lre] = (pr * zr - pi * zi).astype(BF16)
                xcat_ref[crow[c], lim] = (pr * zi + pi * zr).astype(BF16)
            yield
        sy = _dot(xcat_ref[...], ct_ref[...]) + dsk_ref[...] * su
        yield
        sy = _gelu_tanh(sy)
        gate = _dot(sy.astype(BF16), wglu_ref[...])
        yield
        mix_ref[:, 2 * GW:3 * GW] = (sy * _sigmoid(gate) * _silu(PB(SG))).astype(BF16)
        yield

    def retention_stages():
        rq = _rope(PB(RQ), cos_ref[...], s1_ref[...], s2_ref[...])
        rk = _rope(PB(RK), cos_ref[...], s1_ref[...], s2_ref[...])
        rv = PB(RV).astype(BF16)
        rkb = rk.astype(BF16)
        qst = [_stack_heads(rq[crow[c]], hm).astype(BF16) for c in range(nc)]
        yield
        inner = [_dot_nt(qst[c], rkb[crow[c]]) for c in range(nc)]
        yield
        pmat = [(inner[c] * dec_ref[...]).astype(BF16) for c in range(nc)]
        kd = [(rk[crow[c]] * kdec_ref[...]).astype(BF16) for c in range(nc)]
        yield
        rloc = [_dot(pmat[c], rv[crow[c]]) for c in range(nc)]
        upd = [_dot_tn(kd[c], rv[crow[c]]) for c in range(nc)]
        yield
        st = [s_ref[...]]
        for c in range(nc):
            st.append(st[c] * cdec_ref[...] + jnp.where(bd, upd[c], 0.0))
        s_ref[...] = st[nc]
        qd = [(rq[crow[c]] * qdec_ref[...]).astype(BF16) for c in range(nc)]
        yield
        ost = [_dot(qd[c], st[c].astype(BF16)) for c in range(nc)]
        yield
        ro = jnp.concatenate([_fold_heads(rloc[c], hm, L) + ost[c] for c in range(nc)], axis=0)
        mix_ref[:, 0:GW] = (_head_norm(ro, retgn_ref[...], avg) * _silu(PB(RG))).astype(BF16)
        yield

    def mlstm_stages():
        g = PB(GT)
        ig_r = jnp.concatenate([g[crow[c], 0:128].T[0:8] for c in range(nc)], axis=1) + bi_ref[...]
        lf_r = _log_sigmoid(
            jnp.concatenate([g[crow[c], 128:256].T[0:8] for c in range(nc)], axis=1) + bf_ref[...])
        yield
        b_r = [_dot_x3(lf_r[:, crow[c]], tri_u) for c in range(nc)]
        yield
        gd_r = [ig_r[:, crow[c]] - b_r[c] for c in range(nc)]
        lane_r = lax.broadcasted_iota(jnp.int32, (8, L), 1)
        cm_r = list(gd_r)
        for k in range(7):
            d = 1 << k
            cm_r = [jnp.maximum(v, jnp.where(lane_r >= d, pltpu.roll(v, d, 1), NEG_INF))
                    for v in cm_r]
            yield
        m_prev = [m_ref[...]]
        mt_r = []
        for c in range(nc):
            mt = jnp.maximum(b_r[c] + m_prev[c], b_r[c] + cm_r[c])
            mt_r.append(mt)
            m_prev.append(jnp.broadcast_to(mt[:, L - 1:L], (8, L)))
            yield
        m_ref[...] = m_prev[nc]
        cols = []
        for c in range(nc):
            bm = b_r[c] - mt_r[c]
            ws = jnp.exp(b_r[c] + m_prev[c] - mt_r[c])
            wl = jnp.exp(gd_r[c] + jnp.broadcast_to(bm[:, L - 1:L], (8, L)))
            emt = jnp.exp(-mt_r[c])
            cols.append(jnp.concatenate([bm, ws, wl, emt, jnp.zeros((L - 32, L), F32)], axis=0).T)
        mqf = PB(MQ)
        mq = mqf.astype(BF16)
        mkf = PB(MK)
        mkb = mkf.astype(BF16)
        mv_ = PB(MV).astype(BF16)
        qst = [_stack_heads(mqf[crow[c]], hm).astype(BF16) for c in range(nc)]
        yield
        sraw = [_dot_nt(qst[c], mkb[crow[c]]) for c in range(nc)]
        yield
        smat, den_i = [], []
        for c in range(nc):
            parts = []
            for h in range(NH):
                arg = jnp.where(causal, cols[c][:, h:h + 1] + gd_r[c][h:h + 1, :], NEG_INF)
                parts.append(sraw[c][h * L:(h + 1) * L] * jnp.exp(arg))
            s_c = jnp.concatenate(parts, axis=0)
            den_i.append(jnp.sum(s_c, axis=-1, keepdims=True))
            smat.append(s_c.astype(BF16))
            yield
        ones_blk = jnp.ones((L, 128), BF16)
        kws = []
        for c in range(nc):
            kw = jnp.zeros((L, GW), F32)
            for h in range(NH):
                kw = kw + jnp.where(hm[h], mkf[crow[c]] * cols[c][:, 16 + h:17 + h], 0.0)
            kws.append(kw.astype(BF16))
        yield
        rloc = [_dot(smat[c], mv_[crow[c]]) for c in range(nc)]
        u = [_dot_tn(kws[c], jnp.concatenate([mv_[crow[c]], ones_blk], axis=1))
             for c in range(nc)]
        yield
        nmask = (lax.broadcasted_iota(jnp.int32, (GW, 128), 0) // DH
                 == lax.broadcasted_iota(jnp.int32, (GW, 128), 1))
        cst = [c_ref[...]]
        nst = [n_ref[...]]
        for c in range(nc):
            wsl256 = jnp.zeros((1, GW), F32)
            wsl128 = jnp.zeros((1, 128), F32)
            for h in range(NH):
                wsl = cols[c][L - 1:L, 8 + h:9 + h]
                wsl256 = wsl256 + jnp.where(hm[h], wsl, 0.0)
                wsl128 = wsl128 + jnp.where(lane128 == h, wsl, 0.0)
            cst.append(cst[c] * wsl256 + jnp.where(bd, u[c][:, :GW], 0.0))
            nst.append(nst[c] * wsl128 + jnp.where(nmask, u[c][:, GW:], 0.0))
        c_ref[...] = cst[nc]
        n_ref[...] = nst[nc]
        yield
        qc = [_dot(mq[crow[c]], cst[c].astype(BF16)) for c in range(nc)]
        qn = [_dot(mq[crow[c]], nst[c].astype(BF16)) for c in range(nc)]
        yield
        mhs = []
        for c in range(nc):
            mh = jnp.zeros((L, GW), F32)
            for h in range(NH):
                ws = cols[c][:, 8 + h:9 + h]
                den = den_i[c][h * L:(h + 1) * L] + ws * qn[c][:, h:h + 1]
                dn = jnp.maximum(jnp.abs(den), cols[c][:, 24 + h:25 + h])
                mh = mh + jnp.where(hm[h], (rloc[c][h * L:(h + 1) * L] + ws * qc[c]) / dn, 0.0)
            mhs.append(mh)
            yield
        mh = jnp.concatenate(mhs, axis=0) * _sigmoid(PB(MO))
        mix_ref[:, GW:2 * GW] = (_head_norm(mh, mlgn_ref[...], avg) * _silu(PB(MG))).astype(BF16)
        yield

    pending = [mlstm_stages(), s5_stages(), retention_stages(), xattn_stages()]
    while pending:
        pending = [stage for stage in pending if next(stage, _DONE) is not _DONE]

    y = x + _dot(mix_ref[...], wout_ref[...])
    if last_layer:
        y = _rms_norm(y, fnw_ref[...])
    y_ref[0] = y

    @pl.when(t == n_tblocks - 1)
    def _final():
        rets_ref[0] = s_ref[...]
        mlc_ref[0] = c_ref[...]
        mln_ref[0] = n_ref[...]
        mlm_ref[0] = m_ref[...]
        s5re_ref[0] = xre_ref[...]
        s5im_ref[0] = xim_ref[...]


def _prompt_layer(layer, x, mem, w, consts, last_layer):
    bsz, seq, _ = x.shape
    nt = seq // TB
    full = lambda shape: pl.BlockSpec(shape, lambda b, t: (0,) * len(shape),
                                      pipeline_mode=pl.Buffered(1))
    lyr = lambda shape: pl.BlockSpec((None,) + shape, lambda b, t: (layer,) + (0,) * len(shape),
                                     pipeline_mode=pl.Buffered(1))
    tok = lambda width: pl.BlockSpec((TB, width), lambda b, t: (t, 0))
    per_b = lambda r, c: pl.BlockSpec((1, r, c), lambda b, t: (b, 0, 0))
    in_specs = [
        pl.BlockSpec((1, TB, D), lambda b, t: (b, t, 0)),
        per_b(MEM, D),
        lyr((1, D)), lyr((D, DP)), lyr((D, D)), lyr((D, 2 * GW)),
        tok(128), tok(128), tok(128),
        full((NH * L, L)), full((L, GW)), full((L, GW)), full((1, GW)),
        lyr((1, GW)), lyr((1, GW)), lyr((8, TB)), lyr((8, TB)),
        lyr((6, L, NS)), lyr((GW, 2 * NS)), lyr((2 * NS, GW)),
        lyr((1, GW)), lyr((GW, GW)), full((1, D)),
    ]
    out_shape = (
        jax.ShapeDtypeStruct((bsz, seq, D), F32),
        jax.ShapeDtypeStruct((bsz, GW, GW), F32),
        jax.ShapeDtypeStruct((bsz, GW, GW), F32),
        jax.ShapeDtypeStruct((bsz, GW, 128), F32),
        jax.ShapeDtypeStruct((bsz, 8, 128), F32),
        jax.ShapeDtypeStruct((bsz, 8, NS), F32),
        jax.ShapeDtypeStruct((bsz, 8, NS), F32),
        jax.ShapeDtypeStruct((bsz, MEM, GW), F32),
        jax.ShapeDtypeStruct((bsz, MEM, GW), F32),
    )
    out_specs = (
        pl.BlockSpec((1, TB, D), lambda b, t: (b, t, 0)),
        per_b(GW, GW), per_b(GW, GW), per_b(GW, 128), per_b(8, 128),
        per_b(8, NS), per_b(8, NS), per_b(MEM, GW), per_b(MEM, GW),
    )
    scratch = [
        pltpu.VMEM((TB, DP), F32),
        pltpu.VMEM((TB, D), BF16),
        pltpu.VMEM((GW, GW), F32),
        pltpu.VMEM((GW, GW), F32),
        pltpu.VMEM((GW, 128), F32),
        pltpu.VMEM((8, 128), F32),
        pltpu.VMEM((8, NS), F32),
        pltpu.VMEM((8, NS), F32),
        pltpu.VMEM((MEM, GW), BF16),
        pltpu.VMEM((MEM, GW), BF16),
        pltpu.VMEM((TB, 2 * NS), F32),
        pltpu.VMEM((TB, 2 * NS), BF16),
        pltpu.VMEM((TB // LS, 2 * NS), F32),
    ]
    kern = functools.partial(_prompt_kernel, last_layer=last_layer, n_tblocks=nt)
    return pl.pallas_call(
        kern, grid=(bsz, nt), in_specs=in_specs, out_specs=out_specs, out_shape=out_shape,
        scratch_shapes=scratch, name="prompt_layer",
        compiler_params=pltpu.CompilerParams(
            dimension_semantics=("arbitrary", "arbitrary"), vmem_limit_bytes=VMEM_LIMIT),
    )(x, mem, w["norm_w"], w["w_in"], w["w_out"], w["w_mem_kv"],
      consts["cos_p"], consts["s1_p"], consts["s2_p"],
      consts["dec"], consts["qdec"], consts["kdec"], consts["cdec"],
      w["ret_gn"], w["ml_gn"], w["b_i8"], w["b_f8"],
      w["tab"], w["bt"], w["ct"], w["s5_d"], w["w_glu"], w["final_norm_w"])


def _sample_kernel(x_ref, normw_ref, win_ref, wout_ref, cos_ref, s1_ref, s2_ref, gam_ref, gcol_ref,
                   retgn_ref, mlgn_ref, bi_ref, bf_ref,
                   ab_ref, bt_ref, ct_ref, dsk_ref, wglu_ref, fnw_ref,
                   m0_ref, n0_ref, x0re_ref, x0im_ref,
                   rets_ref, mlc_ref, ck_ref, cv_ref,
                   y_ref, retn_ref, mlcn_ref, mlnn_ref, mlmn_ref, s5re_ref, s5im_ref,
                   hs_ref, proj_ref, qr8_ref, kr8_ref, vrh_ref, qmh_ref, vw8_ref, kmh_ref, a8_ref,
                   qa8_ref, ost_ref, cq_ref, xa_ref, wi_ref, ws_ref, emt_ref,
                   *, n_layers, n_blocks):
    layer = pl.program_id(0)
    g = pl.program_id(1)
    nsamp = x_ref.shape[0]

    @pl.when((layer == 0) & (g == 0))
    def _load_x():
        hs_ref[...] = x_ref[...]
    hm = _head_masks()
    ones_bd = _ones_matrix()
    avg = _avg_matrix()
    mask8 = (lax.broadcasted_iota(jnp.int32, (8, GW), 0)
             == lax.broadcasted_iota(jnp.int32, (8, GW), 1) // DH)
    expand = (lax.broadcasted_iota(jnp.int32, (128, GW), 0)
              == lax.broadcasted_iota(jnp.int32, (128, GW), 1) // DH).astype(BF16)
    fold = (lax.broadcasted_iota(jnp.int32, (GW, DH), 0) % DH
            == lax.broadcasted_iota(jnp.int32, (GW, DH), 1)).astype(BF16)
    tile4 = (lax.broadcasted_iota(jnp.int32, (DH, GW), 0)
             == lax.broadcasted_iota(jnp.int32, (DH, GW), 1) % DH).astype(BF16)

    def rows8(v):
        n = v.shape[0]
        return jnp.where(mask8[None], v[:, None, :], 0.0).reshape(8 * n, GW)

    def P(blk):
        return proj_ref[:, blk * GW:(blk + 1) * GW]

    def gates():
        gt = P(GT)
        ig = gt[:, :128] + bi_ref[...]
        lf = _log_sigmoid(gt[:, 128:] + bf_ref[...])
        a = lf + m0_ref[...]
        mt = jnp.maximum(a, ig)
        return ig, a, mt

    @pl.when(g == 0)
    def _pre():
        hn = _rms_norm(hs_ref[...], normw_ref[...]).astype(BF16)
        proj_ref[...] = _dot(hn, win_ref[...])
        cosr, s1, s2 = cos_ref[...], s1_ref[...], s2_ref[...]
        rq = _rope(P(RQ), cosr, s1, s2)
        rk = _rope(P(RK), cosr, s1, s2)
        qr8_ref[...] = rows8(rq * gam_ref[...])
        kr8_ref[...] = rows8(rk)
        vrh_ref[...] = _dot(rows8(P(RV)).astype(BF16), fold)
        ig, a, mt = gates()
        wi = _dot_x2(jnp.exp(ig - mt), expand)
        ws = _dot_x2(jnp.exp(a - mt), expand)
        wi_ref[...] = wi
        ws_ref[...] = ws
        emt_ref[...] = _dot_x2(jnp.exp(-mt), expand)
        qmh_ref[...] = _dot(rows8(P(MQ)).astype(BF16), fold)
        kmh_ref[...] = _dot(rows8(P(MK)).astype(BF16), fold)
        vw8_ref[...] = rows8(wi * P(MV))
        hi, lo = _split2(ws)
        rid = lax.broadcasted_iota(jnp.int32, (1, 8, GW), 1)
        a8 = jnp.where(rid == 0, hi.astype(F32)[:, None, :],
                       jnp.where(rid == 1, lo.astype(F32)[:, None, :], 0.0))
        a8_ref[...] = a8.reshape(8 * nsamp, GW)
        qa8_ref[...] = rows8(P(AQ))
        mlmn_ref[...] = mt
        su = P(SU)
        bu = _dot(su.astype(BF16), bt_ref[...])
        are, aim = ab_ref[0:1, :], ab_ref[1:2, :]
        x0r, x0i = x0re_ref[...], x0im_ref[...]
        s5re_ref[...] = are * x0r - aim * x0i + bu[:, :NS]
        s5im_ref[...] = are * x0i + aim * x0r + bu[:, NS:]

    ones8 = jnp.ones((8, DH), BF16)

    def sample(i, carry):
        bg = g * SB + i
        r8 = pl.ds(pl.multiple_of(bg * 8, 8), 8)
        s_b = rets_ref[i]
        t_o = _dot(qr8_ref[r8, :].astype(BF16), s_b.astype(BF16))
        o_row = jnp.sum(jnp.where(mask8, _dot_x2(t_o, tile4), 0.0), axis=0, keepdims=True)
        ost_ref[pl.ds(bg, 1), :] = o_row
        upd = _dot_tn(kr8_ref[r8, :].astype(BF16), vrh_ref[r8, :].astype(BF16))
        retn_ref[i] = s_b * gcol_ref[...] + upd
        c_b = mlc_ref[i]
        t_c = _dot_nt(qmh_ref[r8, :].astype(BF16), c_b.astype(BF16))
        cq_ref[pl.ds(bg, 1), :] = jnp.sum(jnp.where(mask8, t_c, 0.0), axis=0, keepdims=True)
        wscol = _dot_tn(a8_ref[r8, :].astype(BF16), ones8)
        updc = _dot_tn(vw8_ref[r8, :].astype(BF16), kmh_ref[r8, :].astype(BF16))
        mlcn_ref[i] = c_b * wscol + updc
        k_b = ck_ref[i].astype(BF16)
        v_b = cv_ref[i].astype(BF16)
        sc = _dot(qa8_ref[r8, :].astype(BF16), k_b)
        e = jnp.exp(sc - jnp.max(sc, axis=-1, keepdims=True))
        p = e / jnp.sum(e, axis=-1, keepdims=True)
        t_a = _dot_nt(p.astype(BF16), v_b)
        xa_ref[pl.ds(bg, 1), :] = jnp.sum(jnp.where(mask8, t_a, 0.0), axis=0, keepdims=True)
        return carry

    lax.fori_loop(0, SB, sample, 0)

    @pl.when(g == n_blocks - 1)
    def _post():
        cosr, s1, s2 = cos_ref[...], s1_ref[...], s2_ref[...]
        rq = _rope(P(RQ), cosr, s1, s2)
        rk = _rope(P(RK), cosr, s1, s2)
        ro = _dot_x2(rq * rk, ones_bd) * P(RV) + ost_ref[...]
        ret_out = _head_norm(ro, retgn_ref[...], avg) * _silu(P(RG))
        mq, mk_, mv_ = P(MQ), P(MK), P(MV)
        wi, ws, emt = wi_ref[...], ws_ref[...], emt_ref[...]
        n0 = n0_ref[...]
        s = _dot_x2(mq * mk_, ones_bd) * wi
        num = s * mv_ + ws * cq_ref[...]
        den = s + ws * _dot_x2(n0 * mq, ones_bd)
        mh = num / jnp.maximum(jnp.abs(den), emt)
        mlnn_ref[...] = ws * n0 + wi * mk_
        mh = mh * _sigmoid(P(MO))
        ml_out = _head_norm(mh, mlgn_ref[...], avg) * _silu(P(MG))
        su = P(SU)
        xcat = jnp.concatenate([s5re_ref[...], s5im_ref[...]], axis=1).astype(BF16)
        sy = _dot(xcat, ct_ref[...]) + dsk_ref[...] * su
        sy = _gelu_tanh(sy)
        sy = sy * _sigmoid(_dot(sy.astype(BF16), wglu_ref[...]))
        s5_out = sy * _silu(P(SG))
        xa_out = xa_ref[...] * _silu(P(AG))
        mix = jnp.concatenate([ret_out, ml_out, s5_out, xa_out], axis=1).astype(BF16)
        y = hs_ref[...] + _dot(mix, wout_ref[...])
        hs_ref[...] = y

        @pl.when(layer == n_layers - 1)
        def _emit():
            y_ref[...] = _rms_norm(y, fnw_ref[...])


def _sample_layers(x, st, w, consts):
    nsamp = x.shape[0]
    depth = w["w_in"].shape[0]
    nb = nsamp // SB
    once = lambda shape: pl.BlockSpec(shape, lambda l, g: (0,) * len(shape),
                                      pipeline_mode=pl.Buffered(1))
    lyr = lambda shape: pl.BlockSpec((None,) + shape, lambda l, g: (l,) + (0,) * len(shape),
                                     pipeline_mode=pl.Buffered(1))
    lyr_out = lambda shape: pl.BlockSpec((None,) + shape, lambda l, g: (l,) + (0,) * len(shape))
    blk = lambda r, c: pl.BlockSpec((None, SB, r, c), lambda l, g: (l, g, 0, 0))
    in_specs = [
        once((nsamp, D)), lyr((1, D)), lyr((D, DP)), lyr((D, D)),
        once((1, 128)), once((1, 128)), once((1, 128)), once((1, GW)), once((GW, DH)),
        lyr((1, GW)), lyr((1, GW)), lyr((1, 128)), lyr((1, 128)),
        lyr((2, NS)), lyr((GW, 2 * NS)), lyr((2 * NS, GW)),
        lyr((1, GW)), lyr((GW, GW)), once((1, D)),
        lyr((nsamp, 128)), lyr((nsamp, GW)), lyr((nsamp, NS)), lyr((nsamp, NS)),
        blk(GW, DH), blk(GW, DH), blk(GW, MEM), blk(GW, MEM),
    ]
    out_shape = (
        jax.ShapeDtypeStruct((nsamp, D), F32),
        jax.ShapeDtypeStruct((depth, nsamp, GW, DH), F32),
        jax.ShapeDtypeStruct((depth, nsamp, GW, DH), F32),
        jax.ShapeDtypeStruct((depth, nsamp, GW), F32),
        jax.ShapeDtypeStruct((depth, nsamp, 128), F32),
        jax.ShapeDtypeStruct((depth, nsamp, NS), F32),
        jax.ShapeDtypeStruct((depth, nsamp, NS), F32),
    )
    out_specs = (
        pl.BlockSpec((nsamp, D), lambda l, g: (0, 0)), blk(GW, DH), blk(GW, DH),
        lyr_out((nsamp, GW)), lyr_out((nsamp, 128)), lyr_out((nsamp, NS)), lyr_out((nsamp, NS)),
    )
    r8 = 8 * nsamp
    scratch = [
        pltpu.VMEM((nsamp, D), F32),
        pltpu.VMEM((nsamp, DP), F32),
        pltpu.VMEM((r8, GW), F32), pltpu.VMEM((r8, GW), F32), pltpu.VMEM((r8, DH), F32),
        pltpu.VMEM((r8, DH), F32), pltpu.VMEM((r8, GW), F32), pltpu.VMEM((r8, DH), F32),
        pltpu.VMEM((r8, GW), F32), pltpu.VMEM((r8, GW), F32),
        pltpu.VMEM((nsamp, GW), F32), pltpu.VMEM((nsamp, GW), F32), pltpu.VMEM((nsamp, GW), F32),
        pltpu.VMEM((nsamp, GW), F32), pltpu.VMEM((nsamp, GW), F32), pltpu.VMEM((nsamp, GW), F32),
    ]
    kern = functools.partial(_sample_kernel, n_layers=depth, n_blocks=nb)
    return pl.pallas_call(
        kern, grid=(depth, nb), in_specs=in_specs, out_specs=out_specs, out_shape=out_shape,
        scratch_shapes=scratch, name="sample_layers",
        compiler_params=pltpu.CompilerParams(
            dimension_semantics=("arbitrary", "arbitrary"), vmem_limit_bytes=VMEM_LIMIT),
    )(x, w["norm_w"], w["w_in"], w["w_out"],
      consts["cos_s"], consts["s1_s"], consts["s2_s"], consts["gam"], consts["gcol"],
      w["ret_gn"], w["ml_gn"], w["b_i"], w["b_f"],
      w["ab"], w["bt"], w["ct"], w["s5_d"], w["w_glu"], w["final_norm_w"],
      st["m"], st["n"], st["s5_re"], st["s5_im"], st["ret"], st["c"], st["mem_k"], st["mem_v"])


def _rope_tables(pos):
    half = DH // 2
    inv = ROPE_BASE ** (-jnp.arange(half, dtype=F32) / half)
    ang = pos.astype(F32)[:, None] * inv[None, :]
    cos, sin = jnp.cos(ang), jnp.sin(ang)
    zero = jnp.zeros_like(sin)
    c = jnp.tile(jnp.concatenate([cos, cos], axis=-1), (1, 2))
    s1 = jnp.tile(jnp.concatenate([zero, sin], axis=-1), (1, 2))
    s2 = jnp.tile(jnp.concatenate([-sin, zero], axis=-1), (1, 2))
    return c, s1, s2


def _constants(seq):
    lg = jnp.log1p(-jnp.power(2.0, -5.0 - jnp.arange(NH, dtype=F32)))[:, None]
    idx = jnp.arange(L, dtype=F32)
    diff = idx[:, None] - idx[None, :]
    decay = jnp.where(diff >= 0, jnp.exp(lg[:, :, None] * jnp.maximum(diff, 0.0)), 0.0)
    rep = lambda t: jnp.repeat(t, DH, axis=0).T
    consts = {
        "dec": decay.reshape(NH * L, L),
        "qdec": rep(jnp.exp(lg * (idx + 1.0))),
        "kdec": rep(jnp.exp(lg * (L - 1.0 - idx))),
        "cdec": rep(jnp.exp(lg * L)),
        "gam": rep(jnp.exp(lg * 1.0)),
    }
    consts["gcol"] = jnp.broadcast_to(consts["gam"].T, (GW, DH))
    consts["cos_p"], consts["s1_p"], consts["s2_p"] = _rope_tables(jnp.arange(seq, dtype=jnp.int32))
    consts["cos_s"], consts["s1_s"], consts["s2_s"] = _rope_tables(
        PAST_LEN + jnp.arange(1, dtype=jnp.int32))
    return consts


def _pack_w_in(w_in):
    sizes = (GW,) * 9 + (NH, NH) + (GW,) * 4
    offs = np.concatenate([[0], np.cumsum(sizes)])
    seg = [w_in[:, :, int(offs[i]):int(offs[i + 1])] for i in range(len(sizes))]
    scale = DH ** -0.5
    pad = lambda t: jnp.pad(t, ((0, 0), (0, 0), (0, 128 - t.shape[-1])))
    blocks = [seg[0], seg[1] * scale, seg[2], seg[3],
              seg[4], seg[5] * scale, seg[6], seg[7], seg[8],
              seg[11], seg[12], seg[13] * scale, seg[14],
              pad(seg[9]), pad(seg[10])]
    return jnp.concatenate(blocks, axis=-1).astype(BF16)


def kernel(x_prompt, x_sample, mem_prompt, state_ret, state_mlstm_c, state_mlstm_n, state_mlstm_m,
           state_s5_re, state_s5_im, cache_mem_k, cache_mem_v,
           norm_w, w_in, ret_gn, ml_b_i, ml_b_f, ml_gn,
           s5_a_re, s5_a_im, s5_log_dt, s5_b_re, s5_b_im, s5_c_re, s5_c_im, s5_d, s5_w_glu,
           w_mem_k, w_mem_v, w_out, final_norm_w):
    depth = norm_w.shape[0]
    bp, seq, _ = x_prompt.shape
    bs = x_sample.shape[0]
    consts = _constants(seq)
    ab, bt, ct, tab = _s5_prepare(s5_a_re, s5_a_im, s5_log_dt, s5_b_re, s5_b_im, s5_c_re, s5_c_im)
    pad_lanes = lambda t: jnp.pad(t, ((0, 0),) * (t.ndim - 1) + ((0, 128 - t.shape[-1]),))
    rows8 = lambda t: jnp.pad(jnp.broadcast_to(t[:, :, None], (depth, NH, TB)),
                              ((0, 0), (0, 8 - NH), (0, 0)))
    w = {
        "norm_w": norm_w[:, None], "w_in": _pack_w_in(w_in), "w_out": w_out.astype(BF16),
        "w_mem_kv": jnp.concatenate([w_mem_k, w_mem_v], axis=-1).astype(BF16),
        "ret_gn": ret_gn[:, None], "ml_gn": ml_gn[:, None],
        "b_i": pad_lanes(ml_b_i)[:, None], "b_f": pad_lanes(ml_b_f)[:, None],
        "b_i8": rows8(ml_b_i), "b_f8": rows8(ml_b_f),
        "ab": ab, "tab": tab, "bt": bt, "ct": ct,
        "s5_d": s5_d[:, None], "w_glu": s5_w_glu.astype(BF16),
        "final_norm_w": final_norm_w[None],
    }

    hp = x_prompt
    outs_p = [[] for _ in range(8)]
    for l in range(depth):
        hp, *sp = _prompt_layer(l, hp, mem_prompt, w, consts, l == depth - 1)
        for i in range(8):
            outs_p[i].append(sp[i])
    rs, cs, ns, ms, xr, xi, mk, mv = (jnp.stack(o) for o in outs_p)
    diag = lambda t: jnp.stack([t[:, :, h * DH:(h + 1) * DH, h * DH:(h + 1) * DH]
                                for h in range(NH)], axis=2)
    ret_p = diag(rs)
    mlc_p = jnp.swapaxes(diag(cs), -1, -2)
    mln_p = jnp.stack([ns[:, :, h * DH:(h + 1) * DH, h] for h in range(NH)], axis=2)
    mlm_p = ms[:, :, :NH, 0]
    s5re_p = xr[:, :, 0].reshape(depth, bp, S5G, S5P)
    s5im_p = xi[:, :, 0].reshape(depth, bp, S5G, S5P)
    memk_p = mk.reshape(depth, bp, MEM, NH, DH)
    memv_p = mv.reshape(depth, bp, MEM, NH, DH)

    st = {
        "m": pad_lanes(state_mlstm_m), "n": state_mlstm_n.reshape(depth, bs, GW),
        "s5_re": state_s5_re.reshape(depth, bs, NS), "s5_im": state_s5_im.reshape(depth, bs, NS),
        "ret": state_ret.reshape(depth, bs, GW, DH), "c": state_mlstm_c.reshape(depth, bs, GW, DH),
        "mem_k": jnp.transpose(cache_mem_k, (0, 1, 3, 4, 2)).reshape(depth, bs, GW, MEM),
        "mem_v": jnp.transpose(cache_mem_v, (0, 1, 3, 4, 2)).reshape(depth, bs, GW, MEM),
    }
    hs, rn, cn, nn, mn, sr, si = _sample_layers(x_sample.reshape(bs, D), st, w, consts)
    return (hp, hs.reshape(bs, 1, D),
            ret_p, rn.reshape(depth, bs, NH, DH, DH), mlc_p, cn.reshape(depth, bs, NH, DH, DH),
            mln_p, nn.reshape(depth, bs, NH, DH), mlm_p, mn[:, :, :NH],
            s5re_p, sr.reshape(depth, bs, S5G, S5P), s5im_p, si.reshape(depth, bs, S5G, S5P),
            memk_p, memv_p)
```

```python
import functools
import math

import numpy as np
import jax
import jax.numpy as jnp
from jax import lax
from jax.experimental import pallas as pl
from jax.experimental.pallas import tpu as pltpu

F32 = jnp.float32
BF16 = jnp.bfloat16

D = 1024
GW = 256
NH = 4
DH = 64
L = 128
LS = 8
S5G = 16
S5P = 64
S5C = 16
NS = S5G * S5P
MEM = 256
EPS = 1e-6
NEG_INF = -1e30
ROPE_BASE = 10000.0
PAST_LEN = 16384

TB = 512
SB = 8
XQ = 256
NBLK = 14
DP = NBLK * GW
(RQ, RK, RV, RG, MQ, MK, MV, MO, MG, SU, SG, AQ, AG, GT) = range(NBLK)

VMEM_LIMIT = 56 * 1024 * 1024


_DONE = object()


def _dot(a, b):
    return jnp.dot(a, b, preferred_element_type=F32)


def _dot_nt(a, b):
    return lax.dot_general(a, b, (((1,), (1,)), ((), ())), preferred_element_type=F32)


def _dot_tn(a, b):
    return lax.dot_general(a, b, (((0,), (0,)), ((), ())), preferred_element_type=F32)


def _split2(x):
    hi = x.astype(BF16)
    lo = (x - hi.astype(F32)).astype(BF16)
    return hi, lo


def _dot_x2(x, w):
    hi, lo = _split2(x)
    return _dot(hi, w) + _dot(lo, w)


def _dot_x3(x, w):
    hi = x.astype(BF16)
    r = x - hi.astype(F32)
    mid = r.astype(BF16)
    lo = (r - mid.astype(F32)).astype(BF16)
    return _dot(hi, w) + _dot(mid, w) + _dot(lo, w)


def _sigmoid(x):
    return 1.0 / (1.0 + jnp.exp(-x))


def _silu(x):
    return x * _sigmoid(x)


def _log_sigmoid(x):
    return jnp.minimum(x, 0.0) - jnp.log1p(jnp.exp(-jnp.abs(x)))


def _gelu_tanh(x):
    c = math.sqrt(2.0 / math.pi)
    return x * (0.5 * (1.0 + jnp.tanh(c * (x + 0.044715 * (x * x * x)))))


def _lane_head(n):
    return lax.broadcasted_iota(jnp.int32, (1, n), 1) // DH


def _head_masks():
    lh = _lane_head(GW)
    return [lh == h for h in range(NH)]


def _block_diag_mask():
    r = lax.broadcasted_iota(jnp.int32, (GW, GW), 0) // DH
    c = lax.broadcasted_iota(jnp.int32, (GW, GW), 1) // DH
    return r == c


def _avg_matrix():
    return jnp.where(_block_diag_mask(), 1.0 / DH, 0.0).astype(BF16)


def _ones_matrix():
    return jnp.where(_block_diag_mask(), 1.0, 0.0).astype(BF16)


def _rope(x, cos, s1, s2):
    outs = []
    for j in range(2):
        xs = x[:, j * 128:(j + 1) * 128]
        outs.append(xs * cos + pltpu.roll(xs, 32, 1) * s1 + pltpu.roll(xs, 96, 1) * s2)
    return jnp.concatenate(outs, axis=1)


def _head_norm(x, gain, avg):
    mu = _dot_x2(x, avg)
    d = x - mu
    var = _dot_x2(d * d, avg)
    return d * lax.rsqrt(var + EPS) * gain


def _rms_norm(x, w):
    ms = jnp.mean(x * x, axis=-1, keepdims=True)
    return x * lax.rsqrt(ms + EPS) * w


def _stack_heads(x, hm):
    return jnp.concatenate([jnp.where(hm[h], x, 0.0) for h in range(NH)], axis=0)


def _fold_heads(r, hm, rows):
    out = jnp.where(hm[0], r[0:rows], 0.0)
    for h in range(1, NH):
        out = out + jnp.where(hm[h], r[h * rows:(h + 1) * rows], 0.0)
    return out


def _s5_prep_kernel(are_ref, aim_ref, ldt_ref, bre_ref, bim_ref, cre_ref, cim_ref,
                    abt_ref, bt_ref, ct_ref, tab_ref):
    a_re = are_ref[0]
    a_im = aim_ref[0]
    dt = jnp.exp(ldt_ref[0])
    lam_re = a_re * dt
    lam_im = a_im * dt
    mag = jnp.exp(lam_re)
    ab_re = mag * jnp.cos(lam_im)
    ab_im = mag * jnp.sin(lam_im)
    den = a_re * a_re + a_im * a_im
    nr = ab_re - 1.0
    ni = ab_im
    f_re = (nr * a_re + ni * a_im) / den
    f_im = (ni * a_re - nr * a_im) / den
    abt_ref[0, 0:NS, :] = jnp.broadcast_to(ab_re, (128, NS)).T
    abt_ref[0, NS:2 * NS, :] = jnp.broadcast_to(ab_im, (128, NS)).T
    b_re = bre_ref[0]
    b_im = bim_ref[0]
    bb_re = (f_re * b_re - f_im * b_im).astype(BF16)
    bb_im = (f_re * b_im + f_im * b_re).astype(BF16)
    rep_r = (lax.broadcasted_iota(jnp.int32, (GW, S5C), 0) % S5C
             == lax.broadcasted_iota(jnp.int32, (GW, S5C), 1)).astype(BF16)
    in_blk = (lax.broadcasted_iota(jnp.int32, (GW, NS), 0) // S5C
              == lax.broadcasted_iota(jnp.int32, (GW, NS), 1) // S5P)
    bt_ref[0, :, :NS] = jnp.where(in_blk, _dot(rep_r, bb_re), 0.0).astype(BF16)
    bt_ref[0, :, NS:] = jnp.where(in_blk, _dot(rep_r, bb_im), 0.0).astype(BF16)
    rep_c = (lax.broadcasted_iota(jnp.int32, (S5C, GW), 0)
             == lax.broadcasted_iota(jnp.int32, (S5C, GW), 1) % S5C).astype(BF16)
    out_blk = (lax.broadcasted_iota(jnp.int32, (NS, GW), 0) // S5P
               == lax.broadcasted_iota(jnp.int32, (NS, GW), 1) // S5C)
    ct_ref[0, :NS, :] = jnp.where(out_blk, _dot(cre_ref[0].astype(BF16), rep_c), 0.0).astype(BF16)
    ct_ref[0, NS:, :] = jnp.where(out_blk, -_dot(cim_ref[0].astype(BF16), rep_c), 0.0).astype(BF16)
    i = (lax.broadcasted_iota(jnp.int32, (L, NS), 0) % LS).astype(F32)
    for slot, k in ((0, -i), (2, i), (4, i + 1.0)):
        pmag = jnp.exp(k * lam_re)
        tab_ref[0, slot] = pmag * jnp.cos(k * lam_im)
        tab_ref[0, slot + 1] = pmag * jnp.sin(k * lam_im)


def _s5_prepare(a_re, a_im, log_dt, b_re, b_im, c_re, c_im):
    depth = a_re.shape[0]
    are = a_re.reshape(depth, 1, NS)
    aim = a_im.reshape(depth, 1, NS)
    ldt = jnp.repeat(log_dt, S5P, axis=-1).reshape(depth, 1, NS)
    bre = jnp.transpose(b_re, (0, 3, 1, 2)).reshape(depth, S5C, NS)
    bim = jnp.transpose(b_im, (0, 3, 1, 2)).reshape(depth, S5C, NS)
    cre = jnp.transpose(c_re, (0, 1, 3, 2)).reshape(depth, NS, S5C)
    cim = jnp.transpose(c_im, (0, 1, 3, 2)).reshape(depth, NS, S5C)
    per_layer = lambda *shape: pl.BlockSpec((1,) + shape, lambda l: (l,) + (0,) * len(shape))
    out_shape = (jax.ShapeDtypeStruct((depth, 2 * NS, 128), F32),
                 jax.ShapeDtypeStruct((depth, GW, 2 * NS), BF16),
                 jax.ShapeDtypeStruct((depth, 2 * NS, GW), BF16),
                 jax.ShapeDtypeStruct((depth, 6, L, NS), F32))
    return pl.pallas_call(
        _s5_prep_kernel, grid=(depth,),
        in_specs=[per_layer(1, NS), per_layer(1, NS), per_layer(1, NS),
                  per_layer(S5C, NS), per_layer(S5C, NS), per_layer(NS, S5C), per_layer(NS, S5C)],
        out_specs=(per_layer(2 * NS, 128), per_layer(GW, 2 * NS), per_layer(2 * NS, GW),
                   per_layer(6, L, NS)),
        out_shape=out_shape, name="s5_prepare",
        compiler_params=pltpu.CompilerParams(dimension_semantics=("arbitrary",)),
    )(are, aim, ldt, bre, bim, cre, cim)


def _prompt_kernel(x_ref, mem_ref, normw_ref, win_ref, wout_ref, wmkv_ref,
                   cos_ref, s1_ref, s2_ref, dec_ref, qdec_ref, kdec_ref, cdec_ref,
                   retgn_ref, mlgn_ref, bi_ref, bf_ref,
                   tab_ref, bt_ref, ct_ref, dsk_ref, wglu_ref, fnw_ref,
                   y_ref, rets_ref, mlc_ref, mln_ref, mlm_ref, s5re_ref, s5im_ref,
                   memk_ref, memv_ref,
                   proj_ref, mix_ref, s_ref, c_ref, n_ref, m_ref, xre_ref, xim_ref,
                   mk_ref, mv_ref, bu_ref, xcat_ref, car_ref, *, last_layer, n_tblocks):
    t = pl.program_id(1)
    hm = _head_masks()
    bd = _block_diag_mask()
    avg = _avg_matrix()
    lane128 = lax.broadcasted_iota(jnp.int32, (1, 128), 1)
    row_i = lax.broadcasted_iota(jnp.int32, (L, 128), 0)
    col_i = lax.broadcasted_iota(jnp.int32, (L, 128), 1)
    causal = row_i >= col_i
    tri_u = jnp.where(row_i <= col_i, 1.0, 0.0).astype(BF16)
    tri_sub = jnp.where(causal & (row_i // LS == col_i // LS), 1.0, 0.0)
    blk_sum = jnp.where(lax.broadcasted_iota(jnp.int32, (L // LS, 128), 0)
                        == lax.broadcasted_iota(jnp.int32, (L // LS, 128), 1) // LS, 1.0, 0.0)
    tri_ext = jnp.concatenate([tri_sub, blk_sum], axis=0).astype(BF16)

    @pl.when(t == 0)
    def _init():
        s_ref[...] = jnp.zeros_like(s_ref)
        c_ref[...] = jnp.zeros_like(c_ref)
        n_ref[...] = jnp.zeros_like(n_ref)
        m_ref[...] = jnp.zeros_like(m_ref)
        xre_ref[...] = jnp.zeros_like(xre_ref)
        xim_ref[...] = jnp.zeros_like(xim_ref)
        mkv = _dot(mem_ref[0].astype(BF16), wmkv_ref[...])
        mk = mkv[:, :GW]
        mv = mkv[:, GW:]
        memk_ref[0] = mk
        memv_ref[0] = mv
        mk_ref[...] = mk.astype(BF16)
        mv_ref[...] = mv.astype(BF16)

    x = x_ref[0]
    hn = _rms_norm(x, normw_ref[...]).astype(BF16)
    proj_ref[...] = _dot(hn, win_ref[...])

    nc = TB // L
    crow = [slice(c * L, (c + 1) * L) for c in range(nc)]

    def PB(blk, rows=slice(None)):
        return proj_ref[rows, blk * GW:(blk + 1) * GW]


    def xattn_stages():
        for piece in range(TB // XQ):
            rs = slice(piece * XQ, (piece + 1) * XQ)
            sc = _dot_nt(_stack_heads(PB(AQ, rs), hm).astype(BF16), mk_ref[...])
            yield
            e = jnp.exp(sc - jnp.max(sc, axis=-1, keepdims=True))
            p = (e / jnp.sum(e, axis=-1, keepdims=True)).astype(BF16)
            yield
            r = _dot(p, mv_ref[...])
            yield
            mix_ref[rs, 3 * GW:4 * GW] = (_fold_heads(r, hm, XQ) * _silu(PB(AG, rs))).astype(BF16)
            yield

    def s5_stages():
        lanes = [(slice(j * 128, (j + 1) * 128), slice(NS + j * 128, NS + (j + 1) * 128))
                 for j in range(NS // 128)]
        bpc = L // LS
        nblk = TB // LS
        su = PB(SU)
        bu_ref[...] = _dot(su.astype(BF16), bt_ref[...])
        yield
        for c in range(nc):
            for lre, lim in lanes:
                br = bu_ref[crow[c], lre]
                bi = bu_ref[crow[c], lim]
                wr = tab_ref[0, :, lre]
                wi = tab_ref[1, :, lre]
                xcat_ref[crow[c], lre] = (wr * br - wi * bi).astype(BF16)
                xcat_ref[crow[c], lim] = (wr * bi + wi * br).astype(BF16)
            yield
        for c in range(nc):
            z = _dot(tri_ext, xcat_ref[crow[c], :])
            bu_ref[crow[c], :] = z[0:L]
            car_ref[c * bpc:(c + 1) * bpc, :] = z[L:L + bpc]
            yield
        rowb = lax.broadcasted_iota(jnp.int32, (nblk, 128), 0)
        ers, eis, prs, pis, c0s = [], [], [], [], []
        for lre, lim in lanes:
            zr = car_ref[:, lre]
            zi = car_ref[:, lim]
            pr = tab_ref[2, LS - 1:LS, lre]
            pi = tab_ref[3, LS - 1:LS, lre]
            er = pr * zr - pi * zi
            ei = pr * zi + pi * zr
            pr = tab_ref[4, LS - 1:LS, lre]
            pi = tab_ref[5, LS - 1:LS, lre]
            c0r = xre_ref[0:1, lre]
            c0i = xim_ref[0:1, lre]
            ers.append(er + jnp.where(rowb == 0, pr * c0r - pi * c0i, 0.0))
            eis.append(ei + jnp.where(rowb == 0, pr * c0i + pi * c0r, 0.0))
            prs.append(pr)
            pis.append(pi)
            c0s.append((c0r, c0i))
        yield
        for k in range(nblk.bit_length() - 1):
            d = 1 << k
            for j in range(len(lanes)):
                er, ei, pr, pi = ers[j], eis[j], prs[j], pis[j]
                sr = jnp.where(rowb >= d, pltpu.roll(er, d, 0), 0.0)
                si = jnp.where(rowb >= d, pltpu.roll(ei, d, 0), 0.0)
                ers[j], eis[j] = er + pr * sr - pi * si, ei + pr * si + pi * sr
                prs[j], pis[j] = pr * pr - pi * pi, 2.0 * (pr * pi)
            yield
        for j, (lre, lim) in enumerate(lanes):
            xre_ref[:, lre] = jnp.broadcast_to(ers[j][nblk - 1:nblk, :], (8, 128))
            xim_ref[:, lre] = jnp.broadcast_to(eis[j][nblk - 1:nblk, :], (8, 128))
            cr = jnp.where(rowb == 0, c0s[j][0], pltpu.roll(ers[j], 1, 0))
            ci_ = jnp.where(rowb == 0, c0s[j][1], pltpu.roll(eis[j], 1, 0))
            ar = tab_ref[4, 0:1, lre]
            ai = tab_ref[5, 0:1, lre]
            car_ref[:, lre] = ar * cr - ai * ci_
            car_ref[:, lim] = ar * ci_ + ai * cr
        yield
        for c in range(nc):
            for lre, lim in lanes:
                cr = jnp.broadcast_to(car_ref[c * bpc:(c + 1) * bpc, lre][:, None, :],
                                      (bpc, LS, 128)).reshape(L, 128)
                ci_ = jnp.broadcast_to(car_ref[c * bpc:(c + 1) * bpc, lim][:, None, :],
                                       (bpc, LS, 128)).reshape(L, 128)
                zr = bu_ref[crow[c], lre] + cr
                zi = bu_ref[crow[c], lim] + ci_
                pr = tab_ref[2, :, lre]
                pi = tab_ref[3, :, lre]
                xcat_ref[crow[c], lre] = (pr * zr - pi * zi).astype(BF16)
                xcat_ref[crow[c], lim] = (pr * zi + pi * zr).astype(BF16)
            yield
        sy = _dot(xcat_ref[...], ct_ref[...]) + dsk_ref[...] * su
        yield
        sy = _gelu_tanh(sy)
        gate = _dot(sy.astype(BF16), wglu_ref[...])
        yield
        mix_ref[:, 2 * GW:3 * GW] = (sy * _sigmoid(gate) * _silu(PB(SG))).astype(BF16)
        yield

    def retention_stages():
        rq = _rope(PB(RQ), cos_ref[...], s1_ref[...], s2_ref[...])
        rk = _rope(PB(RK), cos_ref[...], s1_ref[...], s2_ref[...])
        rv = PB(RV).astype(BF16)
        rkb = rk.astype(BF16)
        qst = [_stack_heads(rq[crow[c]], hm).astype(BF16) for c in range(nc)]
        yield
        inner = [_dot_nt(qst[c], rkb[crow[c]]) for c in range(nc)]
        yield
        pmat = [(inner[c] * dec_ref[...]).astype(BF16) for c in range(nc)]
        kd = [(rk[crow[c]] * kdec_ref[...]).astype(BF16) for c in range(nc)]
        yield
        rloc = [_dot(pmat[c], rv[crow[c]]) for c in range(nc)]
        upd = [_dot_tn(kd[c], rv[crow[c]]) for c in range(nc)]
        yield
        st = [s_ref[...]]
        for c in range(nc):
            st.append(st[c] * cdec_ref[...] + jnp.where(bd, upd[c], 0.0))
        s_ref[...] = st[nc]
        qd = [(rq[crow[c]] * qdec_ref[...]).astype(BF16) for c in range(nc)]
        yield
        ost = [_dot(qd[c], st[c].astype(BF16)) for c in range(nc)]
        yield
        ro = jnp.concatenate([_fold_heads(rloc[c], hm, L) + ost[c] for c in range(nc)], axis=0)
        mix_ref[:, 0:GW] = (_head_norm(ro, retgn_ref[...], avg) * _silu(PB(RG))).astype(BF16)
        yield

    def mlstm_stages():
        g = PB(GT)
        ig_r = jnp.concatenate([g[crow[c], 0:128].T[0:8] for c in range(nc)], axis=1) + bi_ref[...]
        lf_r = _log_sigmoid(
            jnp.concatenate([g[crow[c], 128:256].T[0:8] for c in range(nc)], axis=1) + bf_ref[...])
        yield
        b_r = [_dot_x3(lf_r[:, crow[c]], tri_u) for c in range(nc)]
        yield
        gd_r = [ig_r[:, crow[c]] - b_r[c] for c in range(nc)]
        lane_r = lax.broadcasted_iota(jnp.int32, (8, L), 1)
        cm_r = list(gd_r)
        for k in range(7):
            d = 1 << k
            cm_r = [jnp.maximum(v, jnp.where(lane_r >= d, pltpu.roll(v, d, 1), NEG_INF))
                    for v in cm_r]
            yield
        m_prev = [m_ref[...]]
        mt_r = []
        for c in range(nc):
            mt = jnp.maximum(b_r[c] + m_prev[c], b_r[c] + cm_r[c])
            mt_r.append(mt)
            m_prev.append(jnp.broadcast_to(mt[:, L - 1:L], (8, L)))
            yield
        m_ref[...] = m_prev[nc]
        cols = []
        for c in range(nc):
            bm = b_r[c] - mt_r[c]
            ws = jnp.exp(b_r[c] + m_prev[c] - mt_r[c])
            wl = jnp.exp(gd_r[c] + jnp.broadcast_to(bm[:, L - 1:L], (8, L)))
            emt = jnp.exp(-mt_r[c])
            cols.append(jnp.concatenate([bm, ws, wl, emt, jnp.zeros((L - 32, L), F32)], axis=0).T)
        mqf = PB(MQ)
        mq = mqf.astype(BF16)
        mkf = PB(MK)
        mkb = mkf.astype(BF16)
        mv_ = PB(MV).astype(BF16)
        qst = [_stack_heads(mqf[crow[c]], hm).astype(BF16) for c in range(nc)]
        yield
        sraw = [_dot_nt(qst[c], mkb[crow[c]]) for c in range(nc)]
        yield
        smat, den_i = [], []
        for c in range(nc):
            parts = []
            for h in range(NH):
                arg = jnp.where(causal, cols[c][:, h:h + 1] + gd_r[c][h:h + 1, :], NEG_INF)
                parts.append(sraw[c][h * L:(h + 1) * L] * jnp.exp(arg))
            s_c = jnp.concatenate(parts, axis=0)
            den_i.append(jnp.sum(s_c, axis=-1, keepdims=True))
            smat.append(s_c.astype(BF16))
            yield
        ones_blk = jnp.ones((L, 128), BF16)
        kws = []
        for c in range(nc):
            kw = jnp.zeros((L, GW), F32)
            for h in range(NH):
                kw = kw + jnp.where(hm[h], mkf[crow[c]] * cols[c][:, 16 + h:17 + h], 0.0)
            kws.append(kw.astype(BF16))
        yield
        rloc = [_dot(smat[c], mv_[crow[c]]) for c in range(nc)]
        u = [_dot_tn(kws[c], jnp.concatenate([mv_[crow[c]], ones_blk], axis=1))
             for c in range(nc)]
        yield
        nmask = (lax.broadcasted_iota(jnp.int32, (GW, 128), 0) // DH
                 == lax.broadcasted_iota(jnp.int32, (GW, 128), 1))
        cst = [c_ref[...]]
        nst = [n_ref[...]]
        for c in range(nc):
            wsl256 = jnp.zeros((1, GW), F32)
            wsl128 = jnp.zeros((1, 128), F32)
            for h in range(NH):
                wsl = cols[c][L - 1:L, 8 + h:9 + h]
                wsl256 = wsl256 + jnp.where(hm[h], wsl, 0.0)
                wsl128 = wsl128 + jnp.where(lane128 == h, wsl, 0.0)
            cst.append(cst[c] * wsl256 + jnp.where(bd, u[c][:, :GW], 0.0))
            nst.append(nst[c] * wsl128 + jnp.where(nmask, u[c][:, GW:], 0.0))
        c_ref[...] = cst[nc]
        n_ref[...] = nst[nc]
        yield
        qc = [_dot(mq[crow[c]], cst[c].astype(BF16)) for c in range(nc)]
        qn = [_dot(mq[crow[c]], nst[c].astype(BF16)) for c in range(nc)]
        yield
        mhs = []
        for c in range(nc):
            mh = jnp.zeros((L, GW), F32)
            for h in range(NH):
                ws = cols[c][:, 8 + h:9 + h]
                den = den_i[c][h * L:(h + 1) * L] + ws * qn[c][:, h:h + 1]
                dn = jnp.maximum(jnp.abs(den), cols[c][:, 24 + h:25 + h])
                mh = mh + jnp.where(hm[h], (rloc[c][h * L:(h + 1) * L] + ws * qc[c]) / dn, 0.0)
            mhs.append(mh)
            yield
        mh = jnp.concatenate(mhs, axis=0) * _sigmoid(PB(MO))
        mix_ref[:, GW:2 * GW] = (_head_norm(mh, mlgn_ref[...], avg) * _silu(PB(MG))).astype(BF16)
        yield

    pending = [mlstm_stages(), s5_stages(), retention_stages(), xattn_stages()]
    while pending:
        pending = [stage for stage in pending if next(stage, _DONE) is not _DONE]

    y = x + _dot(mix_ref[...], wout_ref[...])
    if last_layer:
        y = _rms_norm(y, fnw_ref[...])
    y_ref[0] = y

    @pl.when(t == n_tblocks - 1)
    def _final():
        rets_ref[0] = s_ref[...]
        mlc_ref[0] = c_ref[...]
        mln_ref[0] = n_ref[...]
        mlm_ref[0] = m_ref[...]
        s5re_ref[0] = xre_ref[...]
        s5im_ref[0] = xim_ref[...]


def _prompt_layer(layer, x, mem, w, consts, last_layer):
    bsz, seq, _ = x.shape
    nt = seq // TB
    full = lambda shape: pl.BlockSpec(shape, lambda b, t: (0,) * len(shape),
                                      pipeline_mode=pl.Buffered(1))
    lyr = lambda shape: pl.BlockSpec((None,) + shape, lambda b, t: (layer,) + (0,) * len(shape),
                                     pipeline_mode=pl.Buffered(1))
    tok = lambda width: pl.BlockSpec((TB, width), lambda b, t: (t, 0))
    per_b = lambda r, c: pl.BlockSpec((1, r, c), lambda b, t: (b, 0, 0))
    in_specs = [
        pl.BlockSpec((1, TB, D), lambda b, t: (b, t, 0)),
        per_b(MEM, D),
        lyr((1, D)), lyr((D, DP)), lyr((D, D)), lyr((D, 2 * GW)),
        tok(128), tok(128), tok(128),
        full((NH * L, L)), full((L, GW)), full((L, GW)), full((1, GW)),
        lyr((1, GW)), lyr((1, GW)), lyr((8, TB)), lyr((8, TB)),
        lyr((6, L, NS)), lyr((GW, 2 * NS)), lyr((2 * NS, GW)),
        lyr((1, GW)), lyr((GW, GW)), full((1, D)),
    ]
    out_shape = (
        jax.ShapeDtypeStruct((bsz, seq, D), F32),
        jax.ShapeDtypeStruct((bsz, GW, GW), F32),
        jax.ShapeDtypeStruct((bsz, GW, GW), F32),
        jax.ShapeDtypeStruct((bsz, GW, 128), F32),
        jax.ShapeDtypeStruct((bsz, 8, 128), F32),
        jax.ShapeDtypeStruct((bsz, 8, NS), F32),
        jax.ShapeDtypeStruct((bsz, 8, NS), F32),
        jax.ShapeDtypeStruct((bsz, MEM, GW), F32),
        jax.ShapeDtypeStruct((bsz, MEM, GW), F32),
    )
    out_specs = (
        pl.BlockSpec((1, TB, D), lambda b, t: (b, t, 0)),
        per_b(GW, GW), per_b(GW, GW), per_b(GW, 128), per_b(8, 128),
        per_b(8, NS), per_b(8, NS), per_b(MEM, GW), per_b(MEM, GW),
    )
    scratch = [
        pltpu.VMEM((TB, DP), F32),
        pltpu.VMEM((TB, D), BF16),
        pltpu.VMEM((GW, GW), F32),
        pltpu.VMEM((GW, GW), F32),
        pltpu.VMEM((GW, 128), F32),
        pltpu.VMEM((8, 128), F32),
        pltpu.VMEM((8, NS), F32),
        pltpu.VMEM((8, NS), F32),
        pltpu.VMEM((MEM, GW), BF16),
        pltpu.VMEM((MEM, GW), BF16),
        pltpu.VMEM((TB, 2 * NS), F32),
        pltpu.VMEM((TB, 2 * NS), BF16),
        pltpu.VMEM((TB // LS, 2 * NS), F32),
    ]
    kern = functools.partial(_prompt_kernel, last_layer=last_layer, n_tblocks=nt)
    return pl.pallas_call(
        kern, grid=(bsz, nt), in_specs=in_specs, out_specs=out_specs, out_shape=out_shape,
        scratch_shapes=scratch, name="prompt_layer",
        compiler_params=pltpu.CompilerParams(
            dimension_semantics=("arbitrary", "arbitrary"), vmem_limit_bytes=VMEM_LIMIT),
    )(x, mem, w["norm_w"], w["w_in"], w["w_out"], w["w_mem_kv"],
      consts["cos_p"], consts["s1_p"], consts["s2_p"],
      consts["dec"], consts["qdec"], consts["kdec"], consts["cdec"],
      w["ret_gn"], w["ml_gn"], w["b_i8"], w["b_f8"],
      w["tab"], w["bt"], w["ct"], w["s5_d"], w["w_glu"], w["final_norm_w"])


SROWS = 16


def _sample_kernel_t(x_ref, normw_ref, win_ref, wout_ref, cos_ref, s1_ref, s2_ref, gam_ref, gam8_ref,
                     retgn_ref, mlgn_ref, bi_ref, bf_ref,
                     abt_ref, bt_ref, ct_ref, dsk_ref, wglu_ref, fnw_ref,
                     m0_ref, n0_ref, x0re_ref, x0im_ref,
                     rets_ref, mlc_ref, ck_ref, cv_ref,
                     y_ref, retn_ref, mlcn_ref, mlnn_ref, mlmn_ref, s5re_ref, s5im_ref,
                     hs_ref, proj_ref, qgt_ref, rkt_ref, rvt_ref, mqt_ref, mkt_ref, mvt_ref,
                     wi_ref, ws_ref, ot_ref, cqt_ref, qa8_ref, xa_ref,
                     *, n_layers, n_blocks):
    layer = pl.program_id(0)
    g = pl.program_id(1)
    nsamp = x_ref.shape[0]
    hm = _head_masks()
    ones_bd = _ones_matrix()
    avg = _avg_matrix()
    mask8 = (lax.broadcasted_iota(jnp.int32, (8, GW), 0)
             == lax.broadcasted_iota(jnp.int32, (8, GW), 1) // DH)

    def P(blk):
        return proj_ref[:, blk * GW:(blk + 1) * GW]

    @pl.when((layer == 0) & (g == 0))
    def _load_x():
        hs_ref[...] = x_ref[...]

    @pl.when(g == 0)
    def _pre():
        hn = _rms_norm(hs_ref[...], normw_ref[...]).astype(BF16)
        proj_ref[...] = _dot(hn, win_ref[...])
        cosr, s1, s2 = cos_ref[...], s1_ref[...], s2_ref[...]
        qgt_ref[...] = (_rope(P(RQ), cosr, s1, s2) * gam_ref[...]).T
        rkt_ref[...] = _rope(P(RK), cosr, s1, s2).T
        rvt_ref[...] = P(RV).T
        mqt_ref[...] = P(MQ).T
        mkt_ref[...] = P(MK).T
        mvt_ref[...] = P(MV).T
        gt = P(GT)
        ig = gt[:, 0:128].T[0:8] + bi_ref[...]
        lf = _log_sigmoid(gt[:, 128:256].T[0:8] + bf_ref[...])
        a = lf + m0_ref[...]
        mt = jnp.maximum(a, ig)
        wi_ref[...] = jnp.exp(ig - mt)
        ws_ref[...] = jnp.exp(a - mt)
        mlmn_ref[...] = mt
        ot_ref[...] = jnp.zeros_like(ot_ref)
        qa8_ref[...] = jnp.where(mask8[None], P(AQ)[:, None, :], 0.0).reshape(8 * nsamp, GW)
        but = _dot_tn(bt_ref[...], P(SU).T.astype(BF16))
        are, aim = abt_ref[0:NS, :], abt_ref[NS:2 * NS, :]
        x0r, x0i = x0re_ref[...], x0im_ref[...]
        s5re_ref[...] = are * x0r - aim * x0i + but[0:NS]
        s5im_ref[...] = are * x0i + aim * x0r + but[NS:2 * NS]

    head = g // (GW // SROWS // NH)
    hrow = pl.ds(pl.multiple_of(head * DH, DH), DH)
    gam_row = gam8_ref[pl.ds(head, 1), :]
    ws_row = ws_ref[pl.ds(head, 1), :]
    wi_row = wi_ref[pl.ds(head, 1), :]
    v_slab = rvt_ref[hrow, :]
    q_slab = mqt_ref[hrow, :]
    k_slab = mkt_ref[hrow, :]
    o_acc = jnp.zeros((DH, nsamp), F32)
    for i in range(SROWS):
        r = pl.ds(g * SROWS + i, 1)
        s_t = rets_ref[i]
        o_acc = o_acc + qgt_ref[r, :] * s_t
        retn_ref[i] = gam_row * s_t + rkt_ref[r, :] * v_slab
        c_t = mlc_ref[i]
        cqt_ref[r, :] = jnp.sum(c_t * q_slab, axis=0, keepdims=True)
        mlcn_ref[i] = ws_row * c_t + (wi_row * mvt_ref[r, :]) * k_slab
    ot_ref[hrow, :] = ot_ref[hrow, :] + o_acc

    tiles = [pl.ds(pl.multiple_of((g * SB + i) * 8, 8), 8) for i in range(SB)]
    q8 = [qa8_ref[tiles[i], :].astype(BF16) for i in range(SB)]
    sc = [_dot(q8[i], ck_ref[i].astype(BF16)) for i in range(SB)]
    ps = []
    for i in range(SB):
        e = jnp.exp(sc[i] - jnp.max(sc[i], axis=-1, keepdims=True))
        ps.append((e / jnp.sum(e, axis=-1, keepdims=True)).astype(BF16))
    ta = [_dot_nt(ps[i], cv_ref[i].astype(BF16)) for i in range(SB)]
    xa_ref[pl.ds(pl.multiple_of(g * SB, SB), SB), :] = jnp.concatenate(
        [jnp.sum(jnp.where(mask8, ta[i], 0.0), axis=0, keepdims=True) for i in range(SB)], axis=0)

    @pl.when(g == n_blocks - 1)
    def _post():
        cosr, s1, s2 = cos_ref[...], s1_ref[...], s2_ref[...]
        rq = _rope(P(RQ), cosr, s1, s2)
        rk = _rope(P(RK), cosr, s1, s2)
        ro = _dot_x2(rq * rk, ones_bd) * P(RV) + ot_ref[...].T
        ret_out = _head_norm(ro, retgn_ref[...], avg) * _silu(P(RG))
        mht = []
        for h in range(NH):
            rows = slice(h * DH, (h + 1) * DH)
            wi = wi_ref[h:h + 1, :]
            ws = ws_ref[h:h + 1, :]
            emt = jnp.exp(-mlmn_ref[h:h + 1, :])
            qt, kt, n0 = mqt_ref[rows, :], mkt_ref[rows, :], n0_ref[rows, :]
            s = jnp.sum(qt * kt, axis=0, keepdims=True) * wi
            den = s + ws * jnp.sum(n0 * qt, axis=0, keepdims=True)
            mht.append((s * mvt_ref[rows, :] + ws * cqt_ref[rows, :])
                       / jnp.maximum(jnp.abs(den), emt))
            mlnn_ref[rows, :] = ws * n0 + wi * kt
        mh = jnp.concatenate(mht, axis=0).T * _sigmoid(P(MO))
        ml_out = _head_norm(mh, mlgn_ref[...], avg) * _silu(P(MG))
        su = P(SU)
        xcat = jnp.concatenate([s5re_ref[...], s5im_ref[...]], axis=0).astype(BF16)
        sy = _dot_tn(xcat, ct_ref[...]) + dsk_ref[...] * su
        sy = _gelu_tanh(sy)
        sy = sy * _sigmoid(_dot(sy.astype(BF16), wglu_ref[...]))
        s5_out = sy * _silu(P(SG))
        xa_out = xa_ref[...] * _silu(P(AG))
        mix = jnp.concatenate([ret_out, ml_out, s5_out, xa_out], axis=1).astype(BF16)
        y = hs_ref[...] + _dot(mix, wout_ref[...])
        hs_ref[...] = y

        @pl.when(layer == n_layers - 1)
        def _emit():
            y_ref[...] = _rms_norm(y, fnw_ref[...])


def _sample_layers_t(x, st, w, consts):
    nsamp = x.shape[0]
    depth = w["w_in"].shape[0]
    nb = GW // SROWS
    assert nsamp == nb * SB and nsamp == 128
    once = lambda shape: pl.BlockSpec(shape, lambda l, g: (0,) * len(shape),
                                      pipeline_mode=pl.Buffered(1))
    lyr = lambda shape: pl.BlockSpec((None,) + shape, lambda l, g: (l,) + (0,) * len(shape),
                                     pipeline_mode=pl.Buffered(1))
    lyr_out = lambda shape: pl.BlockSpec((None,) + shape, lambda l, g: (l,) + (0,) * len(shape))
    srows = pl.BlockSpec((None, SROWS, DH, nsamp), lambda l, g: (l, g, 0, 0))
    cache = pl.BlockSpec((None, SB, GW, MEM), lambda l, g: (l, g, 0, 0))
    in_specs = [
        once((nsamp, D)), lyr((1, D)), lyr((D, DP)), lyr((D, D)),
        once((1, 128)), once((1, 128)), once((1, 128)), once((1, GW)), once((8, nsamp)),
        lyr((1, GW)), lyr((1, GW)), lyr((8, nsamp)), lyr((8, nsamp)),
        lyr((2 * NS, nsamp)), lyr((GW, 2 * NS)), lyr((2 * NS, GW)),
        lyr((1, GW)), lyr((GW, GW)), once((1, D)),
        lyr((8, nsamp)), lyr((GW, nsamp)), lyr((NS, nsamp)), lyr((NS, nsamp)),
        srows, srows, cache, cache,
    ]
    out_shape = (
        jax.ShapeDtypeStruct((nsamp, D), F32),
        jax.ShapeDtypeStruct((depth, GW, DH, nsamp), F32),
        jax.ShapeDtypeStruct((depth, GW, DH, nsamp), F32),
        jax.ShapeDtypeStruct((depth, GW, nsamp), F32),
        jax.ShapeDtypeStruct((depth, 8, nsamp), F32),
        jax.ShapeDtypeStruct((depth, NS, nsamp), F32),
        jax.ShapeDtypeStruct((depth, NS, nsamp), F32),
    )
    out_specs = (
        pl.BlockSpec((nsamp, D), lambda l, g: (0, 0)), srows, srows,
        lyr_out((GW, nsamp)), lyr_out((8, nsamp)), lyr_out((NS, nsamp)), lyr_out((NS, nsamp)),
    )
    scratch = [
        pltpu.VMEM((nsamp, D), F32),
        pltpu.VMEM((nsamp, DP), F32),
        pltpu.VMEM((GW, nsamp), F32), pltpu.VMEM((GW, nsamp), F32), pltpu.VMEM((GW, nsamp), F32),
        pltpu.VMEM((GW, nsamp), F32), pltpu.VMEM((GW, nsamp), F32), pltpu.VMEM((GW, nsamp), F32),
        pltpu.VMEM((8, nsamp), F32), pltpu.VMEM((8, nsamp), F32),
        pltpu.VMEM((GW, nsamp), F32), pltpu.VMEM((GW, nsamp), F32),
        pltpu.VMEM((8 * nsamp, GW), F32), pltpu.VMEM((nsamp, GW), F32),
    ]
    kern = functools.partial(_sample_kernel_t, n_layers=depth, n_blocks=nb)
    return pl.pallas_call(
        kern, grid=(depth, nb), in_specs=in_specs, out_specs=out_specs, out_shape=out_shape,
        scratch_shapes=scratch, name="sample_layers",
        compiler_params=pltpu.CompilerParams(
            dimension_semantics=("arbitrary", "arbitrary"), vmem_limit_bytes=VMEM_LIMIT),
    )(x, w["norm_w"], w["w_in"], w["w_out"],
      consts["cos_s"], consts["s1_s"], consts["s2_s"], consts["gam"], consts["gam8"],
      w["ret_gn"], w["ml_gn"], w["b_i8"], w["b_f8"],
      w["abt"], w["bt"], w["ct"], w["s5_d"], w["w_glu"], w["final_norm_w"],
      st["m"], st["n"], st["s5_re"], st["s5_im"], st["ret"], st["c"], st["mem_k"], st["mem_v"])


def _rope_tables(pos):
    half = DH // 2
    inv = ROPE_BASE ** (-jnp.arange(half, dtype=F32) / half)
    ang = pos.astype(F32)[:, None] * inv[None, :]
    cos, sin = jnp.cos(ang), jnp.sin(ang)
    zero = jnp.zeros_like(sin)
    c = jnp.tile(jnp.concatenate([cos, cos], axis=-1), (1, 2))
    s1 = jnp.tile(jnp.concatenate([zero, sin], axis=-1), (1, 2))
    s2 = jnp.tile(jnp.concatenate([-sin, zero], axis=-1), (1, 2))
    return c, s1, s2


def _constants(seq):
    lg = jnp.log1p(-jnp.power(2.0, -5.0 - jnp.arange(NH, dtype=F32)))[:, None]
    idx = jnp.arange(L, dtype=F32)
    diff = idx[:, None] - idx[None, :]
    decay = jnp.where(diff >= 0, jnp.exp(lg[:, :, None] * jnp.maximum(diff, 0.0)), 0.0)
    rep = lambda t: jnp.repeat(t, DH, axis=0).T
    consts = {
        "dec": decay.reshape(NH * L, L),
        "qdec": rep(jnp.exp(lg * (idx + 1.0))),
        "kdec": rep(jnp.exp(lg * (L - 1.0 - idx))),
        "cdec": rep(jnp.exp(lg * L)),
        "gam": rep(jnp.exp(lg * 1.0)),
    }
    consts["gam8"] = jnp.pad(jnp.broadcast_to(jnp.exp(lg), (NH, 128)), ((0, 8 - NH), (0, 0)))
    consts["cos_p"], consts["s1_p"], consts["s2_p"] = _rope_tables(jnp.arange(seq, dtype=jnp.int32))
    consts["cos_s"], consts["s1_s"], consts["s2_s"] = _rope_tables(
        PAST_LEN + jnp.arange(1, dtype=jnp.int32))
    return consts


def _pack_w_in(w_in):
    sizes = (GW,) * 9 + (NH, NH) + (GW,) * 4
    offs = np.concatenate([[0], np.cumsum(sizes)])
    seg = [w_in[:, :, int(offs[i]):int(offs[i + 1])] for i in range(len(sizes))]
    scale = DH ** -0.5
    pad = lambda t: jnp.pad(t, ((0, 0), (0, 0), (0, 128 - t.shape[-1])))
    blocks = [seg[0], seg[1] * scale, seg[2], seg[3],
              seg[4], seg[5] * scale, seg[6], seg[7], seg[8],
              seg[11], seg[12], seg[13] * scale, seg[14],
              pad(seg[9]), pad(seg[10])]
    return jnp.concatenate(blocks, axis=-1).astype(BF16)


def kernel(x_prompt, x_sample, mem_prompt, state_ret, state_mlstm_c, state_mlstm_n, state_mlstm_m,
           state_s5_re, state_s5_im, cache_mem_k, cache_mem_v,
           norm_w, w_in, ret_gn, ml_b_i, ml_b_f, ml_gn,
           s5_a_re, s5_a_im, s5_log_dt, s5_b_re, s5_b_im, s5_c_re, s5_c_im, s5_d, s5_w_glu,
           w_mem_k, w_mem_v, w_out, final_norm_w):
    depth = norm_w.shape[0]
    bp, seq, _ = x_prompt.shape
    bs = x_sample.shape[0]
    consts = _constants(seq)
    abt, bt, ct, tab = _s5_prepare(s5_a_re, s5_a_im, s5_log_dt, s5_b_re, s5_b_im, s5_c_re, s5_c_im)
    rows8 = lambda t: jnp.pad(jnp.broadcast_to(t[:, :, None], (depth, NH, TB)),
                              ((0, 0), (0, 8 - NH), (0, 0)))
    w = {
        "norm_w": norm_w[:, None], "w_in": _pack_w_in(w_in), "w_out": w_out.astype(BF16),
        "w_mem_kv": jnp.concatenate([w_mem_k, w_mem_v], axis=-1).astype(BF16),
        "ret_gn": ret_gn[:, None], "ml_gn": ml_gn[:, None],
        "b_i8": rows8(ml_b_i), "b_f8": rows8(ml_b_f),
        "abt": abt, "tab": tab, "bt": bt, "ct": ct,
        "s5_d": s5_d[:, None], "w_glu": s5_w_glu.astype(BF16),
        "final_norm_w": final_norm_w[None],
    }

    hp = x_prompt
    outs_p = [[] for _ in range(8)]
    for l in range(depth):
        hp, *sp = _prompt_layer(l, hp, mem_prompt, w, consts, l == depth - 1)
        for i in range(8):
            outs_p[i].append(sp[i])
    rs, cs, ns, ms, xr, xi, mk, mv = (jnp.stack(o) for o in outs_p)
    diag = lambda t: jnp.stack([t[:, :, h * DH:(h + 1) * DH, h * DH:(h + 1) * DH]
                                for h in range(NH)], axis=2)
    ret_p = diag(rs)
    mlc_p = jnp.swapaxes(diag(cs), -1, -2)
    mln_p = jnp.stack([ns[:, :, h * DH:(h + 1) * DH, h] for h in range(NH)], axis=2)
    mlm_p = ms[:, :, :NH, 0]
    s5re_p = xr[:, :, 0].reshape(depth, bp, S5G, S5P)
    s5im_p = xi[:, :, 0].reshape(depth, bp, S5G, S5P)
    memk_p = mk.reshape(depth, bp, MEM, NH, DH)
    memv_p = mv.reshape(depth, bp, MEM, NH, DH)

    st = {
        "m": jnp.pad(jnp.swapaxes(state_mlstm_m, 1, 2), ((0, 0), (0, 8 - NH), (0, 0))),
        "n": jnp.transpose(state_mlstm_n, (0, 2, 3, 1)).reshape(depth, GW, bs),
        "s5_re": jnp.transpose(state_s5_re, (0, 2, 3, 1)).reshape(depth, NS, bs),
        "s5_im": jnp.transpose(state_s5_im, (0, 2, 3, 1)).reshape(depth, NS, bs),
        "ret": jnp.transpose(state_ret, (0, 2, 3, 4, 1)).reshape(depth, GW, DH, bs),
        "c": jnp.transpose(state_mlstm_c, (0, 2, 3, 4, 1)).reshape(depth, GW, DH, bs),
        "mem_k": jnp.transpose(cache_mem_k, (0, 1, 3, 4, 2)).reshape(depth, bs, GW, MEM),
        "mem_v": jnp.transpose(cache_mem_v, (0, 1, 3, 4, 2)).reshape(depth, bs, GW, MEM),
    }
    hs, rn, cn, nn, mn, sr, si = _sample_layers_t(x_sample.reshape(bs, D), st, w, consts)
    back5 = lambda t: jnp.transpose(t.reshape(depth, NH, DH, DH, bs), (0, 4, 1, 2, 3))
    back4 = lambda t, a, b: jnp.transpose(t.reshape(depth, a, b, bs), (0, 3, 1, 2))
    return (hp, hs.reshape(bs, 1, D),
            ret_p, back5(rn), mlc_p, back5(cn),
            mln_p, back4(nn, NH, DH), mlm_p, jnp.swapaxes(mn[:, :NH], 1, 2),
            s5re_p, back4(sr, S5G, S5P), s5im_p, back4(si, S5G, S5P),
            memk_p, memv_p)
```

```python
import functools
import math

import numpy as np
import jax
import jax.numpy as jnp
from jax import lax
from jax.experimental import pallas as pl
from jax.experimental.pallas import tpu as pltpu

F32 = jnp.float32
BF16 = jnp.bfloat16

D = 1024
GW = 256
NH = 4
DH = 64
L = 128
LS = 8
S5G = 16
S5P = 64
S5C = 16
NS = S5G * S5P
MEM = 256
EPS = 1e-6
NEG_INF = -1e30
ROPE_BASE = 10000.0
PAST_LEN = 16384

TB = 512
SB = 8
XQ = 256
NBLK = 14
DP = NBLK * GW
(RQ, RK, RV, RG, MQ, MK, MV, MO, MG, SU, SG, AQ, AG, GT) = range(NBLK)

VMEM_LIMIT = 56 * 1024 * 1024


_DONE = object()


def _dot(a, b):
    return jnp.dot(a, b, preferred_element_type=F32)


def _dot_nt(a, b):
    return lax.dot_general(a, b, (((1,), (1,)), ((), ())), preferred_element_type=F32)


def _dot_tn(a, b):
    return lax.dot_general(a, b, (((0,), (0,)), ((), ())), preferred_element_type=F32)


def _split2(x):
    hi = x.astype(BF16)
    lo = (x - hi.astype(F32)).astype(BF16)
    return hi, lo


def _dot_x2(x, w):
    hi, lo = _split2(x)
    return _dot(hi, w) + _dot(lo, w)


def _dot_x3(x, w):
    hi = x.astype(BF16)
    r = x - hi.astype(F32)
    mid = r.astype(BF16)
    lo = (r - mid.astype(F32)).astype(BF16)
    return _dot(hi, w) + _dot(mid, w) + _dot(lo, w)


def _sigmoid(x):
    return 0.5 * (1.0 + jnp.tanh(0.5 * x))


def _silu(x):
    return x * _sigmoid(x)


def _log_sigmoid(x):
    return jnp.minimum(x, 0.0) - jnp.log1p(jnp.exp(-jnp.abs(x)))


def _gelu_tanh(x):
    c = math.sqrt(2.0 / math.pi)
    return x * (0.5 * (1.0 + jnp.tanh(c * (x + 0.044715 * (x * x * x)))))


def _lane_head(n):
    return lax.broadcasted_iota(jnp.int32, (1, n), 1) // DH


def _head_masks():
    lh = _lane_head(GW)
    return [lh == h for h in range(NH)]


def _block_diag_mask():
    r = lax.broadcasted_iota(jnp.int32, (GW, GW), 0) // DH
    c = lax.broadcasted_iota(jnp.int32, (GW, GW), 1) // DH
    return r == c


def _avg_matrix():
    return jnp.where(_block_diag_mask(), 1.0 / DH, 0.0).astype(BF16)


def _ones_matrix():
    return jnp.where(_block_diag_mask(), 1.0, 0.0).astype(BF16)


def _rope(x, cos, s1, s2):
    outs = []
    for j in range(2):
        xs = x[:, j * 128:(j + 1) * 128]
        outs.append(xs * cos + pltpu.roll(xs, 32, 1) * s1 + pltpu.roll(xs, 96, 1) * s2)
    return jnp.concatenate(outs, axis=1)


def _head_norm(x, gain, avg):
    mu = _dot_x2(x, avg)
    d = x - mu
    var = _dot_x2(d * d, avg)
    return d * lax.rsqrt(var + EPS) * gain


def _rms_norm(x, w):
    ms = jnp.mean(x * x, axis=-1, keepdims=True)
    return x * lax.rsqrt(ms + EPS) * w


def _stack_heads(x, hm):
    return jnp.concatenate([jnp.where(hm[h], x, 0.0) for h in range(NH)], axis=0)


def _s5_prep_kernel(are_ref, aim_ref, ldt_ref, bre_ref, bim_ref, cre_ref, cim_ref,
                    abt_ref, bt_ref, ct_ref, tab_ref):
    a_re = are_ref[0]
    a_im = aim_ref[0]
    dt = jnp.exp(ldt_ref[0])
    lam_re = a_re * dt
    lam_im = a_im * dt
    mag = jnp.exp(lam_re)
    ab_re = mag * jnp.cos(lam_im)
    ab_im = mag * jnp.sin(lam_im)
    den = a_re * a_re + a_im * a_im
    nr = ab_re - 1.0
    ni = ab_im
    f_re = (nr * a_re + ni * a_im) / den
    f_im = (ni * a_re - nr * a_im) / den
    abt_ref[0, 0:NS, :] = jnp.broadcast_to(ab_re, (128, NS)).T
    abt_ref[0, NS:2 * NS, :] = jnp.broadcast_to(ab_im, (128, NS)).T
    b_re = bre_ref[0]
    b_im = bim_ref[0]
    bb_re = (f_re * b_re - f_im * b_im).astype(BF16)
    bb_im = (f_re * b_im + f_im * b_re).astype(BF16)
    rep_r = (lax.broadcasted_iota(jnp.int32, (GW, S5C), 0) % S5C
             == lax.broadcasted_iota(jnp.int32, (GW, S5C), 1)).astype(BF16)
    in_blk = (lax.broadcasted_iota(jnp.int32, (GW, NS), 0) // S5C
              == lax.broadcasted_iota(jnp.int32, (GW, NS), 1) // S5P)
    bt_ref[0, :, :NS] = jnp.where(in_blk, _dot(rep_r, bb_re), 0.0).astype(BF16)
    bt_ref[0, :, NS:] = jnp.where(in_blk, _dot(rep_r, bb_im), 0.0).astype(BF16)
    rep_c = (lax.broadcasted_iota(jnp.int32, (S5C, GW), 0)
             == lax.broadcasted_iota(jnp.int32, (S5C, GW), 1) % S5C).astype(BF16)
    out_blk = (lax.broadcasted_iota(jnp.int32, (NS, GW), 0) // S5P
               == lax.broadcasted_iota(jnp.int32, (NS, GW), 1) // S5C)
    ct_ref[0, :NS, :] = jnp.where(out_blk, _dot(cre_ref[0].astype(BF16), rep_c), 0.0).astype(BF16)
    ct_ref[0, NS:, :] = jnp.where(out_blk, -_dot(cim_ref[0].astype(BF16), rep_c), 0.0).astype(BF16)
    i = lax.broadcasted_iota(jnp.int32, (LS, NS), 0).astype(F32)
    for slot, k in ((0, -i), (2, i), (4, i + 1.0)):
        pmag = jnp.exp(k * lam_re)
        tab_ref[0, slot] = pmag * jnp.cos(k * lam_im)
        tab_ref[0, slot + 1] = pmag * jnp.sin(k * lam_im)


def _s5_prepare(a_re, a_im, log_dt, b_re, b_im, c_re, c_im):
    depth = a_re.shape[0]
    are = a_re.reshape(depth, 1, NS)
    aim = a_im.reshape(depth, 1, NS)
    ldt = jnp.repeat(log_dt, S5P, axis=-1).reshape(depth, 1, NS)
    bre = jnp.transpose(b_re, (0, 3, 1, 2)).reshape(depth, S5C, NS)
    bim = jnp.transpose(b_im, (0, 3, 1, 2)).reshape(depth, S5C, NS)
    cre = jnp.transpose(c_re, (0, 1, 3, 2)).reshape(depth, NS, S5C)
    cim = jnp.transpose(c_im, (0, 1, 3, 2)).reshape(depth, NS, S5C)
    per_layer = lambda *shape: pl.BlockSpec((1,) + shape, lambda l: (l,) + (0,) * len(shape))
    out_shape = (jax.ShapeDtypeStruct((depth, 2 * NS, 128), F32),
                 jax.ShapeDtypeStruct((depth, GW, 2 * NS), BF16),
                 jax.ShapeDtypeStruct((depth, 2 * NS, GW), BF16),
                 jax.ShapeDtypeStruct((depth, 6, LS, NS), F32))
    return pl.pallas_call(
        _s5_prep_kernel, grid=(depth,),
        in_specs=[per_layer(1, NS), per_layer(1, NS), per_layer(1, NS),
                  per_layer(S5C, NS), per_layer(S5C, NS), per_layer(NS, S5C), per_layer(NS, S5C)],
        out_specs=(per_layer(2 * NS, 128), per_layer(GW, 2 * NS), per_layer(2 * NS, GW),
                   per_layer(6, LS, NS)),
        out_shape=out_shape, name="s5_prepare",
        compiler_params=pltpu.CompilerParams(dimension_semantics=("arbitrary",)),
    )(are, aim, ldt, bre, bim, cre, cim)


def _prompt_kernel(x_ref, mem_ref, normw_ref, win_ref, wout_ref, wmkv_ref,
                   cos_ref, s1_ref, s2_ref, dec_ref, qdec_ref, kdec_ref, cdec_ref,
                   retgn_ref, mlgn_ref, bi_ref, bf_ref,
                   tab_ref, bt_ref, ct_ref, dsk_ref, wglu_ref, fnw_ref,
                   y_ref, rets_ref, mlc_ref, mln_ref, mlm_ref, s5re_ref, s5im_ref,
                   memk_ref, memv_ref,
                   proj_ref, mix_ref, s_ref, c_ref, n_ref, m_ref, xre_ref, xim_ref,
                   mk_ref, mv_ref, bu_ref, xcat_ref, car_ref, *, last_layer, n_tblocks):
    t = pl.program_id(1)
    hm = _head_masks()
    bd = _block_diag_mask()
    avg = _avg_matrix()
    lane128 = lax.broadcasted_iota(jnp.int32, (1, 128), 1)
    row_i = lax.broadcasted_iota(jnp.int32, (L, 128), 0)
    col_i = lax.broadcasted_iota(jnp.int32, (L, 128), 1)
    causal = row_i >= col_i
    tri_u = jnp.where(row_i <= col_i, 1.0, 0.0).astype(BF16)
    tri_sub = jnp.where(causal & (row_i // LS == col_i // LS), 1.0, 0.0)
    blk_sum = jnp.where(lax.broadcasted_iota(jnp.int32, (L // LS, 128), 0)
                        == lax.broadcasted_iota(jnp.int32, (L // LS, 128), 1) // LS, 1.0, 0.0)
    tri_ext = jnp.concatenate([tri_sub, blk_sum], axis=0).astype(BF16)

    @pl.when(t == 0)
    def _init():
        s_ref[...] = jnp.zeros_like(s_ref)
        c_ref[...] = jnp.zeros_like(c_ref)
        n_ref[...] = jnp.zeros_like(n_ref)
        m_ref[...] = jnp.zeros_like(m_ref)
        xre_ref[...] = jnp.zeros_like(xre_ref)
        xim_ref[...] = jnp.zeros_like(xim_ref)
        mkv = _dot(mem_ref[0].astype(BF16), wmkv_ref[...])
        mk = mkv[:, :GW]
        mv = mkv[:, GW:]
        memk_ref[0] = mk
        memv_ref[0] = mv
        mk_ref[...] = _stack_heads(mk, hm).astype(BF16)
        mv_ref[...] = _stack_heads(mv, hm).astype(BF16)

    x = x_ref[0]
    hn = _rms_norm(x, normw_ref[...]).astype(BF16)
    proj_ref[...] = _dot_nt(hn, win_ref[...])

    nc = TB // L
    crow = [slice(c * L, (c + 1) * L) for c in range(nc)]

    def PB(blk, rows=slice(None)):
        return proj_ref[rows, blk * GW:(blk + 1) * GW]


    def xattn_stages():
        for piece in range(TB // XQ):
            rs = slice(piece * XQ, (piece + 1) * XQ)
            sc = _dot_nt(PB(AQ, rs).astype(BF16), mk_ref[...])
            yield
            ps = []
            for h in range(NH):
                seg = sc[:, h * MEM:(h + 1) * MEM]
                e = jnp.exp(seg - jnp.max(seg, axis=-1, keepdims=True))
                ps.append((e / jnp.sum(e, axis=-1, keepdims=True)).astype(BF16))
            p = jnp.concatenate(ps, axis=1)
            yield
            xa = _dot(p, mv_ref[...])
            yield
            mix_ref[rs, 3 * GW:4 * GW] = (xa * _silu(PB(AG, rs))).astype(BF16)
            yield

    def s5_stages():
        lanes = [(slice(j * 128, (j + 1) * 128), slice(NS + j * 128, NS + (j + 1) * 128))
                 for j in range(NS // 128)]
        bpc = L // LS
        nblk = TB // LS
        su = PB(SU)
        bu_ref[...] = _dot(su.astype(BF16), bt_ref[...])
        yield
        for c in range(nc):
            for lre, lim in lanes:
                br = bu_ref[crow[c], lre]
                bi = bu_ref[crow[c], lim]
                wr = jnp.tile(tab_ref[0, :, lre], (bpc, 1))
                wi = jnp.tile(tab_ref[1, :, lre], (bpc, 1))
                xcat_ref[crow[c], lre] = (wr * br - wi * bi).astype(BF16)
                xcat_ref[crow[c], lim] = (wr * bi + wi * br).astype(BF16)
            yield
        for c in range(nc):
            z = _dot(tri_ext, xcat_ref[crow[c], :])
            bu_ref[crow[c], :] = z[0:L]
            car_ref[c * bpc:(c + 1) * bpc, :] = z[L:L + bpc]
            yield
        rowb = lax.broadcasted_iota(jnp.int32, (nblk, 128), 0)
        ers, eis, prs, pis, c0s = [], [], [], [], []
        for lre, lim in lanes:
            zr = car_ref[:, lre]
            zi = car_ref[:, lim]
            pr = tab_ref[2, LS - 1:LS, lre]
            pi = tab_ref[3, LS - 1:LS, lre]
            er = pr * zr - pi * zi
            ei = pr * zi + pi * zr
            pr = tab_ref[4, LS - 1:LS, lre]
            pi = tab_ref[5, LS - 1:LS, lre]
            c0r = xre_ref[0:1, lre]
            c0i = xim_ref[0:1, lre]
            ers.append(er + jnp.where(rowb == 0, pr * c0r - pi * c0i, 0.0))
            eis.append(ei + jnp.where(rowb == 0, pr * c0i + pi * c0r, 0.0))
            prs.append(pr)
            pis.append(pi)
            c0s.append((c0r, c0i))
        yield
        for k in range(nblk.bit_length() - 1):
            d = 1 << k
            for j in range(len(lanes)):
                er, ei, pr, pi = ers[j], eis[j], prs[j], pis[j]
                sr = jnp.where(rowb >= d, pltpu.roll(er, d, 0), 0.0)
                si = jnp.where(rowb >= d, pltpu.roll(ei, d, 0), 0.0)
                ers[j], eis[j] = er + pr * sr - pi * si, ei + pr * si + pi * sr
                prs[j], pis[j] = pr * pr - pi * pi, 2.0 * (pr * pi)
            yield
        for j, (lre, lim) in enumerate(lanes):
            xre_ref[:, lre] = jnp.broadcast_to(ers[j][nblk - 1:nblk, :], (8, 128))
            xim_ref[:, lre] = jnp.broadcast_to(eis[j][nblk - 1:nblk, :], (8, 128))
            cr = jnp.where(rowb == 0, c0s[j][0], pltpu.roll(ers[j], 1, 0))
            ci_ = jnp.where(rowb == 0, c0s[j][1], pltpu.roll(eis[j], 1, 0))
            ar = tab_ref[4, 0:1, lre]
            ai = tab_ref[5, 0:1, lre]
            car_ref[:, lre] = ar * cr - ai * ci_
            car_ref[:, lim] = ar * ci_ + ai * cr
        yield
        for c in range(nc):
            for lre, lim in lanes:
                cr = jnp.concatenate(
                    [jnp.broadcast_to(car_ref[c * bpc + j:c * bpc + j + 1, lre], (LS, 128))
                     for j in range(bpc)], axis=0)
                ci_ = jnp.concatenate(
                    [jnp.broadcast_to(car_ref[c * bpc + j:c * bpc + j + 1, lim], (LS, 128))
                     for j in range(bpc)], axis=0)
                zr = bu_ref[crow[c], lre] + cr
                zi = bu_ref[crow[c], lim] + ci_
                pr = jnp.tile(tab_ref[2, :, lre], (bpc, 1))
                pi = jnp.tile(tab_ref[3, :, lre], (bpc, 1))
                xcat_ref[crow[c], lre] = (pr * zr - pi * zi).astype(BF16)
                xcat_ref[crow[c], lim] = (pr * zi + pi * zr).astype(BF16)
            yield
        sy = _dot(xcat_ref[...], ct_ref[...]) + dsk_ref[...] * su
        yield
        sy = _gelu_tanh(sy)
        gate = _dot(sy.astype(BF16), wglu_ref[...])
        yield
        mix_ref[:, 2 * GW:3 * GW] = (sy * _sigmoid(gate) * _silu(PB(SG))).astype(BF16)
        yield

    def retention_stages():
        rq = _rope(PB(RQ), cos_ref[...], s1_ref[...], s2_ref[...])
        rk = _rope(PB(RK), cos_ref[...], s1_ref[...], s2_ref[...])
        rvf = PB(RV)
        rv = rvf.astype(BF16)
        rqb = rq.astype(BF16)
        kst = [_stack_heads(rk[crow[c]], hm).astype(BF16) for c in range(nc)]
        vst = [_stack_heads(rvf[crow[c]], hm).astype(BF16) for c in range(nc)]
        yield
        inner = [_dot_nt(rqb[crow[c]], kst[c]) for c in range(nc)]
        yield
        pmat = [(inner[c] * dec_ref[...]).astype(BF16) for c in range(nc)]
        kd = [(rk[crow[c]] * kdec_ref[...]).astype(BF16) for c in range(nc)]
        yield
        rloc = [_dot(pmat[c], vst[c]) for c in range(nc)]
        upd = [_dot_tn(kd[c], rv[crow[c]]) for c in range(nc)]
        yield
        st = [s_ref[...]]
        for c in range(nc):
            st.append(st[c] * cdec_ref[...] + jnp.where(bd, upd[c], 0.0))
        s_ref[...] = st[nc]
        qd = [(rq[crow[c]] * qdec_ref[...]).astype(BF16) for c in range(nc)]
        yield
        ost = [_dot(qd[c], st[c].astype(BF16)) for c in range(nc)]
        yield
        ro = jnp.concatenate([rloc[c] + ost[c] for c in range(nc)], axis=0)
        mix_ref[:, 0:GW] = (_head_norm(ro, retgn_ref[...], avg) * _silu(PB(RG))).astype(BF16)
        yield

    def mlstm_stages():
        g = PB(GT)
        ig_r = jnp.concatenate([g[crow[c], 0:128].T[0:8] for c in range(nc)], axis=1) + bi_ref[...]
        lf_r = _log_sigmoid(
            jnp.concatenate([g[crow[c], 128:256].T[0:8] for c in range(nc)], axis=1) + bf_ref[...])
        yield
        b_r = [_dot_x3(lf_r[:, crow[c]], tri_u) for c in range(nc)]
        yield
        gd_r = [ig_r[:, crow[c]] - b_r[c] for c in range(nc)]
        lane_r = lax.broadcasted_iota(jnp.int32, (8, L), 1)
        cm_r = list(gd_r)
        for k in range(7):
            d = 1 << k
            cm_r = [jnp.maximum(v, jnp.where(lane_r >= d, pltpu.roll(v, d, 1), NEG_INF))
                    for v in cm_r]
            yield
        m_prev = [m_ref[...]]
        mt_r = []
        for c in range(nc):
            mt = jnp.maximum(b_r[c] + m_prev[c], b_r[c] + cm_r[c])
            mt_r.append(mt)
            m_prev.append(jnp.broadcast_to(mt[:, L - 1:L], (8, L)))
            yield
        m_ref[...] = m_prev[nc]
        cols = []
        for c in range(nc):
            bm = b_r[c] - mt_r[c]
            ws = jnp.exp(b_r[c] + m_prev[c] - mt_r[c])
            wl = jnp.exp(gd_r[c] + jnp.broadcast_to(bm[:, L - 1:L], (8, L)))
            emt = jnp.exp(-mt_r[c])
            cols.append(jnp.concatenate([bm, ws, wl, emt, jnp.zeros((L - 32, L), F32)], axis=0).T)
        mqf = PB(MQ)
        mq = mqf.astype(BF16)
        mkf = PB(MK)
        mvf = PB(MV)
        mv_ = mvf.astype(BF16)
        kst = [_stack_heads(mkf[crow[c]], hm).astype(BF16) for c in range(nc)]
        vst = [_stack_heads(mvf[crow[c]], hm).astype(BF16) for c in range(nc)]
        yield
        sraw = [_dot_nt(mq[crow[c]], kst[c]) for c in range(nc)]
        yield
        smat, den_i = [], []
        for c in range(nc):
            parts, dens = [], []
            for h in range(NH):
                arg = jnp.where(causal, cols[c][:, h:h + 1] + gd_r[c][h:h + 1, :], NEG_INF)
                s_h = sraw[c][:, h * L:(h + 1) * L] * jnp.exp(arg)
                dens.append(jnp.sum(s_h, axis=-1, keepdims=True))
                parts.append(s_h.astype(BF16))
            den_i.append(dens)
            smat.append(jnp.concatenate(parts, axis=1))
            yield
        ones_blk = jnp.ones((L, 128), BF16)
        kws = []
        for c in range(nc):
            kw = jnp.zeros((L, GW), F32)
            for h in range(NH):
                kw = kw + jnp.where(hm[h], mkf[crow[c]] * cols[c][:, 16 + h:17 + h], 0.0)
            kws.append(kw.astype(BF16))
        yield
        rloc = [_dot(smat[c], vst[c]) for c in range(nc)]
        u = [_dot_tn(kws[c], jnp.concatenate([mv_[crow[c]], ones_blk], axis=1))
             for c in range(nc)]
        yield
        nmask = (lax.broadcasted_iota(jnp.int32, (GW, 128), 0) // DH
                 == lax.broadcasted_iota(jnp.int32, (GW, 128), 1))
        cst = [c_ref[...]]
        nst = [n_ref[...]]
        for c in range(nc):
            wsl256 = jnp.zeros((1, GW), F32)
            wsl128 = jnp.zeros((1, 128), F32)
            for h in range(NH):
                wsl = cols[c][L - 1:L, 8 + h:9 + h]
                wsl256 = wsl256 + jnp.where(hm[h], wsl, 0.0)
                wsl128 = wsl128 + jnp.where(lane128 == h, wsl, 0.0)
            cst.append(cst[c] * wsl256 + jnp.where(bd, u[c][:, :GW], 0.0))
            nst.append(nst[c] * wsl128 + jnp.where(nmask, u[c][:, GW:], 0.0))
        c_ref[...] = cst[nc]
        n_ref[...] = nst[nc]
        yield
        qc = [_dot(mq[crow[c]], cst[c].astype(BF16)) for c in range(nc)]
        qn = [_dot(mq[crow[c]], nst[c].astype(BF16)) for c in range(nc)]
        yield
        mhs = []
        for c in range(nc):
            mh = jnp.zeros((L, GW), F32)
            for h in range(NH):
                ws = cols[c][:, 8 + h:9 + h]
                den = den_i[c][h] + ws * qn[c][:, h:h + 1]
                dn = jnp.maximum(jnp.abs(den), cols[c][:, 24 + h:25 + h])
                mh = mh + jnp.where(hm[h], (rloc[c] + ws * qc[c]) / dn, 0.0)
            mhs.append(mh)
            yield
        mh = jnp.concatenate(mhs, axis=0) * _sigmoid(PB(MO))
        mix_ref[:, GW:2 * GW] = (_head_norm(mh, mlgn_ref[...], avg) * _silu(PB(MG))).astype(BF16)
        yield

    pending = [mlstm_stages(), s5_stages(), retention_stages(), xattn_stages()]
    while pending:
        pending = [stage for stage in pending if next(stage, _DONE) is not _DONE]

    y = x_ref[0] + _dot(mix_ref[...], wout_ref[...])
    if last_layer:
        y = _rms_norm(y, fnw_ref[...])
    y_ref[0] = y

    @pl.when(t == n_tblocks - 1)
    def _final():
        rets_ref[0] = s_ref[...]
        mlc_ref[0] = c_ref[...]
        mln_ref[0] = n_ref[...]
        mlm_ref[0] = m_ref[...]
        s5re_ref[0] = xre_ref[...]
        s5im_ref[0] = xim_ref[...]


def _prompt_layer(layer, x, mem, w, consts, last_layer):
    bsz, seq, _ = x.shape
    nt = seq // TB
    full = lambda shape: pl.BlockSpec(shape, lambda b, t: (0,) * len(shape),
                                      pipeline_mode=pl.Buffered(1))
    lyr = lambda shape: pl.BlockSpec((None,) + shape, lambda b, t: (layer,) + (0,) * len(shape),
                                     pipeline_mode=pl.Buffered(1))
    tok = lambda width: pl.BlockSpec((TB, width), lambda b, t: (t, 0))
    per_b = lambda r, c: pl.BlockSpec((1, r, c), lambda b, t: (b, 0, 0))
    in_specs = [
        pl.BlockSpec((1, TB, D), lambda b, t: (b, t, 0)),
        per_b(MEM, D),
        lyr((1, D)), lyr((DP, D)), lyr((D, D)), lyr((D, 2 * GW)),
        tok(128), tok(128), tok(128),
        full((L, NH * L)), full((L, GW)), full((L, GW)), full((1, GW)),
        lyr((1, GW)), lyr((1, GW)), lyr((8, TB)), lyr((8, TB)),
        lyr((6, LS, NS)), lyr((GW, 2 * NS)), lyr((2 * NS, GW)),
        lyr((1, GW)), lyr((GW, GW)), full((1, D)),
    ]
    out_shape = (
        jax.ShapeDtypeStruct((bsz, seq, D), F32),
        jax.ShapeDtypeStruct((bsz, GW, GW), F32),
        jax.ShapeDtypeStruct((bsz, GW, GW), F32),
        jax.ShapeDtypeStruct((bsz, GW, 128), F32),
        jax.ShapeDtypeStruct((bsz, 8, 128), F32),
        jax.ShapeDtypeStruct((bsz, 8, NS), F32),
        jax.ShapeDtypeStruct((bsz, 8, NS), F32),
        jax.ShapeDtypeStruct((bsz, MEM, GW), F32),
        jax.ShapeDtypeStruct((bsz, MEM, GW), F32),
    )
    out_specs = (
        pl.BlockSpec((1, TB, D), lambda b, t: (b, t, 0)),
        per_b(GW, GW), per_b(GW, GW), per_b(GW, 128), per_b(8, 128),
        per_b(8, NS), per_b(8, NS), per_b(MEM, GW), per_b(MEM, GW),
    )
    scratch = [
        pltpu.VMEM((TB, DP), F32),
        pltpu.VMEM((TB, D), BF16),
        pltpu.VMEM((GW, GW), F32),
        pltpu.VMEM((GW, GW), F32),
        pltpu.VMEM((GW, 128), F32),
        pltpu.VMEM((8, 128), F32),
        pltpu.VMEM((8, NS), F32),
        pltpu.VMEM((8, NS), F32),
        pltpu.VMEM((NH * MEM, GW), BF16),
        pltpu.VMEM((NH * MEM, GW), BF16),
        pltpu.VMEM((TB, 2 * NS), F32),
        pltpu.VMEM((TB, 2 * NS), BF16),
        pltpu.VMEM((TB // LS, 2 * NS), F32),
    ]
    kern = functools.partial(_prompt_kernel, last_layer=last_layer, n_tblocks=nt)
    return pl.pallas_call(
        kern, grid=(bsz, nt), in_specs=in_specs, out_specs=out_specs, out_shape=out_shape,
        scratch_shapes=scratch, name="prompt_layer",
        compiler_params=pltpu.CompilerParams(
            dimension_semantics=("arbitrary", "arbitrary"), vmem_limit_bytes=VMEM_LIMIT),
    )(x, mem, w["norm_w"], w["w_in"], w["w_out"], w["w_mem_kv"],
      consts["cos_p"], consts["s1_p"], consts["s2_p"],
      consts["dec"], consts["qdec"], consts["kdec"], consts["cdec"],
      w["ret_gn"], w["ml_gn"], w["b_i8"], w["b_f8"],
      w["tab"], w["bt"], w["ct"], w["s5_d"], w["w_glu"], w["final_norm_w"])


SROWS = 16


def _sample_kernel_t(x_ref, normw_ref, win_ref, wout_ref, cos_ref, s1_ref, s2_ref, gam_ref, gam8_ref,
                     retgn_ref, mlgn_ref, bi_ref, bf_ref,
                     abt_ref, bt_ref, ct_ref, dsk_ref, wglu_ref, fnw_ref,
                     m0_ref, n0_ref, x0re_ref, x0im_ref,
                     rets_ref, mlc_ref, ck_ref, cv_ref,
                     y_ref, retn_ref, mlcn_ref, mlnn_ref, mlmn_ref, s5re_ref, s5im_ref,
                     hs_ref, proj_ref, qgt_ref, rkt_ref, rvt_ref, mqt_ref, mkt_ref, mvt_ref,
                     wi_ref, ws_ref, ot_ref, cqt_ref, qa8_ref, xa_ref,
                     *, n_layers, n_blocks):
    layer = pl.program_id(0)
    g = pl.program_id(1)
    nsamp = x_ref.shape[0]
    hm = _head_masks()
    ones_bd = _ones_matrix()
    avg = _avg_matrix()
    mask8 = (lax.broadcasted_iota(jnp.int32, (8, GW), 0)
             == lax.broadcasted_iota(jnp.int32, (8, GW), 1) // DH)

    def P(blk):
        return proj_ref[:, blk * GW:(blk + 1) * GW]

    @pl.when((layer == 0) & (g == 0))
    def _load_x():
        hs_ref[...] = x_ref[...]

    @pl.when(g == 0)
    def _pre():
        hn = _rms_norm(hs_ref[...], normw_ref[...]).astype(BF16)
        proj_ref[...] = _dot_nt(hn, win_ref[...])
        cosr, s1, s2 = cos_ref[...], s1_ref[...], s2_ref[...]
        qgt_ref[...] = (_rope(P(RQ), cosr, s1, s2) * gam_ref[...]).T
        rkt_ref[...] = _rope(P(RK), cosr, s1, s2).T
        rvt_ref[...] = P(RV).T
        mqt_ref[...] = P(MQ).T
        mkt_ref[...] = P(MK).T
        mvt_ref[...] = P(MV).T
        gt = P(GT)
        ig = gt[:, 0:128].T[0:8] + bi_ref[...]
        lf = _log_sigmoid(gt[:, 128:256].T[0:8] + bf_ref[...])
        a = lf + m0_ref[...]
        mt = jnp.maximum(a, ig)
        wi_ref[...] = jnp.exp(ig - mt)
        ws_ref[...] = jnp.exp(a - mt)
        mlmn_ref[...] = mt
        ot_ref[...] = jnp.zeros_like(ot_ref)
        qa8_ref[...] = jnp.where(mask8[None], P(AQ)[:, None, :], 0.0).reshape(8 * nsamp, GW)
        but = _dot_tn(bt_ref[...], P(SU).T.astype(BF16))
        are, aim = abt_ref[0:NS, :], abt_ref[NS:2 * NS, :]
        x0r, x0i = x0re_ref[...], x0im_ref[...]
        s5re_ref[...] = are * x0r - aim * x0i + but[0:NS]
        s5im_ref[...] = are * x0i + aim * x0r + but[NS:2 * NS]

    head = g // (GW // SROWS // NH)
    hrow = pl.ds(pl.multiple_of(head * DH, DH), DH)
    gam_row = gam8_ref[pl.ds(head, 1), :]
    ws_row = ws_ref[pl.ds(head, 1), :]
    wi_row = wi_ref[pl.ds(head, 1), :]
    v_slab = rvt_ref[hrow, :]
    q_slab = mqt_ref[hrow, :]
    k_slab = mkt_ref[hrow, :]
    o_acc = jnp.zeros((DH, nsamp), F32)
    for i in range(SROWS):
        r = pl.ds(g * SROWS + i, 1)
        s_t = rets_ref[i]
        o_acc = o_acc + qgt_ref[r, :] * s_t
        retn_ref[i] = gam_row * s_t + rkt_ref[r, :] * v_slab
        c_t = mlc_ref[i]
        cqt_ref[r, :] = jnp.sum(c_t * q_slab, axis=0, keepdims=True)
        mlcn_ref[i] = ws_row * c_t + (wi_row * mvt_ref[r, :]) * k_slab
    ot_ref[hrow, :] = ot_ref[hrow, :] + o_acc

    tiles = [pl.ds(pl.multiple_of((g * SB + i) * 8, 8), 8) for i in range(SB)]
    q8 = [qa8_ref[tiles[i], :].astype(BF16) for i in range(SB)]
    sc = [_dot(q8[i], ck_ref[i].astype(BF16)) for i in range(SB)]
    ps = []
    for i in range(SB):
        e = jnp.exp(sc[i] - jnp.max(sc[i], axis=-1, keepdims=True))
        ps.append((e / jnp.sum(e, axis=-1, keepdims=True)).astype(BF16))
    ta = [_dot_nt(ps[i], cv_ref[i].astype(BF16)) for i in range(SB)]
    xa_ref[pl.ds(pl.multiple_of(g * SB, SB), SB), :] = jnp.concatenate(
        [jnp.sum(jnp.where(mask8, ta[i], 0.0), axis=0, keepdims=True) for i in range(SB)], axis=0)

    @pl.when(g == n_blocks - 1)
    def _post():
        cosr, s1, s2 = cos_ref[...], s1_ref[...], s2_ref[...]
        rq = _rope(P(RQ), cosr, s1, s2)
        rk = _rope(P(RK), cosr, s1, s2)
        ro = _dot_x2(rq * rk, ones_bd) * P(RV) + ot_ref[...].T
        ret_out = _head_norm(ro, retgn_ref[...], avg) * _silu(P(RG))
        mht = []
        for h in range(NH):
            rows = slice(h * DH, (h + 1) * DH)
            wi = wi_ref[h:h + 1, :]
            ws = ws_ref[h:h + 1, :]
            emt = jnp.exp(-mlmn_ref[h:h + 1, :])
            qt, kt, n0 = mqt_ref[rows, :], mkt_ref[rows, :], n0_ref[rows, :]
            s = jnp.sum(qt * kt, axis=0, keepdims=True) * wi
            den = s + ws * jnp.sum(n0 * qt, axis=0, keepdims=True)
            mht.append((s * mvt_ref[rows, :] + ws * cqt_ref[rows, :])
                       / jnp.maximum(jnp.abs(den), emt))
            mlnn_ref[rows, :] = ws * n0 + wi * kt
        mh = jnp.concatenate(mht, axis=0).T * _sigmoid(P(MO))
        ml_out = _head_norm(mh, mlgn_ref[...], avg) * _silu(P(MG))
        su = P(SU)
        xcat = jnp.concatenate([s5re_ref[...], s5im_ref[...]], axis=0).astype(BF16)
        sy = _dot_tn(xcat, ct_ref[...]) + dsk_ref[...] * su
        sy = _gelu_tanh(sy)
        sy = sy * _sigmoid(_dot(sy.astype(BF16), wglu_ref[...]))
        s5_out = sy * _silu(P(SG))
        xa_out = xa_ref[...] * _silu(P(AG))
        mix = jnp.concatenate([ret_out, ml_out, s5_out, xa_out], axis=1).astype(BF16)
        y = hs_ref[...] + _dot(mix, wout_ref[...])
        hs_ref[...] = y

        @pl.when(layer == n_layers - 1)
        def _emit():
            y_ref[...] = _rms_norm(y, fnw_ref[...])


def _sample_layers_t(x, st, w, consts):
    nsamp = x.shape[0]
    depth = w["w_in"].shape[0]
    nb = GW // SROWS
    assert nsamp == nb * SB and nsamp == 128
    once = lambda shape: pl.BlockSpec(shape, lambda l, g: (0,) * len(shape),
                                      pipeline_mode=pl.Buffered(1))
    lyr = lambda shape: pl.BlockSpec((None,) + shape, lambda l, g: (l,) + (0,) * len(shape),
                                     pipeline_mode=pl.Buffered(1))
    lyr_out = lambda shape: pl.BlockSpec((None,) + shape, lambda l, g: (l,) + (0,) * len(shape))
    srows = pl.BlockSpec((None, SROWS, DH, nsamp), lambda l, g: (l, g, 0, 0))
    cache = pl.BlockSpec((None, SB, GW, MEM), lambda l, g: (l, g, 0, 0))
    in_specs = [
        once((nsamp, D)), lyr((1, D)), lyr((DP, D)), lyr((D, D)),
        once((1, 128)), once((1, 128)), once((1, 128)), once((1, GW)), once((8, nsamp)),
        lyr((1, GW)), lyr((1, GW)), lyr((8, nsamp)), lyr((8, nsamp)),
        lyr((2 * NS, nsamp)), lyr((GW, 2 * NS)), lyr((2 * NS, GW)),
        lyr((1, GW)), lyr((GW, GW)), once((1, D)),
        lyr((8, nsamp)), lyr((GW, nsamp)), lyr((NS, nsamp)), lyr((NS, nsamp)),
        srows, srows, cache, cache,
    ]
    out_shape = (
        jax.ShapeDtypeStruct((nsamp, D), F32),
        jax.ShapeDtypeStruct((depth, GW, DH, nsamp), F32),
        jax.ShapeDtypeStruct((depth, GW, DH, nsamp), F32),
        jax.ShapeDtypeStruct((depth, GW, nsamp), F32),
        jax.ShapeDtypeStruct((depth, 8, nsamp), F32),
        jax.ShapeDtypeStruct((depth, NS, nsamp), F32),
        jax.ShapeDtypeStruct((depth, NS, nsamp), F32),
    )
    out_specs = (
        pl.BlockSpec((nsamp, D), lambda l, g: (0, 0)), srows, srows,
        lyr_out((GW, nsamp)), lyr_out((8, nsamp)), lyr_out((NS, nsamp)), lyr_out((NS, nsamp)),
    )
    scratch = [
        pltpu.VMEM((nsamp, D), F32),
        pltpu.VMEM((nsamp, DP), F32),
        pltpu.VMEM((GW, nsamp), F32), pltpu.VMEM((GW, nsamp), F32), pltpu.VMEM((GW, nsamp), F32),
        pltpu.VMEM((GW, nsamp), F32), pltpu.VMEM((GW, nsamp), F32), pltpu.VMEM((GW, nsamp), F32),
        pltpu.VMEM((8, nsamp), F32), pltpu.VMEM((8, nsamp), F32),
        pltpu.VMEM((GW, nsamp), F32), pltpu.VMEM((GW, nsamp), F32),
        pltpu.VMEM((8 * nsamp, GW), F32), pltpu.VMEM((nsamp, GW), F32),
    ]
    kern = functools.partial(_sample_kernel_t, n_layers=depth, n_blocks=nb)
    return pl.pallas_call(
        kern, grid=(depth, nb), in_specs=in_specs, out_specs=out_specs, out_shape=out_shape,
        scratch_shapes=scratch, name="sample_layers",
        compiler_params=pltpu.CompilerParams(
            dimension_semantics=("arbitrary", "arbitrary"), vmem_limit_bytes=VMEM_LIMIT),
    )(x, w["norm_w"], w["w_in"], w["w_out"],
      consts["cos_s"], consts["s1_s"], consts["s2_s"], consts["gam"], consts["gam8"],
      w["ret_gn"], w["ml_gn"], w["b_i8"], w["b_f8"],
      w["abt"], w["bt"], w["ct"], w["s5_d"], w["w_glu"], w["final_norm_w"],
      st["m"], st["n"], st["s5_re"], st["s5_im"], st["ret"], st["c"], st["mem_k"], st["mem_v"])


def _rope_tables(pos):
    half = DH // 2
    inv = ROPE_BASE ** (-jnp.arange(half, dtype=F32) / half)
    ang = pos.astype(F32)[:, None] * inv[None, :]
    cos, sin = jnp.cos(ang), jnp.sin(ang)
    zero = jnp.zeros_like(sin)
    c = jnp.tile(jnp.concatenate([cos, cos], axis=-1), (1, 2))
    s1 = jnp.tile(jnp.concatenate([zero, sin], axis=-1), (1, 2))
    s2 = jnp.tile(jnp.concatenate([-sin, zero], axis=-1), (1, 2))
    return c, s1, s2


def _constants(seq):
    lg = jnp.log1p(-jnp.power(2.0, -5.0 - jnp.arange(NH, dtype=F32)))[:, None]
    idx = jnp.arange(L, dtype=F32)
    diff = idx[:, None] - idx[None, :]
    decay = jnp.where(diff >= 0, jnp.exp(lg[:, :, None] * jnp.maximum(diff, 0.0)), 0.0)
    rep = lambda t: jnp.repeat(t, DH, axis=0).T
    consts = {
        "dec": jnp.transpose(decay, (1, 0, 2)).reshape(L, NH * L),
        "qdec": rep(jnp.exp(lg * (idx + 1.0))),
        "kdec": rep(jnp.exp(lg * (L - 1.0 - idx))),
        "cdec": rep(jnp.exp(lg * L)),
        "gam": rep(jnp.exp(lg * 1.0)),
    }
    consts["gam8"] = jnp.pad(jnp.broadcast_to(jnp.exp(lg), (NH, 128)), ((0, 8 - NH), (0, 0)))
    consts["cos_p"], consts["s1_p"], consts["s2_p"] = _rope_tables(jnp.arange(seq, dtype=jnp.int32))
    consts["cos_s"], consts["s1_s"], consts["s2_s"] = _rope_tables(
        PAST_LEN + jnp.arange(1, dtype=jnp.int32))
    return consts


def _pack_w_in(w_in):
    wt = jnp.swapaxes(w_in, 1, 2)
    sizes = (GW,) * 9 + (NH, NH) + (GW,) * 4
    offs = np.concatenate([[0], np.cumsum(sizes)])
    seg = [wt[:, int(offs[i]):int(offs[i + 1]), :] for i in range(len(sizes))]
    scale = DH ** -0.5
    pad = lambda t: jnp.pad(t, ((0, 0), (0, 128 - t.shape[1]), (0, 0)))
    blocks = [seg[0], seg[1] * scale, seg[2], seg[3],
              seg[4], seg[5] * scale, seg[6], seg[7], seg[8],
              seg[11], seg[12], seg[13] * scale, seg[14],
              pad(seg[9]), pad(seg[10])]
    return jnp.concatenate(blocks, axis=1).astype(BF16)


def kernel(x_prompt, x_sample, mem_prompt, state_ret, state_mlstm_c, state_mlstm_n, state_mlstm_m,
           state_s5_re, state_s5_im, cache_mem_k, cache_mem_v,
           norm_w, w_in, ret_gn, ml_b_i, ml_b_f, ml_gn,
           s5_a_re, s5_a_im, s5_log_dt, s5_b_re, s5_b_im, s5_c_re, s5_c_im, s5_d, s5_w_glu,
           w_mem_k, w_mem_v, w_out, final_norm_w):
    depth = norm_w.shape[0]
    bp, seq, _ = x_prompt.shape
    bs = x_sample.shape[0]
    consts = _constants(seq)
    abt, bt, ct, tab = _s5_prepare(s5_a_re, s5_a_im, s5_log_dt, s5_b_re, s5_b_im, s5_c_re, s5_c_im)
    rows8 = lambda t: jnp.pad(jnp.broadcast_to(t[:, :, None], (depth, NH, TB)),
                              ((0, 0), (0, 8 - NH), (0, 0)))
    w = {
        "norm_w": norm_w[:, None], "w_in": _pack_w_in(w_in), "w_out": w_out.astype(BF16),
        "w_mem_kv": jnp.concatenate([w_mem_k, w_mem_v], axis=-1).astype(BF16),
        "ret_gn": ret_gn[:, None], "ml_gn": ml_gn[:, None],
        "b_i8": rows8(ml_b_i), "b_f8": rows8(ml_b_f),
        "abt": abt, "tab": tab, "bt": bt, "ct": ct,
        "s5_d": s5_d[:, None], "w_glu": s5_w_glu.astype(BF16),
        "final_norm_w": final_norm_w[None],
    }

    hp = x_prompt
    outs_p = [[] for _ in range(8)]
    for l in range(depth):
        hp, *sp = _prompt_layer(l, hp, mem_prompt, w, consts, l == depth - 1)
        for i in range(8):
            outs_p[i].append(sp[i])
    rs, cs, ns, ms, xr, xi, mk, mv = (jnp.stack(o) for o in outs_p)
    diag = lambda t: jnp.stack([t[:, :, h * DH:(h + 1) * DH, h * DH:(h + 1) * DH]
                                for h in range(NH)], axis=2)
    ret_p = diag(rs)
    mlc_p = jnp.swapaxes(diag(cs), -1, -2)
    mln_p = jnp.stack([ns[:, :, h * DH:(h + 1) * DH, h] for h in range(NH)], axis=2)
    mlm_p = ms[:, :, :NH, 0]
    s5re_p = xr[:, :, 0].reshape(depth, bp, S5G, S5P)
    s5im_p = xi[:, :, 0].reshape(depth, bp, S5G, S5P)
    memk_p = mk.reshape(depth, bp, MEM, NH, DH)
    memv_p = mv.reshape(depth, bp, MEM, NH, DH)

    st = {
        "m": jnp.pad(jnp.swapaxes(state_mlstm_m, 1, 2), ((0, 0), (0, 8 - NH), (0, 0))),
        "n": jnp.transpose(state_mlstm_n, (0, 2, 3, 1)).reshape(depth, GW, bs),
        "s5_re": jnp.transpose(state_s5_re, (0, 2, 3, 1)).reshape(depth, NS, bs),
        "s5_im": jnp.transpose(state_s5_im, (0, 2, 3, 1)).reshape(depth, NS, bs),
        "ret": jnp.transpose(state_ret, (0, 2, 3, 4, 1)).reshape(depth, GW, DH, bs),
        "c": jnp.transpose(state_mlstm_c, (0, 2, 3, 4, 1)).reshape(depth, GW, DH, bs),
        "mem_k": jnp.transpose(cache_mem_k, (0, 1, 3, 4, 2)).reshape(depth, bs, GW, MEM),
        "mem_v": jnp.transpose(cache_mem_v, (0, 1, 3, 4, 2)).reshape(depth, bs, GW, MEM),
    }
    hs, rn, cn, nn, mn, sr, si = _sample_layers_t(x_sample.reshape(bs, D), st, w, consts)
    back5 = lambda t: jnp.transpose(t.reshape(depth, NH, DH, DH, bs), (0, 4, 1, 2, 3))
    back4 = lambda t, a, b: jnp.transpose(t.reshape(depth, a, b, bs), (0, 3, 1, 2))
    return (hp, hs.reshape(bs, 1, D),
            ret_p, back5(rn), mlc_p, back5(cn),
            mln_p, back4(nn, NH, DH), mlm_p, jnp.swapaxes(mn[:, :NH], 1, 2),
            s5re_p, back4(sr, S5G, S5P), s5im_p, back4(si, S5G, S5P),
            memk_p, memv_p)
```

```python
import functools
import math

import numpy as np
import jax
import jax.numpy as jnp
from jax import lax
from jax.experimental import pallas as pl
from jax.experimental.pallas import tpu as pltpu

F32 = jnp.float32
BF16 = jnp.bfloat16

D = 1024
GW = 256
NH = 4
DH = 64
L = 128
LS = 8
S5G = 16
S5P = 64
S5C = 16
NS = S5G * S5P
MEM = 256
EPS = 1e-6
NEG_INF = -1e30
ROPE_BASE = 10000.0
PAST_LEN = 16384

TB = 512
SB = 8
XQ = 512
NBLK = 14
DP = NBLK * GW
(GT, SU, RQ, RK, RV, RG, MQ, MK, MV, MO, MG, SG, AQ, AG) = range(NBLK)

VMEM_LIMIT = 56 * 1024 * 1024


_DONE = object()


def _dot(a, b):
    return jnp.dot(a, b, preferred_element_type=F32)


def _dot_nt(a, b):
    return lax.dot_general(a, b, (((1,), (1,)), ((), ())), preferred_element_type=F32)


def _dot_tn(a, b):
    return lax.dot_general(a, b, (((0,), (0,)), ((), ())), preferred_element_type=F32)


def _split2(x):
    hi = x.astype(BF16)
    lo = (x - hi.astype(F32)).astype(BF16)
    return hi, lo


def _dot_x2(x, w):
    hi, lo = _split2(x)
    return _dot(hi, w) + _dot(lo, w)


def _dot_x3(x, w):
    hi = x.astype(BF16)
    r = x - hi.astype(F32)
    mid = r.astype(BF16)
    lo = (r - mid.astype(F32)).astype(BF16)
    return _dot(hi, w) + _dot(mid, w) + _dot(lo, w)


def _sigmoid(x):
    return 0.5 * (1.0 + jnp.tanh(0.5 * x))


def _silu(x):
    return x * _sigmoid(x)


def _log_sigmoid(x):
    return jnp.minimum(x, 0.0) - jnp.log1p(jnp.exp(-jnp.abs(x)))


def _gelu_tanh(x):
    c = math.sqrt(2.0 / math.pi)
    return x * (0.5 * (1.0 + jnp.tanh(c * (x + 0.044715 * (x * x * x)))))


def _lane_head(n):
    return lax.broadcasted_iota(jnp.int32, (1, n), 1) // DH


def _head_masks():
    lh = _lane_head(GW)
    return [lh == h for h in range(NH)]


def _block_diag_mask():
    r = lax.broadcasted_iota(jnp.int32, (GW, GW), 0) // DH
    c = lax.broadcasted_iota(jnp.int32, (GW, GW), 1) // DH
    return r == c


def _avg_matrix():
    return jnp.where(_block_diag_mask(), 1.0 / DH, 0.0).astype(BF16)


def _ones_matrix():
    return jnp.where(_block_diag_mask(), 1.0, 0.0).astype(BF16)


def _rope(x, cos, s1, s2):
    outs = []
    for j in range(2):
        xs = x[:, j * 128:(j + 1) * 128]
        outs.append(xs * cos + pltpu.roll(xs, 32, 1) * s1 + pltpu.roll(xs, 96, 1) * s2)
    return jnp.concatenate(outs, axis=1)


def _head_norm(x, gain, avg):
    mu = _dot_x2(x, avg)
    d = x - mu
    var = _dot_x2(d * d, avg)
    return d * lax.rsqrt(var + EPS) * gain


def _rms_norm(x, w):
    ms = jnp.mean(x * x, axis=-1, keepdims=True)
    return x * lax.rsqrt(ms + EPS) * w


def _stack_heads(x, hm):
    return jnp.concatenate([jnp.where(hm[h], x, 0.0) for h in range(NH)], axis=0)


def _s5_prep_kernel(are_ref, aim_ref, ldt_ref, bre_ref, bim_ref, cre_ref, cim_ref,
                    abt_ref, bt_ref, ct_ref, tab_ref):
    a_re = are_ref[0]
    a_im = aim_ref[0]
    dt = jnp.exp(ldt_ref[0])
    lam_re = a_re * dt
    lam_im = a_im * dt
    mag = jnp.exp(lam_re)
    ab_re = mag * jnp.cos(lam_im)
    ab_im = mag * jnp.sin(lam_im)
    den = a_re * a_re + a_im * a_im
    nr = ab_re - 1.0
    ni = ab_im
    f_re = (nr * a_re + ni * a_im) / den
    f_im = (ni * a_re - nr * a_im) / den
    abt_ref[0, 0:NS, :] = jnp.broadcast_to(ab_re, (128, NS)).T
    abt_ref[0, NS:2 * NS, :] = jnp.broadcast_to(ab_im, (128, NS)).T
    b_re = bre_ref[0]
    b_im = bim_ref[0]
    bb_re = (f_re * b_re - f_im * b_im).astype(BF16)
    bb_im = (f_re * b_im + f_im * b_re).astype(BF16)
    rep_r = (lax.broadcasted_iota(jnp.int32, (GW, S5C), 0) % S5C
             == lax.broadcasted_iota(jnp.int32, (GW, S5C), 1)).astype(BF16)
    in_blk = (lax.broadcasted_iota(jnp.int32, (GW, NS), 0) // S5C
              == lax.broadcasted_iota(jnp.int32, (GW, NS), 1) // S5P)
    bt_ref[0, :, :NS] = jnp.where(in_blk, _dot(rep_r, bb_re), 0.0).astype(BF16)
    bt_ref[0, :, NS:] = jnp.where(in_blk, _dot(rep_r, bb_im), 0.0).astype(BF16)
    rep_c = (lax.broadcasted_iota(jnp.int32, (S5C, GW), 0)
             == lax.broadcasted_iota(jnp.int32, (S5C, GW), 1) % S5C).astype(BF16)
    out_blk = (lax.broadcasted_iota(jnp.int32, (NS, GW), 0) // S5P
               == lax.broadcasted_iota(jnp.int32, (NS, GW), 1) // S5C)
    ct_ref[0, :NS, :] = jnp.where(out_blk, _dot(cre_ref[0].astype(BF16), rep_c), 0.0).astype(BF16)
    ct_ref[0, NS:, :] = jnp.where(out_blk, -_dot(cim_ref[0].astype(BF16), rep_c), 0.0).astype(BF16)
    i = lax.broadcasted_iota(jnp.int32, (LS, NS), 0).astype(F32)
    for slot, k in ((0, -i), (2, i), (4, i + 1.0)):
        pmag = jnp.exp(k * lam_re)
        tab_ref[0, slot] = pmag * jnp.cos(k * lam_im)
        tab_ref[0, slot + 1] = pmag * jnp.sin(k * lam_im)


def _s5_prepare(a_re, a_im, log_dt, b_re, b_im, c_re, c_im):
    depth = a_re.shape[0]
    are = a_re.reshape(depth, 1, NS)
    aim = a_im.reshape(depth, 1, NS)
    ldt = jnp.repeat(log_dt, S5P, axis=-1).reshape(depth, 1, NS)
    bre = jnp.transpose(b_re, (0, 3, 1, 2)).reshape(depth, S5C, NS)
    bim = jnp.transpose(b_im, (0, 3, 1, 2)).reshape(depth, S5C, NS)
    cre = jnp.transpose(c_re, (0, 1, 3, 2)).reshape(depth, NS, S5C)
    cim = jnp.transpose(c_im, (0, 1, 3, 2)).reshape(depth, NS, S5C)
    per_layer = lambda *shape: pl.BlockSpec((1,) + shape, lambda l: (l,) + (0,) * len(shape))
    out_shape = (jax.ShapeDtypeStruct((depth, 2 * NS, 128), F32),
                 jax.ShapeDtypeStruct((depth, GW, 2 * NS), BF16),
                 jax.ShapeDtypeStruct((depth, 2 * NS, GW), BF16),
                 jax.ShapeDtypeStruct((depth, 6, LS, NS), F32))
    return pl.pallas_call(
        _s5_prep_kernel, grid=(depth,),
        in_specs=[per_layer(1, NS), per_layer(1, NS), per_layer(1, NS),
                  per_layer(S5C, NS), per_layer(S5C, NS), per_layer(NS, S5C), per_layer(NS, S5C)],
        out_specs=(per_layer(2 * NS, 128), per_layer(GW, 2 * NS), per_layer(2 * NS, GW),
                   per_layer(6, LS, NS)),
        out_shape=out_shape, name="s5_prepare",
        compiler_params=pltpu.CompilerParams(dimension_semantics=("arbitrary",)),
    )(are, aim, ldt, bre, bim, cre, cim)


def _prompt_kernel(x_ref, mem_ref, normw_ref, win_ref, wout_ref, wmkv_ref,
                   cos_ref, s1_ref, s2_ref, dec_ref, qdec_ref, kdec_ref, cdec_ref,
                   retgn_ref, mlgn_ref, bi_ref, bf_ref,
                   tab_ref, bt_ref, ct_ref, dsk_ref, wglu_ref, fnw_ref,
                   y_ref, rets_ref, mlc_ref, mln_ref, mlm_ref, s5re_ref, s5im_ref,
                   memk_ref, memv_ref,
                   proj_ref, mix_ref, s_ref, c_ref, n_ref, m_ref, xre_ref, xim_ref,
                   mk_ref, mv_ref, bu_ref, xcat_ref, car_ref, *, last_layer, n_tblocks):
    t = pl.program_id(1)
    hm = _head_masks()
    bd = _block_diag_mask()
    avg = _avg_matrix()
    lane128 = lax.broadcasted_iota(jnp.int32, (1, 128), 1)
    row_i = lax.broadcasted_iota(jnp.int32, (L, 128), 0)
    col_i = lax.broadcasted_iota(jnp.int32, (L, 128), 1)
    causal = row_i >= col_i
    tri_u = jnp.where(row_i <= col_i, 1.0, 0.0).astype(BF16)
    tri_sub = jnp.where(causal & (row_i // LS == col_i // LS), 1.0, 0.0)
    blk_sum = jnp.where(lax.broadcasted_iota(jnp.int32, (L // LS, 128), 0)
                        == lax.broadcasted_iota(jnp.int32, (L // LS, 128), 1) // LS, 1.0, 0.0)
    tri_ext = jnp.concatenate([tri_sub, blk_sum], axis=0).astype(BF16)

    @pl.when(t == 0)
    def _init():
        s_ref[...] = jnp.zeros_like(s_ref)
        c_ref[...] = jnp.zeros_like(c_ref)
        n_ref[...] = jnp.zeros_like(n_ref)
        m_ref[...] = jnp.zeros_like(m_ref)
        xre_ref[...] = jnp.zeros_like(xre_ref)
        xim_ref[...] = jnp.zeros_like(xim_ref)
        mkv = _dot(mem_ref[0].astype(BF16), wmkv_ref[...])
        mk = mkv[:, :GW]
        mv = mkv[:, GW:]
        memk_ref[0] = mk
        memv_ref[0] = mv
        mk_ref[...] = _stack_heads(mk, hm).astype(BF16)
        mv_ref[...] = _stack_heads(mv, hm).astype(BF16)

    x = x_ref[0]
    hn = _rms_norm(x, normw_ref[...]).astype(BF16)
    proj_ref[...] = _dot_nt(hn, win_ref[...])

    nc = TB // L
    crow = [slice(c * L, (c + 1) * L) for c in range(nc)]

    def PB(blk, rows=slice(None)):
        return proj_ref[rows, blk * GW:(blk + 1) * GW]


    def xattn_stages():
        for piece in range(TB // XQ):
            rs = slice(piece * XQ, (piece + 1) * XQ)
            sc = _dot_nt(PB(AQ, rs).astype(BF16), mk_ref[...])
            yield
            ps = []
            for h in range(NH):
                seg = sc[:, h * MEM:(h + 1) * MEM]
                e = jnp.exp(seg - jnp.max(seg, axis=-1, keepdims=True))
                ps.append((e / jnp.sum(e, axis=-1, keepdims=True)).astype(BF16))
            p = jnp.concatenate(ps, axis=1)
            yield
            xa = _dot(p, mv_ref[...])
            yield
            mix_ref[rs, 3 * GW:4 * GW] = (xa * _silu(PB(AG, rs))).astype(BF16)
            yield

    def s5_stages():
        lanes = [(slice(j * 128, (j + 1) * 128), slice(NS + j * 128, NS + (j + 1) * 128))
                 for j in range(NS // 128)]
        bpc = L // LS
        nblk = TB // LS
        su = PB(SU)
        bu_ref[...] = _dot(su.astype(BF16), bt_ref[...])
        yield
        for c in range(nc):
            for lre, lim in lanes:
                br = bu_ref[crow[c], lre]
                bi = bu_ref[crow[c], lim]
                wr = jnp.tile(tab_ref[0, :, lre], (bpc, 1))
                wi = jnp.tile(tab_ref[1, :, lre], (bpc, 1))
                xcat_ref[crow[c], lre] = (wr * br - wi * bi).astype(BF16)
                xcat_ref[crow[c], lim] = (wr * bi + wi * br).astype(BF16)
            yield
        for c in range(nc):
            z = _dot(tri_ext, xcat_ref[crow[c], :])
            bu_ref[crow[c], :] = z[0:L]
            car_ref[c * bpc:(c + 1) * bpc, :] = z[L:L + bpc]
            yield
        rowb = lax.broadcasted_iota(jnp.int32, (nblk, 128), 0)
        ers, eis, prs, pis, c0s = [], [], [], [], []
        for lre, lim in lanes:
            zr = car_ref[:, lre]
            zi = car_ref[:, lim]
            pr = tab_ref[2, LS - 1:LS, lre]
            pi = tab_ref[3, LS - 1:LS, lre]
            er = pr * zr - pi * zi
            ei = pr * zi + pi * zr
            pr = tab_ref[4, LS - 1:LS, lre]
            pi = tab_ref[5, LS - 1:LS, lre]
            c0r = xre_ref[0:1, lre]
            c0i = xim_ref[0:1, lre]
            ers.append(er + jnp.where(rowb == 0, pr * c0r - pi * c0i, 0.0))
            eis.append(ei + jnp.where(rowb == 0, pr * c0i + pi * c0r, 0.0))
            prs.append(pr)
            pis.append(pi)
            c0s.append((c0r, c0i))
        yield
        for k in range(nblk.bit_length() - 1):
            d = 1 << k
            for j in range(len(lanes)):
                er, ei, pr, pi = ers[j], eis[j], prs[j], pis[j]
                sr = jnp.where(rowb >= d, pltpu.roll(er, d, 0), 0.0)
                si = jnp.where(rowb >= d, pltpu.roll(ei, d, 0), 0.0)
                ers[j], eis[j] = er + pr * sr - pi * si, ei + pr * si + pi * sr
                prs[j], pis[j] = pr * pr - pi * pi, 2.0 * (pr * pi)
            yield
        for j, (lre, lim) in enumerate(lanes):
            xre_ref[:, lre] = jnp.broadcast_to(ers[j][nblk - 1:nblk, :], (8, 128))
            xim_ref[:, lre] = jnp.broadcast_to(eis[j][nblk - 1:nblk, :], (8, 128))
            cr = jnp.where(rowb == 0, c0s[j][0], pltpu.roll(ers[j], 1, 0))
            ci_ = jnp.where(rowb == 0, c0s[j][1], pltpu.roll(eis[j], 1, 0))
            ar = tab_ref[4, 0:1, lre]
            ai = tab_ref[5, 0:1, lre]
            car_ref[:, lre] = ar * cr - ai * ci_
            car_ref[:, lim] = ar * ci_ + ai * cr
        yield
        for c in range(nc):
            for lre, lim in lanes:
                cr = jnp.concatenate(
                    [jnp.broadcast_to(car_ref[c * bpc + j:c * bpc + j + 1, lre], (LS, 128))
                     for j in range(bpc)], axis=0)
                ci_ = jnp.concatenate(
                    [jnp.broadcast_to(car_ref[c * bpc + j:c * bpc + j + 1, lim], (LS, 128))
                     for j in range(bpc)], axis=0)
                zr = bu_ref[crow[c], lre] + cr
                zi = bu_ref[crow[c], lim] + ci_
                pr = jnp.tile(tab_ref[2, :, lre], (bpc, 1))
                pi = jnp.tile(tab_ref[3, :, lre], (bpc, 1))
                xcat_ref[crow[c], lre] = (pr * zr - pi * zi).astype(BF16)
                xcat_ref[crow[c], lim] = (pr * zi + pi * zr).astype(BF16)
            yield
        sy = _dot(xcat_ref[...], ct_ref[...]) + dsk_ref[...] * su
        yield
        sy = _gelu_tanh(sy)
        gate = _dot(sy.astype(BF16), wglu_ref[...])
        yield
        mix_ref[:, 2 * GW:3 * GW] = (sy * _sigmoid(gate) * _silu(PB(SG))).astype(BF16)
        yield

    def retention_stages():
        rq = _rope(PB(RQ), cos_ref[...], s1_ref[...], s2_ref[...])
        rk = _rope(PB(RK), cos_ref[...], s1_ref[...], s2_ref[...])
        rvf = PB(RV)
        rv = rvf.astype(BF16)
        rqb = rq.astype(BF16)
        kst = [_stack_heads(rk[crow[c]], hm).astype(BF16) for c in range(nc)]
        vst = [_stack_heads(rvf[crow[c]], hm).astype(BF16) for c in range(nc)]
        yield
        inner = [_dot_nt(rqb[crow[c]], kst[c]) for c in range(nc)]
        yield
        pmat = [(inner[c] * dec_ref[...]).astype(BF16) for c in range(nc)]
        kd = [(rk[crow[c]] * kdec_ref[...]).astype(BF16) for c in range(nc)]
        yield
        rloc = [_dot(pmat[c], vst[c]) for c in range(nc)]
        upd = [_dot_tn(kd[c], rv[crow[c]]) for c in range(nc)]
        yield
        st = [s_ref[...]]
        for c in range(nc):
            st.append(st[c] * cdec_ref[...] + jnp.where(bd, upd[c], 0.0))
        s_ref[...] = st[nc]
        qd = [(rq[crow[c]] * qdec_ref[...]).astype(BF16) for c in range(nc)]
        yield
        ost = [_dot(qd[c], st[c].astype(BF16)) for c in range(nc)]
        yield
        ro = jnp.concatenate([rloc[c] + ost[c] for c in range(nc)], axis=0)
        mix_ref[:, 0:GW] = (_head_norm(ro, retgn_ref[...], avg) * _silu(PB(RG))).astype(BF16)
        yield

    def mlstm_stages():
        g = PB(GT)
        ig_r = jnp.concatenate([g[crow[c], 0:128].T[0:8] for c in range(nc)], axis=1) + bi_ref[...]
        lf_r = _log_sigmoid(
            jnp.concatenate([g[crow[c], 128:256].T[0:8] for c in range(nc)], axis=1) + bf_ref[...])
        yield
        b_r = [_dot_x3(lf_r[:, crow[c]], tri_u) for c in range(nc)]
        yield
        gd_r = [ig_r[:, crow[c]] - b_r[c] for c in range(nc)]
        lane_r = lax.broadcasted_iota(jnp.int32, (8, L), 1)
        cm_r = list(gd_r)
        for k in range(7):
            d = 1 << k
            cm_r = [jnp.maximum(v, jnp.where(lane_r >= d, pltpu.roll(v, d, 1), NEG_INF))
                    for v in cm_r]
            yield
        m_prev = [m_ref[...]]
        mt_r = []
        for c in range(nc):
            mt = jnp.maximum(b_r[c] + m_prev[c], b_r[c] + cm_r[c])
            mt_r.append(mt)
            m_prev.append(jnp.broadcast_to(mt[:, L - 1:L], (8, L)))
            yield
        m_ref[...] = m_prev[nc]
        cols = []
        for c in range(nc):
            bm = b_r[c] - mt_r[c]
            ws = jnp.exp(b_r[c] + m_prev[c] - mt_r[c])
            wl = jnp.exp(gd_r[c] + jnp.broadcast_to(bm[:, L - 1:L], (8, L)))
            emt = jnp.exp(-mt_r[c])
            cols.append(jnp.concatenate([bm, ws, wl, emt, jnp.zeros((L - 32, L), F32)], axis=0).T)
        mqf = PB(MQ)
        mq = mqf.astype(BF16)
        mkf = PB(MK)
        mvf = PB(MV)
        mv_ = mvf.astype(BF16)
        kst = [_stack_heads(mkf[crow[c]], hm).astype(BF16) for c in range(nc)]
        vst = [_stack_heads(mvf[crow[c]], hm).astype(BF16) for c in range(nc)]
        yield
        sraw = [_dot_nt(mq[crow[c]], kst[c]) for c in range(nc)]
        yield
        smat, den_i = [], []
        for c in range(nc):
            parts, dens = [], []
            for h in range(NH):
                arg = jnp.where(causal, cols[c][:, h:h + 1] + gd_r[c][h:h + 1, :], NEG_INF)
                s_h = sraw[c][:, h * L:(h + 1) * L] * jnp.exp(arg)
                dens.append(jnp.sum(s_h, axis=-1, keepdims=True))
                parts.append(s_h.astype(BF16))
            den_i.append(dens)
            smat.append(jnp.concatenate(parts, axis=1))
            yield
        ones_blk = jnp.ones((L, 128), BF16)
        kws = []
        for c in range(nc):
            kw = jnp.zeros((L, GW), F32)
            for h in range(NH):
                kw = kw + jnp.where(hm[h], mkf[crow[c]] * cols[c][:, 16 + h:17 + h], 0.0)
            kws.append(kw.astype(BF16))
        yield
        rloc = [_dot(smat[c], vst[c]) for c in range(nc)]
        u = [_dot_tn(kws[c], jnp.concatenate([mv_[crow[c]], ones_blk], axis=1))
             for c in range(nc)]
        yield
        nmask = (lax.broadcasted_iota(jnp.int32, (GW, 128), 0) // DH
                 == lax.broadcasted_iota(jnp.int32, (GW, 128), 1))
        cst = [c_ref[...]]
        nst = [n_ref[...]]
        for c in range(nc):
            wsl256 = jnp.zeros((1, GW), F32)
            wsl128 = jnp.zeros((1, 128), F32)
            for h in range(NH):
                wsl = cols[c][L - 1:L, 8 + h:9 + h]
                wsl256 = wsl256 + jnp.where(hm[h], wsl, 0.0)
                wsl128 = wsl128 + jnp.where(lane128 == h, wsl, 0.0)
            cst.append(cst[c] * wsl256 + jnp.where(bd, u[c][:, :GW], 0.0))
            nst.append(nst[c] * wsl128 + jnp.where(nmask, u[c][:, GW:], 0.0))
        c_ref[...] = cst[nc]
        n_ref[...] = nst[nc]
        yield
        qc = [_dot(mq[crow[c]], cst[c].astype(BF16)) for c in range(nc)]
        qn = [_dot(mq[crow[c]], nst[c].astype(BF16)) for c in range(nc)]
        yield
        mhs = []
        for c in range(nc):
            mh = jnp.zeros((L, GW), F32)
            for h in range(NH):
                ws = cols[c][:, 8 + h:9 + h]
                den = den_i[c][h] + ws * qn[c][:, h:h + 1]
                dn = jnp.maximum(jnp.abs(den), cols[c][:, 24 + h:25 + h])
                mh = mh + jnp.where(hm[h], (rloc[c] + ws * qc[c]) / dn, 0.0)
            mhs.append(mh)
            yield
        mh = jnp.concatenate(mhs, axis=0) * _sigmoid(PB(MO))
        mix_ref[:, GW:2 * GW] = (_head_norm(mh, mlgn_ref[...], avg) * _silu(PB(MG))).astype(BF16)
        yield

    pending = [mlstm_stages(), s5_stages(), retention_stages(), xattn_stages()]
    while pending:
        pending = [stage for stage in pending if next(stage, _DONE) is not _DONE]

    y = x_ref[0] + _dot(mix_ref[...], wout_ref[...])
    if last_layer:
        y = _rms_norm(y, fnw_ref[...])
    y_ref[0] = y

    @pl.when(t == n_tblocks - 1)
    def _final():
        rets_ref[0] = s_ref[...]
        mlc_ref[0] = c_ref[...]
        mln_ref[0] = n_ref[...]
        mlm_ref[0] = m_ref[...]
        s5re_ref[0] = xre_ref[...]
        s5im_ref[0] = xim_ref[...]


def _prompt_layer(layer, x, mem, w, consts, last_layer):
    bsz, seq, _ = x.shape
    nt = seq // TB
    full = lambda shape: pl.BlockSpec(shape, lambda b, t: (0,) * len(shape),
                                      pipeline_mode=pl.Buffered(1))
    lyr = lambda shape: pl.BlockSpec((None,) + shape, lambda b, t: (layer,) + (0,) * len(shape),
                                     pipeline_mode=pl.Buffered(1))
    tok = lambda width: pl.BlockSpec((TB, width), lambda b, t: (t, 0))
    per_b = lambda r, c: pl.BlockSpec((1, r, c), lambda b, t: (b, 0, 0))
    in_specs = [
        pl.BlockSpec((1, TB, D), lambda b, t: (b, t, 0)),
        per_b(MEM, D),
        lyr((1, D)), lyr((DP, D)), lyr((D, D)), lyr((D, 2 * GW)),
        tok(128), tok(128), tok(128),
        full((L, NH * L)), full((L, GW)), full((L, GW)), full((1, GW)),
        lyr((1, GW)), lyr((1, GW)), lyr((8, TB)), lyr((8, TB)),
        lyr((6, LS, NS)), lyr((GW, 2 * NS)), lyr((2 * NS, GW)),
        lyr((1, GW)), lyr((GW, GW)), full((1, D)),
    ]
    out_shape = (
        jax.ShapeDtypeStruct((bsz, seq, D), F32),
        jax.ShapeDtypeStruct((bsz, GW, GW), F32),
        jax.ShapeDtypeStruct((bsz, GW, GW), F32),
        jax.ShapeDtypeStruct((bsz, GW, 128), F32),
        jax.ShapeDtypeStruct((bsz, 8, 128), F32),
        jax.ShapeDtypeStruct((bsz, 8, NS), F32),
        jax.ShapeDtypeStruct((bsz, 8, NS), F32),
        jax.ShapeDtypeStruct((bsz, MEM, GW), F32),
        jax.ShapeDtypeStruct((bsz, MEM, GW), F32),
    )
    out_specs = (
        pl.BlockSpec((1, TB, D), lambda b, t: (b, t, 0)),
        per_b(GW, GW), per_b(GW, GW), per_b(GW, 128), per_b(8, 128),
        per_b(8, NS), per_b(8, NS), per_b(MEM, GW), per_b(MEM, GW),
    )
    scratch = [
        pltpu.VMEM((TB, DP), F32),
        pltpu.VMEM((TB, D), BF16),
        pltpu.VMEM((GW, GW), F32),
        pltpu.VMEM((GW, GW), F32),
        pltpu.VMEM((GW, 128), F32),
        pltpu.VMEM((8, 128), F32),
        pltpu.VMEM((8, NS), F32),
        pltpu.VMEM((8, NS), F32),
        pltpu.VMEM((NH * MEM, GW), BF16),
        pltpu.VMEM((NH * MEM, GW), BF16),
        pltpu.VMEM((TB, 2 * NS), F32),
        pltpu.VMEM((TB, 2 * NS), BF16),
        pltpu.VMEM((TB // LS, 2 * NS), F32),
    ]
    kern = functools.partial(_prompt_kernel, last_layer=last_layer, n_tblocks=nt)
    return pl.pallas_call(
        kern, grid=(bsz, nt), in_specs=in_specs, out_specs=out_specs, out_shape=out_shape,
        scratch_shapes=scratch, name="prompt_layer",
        compiler_params=pltpu.CompilerParams(
            dimension_semantics=("arbitrary", "arbitrary"), vmem_limit_bytes=VMEM_LIMIT),
    )(x, mem, w["norm_w"], w["w_in"], w["w_out"], w["w_mem_kv"],
      consts["cos_p"], consts["s1_p"], consts["s2_p"],
      consts["dec"], consts["qdec"], consts["kdec"], consts["cdec"],
      w["ret_gn"], w["ml_gn"], w["b_i8"], w["b_f8"],
      w["tab"], w["bt"], w["ct"], w["s5_d"], w["w_glu"], w["final_norm_w"])


SROWS = 16


def _sample_kernel_t(x_ref, normw_ref, win_ref, wout_ref, cos_ref, s1_ref, s2_ref, gam_ref, gam8_ref,
                     retgn_ref, mlgn_ref, bi_ref, bf_ref,
                     abt_ref, bt_ref, ct_ref, dsk_ref, wglu_ref, fnw_ref,
                     m0_ref, n0_ref, x0re_ref, x0im_ref,
                     rets_ref, mlc_ref, ck_ref, cv_ref,
                     y_ref, retn_ref, mlcn_ref, mlnn_ref, mlmn_ref, s5re_ref, s5im_ref,
                     hs_ref, proj_ref, qgt_ref, rkt_ref, rvt_ref, mqt_ref, mkt_ref, mvt_ref,
                     wi_ref, ws_ref, ot_ref, cqt_ref, qa8_ref, xa_ref,
                     *, n_layers, n_blocks):
    layer = pl.program_id(0)
    g = pl.program_id(1)
    nsamp = x_ref.shape[0]
    hm = _head_masks()
    ones_bd = _ones_matrix()
    avg = _avg_matrix()
    mask8 = (lax.broadcasted_iota(jnp.int32, (8, GW), 0)
             == lax.broadcasted_iota(jnp.int32, (8, GW), 1) // DH)

    def P(blk):
        return proj_ref[:, blk * GW:(blk + 1) * GW]

    @pl.when((layer == 0) & (g == 0))
    def _load_x():
        hs_ref[...] = x_ref[...]

    @pl.when(g == 0)
    def _pre():
        hn = _rms_norm(hs_ref[...], normw_ref[...]).astype(BF16)
        proj_ref[...] = _dot_nt(hn, win_ref[...])
        cosr, s1, s2 = cos_ref[...], s1_ref[...], s2_ref[...]
        qgt_ref[...] = (_rope(P(RQ), cosr, s1, s2) * gam_ref[...]).T
        rkt_ref[...] = _rope(P(RK), cosr, s1, s2).T
        rvt_ref[...] = P(RV).T
        mqt_ref[...] = P(MQ).T
        mkt_ref[...] = P(MK).T
        mvt_ref[...] = P(MV).T
        gt = P(GT)
        ig = gt[:, 0:128].T[0:8] + bi_ref[...]
        lf = _log_sigmoid(gt[:, 128:256].T[0:8] + bf_ref[...])
        a = lf + m0_ref[...]
        mt = jnp.maximum(a, ig)
        wi_ref[...] = jnp.exp(ig - mt)
        ws_ref[...] = jnp.exp(a - mt)
        mlmn_ref[...] = mt
        ot_ref[...] = jnp.zeros_like(ot_ref)
        qa8_ref[...] = jnp.where(mask8[None], P(AQ)[:, None, :], 0.0).reshape(8 * nsamp, GW)
        but = _dot_tn(bt_ref[...], P(SU).T.astype(BF16))
        are, aim = abt_ref[0:NS, :], abt_ref[NS:2 * NS, :]
        x0r, x0i = x0re_ref[...], x0im_ref[...]
        s5re_ref[...] = are * x0r - aim * x0i + but[0:NS]
        s5im_ref[...] = are * x0i + aim * x0r + but[NS:2 * NS]

    head = g // (GW // SROWS // NH)
    hrow = pl.ds(pl.multiple_of(head * DH, DH), DH)
    gam_row = gam8_ref[pl.ds(head, 1), :]
    ws_row = ws_ref[pl.ds(head, 1), :]
    wi_row = wi_ref[pl.ds(head, 1), :]
    v_slab = rvt_ref[hrow, :]
    q_slab = mqt_ref[hrow, :]
    k_slab = mkt_ref[hrow, :]
    o_acc = jnp.zeros((DH, nsamp), F32)
    for i in range(SROWS):
        r = pl.ds(g * SROWS + i, 1)
        s_t = rets_ref[i]
        o_acc = o_acc + qgt_ref[r, :] * s_t
        retn_ref[i] = gam_row * s_t + rkt_ref[r, :] * v_slab
        c_t = mlc_ref[i]
        cqt_ref[r, :] = jnp.sum(c_t * q_slab, axis=0, keepdims=True)
        mlcn_ref[i] = ws_row * c_t + (wi_row * mvt_ref[r, :]) * k_slab
    ot_ref[hrow, :] = ot_ref[hrow, :] + o_acc

    tiles = [pl.ds(pl.multiple_of((g * SB + i) * 8, 8), 8) for i in range(SB)]
    q8 = [qa8_ref[tiles[i], :].astype(BF16) for i in range(SB)]
    sc = [_dot(q8[i], ck_ref[i].astype(BF16)) for i in range(SB)]
    ps = []
    for i in range(SB):
        e = jnp.exp(sc[i] - jnp.max(sc[i], axis=-1, keepdims=True))
        ps.append((e / jnp.sum(e, axis=-1, keepdims=True)).astype(BF16))
    ta = [_dot_nt(ps[i], cv_ref[i].astype(BF16)) for i in range(SB)]
    xa_ref[pl.ds(pl.multiple_of(g * SB, SB), SB), :] = jnp.concatenate(
        [jnp.sum(jnp.where(mask8, ta[i], 0.0), axis=0, keepdims=True) for i in range(SB)], axis=0)

    @pl.when(g == n_blocks - 1)
    def _post():
        cosr, s1, s2 = cos_ref[...], s1_ref[...], s2_ref[...]
        rq = _rope(P(RQ), cosr, s1, s2)
        rk = _rope(P(RK), cosr, s1, s2)
        ro = _dot_x2(rq * rk, ones_bd) * P(RV) + ot_ref[...].T
        ret_out = _head_norm(ro, retgn_ref[...], avg) * _silu(P(RG))
        mht = []
        for h in range(NH):
            rows = slice(h * DH, (h + 1) * DH)
            wi = wi_ref[h:h + 1, :]
            ws = ws_ref[h:h + 1, :]
            emt = jnp.exp(-mlmn_ref[h:h + 1, :])
            qt, kt, n0 = mqt_ref[rows, :], mkt_ref[rows, :], n0_ref[rows, :]
            s = jnp.sum(qt * kt, axis=0, keepdims=True) * wi
            den = s + ws * jnp.sum(n0 * qt, axis=0, keepdims=True)
            mht.append((s * mvt_ref[rows, :] + ws * cqt_ref[rows, :])
                       / jnp.maximum(jnp.abs(den), emt))
            mlnn_ref[rows, :] = ws * n0 + wi * kt
        mh = jnp.concatenate(mht, axis=0).T * _sigmoid(P(MO))
        ml_out = _head_norm(mh, mlgn_ref[...], avg) * _silu(P(MG))
        su = P(SU)
        xcat = jnp.concatenate([s5re_ref[...], s5im_ref[...]], axis=0).astype(BF16)
        sy = _dot_tn(xcat, ct_ref[...]) + dsk_ref[...] * su
        sy = _gelu_tanh(sy)
        sy = sy * _sigmoid(_dot(sy.astype(BF16), wglu_ref[...]))
        s5_out = sy * _silu(P(SG))
        xa_out = xa_ref[...] * _silu(P(AG))
        mix = jnp.concatenate([ret_out, ml_out, s5_out, xa_out], axis=1).astype(BF16)
        y = hs_ref[...] + _dot(mix, wout_ref[...])
        hs_ref[...] = y

        @pl.when(layer == n_layers - 1)
        def _emit():
            y_ref[...] = _rms_norm(y, fnw_ref[...])


def _sample_layers_t(x, st, w, consts):
    nsamp = x.shape[0]
    depth = w["w_in"].shape[0]
    nb = GW // SROWS
    assert nsamp == nb * SB and nsamp == 128
    once = lambda shape: pl.BlockSpec(shape, lambda l, g: (0,) * len(shape),
                                      pipeline_mode=pl.Buffered(1))
    lyr = lambda shape: pl.BlockSpec((None,) + shape, lambda l, g: (l,) + (0,) * len(shape))
    lyr_out = lyr
    srows = pl.BlockSpec((None, SROWS, DH, nsamp), lambda l, g: (l, g, 0, 0))
    cache = pl.BlockSpec((None, SB, GW, MEM), lambda l, g: (l, g, 0, 0))
    in_specs = [
        once((nsamp, D)), lyr((1, D)), lyr((DP, D)), lyr((D, D)),
        once((1, 128)), once((1, 128)), once((1, 128)), once((1, GW)), once((8, nsamp)),
        lyr((1, GW)), lyr((1, GW)), lyr((8, nsamp)), lyr((8, nsamp)),
        lyr((2 * NS, nsamp)), lyr((GW, 2 * NS)), lyr((2 * NS, GW)),
        lyr((1, GW)), lyr((GW, GW)), once((1, D)),
        lyr((8, nsamp)), lyr((GW, nsamp)), lyr((NS, nsamp)), lyr((NS, nsamp)),
        srows, srows, cache, cache,
    ]
    out_shape = (
        jax.ShapeDtypeStruct((nsamp, D), F32),
        jax.ShapeDtypeStruct((depth, GW, DH, nsamp), F32),
        jax.ShapeDtypeStruct((depth, GW, DH, nsamp), F32),
        jax.ShapeDtypeStruct((depth, GW, nsamp), F32),
        jax.ShapeDtypeStruct((depth, 8, nsamp), F32),
        jax.ShapeDtypeStruct((depth, NS, nsamp), F32),
        jax.ShapeDtypeStruct((depth, NS, nsamp), F32),
    )
    out_specs = (
        pl.BlockSpec((nsamp, D), lambda l, g: (0, 0)), srows, srows,
        lyr_out((GW, nsamp)), lyr_out((8, nsamp)), lyr_out((NS, nsamp)), lyr_out((NS, nsamp)),
    )
    scratch = [
        pltpu.VMEM((nsamp, D), F32),
        pltpu.VMEM((nsamp, DP), F32),
        pltpu.VMEM((GW, nsamp), F32), pltpu.VMEM((GW, nsamp), F32), pltpu.VMEM((GW, nsamp), F32),
        pltpu.VMEM((GW, nsamp), F32), pltpu.VMEM((GW, nsamp), F32), pltpu.VMEM((GW, nsamp), F32),
        pltpu.VMEM((8, nsamp), F32), pltpu.VMEM((8, nsamp), F32),
        pltpu.VMEM((GW, nsamp), F32), pltpu.VMEM((GW, nsamp), F32),
        pltpu.VMEM((8 * nsamp, GW), F32), pltpu.VMEM((nsamp, GW), F32),
    ]
    kern = functools.partial(_sample_kernel_t, n_layers=depth, n_blocks=nb)
    return pl.pallas_call(
        kern, grid=(depth, nb), in_specs=in_specs, out_specs=out_specs, out_shape=out_shape,
        scratch_shapes=scratch, name="sample_layers",
        compiler_params=pltpu.CompilerParams(
            dimension_semantics=("arbitrary", "arbitrary"), vmem_limit_bytes=VMEM_LIMIT),
    )(x, w["norm_w"], w["w_in"], w["w_out"],
      consts["cos_s"], consts["s1_s"], consts["s2_s"], consts["gam"], consts["gam8"],
      w["ret_gn"], w["ml_gn"], w["b_i8"], w["b_f8"],
      w["abt"], w["bt"], w["ct"], w["s5_d"], w["w_glu"], w["final_norm_w"],
      st["m"], st["n"], st["s5_re"], st["s5_im"], st["ret"], st["c"], st["mem_k"], st["mem_v"])


def _rope_tables(pos):
    half = DH // 2
    inv = ROPE_BASE ** (-np.arange(half, dtype=np.float64) / half)
    ang = np.asarray(pos, np.float64)[:, None] * inv[None, :]
    cos, sin = np.cos(ang), np.sin(ang)
    zero = np.zeros_like(sin)
    c = np.tile(np.concatenate([cos, cos], axis=-1), (1, 2))
    s1 = np.tile(np.concatenate([zero, sin], axis=-1), (1, 2))
    s2 = np.tile(np.concatenate([-sin, zero], axis=-1), (1, 2))
    return tuple(jnp.asarray(t, F32) for t in (c, s1, s2))


def _constants(seq):
    lg = np.log1p(-np.power(2.0, -5.0 - np.arange(NH, dtype=np.float64)))[:, None]
    idx = np.arange(L, dtype=np.float64)
    diff = idx[:, None] - idx[None, :]
    decay = np.where(diff >= 0, np.exp(lg[:, :, None] * np.maximum(diff, 0.0)), 0.0)
    rep = lambda t: np.repeat(t, DH, axis=0).T
    consts = {
        "dec": np.transpose(decay, (1, 0, 2)).reshape(L, NH * L),
        "qdec": rep(np.exp(lg * (idx + 1.0))),
        "kdec": rep(np.exp(lg * (L - 1.0 - idx))),
        "cdec": rep(np.exp(lg * L)),
        "gam": rep(np.exp(lg * 1.0)),
        "gam8": np.pad(np.broadcast_to(np.exp(lg), (NH, 128)), ((0, 8 - NH), (0, 0))),
    }
    consts = {k: jnp.asarray(v, F32) for k, v in consts.items()}
    consts["cos_p"], consts["s1_p"], consts["s2_p"] = _rope_tables(np.arange(seq))
    consts["cos_s"], consts["s1_s"], consts["s2_s"] = _rope_tables(PAST_LEN + np.arange(1))
    return consts


def _pack_w_in(w_in):
    wt = jnp.swapaxes(w_in, 1, 2)
    sizes = (GW,) * 9 + (NH, NH) + (GW,) * 4
    offs = np.concatenate([[0], np.cumsum(sizes)])
    seg = [wt[:, int(offs[i]):int(offs[i + 1]), :] for i in range(len(sizes))]
    scale = DH ** -0.5
    pad = lambda t: jnp.pad(t, ((0, 0), (0, 128 - t.shape[1]), (0, 0)))
    blocks = [pad(seg[9]), pad(seg[10]), seg[11],
              seg[0], seg[1] * scale, seg[2], seg[3],
              seg[4], seg[5] * scale, seg[6], seg[7], seg[8],
              seg[12], seg[13] * scale, seg[14]]
    return jnp.concatenate(blocks, axis=1).astype(BF16)


def kernel(x_prompt, x_sample, mem_prompt, state_ret, state_mlstm_c, state_mlstm_n, state_mlstm_m,
           state_s5_re, state_s5_im, cache_mem_k, cache_mem_v,
           norm_w, w_in, ret_gn, ml_b_i, ml_b_f, ml_gn,
           s5_a_re, s5_a_im, s5_log_dt, s5_b_re, s5_b_im, s5_c_re, s5_c_im, s5_d, s5_w_glu,
           w_mem_k, w_mem_v, w_out, final_norm_w):
    depth = norm_w.shape[0]
    bp, seq, _ = x_prompt.shape
    bs = x_sample.shape[0]
    consts = _constants(seq)
    abt, bt, ct, tab = _s5_prepare(s5_a_re, s5_a_im, s5_log_dt, s5_b_re, s5_b_im, s5_c_re, s5_c_im)
    rows8 = lambda t: jnp.pad(jnp.broadcast_to(t[:, :, None], (depth, NH, TB)),
                              ((0, 0), (0, 8 - NH), (0, 0)))
    w = {
        "norm_w": norm_w[:, None], "w_in": _pack_w_in(w_in), "w_out": w_out.astype(BF16),
        "w_mem_kv": jnp.concatenate([w_mem_k, w_mem_v], axis=-1).astype(BF16),
        "ret_gn": ret_gn[:, None], "ml_gn": ml_gn[:, None],
        "b_i8": rows8(ml_b_i), "b_f8": rows8(ml_b_f),
        "abt": abt, "tab": tab, "bt": bt, "ct": ct,
        "s5_d": s5_d[:, None], "w_glu": s5_w_glu.astype(BF16),
        "final_norm_w": final_norm_w[None],
    }

    hp = x_prompt
    outs_p = [[] for _ in range(8)]
    for l in range(depth):
        hp, *sp = _prompt_layer(l, hp, mem_prompt, w, consts, l == depth - 1)
        for i in range(8):
            outs_p[i].append(sp[i])
    rs, cs, ns, ms, xr, xi, mk, mv = (jnp.stack(o) for o in outs_p)
    diag = lambda t: jnp.stack([t[:, :, h * DH:(h + 1) * DH, h * DH:(h + 1) * DH]
                                for h in range(NH)], axis=2)
    ret_p = diag(rs)
    mlc_p = jnp.swapaxes(diag(cs), -1, -2)
    mln_p = jnp.stack([ns[:, :, h * DH:(h + 1) * DH, h] for h in range(NH)], axis=2)
    mlm_p = ms[:, :, :NH, 0]
    s5re_p = xr[:, :, 0].reshape(depth, bp, S5G, S5P)
    s5im_p = xi[:, :, 0].reshape(depth, bp, S5G, S5P)
    memk_p = mk.reshape(depth, bp, MEM, NH, DH)
    memv_p = mv.reshape(depth, bp, MEM, NH, DH)

    st = {
        "m": jnp.pad(jnp.swapaxes(state_mlstm_m, 1, 2), ((0, 0), (0, 8 - NH), (0, 0))),
        "n": jnp.transpose(state_mlstm_n, (0, 2, 3, 1)).reshape(depth, GW, bs),
        "s5_re": jnp.transpose(state_s5_re, (0, 2, 3, 1)).reshape(depth, NS, bs),
        "s5_im": jnp.transpose(state_s5_im, (0, 2, 3, 1)).reshape(depth, NS, bs),
        "ret": jnp.transpose(state_ret, (0, 2, 3, 4, 1)).reshape(depth, GW, DH, bs),
        "c": jnp.transpose(state_mlstm_c, (0, 2, 3, 4, 1)).reshape(depth, GW, DH, bs),
        "mem_k": jnp.transpose(cache_mem_k, (0, 1, 3, 4, 2)).reshape(depth, bs, GW, MEM),
        "mem_v": jnp.transpose(cache_mem_v, (0, 1, 3, 4, 2)).reshape(depth, bs, GW, MEM),
    }
    hs, rn, cn, nn, mn, sr, si = _sample_layers_t(x_sample.reshape(bs, D), st, w, consts)
    back5 = lambda t: jnp.transpose(t.reshape(depth, NH, DH, DH, bs), (0, 4, 1, 2, 3))
    back4 = lambda t, a, b: jnp.transpose(t.reshape(depth, a, b, bs), (0, 3, 1, 2))
    return (hp, hs.reshape(bs, 1, D),
            ret_p, back5(rn), mlc_p, back5(cn),
            mln_p, back4(nn, NH, DH), mlm_p, jnp.swapaxes(mn[:, :NH], 1, 2),
            s5re_p, back4(sr, S5G, S5P), s5im_p, back4(si, S5G, S5P),
            memk_p, memv_p)
```

```python
import functools
import math

import numpy as np
import jax
import jax.numpy as jnp
from jax import lax
from jax.experimental import pallas as pl
from jax.experimental.pallas import tpu as pltpu

F32 = jnp.float32
BF16 = jnp.bfloat16

D = 1024
GW = 256
NH = 4
DH = 64
L = 128
LS = 8
S5G = 16
S5P = 64
S5C = 16
NS = S5G * S5P
MEM = 256
EPS = 1e-6
NEG_INF = -1e30
ROPE_BASE = 10000.0
PAST_LEN = 16384

TB = 512
SB = 8
XQ = 512
NBLK = 14
DP = NBLK * GW
(GT, SU, RQ, RK, RV, RG, MQ, MK, MV, MO, MG, SG, AQ, AG) = range(NBLK)

VMEM_LIMIT = 56 * 1024 * 1024


_DONE = object()


def _dot(a, b):
    return jnp.dot(a, b, preferred_element_type=F32)


def _dot_nt(a, b):
    return lax.dot_general(a, b, (((1,), (1,)), ((), ())), preferred_element_type=F32)


def _dot_tn(a, b):
    return lax.dot_general(a, b, (((0,), (0,)), ((), ())), preferred_element_type=F32)


def _split2(x):
    hi = x.astype(BF16)
    lo = (x - hi.astype(F32)).astype(BF16)
    return hi, lo


def _dot_x2(x, w):
    hi, lo = _split2(x)
    return _dot(hi, w) + _dot(lo, w)


def _dot_x3(x, w):
    hi = x.astype(BF16)
    r = x - hi.astype(F32)
    mid = r.astype(BF16)
    lo = (r - mid.astype(F32)).astype(BF16)
    return _dot(hi, w) + _dot(mid, w) + _dot(lo, w)


def _sigmoid(x):
    return 0.5 * (1.0 + jnp.tanh(0.5 * x))


def _silu(x):
    return x * _sigmoid(x)


def _log_sigmoid(x):
    return jnp.minimum(x, 0.0) - jnp.log1p(jnp.exp(-jnp.abs(x)))


def _gelu_tanh(x):
    c = math.sqrt(2.0 / math.pi)
    return x * (0.5 * (1.0 + jnp.tanh(c * (x + 0.044715 * (x * x * x)))))


def _lane_head(n):
    return lax.broadcasted_iota(jnp.int32, (1, n), 1) // DH


def _head_masks():
    lh = _lane_head(GW)
    return [lh == h for h in range(NH)]


def _block_diag_mask():
    r = lax.broadcasted_iota(jnp.int32, (GW, GW), 0) // DH
    c = lax.broadcasted_iota(jnp.int32, (GW, GW), 1) // DH
    return r == c


def _avg_matrix():
    return jnp.where(_block_diag_mask(), 1.0 / DH, 0.0).astype(BF16)


def _ones_matrix():
    return jnp.where(_block_diag_mask(), 1.0, 0.0).astype(BF16)


def _rope(x, cos, s1, s2):
    outs = []
    for j in range(2):
        xs = x[:, j * 128:(j + 1) * 128]
        outs.append(xs * cos + pltpu.roll(xs, 32, 1) * s1 + pltpu.roll(xs, 96, 1) * s2)
    return jnp.concatenate(outs, axis=1)


def _head_norm(x, gain, avg):
    mu = _dot_x2(x, avg)
    d = x - mu
    var = _dot_x2(d * d, avg)
    return d * lax.rsqrt(var + EPS) * gain


def _rms_norm(x, w):
    ms = jnp.mean(x * x, axis=-1, keepdims=True)
    return x * lax.rsqrt(ms + EPS) * w


def _stack_heads(x, hm):
    return jnp.concatenate([jnp.where(hm[h], x, 0.0) for h in range(NH)], axis=0)


def _s5_prep_kernel(are_ref, aim_ref, ldt_ref, bre_ref, bim_ref, cre_ref, cim_ref,
                    abt_ref, bt_ref, ct_ref, tab_ref):
    a_re = are_ref[0]
    a_im = aim_ref[0]
    dt = jnp.exp(ldt_ref[0])
    lam_re = a_re * dt
    lam_im = a_im * dt
    mag = jnp.exp(lam_re)
    ab_re = mag * jnp.cos(lam_im)
    ab_im = mag * jnp.sin(lam_im)
    den = a_re * a_re + a_im * a_im
    nr = ab_re - 1.0
    ni = ab_im
    f_re = (nr * a_re + ni * a_im) / den
    f_im = (ni * a_re - nr * a_im) / den
    abt_ref[0, 0:NS, :] = jnp.broadcast_to(ab_re, (128, NS)).T
    abt_ref[0, NS:2 * NS, :] = jnp.broadcast_to(ab_im, (128, NS)).T
    b_re = bre_ref[0]
    b_im = bim_ref[0]
    bb_re = (f_re * b_re - f_im * b_im).astype(BF16)
    bb_im = (f_re * b_im + f_im * b_re).astype(BF16)
    rep_r = (lax.broadcasted_iota(jnp.int32, (GW, S5C), 0) % S5C
             == lax.broadcasted_iota(jnp.int32, (GW, S5C), 1)).astype(BF16)
    in_blk = (lax.broadcasted_iota(jnp.int32, (GW, NS), 0) // S5C
              == lax.broadcasted_iota(jnp.int32, (GW, NS), 1) // S5P)
    bt_ref[0, :, :NS] = jnp.where(in_blk, _dot(rep_r, bb_re), 0.0).astype(BF16)
    bt_ref[0, :, NS:] = jnp.where(in_blk, _dot(rep_r, bb_im), 0.0).astype(BF16)
    rep_c = (lax.broadcasted_iota(jnp.int32, (S5C, GW), 0)
             == lax.broadcasted_iota(jnp.int32, (S5C, GW), 1) % S5C).astype(BF16)
    out_blk = (lax.broadcasted_iota(jnp.int32, (NS, GW), 0) // S5P
               == lax.broadcasted_iota(jnp.int32, (NS, GW), 1) // S5C)
    ct_ref[0, :NS, :] = jnp.where(out_blk, _dot(cre_ref[0].astype(BF16), rep_c), 0.0).astype(BF16)
    ct_ref[0, NS:, :] = jnp.where(out_blk, -_dot(cim_ref[0].astype(BF16), rep_c), 0.0).astype(BF16)
    i = lax.broadcasted_iota(jnp.int32, (LS, NS), 0).astype(F32)
    for slot, k in ((0, -i), (2, i), (4, i + 1.0)):
        pmag = jnp.exp(k * lam_re)
        tab_ref[0, slot] = pmag * jnp.cos(k * lam_im)
        tab_ref[0, slot + 1] = pmag * jnp.sin(k * lam_im)


def _s5_prepare(a_re, a_im, log_dt, b_re, b_im, c_re, c_im):
    depth = a_re.shape[0]
    are = a_re.reshape(depth, 1, NS)
    aim = a_im.reshape(depth, 1, NS)
    ldt = jnp.repeat(log_dt, S5P, axis=-1).reshape(depth, 1, NS)
    bre = jnp.transpose(b_re, (0, 3, 1, 2)).reshape(depth, S5C, NS)
    bim = jnp.transpose(b_im, (0, 3, 1, 2)).reshape(depth, S5C, NS)
    cre = jnp.transpose(c_re, (0, 1, 3, 2)).reshape(depth, NS, S5C)
    cim = jnp.transpose(c_im, (0, 1, 3, 2)).reshape(depth, NS, S5C)
    per_layer = lambda *shape: pl.BlockSpec((1,) + shape, lambda l: (l,) + (0,) * len(shape))
    out_shape = (jax.ShapeDtypeStruct((depth, 2 * NS, 128), F32),
                 jax.ShapeDtypeStruct((depth, GW, 2 * NS), BF16),
                 jax.ShapeDtypeStruct((depth, 2 * NS, GW), BF16),
                 jax.ShapeDtypeStruct((depth, 6, LS, NS), F32))
    return pl.pallas_call(
        _s5_prep_kernel, grid=(depth,),
        in_specs=[per_layer(1, NS), per_layer(1, NS), per_layer(1, NS),
                  per_layer(S5C, NS), per_layer(S5C, NS), per_layer(NS, S5C), per_layer(NS, S5C)],
        out_specs=(per_layer(2 * NS, 128), per_layer(GW, 2 * NS), per_layer(2 * NS, GW),
                   per_layer(6, LS, NS)),
        out_shape=out_shape, name="s5_prepare",
        compiler_params=pltpu.CompilerParams(dimension_semantics=("arbitrary",)),
    )(are, aim, ldt, bre, bim, cre, cim)


N_STATE_OUT = 8


def _prompt_kernel(x_ref, mem_ref, normw_ref, win_ref, wout_ref, wmkv_ref,
                   cos_ref, s1_ref, s2_ref, dec_ref, qdec_ref, kdec_ref, cdec_ref,
                   retgn_ref, mlgn_ref, bi_ref, bf_ref,
                   tab_ref, bt_ref, ct_ref, dsk_ref, wglu_ref, fnw_ref,
                   *rest, last_layer, n_tblocks, n_prev):
    (y_ref, rets_ref, mlc_ref, mln_ref, mlm_ref, s5re_ref, s5im_ref, memk_ref, memv_ref,
     proj_ref, mix_ref, s_ref, c_ref, n_ref, m_ref, xre_ref, xim_ref,
     mk_ref, mv_ref, bu_ref, xcat_ref, car_ref) = rest[n_prev:]
    t = pl.program_id(1)
    hm = _head_masks()
    bd = _block_diag_mask()
    avg = _avg_matrix()
    lane128 = lax.broadcasted_iota(jnp.int32, (1, 128), 1)
    row_i = lax.broadcasted_iota(jnp.int32, (L, 128), 0)
    col_i = lax.broadcasted_iota(jnp.int32, (L, 128), 1)
    causal = row_i >= col_i
    tri_u = jnp.where(row_i <= col_i, 1.0, 0.0).astype(BF16)
    tri_sub = jnp.where(causal & (row_i // LS == col_i // LS), 1.0, 0.0)
    blk_sum = jnp.where(lax.broadcasted_iota(jnp.int32, (L // LS, 128), 0)
                        == lax.broadcasted_iota(jnp.int32, (L // LS, 128), 1) // LS, 1.0, 0.0)
    tri_ext = jnp.concatenate([tri_sub, blk_sum], axis=0).astype(BF16)

    @pl.when(t == 0)
    def _init():
        s_ref[...] = jnp.zeros_like(s_ref)
        c_ref[...] = jnp.zeros_like(c_ref)
        n_ref[...] = jnp.zeros_like(n_ref)
        m_ref[...] = jnp.zeros_like(m_ref)
        xre_ref[...] = jnp.zeros_like(xre_ref)
        xim_ref[...] = jnp.zeros_like(xim_ref)
        mkv = _dot(mem_ref[0].astype(BF16), wmkv_ref[...])
        mk = mkv[:, :GW]
        mv = mkv[:, GW:]
        memk_ref[0] = mk
        memv_ref[0] = mv
        mk_ref[...] = _stack_heads(mk, hm).astype(BF16)
        mv_ref[...] = _stack_heads(mv, hm).astype(BF16)

    x = x_ref[0]
    hn = _rms_norm(x, normw_ref[...]).astype(BF16)
    proj_ref[...] = _dot_nt(hn, win_ref[...])

    nc = TB // L
    crow = [slice(c * L, (c + 1) * L) for c in range(nc)]

    def PB(blk, rows=slice(None)):
        return proj_ref[rows, blk * GW:(blk + 1) * GW]


    def xattn_stages():
        for piece in range(TB // XQ):
            rs = slice(piece * XQ, (piece + 1) * XQ)
            sc = _dot_nt(PB(AQ, rs).astype(BF16), mk_ref[...])
            yield
            ps = []
            for h in range(NH):
                seg = sc[:, h * MEM:(h + 1) * MEM]
                e = jnp.exp(seg - jnp.max(seg, axis=-1, keepdims=True))
                ps.append((e / jnp.sum(e, axis=-1, keepdims=True)).astype(BF16))
            p = jnp.concatenate(ps, axis=1)
            yield
            xa = _dot(p, mv_ref[...])
            yield
            mix_ref[rs, 3 * GW:4 * GW] = (xa * _silu(PB(AG, rs))).astype(BF16)
            yield

    def s5_stages():
        lanes = [(slice(j * 128, (j + 1) * 128), slice(NS + j * 128, NS + (j + 1) * 128))
                 for j in range(NS // 128)]
        bpc = L // LS
        nblk = TB // LS
        su = PB(SU)
        bu_ref[...] = _dot(su.astype(BF16), bt_ref[...])
        yield
        for c in range(nc):
            for lre, lim in lanes:
                br = bu_ref[crow[c], lre]
                bi = bu_ref[crow[c], lim]
                wr = jnp.tile(tab_ref[0, :, lre], (bpc, 1))
                wi = jnp.tile(tab_ref[1, :, lre], (bpc, 1))
                xcat_ref[crow[c], lre] = (wr * br - wi * bi).astype(BF16)
                xcat_ref[crow[c], lim] = (wr * bi + wi * br).astype(BF16)
            yield
        for c in range(nc):
            z = _dot(tri_ext, xcat_ref[crow[c], :])
            bu_ref[crow[c], :] = z[0:L]
            car_ref[c * bpc:(c + 1) * bpc, :] = z[L:L + bpc]
            yield
        rowb = lax.broadcasted_iota(jnp.int32, (nblk, 128), 0)
        ers, eis, prs, pis, c0s = [], [], [], [], []
        for lre, lim in lanes:
            zr = car_ref[:, lre]
            zi = car_ref[:, lim]
            pr = tab_ref[2, LS - 1:LS, lre]
            pi = tab_ref[3, LS - 1:LS, lre]
            er = pr * zr - pi * zi
            ei = pr * zi + pi * zr
            pr = tab_ref[4, LS - 1:LS, lre]
            pi = tab_ref[5, LS - 1:LS, lre]
            c0r = xre_ref[0:1, lre]
            c0i = xim_ref[0:1, lre]
            ers.append(er + jnp.where(rowb == 0, pr * c0r - pi * c0i, 0.0))
            eis.append(ei + jnp.where(rowb == 0, pr * c0i + pi * c0r, 0.0))
            prs.append(pr)
            pis.append(pi)
            c0s.append((c0r, c0i))
        yield
        for k in range(nblk.bit_length() - 1):
            d = 1 << k
            for j in range(len(lanes)):
                er, ei, pr, pi = ers[j], eis[j], prs[j], pis[j]
                sr = jnp.where(rowb >= d, pltpu.roll(er, d, 0), 0.0)
                si = jnp.where(rowb >= d, pltpu.roll(ei, d, 0), 0.0)
                ers[j], eis[j] = er + pr * sr - pi * si, ei + pr * si + pi * sr
                prs[j], pis[j] = pr * pr - pi * pi, 2.0 * (pr * pi)
            yield
        for j, (lre, lim) in enumerate(lanes):
            xre_ref[:, lre] = jnp.broadcast_to(ers[j][nblk - 1:nblk, :], (8, 128))
            xim_ref[:, lre] = jnp.broadcast_to(eis[j][nblk - 1:nblk, :], (8, 128))
            cr = jnp.where(rowb == 0, c0s[j][0], pltpu.roll(ers[j], 1, 0))
            ci_ = jnp.where(rowb == 0, c0s[j][1], pltpu.roll(eis[j], 1, 0))
            ar = tab_ref[4, 0:1, lre]
            ai = tab_ref[5, 0:1, lre]
            car_ref[:, lre] = ar * cr - ai * ci_
            car_ref[:, lim] = ar * ci_ + ai * cr
        yield
        for c in range(nc):
            for lre, lim in lanes:
                cr = jnp.concatenate(
                    [jnp.broadcast_to(car_ref[c * bpc + j:c * bpc + j + 1, lre], (LS, 128))
                     for j in range(bpc)], axis=0)
                ci_ = jnp.concatenate(
                    [jnp.broadcast_to(car_ref[c * bpc + j:c * bpc + j + 1, lim], (LS, 128))
                     for j in range(bpc)], axis=0)
                zr = bu_ref[crow[c], lre] + cr
                zi = bu_ref[crow[c], lim] + ci_
                pr = jnp.tile(tab_ref[2, :, lre], (bpc, 1))
                pi = jnp.tile(tab_ref[3, :, lre], (bpc, 1))
                xcat_ref[crow[c], lre] = (pr * zr - pi * zi).astype(BF16)
                xcat_ref[crow[c], lim] = (pr * zi + pi * zr).astype(BF16)
            yield
        sy = _dot(xcat_ref[...], ct_ref[...]) + dsk_ref[...] * su
        yield
        sy = _gelu_tanh(sy)
        gate = _dot(sy.astype(BF16), wglu_ref[...])
        yield
        mix_ref[:, 2 * GW:3 * GW] = (sy * _sigmoid(gate) * _silu(PB(SG))).astype(BF16)
        yield

    def retention_stages():
        rq = _rope(PB(RQ), cos_ref[...], s1_ref[...], s2_ref[...])
        rk = _rope(PB(RK), cos_ref[...], s1_ref[...], s2_ref[...])
        rvf = PB(RV)
        rv = rvf.astype(BF16)
        rqb = rq.astype(BF16)
        kst = [_stack_heads(rk[crow[c]], hm).astype(BF16) for c in range(nc)]
        vst = [_stack_heads(rvf[crow[c]], hm).astype(BF16) for c in range(nc)]
        yield
        inner = [_dot_nt(rqb[crow[c]], kst[c]) for c in range(nc)]
        yield
        pmat = [(inner[c] * dec_ref[...]).astype(BF16) for c in range(nc)]
        kd = [(rk[crow[c]] * kdec_ref[...]).astype(BF16) for c in range(nc)]
        yield
        rloc = [_dot(pmat[c], vst[c]) for c in range(nc)]
        upd = [_dot_tn(kd[c], rv[crow[c]]) for c in range(nc)]
        yield
        st = [s_ref[...]]
        for c in range(nc):
            st.append(st[c] * cdec_ref[...] + jnp.where(bd, upd[c], 0.0))
        s_ref[...] = st[nc]
        qd = [(rq[crow[c]] * qdec_ref[...]).astype(BF16) for c in range(nc)]
        yield
        ost = [_dot(qd[c], st[c].astype(BF16)) for c in range(nc)]
        yield
        ro = jnp.concatenate([rloc[c] + ost[c] for c in range(nc)], axis=0)
        mix_ref[:, 0:GW] = (_head_norm(ro, retgn_ref[...], avg) * _silu(PB(RG))).astype(BF16)
        yield

    def mlstm_stages():
        g = PB(GT)
        ig_r = jnp.concatenate([g[crow[c], 0:128].T[0:8] for c in range(nc)], axis=1) + bi_ref[...]
        lf_r = _log_sigmoid(
            jnp.concatenate([g[crow[c], 128:256].T[0:8] for c in range(nc)], axis=1) + bf_ref[...])
        yield
        b_r = [_dot_x3(lf_r[:, crow[c]], tri_u) for c in range(nc)]
        yield
        gd_r = [ig_r[:, crow[c]] - b_r[c] for c in range(nc)]
        lane_r = lax.broadcasted_iota(jnp.int32, (8, L), 1)
        cm_r = list(gd_r)
        for k in range(7):
            d = 1 << k
            cm_r = [jnp.maximum(v, jnp.where(lane_r >= d, pltpu.roll(v, d, 1), NEG_INF))
                    for v in cm_r]
            yield
        m_prev = [m_ref[...]]
        mt_r = []
        for c in range(nc):
            mt = jnp.maximum(b_r[c] + m_prev[c], b_r[c] + cm_r[c])
            mt_r.append(mt)
            m_prev.append(jnp.broadcast_to(mt[:, L - 1:L], (8, L)))
            yield
        m_ref[...] = m_prev[nc]
        cols = []
        for c in range(nc):
            bm = b_r[c] - mt_r[c]
            ws = jnp.exp(b_r[c] + m_prev[c] - mt_r[c])
            wl = jnp.exp(gd_r[c] + jnp.broadcast_to(bm[:, L - 1:L], (8, L)))
            emt = jnp.exp(-mt_r[c])
            cols.append(jnp.concatenate([bm, ws, wl, emt, jnp.zeros((L - 32, L), F32)], axis=0).T)
        mqf = PB(MQ)
        mq = mqf.astype(BF16)
        mkf = PB(MK)
        mvf = PB(MV)
        mv_ = mvf.astype(BF16)
        kst = [_stack_heads(mkf[crow[c]], hm).astype(BF16) for c in range(nc)]
        vst = [_stack_heads(mvf[crow[c]], hm).astype(BF16) for c in range(nc)]
        yield
        sraw = [_dot_nt(mq[crow[c]], kst[c]) for c in range(nc)]
        yield
        smat, den_i = [], []
        for c in range(nc):
            parts, dens = [], []
            for h in range(NH):
                arg = jnp.where(causal, cols[c][:, h:h + 1] + gd_r[c][h:h + 1, :], NEG_INF)
                s_h = sraw[c][:, h * L:(h + 1) * L] * jnp.exp(arg)
                dens.append(jnp.sum(s_h, axis=-1, keepdims=True))
                parts.append(s_h.astype(BF16))
            den_i.append(dens)
            smat.append(jnp.concatenate(parts, axis=1))
            yield
        ones_blk = jnp.ones((L, 128), BF16)
        kws = []
        for c in range(nc):
            kw = jnp.zeros((L, GW), F32)
            for h in range(NH):
                kw = kw + jnp.where(hm[h], mkf[crow[c]] * cols[c][:, 16 + h:17 + h], 0.0)
            kws.append(kw.astype(BF16))
        yield
        rloc = [_dot(smat[c], vst[c]) for c in range(nc)]
        u = [_dot_tn(kws[c], jnp.concatenate([mv_[crow[c]], ones_blk], axis=1))
             for c in range(nc)]
        yield
        nmask = (lax.broadcasted_iota(jnp.int32, (GW, 128), 0) // DH
                 == lax.broadcasted_iota(jnp.int32, (GW, 128), 1))
        cst = [c_ref[...]]
        nst = [n_ref[...]]
        for c in range(nc):
            wsl256 = jnp.zeros((1, GW), F32)
            wsl128 = jnp.zeros((1, 128), F32)
            for h in range(NH):
                wsl = cols[c][L - 1:L, 8 + h:9 + h]
                wsl256 = wsl256 + jnp.where(hm[h], wsl, 0.0)
                wsl128 = wsl128 + jnp.where(lane128 == h, wsl, 0.0)
            cst.append(cst[c] * wsl256 + jnp.where(bd, u[c][:, :GW], 0.0))
            nst.append(nst[c] * wsl128 + jnp.where(nmask, u[c][:, GW:], 0.0))
        c_ref[...] = cst[nc]
        n_ref[...] = nst[nc]
        yield
        qc = [_dot(mq[crow[c]], cst[c].astype(BF16)) for c in range(nc)]
        qn = [_dot(mq[crow[c]], nst[c].astype(BF16)) for c in range(nc)]
        yield
        mhs = []
        for c in range(nc):
            mh = jnp.zeros((L, GW), F32)
            for h in range(NH):
                ws = cols[c][:, 8 + h:9 + h]
                den = den_i[c][h] + ws * qn[c][:, h:h + 1]
                dn = jnp.maximum(jnp.abs(den), cols[c][:, 24 + h:25 + h])
                mh = mh + jnp.where(hm[h], (rloc[c] + ws * qc[c]) / dn, 0.0)
            mhs.append(mh)
            yield
        mh = jnp.concatenate(mhs, axis=0) * _sigmoid(PB(MO))
        mix_ref[:, GW:2 * GW] = (_head_norm(mh, mlgn_ref[...], avg) * _silu(PB(MG))).astype(BF16)
        yield

    pending = [mlstm_stages(), s5_stages(), retention_stages(), xattn_stages()]
    while pending:
        pending = [stage for stage in pending if next(stage, _DONE) is not _DONE]

    y = x_ref[0] + _dot(mix_ref[...], wout_ref[...])
    if last_layer:
        y = _rms_norm(y, fnw_ref[...])
    y_ref[0] = y

    @pl.when(t == n_tblocks - 1)
    def _final():
        s_fin = s_ref[...]
        c_fin = c_ref[...].T
        n_fin = n_ref[...].T
        for h in range(NH):
            blk = slice(h * DH, (h + 1) * DH)
            rets_ref[0, h] = s_fin[blk, blk]
            mlc_ref[0, h] = c_fin[blk, blk]
        mln_ref[0] = jnp.concatenate([n_fin[h:h + 1, h * DH:(h + 1) * DH] for h in range(NH)],
                                     axis=0)
        mlm_ref[0] = m_ref[...]
        s5re_ref[0] = xre_ref[...]
        s5im_ref[0] = xim_ref[...]


def _prompt_layer(layer, x, mem, w, consts, last_layer, prev):
    bsz, seq, _ = x.shape
    depth = w["w_in"].shape[0]
    nt = seq // TB
    full = lambda shape: pl.BlockSpec(shape, lambda b, t: (0,) * len(shape),
                                      pipeline_mode=pl.Buffered(1))
    lyr = lambda shape: pl.BlockSpec((None,) + shape, lambda b, t: (layer,) + (0,) * len(shape),
                                     pipeline_mode=pl.Buffered(1))
    tok = lambda width: pl.BlockSpec((TB, width), lambda b, t: (t, 0))
    per_b = lambda r, c: pl.BlockSpec((1, r, c), lambda b, t: (b, 0, 0))
    in_specs = [
        pl.BlockSpec((1, TB, D), lambda b, t: (b, t, 0)),
        per_b(MEM, D),
        lyr((1, D)), lyr((DP, D)), lyr((D, D)), lyr((D, 2 * GW)),
        tok(128), tok(128), tok(128),
        full((L, NH * L)), full((L, GW)), full((L, GW)), full((1, GW)),
        lyr((1, GW)), lyr((1, GW)), lyr((8, TB)), lyr((8, TB)),
        lyr((6, LS, NS)), lyr((GW, 2 * NS)), lyr((2 * NS, GW)),
        lyr((1, GW)), lyr((GW, GW)), full((1, D)),
    ] + [pl.BlockSpec(memory_space=pl.ANY)] * len(prev)
    n_in = len(in_specs) - len(prev)
    state_shapes = [(NH, DH, DH),
                    (NH, DH, DH),
                    (NH, DH),
                    (8, 128),
                    (8, NS),
                    (8, NS),
                    (MEM, GW),
                    (MEM, GW)]
    assert len(state_shapes) == N_STATE_OUT and len(prev) in (0, N_STATE_OUT)
    out_shape = ((jax.ShapeDtypeStruct((bsz, seq, D), F32),)
                 + tuple(jax.ShapeDtypeStruct((depth, bsz) + s, F32) for s in state_shapes))
    state_spec = lambda s: pl.BlockSpec((None, 1) + s,
                                        lambda b, t: (layer, b) + (0,) * len(s))
    out_specs = ((pl.BlockSpec((1, TB, D), lambda b, t: (b, t, 0)),)
                 + tuple(state_spec(s) for s in state_shapes))
    scratch = [
        pltpu.VMEM((TB, DP), F32),
        pltpu.VMEM((TB, D), BF16),
        pltpu.VMEM((GW, GW), F32),
        pltpu.VMEM((GW, GW), F32),
        pltpu.VMEM((GW, 128), F32),
        pltpu.VMEM((8, 128), F32),
        pltpu.VMEM((8, NS), F32),
        pltpu.VMEM((8, NS), F32),
        pltpu.VMEM((NH * MEM, GW), BF16),
        pltpu.VMEM((NH * MEM, GW), BF16),
        pltpu.VMEM((TB, 2 * NS), F32),
        pltpu.VMEM((TB, 2 * NS), BF16),
        pltpu.VMEM((TB // LS, 2 * NS), F32),
    ]
    kern = functools.partial(_prompt_kernel, last_layer=last_layer, n_tblocks=nt, n_prev=len(prev))
    return pl.pallas_call(
        kern, grid=(bsz, nt), in_specs=in_specs, out_specs=out_specs, out_shape=out_shape,
        scratch_shapes=scratch, name="prompt_layer",
        input_output_aliases={n_in + i: 1 + i for i in range(len(prev))},
        compiler_params=pltpu.CompilerParams(
            dimension_semantics=("arbitrary", "arbitrary"), vmem_limit_bytes=VMEM_LIMIT),
    )(x, mem, w["norm_w"], w["w_in"], w["w_out"], w["w_mem_kv"],
      consts["cos_p"], consts["s1_p"], consts["s2_p"],
      consts["dec"], consts["qdec"], consts["kdec"], consts["cdec"],
      w["ret_gn"], w["ml_gn"], w["b_i8"], w["b_f8"],
      w["tab"], w["bt"], w["ct"], w["s5_d"], w["w_glu"], w["final_norm_w"], *prev)


SROWS = 16


def _sample_kernel_t(x_ref, normw_ref, win_ref, wout_ref, cos_ref, s1_ref, s2_ref, gam_ref, gam8_ref,
                     retgn_ref, mlgn_ref, bi_ref, bf_ref,
                     abt_ref, bt_ref, ct_ref, dsk_ref, wglu_ref, fnw_ref,
                     m0_ref, n0_ref, x0re_ref, x0im_ref,
                     rets_ref, mlc_ref, ck_ref, cv_ref,
                     y_ref, retn_ref, mlcn_ref, mlnn_ref, mlmn_ref, s5re_ref, s5im_ref,
                     hs_ref, proj_ref, qgt_ref, rkt_ref, rvt_ref, mqt_ref, mkt_ref, mvt_ref,
                     wi_ref, ws_ref, ot_ref, cqt_ref, qa8_ref, xa_ref,
                     *, n_layers, n_blocks):
    layer = pl.program_id(0)
    g = pl.program_id(1)
    nsamp = x_ref.shape[0]
    hm = _head_masks()
    ones_bd = _ones_matrix()
    avg = _avg_matrix()
    mask8 = (lax.broadcasted_iota(jnp.int32, (8, GW), 0)
             == lax.broadcasted_iota(jnp.int32, (8, GW), 1) // DH)

    def P(blk):
        return proj_ref[:, blk * GW:(blk + 1) * GW]

    @pl.when((layer == 0) & (g == 0))
    def _load_x():
        hs_ref[...] = x_ref[...]

    @pl.when(g == 0)
    def _pre():
        hn = _rms_norm(hs_ref[...], normw_ref[...]).astype(BF16)
        proj_ref[...] = _dot_nt(hn, win_ref[...])
        cosr, s1, s2 = cos_ref[...], s1_ref[...], s2_ref[...]
        qgt_ref[...] = (_rope(P(RQ), cosr, s1, s2) * gam_ref[...]).T
        rkt_ref[...] = _rope(P(RK), cosr, s1, s2).T
        rvt_ref[...] = P(RV).T
        mqt_ref[...] = P(MQ).T
        mkt_ref[...] = P(MK).T
        mvt_ref[...] = P(MV).T
        gt = P(GT)
        ig = gt[:, 0:128].T[0:8] + bi_ref[...]
        lf = _log_sigmoid(gt[:, 128:256].T[0:8] + bf_ref[...])
        a = lf + m0_ref[...]
        mt = jnp.maximum(a, ig)
        wi_ref[...] = jnp.exp(ig - mt)
        ws_ref[...] = jnp.exp(a - mt)
        mlmn_ref[...] = mt
        ot_ref[...] = jnp.zeros_like(ot_ref)
        qa8_ref[...] = jnp.where(mask8[None], P(AQ)[:, None, :], 0.0).reshape(8 * nsamp, GW)
        but = _dot_tn(bt_ref[...], P(SU).T.astype(BF16))
        are, aim = abt_ref[0:NS, :], abt_ref[NS:2 * NS, :]
        x0r, x0i = x0re_ref[...], x0im_ref[...]
        s5re_ref[...] = are * x0r - aim * x0i + but[0:NS]
        s5im_ref[...] = are * x0i + aim * x0r + but[NS:2 * NS]

    head = g // (GW // SROWS // NH)
    hrow = pl.ds(pl.multiple_of(head * DH, DH), DH)
    gam_row = gam8_ref[pl.ds(head, 1), :]
    ws_row = ws_ref[pl.ds(head, 1), :]
    wi_row = wi_ref[pl.ds(head, 1), :]
    v_slab = rvt_ref[hrow, :]
    q_slab = mqt_ref[hrow, :]
    k_slab = mkt_ref[hrow, :]
    o_acc = jnp.zeros((DH, nsamp), F32)
    for i in range(SROWS):
        r = pl.ds(g * SROWS + i, 1)
        s_t = rets_ref[i]
        o_acc = o_acc + qgt_ref[r, :] * s_t
        retn_ref[i] = gam_row * s_t + rkt_ref[r, :] * v_slab
        c_t = mlc_ref[i]
        cqt_ref[r, :] = jnp.sum(c_t * q_slab, axis=0, keepdims=True)
        mlcn_ref[i] = ws_row * c_t + (wi_row * mvt_ref[r, :]) * k_slab
    ot_ref[hrow, :] = ot_ref[hrow, :] + o_acc

    tiles = [pl.ds(pl.multiple_of((g * SB + i) * 8, 8), 8) for i in range(SB)]
    q8 = [qa8_ref[tiles[i], :].astype(BF16) for i in range(SB)]
    sc = [_dot(q8[i], ck_ref[i].astype(BF16)) for i in range(SB)]
    ps = []
    for i in range(SB):
        e = jnp.exp(sc[i] - jnp.max(sc[i], axis=-1, keepdims=True))
        ps.append((e / jnp.sum(e, axis=-1, keepdims=True)).astype(BF16))
    ta = [_dot_nt(ps[i], cv_ref[i].astype(BF16)) for i in range(SB)]
    xa_ref[pl.ds(pl.multiple_of(g * SB, SB), SB), :] = jnp.concatenate(
        [jnp.sum(jnp.where(mask8, ta[i], 0.0), axis=0, keepdims=True) for i in range(SB)], axis=0)

    @pl.when(g == n_blocks - 1)
    def _post():
        cosr, s1, s2 = cos_ref[...], s1_ref[...], s2_ref[...]
        rq = _rope(P(RQ), cosr, s1, s2)
        rk = _rope(P(RK), cosr, s1, s2)
        ro = _dot_x2(rq * rk, ones_bd) * P(RV) + ot_ref[...].T
        ret_out = _head_norm(ro, retgn_ref[...], avg) * _silu(P(RG))
        mht = []
        for h in range(NH):
            rows = slice(h * DH, (h + 1) * DH)
            wi = wi_ref[h:h + 1, :]
            ws = ws_ref[h:h + 1, :]
            emt = jnp.exp(-mlmn_ref[h:h + 1, :])
            qt, kt, n0 = mqt_ref[rows, :], mkt_ref[rows, :], n0_ref[rows, :]
            s = jnp.sum(qt * kt, axis=0, keepdims=True) * wi
            den = s + ws * jnp.sum(n0 * qt, axis=0, keepdims=True)
            mht.append((s * mvt_ref[rows, :] + ws * cqt_ref[rows, :])
                       / jnp.maximum(jnp.abs(den), emt))
            mlnn_ref[rows, :] = ws * n0 + wi * kt
        mh = jnp.concatenate(mht, axis=0).T * _sigmoid(P(MO))
        ml_out = _head_norm(mh, mlgn_ref[...], avg) * _silu(P(MG))
        su = P(SU)
        xcat = jnp.concatenate([s5re_ref[...], s5im_ref[...]], axis=0).astype(BF16)
        sy = _dot_tn(xcat, ct_ref[...]) + dsk_ref[...] * su
        sy = _gelu_tanh(sy)
        sy = sy * _sigmoid(_dot(sy.astype(BF16), wglu_ref[...]))
        s5_out = sy * _silu(P(SG))
        xa_out = xa_ref[...] * _silu(P(AG))
        mix = jnp.concatenate([ret_out, ml_out, s5_out, xa_out], axis=1).astype(BF16)
        y = hs_ref[...] + _dot(mix, wout_ref[...])
        hs_ref[...] = y

        @pl.when(layer == n_layers - 1)
        def _emit():
            y_ref[...] = _rms_norm(y, fnw_ref[...])


def _sample_layers_t(x, st, w, consts):
    nsamp = x.shape[0]
    depth = w["w_in"].shape[0]
    nb = GW // SROWS
    assert nsamp == nb * SB and nsamp == 128
    once = lambda shape: pl.BlockSpec(shape, lambda l, g: (0,) * len(shape),
                                      pipeline_mode=pl.Buffered(1))
    lyr = lambda shape: pl.BlockSpec((None,) + shape, lambda l, g: (l,) + (0,) * len(shape))
    lyr_out = lyr
    srows = pl.BlockSpec((None, SROWS, DH, nsamp), lambda l, g: (l, g, 0, 0))
    cache = pl.BlockSpec((None, SB, GW, MEM), lambda l, g: (l, g, 0, 0))
    in_specs = [
        once((nsamp, D)), lyr((1, D)), lyr((DP, D)), lyr((D, D)),
        once((1, 128)), once((1, 128)), once((1, 128)), once((1, GW)), once((8, nsamp)),
        lyr((1, GW)), lyr((1, GW)), lyr((8, nsamp)), lyr((8, nsamp)),
        lyr((2 * NS, nsamp)), lyr((GW, 2 * NS)), lyr((2 * NS, GW)),
        lyr((1, GW)), lyr((GW, GW)), once((1, D)),
        lyr((8, nsamp)), lyr((GW, nsamp)), lyr((NS, nsamp)), lyr((NS, nsamp)),
        srows, srows, cache, cache,
    ]
    out_shape = (
        jax.ShapeDtypeStruct((nsamp, D), F32),
        jax.ShapeDtypeStruct((depth, GW, DH, nsamp), F32),
        jax.ShapeDtypeStruct((depth, GW, DH, nsamp), F32),
        jax.ShapeDtypeStruct((depth, GW, nsamp), F32),
        jax.ShapeDtypeStruct((depth, 8, nsamp), F32),
        jax.ShapeDtypeStruct((depth, NS, nsamp), F32),
        jax.ShapeDtypeStruct((depth, NS, nsamp), F32),
    )
    out_specs = (
        pl.BlockSpec((nsamp, D), lambda l, g: (0, 0)), srows, srows,
        lyr_out((GW, nsamp)), lyr_out((8, nsamp)), lyr_out((NS, nsamp)), lyr_out((NS, nsamp)),
    )
    scratch = [
        pltpu.VMEM((nsamp, D), F32),
        pltpu.VMEM((nsamp, DP), F32),
        pltpu.VMEM((GW, nsamp), F32), pltpu.VMEM((GW, nsamp), F32), pltpu.VMEM((GW, nsamp), F32),
        pltpu.VMEM((GW, nsamp), F32), pltpu.VMEM((GW, nsamp), F32), pltpu.VMEM((GW, nsamp), F32),
        pltpu.VMEM((8, nsamp), F32), pltpu.VMEM((8, nsamp), F32),
        pltpu.VMEM((GW, nsamp), F32), pltpu.VMEM((GW, nsamp), F32),
        pltpu.VMEM((8 * nsamp, GW), F32), pltpu.VMEM((nsamp, GW), F32),
    ]
    kern = functools.partial(_sample_kernel_t, n_layers=depth, n_blocks=nb)
    return pl.pallas_call(
        kern, grid=(depth, nb), in_specs=in_specs, out_specs=out_specs, out_shape=out_shape,
        scratch_shapes=scratch, name="sample_layers",
        compiler_params=pltpu.CompilerParams(
            dimension_semantics=("arbitrary", "arbitrary"), vmem_limit_bytes=VMEM_LIMIT),
    )(x, w["norm_w"], w["w_in"], w["w_out"],
      consts["cos_s"], consts["s1_s"], consts["s2_s"], consts["gam"], consts["gam8"],
      w["ret_gn"], w["ml_gn"], w["b_i8"], w["b_f8"],
      w["abt"], w["bt"], w["ct"], w["s5_d"], w["w_glu"], w["final_norm_w"],
      st["m"], st["n"], st["s5_re"], st["s5_im"], st["ret"], st["c"], st["mem_k"], st["mem_v"])


def _rope_tables(pos):
    half = DH // 2
    inv = ROPE_BASE ** (-np.arange(half, dtype=np.float64) / half)
    ang = np.asarray(pos, np.float64)[:, None] * inv[None, :]
    cos, sin = np.cos(ang), np.sin(ang)
    zero = np.zeros_like(sin)
    c = np.tile(np.concatenate([cos, cos], axis=-1), (1, 2))
    s1 = np.tile(np.concatenate([zero, sin], axis=-1), (1, 2))
    s2 = np.tile(np.concatenate([-sin, zero], axis=-1), (1, 2))
    return tuple(jnp.asarray(t, F32) for t in (c, s1, s2))


def _constants(seq):
    lg = np.log1p(-np.power(2.0, -5.0 - np.arange(NH, dtype=np.float64)))[:, None]
    idx = np.arange(L, dtype=np.float64)
    diff = idx[:, None] - idx[None, :]
    decay = np.where(diff >= 0, np.exp(lg[:, :, None] * np.maximum(diff, 0.0)), 0.0)
    rep = lambda t: np.repeat(t, DH, axis=0).T
    consts = {
        "dec": np.transpose(decay, (1, 0, 2)).reshape(L, NH * L),
        "qdec": rep(np.exp(lg * (idx + 1.0))),
        "kdec": rep(np.exp(lg * (L - 1.0 - idx))),
        "cdec": rep(np.exp(lg * L)),
        "gam": rep(np.exp(lg * 1.0)),
        "gam8": np.pad(np.broadcast_to(np.exp(lg), (NH, 128)), ((0, 8 - NH), (0, 0))),
    }
    consts = {k: jnp.asarray(v, F32) for k, v in consts.items()}
    consts["cos_p"], consts["s1_p"], consts["s2_p"] = _rope_tables(np.arange(seq))
    consts["cos_s"], consts["s1_s"], consts["s2_s"] = _rope_tables(PAST_LEN + np.arange(1))
    return consts


def _pack_w_in(w_in):
    wt = jnp.swapaxes(w_in, 1, 2)
    sizes = (GW,) * 9 + (NH, NH) + (GW,) * 4
    offs = np.concatenate([[0], np.cumsum(sizes)])
    seg = [wt[:, int(offs[i]):int(offs[i + 1]), :] for i in range(len(sizes))]
    scale = DH ** -0.5
    pad = lambda t: jnp.pad(t, ((0, 0), (0, 128 - t.shape[1]), (0, 0)))
    blocks = [pad(seg[9]), pad(seg[10]), seg[11],
              seg[0], seg[1] * scale, seg[2], seg[3],
              seg[4], seg[5] * scale, seg[6], seg[7], seg[8],
              seg[12], seg[13] * scale, seg[14]]
    return jnp.concatenate(blocks, axis=1).astype(BF16)


def kernel(x_prompt, x_sample, mem_prompt, state_ret, state_mlstm_c, state_mlstm_n, state_mlstm_m,
           state_s5_re, state_s5_im, cache_mem_k, cache_mem_v,
           norm_w, w_in, ret_gn, ml_b_i, ml_b_f, ml_gn,
           s5_a_re, s5_a_im, s5_log_dt, s5_b_re, s5_b_im, s5_c_re, s5_c_im, s5_d, s5_w_glu,
           w_mem_k, w_mem_v, w_out, final_norm_w):
    depth = norm_w.shape[0]
    bp, seq, _ = x_prompt.shape
    bs = x_sample.shape[0]
    consts = _constants(seq)
    abt, bt, ct, tab = _s5_prepare(s5_a_re, s5_a_im, s5_log_dt, s5_b_re, s5_b_im, s5_c_re, s5_c_im)
    rows8 = lambda t: jnp.pad(jnp.broadcast_to(t[:, :, None], (depth, NH, TB)),
                              ((0, 0), (0, 8 - NH), (0, 0)))
    w = {
        "norm_w": norm_w[:, None], "w_in": _pack_w_in(w_in), "w_out": w_out.astype(BF16),
        "w_mem_kv": jnp.concatenate([w_mem_k, w_mem_v], axis=-1).astype(BF16),
        "ret_gn": ret_gn[:, None], "ml_gn": ml_gn[:, None],
        "b_i8": rows8(ml_b_i), "b_f8": rows8(ml_b_f),
        "abt": abt, "tab": tab, "bt": bt, "ct": ct,
        "s5_d": s5_d[:, None], "w_glu": s5_w_glu.astype(BF16),
        "final_norm_w": final_norm_w[None],
    }

    hp = x_prompt
    states = ()
    for l in range(depth):
        hp, *states = _prompt_layer(l, hp, mem_prompt, w, consts, l == depth - 1, tuple(states))
    ret_p, mlc_p, mln_p, ms, xr, xi, mk, mv = states
    mlm_p = ms[:, :, :NH, 0]
    s5re_p = xr[:, :, 0].reshape(depth, bp, S5G, S5P)
    s5im_p = xi[:, :, 0].reshape(depth, bp, S5G, S5P)
    memk_p = mk.reshape(depth, bp, MEM, NH, DH)
    memv_p = mv.reshape(depth, bp, MEM, NH, DH)

    st = {
        "m": jnp.pad(jnp.swapaxes(state_mlstm_m, 1, 2), ((0, 0), (0, 8 - NH), (0, 0))),
        "n": jnp.transpose(state_mlstm_n, (0, 2, 3, 1)).reshape(depth, GW, bs),
        "s5_re": jnp.transpose(state_s5_re, (0, 2, 3, 1)).reshape(depth, NS, bs),
        "s5_im": jnp.transpose(state_s5_im, (0, 2, 3, 1)).reshape(depth, NS, bs),
        "ret": jnp.transpose(state_ret, (0, 2, 3, 4, 1)).reshape(depth, GW, DH, bs),
        "c": jnp.transpose(state_mlstm_c, (0, 2, 3, 4, 1)).reshape(depth, GW, DH, bs),
        "mem_k": jnp.transpose(cache_mem_k, (0, 1, 3, 4, 2)).reshape(depth, bs, GW, MEM),
        "mem_v": jnp.transpose(cache_mem_v, (0, 1, 3, 4, 2)).reshape(depth, bs, GW, MEM),
    }
    hs, rn, cn, nn, mn, sr, si = _sample_layers_t(x_sample.reshape(bs, D), st, w, consts)
    back5 = lambda t: jnp.transpose(t.reshape(depth, NH, DH, DH, bs), (0, 4, 1, 2, 3))
    back4 = lambda t, a, b: jnp.transpose(t.reshape(depth, a, b, bs), (0, 3, 1, 2))
    return (hp, hs.reshape(bs, 1, D),
            ret_p, back5(rn), mlc_p, back5(cn),
            mln_p, back4(nn, NH, DH), mlm_p, jnp.swapaxes(mn[:, :NH], 1, 2),
            s5re_p, back4(sr, S5G, S5P), s5im_p, back4(si, S5G, S5P),
            memk_p, memv_p)
```

```python
import functools
import math

import numpy as np
import jax
import jax.numpy as jnp
from jax import lax
from jax.experimental import pallas as pl
from jax.experimental.pallas import tpu as pltpu

F32 = jnp.float32
BF16 = jnp.bfloat16

D = 1024
GW = 256
NH = 4
DH = 64
L = 128
LS = 8
S5G = 16
S5P = 64
S5C = 16
NS = S5G * S5P
MEM = 256
EPS = 1e-6
NEG_INF = -1e30
ROPE_BASE = 10000.0
PAST_LEN = 16384

TB = 512
SB = 8
XQ = 512
NBLK = 14
DP = NBLK * GW
(GT, SU, RQ, RK, RV, RG, MQ, MK, MV, MO, MG, SG, AQ, AG) = range(NBLK)

VMEM_LIMIT = 56 * 1024 * 1024


_DONE = object()


def _dot(a, b):
    return jnp.dot(a, b, preferred_element_type=F32)


def _dot_nt(a, b):
    return lax.dot_general(a, b, (((1,), (1,)), ((), ())), preferred_element_type=F32)


def _dot_tn(a, b):
    return lax.dot_general(a, b, (((0,), (0,)), ((), ())), preferred_element_type=F32)


def _split2(x):
    hi = x.astype(BF16)
    lo = (x - hi.astype(F32)).astype(BF16)
    return hi, lo


def _dot_x2(x, w):
    hi, lo = _split2(x)
    return _dot(hi, w) + _dot(lo, w)


def _dot_x3(x, w):
    hi = x.astype(BF16)
    r = x - hi.astype(F32)
    mid = r.astype(BF16)
    lo = (r - mid.astype(F32)).astype(BF16)
    return _dot(hi, w) + _dot(mid, w) + _dot(lo, w)


def _sigmoid(x):
    return 0.5 * (1.0 + jnp.tanh(0.5 * x))


def _silu(x):
    return x * _sigmoid(x)


def _log_sigmoid(x):
    return jnp.minimum(x, 0.0) - jnp.log1p(jnp.exp(-jnp.abs(x)))


def _gelu_tanh(x):
    c = math.sqrt(2.0 / math.pi)
    return x * (0.5 * (1.0 + jnp.tanh(c * (x + 0.044715 * (x * x * x)))))


def _lane_head(n):
    return lax.broadcasted_iota(jnp.int32, (1, n), 1) // DH


def _head_masks():
    lh = _lane_head(GW)
    return [lh == h for h in range(NH)]


def _block_diag_mask():
    r = lax.broadcasted_iota(jnp.int32, (GW, GW), 0) // DH
    c = lax.broadcasted_iota(jnp.int32, (GW, GW), 1) // DH
    return r == c


def _avg_matrix():
    return jnp.where(_block_diag_mask(), 1.0 / DH, 0.0).astype(BF16)


def _ones_matrix():
    return jnp.where(_block_diag_mask(), 1.0, 0.0).astype(BF16)


def _rope(x, cos, s1, s2):
    outs = []
    for j in range(2):
        xs = x[:, j * 128:(j + 1) * 128]
        outs.append(xs * cos + pltpu.roll(xs, 32, 1) * s1 + pltpu.roll(xs, 96, 1) * s2)
    return jnp.concatenate(outs, axis=1)


def _head_norm(x, gain, avg):
    mu = _dot_x2(x, avg)
    d = x - mu
    var = _dot_x2(d * d, avg)
    return d * lax.rsqrt(var + EPS) * gain


def _rms_norm(x, w):
    ms = jnp.mean(x * x, axis=-1, keepdims=True)
    return x * lax.rsqrt(ms + EPS) * w


def _stack_heads(x, hm):
    return jnp.concatenate([jnp.where(hm[h], x, 0.0) for h in range(NH)], axis=0)


def _s5_prep_kernel(are_ref, aim_ref, ldt_ref, bre_ref, bim_ref, cre_ref, cim_ref,
                    abt_ref, bt_ref, ct_ref, tab_ref):
    a_re = are_ref[0]
    a_im = aim_ref[0]
    dt = jnp.exp(ldt_ref[0])
    lam_re = a_re * dt
    lam_im = a_im * dt
    mag = jnp.exp(lam_re)
    ab_re = mag * jnp.cos(lam_im)
    ab_im = mag * jnp.sin(lam_im)
    den = a_re * a_re + a_im * a_im
    nr = ab_re - 1.0
    ni = ab_im
    f_re = (nr * a_re + ni * a_im) / den
    f_im = (ni * a_re - nr * a_im) / den
    abt_ref[0, 0:NS, :] = jnp.broadcast_to(ab_re, (128, NS)).T
    abt_ref[0, NS:2 * NS, :] = jnp.broadcast_to(ab_im, (128, NS)).T
    b_re = bre_ref[0]
    b_im = bim_ref[0]
    bb_re = (f_re * b_re - f_im * b_im).astype(BF16)
    bb_im = (f_re * b_im + f_im * b_re).astype(BF16)
    rep_r = (lax.broadcasted_iota(jnp.int32, (GW, S5C), 0) % S5C
             == lax.broadcasted_iota(jnp.int32, (GW, S5C), 1)).astype(BF16)
    in_blk = (lax.broadcasted_iota(jnp.int32, (GW, NS), 0) // S5C
              == lax.broadcasted_iota(jnp.int32, (GW, NS), 1) // S5P)
    bt_ref[0, :, :NS] = jnp.where(in_blk, _dot(rep_r, bb_re), 0.0).astype(BF16)
    bt_ref[0, :, NS:] = jnp.where(in_blk, _dot(rep_r, bb_im), 0.0).astype(BF16)
    rep_c = (lax.broadcasted_iota(jnp.int32, (S5C, GW), 0)
             == lax.broadcasted_iota(jnp.int32, (S5C, GW), 1) % S5C).astype(BF16)
    out_blk = (lax.broadcasted_iota(jnp.int32, (NS, GW), 0) // S5P
               == lax.broadcasted_iota(jnp.int32, (NS, GW), 1) // S5C)
    ct_ref[0, :NS, :] = jnp.where(out_blk, _dot(cre_ref[0].astype(BF16), rep_c), 0.0).astype(BF16)
    ct_ref[0, NS:, :] = jnp.where(out_blk, -_dot(cim_ref[0].astype(BF16), rep_c), 0.0).astype(BF16)
    i = lax.broadcasted_iota(jnp.int32, (LS, NS), 0).astype(F32)
    for slot, k in ((0, -i), (2, i), (4, i + 1.0)):
        pmag = jnp.exp(k * lam_re)
        tab_ref[0, slot] = pmag * jnp.cos(k * lam_im)
        tab_ref[0, slot + 1] = pmag * jnp.sin(k * lam_im)


def _s5_prepare(a_re, a_im, log_dt, b_re, b_im, c_re, c_im):
    depth = a_re.shape[0]
    are = a_re.reshape(depth, 1, NS)
    aim = a_im.reshape(depth, 1, NS)
    ldt = jnp.repeat(log_dt, S5P, axis=-1).reshape(depth, 1, NS)
    bre = jnp.transpose(b_re, (0, 3, 1, 2)).reshape(depth, S5C, NS)
    bim = jnp.transpose(b_im, (0, 3, 1, 2)).reshape(depth, S5C, NS)
    cre = jnp.transpose(c_re, (0, 1, 3, 2)).reshape(depth, NS, S5C)
    cim = jnp.transpose(c_im, (0, 1, 3, 2)).reshape(depth, NS, S5C)
    per_layer = lambda *shape: pl.BlockSpec((1,) + shape, lambda l: (l,) + (0,) * len(shape))
    out_shape = (jax.ShapeDtypeStruct((depth, 2 * NS, 128), F32),
                 jax.ShapeDtypeStruct((depth, GW, 2 * NS), BF16),
                 jax.ShapeDtypeStruct((depth, 2 * NS, GW), BF16),
                 jax.ShapeDtypeStruct((depth, 6, LS, NS), F32))
    return pl.pallas_call(
        _s5_prep_kernel, grid=(depth,),
        in_specs=[per_layer(1, NS), per_layer(1, NS), per_layer(1, NS),
                  per_layer(S5C, NS), per_layer(S5C, NS), per_layer(NS, S5C), per_layer(NS, S5C)],
        out_specs=(per_layer(2 * NS, 128), per_layer(GW, 2 * NS), per_layer(2 * NS, GW),
                   per_layer(6, LS, NS)),
        out_shape=out_shape, name="s5_prepare",
        compiler_params=pltpu.CompilerParams(dimension_semantics=("arbitrary",)),
    )(are, aim, ldt, bre, bim, cre, cim)


N_STATE_OUT = 8


def _prompt_kernel(x_ref, mem_ref, normw_ref, win_ref, wout_ref, wmkv_ref,
                   cos_ref, s1_ref, s2_ref, dec_ref, qdec_ref, kdec_ref, cdec_ref,
                   retgn_ref, mlgn_ref, bi_ref, bf_ref,
                   tab_ref, bt_ref, ct_ref, dsk_ref, wglu_ref, fnw_ref,
                   *rest, last_layer, n_tblocks, n_prev):
    (y_ref, rets_ref, mlc_ref, mln_ref, mlm_ref, s5re_ref, s5im_ref, memk_ref, memv_ref,
     proj_ref, mix_ref, s_ref, c_ref, n_ref, m_ref, xre_ref, xim_ref,
     mk_ref, mv_ref, bu_ref, xcat_ref, car_ref) = rest[n_prev:]
    t = pl.program_id(1)
    hm = _head_masks()
    bd = _block_diag_mask()
    avg = _avg_matrix()
    lane128 = lax.broadcasted_iota(jnp.int32, (1, 128), 1)
    row_i = lax.broadcasted_iota(jnp.int32, (L, 128), 0)
    col_i = lax.broadcasted_iota(jnp.int32, (L, 128), 1)
    causal = row_i >= col_i
    tri_u = jnp.where(row_i <= col_i, 1.0, 0.0).astype(BF16)
    tri_sub = jnp.where(causal & (row_i // LS == col_i // LS), 1.0, 0.0)
    blk_sum = jnp.where(lax.broadcasted_iota(jnp.int32, (L // LS, 128), 0)
                        == lax.broadcasted_iota(jnp.int32, (L // LS, 128), 1) // LS, 1.0, 0.0)
    tri_ext = jnp.concatenate([tri_sub, blk_sum], axis=0).astype(BF16)

    @pl.when(t == 0)
    def _init():
        s_ref[...] = jnp.zeros_like(s_ref)
        c_ref[...] = jnp.zeros_like(c_ref)
        n_ref[...] = jnp.zeros_like(n_ref)
        m_ref[...] = jnp.zeros_like(m_ref)
        xre_ref[...] = jnp.zeros_like(xre_ref)
        xim_ref[...] = jnp.zeros_like(xim_ref)
        mkv = _dot(mem_ref[0].astype(BF16), wmkv_ref[...])
        mk = mkv[:, :GW]
        mv = mkv[:, GW:]
        memk_ref[0] = mk.T
        memv_ref[0] = mv.T
        mk_ref[...] = _stack_heads(mk, hm).astype(BF16)
        mv_ref[...] = _stack_heads(mv, hm).astype(BF16)

    x = x_ref[0]
    hn = _rms_norm(x, normw_ref[...]).astype(BF16)
    proj_ref[...] = _dot_nt(hn, win_ref[...])

    nc = TB // L
    crow = [slice(c * L, (c + 1) * L) for c in range(nc)]

    def PB(blk, rows=slice(None)):
        return proj_ref[rows, blk * GW:(blk + 1) * GW]


    def xattn_stages():
        for piece in range(TB // XQ):
            rs = slice(piece * XQ, (piece + 1) * XQ)
            sc = _dot_nt(PB(AQ, rs).astype(BF16), mk_ref[...])
            yield
            ps = []
            for h in range(NH):
                seg = sc[:, h * MEM:(h + 1) * MEM]
                e = jnp.exp(seg - jnp.max(seg, axis=-1, keepdims=True))
                ps.append((e / jnp.sum(e, axis=-1, keepdims=True)).astype(BF16))
            p = jnp.concatenate(ps, axis=1)
            yield
            xa = _dot(p, mv_ref[...])
            yield
            mix_ref[rs, 3 * GW:4 * GW] = (xa * _silu(PB(AG, rs))).astype(BF16)
            yield

    def s5_stages():
        lanes = [(slice(j * 128, (j + 1) * 128), slice(NS + j * 128, NS + (j + 1) * 128))
                 for j in range(NS // 128)]
        bpc = L // LS
        nblk = TB // LS
        su = PB(SU)
        bu_ref[...] = _dot(su.astype(BF16), bt_ref[...])
        yield
        for c in range(nc):
            for lre, lim in lanes:
                br = bu_ref[crow[c], lre]
                bi = bu_ref[crow[c], lim]
                wr = jnp.tile(tab_ref[0, :, lre], (bpc, 1))
                wi = jnp.tile(tab_ref[1, :, lre], (bpc, 1))
                xcat_ref[crow[c], lre] = (wr * br - wi * bi).astype(BF16)
                xcat_ref[crow[c], lim] = (wr * bi + wi * br).astype(BF16)
            yield
        for c in range(nc):
            z = _dot(tri_ext, xcat_ref[crow[c], :])
            bu_ref[crow[c], :] = z[0:L]
            car_ref[c * bpc:(c + 1) * bpc, :] = z[L:L + bpc]
            yield
        rowb = lax.broadcasted_iota(jnp.int32, (nblk, 128), 0)
        ers, eis, prs, pis, c0s = [], [], [], [], []
        for lre, lim in lanes:
            zr = car_ref[:, lre]
            zi = car_ref[:, lim]
            pr = tab_ref[2, LS - 1:LS, lre]
            pi = tab_ref[3, LS - 1:LS, lre]
            er = pr * zr - pi * zi
            ei = pr * zi + pi * zr
            pr = tab_ref[4, LS - 1:LS, lre]
            pi = tab_ref[5, LS - 1:LS, lre]
            c0r = xre_ref[0:1, lre]
            c0i = xim_ref[0:1, lre]
            ers.append(er + jnp.where(rowb == 0, pr * c0r - pi * c0i, 0.0))
            eis.append(ei + jnp.where(rowb == 0, pr * c0i + pi * c0r, 0.0))
            prs.append(pr)
            pis.append(pi)
            c0s.append((c0r, c0i))
        yield
        for k in range(nblk.bit_length() - 1):
            d = 1 << k
            for j in range(len(lanes)):
                er, ei, pr, pi = ers[j], eis[j], prs[j], pis[j]
                sr = jnp.where(rowb >= d, pltpu.roll(er, d, 0), 0.0)
                si = jnp.where(rowb >= d, pltpu.roll(ei, d, 0), 0.0)
                ers[j], eis[j] = er + pr * sr - pi * si, ei + pr * si + pi * sr
                prs[j], pis[j] = pr * pr - pi * pi, 2.0 * (pr * pi)
            yield
        for j, (lre, lim) in enumerate(lanes):
            xre_ref[:, lre] = jnp.broadcast_to(ers[j][nblk - 1:nblk, :], (8, 128))
            xim_ref[:, lre] = jnp.broadcast_to(eis[j][nblk - 1:nblk, :], (8, 128))
            cr = jnp.where(rowb == 0, c0s[j][0], pltpu.roll(ers[j], 1, 0))
            ci_ = jnp.where(rowb == 0, c0s[j][1], pltpu.roll(eis[j], 1, 0))
            ar = tab_ref[4, 0:1, lre]
            ai = tab_ref[5, 0:1, lre]
            car_ref[:, lre] = ar * cr - ai * ci_
            car_ref[:, lim] = ar * ci_ + ai * cr
        yield
        for c in range(nc):
            for lre, lim in lanes:
                cr = jnp.concatenate(
                    [jnp.broadcast_to(car_ref[c * bpc + j:c * bpc + j + 1, lre], (LS, 128))
                     for j in range(bpc)], axis=0)
                ci_ = jnp.concatenate(
                    [jnp.broadcast_to(car_ref[c * bpc + j:c * bpc + j + 1, lim], (LS, 128))
                     for j in range(bpc)], axis=0)
                zr = bu_ref[crow[c], lre] + cr
                zi = bu_ref[crow[c], lim] + ci_
                pr = jnp.tile(tab_ref[2, :, lre], (bpc, 1))
                pi = jnp.tile(tab_ref[3, :, lre], (bpc, 1))
                xcat_ref[crow[c], lre] = (pr * zr - pi * zi).astype(BF16)
                xcat_ref[crow[c], lim] = (pr * zi + pi * zr).astype(BF16)
            yield
        sy = _dot(xcat_ref[...], ct_ref[...]) + dsk_ref[...] * su
        yield
        sy = _gelu_tanh(sy)
        gate = _dot(sy.astype(BF16), wglu_ref[...])
        yield
        mix_ref[:, 2 * GW:3 * GW] = (sy * _sigmoid(gate) * _silu(PB(SG))).astype(BF16)
        yield

    def retention_stages():
        rq = _rope(PB(RQ), cos_ref[...], s1_ref[...], s2_ref[...])
        rk = _rope(PB(RK), cos_ref[...], s1_ref[...], s2_ref[...])
        rvf = PB(RV)
        rv = rvf.astype(BF16)
        rqb = rq.astype(BF16)
        kst = [_stack_heads(rk[crow[c]], hm).astype(BF16) for c in range(nc)]
        vst = [_stack_heads(rvf[crow[c]], hm).astype(BF16) for c in range(nc)]
        yield
        inner = [_dot_nt(rqb[crow[c]], kst[c]) for c in range(nc)]
        yield
        pmat = [(inner[c] * dec_ref[...]).astype(BF16) for c in range(nc)]
        kd = [(rk[crow[c]] * kdec_ref[...]).astype(BF16) for c in range(nc)]
        yield
        rloc = [_dot(pmat[c], vst[c]) for c in range(nc)]
        upd = [_dot_tn(kd[c], rv[crow[c]]) for c in range(nc)]
        yield
        st = [s_ref[...]]
        for c in range(nc):
            st.append(st[c] * cdec_ref[...] + jnp.where(bd, upd[c], 0.0))
        s_ref[...] = st[nc]
        qd = [(rq[crow[c]] * qdec_ref[...]).astype(BF16) for c in range(nc)]
        yield
        ost = [_dot(qd[c], st[c].astype(BF16)) for c in range(nc)]
        yield
        ro = jnp.concatenate([rloc[c] + ost[c] for c in range(nc)], axis=0)
        mix_ref[:, 0:GW] = (_head_norm(ro, retgn_ref[...], avg) * _silu(PB(RG))).astype(BF16)
        yield

    def mlstm_stages():
        g = PB(GT)
        ig_r = jnp.concatenate([g[crow[c], 0:128].T[0:8] for c in range(nc)], axis=1) + bi_ref[...]
        lf_r = _log_sigmoid(
            jnp.concatenate([g[crow[c], 128:256].T[0:8] for c in range(nc)], axis=1) + bf_ref[...])
        yield
        b_r = [_dot_x3(lf_r[:, crow[c]], tri_u) for c in range(nc)]
        yield
        gd_r = [ig_r[:, crow[c]] - b_r[c] for c in range(nc)]
        lane_r = lax.broadcasted_iota(jnp.int32, (8, L), 1)
        cm_r = list(gd_r)
        for k in range(7):
            d = 1 << k
            cm_r = [jnp.maximum(v, jnp.where(lane_r >= d, pltpu.roll(v, d, 1), NEG_INF))
                    for v in cm_r]
            yield
        m_prev = [m_ref[...]]
        mt_r = []
        for c in range(nc):
            mt = jnp.maximum(b_r[c] + m_prev[c], b_r[c] + cm_r[c])
            mt_r.append(mt)
            m_prev.append(jnp.broadcast_to(mt[:, L - 1:L], (8, L)))
            yield
        m_ref[...] = m_prev[nc]
        cols = []
        for c in range(nc):
            bm = b_r[c] - mt_r[c]
            ws = jnp.exp(b_r[c] + m_prev[c] - mt_r[c])
            wl = jnp.exp(gd_r[c] + jnp.broadcast_to(bm[:, L - 1:L], (8, L)))
            emt = jnp.exp(-mt_r[c])
            cols.append(jnp.concatenate([bm, ws, wl, emt, jnp.zeros((L - 32, L), F32)], axis=0).T)
        mqf = PB(MQ)
        mq = mqf.astype(BF16)
        mkf = PB(MK)
        mvf = PB(MV)
        mv_ = mvf.astype(BF16)
        kst = [_stack_heads(mkf[crow[c]], hm).astype(BF16) for c in range(nc)]
        vst = [_stack_heads(mvf[crow[c]], hm).astype(BF16) for c in range(nc)]
        yield
        sraw = [_dot_nt(mq[crow[c]], kst[c]) for c in range(nc)]
        yield
        smat, den_i = [], []
        for c in range(nc):
            parts, dens = [], []
            for h in range(NH):
                arg = jnp.where(causal, cols[c][:, h:h + 1] + gd_r[c][h:h + 1, :], NEG_INF)
                s_h = sraw[c][:, h * L:(h + 1) * L] * jnp.exp(arg)
                dens.append(jnp.sum(s_h, axis=-1, keepdims=True))
                parts.append(s_h.astype(BF16))
            den_i.append(dens)
            smat.append(jnp.concatenate(parts, axis=1))
            yield
        ones_blk = jnp.ones((L, 128), BF16)
        def per_head(columns):
            return jnp.where(hm[0], columns[0],
                             jnp.where(hm[1], columns[1], jnp.where(hm[2], columns[2], columns[3])))

        kws = [(mkf[crow[c]] * per_head([cols[c][:, 16 + h:17 + h] for h in range(NH)])
                ).astype(BF16) for c in range(nc)]
        yield
        rloc = [_dot(smat[c], vst[c]) for c in range(nc)]
        u = [_dot_tn(kws[c], jnp.concatenate([mv_[crow[c]], ones_blk], axis=1))
             for c in range(nc)]
        yield
        nmask = (lax.broadcasted_iota(jnp.int32, (GW, 128), 0) // DH
                 == lax.broadcasted_iota(jnp.int32, (GW, 128), 1))
        cst = [c_ref[...]]
        nst = [n_ref[...]]
        for c in range(nc):
            wsl256 = jnp.zeros((1, GW), F32)
            wsl128 = jnp.zeros((1, 128), F32)
            for h in range(NH):
                wsl = cols[c][L - 1:L, 8 + h:9 + h]
                wsl256 = wsl256 + jnp.where(hm[h], wsl, 0.0)
                wsl128 = wsl128 + jnp.where(lane128 == h, wsl, 0.0)
            cst.append(cst[c] * wsl256 + jnp.where(bd, u[c][:, :GW], 0.0))
            nst.append(nst[c] * wsl128 + jnp.where(nmask, u[c][:, GW:], 0.0))
        c_ref[...] = cst[nc]
        n_ref[...] = nst[nc]
        yield
        qc = [_dot(mq[crow[c]], cst[c].astype(BF16)) for c in range(nc)]
        qn = [_dot(mq[crow[c]], nst[c].astype(BF16)) for c in range(nc)]
        yield
        mhs = []
        for c in range(nc):
            wsc = [cols[c][:, 8 + h:9 + h] for h in range(NH)]
            rdn = []
            for h in range(NH):
                den = den_i[c][h] + wsc[h] * qn[c][:, h:h + 1]
                rdn.append(1.0 / jnp.maximum(jnp.abs(den), cols[c][:, 24 + h:25 + h]))
            mhs.append((rloc[c] + per_head(wsc) * qc[c]) * per_head(rdn))
            yield
        mh = jnp.concatenate(mhs, axis=0) * _sigmoid(PB(MO))
        mix_ref[:, GW:2 * GW] = (_head_norm(mh, mlgn_ref[...], avg) * _silu(PB(MG))).astype(BF16)
        yield

    pending = [mlstm_stages(), s5_stages(), retention_stages(), xattn_stages()]
    while pending:
        pending = [stage for stage in pending if next(stage, _DONE) is not _DONE]

    y = x_ref[0] + _dot(mix_ref[...], wout_ref[...])
    if last_layer:
        y = _rms_norm(y, fnw_ref[...])
    y_ref[0] = y

    @pl.when(t == n_tblocks - 1)
    def _final():
        s_fin = s_ref[...]
        c_fin = c_ref[...].T
        n_fin = n_ref[...].T
        for h in range(NH):
            blk = slice(h * DH, (h + 1) * DH)
            rets_ref[0, h] = s_fin[blk, blk]
            mlc_ref[0, h] = c_fin[blk, blk]
        mln_ref[0] = jnp.concatenate([n_fin[h:h + 1, h * DH:(h + 1) * DH] for h in range(NH)],
                                     axis=0)
        mlm_ref[0] = m_ref[...]
        s5re_ref[0] = xre_ref[...]
        s5im_ref[0] = xim_ref[...]


def _prompt_layer(layer, x, mem, w, consts, last_layer, prev):
    bsz, seq, _ = x.shape
    depth = w["w_in"].shape[0]
    nt = seq // TB
    full = lambda shape: pl.BlockSpec(shape, lambda b, t: (0,) * len(shape),
                                      pipeline_mode=pl.Buffered(1))
    lyr = lambda shape: pl.BlockSpec((None,) + shape, lambda b, t: (layer,) + (0,) * len(shape),
                                     pipeline_mode=pl.Buffered(1))
    tok = lambda width: pl.BlockSpec((TB, width), lambda b, t: (t, 0))
    per_b = lambda r, c: pl.BlockSpec((1, r, c), lambda b, t: (b, 0, 0))
    in_specs = [
        pl.BlockSpec((1, TB, D), lambda b, t: (b, t, 0)),
        per_b(MEM, D),
        lyr((1, D)), lyr((DP, D)), lyr((D, D)), lyr((D, 2 * GW)),
        tok(128), tok(128), tok(128),
        full((L, NH * L)), full((L, GW)), full((L, GW)), full((1, GW)),
        lyr((1, GW)), lyr((1, GW)), lyr((8, TB)), lyr((8, TB)),
        lyr((6, LS, NS)), lyr((GW, 2 * NS)), lyr((2 * NS, GW)),
        lyr((1, GW)), lyr((GW, GW)), full((1, D)),
    ] + [pl.BlockSpec(memory_space=pl.ANY)] * len(prev)
    n_in = len(in_specs) - len(prev)
    state_shapes = [(NH, DH, DH),
                    (NH, DH, DH),
                    (NH, DH),
                    (8, 128),
                    (8, NS),
                    (8, NS),
                    (GW, MEM),
                    (GW, MEM)]
    assert len(state_shapes) == N_STATE_OUT and len(prev) in (0, N_STATE_OUT)
    out_shape = ((jax.ShapeDtypeStruct((bsz, seq, D), F32),)
                 + tuple(jax.ShapeDtypeStruct((depth, bsz) + s, F32) for s in state_shapes))
    state_spec = lambda s: pl.BlockSpec((None, 1) + s,
                                        lambda b, t: (layer, b) + (0,) * len(s))
    out_specs = ((pl.BlockSpec((1, TB, D), lambda b, t: (b, t, 0)),)
                 + tuple(state_spec(s) for s in state_shapes))
    scratch = [
        pltpu.VMEM((TB, DP), F32),
        pltpu.VMEM((TB, D), BF16),
        pltpu.VMEM((GW, GW), F32),
        pltpu.VMEM((GW, GW), F32),
        pltpu.VMEM((GW, 128), F32),
        pltpu.VMEM((8, 128), F32),
        pltpu.VMEM((8, NS), F32),
        pltpu.VMEM((8, NS), F32),
        pltpu.VMEM((NH * MEM, GW), BF16),
        pltpu.VMEM((NH * MEM, GW), BF16),
        pltpu.VMEM((TB, 2 * NS), F32),
        pltpu.VMEM((TB, 2 * NS), BF16),
        pltpu.VMEM((TB // LS, 2 * NS), F32),
    ]
    kern = functools.partial(_prompt_kernel, last_layer=last_layer, n_tblocks=nt, n_prev=len(prev))
    return pl.pallas_call(
        kern, grid=(bsz, nt), in_specs=in_specs, out_specs=out_specs, out_shape=out_shape,
        scratch_shapes=scratch, name="prompt_layer",
        input_output_aliases={n_in + i: 1 + i for i in range(len(prev))},
        compiler_params=pltpu.CompilerParams(
            dimension_semantics=("arbitrary", "arbitrary"), vmem_limit_bytes=VMEM_LIMIT),
    )(x, mem, w["norm_w"], w["w_in"], w["w_out"], w["w_mem_kv"],
      consts["cos_p"], consts["s1_p"], consts["s2_p"],
      consts["dec"], consts["qdec"], consts["kdec"], consts["cdec"],
      w["ret_gn"], w["ml_gn"], w["b_i8"], w["b_f8"],
      w["tab"], w["bt"], w["ct"], w["s5_d"], w["w_glu"], w["final_norm_w"], *prev)


SROWS = 16


def _sample_kernel_t(x_ref, normw_ref, win_ref, wout_ref, cos_ref, s1_ref, s2_ref, gam_ref, gam8_ref,
                     retgn_ref, mlgn_ref, bi_ref, bf_ref,
                     abt_ref, bt_ref, ct_ref, dsk_ref, wglu_ref, fnw_ref,
                     m0_ref, n0_ref, x0re_ref, x0im_ref,
                     rets_ref, mlc_ref, ck_ref, cv_ref,
                     y_ref, retn_ref, mlcn_ref, mlnn_ref, mlmn_ref, s5re_ref, s5im_ref,
                     hs_ref, proj_ref, qgt_ref, rkt_ref, rvt_ref, mqt_ref, mkt_ref, mvt_ref,
                     wi_ref, ws_ref, ot_ref, cqt_ref, qa8_ref, xa_ref,
                     *, n_layers, n_blocks):
    layer = pl.program_id(0)
    g = pl.program_id(1)
    nsamp = x_ref.shape[0]
    hm = _head_masks()
    ones_bd = _ones_matrix()
    avg = _avg_matrix()
    mask8 = (lax.broadcasted_iota(jnp.int32, (8, GW), 0)
             == lax.broadcasted_iota(jnp.int32, (8, GW), 1) // DH)

    def P(blk):
        return proj_ref[:, blk * GW:(blk + 1) * GW]

    @pl.when((layer == 0) & (g == 0))
    def _load_x():
        hs_ref[...] = x_ref[...]

    @pl.when(g == 0)
    def _pre():
        hn = _rms_norm(hs_ref[...], normw_ref[...]).astype(BF16)
        proj_ref[...] = _dot_nt(hn, win_ref[...])
        cosr, s1, s2 = cos_ref[...], s1_ref[...], s2_ref[...]
        qgt_ref[...] = (_rope(P(RQ), cosr, s1, s2) * gam_ref[...]).T
        rkt_ref[...] = _rope(P(RK), cosr, s1, s2).T
        rvt_ref[...] = P(RV).T
        mqt_ref[...] = P(MQ).T
        mkt_ref[...] = P(MK).T
        mvt_ref[...] = P(MV).T
        gt = P(GT)
        ig = gt[:, 0:128].T[0:8] + bi_ref[...]
        lf = _log_sigmoid(gt[:, 128:256].T[0:8] + bf_ref[...])
        a = lf + m0_ref[...]
        mt = jnp.maximum(a, ig)
        wi_ref[...] = jnp.exp(ig - mt)
        ws_ref[...] = jnp.exp(a - mt)
        mlmn_ref[...] = mt
        ot_ref[...] = jnp.zeros_like(ot_ref)
        qa8_ref[...] = jnp.where(mask8[None], P(AQ)[:, None, :], 0.0).reshape(8 * nsamp, GW)
        but = _dot_tn(bt_ref[...], P(SU).T.astype(BF16))
        are, aim = abt_ref[0:NS, :], abt_ref[NS:2 * NS, :]
        x0r, x0i = x0re_ref[...], x0im_ref[...]
        s5re_ref[...] = are * x0r - aim * x0i + but[0:NS]
        s5im_ref[...] = are * x0i + aim * x0r + but[NS:2 * NS]

    head = g // (GW // SROWS // NH)
    hrow = pl.ds(pl.multiple_of(head * DH, DH), DH)
    gam_row = gam8_ref[pl.ds(head, 1), :]
    ws_row = ws_ref[pl.ds(head, 1), :]
    wi_row = wi_ref[pl.ds(head, 1), :]
    v_slab = rvt_ref[hrow, :]
    q_slab = mqt_ref[hrow, :]
    k_slab = mkt_ref[hrow, :]
    o_acc = jnp.zeros((DH, nsamp), F32)
    for i in range(SROWS):
        r = pl.ds(g * SROWS + i, 1)
        s_t = rets_ref[i]
        o_acc = o_acc + qgt_ref[r, :] * s_t
        retn_ref[i] = gam_row * s_t + rkt_ref[r, :] * v_slab
        c_t = mlc_ref[i]
        cqt_ref[r, :] = jnp.sum(c_t * q_slab, axis=0, keepdims=True)
        mlcn_ref[i] = ws_row * c_t + (wi_row * mvt_ref[r, :]) * k_slab
    ot_ref[hrow, :] = ot_ref[hrow, :] + o_acc

    tiles = [pl.ds(pl.multiple_of((g * SB + i) * 8, 8), 8) for i in range(SB)]
    q8 = [qa8_ref[tiles[i], :].astype(BF16) for i in range(SB)]
    sc = [_dot(q8[i], ck_ref[i].astype(BF16)) for i in range(SB)]
    ps = []
    for i in range(SB):
        e = jnp.exp(sc[i] - jnp.max(sc[i], axis=-1, keepdims=True))
        ps.append((e / jnp.sum(e, axis=-1, keepdims=True)).astype(BF16))
    ta = [_dot_nt(ps[i], cv_ref[i].astype(BF16)) for i in range(SB)]
    xa_ref[pl.ds(pl.multiple_of(g * SB, SB), SB), :] = jnp.concatenate(
        [jnp.sum(jnp.where(mask8, ta[i], 0.0), axis=0, keepdims=True) for i in range(SB)], axis=0)

    @pl.when(g == n_blocks - 1)
    def _post():
        cosr, s1, s2 = cos_ref[...], s1_ref[...], s2_ref[...]
        rq = _rope(P(RQ), cosr, s1, s2)
        rk = _rope(P(RK), cosr, s1, s2)
        ro = _dot_x2(rq * rk, ones_bd) * P(RV) + ot_ref[...].T
        ret_out = _head_norm(ro, retgn_ref[...], avg) * _silu(P(RG))
        mht = []
        for h in range(NH):
            rows = slice(h * DH, (h + 1) * DH)
            wi = wi_ref[h:h + 1, :]
            ws = ws_ref[h:h + 1, :]
            emt = jnp.exp(-mlmn_ref[h:h + 1, :])
            qt, kt, n0 = mqt_ref[rows, :], mkt_ref[rows, :], n0_ref[rows, :]
            s = jnp.sum(qt * kt, axis=0, keepdims=True) * wi
            den = s + ws * jnp.sum(n0 * qt, axis=0, keepdims=True)
            mht.append((s * mvt_ref[rows, :] + ws * cqt_ref[rows, :])
                       / jnp.maximum(jnp.abs(den), emt))
            mlnn_ref[rows, :] = ws * n0 + wi * kt
        mh = jnp.concatenate(mht, axis=0).T * _sigmoid(P(MO))
        ml_out = _head_norm(mh, mlgn_ref[...], avg) * _silu(P(MG))
        su = P(SU)
        xcat = jnp.concatenate([s5re_ref[...], s5im_ref[...]], axis=0).astype(BF16)
        sy = _dot_tn(xcat, ct_ref[...]) + dsk_ref[...] * su
        sy = _gelu_tanh(sy)
        sy = sy * _sigmoid(_dot(sy.astype(BF16), wglu_ref[...]))
        s5_out = sy * _silu(P(SG))
        xa_out = xa_ref[...] * _silu(P(AG))
        mix = jnp.concatenate([ret_out, ml_out, s5_out, xa_out], axis=1).astype(BF16)
        y = hs_ref[...] + _dot(mix, wout_ref[...])
        hs_ref[...] = y

        @pl.when(layer == n_layers - 1)
        def _emit():
            y_ref[...] = _rms_norm(y, fnw_ref[...])


def _sample_layers_t(x, st, w, consts):
    nsamp = x.shape[0]
    depth = w["w_in"].shape[0]
    nb = GW // SROWS
    assert nsamp == nb * SB and nsamp == 128
    once = lambda shape: pl.BlockSpec(shape, lambda l, g: (0,) * len(shape),
                                      pipeline_mode=pl.Buffered(1))
    lyr = lambda shape: pl.BlockSpec((None,) + shape, lambda l, g: (l,) + (0,) * len(shape))
    lyr_out = lyr
    srows = pl.BlockSpec((None, SROWS, DH, nsamp), lambda l, g: (l, g, 0, 0))
    cache = pl.BlockSpec((None, SB, GW, MEM), lambda l, g: (l, g, 0, 0))
    in_specs = [
        once((nsamp, D)), lyr((1, D)), lyr((DP, D)), lyr((D, D)),
        once((1, 128)), once((1, 128)), once((1, 128)), once((1, GW)), once((8, nsamp)),
        lyr((1, GW)), lyr((1, GW)), lyr((8, nsamp)), lyr((8, nsamp)),
        lyr((2 * NS, nsamp)), lyr((GW, 2 * NS)), lyr((2 * NS, GW)),
        lyr((1, GW)), lyr((GW, GW)), once((1, D)),
        lyr((8, nsamp)), lyr((GW, nsamp)), lyr((NS, nsamp)), lyr((NS, nsamp)),
        srows, srows, cache, cache,
    ]
    out_shape = (
        jax.ShapeDtypeStruct((nsamp, D), F32),
        jax.ShapeDtypeStruct((depth, GW, DH, nsamp), F32),
        jax.ShapeDtypeStruct((depth, GW, DH, nsamp), F32),
        jax.ShapeDtypeStruct((depth, GW, nsamp), F32),
        jax.ShapeDtypeStruct((depth, 8, nsamp), F32),
        jax.ShapeDtypeStruct((depth, NS, nsamp), F32),
        jax.ShapeDtypeStruct((depth, NS, nsamp), F32),
    )
    out_specs = (
        pl.BlockSpec((nsamp, D), lambda l, g: (0, 0)), srows, srows,
        lyr_out((GW, nsamp)), lyr_out((8, nsamp)), lyr_out((NS, nsamp)), lyr_out((NS, nsamp)),
    )
    scratch = [
        pltpu.VMEM((nsamp, D), F32),
        pltpu.VMEM((nsamp, DP), F32),
        pltpu.VMEM((GW, nsamp), F32), pltpu.VMEM((GW, nsamp), F32), pltpu.VMEM((GW, nsamp), F32),
        pltpu.VMEM((GW, nsamp), F32), pltpu.VMEM((GW, nsamp), F32), pltpu.VMEM((GW, nsamp), F32),
        pltpu.VMEM((8, nsamp), F32), pltpu.VMEM((8, nsamp), F32),
        pltpu.VMEM((GW, nsamp), F32), pltpu.VMEM((GW, nsamp), F32),
        pltpu.VMEM((8 * nsamp, GW), F32), pltpu.VMEM((nsamp, GW), F32),
    ]
    kern = functools.partial(_sample_kernel_t, n_layers=depth, n_blocks=nb)
    return pl.pallas_call(
        kern, grid=(depth, nb), in_specs=in_specs, out_specs=out_specs, out_shape=out_shape,
        scratch_shapes=scratch, name="sample_layers",
        compiler_params=pltpu.CompilerParams(
            dimension_semantics=("arbitrary", "arbitrary"), vmem_limit_bytes=VMEM_LIMIT),
    )(x, w["norm_w"], w["w_in"], w["w_out"],
      consts["cos_s"], consts["s1_s"], consts["s2_s"], consts["gam"], consts["gam8"],
      w["ret_gn"], w["ml_gn"], w["b_i8"], w["b_f8"],
      w["abt"], w["bt"], w["ct"], w["s5_d"], w["w_glu"], w["final_norm_w"],
      st["m"], st["n"], st["s5_re"], st["s5_im"], st["ret"], st["c"], st["mem_k"], st["mem_v"])


def _rope_tables(pos):
    half = DH // 2
    inv = ROPE_BASE ** (-np.arange(half, dtype=np.float64) / half)
    ang = np.asarray(pos, np.float64)[:, None] * inv[None, :]
    cos, sin = np.cos(ang), np.sin(ang)
    zero = np.zeros_like(sin)
    c = np.tile(np.concatenate([cos, cos], axis=-1), (1, 2))
    s1 = np.tile(np.concatenate([zero, sin], axis=-1), (1, 2))
    s2 = np.tile(np.concatenate([-sin, zero], axis=-1), (1, 2))
    return tuple(jnp.asarray(t, F32) for t in (c, s1, s2))


def _constants(seq):
    lg = np.log1p(-np.power(2.0, -5.0 - np.arange(NH, dtype=np.float64)))[:, None]
    idx = np.arange(L, dtype=np.float64)
    diff = idx[:, None] - idx[None, :]
    decay = np.where(diff >= 0, np.exp(lg[:, :, None] * np.maximum(diff, 0.0)), 0.0)
    rep = lambda t: np.repeat(t, DH, axis=0).T
    consts = {
        "dec": np.transpose(decay, (1, 0, 2)).reshape(L, NH * L),
        "qdec": rep(np.exp(lg * (idx + 1.0))),
        "kdec": rep(np.exp(lg * (L - 1.0 - idx))),
        "cdec": rep(np.exp(lg * L)),
        "gam": rep(np.exp(lg * 1.0)),
        "gam8": np.pad(np.broadcast_to(np.exp(lg), (NH, 128)), ((0, 8 - NH), (0, 0))),
    }
    consts = {k: jnp.asarray(v, F32) for k, v in consts.items()}
    consts["cos_p"], consts["s1_p"], consts["s2_p"] = _rope_tables(np.arange(seq))
    consts["cos_s"], consts["s1_s"], consts["s2_s"] = _rope_tables(PAST_LEN + np.arange(1))
    return consts


def _pack_w_in(w_in):
    wt = jnp.swapaxes(w_in, 1, 2)
    sizes = (GW,) * 9 + (NH, NH) + (GW,) * 4
    offs = np.concatenate([[0], np.cumsum(sizes)])
    seg = [wt[:, int(offs[i]):int(offs[i + 1]), :] for i in range(len(sizes))]
    scale = DH ** -0.5
    pad = lambda t: jnp.pad(t, ((0, 0), (0, 128 - t.shape[1]), (0, 0)))
    blocks = [pad(seg[9]), pad(seg[10]), seg[11],
              seg[0], seg[1] * scale, seg[2], seg[3],
              seg[4], seg[5] * scale, seg[6], seg[7], seg[8],
              seg[12], seg[13] * scale, seg[14]]
    return jnp.concatenate(blocks, axis=1).astype(BF16)


def kernel(x_prompt, x_sample, mem_prompt, state_ret, state_mlstm_c, state_mlstm_n, state_mlstm_m,
           state_s5_re, state_s5_im, cache_mem_k, cache_mem_v,
           norm_w, w_in, ret_gn, ml_b_i, ml_b_f, ml_gn,
           s5_a_re, s5_a_im, s5_log_dt, s5_b_re, s5_b_im, s5_c_re, s5_c_im, s5_d, s5_w_glu,
           w_mem_k, w_mem_v, w_out, final_norm_w):
    depth = norm_w.shape[0]
    bp, seq, _ = x_prompt.shape
    bs = x_sample.shape[0]
    consts = _constants(seq)
    abt, bt, ct, tab = _s5_prepare(s5_a_re, s5_a_im, s5_log_dt, s5_b_re, s5_b_im, s5_c_re, s5_c_im)
    rows8 = lambda t: jnp.pad(jnp.broadcast_to(t[:, :, None], (depth, NH, TB)),
                              ((0, 0), (0, 8 - NH), (0, 0)))
    w = {
        "norm_w": norm_w[:, None], "w_in": _pack_w_in(w_in), "w_out": w_out.astype(BF16),
        "w_mem_kv": jnp.concatenate([w_mem_k, w_mem_v], axis=-1).astype(BF16),
        "ret_gn": ret_gn[:, None], "ml_gn": ml_gn[:, None],
        "b_i8": rows8(ml_b_i), "b_f8": rows8(ml_b_f),
        "abt": abt, "tab": tab, "bt": bt, "ct": ct,
        "s5_d": s5_d[:, None], "w_glu": s5_w_glu.astype(BF16),
        "final_norm_w": final_norm_w[None],
    }

    hp = x_prompt
    states = ()
    for l in range(depth):
        hp, *states = _prompt_layer(l, hp, mem_prompt, w, consts, l == depth - 1, tuple(states))
    ret_p, mlc_p, mln_p, ms, xr, xi, mk, mv = states
    mlm_p = ms[:, :, :NH, 0]
    s5re_p = xr[:, :, 0].reshape(depth, bp, S5G, S5P)
    s5im_p = xi[:, :, 0].reshape(depth, bp, S5G, S5P)
    memk_p = jnp.transpose(mk.reshape(depth, bp, NH, DH, MEM), (0, 1, 4, 2, 3))
    memv_p = jnp.transpose(mv.reshape(depth, bp, NH, DH, MEM), (0, 1, 4, 2, 3))

    st = {
        "m": jnp.pad(jnp.swapaxes(state_mlstm_m, 1, 2), ((0, 0), (0, 8 - NH), (0, 0))),
        "n": jnp.transpose(state_mlstm_n, (0, 2, 3, 1)).reshape(depth, GW, bs),
        "s5_re": jnp.transpose(state_s5_re, (0, 2, 3, 1)).reshape(depth, NS, bs),
        "s5_im": jnp.transpose(state_s5_im, (0, 2, 3, 1)).reshape(depth, NS, bs),
        "ret": jnp.transpose(state_ret, (0, 2, 3, 4, 1)).reshape(depth, GW, DH, bs),
        "c": jnp.transpose(state_mlstm_c, (0, 2, 3, 4, 1)).reshape(depth, GW, DH, bs),
        "mem_k": jnp.transpose(cache_mem_k, (0, 1, 3, 4, 2)).reshape(depth, bs, GW, MEM),
        "mem_v": jnp.transpose(cache_mem_v, (0, 1, 3, 4, 2)).reshape(depth, bs, GW, MEM),
    }
    hs, rn, cn, nn, mn, sr, si = _sample_layers_t(x_sample.reshape(bs, D), st, w, consts)
    back5 = lambda t: jnp.transpose(t.reshape(depth, NH, DH, DH, bs), (0, 4, 1, 2, 3))
    back4 = lambda t, a, b: jnp.transpose(t.reshape(depth, a, b, bs), (0, 3, 1, 2))
    return (hp, hs.reshape(bs, 1, D),
            ret_p, back5(rn), mlc_p, back5(cn),
            mln_p, back4(nn, NH, DH), mlm_p, jnp.swapaxes(mn[:, :NH], 1, 2),
            s5re_p, back4(sr, S5G, S5P), s5im_p, back4(si, S5G, S5P),
            memk_p, memv_p)
```

```python
import functools
import math

import numpy as np
import jax
import jax.numpy as jnp
from jax import lax
from jax.experimental import pallas as pl
from jax.experimental.pallas import tpu as pltpu

F32 = jnp.float32
BF16 = jnp.bfloat16

D = 1024
GW = 256
NH = 4
DH = 64
L = 128
LS = 8
S5G = 16
S5P = 64
S5C = 16
NS = S5G * S5P
MEM = 256
EPS = 1e-6
NEG_INF = -1e30
ROPE_BASE = 10000.0
PAST_LEN = 16384

TB = 512
SB = 8
XQ = 512
NBLK = 13
DP = NBLK * GW
(SU, RQ, RK, RV, RG, MQ, MK, MV, MO, MG, SG, AQ, AG) = range(NBLK)
NGATE = 16

VMEM_LIMIT = 56 * 1024 * 1024


_DONE = object()


def _dot(a, b):
    return jnp.dot(a, b, preferred_element_type=F32)


def _dot_nt(a, b):
    return lax.dot_general(a, b, (((1,), (1,)), ((), ())), preferred_element_type=F32)


def _dot_tn(a, b):
    return lax.dot_general(a, b, (((0,), (0,)), ((), ())), preferred_element_type=F32)


def _split2(x):
    hi = x.astype(BF16)
    lo = (x - hi.astype(F32)).astype(BF16)
    return hi, lo


def _dot_x2(x, w):
    hi, lo = _split2(x)
    return _dot(hi, w) + _dot(lo, w)


def _dot_x3(x, w):
    hi = x.astype(BF16)
    r = x - hi.astype(F32)
    mid = r.astype(BF16)
    lo = (r - mid.astype(F32)).astype(BF16)
    return _dot(hi, w) + _dot(mid, w) + _dot(lo, w)


def _sigmoid(x):
    return 0.5 * (1.0 + jnp.tanh(0.5 * x))


def _silu(x):
    return x * _sigmoid(x)


def _log_sigmoid(x):
    return jnp.minimum(x, 0.0) - jnp.log1p(jnp.exp(-jnp.abs(x)))


def _gelu_tanh(x):
    c = math.sqrt(2.0 / math.pi)
    return x * (0.5 * (1.0 + jnp.tanh(c * (x + 0.044715 * (x * x * x)))))


def _lane_head(n):
    return lax.broadcasted_iota(jnp.int32, (1, n), 1) // DH


def _head_masks():
    lh = _lane_head(GW)
    return [lh == h for h in range(NH)]


def _block_diag_mask():
    r = lax.broadcasted_iota(jnp.int32, (GW, GW), 0) // DH
    c = lax.broadcasted_iota(jnp.int32, (GW, GW), 1) // DH
    return r == c


def _avg_matrix():
    return jnp.where(_block_diag_mask(), 1.0 / DH, 0.0).astype(BF16)


def _ones_matrix():
    return jnp.where(_block_diag_mask(), 1.0, 0.0).astype(BF16)


def _rope(x, cos, s1, s2):
    outs = []
    for j in range(2):
        xs = x[:, j * 128:(j + 1) * 128]
        outs.append(xs * cos + pltpu.roll(xs, 32, 1) * s1 + pltpu.roll(xs, 96, 1) * s2)
    return jnp.concatenate(outs, axis=1)


def _head_norm(x, gain, avg):
    mu = _dot_x2(x, avg)
    d = x - mu
    var = _dot_x2(d * d, avg)
    return d * lax.rsqrt(var + EPS) * gain


def _rms_norm(x, w):
    ms = jnp.mean(x * x, axis=-1, keepdims=True)
    return x * lax.rsqrt(ms + EPS) * w


def _stack_heads(x, hm):
    return jnp.concatenate([jnp.where(hm[h], x, 0.0) for h in range(NH)], axis=0)


def _s5_prep_kernel(are_ref, aim_ref, ldt_ref, bre_ref, bim_ref, cre_ref, cim_ref,
                    abt_ref, bt_ref, ct_ref, tab_ref):
    a_re = are_ref[0]
    a_im = aim_ref[0]
    dt = jnp.exp(ldt_ref[0])
    lam_re = a_re * dt
    lam_im = a_im * dt
    mag = jnp.exp(lam_re)
    ab_re = mag * jnp.cos(lam_im)
    ab_im = mag * jnp.sin(lam_im)
    den = a_re * a_re + a_im * a_im
    nr = ab_re - 1.0
    ni = ab_im
    f_re = (nr * a_re + ni * a_im) / den
    f_im = (ni * a_re - nr * a_im) / den
    abt_ref[0, 0:NS, :] = jnp.broadcast_to(ab_re, (128, NS)).T
    abt_ref[0, NS:2 * NS, :] = jnp.broadcast_to(ab_im, (128, NS)).T
    b_re = bre_ref[0]
    b_im = bim_ref[0]
    bb_re = (f_re * b_re - f_im * b_im).astype(BF16)
    bb_im = (f_re * b_im + f_im * b_re).astype(BF16)
    rep_r = (lax.broadcasted_iota(jnp.int32, (GW, S5C), 0) % S5C
             == lax.broadcasted_iota(jnp.int32, (GW, S5C), 1)).astype(BF16)
    in_blk = (lax.broadcasted_iota(jnp.int32, (GW, NS), 0) // S5C
              == lax.broadcasted_iota(jnp.int32, (GW, NS), 1) // S5P)
    bt_ref[0, :, :NS] = jnp.where(in_blk, _dot(rep_r, bb_re), 0.0).astype(BF16)
    bt_ref[0, :, NS:] = jnp.where(in_blk, _dot(rep_r, bb_im), 0.0).astype(BF16)
    rep_c = (lax.broadcasted_iota(jnp.int32, (S5C, GW), 0)
             == lax.broadcasted_iota(jnp.int32, (S5C, GW), 1) % S5C).astype(BF16)
    out_blk = (lax.broadcasted_iota(jnp.int32, (NS, GW), 0) // S5P
               == lax.broadcasted_iota(jnp.int32, (NS, GW), 1) // S5C)
    ct_ref[0, :NS, :] = jnp.where(out_blk, _dot(cre_ref[0].astype(BF16), rep_c), 0.0).astype(BF16)
    ct_ref[0, NS:, :] = jnp.where(out_blk, -_dot(cim_ref[0].astype(BF16), rep_c), 0.0).astype(BF16)
    i = lax.broadcasted_iota(jnp.int32, (LS, NS), 0).astype(F32)
    for slot, k in ((0, -i), (2, i), (4, i + 1.0)):
        pmag = jnp.exp(k * lam_re)
        tab_ref[0, slot] = pmag * jnp.cos(k * lam_im)
        tab_ref[0, slot + 1] = pmag * jnp.sin(k * lam_im)


def _s5_prepare(a_re, a_im, log_dt, b_re, b_im, c_re, c_im):
    depth = a_re.shape[0]
    are = a_re.reshape(depth, 1, NS)
    aim = a_im.reshape(depth, 1, NS)
    ldt = jnp.repeat(log_dt, S5P, axis=-1).reshape(depth, 1, NS)
    bre = jnp.transpose(b_re, (0, 3, 1, 2)).reshape(depth, S5C, NS)
    bim = jnp.transpose(b_im, (0, 3, 1, 2)).reshape(depth, S5C, NS)
    cre = jnp.transpose(c_re, (0, 1, 3, 2)).reshape(depth, NS, S5C)
    cim = jnp.transpose(c_im, (0, 1, 3, 2)).reshape(depth, NS, S5C)
    per_layer = lambda *shape: pl.BlockSpec((1,) + shape, lambda l: (l,) + (0,) * len(shape))
    out_shape = (jax.ShapeDtypeStruct((depth, 2 * NS, 128), F32),
                 jax.ShapeDtypeStruct((depth, GW, 2 * NS), BF16),
                 jax.ShapeDtypeStruct((depth, 2 * NS, GW), BF16),
                 jax.ShapeDtypeStruct((depth, 6, LS, NS), F32))
    return pl.pallas_call(
        _s5_prep_kernel, grid=(depth,),
        in_specs=[per_layer(1, NS), per_layer(1, NS), per_layer(1, NS),
                  per_layer(S5C, NS), per_layer(S5C, NS), per_layer(NS, S5C), per_layer(NS, S5C)],
        out_specs=(per_layer(2 * NS, 128), per_layer(GW, 2 * NS), per_layer(2 * NS, GW),
                   per_layer(6, LS, NS)),
        out_shape=out_shape, name="s5_prepare",
        compiler_params=pltpu.CompilerParams(dimension_semantics=("arbitrary",)),
    )(are, aim, ldt, bre, bim, cre, cim)


N_STATE_OUT = 8


def _prompt_kernel(x_ref, mem_ref, normw_ref, win_ref, wg_ref, wout_ref, wmkv_ref,
                   cos_ref, s1_ref, s2_ref, dec_ref, qdec_ref, kdec_ref, cdec_ref,
                   retgn_ref, mlgn_ref, bi_ref, bf_ref,
                   tab_ref, bt_ref, ct_ref, dsk_ref, wglu_ref, fnw_ref,
                   *rest, last_layer, n_tblocks, n_prev):
    (y_ref, rets_ref, mlc_ref, mln_ref, mlm_ref, s5re_ref, s5im_ref, memk_ref, memv_ref,
     proj_ref, mix_ref, s_ref, c_ref, n_ref, m_ref, xre_ref, xim_ref,
     mk_ref, mv_ref, bu_ref, xcat_ref, car_ref) = rest[n_prev:]
    t = pl.program_id(1)
    hm = _head_masks()
    bd = _block_diag_mask()
    avg = _avg_matrix()
    lane128 = lax.broadcasted_iota(jnp.int32, (1, 128), 1)
    row_i = lax.broadcasted_iota(jnp.int32, (L, 128), 0)
    col_i = lax.broadcasted_iota(jnp.int32, (L, 128), 1)
    causal = row_i >= col_i
    tri_u = jnp.where(row_i <= col_i, 1.0, 0.0).astype(BF16)
    tri_sub = jnp.where(causal & (row_i // LS == col_i // LS), 1.0, 0.0)
    blk_sum = jnp.where(lax.broadcasted_iota(jnp.int32, (L // LS, 128), 0)
                        == lax.broadcasted_iota(jnp.int32, (L // LS, 128), 1) // LS, 1.0, 0.0)
    tri_ext = jnp.concatenate([tri_sub, blk_sum], axis=0).astype(BF16)

    @pl.when(t == 0)
    def _init():
        s_ref[...] = jnp.zeros_like(s_ref)
        c_ref[...] = jnp.zeros_like(c_ref)
        n_ref[...] = jnp.zeros_like(n_ref)
        m_ref[...] = jnp.zeros_like(m_ref)
        xre_ref[...] = jnp.zeros_like(xre_ref)
        xim_ref[...] = jnp.zeros_like(xim_ref)
        mkv = _dot(mem_ref[0].astype(BF16), wmkv_ref[...])
        mk = mkv[:, :GW]
        mv = mkv[:, GW:]
        memk_ref[0] = mk.T
        memv_ref[0] = mv.T
        mk_ref[...] = _stack_heads(mk, hm).astype(BF16)
        mv_ref[...] = _stack_heads(mv, hm).astype(BF16)

    x = x_ref[0]
    hn = _rms_norm(x, normw_ref[...]).astype(BF16)
    proj_ref[...] = _dot_nt(hn, win_ref[...])
    gates_t = _dot_nt(wg_ref[...], hn)

    nc = TB // L
    crow = [slice(c * L, (c + 1) * L) for c in range(nc)]

    def PB(blk, rows=slice(None)):
        return proj_ref[rows, blk * GW:(blk + 1) * GW]


    def xattn_stages():
        for piece in range(TB // XQ):
            rs = slice(piece * XQ, (piece + 1) * XQ)
            sc = _dot_nt(PB(AQ, rs).astype(BF16), mk_ref[...])
            yield
            ps = []
            for h in range(NH):
                seg = sc[:, h * MEM:(h + 1) * MEM]
                e = jnp.exp(seg - jnp.max(seg, axis=-1, keepdims=True))
                ps.append((e / jnp.sum(e, axis=-1, keepdims=True)).astype(BF16))
            p = jnp.concatenate(ps, axis=1)
            yield
            xa = _dot(p, mv_ref[...])
            yield
            mix_ref[rs, 3 * GW:4 * GW] = (xa * _silu(PB(AG, rs))).astype(BF16)
            yield

    def s5_stages():
        lanes = [(slice(j * 128, (j + 1) * 128), slice(NS + j * 128, NS + (j + 1) * 128))
                 for j in range(NS // 128)]
        bpc = L // LS
        nblk = TB // LS
        su = PB(SU)
        bu_ref[...] = _dot(su.astype(BF16), bt_ref[...])
        yield
        for c in range(nc):
            for lre, lim in lanes:
                br = bu_ref[crow[c], lre]
                bi = bu_ref[crow[c], lim]
                wr = jnp.tile(tab_ref[0, :, lre], (bpc, 1))
                wi = jnp.tile(tab_ref[1, :, lre], (bpc, 1))
                xcat_ref[crow[c], lre] = (wr * br - wi * bi).astype(BF16)
                xcat_ref[crow[c], lim] = (wr * bi + wi * br).astype(BF16)
            yield
        for c in range(nc):
            z = _dot(tri_ext, xcat_ref[crow[c], :])
            bu_ref[crow[c], :] = z[0:L]
            car_ref[c * bpc:(c + 1) * bpc, :] = z[L:L + bpc]
            yield
        rowb = lax.broadcasted_iota(jnp.int32, (nblk, 128), 0)
        ers, eis, prs, pis, c0s = [], [], [], [], []
        for lre, lim in lanes:
            zr = car_ref[:, lre]
            zi = car_ref[:, lim]
            pr = tab_ref[2, LS - 1:LS, lre]
            pi = tab_ref[3, LS - 1:LS, lre]
            er = pr * zr - pi * zi
            ei = pr * zi + pi * zr
            pr = tab_ref[4, LS - 1:LS, lre]
            pi = tab_ref[5, LS - 1:LS, lre]
            c0r = xre_ref[0:1, lre]
            c0i = xim_ref[0:1, lre]
            ers.append(er + jnp.where(rowb == 0, pr * c0r - pi * c0i, 0.0))
            eis.append(ei + jnp.where(rowb == 0, pr * c0i + pi * c0r, 0.0))
            prs.append(pr)
            pis.append(pi)
            c0s.append((c0r, c0i))
        yield
        for k in range(nblk.bit_length() - 1):
            d = 1 << k
            for j in range(len(lanes)):
                er, ei, pr, pi = ers[j], eis[j], prs[j], pis[j]
                sr = jnp.where(rowb >= d, pltpu.roll(er, d, 0), 0.0)
                si = jnp.where(rowb >= d, pltpu.roll(ei, d, 0), 0.0)
                ers[j], eis[j] = er + pr * sr - pi * si, ei + pr * si + pi * sr
                prs[j], pis[j] = pr * pr - pi * pi, 2.0 * (pr * pi)
            yield
        for j, (lre, lim) in enumerate(lanes):
            xre_ref[:, lre] = jnp.broadcast_to(ers[j][nblk - 1:nblk, :], (8, 128))
            xim_ref[:, lre] = jnp.broadcast_to(eis[j][nblk - 1:nblk, :], (8, 128))
            cr = jnp.where(rowb == 0, c0s[j][0], pltpu.roll(ers[j], 1, 0))
            ci_ = jnp.where(rowb == 0, c0s[j][1], pltpu.roll(eis[j], 1, 0))
            ar = tab_ref[4, 0:1, lre]
            ai = tab_ref[5, 0:1, lre]
            car_ref[:, lre] = ar * cr - ai * ci_
            car_ref[:, lim] = ar * ci_ + ai * cr
        yield
        for c in range(nc):
            for lre, lim in lanes:
                cr = jnp.concatenate(
                    [jnp.broadcast_to(car_ref[c * bpc + j:c * bpc + j + 1, lre], (LS, 128))
                     for j in range(bpc)], axis=0)
                ci_ = jnp.concatenate(
                    [jnp.broadcast_to(car_ref[c * bpc + j:c * bpc + j + 1, lim], (LS, 128))
                     for j in range(bpc)], axis=0)
                zr = bu_ref[crow[c], lre] + cr
                zi = bu_ref[crow[c], lim] + ci_
                pr = jnp.tile(tab_ref[2, :, lre], (bpc, 1))
                pi = jnp.tile(tab_ref[3, :, lre], (bpc, 1))
                xcat_ref[crow[c], lre] = (pr * zr - pi * zi).astype(BF16)
                xcat_ref[crow[c], lim] = (pr * zi + pi * zr).astype(BF16)
            yield
        sy = _dot(xcat_ref[...], ct_ref[...]) + dsk_ref[...] * su
        yield
        sy = _gelu_tanh(sy)
        gate = _dot(sy.astype(BF16), wglu_ref[...])
        yield
        mix_ref[:, 2 * GW:3 * GW] = (sy * _sigmoid(gate) * _silu(PB(SG))).astype(BF16)
        yield

    def retention_stages():
        rq = _rope(PB(RQ), cos_ref[...], s1_ref[...], s2_ref[...])
        rk = _rope(PB(RK), cos_ref[...], s1_ref[...], s2_ref[...])
        rvf = PB(RV)
        rv = rvf.astype(BF16)
        rqb = rq.astype(BF16)
        kst = [_stack_heads(rk[crow[c]], hm).astype(BF16) for c in range(nc)]
        vst = [_stack_heads(rvf[crow[c]], hm).astype(BF16) for c in range(nc)]
        yield
        inner = [_dot_nt(rqb[crow[c]], kst[c]) for c in range(nc)]
        yield
        pmat = [(inner[c] * dec_ref[...]).astype(BF16) for c in range(nc)]
        kd = [(rk[crow[c]] * kdec_ref[...]).astype(BF16) for c in range(nc)]
        yield
        rloc = [_dot(pmat[c], vst[c]) for c in range(nc)]
        upd = [_dot_tn(kd[c], rv[crow[c]]) for c in range(nc)]
        yield
        st = [s_ref[...]]
        for c in range(nc):
            st.append(st[c] * cdec_ref[...] + jnp.where(bd, upd[c], 0.0))
        s_ref[...] = st[nc]
        qd = [(rq[crow[c]] * qdec_ref[...]).astype(BF16) for c in range(nc)]
        yield
        ost = [_dot(qd[c], st[c].astype(BF16)) for c in range(nc)]
        yield
        ro = jnp.concatenate([rloc[c] + ost[c] for c in range(nc)], axis=0)
        mix_ref[:, 0:GW] = (_head_norm(ro, retgn_ref[...], avg) * _silu(PB(RG))).astype(BF16)
        yield

    def mlstm_stages():
        ig_r = gates_t[0:8] + bi_ref[...]
        lf_r = _log_sigmoid(gates_t[8:16] + bf_ref[...])
        yield
        b_r = [_dot_x3(lf_r[:, crow[c]], tri_u) for c in range(nc)]
        yield
        gd_r = [ig_r[:, crow[c]] - b_r[c] for c in range(nc)]
        lane_r = lax.broadcasted_iota(jnp.int32, (8, L), 1)
        cm_r = list(gd_r)
        for k in range(7):
            d = 1 << k
            cm_r = [jnp.maximum(v, jnp.where(lane_r >= d, pltpu.roll(v, d, 1), NEG_INF))
                    for v in cm_r]
            yield
        m_prev = [m_ref[...]]
        mt_r = []
        for c in range(nc):
            mt = jnp.maximum(b_r[c] + m_prev[c], b_r[c] + cm_r[c])
            mt_r.append(mt)
            m_prev.append(jnp.broadcast_to(mt[:, L - 1:L], (8, L)))
            yield
        m_ref[...] = m_prev[nc]
        cols = []
        for c in range(nc):
            bm = b_r[c] - mt_r[c]
            ws = jnp.exp(b_r[c] + m_prev[c] - mt_r[c])
            wl = jnp.exp(gd_r[c] + jnp.broadcast_to(bm[:, L - 1:L], (8, L)))
            emt = jnp.exp(-mt_r[c])
            cols.append(jnp.concatenate([bm, ws, wl, emt, jnp.zeros((L - 32, L), F32)], axis=0).T)
        mqf = PB(MQ)
        mq = mqf.astype(BF16)
        mkf = PB(MK)
        mvf = PB(MV)
        mv_ = mvf.astype(BF16)
        kst = [_stack_heads(mkf[crow[c]], hm).astype(BF16) for c in range(nc)]
        vst = [_stack_heads(mvf[crow[c]], hm).astype(BF16) for c in range(nc)]
        yield
        sraw = [_dot_nt(mq[crow[c]], kst[c]) for c in range(nc)]
        yield
        smat, den_i = [], []
        for c in range(nc):
            parts, dens = [], []
            for h in range(NH):
                arg = jnp.where(causal, cols[c][:, h:h + 1] + gd_r[c][h:h + 1, :], NEG_INF)
                s_h = sraw[c][:, h * L:(h + 1) * L] * jnp.exp(arg)
                dens.append(jnp.sum(s_h, axis=-1, keepdims=True))
                parts.append(s_h.astype(BF16))
            den_i.append(dens)
            smat.append(jnp.concatenate(parts, axis=1))
            yield
        ones_blk = jnp.ones((L, 128), BF16)
        def per_head(columns):
            return jnp.where(hm[0], columns[0],
                             jnp.where(hm[1], columns[1], jnp.where(hm[2], columns[2], columns[3])))

        kws = [(mkf[crow[c]] * per_head([cols[c][:, 16 + h:17 + h] for h in range(NH)])
                ).astype(BF16) for c in range(nc)]
        yield
        rloc = [_dot(smat[c], vst[c]) for c in range(nc)]
        u = [_dot_tn(kws[c], jnp.concatenate([mv_[crow[c]], ones_blk], axis=1))
             for c in range(nc)]
        yield
        nmask = (lax.broadcasted_iota(jnp.int32, (GW, 128), 0) // DH
                 == lax.broadcasted_iota(jnp.int32, (GW, 128), 1))
        cst = [c_ref[...]]
        nst = [n_ref[...]]
        for c in range(nc):
            wsl256 = jnp.zeros((1, GW), F32)
            wsl128 = jnp.zeros((1, 128), F32)
            for h in range(NH):
                wsl = cols[c][L - 1:L, 8 + h:9 + h]
                wsl256 = wsl256 + jnp.where(hm[h], wsl, 0.0)
                wsl128 = wsl128 + jnp.where(lane128 == h, wsl, 0.0)
            cst.append(cst[c] * wsl256 + jnp.where(bd, u[c][:, :GW], 0.0))
            nst.append(nst[c] * wsl128 + jnp.where(nmask, u[c][:, GW:], 0.0))
        c_ref[...] = cst[nc]
        n_ref[...] = nst[nc]
        yield
        qc = [_dot(mq[crow[c]], cst[c].astype(BF16)) for c in range(nc)]
        qn = [_dot(mq[crow[c]], nst[c].astype(BF16)) for c in range(nc)]
        yield
        mhs = []
        for c in range(nc):
            wsc = [cols[c][:, 8 + h:9 + h] for h in range(NH)]
            rdn = []
            for h in range(NH):
                den = den_i[c][h] + wsc[h] * qn[c][:, h:h + 1]
                rdn.append(1.0 / jnp.maximum(jnp.abs(den), cols[c][:, 24 + h:25 + h]))
            mhs.append((rloc[c] + per_head(wsc) * qc[c]) * per_head(rdn))
            yield
        mh = jnp.concatenate(mhs, axis=0) * _sigmoid(PB(MO))
        mix_ref[:, GW:2 * GW] = (_head_norm(mh, mlgn_ref[...], avg) * _silu(PB(MG))).astype(BF16)
        yield

    pending = [mlstm_stages(), s5_stages(), retention_stages(), xattn_stages()]
    while pending:
        pending = [stage for stage in pending if next(stage, _DONE) is not _DONE]

    y = x_ref[0] + _dot(mix_ref[...], wout_ref[...])
    if last_layer:
        y = _rms_norm(y, fnw_ref[...])
    y_ref[0] = y

    @pl.when(t == n_tblocks - 1)
    def _final():
        s_fin = s_ref[...]
        c_fin = c_ref[...].T
        n_fin = n_ref[...].T
        for h in range(NH):
            blk = slice(h * DH, (h + 1) * DH)
            rets_ref[0, h] = s_fin[blk, blk]
            mlc_ref[0, h] = c_fin[blk, blk]
        mln_ref[0] = jnp.concatenate([n_fin[h:h + 1, h * DH:(h + 1) * DH] for h in range(NH)],
                                     axis=0)
        mlm_ref[0] = m_ref[...]
        s5re_ref[0] = xre_ref[...]
        s5im_ref[0] = xim_ref[...]


def _prompt_layer(layer, x, mem, w, consts, last_layer, prev):
    bsz, seq, _ = x.shape
    depth = w["w_in"].shape[0]
    nt = seq // TB
    full = lambda shape: pl.BlockSpec(shape, lambda b, t: (0,) * len(shape),
                                      pipeline_mode=pl.Buffered(1))
    lyr = lambda shape: pl.BlockSpec((None,) + shape, lambda b, t: (layer,) + (0,) * len(shape),
                                     pipeline_mode=pl.Buffered(1))
    tok = lambda width: pl.BlockSpec((TB, width), lambda b, t: (t, 0))
    per_b = lambda r, c: pl.BlockSpec((1, r, c), lambda b, t: (b, 0, 0))
    in_specs = [
        pl.BlockSpec((1, TB, D), lambda b, t: (b, t, 0)),
        per_b(MEM, D),
        lyr((1, D)), lyr((DP, D)), lyr((NGATE, D)), lyr((D, D)), lyr((D, 2 * GW)),
        tok(128), tok(128), tok(128),
        full((L, NH * L)), full((L, GW)), full((L, GW)), full((1, GW)),
        lyr((1, GW)), lyr((1, GW)), lyr((8, TB)), lyr((8, TB)),
        lyr((6, LS, NS)), lyr((GW, 2 * NS)), lyr((2 * NS, GW)),
        lyr((1, GW)), lyr((GW, GW)), full((1, D)),
    ] + [pl.BlockSpec(memory_space=pl.ANY)] * len(prev)
    n_in = len(in_specs) - len(prev)
    state_shapes = [(NH, DH, DH),
                    (NH, DH, DH),
                    (NH, DH),
                    (8, 128),
                    (8, NS),
                    (8, NS),
                    (GW, MEM),
                    (GW, MEM)]
    assert len(state_shapes) == N_STATE_OUT and len(prev) in (0, N_STATE_OUT)
    out_shape = ((jax.ShapeDtypeStruct((bsz, seq, D), F32),)
                 + tuple(jax.ShapeDtypeStruct((depth, bsz) + s, F32) for s in state_shapes))
    state_spec = lambda s: pl.BlockSpec((None, 1) + s,
                                        lambda b, t: (layer, b) + (0,) * len(s))
    out_specs = ((pl.BlockSpec((1, TB, D), lambda b, t: (b, t, 0)),)
                 + tuple(state_spec(s) for s in state_shapes))
    scratch = [
        pltpu.VMEM((TB, DP), F32),
        pltpu.VMEM((TB, D), BF16),
        pltpu.VMEM((GW, GW), F32),
        pltpu.VMEM((GW, GW), F32),
        pltpu.VMEM((GW, 128), F32),
        pltpu.VMEM((8, 128), F32),
        pltpu.VMEM((8, NS), F32),
        pltpu.VMEM((8, NS), F32),
        pltpu.VMEM((NH * MEM, GW), BF16),
        pltpu.VMEM((NH * MEM, GW), BF16),
        pltpu.VMEM((TB, 2 * NS), F32),
        pltpu.VMEM((TB, 2 * NS), BF16),
        pltpu.VMEM((TB // LS, 2 * NS), F32),
    ]
    kern = functools.partial(_prompt_kernel, last_layer=last_layer, n_tblocks=nt, n_prev=len(prev))
    return pl.pallas_call(
        kern, grid=(bsz, nt), in_specs=in_specs, out_specs=out_specs, out_shape=out_shape,
        scratch_shapes=scratch, name="prompt_layer",
        input_output_aliases={n_in + i: 1 + i for i in range(len(prev))},
        compiler_params=pltpu.CompilerParams(
            dimension_semantics=("arbitrary", "arbitrary"), vmem_limit_bytes=VMEM_LIMIT),
    )(x, mem, w["norm_w"], w["w_in"], w["w_gate"], w["w_out"], w["w_mem_kv"],
      consts["cos_p"], consts["s1_p"], consts["s2_p"],
      consts["dec"], consts["qdec"], consts["kdec"], consts["cdec"],
      w["ret_gn"], w["ml_gn"], w["b_i8"], w["b_f8"],
      w["tab"], w["bt"], w["ct"], w["s5_d"], w["w_glu"], w["final_norm_w"], *prev)


SROWS = 16


def _sample_kernel_t(x_ref, normw_ref, win_ref, wg_ref, wout_ref,
                     cos_ref, s1_ref, s2_ref, gam_ref, gam8_ref,
                     retgn_ref, mlgn_ref, bi_ref, bf_ref,
                     abt_ref, bt_ref, ct_ref, dsk_ref, wglu_ref, fnw_ref,
                     m0_ref, n0_ref, x0re_ref, x0im_ref,
                     rets_ref, mlc_ref, ck_ref, cv_ref,
                     y_ref, retn_ref, mlcn_ref, mlnn_ref, mlmn_ref, s5re_ref, s5im_ref,
                     hs_ref, proj_ref, qgt_ref, rkt_ref, rvt_ref, mqt_ref, mkt_ref, mvt_ref,
                     wi_ref, ws_ref, ot_ref, cqt_ref, qa8_ref, xa_ref,
                     *, n_layers, n_blocks):
    layer = pl.program_id(0)
    g = pl.program_id(1)
    nsamp = x_ref.shape[0]
    ones_bd = _ones_matrix()
    avg = _avg_matrix()
    mask8 = (lax.broadcasted_iota(jnp.int32, (8, GW), 0)
             == lax.broadcasted_iota(jnp.int32, (8, GW), 1) // DH)

    def P(blk):
        return proj_ref[:, blk * GW:(blk + 1) * GW]

    @pl.when((layer == 0) & (g == 0))
    def _load_x():
        hs_ref[...] = x_ref[...]

    @pl.when(g == 0)
    def _pre():
        hn = _rms_norm(hs_ref[...], normw_ref[...]).astype(BF16)
        proj_ref[...] = _dot_nt(hn, win_ref[...])
        cosr, s1, s2 = cos_ref[...], s1_ref[...], s2_ref[...]
        qgt_ref[...] = (_rope(P(RQ), cosr, s1, s2) * gam_ref[...]).T
        rkt_ref[...] = _rope(P(RK), cosr, s1, s2).T
        rvt_ref[...] = P(RV).T
        mqt_ref[...] = P(MQ).T
        mkt_ref[...] = P(MK).T
        mvt_ref[...] = P(MV).T
        gt = _dot_nt(wg_ref[...], hn)
        ig = gt[0:8] + bi_ref[...]
        lf = _log_sigmoid(gt[8:16] + bf_ref[...])
        a = lf + m0_ref[...]
        mt = jnp.maximum(a, ig)
        wi_ref[...] = jnp.exp(ig - mt)
        ws_ref[...] = jnp.exp(a - mt)
        mlmn_ref[...] = mt
        ot_ref[...] = jnp.zeros_like(ot_ref)
        qa8_ref[...] = jnp.where(mask8[None], P(AQ)[:, None, :], 0.0).reshape(8 * nsamp, GW)
        but = _dot_tn(bt_ref[...], P(SU).T.astype(BF16))
        are, aim = abt_ref[0:NS, :], abt_ref[NS:2 * NS, :]
        x0r, x0i = x0re_ref[...], x0im_ref[...]
        s5re_ref[...] = are * x0r - aim * x0i + but[0:NS]
        s5im_ref[...] = are * x0i + aim * x0r + but[NS:2 * NS]

    head = g // (GW // SROWS // NH)
    hrow = pl.ds(pl.multiple_of(head * DH, DH), DH)
    gam_row = gam8_ref[pl.ds(head, 1), :]
    ws_row = ws_ref[pl.ds(head, 1), :]
    wi_row = wi_ref[pl.ds(head, 1), :]
    v_slab = rvt_ref[hrow, :]
    q_slab = mqt_ref[hrow, :]
    k_slab = mkt_ref[hrow, :]
    o_acc = jnp.zeros((DH, nsamp), F32)
    for i in range(SROWS):
        r = pl.ds(g * SROWS + i, 1)
        s_t = rets_ref[i]
        o_acc = o_acc + qgt_ref[r, :] * s_t
        retn_ref[i] = gam_row * s_t + rkt_ref[r, :] * v_slab
        c_t = mlc_ref[i]
        cqt_ref[r, :] = jnp.sum(c_t * q_slab, axis=0, keepdims=True)
        mlcn_ref[i] = ws_row * c_t + (wi_row * mvt_ref[r, :]) * k_slab
    ot_ref[hrow, :] = ot_ref[hrow, :] + o_acc

    tiles = [pl.ds(pl.multiple_of((g * SB + i) * 8, 8), 8) for i in range(SB)]
    q8 = [qa8_ref[tiles[i], :].astype(BF16) for i in range(SB)]
    sc = [_dot(q8[i], ck_ref[i].astype(BF16)) for i in range(SB)]
    ps = []
    for i in range(SB):
        e = jnp.exp(sc[i] - jnp.max(sc[i], axis=-1, keepdims=True))
        ps.append((e / jnp.sum(e, axis=-1, keepdims=True)).astype(BF16))
    ta = [_dot_nt(ps[i], cv_ref[i].astype(BF16)) for i in range(SB)]
    xa_ref[pl.ds(pl.multiple_of(g * SB, SB), SB), :] = jnp.concatenate(
        [jnp.sum(jnp.where(mask8, ta[i], 0.0), axis=0, keepdims=True) for i in range(SB)], axis=0)

    @pl.when(g == n_blocks - 1)
    def _post():
        cosr, s1, s2 = cos_ref[...], s1_ref[...], s2_ref[...]
        rq = _rope(P(RQ), cosr, s1, s2)
        rk = _rope(P(RK), cosr, s1, s2)
        ro = _dot_x2(rq * rk, ones_bd) * P(RV) + ot_ref[...].T
        ret_out = _head_norm(ro, retgn_ref[...], avg) * _silu(P(RG))
        mht = []
        for h in range(NH):
            rows = slice(h * DH, (h + 1) * DH)
            wi = wi_ref[h:h + 1, :]
            ws = ws_ref[h:h + 1, :]
            emt = jnp.exp(-mlmn_ref[h:h + 1, :])
            qt, kt, n0 = mqt_ref[rows, :], mkt_ref[rows, :], n0_ref[rows, :]
            s = jnp.sum(qt * kt, axis=0, keepdims=True) * wi
            den = s + ws * jnp.sum(n0 * qt, axis=0, keepdims=True)
            mht.append((s * mvt_ref[rows, :] + ws * cqt_ref[rows, :])
                       / jnp.maximum(jnp.abs(den), emt))
            mlnn_ref[rows, :] = ws * n0 + wi * kt
        mh = jnp.concatenate(mht, axis=0).T * _sigmoid(P(MO))
        ml_out = _head_norm(mh, mlgn_ref[...], avg) * _silu(P(MG))
        su = P(SU)
        xcat = jnp.concatenate([s5re_ref[...], s5im_ref[...]], axis=0).astype(BF16)
        sy = _dot_tn(xcat, ct_ref[...]) + dsk_ref[...] * su
        sy = _gelu_tanh(sy)
        sy = sy * _sigmoid(_dot(sy.astype(BF16), wglu_ref[...]))
        s5_out = sy * _silu(P(SG))
        xa_out = xa_ref[...] * _silu(P(AG))
        mix = jnp.concatenate([ret_out, ml_out, s5_out, xa_out], axis=1).astype(BF16)
        y = hs_ref[...] + _dot(mix, wout_ref[...])
        hs_ref[...] = y

        @pl.when(layer == n_layers - 1)
        def _emit():
            y_ref[...] = _rms_norm(y, fnw_ref[...])


def _sample_layers_t(x, st, w, consts):
    nsamp = x.shape[0]
    depth = w["w_in"].shape[0]
    nb = GW // SROWS
    assert nsamp == nb * SB and nsamp == 128
    once = lambda shape: pl.BlockSpec(shape, lambda l, g: (0,) * len(shape),
                                      pipeline_mode=pl.Buffered(1))
    lyr = lambda shape: pl.BlockSpec((None,) + shape, lambda l, g: (l,) + (0,) * len(shape))
    lyr_out = lyr
    srows = pl.BlockSpec((None, SROWS, DH, nsamp), lambda l, g: (l, g, 0, 0))
    cache = pl.BlockSpec((None, SB, GW, MEM), lambda l, g: (l, g, 0, 0))
    in_specs = [
        once((nsamp, D)), lyr((1, D)), lyr((DP, D)), lyr((NGATE, D)), lyr((D, D)),
        once((1, 128)), once((1, 128)), once((1, 128)), once((1, GW)), once((8, nsamp)),
        lyr((1, GW)), lyr((1, GW)), lyr((8, nsamp)), lyr((8, nsamp)),
        lyr((2 * NS, nsamp)), lyr((GW, 2 * NS)), lyr((2 * NS, GW)),
        lyr((1, GW)), lyr((GW, GW)), once((1, D)),
        lyr((8, nsamp)), lyr((GW, nsamp)), lyr((NS, nsamp)), lyr((NS, nsamp)),
        srows, srows, cache, cache,
    ]
    out_shape = (
        jax.ShapeDtypeStruct((nsamp, D), F32),
        jax.ShapeDtypeStruct((depth, GW, DH, nsamp), F32),
        jax.ShapeDtypeStruct((depth, GW, DH, nsamp), F32),
        jax.ShapeDtypeStruct((depth, GW, nsamp), F32),
        jax.ShapeDtypeStruct((depth, 8, nsamp), F32),
        jax.ShapeDtypeStruct((depth, NS, nsamp), F32),
        jax.ShapeDtypeStruct((depth, NS, nsamp), F32),
    )
    out_specs = (
        pl.BlockSpec((nsamp, D), lambda l, g: (0, 0)), srows, srows,
        lyr_out((GW, nsamp)), lyr_out((8, nsamp)), lyr_out((NS, nsamp)), lyr_out((NS, nsamp)),
    )
    scratch = [
        pltpu.VMEM((nsamp, D), F32),
        pltpu.VMEM((nsamp, DP), F32),
        pltpu.VMEM((GW, nsamp), F32), pltpu.VMEM((GW, nsamp), F32), pltpu.VMEM((GW, nsamp), F32),
        pltpu.VMEM((GW, nsamp), F32), pltpu.VMEM((GW, nsamp), F32), pltpu.VMEM((GW, nsamp), F32),
        pltpu.VMEM((8, nsamp), F32), pltpu.VMEM((8, nsamp), F32),
        pltpu.VMEM((GW, nsamp), F32), pltpu.VMEM((GW, nsamp), F32),
        pltpu.VMEM((8 * nsamp, GW), F32), pltpu.VMEM((nsamp, GW), F32),
    ]
    kern = functools.partial(_sample_kernel_t, n_layers=depth, n_blocks=nb)
    return pl.pallas_call(
        kern, grid=(depth, nb), in_specs=in_specs, out_specs=out_specs, out_shape=out_shape,
        scratch_shapes=scratch, name="sample_layers",
        compiler_params=pltpu.CompilerParams(
            dimension_semantics=("arbitrary", "arbitrary"), vmem_limit_bytes=VMEM_LIMIT),
    )(x, w["norm_w"], w["w_in"], w["w_gate"], w["w_out"],
      consts["cos_s"], consts["s1_s"], consts["s2_s"], consts["gam"], consts["gam8"],
      w["ret_gn"], w["ml_gn"], w["b_i8"], w["b_f8"],
      w["abt"], w["bt"], w["ct"], w["s5_d"], w["w_glu"], w["final_norm_w"],
      st["m"], st["n"], st["s5_re"], st["s5_im"], st["ret"], st["c"], st["mem_k"], st["mem_v"])


def _rope_tables(pos):
    half = DH // 2
    inv = ROPE_BASE ** (-np.arange(half, dtype=np.float64) / half)
    ang = np.asarray(pos, np.float64)[:, None] * inv[None, :]
    cos, sin = np.cos(ang), np.sin(ang)
    zero = np.zeros_like(sin)
    c = np.tile(np.concatenate([cos, cos], axis=-1), (1, 2))
    s1 = np.tile(np.concatenate([zero, sin], axis=-1), (1, 2))
    s2 = np.tile(np.concatenate([-sin, zero], axis=-1), (1, 2))
    return tuple(jnp.asarray(t, F32) for t in (c, s1, s2))


def _constants(seq):
    lg = np.log1p(-np.power(2.0, -5.0 - np.arange(NH, dtype=np.float64)))[:, None]
    idx = np.arange(L, dtype=np.float64)
    diff = idx[:, None] - idx[None, :]
    decay = np.where(diff >= 0, np.exp(lg[:, :, None] * np.maximum(diff, 0.0)), 0.0)
    rep = lambda t: np.repeat(t, DH, axis=0).T
    consts = {
        "dec": np.transpose(decay, (1, 0, 2)).reshape(L, NH * L),
        "qdec": rep(np.exp(lg * (idx + 1.0))),
        "kdec": rep(np.exp(lg * (L - 1.0 - idx))),
        "cdec": rep(np.exp(lg * L)),
        "gam": rep(np.exp(lg * 1.0)),
        "gam8": np.pad(np.broadcast_to(np.exp(lg), (NH, 128)), ((0, 8 - NH), (0, 0))),
    }
    consts = {k: jnp.asarray(v, F32) for k, v in consts.items()}
    consts["cos_p"], consts["s1_p"], consts["s2_p"] = _rope_tables(np.arange(seq))
    consts["cos_s"], consts["s1_s"], consts["s2_s"] = _rope_tables(PAST_LEN + np.arange(1))
    return consts


def _pack_w_in(w_in):
    wt = jnp.swapaxes(w_in, 1, 2)
    sizes = (GW,) * 9 + (NH, NH) + (GW,) * 4
    offs = np.concatenate([[0], np.cumsum(sizes)])
    seg = [wt[:, int(offs[i]):int(offs[i + 1]), :] for i in range(len(sizes))]
    scale = DH ** -0.5
    pad8 = lambda t: jnp.pad(t, ((0, 0), (0, 8 - t.shape[1]), (0, 0)))
    blocks = [seg[11],
              seg[0], seg[1] * scale, seg[2], seg[3],
              seg[4], seg[5] * scale, seg[6], seg[7], seg[8],
              seg[12], seg[13] * scale, seg[14]]
    gates = jnp.concatenate([pad8(seg[9]), pad8(seg[10])], axis=1)
    return jnp.concatenate(blocks, axis=1).astype(BF16), gates.astype(BF16)


def kernel(x_prompt, x_sample, mem_prompt, state_ret, state_mlstm_c, state_mlstm_n, state_mlstm_m,
           state_s5_re, state_s5_im, cache_mem_k, cache_mem_v,
           norm_w, w_in, ret_gn, ml_b_i, ml_b_f, ml_gn,
           s5_a_re, s5_a_im, s5_log_dt, s5_b_re, s5_b_im, s5_c_re, s5_c_im, s5_d, s5_w_glu,
           w_mem_k, w_mem_v, w_out, final_norm_w):
    depth = norm_w.shape[0]
    bp, seq, _ = x_prompt.shape
    bs = x_sample.shape[0]
    consts = _constants(seq)
    abt, bt, ct, tab = _s5_prepare(s5_a_re, s5_a_im, s5_log_dt, s5_b_re, s5_b_im, s5_c_re, s5_c_im)
    rows8 = lambda t: jnp.pad(jnp.broadcast_to(t[:, :, None], (depth, NH, TB)),
                              ((0, 0), (0, 8 - NH), (0, 0)))
    w_main, w_gate = _pack_w_in(w_in)
    w = {
        "norm_w": norm_w[:, None], "w_in": w_main, "w_gate": w_gate, "w_out": w_out.astype(BF16),
        "w_mem_kv": jnp.concatenate([w_mem_k, w_mem_v], axis=-1).astype(BF16),
        "ret_gn": ret_gn[:, None], "ml_gn": ml_gn[:, None],
        "b_i8": rows8(ml_b_i), "b_f8": rows8(ml_b_f),
        "abt": abt, "tab": tab, "bt": bt, "ct": ct,
        "s5_d": s5_d[:, None], "w_glu": s5_w_glu.astype(BF16),
        "final_norm_w": final_norm_w[None],
    }

    hp = x_prompt
    states = ()
    for l in range(depth):
        hp, *states = _prompt_layer(l, hp, mem_prompt, w, consts, l == depth - 1, tuple(states))
    ret_p, mlc_p, mln_p, ms, xr, xi, mk, mv = states
    mlm_p = ms[:, :, :NH, 0]
    s5re_p = xr[:, :, 0].reshape(depth, bp, S5G, S5P)
    s5im_p = xi[:, :, 0].reshape(depth, bp, S5G, S5P)
    memk_p = jnp.transpose(mk.reshape(depth, bp, NH, DH, MEM), (0, 1, 4, 2, 3))
    memv_p = jnp.transpose(mv.reshape(depth, bp, NH, DH, MEM), (0, 1, 4, 2, 3))

    st = {
        "m": jnp.pad(jnp.swapaxes(state_mlstm_m, 1, 2), ((0, 0), (0, 8 - NH), (0, 0))),
        "n": jnp.transpose(state_mlstm_n, (0, 2, 3, 1)).reshape(depth, GW, bs),
        "s5_re": jnp.transpose(state_s5_re, (0, 2, 3, 1)).reshape(depth, NS, bs),
        "s5_im": jnp.transpose(state_s5_im, (0, 2, 3, 1)).reshape(depth, NS, bs),
        "ret": jnp.transpose(state_ret, (0, 2, 3, 4, 1)).reshape(depth, GW, DH, bs),
        "c": jnp.transpose(state_mlstm_c, (0, 2, 3, 4, 1)).reshape(depth, GW, DH, bs),
        "mem_k": jnp.transpose(cache_mem_k, (0, 1, 3, 4, 2)).reshape(depth, bs, GW, MEM),
        "mem_v": jnp.transpose(cache_mem_v, (0, 1, 3, 4, 2)).reshape(depth, bs, GW, MEM),
    }
    hs, rn, cn, nn, mn, sr, si = _sample_layers_t(x_sample.reshape(bs, D), st, w, consts)
    back5 = lambda t: jnp.transpose(t.reshape(depth, NH, DH, DH, bs), (0, 4, 1, 2, 3))
    back4 = lambda t, a, b: jnp.transpose(t.reshape(depth, a, b, bs), (0, 3, 1, 2))
    return (hp, hs.reshape(bs, 1, D),
            ret_p, back5(rn), mlc_p, back5(cn),
            mln_p, back4(nn, NH, DH), mlm_p, jnp.swapaxes(mn[:, :NH], 1, 2),
            s5re_p, back4(sr, S5G, S5P), s5im_p, back4(si, S5G, S5P),
            memk_p, memv_p)
```

```python
import functools
import math

import numpy as np
import jax
import jax.numpy as jnp
from jax import lax
from jax.experimental import pallas as pl
from jax.experimental.pallas import tpu as pltpu

F32 = jnp.float32
BF16 = jnp.bfloat16

D = 1024
GW = 256
NH = 4
DH = 64
L = 128
LS = 8
S5G = 16
S5P = 64
S5C = 16
NS = S5G * S5P
MEM = 256
EPS = 1e-6
NEG_INF = -1e30
ROPE_BASE = 10000.0
PAST_LEN = 16384

TB = 512
XQ = 512
NBLK = 13
DP = NBLK * GW
(SU, RQ, RK, RV, RG, MQ, MK, MV, MO, MG, SG, AQ, AG) = range(NBLK)
NGATE = 16

VMEM_LIMIT = 60 * 1024 * 1024


_DONE = object()


def _dot(a, b):
    return jnp.dot(a, b, preferred_element_type=F32)


def _dot_nt(a, b):
    return lax.dot_general(a, b, (((1,), (1,)), ((), ())), preferred_element_type=F32)


def _dot_tn(a, b):
    return lax.dot_general(a, b, (((0,), (0,)), ((), ())), preferred_element_type=F32)


def _split2(x):
    hi = x.astype(BF16)
    lo = (x - hi.astype(F32)).astype(BF16)
    return hi, lo


def _dot_x2(x, w):
    hi, lo = _split2(x)
    return _dot(hi, w) + _dot(lo, w)


def _dot_x3(x, w):
    hi = x.astype(BF16)
    r = x - hi.astype(F32)
    mid = r.astype(BF16)
    lo = (r - mid.astype(F32)).astype(BF16)
    return _dot(hi, w) + _dot(mid, w) + _dot(lo, w)


def _sigmoid(x):
    return 0.5 * (1.0 + jnp.tanh(0.5 * x))


def _silu(x):
    return x * _sigmoid(x)


def _log_sigmoid(x):
    return jnp.minimum(x, 0.0) - jnp.log1p(jnp.exp(-jnp.abs(x)))


def _gelu_tanh(x):
    c = math.sqrt(2.0 / math.pi)
    return x * (0.5 * (1.0 + jnp.tanh(c * (x + 0.044715 * (x * x * x)))))


def _lane_head(n):
    return lax.broadcasted_iota(jnp.int32, (1, n), 1) // DH


def _head_masks():
    lh = _lane_head(GW)
    return [lh == h for h in range(NH)]


def _block_diag_mask():
    r = lax.broadcasted_iota(jnp.int32, (GW, GW), 0) // DH
    c = lax.broadcasted_iota(jnp.int32, (GW, GW), 1) // DH
    return r == c


def _avg_matrix():
    return jnp.where(_block_diag_mask(), 1.0 / DH, 0.0).astype(BF16)


def _ones_matrix():
    return jnp.where(_block_diag_mask(), 1.0, 0.0).astype(BF16)


def _rope(x, cos, s1, s2):
    outs = []
    for j in range(2):
        xs = x[:, j * 128:(j + 1) * 128]
        outs.append(xs * cos + pltpu.roll(xs, 32, 1) * s1 + pltpu.roll(xs, 96, 1) * s2)
    return jnp.concatenate(outs, axis=1)


def _head_norm(x, gain, avg):
    mu = _dot_x2(x, avg)
    d = x - mu
    var = _dot_x2(d * d, avg)
    return d * lax.rsqrt(var + EPS) * gain


def _rms_norm(x, w):
    ms = jnp.mean(x * x, axis=-1, keepdims=True)
    return x * lax.rsqrt(ms + EPS) * w


def _stack_heads(x, hm):
    return jnp.concatenate([jnp.where(hm[h], x, 0.0) for h in range(NH)], axis=0)


def _s5_prep_kernel(are_ref, aim_ref, ldt_ref, bre_ref, bim_ref, cre_ref, cim_ref,
                    abt_ref, bt_ref, ct_ref, tab_ref):
    a_re = are_ref[0]
    a_im = aim_ref[0]
    dt = jnp.exp(ldt_ref[0])
    lam_re = a_re * dt
    lam_im = a_im * dt
    mag = jnp.exp(lam_re)
    ab_re = mag * jnp.cos(lam_im)
    ab_im = mag * jnp.sin(lam_im)
    den = a_re * a_re + a_im * a_im
    nr = ab_re - 1.0
    ni = ab_im
    f_re = (nr * a_re + ni * a_im) / den
    f_im = (ni * a_re - nr * a_im) / den
    abt_ref[0, 0:NS, :] = jnp.broadcast_to(ab_re, (128, NS)).T
    abt_ref[0, NS:2 * NS, :] = jnp.broadcast_to(ab_im, (128, NS)).T
    b_re = bre_ref[0]
    b_im = bim_ref[0]
    bb_re = (f_re * b_re - f_im * b_im).astype(BF16)
    bb_im = (f_re * b_im + f_im * b_re).astype(BF16)
    rep_r = (lax.broadcasted_iota(jnp.int32, (GW, S5C), 0) % S5C
             == lax.broadcasted_iota(jnp.int32, (GW, S5C), 1)).astype(BF16)
    in_blk = (lax.broadcasted_iota(jnp.int32, (GW, NS), 0) // S5C
              == lax.broadcasted_iota(jnp.int32, (GW, NS), 1) // S5P)
    bt_ref[0, :, :NS] = jnp.where(in_blk, _dot(rep_r, bb_re), 0.0).astype(BF16)
    bt_ref[0, :, NS:] = jnp.where(in_blk, _dot(rep_r, bb_im), 0.0).astype(BF16)
    rep_c = (lax.broadcasted_iota(jnp.int32, (S5C, GW), 0)
             == lax.broadcasted_iota(jnp.int32, (S5C, GW), 1) % S5C).astype(BF16)
    out_blk = (lax.broadcasted_iota(jnp.int32, (NS, GW), 0) // S5P
               == lax.broadcasted_iota(jnp.int32, (NS, GW), 1) // S5C)
    ct_ref[0, :NS, :] = jnp.where(out_blk, _dot(cre_ref[0].astype(BF16), rep_c), 0.0).astype(BF16)
    ct_ref[0, NS:, :] = jnp.where(out_blk, -_dot(cim_ref[0].astype(BF16), rep_c), 0.0).astype(BF16)
    i = lax.broadcasted_iota(jnp.int32, (LS, NS), 0).astype(F32)
    for slot, k in ((0, -i), (2, i), (4, i + 1.0)):
        pmag = jnp.exp(k * lam_re)
        tab_ref[0, slot] = pmag * jnp.cos(k * lam_im)
        tab_ref[0, slot + 1] = pmag * jnp.sin(k * lam_im)


def _s5_prepare(a_re, a_im, log_dt, b_re, b_im, c_re, c_im):
    depth = a_re.shape[0]
    are = a_re.reshape(depth, 1, NS)
    aim = a_im.reshape(depth, 1, NS)
    ldt = jnp.repeat(log_dt, S5P, axis=-1).reshape(depth, 1, NS)
    bre = jnp.transpose(b_re, (0, 3, 1, 2)).reshape(depth, S5C, NS)
    bim = jnp.transpose(b_im, (0, 3, 1, 2)).reshape(depth, S5C, NS)
    cre = jnp.transpose(c_re, (0, 1, 3, 2)).reshape(depth, NS, S5C)
    cim = jnp.transpose(c_im, (0, 1, 3, 2)).reshape(depth, NS, S5C)
    per_layer = lambda *shape: pl.BlockSpec((1,) + shape, lambda l: (l,) + (0,) * len(shape))
    out_shape = (jax.ShapeDtypeStruct((depth, 2 * NS, 128), F32),
                 jax.ShapeDtypeStruct((depth, GW, 2 * NS), BF16),
                 jax.ShapeDtypeStruct((depth, 2 * NS, GW), BF16),
                 jax.ShapeDtypeStruct((depth, 6, LS, NS), F32))
    return pl.pallas_call(
        _s5_prep_kernel, grid=(depth,),
        in_specs=[per_layer(1, NS), per_layer(1, NS), per_layer(1, NS),
                  per_layer(S5C, NS), per_layer(S5C, NS), per_layer(NS, S5C), per_layer(NS, S5C)],
        out_specs=(per_layer(2 * NS, 128), per_layer(GW, 2 * NS), per_layer(2 * NS, GW),
                   per_layer(6, LS, NS)),
        out_shape=out_shape, name="s5_prepare",
        compiler_params=pltpu.CompilerParams(dimension_semantics=("arbitrary",)),
    )(are, aim, ldt, bre, bim, cre, cim)


N_STATE_OUT = 8
N_SAMPLE_IN = 15
N_SAMPLE_STATE_OUT = 6


def _prompt_kernel(x_ref, mem_ref, normw_ref, win_ref, wg_ref, wout_ref, wmkv_ref,
                   cos_ref, s1_ref, s2_ref, dec_ref, qdec_ref, kdec_ref, cdec_ref,
                   retgn_ref, mlgn_ref, bi_ref, bf_ref,
                   tab_ref, bt_ref, ct_ref, dsk_ref, wglu_ref, fnw_ref,
                   *rest, last_layer, n_tblocks, n_steps, n_prev):
    (hs_in, cos_s, s1_s, s2_s, gam, gam8, abt, m0, n0, x0re, x0im,
     srets, smlc, ck, cv) = rest[:N_SAMPLE_IN]
    (y_ref, rets_ref, mlc_ref, mln_ref, mlm_ref, s5re_ref, s5im_ref, memk_ref, memv_ref,
     ys, sretn, smlcn, smlnn, smlmn, ss5re, ss5im,
     proj_ref, mix_ref, s_ref, c_ref, n_ref, m_ref, xre_ref, xim_ref,
     mk_ref, mv_ref, bu_ref, xcat_ref, car_ref,
     sproj, qgt, rkt, rvt, mqt, mkt, mvt, swi, sws, sot, scqt, qa8, sxa) = rest[N_SAMPLE_IN + n_prev:]
    smp = _Refs(hs_in=hs_in, normw=normw_ref, win=win_ref, wg=wg_ref, wout=wout_ref,
                cos=cos_s, s1=s1_s, s2=s2_s, gam=gam, gam8=gam8, retgn=retgn_ref, mlgn=mlgn_ref,
                bi=bi_ref, bf=bf_ref, abt=abt, bt=bt_ref, ct=ct_ref, dsk=dsk_ref, wglu=wglu_ref,
                fnw=fnw_ref, m0=m0, n0=n0, x0re=x0re, x0im=x0im, rets=srets, mlc=smlc, ck=ck, cv=cv,
                y=ys, retn=sretn, mlcn=smlcn, mlnn=smlnn, mlmn=smlmn, s5re=ss5re, s5im=ss5im,
                proj=sproj, qgt=qgt, rkt=rkt, rvt=rvt, mqt=mqt, mkt=mkt, mvt=mvt,
                wi=swi, ws=sws, ot=sot, cqt=scqt, qa8=qa8, xa=sxa)
    t = pl.program_id(1)
    step = pl.program_id(0) * n_tblocks + t

    @pl.when(step == 0)
    def _sample_first():
        _sample_pre(smp)
    hm = _head_masks()
    bd = _block_diag_mask()
    avg = _avg_matrix()
    lane128 = lax.broadcasted_iota(jnp.int32, (1, 128), 1)
    row_i = lax.broadcasted_iota(jnp.int32, (L, 128), 0)
    col_i = lax.broadcasted_iota(jnp.int32, (L, 128), 1)
    causal = row_i >= col_i
    tri_u = jnp.where(row_i <= col_i, 1.0, 0.0).astype(BF16)
    tri_sub = jnp.where(causal & (row_i // LS == col_i // LS), 1.0, 0.0)
    blk_sum = jnp.where(lax.broadcasted_iota(jnp.int32, (L // LS, 128), 0)
                        == lax.broadcasted_iota(jnp.int32, (L // LS, 128), 1) // LS, 1.0, 0.0)
    tri_ext = jnp.concatenate([tri_sub, blk_sum], axis=0).astype(BF16)

    @pl.when(t == 0)
    def _init():
        s_ref[...] = jnp.zeros_like(s_ref)
        c_ref[...] = jnp.zeros_like(c_ref)
        n_ref[...] = jnp.zeros_like(n_ref)
        m_ref[...] = jnp.zeros_like(m_ref)
        xre_ref[...] = jnp.zeros_like(xre_ref)
        xim_ref[...] = jnp.zeros_like(xim_ref)
        mkv = _dot(mem_ref[0].astype(BF16), wmkv_ref[...])
        mk = mkv[:, :GW]
        mv = mkv[:, GW:]
        memk_ref[0] = mk.T
        memv_ref[0] = mv.T
        mk_ref[...] = _stack_heads(mk, hm).astype(BF16)
        mv_ref[...] = _stack_heads(mv, hm).astype(BF16)

    _sample_body(smp, step, n_steps)

    x = x_ref[0]
    hn = _rms_norm(x, normw_ref[...]).astype(BF16)
    proj_ref[...] = _dot_nt(hn, win_ref[...])
    gates_t = _dot_nt(wg_ref[...], hn)

    nc = TB // L
    crow = [slice(c * L, (c + 1) * L) for c in range(nc)]

    def PB(blk, rows=slice(None)):
        return proj_ref[rows, blk * GW:(blk + 1) * GW]


    def xattn_stages():
        for piece in range(TB // XQ):
            rs = slice(piece * XQ, (piece + 1) * XQ)
            sc = _dot_nt(PB(AQ, rs).astype(BF16), mk_ref[...])
            yield
            ps = []
            for h in range(NH):
                seg = sc[:, h * MEM:(h + 1) * MEM]
                e = jnp.exp(seg - jnp.max(seg, axis=-1, keepdims=True))
                ps.append((e / jnp.sum(e, axis=-1, keepdims=True)).astype(BF16))
            p = jnp.concatenate(ps, axis=1)
            yield
            xa = _dot(p, mv_ref[...])
            yield
            mix_ref[rs, 3 * GW:4 * GW] = (xa * _silu(PB(AG, rs))).astype(BF16)
            yield

    def s5_stages():
        lanes = [(slice(j * 128, (j + 1) * 128), slice(NS + j * 128, NS + (j + 1) * 128))
                 for j in range(NS // 128)]
        bpc = L // LS
        nblk = TB // LS
        su = PB(SU)
        bu_ref[...] = _dot(su.astype(BF16), bt_ref[...])
        yield
        for c in range(nc):
            for lre, lim in lanes:
                br = bu_ref[crow[c], lre]
                bi = bu_ref[crow[c], lim]
                wr = jnp.tile(tab_ref[0, :, lre], (bpc, 1))
                wi = jnp.tile(tab_ref[1, :, lre], (bpc, 1))
                xcat_ref[crow[c], lre] = (wr * br - wi * bi).astype(BF16)
                xcat_ref[crow[c], lim] = (wr * bi + wi * br).astype(BF16)
            yield
        for c in range(nc):
            z = _dot(tri_ext, xcat_ref[crow[c], :])
            bu_ref[crow[c], :] = z[0:L]
            car_ref[c * bpc:(c + 1) * bpc, :] = z[L:L + bpc]
            yield
        rowb = lax.broadcasted_iota(jnp.int32, (nblk, 128), 0)
        ers, eis, prs, pis, c0s = [], [], [], [], []
        for lre, lim in lanes:
            zr = car_ref[:, lre]
            zi = car_ref[:, lim]
            pr = tab_ref[2, LS - 1:LS, lre]
            pi = tab_ref[3, LS - 1:LS, lre]
            er = pr * zr - pi * zi
            ei = pr * zi + pi * zr
            pr = tab_ref[4, LS - 1:LS, lre]
            pi = tab_ref[5, LS - 1:LS, lre]
            c0r = xre_ref[0:1, lre]
            c0i = xim_ref[0:1, lre]
            ers.append(er + jnp.where(rowb == 0, pr * c0r - pi * c0i, 0.0))
            eis.append(ei + jnp.where(rowb == 0, pr * c0i + pi * c0r, 0.0))
            prs.append(pr)
            pis.append(pi)
            c0s.append((c0r, c0i))
        yield
        for k in range(nblk.bit_length() - 1):
            d = 1 << k
            for j in range(len(lanes)):
                er, ei, pr, pi = ers[j], eis[j], prs[j], pis[j]
                sr = jnp.where(rowb >= d, pltpu.roll(er, d, 0), 0.0)
                si = jnp.where(rowb >= d, pltpu.roll(ei, d, 0), 0.0)
                ers[j], eis[j] = er + pr * sr - pi * si, ei + pr * si + pi * sr
                prs[j], pis[j] = pr * pr - pi * pi, 2.0 * (pr * pi)
            yield
        for j, (lre, lim) in enumerate(lanes):
            xre_ref[:, lre] = jnp.broadcast_to(ers[j][nblk - 1:nblk, :], (8, 128))
            xim_ref[:, lre] = jnp.broadcast_to(eis[j][nblk - 1:nblk, :], (8, 128))
            cr = jnp.where(rowb == 0, c0s[j][0], pltpu.roll(ers[j], 1, 0))
            ci_ = jnp.where(rowb == 0, c0s[j][1], pltpu.roll(eis[j], 1, 0))
            ar = tab_ref[4, 0:1, lre]
            ai = tab_ref[5, 0:1, lre]
            car_ref[:, lre] = ar * cr - ai * ci_
            car_ref[:, lim] = ar * ci_ + ai * cr
        yield
        for c in range(nc):
            for lre, lim in lanes:
                cr = jnp.concatenate(
                    [jnp.broadcast_to(car_ref[c * bpc + j:c * bpc + j + 1, lre], (LS, 128))
                     for j in range(bpc)], axis=0)
                ci_ = jnp.concatenate(
                    [jnp.broadcast_to(car_ref[c * bpc + j:c * bpc + j + 1, lim], (LS, 128))
                     for j in range(bpc)], axis=0)
                zr = bu_ref[crow[c], lre] + cr
                zi = bu_ref[crow[c], lim] + ci_
                pr = jnp.tile(tab_ref[2, :, lre], (bpc, 1))
                pi = jnp.tile(tab_ref[3, :, lre], (bpc, 1))
                xcat_ref[crow[c], lre] = (pr * zr - pi * zi).astype(BF16)
                xcat_ref[crow[c], lim] = (pr * zi + pi * zr).astype(BF16)
            yield
        sy = _dot(xcat_ref[...], ct_ref[...]) + dsk_ref[...] * su
        yield
        sy = _gelu_tanh(sy)
        gate = _dot(sy.astype(BF16), wglu_ref[...])
        yield
        mix_ref[:, 2 * GW:3 * GW] = (sy * _sigmoid(gate) * _silu(PB(SG))).astype(BF16)
        yield

    def retention_stages():
        rq = _rope(PB(RQ), cos_ref[...], s1_ref[...], s2_ref[...])
        rk = _rope(PB(RK), cos_ref[...], s1_ref[...], s2_ref[...])
        rvf = PB(RV)
        rv = rvf.astype(BF16)
        rqb = rq.astype(BF16)
        kst = [_stack_heads(rk[crow[c]], hm).astype(BF16) for c in range(nc)]
        vst = [_stack_heads(rvf[crow[c]], hm).astype(BF16) for c in range(nc)]
        yield
        inner = [_dot_nt(rqb[crow[c]], kst[c]) for c in range(nc)]
        yield
        pmat = [(inner[c] * dec_ref[...]).astype(BF16) for c in range(nc)]
        kd = [(rk[crow[c]] * kdec_ref[...]).astype(BF16) for c in range(nc)]
        yield
        rloc = [_dot(pmat[c], vst[c]) for c in range(nc)]
        upd = [_dot_tn(kd[c], rv[crow[c]]) for c in range(nc)]
        yield
        st = [s_ref[...]]
        for c in range(nc):
            st.append(st[c] * cdec_ref[...] + jnp.where(bd, upd[c], 0.0))
        s_ref[...] = st[nc]
        qd = [(rq[crow[c]] * qdec_ref[...]).astype(BF16) for c in range(nc)]
        yield
        ost = [_dot(qd[c], st[c].astype(BF16)) for c in range(nc)]
        yield
        ro = jnp.concatenate([rloc[c] + ost[c] for c in range(nc)], axis=0)
        mix_ref[:, 0:GW] = (_head_norm(ro, retgn_ref[...], avg) * _silu(PB(RG))).astype(BF16)
        yield

    def mlstm_stages():
        ig_r = gates_t[0:8] + bi_ref[...]
        lf_r = _log_sigmoid(gates_t[8:16] + bf_ref[...])
        yield
        b_r = [_dot_x3(lf_r[:, crow[c]], tri_u) for c in range(nc)]
        yield
        gd_r = [ig_r[:, crow[c]] - b_r[c] for c in range(nc)]
        lane_r = lax.broadcasted_iota(jnp.int32, (8, L), 1)
        cm_r = list(gd_r)
        for k in range(7):
            d = 1 << k
            cm_r = [jnp.maximum(v, jnp.where(lane_r >= d, pltpu.roll(v, d, 1), NEG_INF))
                    for v in cm_r]
            yield
        m_prev = [m_ref[...]]
        mt_r = []
        for c in range(nc):
            mt = jnp.maximum(b_r[c] + m_prev[c], b_r[c] + cm_r[c])
            mt_r.append(mt)
            m_prev.append(jnp.broadcast_to(mt[:, L - 1:L], (8, L)))
            yield
        m_ref[...] = m_prev[nc]
        cols = []
        for c in range(nc):
            bm = b_r[c] - mt_r[c]
            ws = jnp.exp(b_r[c] + m_prev[c] - mt_r[c])
            wl = jnp.exp(gd_r[c] + jnp.broadcast_to(bm[:, L - 1:L], (8, L)))
            emt = jnp.exp(-mt_r[c])
            cols.append(jnp.concatenate([bm, ws, wl, emt, jnp.zeros((L - 32, L), F32)], axis=0).T)
        mqf = PB(MQ)
        mq = mqf.astype(BF16)
        mkf = PB(MK)
        mvf = PB(MV)
        mv_ = mvf.astype(BF16)
        kst = [_stack_heads(mkf[crow[c]], hm).astype(BF16) for c in range(nc)]
        vst = [_stack_heads(mvf[crow[c]], hm).astype(BF16) for c in range(nc)]
        yield
        sraw = [_dot_nt(mq[crow[c]], kst[c]) for c in range(nc)]
        yield
        smat, den_i = [], []
        for c in range(nc):
            parts, dens = [], []
            for h in range(NH):
                arg = jnp.where(causal, cols[c][:, h:h + 1] + gd_r[c][h:h + 1, :], NEG_INF)
                s_h = sraw[c][:, h * L:(h + 1) * L] * jnp.exp(arg)
                dens.append(jnp.sum(s_h, axis=-1, keepdims=True))
                parts.append(s_h.astype(BF16))
            den_i.append(dens)
            smat.append(jnp.concatenate(parts, axis=1))
            yield
        ones_blk = jnp.ones((L, 128), BF16)
        def per_head(columns):
            return jnp.where(hm[0], columns[0],
                             jnp.where(hm[1], columns[1], jnp.where(hm[2], columns[2], columns[3])))

        kws = [(mkf[crow[c]] * per_head([cols[c][:, 16 + h:17 + h] for h in range(NH)])
                ).astype(BF16) for c in range(nc)]
        yield
        rloc = [_dot(smat[c], vst[c]) for c in range(nc)]
        u = [_dot_tn(kws[c], jnp.concatenate([mv_[crow[c]], ones_blk], axis=1))
             for c in range(nc)]
        yield
        nmask = (lax.broadcasted_iota(jnp.int32, (GW, 128), 0) // DH
                 == lax.broadcasted_iota(jnp.int32, (GW, 128), 1))
        cst = [c_ref[...]]
        nst = [n_ref[...]]
        for c in range(nc):
            wsl256 = jnp.zeros((1, GW), F32)
            wsl128 = jnp.zeros((1, 128), F32)
            for h in range(NH):
                wsl = cols[c][L - 1:L, 8 + h:9 + h]
                wsl256 = wsl256 + jnp.where(hm[h], wsl, 0.0)
                wsl128 = wsl128 + jnp.where(lane128 == h, wsl, 0.0)
            cst.append(cst[c] * wsl256 + jnp.where(bd, u[c][:, :GW], 0.0))
            nst.append(nst[c] * wsl128 + jnp.where(nmask, u[c][:, GW:], 0.0))
        c_ref[...] = cst[nc]
        n_ref[...] = nst[nc]
        yield
        qc = [_dot(mq[crow[c]], cst[c].astype(BF16)) for c in range(nc)]
        qn = [_dot(mq[crow[c]], nst[c].astype(BF16)) for c in range(nc)]
        yield
        mhs = []
        for c in range(nc):
            wsc = [cols[c][:, 8 + h:9 + h] for h in range(NH)]
            rdn = []
            for h in range(NH):
                den = den_i[c][h] + wsc[h] * qn[c][:, h:h + 1]
                rdn.append(1.0 / jnp.maximum(jnp.abs(den), cols[c][:, 24 + h:25 + h]))
            mhs.append((rloc[c] + per_head(wsc) * qc[c]) * per_head(rdn))
            yield
        mh = jnp.concatenate(mhs, axis=0) * _sigmoid(PB(MO))
        mix_ref[:, GW:2 * GW] = (_head_norm(mh, mlgn_ref[...], avg) * _silu(PB(MG))).astype(BF16)
        yield

    pending = [mlstm_stages(), s5_stages(), retention_stages(), xattn_stages()]
    while pending:
        pending = [stage for stage in pending if next(stage, _DONE) is not _DONE]

    y = x_ref[0] + _dot(mix_ref[...], wout_ref[...])
    if last_layer:
        y = _rms_norm(y, fnw_ref[...])
    y_ref[0] = y

    @pl.when(t == n_tblocks - 1)
    def _final():
        s_fin = s_ref[...]
        c_fin = c_ref[...].T
        n_fin = n_ref[...].T
        for h in range(NH):
            blk = slice(h * DH, (h + 1) * DH)
            rets_ref[0, h] = s_fin[blk, blk]
            mlc_ref[0, h] = c_fin[blk, blk]
        mln_ref[0] = jnp.concatenate([n_fin[h:h + 1, h * DH:(h + 1) * DH] for h in range(NH)],
                                     axis=0)
        mlm_ref[0] = m_ref[...]
        s5re_ref[0] = xre_ref[...]
        s5im_ref[0] = xim_ref[...]

    @pl.when(step == n_steps - 1)
    def _sample_last():
        _sample_post(smp, last_layer)


def _layer(layer, x, mem, hs, st, w, consts, last_layer, prev):
    bsz, seq, _ = x.shape
    nsamp = hs.shape[0]
    depth = w["w_in"].shape[0]
    nt = seq // TB
    n_steps = bsz * nt
    srows = GW // n_steps
    sb = nsamp // n_steps
    assert srows * n_steps == GW and sb * n_steps == nsamp and n_steps % NH == 0
    full = lambda shape: pl.BlockSpec(shape, lambda b, t: (0,) * len(shape),
                                      pipeline_mode=pl.Buffered(1))
    lyr = lambda shape: pl.BlockSpec((None,) + shape, lambda b, t: (layer,) + (0,) * len(shape),
                                     pipeline_mode=pl.Buffered(1))
    tok = lambda width: pl.BlockSpec((TB, width), lambda b, t: (t, 0))
    per_b = lambda r, c: pl.BlockSpec((1, r, c), lambda b, t: (b, 0, 0),
                                      pipeline_mode=pl.Buffered(1))
    in_specs = [
        pl.BlockSpec((1, TB, D), lambda b, t: (b, t, 0)),
        per_b(MEM, D),
        lyr((1, D)), lyr((DP, D)), lyr((NGATE, D)), lyr((D, D)), lyr((D, 2 * GW)),
        tok(128), tok(128), tok(128),
        full((L, NH * L)), full((L, GW)), full((L, GW)), full((1, GW)),
        lyr((1, GW)), lyr((1, GW)), lyr((8, TB)), lyr((8, TB)),
        lyr((6, LS, NS)), lyr((GW, 2 * NS)), lyr((2 * NS, GW)),
        lyr((1, GW)), lyr((GW, GW)), full((1, D)),
    ]
    slab = pl.BlockSpec((None, srows, DH, nsamp), lambda b, t: (layer, b * nt + t, 0, 0))
    cache = pl.BlockSpec((None, sb, GW, MEM), lambda b, t: (layer, b * nt + t, 0, 0))
    sample_in_specs = [
        full((nsamp, D)), full((1, 128)), full((1, 128)), full((1, 128)),
        full((1, GW)), full((8, nsamp)),
        lyr((2 * NS, nsamp)), lyr((8, nsamp)), lyr((GW, nsamp)), lyr((NS, nsamp)), lyr((NS, nsamp)),
        slab, slab, cache, cache,
    ]
    assert len(sample_in_specs) == N_SAMPLE_IN
    n_in = len(in_specs) + N_SAMPLE_IN
    in_specs = in_specs + sample_in_specs + [pl.BlockSpec(memory_space=pl.ANY)] * len(prev)
    state_shapes = [(NH, DH, DH),
                    (NH, DH, DH),
                    (NH, DH),
                    (8, 128),
                    (8, NS),
                    (8, NS),
                    (GW, MEM),
                    (GW, MEM)]
    n_state = N_STATE_OUT + N_SAMPLE_STATE_OUT
    assert len(state_shapes) == N_STATE_OUT and len(prev) in (0, n_state)
    lyr_out = lambda shape: pl.BlockSpec((None,) + shape,
                                         lambda b, t: (layer,) + (0,) * len(shape))
    sample_state = [((GW, DH, nsamp), slab), ((GW, DH, nsamp), slab),
                    ((GW, nsamp), lyr_out((GW, nsamp))),
                    ((8, nsamp), lyr_out((8, nsamp))),
                    ((NS, nsamp), lyr_out((NS, nsamp))),
                    ((NS, nsamp), lyr_out((NS, nsamp)))]
    assert len(sample_state) == N_SAMPLE_STATE_OUT
    out_shape = ((jax.ShapeDtypeStruct((bsz, seq, D), F32),)
                 + tuple(jax.ShapeDtypeStruct((depth, bsz) + s, F32) for s in state_shapes)
                 + (jax.ShapeDtypeStruct((nsamp, D), F32),)
                 + tuple(jax.ShapeDtypeStruct((depth,) + s, F32) for s, _ in sample_state))
    state_spec = lambda s: pl.BlockSpec((None, 1) + s,
                                        lambda b, t: (layer, b) + (0,) * len(s))
    out_specs = ((pl.BlockSpec((1, TB, D), lambda b, t: (b, t, 0)),)
                 + tuple(state_spec(s) for s in state_shapes)
                 + (pl.BlockSpec((nsamp, D), lambda b, t: (0, 0)),)
                 + tuple(spec for _, spec in sample_state))
    aliased_outs = list(range(1, 1 + N_STATE_OUT)) + list(range(2 + N_STATE_OUT, 2 + n_state))
    scratch = [
        pltpu.VMEM((TB, DP), F32),
        pltpu.VMEM((TB, D), BF16),
        pltpu.VMEM((GW, GW), F32),
        pltpu.VMEM((GW, GW), F32),
        pltpu.VMEM((GW, 128), F32),
        pltpu.VMEM((8, 128), F32),
        pltpu.VMEM((8, NS), F32),
        pltpu.VMEM((8, NS), F32),
        pltpu.VMEM((NH * MEM, GW), BF16),
        pltpu.VMEM((NH * MEM, GW), BF16),
        pltpu.VMEM((TB, 2 * NS), F32),
        pltpu.VMEM((TB, 2 * NS), BF16),
        pltpu.VMEM((TB // LS, 2 * NS), F32),
        pltpu.VMEM((nsamp, DP), F32),
        pltpu.VMEM((GW, nsamp), F32), pltpu.VMEM((GW, nsamp), F32), pltpu.VMEM((GW, nsamp), F32),
        pltpu.VMEM((GW, nsamp), F32), pltpu.VMEM((GW, nsamp), F32), pltpu.VMEM((GW, nsamp), F32),
        pltpu.VMEM((8, nsamp), F32), pltpu.VMEM((8, nsamp), F32),
        pltpu.VMEM((GW, nsamp), F32), pltpu.VMEM((GW, nsamp), F32),
        pltpu.VMEM((8 * nsamp, GW), F32),
        pltpu.VMEM((nsamp, GW), F32),
    ]
    kern = functools.partial(_prompt_kernel, last_layer=last_layer, n_tblocks=nt,
                             n_steps=n_steps, n_prev=len(prev))
    outs = pl.pallas_call(
        kern, grid=(bsz, nt), in_specs=in_specs, out_specs=out_specs, out_shape=out_shape,
        scratch_shapes=scratch, name="layer",
        input_output_aliases={n_in + i: aliased_outs[i] for i in range(len(prev))},
        compiler_params=pltpu.CompilerParams(
            dimension_semantics=("arbitrary", "arbitrary"), vmem_limit_bytes=VMEM_LIMIT),
    )(x, mem, w["norm_w"], w["w_in"], w["w_gate"], w["w_out"], w["w_mem_kv"],
      consts["cos_p"], consts["s1_p"], consts["s2_p"],
      consts["dec"], consts["qdec"], consts["kdec"], consts["cdec"],
      w["ret_gn"], w["ml_gn"], w["b_i8"], w["b_f8"],
      w["tab"], w["bt"], w["ct"], w["s5_d"], w["w_glu"], w["final_norm_w"],
      hs, consts["cos_s"], consts["s1_s"], consts["s2_s"], consts["gam"], consts["gam8"],
      w["abt"], st["m"], st["n"], st["s5_re"], st["s5_im"], st["ret"], st["c"],
      st["mem_k"], st["mem_v"], *prev)
    y = outs[0]
    prompt_states = outs[1:1 + N_STATE_OUT]
    hs_out = outs[1 + N_STATE_OUT]
    sample_states = outs[2 + N_STATE_OUT:]
    return y, hs_out, tuple(prompt_states), tuple(sample_states)


class _Refs:
    def __init__(self, **refs):
        self.__dict__.update(refs)


def _sample_proj(r, blk):
    return r.proj[:, blk * GW:(blk + 1) * GW]


def _mask8():
    return (lax.broadcasted_iota(jnp.int32, (8, GW), 0)
            == lax.broadcasted_iota(jnp.int32, (8, GW), 1) // DH)


def _sample_pre(r):
    nsamp = r.hs_in.shape[0]
    P = functools.partial(_sample_proj, r)
    hn = _rms_norm(r.hs_in[...], r.normw[...]).astype(BF16)
    r.proj[...] = _dot_nt(hn, r.win[...])
    cosr, s1, s2 = r.cos[...], r.s1[...], r.s2[...]
    r.qgt[...] = (_rope(P(RQ), cosr, s1, s2) * r.gam[...]).T
    r.rkt[...] = _rope(P(RK), cosr, s1, s2).T
    r.rvt[...] = P(RV).T
    r.mqt[...] = P(MQ).T
    r.mkt[...] = P(MK).T
    r.mvt[...] = P(MV).T
    gt = _dot_nt(r.wg[...], hn)
    ig = gt[0:8] + r.bi[:, 0:nsamp]
    lf = _log_sigmoid(gt[8:16] + r.bf[:, 0:nsamp])
    a = lf + r.m0[...]
    mt = jnp.maximum(a, ig)
    r.wi[...] = jnp.exp(ig - mt)
    r.ws[...] = jnp.exp(a - mt)
    r.mlmn[...] = mt
    r.ot[...] = jnp.zeros_like(r.ot)
    r.qa8[...] = jnp.where(_mask8()[None], P(AQ)[:, None, :], 0.0).reshape(8 * nsamp, GW)
    but = _dot_tn(r.bt[...], P(SU).T.astype(BF16))
    are, aim = r.abt[0:NS, :], r.abt[NS:2 * NS, :]
    x0r, x0i = r.x0re[...], r.x0im[...]
    r.s5re[...] = are * x0r - aim * x0i + but[0:NS]
    r.s5im[...] = are * x0i + aim * x0r + but[NS:2 * NS]


def _sample_body(r, g, n_steps):
    nsamp = r.hs_in.shape[0]
    srows = GW // n_steps
    sb = nsamp // n_steps
    mask8 = _mask8()
    head = g // (n_steps // NH)
    hrow = pl.ds(pl.multiple_of(head * DH, DH), DH)
    gam_row = r.gam8[pl.ds(head, 1), :]
    ws_row = r.ws[pl.ds(head, 1), :]
    wi_row = r.wi[pl.ds(head, 1), :]
    v_slab = r.rvt[hrow, :]
    q_slab = r.mqt[hrow, :]
    k_slab = r.mkt[hrow, :]
    o_acc = jnp.zeros((DH, nsamp), F32)
    for i in range(srows):
        row = pl.ds(g * srows + i, 1)
        s_t = r.rets[i]
        o_acc = o_acc + r.qgt[row, :] * s_t
        r.retn[i] = gam_row * s_t + r.rkt[row, :] * v_slab
        c_t = r.mlc[i]
        r.cqt[row, :] = jnp.sum(c_t * q_slab, axis=0, keepdims=True)
        r.mlcn[i] = ws_row * c_t + (wi_row * r.mvt[row, :]) * k_slab
    r.ot[hrow, :] = r.ot[hrow, :] + o_acc
    tiles = [pl.ds(pl.multiple_of((g * sb + i) * 8, 8), 8) for i in range(sb)]
    q8 = [r.qa8[tiles[i], :].astype(BF16) for i in range(sb)]
    sc = [_dot(q8[i], r.ck[i].astype(BF16)) for i in range(sb)]
    ps = []
    for i in range(sb):
        e = jnp.exp(sc[i] - jnp.max(sc[i], axis=-1, keepdims=True))
        ps.append((e / jnp.sum(e, axis=-1, keepdims=True)).astype(BF16))
    ta = [_dot_nt(ps[i], r.cv[i].astype(BF16)) for i in range(sb)]
    for i in range(sb):
        r.xa[pl.ds(g * sb + i, 1), :] = jnp.sum(jnp.where(mask8, ta[i], 0.0), axis=0, keepdims=True)


def _sample_post(r, last_layer):
    P = functools.partial(_sample_proj, r)
    ones_bd = _ones_matrix()
    avg = _avg_matrix()
    cosr, s1, s2 = r.cos[...], r.s1[...], r.s2[...]
    rq = _rope(P(RQ), cosr, s1, s2)
    rk = _rope(P(RK), cosr, s1, s2)
    ro = _dot_x2(rq * rk, ones_bd) * P(RV) + r.ot[...].T
    ret_out = _head_norm(ro, r.retgn[...], avg) * _silu(P(RG))
    mht = []
    for h in range(NH):
        rows = slice(h * DH, (h + 1) * DH)
        wi = r.wi[h:h + 1, :]
        ws = r.ws[h:h + 1, :]
        emt = jnp.exp(-r.mlmn[h:h + 1, :])
        qt, kt, n0 = r.mqt[rows, :], r.mkt[rows, :], r.n0[rows, :]
        s = jnp.sum(qt * kt, axis=0, keepdims=True) * wi
        den = s + ws * jnp.sum(n0 * qt, axis=0, keepdims=True)
        mht.append((s * r.mvt[rows, :] + ws * r.cqt[rows, :]) / jnp.maximum(jnp.abs(den), emt))
        r.mlnn[rows, :] = ws * n0 + wi * kt
    mh = jnp.concatenate(mht, axis=0).T * _sigmoid(P(MO))
    ml_out = _head_norm(mh, r.mlgn[...], avg) * _silu(P(MG))
    su = P(SU)
    xcat = jnp.concatenate([r.s5re[...], r.s5im[...]], axis=0).astype(BF16)
    sy = _dot_tn(xcat, r.ct[...]) + r.dsk[...] * su
    sy = _gelu_tanh(sy)
    sy = sy * _sigmoid(_dot(sy.astype(BF16), r.wglu[...]))
    s5_out = sy * _silu(P(SG))
    xa_out = r.xa[...] * _silu(P(AG))
    mix = jnp.concatenate([ret_out, ml_out, s5_out, xa_out], axis=1).astype(BF16)
    y = r.hs_in[...] + _dot(mix, r.wout[...])
    if last_layer:
        y = _rms_norm(y, r.fnw[...])
    r.y[...] = y


def _rope_tables(pos):
    half = DH // 2
    inv = ROPE_BASE ** (-np.arange(half, dtype=np.float64) / half)
    ang = np.asarray(pos, np.float64)[:, None] * inv[None, :]
    cos, sin = np.cos(ang), np.sin(ang)
    zero = np.zeros_like(sin)
    c = np.tile(np.concatenate([cos, cos], axis=-1), (1, 2))
    s1 = np.tile(np.concatenate([zero, sin], axis=-1), (1, 2))
    s2 = np.tile(np.concatenate([-sin, zero], axis=-1), (1, 2))
    return tuple(jnp.asarray(t, F32) for t in (c, s1, s2))


def _constants(seq):
    lg = np.log1p(-np.power(2.0, -5.0 - np.arange(NH, dtype=np.float64)))[:, None]
    idx = np.arange(L, dtype=np.float64)
    diff = idx[:, None] - idx[None, :]
    decay = np.where(diff >= 0, np.exp(lg[:, :, None] * np.maximum(diff, 0.0)), 0.0)
    rep = lambda t: np.repeat(t, DH, axis=0).T
    consts = {
        "dec": np.transpose(decay, (1, 0, 2)).reshape(L, NH * L),
        "qdec": rep(np.exp(lg * (idx + 1.0))),
        "kdec": rep(np.exp(lg * (L - 1.0 - idx))),
        "cdec": rep(np.exp(lg * L)),
        "gam": rep(np.exp(lg * 1.0)),
        "gam8": np.pad(np.broadcast_to(np.exp(lg), (NH, 128)), ((0, 8 - NH), (0, 0))),
    }
    consts = {k: jnp.asarray(v, F32) for k, v in consts.items()}
    consts["cos_p"], consts["s1_p"], consts["s2_p"] = _rope_tables(np.arange(seq))
    consts["cos_s"], consts["s1_s"], consts["s2_s"] = _rope_tables(PAST_LEN + np.arange(1))
    return consts


def _pack_w_in(w_in):
    wt = jnp.swapaxes(w_in, 1, 2)
    sizes = (GW,) * 9 + (NH, NH) + (GW,) * 4
    offs = np.concatenate([[0], np.cumsum(sizes)])
    seg = [wt[:, int(offs[i]):int(offs[i + 1]), :] for i in range(len(sizes))]
    scale = DH ** -0.5
    pad8 = lambda t: jnp.pad(t, ((0, 0), (0, 8 - t.shape[1]), (0, 0)))
    blocks = [seg[11],
              seg[0], seg[1] * scale, seg[2], seg[3],
              seg[4], seg[5] * scale, seg[6], seg[7], seg[8],
              seg[12], seg[13] * scale, seg[14]]
    gates = jnp.concatenate([pad8(seg[9]), pad8(seg[10])], axis=1)
    return jnp.concatenate(blocks, axis=1).astype(BF16), gates.astype(BF16)


def kernel(x_prompt, x_sample, mem_prompt, state_ret, state_mlstm_c, state_mlstm_n, state_mlstm_m,
           state_s5_re, state_s5_im, cache_mem_k, cache_mem_v,
           norm_w, w_in, ret_gn, ml_b_i, ml_b_f, ml_gn,
           s5_a_re, s5_a_im, s5_log_dt, s5_b_re, s5_b_im, s5_c_re, s5_c_im, s5_d, s5_w_glu,
           w_mem_k, w_mem_v, w_out, final_norm_w):
    depth = norm_w.shape[0]
    bp, seq, _ = x_prompt.shape
    bs = x_sample.shape[0]
    consts = _constants(seq)
    abt, bt, ct, tab = _s5_prepare(s5_a_re, s5_a_im, s5_log_dt, s5_b_re, s5_b_im, s5_c_re, s5_c_im)
    rows8 = lambda t: jnp.pad(jnp.broadcast_to(t[:, :, None], (depth, NH, TB)),
                              ((0, 0), (0, 8 - NH), (0, 0)))
    w_main, w_gate = _pack_w_in(w_in)
    w = {
        "norm_w": norm_w[:, None], "w_in": w_main, "w_gate": w_gate, "w_out": w_out.astype(BF16),
        "w_mem_kv": jnp.concatenate([w_mem_k, w_mem_v], axis=-1).astype(BF16),
        "ret_gn": ret_gn[:, None], "ml_gn": ml_gn[:, None],
        "b_i8": rows8(ml_b_i), "b_f8": rows8(ml_b_f),
        "abt": abt, "tab": tab, "bt": bt, "ct": ct,
        "s5_d": s5_d[:, None], "w_glu": s5_w_glu.astype(BF16),
        "final_norm_w": final_norm_w[None],
    }

    st = {
        "m": jnp.pad(jnp.swapaxes(state_mlstm_m, 1, 2), ((0, 0), (0, 8 - NH), (0, 0))),
        "n": jnp.transpose(state_mlstm_n, (0, 2, 3, 1)).reshape(depth, GW, bs),
        "s5_re": jnp.transpose(state_s5_re, (0, 2, 3, 1)).reshape(depth, NS, bs),
        "s5_im": jnp.transpose(state_s5_im, (0, 2, 3, 1)).reshape(depth, NS, bs),
        "ret": jnp.transpose(state_ret, (0, 2, 3, 4, 1)).reshape(depth, GW, DH, bs),
        "c": jnp.transpose(state_mlstm_c, (0, 2, 3, 4, 1)).reshape(depth, GW, DH, bs),
        "mem_k": jnp.transpose(cache_mem_k, (0, 1, 3, 4, 2)).reshape(depth, bs, GW, MEM),
        "mem_v": jnp.transpose(cache_mem_v, (0, 1, 3, 4, 2)).reshape(depth, bs, GW, MEM),
    }
    hp = x_prompt
    hs = x_sample.reshape(bs, D)
    states = ()
    for l in range(depth):
        hp, hs, p_states, s_states = _layer(l, hp, mem_prompt, hs, st, w, consts,
                                            l == depth - 1, states)
        states = p_states + s_states
    ret_p, mlc_p, mln_p, ms, xr, xi, mk, mv = p_states
    rn, cn, nn, mn, sr, si = s_states
    mlm_p = ms[:, :, :NH, 0]
    s5re_p = xr[:, :, 0].reshape(depth, bp, S5G, S5P)
    s5im_p = xi[:, :, 0].reshape(depth, bp, S5G, S5P)
    memk_p = jnp.transpose(mk.reshape(depth, bp, NH, DH, MEM), (0, 1, 4, 2, 3))
    memv_p = jnp.transpose(mv.reshape(depth, bp, NH, DH, MEM), (0, 1, 4, 2, 3))
    back5 = lambda t: jnp.transpose(t.reshape(depth, NH, DH, DH, bs), (0, 4, 1, 2, 3))
    back4 = lambda t, a, b: jnp.transpose(t.reshape(depth, a, b, bs), (0, 3, 1, 2))
    return (hp, hs.reshape(bs, 1, D),
            ret_p, back5(rn), mlc_p, back5(cn),
            mln_p, back4(nn, NH, DH), mlm_p, jnp.swapaxes(mn[:, :NH], 1, 2),
            s5re_p, back4(sr, S5G, S5P), s5im_p, back4(si, S5G, S5P),
            memk_p, memv_p)
```

```python
import functools
import math

import numpy as np
import jax
import jax.numpy as jnp
from jax import lax
from jax.experimental import pallas as pl
from jax.experimental.pallas import tpu as pltpu

F32 = jnp.float32
BF16 = jnp.bfloat16

D = 1024
GW = 256
NH = 4
DH = 64
L = 128
LS = 8
S5G = 16
S5P = 64
S5C = 16
NS = S5G * S5P
MEM = 256
EPS = 1e-6
NEG_INF = -1e30
ROPE_BASE = 10000.0
PAST_LEN = 16384

TB = 512
SB = 8
XQ = 512
NBLK = 13
DP = NBLK * GW
(SU, RQ, RK, RV, RG, MQ, MK, MV, MO, MG, SG, AQ, AG) = range(NBLK)
NGATE = 16

VMEM_LIMIT = 56 * 1024 * 1024


_DONE = object()


def _dot(a, b):
    return jnp.dot(a, b, preferred_element_type=F32)


def _dot_nt(a, b):
    return lax.dot_general(a, b, (((1,), (1,)), ((), ())), preferred_element_type=F32)


def _dot_tn(a, b):
    return lax.dot_general(a, b, (((0,), (0,)), ((), ())), preferred_element_type=F32)


def _split2(x):
    hi = x.astype(BF16)
    lo = (x - hi.astype(F32)).astype(BF16)
    return hi, lo


def _dot_x2(x, w):
    hi, lo = _split2(x)
    return _dot(hi, w) + _dot(lo, w)


def _dot_x3(x, w):
    hi = x.astype(BF16)
    r = x - hi.astype(F32)
    mid = r.astype(BF16)
    lo = (r - mid.astype(F32)).astype(BF16)
    return _dot(hi, w) + _dot(mid, w) + _dot(lo, w)


def _sigmoid(x):
    return 0.5 * (1.0 + jnp.tanh(0.5 * x))


def _silu(x):
    h = 0.5 * x
    return h + h * jnp.tanh(h)


def _log_sigmoid(x):
    return jnp.minimum(x, 0.0) - jnp.log1p(jnp.exp(-jnp.abs(x)))


def _gelu_tanh(x):
    c = math.sqrt(2.0 / math.pi)
    h = 0.5 * x
    return h + h * jnp.tanh(x * (c + (0.044715 * c) * (x * x)))


def _lane_head(n):
    return lax.broadcasted_iota(jnp.int32, (1, n), 1) // DH


def _head_masks():
    lh = _lane_head(GW)
    return [lh == h for h in range(NH)]


def _block_diag_mask():
    r = lax.broadcasted_iota(jnp.int32, (GW, GW), 0) // DH
    c = lax.broadcasted_iota(jnp.int32, (GW, GW), 1) // DH
    return r == c


def _avg_matrix():
    return jnp.where(_block_diag_mask(), 1.0 / DH, 0.0).astype(BF16)


def _ones_matrix():
    return jnp.where(_block_diag_mask(), 1.0, 0.0).astype(BF16)


def _rope(x, cos, s1, s2):
    outs = []
    for j in range(2):
        xs = x[:, j * 128:(j + 1) * 128]
        outs.append(xs * cos + pltpu.roll(xs, 32, 1) * s1 + pltpu.roll(xs, 96, 1) * s2)
    return jnp.concatenate(outs, axis=1)


def _head_norm(x, gain, avg):
    mu = _dot_x2(x, avg)
    d = x - mu
    var = _dot((d * d).astype(BF16), avg)
    return d * lax.rsqrt(var + EPS) * gain


def _rms_norm(x, w):
    ms = jnp.mean(x * x, axis=-1, keepdims=True)
    return x * lax.rsqrt(ms + EPS) * w


def _stack_heads(x, hm):
    return jnp.concatenate([jnp.where(hm[h], x, 0.0) for h in range(NH)], axis=0)


def _s5_prep_kernel(are_ref, aim_ref, ldt_ref, bre_ref, bim_ref, cre_ref, cim_ref,
                    abt_ref, bt_ref, ct_ref, tab_ref):
    a_re = are_ref[0]
    a_im = aim_ref[0]
    dt = jnp.exp(ldt_ref[0])
    lam_re = a_re * dt
    lam_im = a_im * dt
    mag = jnp.exp(lam_re)
    ab_re = mag * jnp.cos(lam_im)
    ab_im = mag * jnp.sin(lam_im)
    den = a_re * a_re + a_im * a_im
    nr = ab_re - 1.0
    ni = ab_im
    f_re = (nr * a_re + ni * a_im) / den
    f_im = (ni * a_re - nr * a_im) / den
    abt_ref[0, 0:NS, :] = jnp.broadcast_to(ab_re, (128, NS)).T
    abt_ref[0, NS:2 * NS, :] = jnp.broadcast_to(ab_im, (128, NS)).T
    b_re = bre_ref[0]
    b_im = bim_ref[0]
    bb_re = (f_re * b_re - f_im * b_im).astype(BF16)
    bb_im = (f_re * b_im + f_im * b_re).astype(BF16)
    rep_r = (lax.broadcasted_iota(jnp.int32, (GW, S5C), 0) % S5C
             == lax.broadcasted_iota(jnp.int32, (GW, S5C), 1)).astype(BF16)
    in_blk = (lax.broadcasted_iota(jnp.int32, (GW, NS), 0) // S5C
              == lax.broadcasted_iota(jnp.int32, (GW, NS), 1) // S5P)
    bt_ref[0, :, :NS] = jnp.where(in_blk, _dot(rep_r, bb_re), 0.0).astype(BF16)
    bt_ref[0, :, NS:] = jnp.where(in_blk, _dot(rep_r, bb_im), 0.0).astype(BF16)
    rep_c = (lax.broadcasted_iota(jnp.int32, (S5C, GW), 0)
             == lax.broadcasted_iota(jnp.int32, (S5C, GW), 1) % S5C).astype(BF16)
    out_blk = (lax.broadcasted_iota(jnp.int32, (NS, GW), 0) // S5P
               == lax.broadcasted_iota(jnp.int32, (NS, GW), 1) // S5C)
    ct_ref[0, :NS, :] = jnp.where(out_blk, _dot(cre_ref[0].astype(BF16), rep_c), 0.0).astype(BF16)
    ct_ref[0, NS:, :] = jnp.where(out_blk, -_dot(cim_ref[0].astype(BF16), rep_c), 0.0).astype(BF16)
    i = lax.broadcasted_iota(jnp.int32, (LS, NS), 0).astype(F32)
    for slot, k in ((0, -i), (2, i), (4, i + 1.0)):
        pmag = jnp.exp(k * lam_re)
        tab_ref[0, slot] = pmag * jnp.cos(k * lam_im)
        tab_ref[0, slot + 1] = pmag * jnp.sin(k * lam_im)


def _s5_prepare(a_re, a_im, log_dt, b_re, b_im, c_re, c_im):
    depth = a_re.shape[0]
    are = a_re.reshape(depth, 1, NS)
    aim = a_im.reshape(depth, 1, NS)
    ldt = jnp.repeat(log_dt, S5P, axis=-1).reshape(depth, 1, NS)
    bre = jnp.transpose(b_re, (0, 3, 1, 2)).reshape(depth, S5C, NS)
    bim = jnp.transpose(b_im, (0, 3, 1, 2)).reshape(depth, S5C, NS)
    cre = jnp.transpose(c_re, (0, 1, 3, 2)).reshape(depth, NS, S5C)
    cim = jnp.transpose(c_im, (0, 1, 3, 2)).reshape(depth, NS, S5C)
    per_layer = lambda *shape: pl.BlockSpec((1,) + shape, lambda l: (l,) + (0,) * len(shape))
    out_shape = (jax.ShapeDtypeStruct((depth, 2 * NS, 128), F32),
                 jax.ShapeDtypeStruct((depth, GW, 2 * NS), BF16),
                 jax.ShapeDtypeStruct((depth, 2 * NS, GW), BF16),
                 jax.ShapeDtypeStruct((depth, 6, LS, NS), F32))
    return pl.pallas_call(
        _s5_prep_kernel, grid=(depth,),
        in_specs=[per_layer(1, NS), per_layer(1, NS), per_layer(1, NS),
                  per_layer(S5C, NS), per_layer(S5C, NS), per_layer(NS, S5C), per_layer(NS, S5C)],
        out_specs=(per_layer(2 * NS, 128), per_layer(GW, 2 * NS), per_layer(2 * NS, GW),
                   per_layer(6, LS, NS)),
        out_shape=out_shape, name="s5_prepare",
        compiler_params=pltpu.CompilerParams(dimension_semantics=("arbitrary",)),
    )(are, aim, ldt, bre, bim, cre, cim)


N_STATE_OUT = 8


def _prompt_kernel(x_ref, mem_ref, normw_ref, win_ref, wg_ref, wout_ref, wmkv_ref,
                   cos_ref, s1_ref, s2_ref, dec_ref, qdec_ref, kdec_ref, cdec_ref,
                   retgn_ref, mlgn_ref, bi_ref, bf_ref,
                   tab_ref, bt_ref, ct_ref, dsk_ref, wglu_ref, fnw_ref,
                   *rest, last_layer, n_tblocks, n_prev):
    (y_ref, rets_ref, mlc_ref, mln_ref, mlm_ref, s5re_ref, s5im_ref, memk_ref, memv_ref,
     proj_ref, mix_ref, s_ref, c_ref, n_ref, m_ref, xre_ref, xim_ref,
     mk_ref, mv_ref, bu_ref, xcat_ref, car_ref) = rest[n_prev:]
    t = pl.program_id(1)
    hm = _head_masks()
    bd = _block_diag_mask()
    avg = _avg_matrix()
    lane128 = lax.broadcasted_iota(jnp.int32, (1, 128), 1)
    row_i = lax.broadcasted_iota(jnp.int32, (L, 128), 0)
    col_i = lax.broadcasted_iota(jnp.int32, (L, 128), 1)
    causal = row_i >= col_i
    tri_u = jnp.where(row_i <= col_i, 1.0, 0.0).astype(BF16)
    tri_sub = jnp.where(causal & (row_i // LS == col_i // LS), 1.0, 0.0)
    blk_sum = jnp.where(lax.broadcasted_iota(jnp.int32, (L // LS, 128), 0)
                        == lax.broadcasted_iota(jnp.int32, (L // LS, 128), 1) // LS, 1.0, 0.0)
    tri_ext = jnp.concatenate([tri_sub, blk_sum], axis=0).astype(BF16)

    @pl.when(t == 0)
    def _init():
        s_ref[...] = jnp.zeros_like(s_ref)
        c_ref[...] = jnp.zeros_like(c_ref)
        n_ref[...] = jnp.zeros_like(n_ref)
        m_ref[...] = jnp.zeros_like(m_ref)
        xre_ref[...] = jnp.zeros_like(xre_ref)
        xim_ref[...] = jnp.zeros_like(xim_ref)
        mkv = _dot(mem_ref[0].astype(BF16), wmkv_ref[...])
        mk = mkv[:, :GW]
        mv = mkv[:, GW:]
        memk_ref[0] = mk.T
        memv_ref[0] = mv.T
        mk_ref[...] = _stack_heads(mk, hm).astype(BF16)
        mv_ref[...] = _stack_heads(mv, hm).astype(BF16)

    x = x_ref[0]
    hn = _rms_norm(x, normw_ref[...]).astype(BF16)
    proj_ref[...] = _dot_nt(hn, win_ref[...])
    gates_t = _dot_nt(wg_ref[...], hn)

    nc = TB // L
    crow = [slice(c * L, (c + 1) * L) for c in range(nc)]

    def PB(blk, rows=slice(None)):
        return proj_ref[rows, blk * GW:(blk + 1) * GW]


    def xattn_stages():
        for piece in range(TB // XQ):
            rs = slice(piece * XQ, (piece + 1) * XQ)
            sc = _dot_nt(PB(AQ, rs).astype(BF16), mk_ref[...])
            yield
            ps = []
            for h in range(NH):
                seg = sc[:, h * MEM:(h + 1) * MEM]
                e = jnp.exp(seg - jnp.max(seg, axis=-1, keepdims=True))
                ps.append((e / jnp.sum(e, axis=-1, keepdims=True)).astype(BF16))
            p = jnp.concatenate(ps, axis=1)
            yield
            xa = _dot(p, mv_ref[...])
            yield
            mix_ref[rs, 3 * GW:4 * GW] = (xa * _silu(PB(AG, rs))).astype(BF16)
            yield

    def s5_stages():
        lanes = [(slice(j * 128, (j + 1) * 128), slice(NS + j * 128, NS + (j + 1) * 128))
                 for j in range(NS // 128)]
        bpc = L // LS
        nblk = TB // LS
        su = PB(SU)
        bu_ref[...] = _dot(su.astype(BF16), bt_ref[...])
        yield
        for c in range(nc):
            for lre, lim in lanes:
                br = bu_ref[crow[c], lre]
                bi = bu_ref[crow[c], lim]
                wr = jnp.tile(tab_ref[0, :, lre], (bpc, 1))
                wi = jnp.tile(tab_ref[1, :, lre], (bpc, 1))
                xcat_ref[crow[c], lre] = (wr * br - wi * bi).astype(BF16)
                xcat_ref[crow[c], lim] = (wr * bi + wi * br).astype(BF16)
            yield
        for c in range(nc):
            z = _dot(tri_ext, xcat_ref[crow[c], :])
            bu_ref[crow[c], :] = z[0:L]
            car_ref[c * bpc:(c + 1) * bpc, :] = z[L:L + bpc]
            yield
        rowb = lax.broadcasted_iota(jnp.int32, (nblk, 128), 0)
        ers, eis, prs, pis, c0s = [], [], [], [], []
        for lre, lim in lanes:
            zr = car_ref[:, lre]
            zi = car_ref[:, lim]
            pr = tab_ref[2, LS - 1:LS, lre]
            pi = tab_ref[3, LS - 1:LS, lre]
            er = pr * zr - pi * zi
            ei = pr * zi + pi * zr
            pr = tab_ref[4, LS - 1:LS, lre]
            pi = tab_ref[5, LS - 1:LS, lre]
            c0r = xre_ref[0:1, lre]
            c0i = xim_ref[0:1, lre]
            ers.append(er + jnp.where(rowb == 0, pr * c0r - pi * c0i, 0.0))
            eis.append(ei + jnp.where(rowb == 0, pr * c0i + pi * c0r, 0.0))
            prs.append(pr)
            pis.append(pi)
            c0s.append((c0r, c0i))
        yield
        for k in range(nblk.bit_length() - 1):
            d = 1 << k
            for j in range(len(lanes)):
                er, ei, pr, pi = ers[j], eis[j], prs[j], pis[j]
                sr = jnp.where(rowb >= d, pltpu.roll(er, d, 0), 0.0)
                si = jnp.where(rowb >= d, pltpu.roll(ei, d, 0), 0.0)
                ers[j], eis[j] = er + pr * sr - pi * si, ei + pr * si + pi * sr
                prs[j], pis[j] = pr * pr - pi * pi, 2.0 * (pr * pi)
            yield
        for j, (lre, lim) in enumerate(lanes):
            xre_ref[:, lre] = jnp.broadcast_to(ers[j][nblk - 1:nblk, :], (8, 128))
            xim_ref[:, lre] = jnp.broadcast_to(eis[j][nblk - 1:nblk, :], (8, 128))
            cr = jnp.where(rowb == 0, c0s[j][0], pltpu.roll(ers[j], 1, 0))
            ci_ = jnp.where(rowb == 0, c0s[j][1], pltpu.roll(eis[j], 1, 0))
            ar = tab_ref[4, 0:1, lre]
            ai = tab_ref[5, 0:1, lre]
            car_ref[:, lre] = ar * cr - ai * ci_
            car_ref[:, lim] = ar * ci_ + ai * cr
        yield
        for c in range(nc):
            for lre, lim in lanes:
                cr = jnp.concatenate(
                    [jnp.broadcast_to(car_ref[c * bpc + j:c * bpc + j + 1, lre], (LS, 128))
                     for j in range(bpc)], axis=0)
                ci_ = jnp.concatenate(
                    [jnp.broadcast_to(car_ref[c * bpc + j:c * bpc + j + 1, lim], (LS, 128))
                     for j in range(bpc)], axis=0)
                zr = bu_ref[crow[c], lre] + cr
                zi = bu_ref[crow[c], lim] + ci_
                pr = jnp.tile(tab_ref[2, :, lre], (bpc, 1))
                pi = jnp.tile(tab_ref[3, :, lre], (bpc, 1))
                xcat_ref[crow[c], lre] = (pr * zr - pi * zi).astype(BF16)
                xcat_ref[crow[c], lim] = (pr * zi + pi * zr).astype(BF16)
            yield
        sy = _dot(xcat_ref[...], ct_ref[...]) + dsk_ref[...] * su
        yield
        sy = _gelu_tanh(sy)
        gate = _dot(sy.astype(BF16), wglu_ref[...])
        yield
        mix_ref[:, 2 * GW:3 * GW] = (sy * _sigmoid(gate) * _silu(PB(SG))).astype(BF16)
        yield

    def retention_stages():
        rq = _rope(PB(RQ), cos_ref[...], s1_ref[...], s2_ref[...])
        rk = _rope(PB(RK), cos_ref[...], s1_ref[...], s2_ref[...])
        rvf = PB(RV)
        rv = rvf.astype(BF16)
        rqb = rq.astype(BF16)
        kst = [_stack_heads(rk[crow[c]], hm).astype(BF16) for c in range(nc)]
        vst = [_stack_heads(rvf[crow[c]], hm).astype(BF16) for c in range(nc)]
        yield
        inner = [_dot_nt(rqb[crow[c]], kst[c]) for c in range(nc)]
        yield
        pmat = [(inner[c] * dec_ref[...]).astype(BF16) for c in range(nc)]
        kd = [(rk[crow[c]] * kdec_ref[...]).astype(BF16) for c in range(nc)]
        yield
        rloc = [_dot(pmat[c], vst[c]) for c in range(nc)]
        upd = [_dot_tn(kd[c], rv[crow[c]]) for c in range(nc)]
        yield
        st = [s_ref[...]]
        for c in range(nc):
            st.append(st[c] * cdec_ref[...] + jnp.where(bd, upd[c], 0.0))
        s_ref[...] = st[nc]
        qd = [(rq[crow[c]] * qdec_ref[...]).astype(BF16) for c in range(nc)]
        yield
        ost = [_dot(qd[c], st[c].astype(BF16)) for c in range(nc)]
        yield
        ro = jnp.concatenate([rloc[c] + ost[c] for c in range(nc)], axis=0)
        mix_ref[:, 0:GW] = (_head_norm(ro, retgn_ref[...], avg) * _silu(PB(RG))).astype(BF16)
        yield

    def mlstm_stages():
        ig_r = gates_t[0:8] + bi_ref[...]
        lf_r = _log_sigmoid(gates_t[8:16] + bf_ref[...])
        yield
        b_r = [_dot_x3(lf_r[:, crow[c]], tri_u) for c in range(nc)]
        yield
        gd_r = [ig_r[:, crow[c]] - b_r[c] for c in range(nc)]
        lane_r = lax.broadcasted_iota(jnp.int32, (8, L), 1)
        cm_r = list(gd_r)
        for k in range(7):
            d = 1 << k
            cm_r = [jnp.maximum(v, jnp.where(lane_r >= d, pltpu.roll(v, d, 1), NEG_INF))
                    for v in cm_r]
            yield
        m_prev = [m_ref[...]]
        mt_r = []
        for c in range(nc):
            mt = jnp.maximum(b_r[c] + m_prev[c], b_r[c] + cm_r[c])
            mt_r.append(mt)
            m_prev.append(jnp.broadcast_to(mt[:, L - 1:L], (8, L)))
            yield
        m_ref[...] = m_prev[nc]
        cols = []
        for c in range(nc):
            bm = b_r[c] - mt_r[c]
            ws = jnp.exp(b_r[c] + m_prev[c] - mt_r[c])
            wl = jnp.exp(gd_r[c] + jnp.broadcast_to(bm[:, L - 1:L], (8, L)))
            emt = jnp.exp(-mt_r[c])
            cols.append(jnp.concatenate([bm, ws, wl, emt, jnp.zeros((L - 32, L), F32)], axis=0).T)
        mqf = PB(MQ)
        mq = mqf.astype(BF16)
        mkf = PB(MK)
        mvf = PB(MV)
        mv_ = mvf.astype(BF16)
        kst = [_stack_heads(mkf[crow[c]], hm).astype(BF16) for c in range(nc)]
        vst = [_stack_heads(mvf[crow[c]], hm).astype(BF16) for c in range(nc)]
        yield
        sraw = [_dot_nt(mq[crow[c]], kst[c]) for c in range(nc)]
        yield
        smat, den_i = [], []
        for c in range(nc):
            parts, dens = [], []
            for h in range(NH):
                arg = jnp.where(causal, cols[c][:, h:h + 1] + gd_r[c][h:h + 1, :], NEG_INF)
                s_h = sraw[c][:, h * L:(h + 1) * L] * jnp.exp(arg)
                dens.append(jnp.sum(s_h, axis=-1, keepdims=True))
                parts.append(s_h.astype(BF16))
            den_i.append(dens)
            smat.append(jnp.concatenate(parts, axis=1))
            yield
        ones_blk = jnp.ones((L, 128), BF16)
        def per_head(columns):
            return jnp.where(hm[0], columns[0],
                             jnp.where(hm[1], columns[1], jnp.where(hm[2], columns[2], columns[3])))

        kws = [(mkf[crow[c]] * per_head([cols[c][:, 16 + h:17 + h] for h in range(NH)])
                ).astype(BF16) for c in range(nc)]
        yield
        rloc = [_dot(smat[c], vst[c]) for c in range(nc)]
        u = [_dot_tn(kws[c], jnp.concatenate([mv_[crow[c]], ones_blk], axis=1))
             for c in range(nc)]
        yield
        nmask = (lax.broadcasted_iota(jnp.int32, (GW, 128), 0) // DH
                 == lax.broadcasted_iota(jnp.int32, (GW, 128), 1))
        cst = [c_ref[...]]
        nst = [n_ref[...]]
        for c in range(nc):
            wsl256 = jnp.zeros((1, GW), F32)
            wsl128 = jnp.zeros((1, 128), F32)
            for h in range(NH):
                wsl = cols[c][L - 1:L, 8 + h:9 + h]
                wsl256 = wsl256 + jnp.where(hm[h], wsl, 0.0)
                wsl128 = wsl128 + jnp.where(lane128 == h, wsl, 0.0)
            cst.append(cst[c] * wsl256 + jnp.where(bd, u[c][:, :GW], 0.0))
            nst.append(nst[c] * wsl128 + jnp.where(nmask, u[c][:, GW:], 0.0))
        c_ref[...] = cst[nc]
        n_ref[...] = nst[nc]
        yield
        qc = [_dot(mq[crow[c]], cst[c].astype(BF16)) for c in range(nc)]
        qn = [_dot(mq[crow[c]], nst[c].astype(BF16)) for c in range(nc)]
        yield
        mhs = []
        for c in range(nc):
            wsc = [cols[c][:, 8 + h:9 + h] for h in range(NH)]
            rdn = []
            for h in range(NH):
                den = den_i[c][h] + wsc[h] * qn[c][:, h:h + 1]
                rdn.append(1.0 / jnp.maximum(jnp.abs(den), cols[c][:, 24 + h:25 + h]))
            mhs.append((rloc[c] + per_head(wsc) * qc[c]) * per_head(rdn))
            yield
        mh = jnp.concatenate(mhs, axis=0) * _sigmoid(PB(MO))
        mix_ref[:, GW:2 * GW] = (_head_norm(mh, mlgn_ref[...], avg) * _silu(PB(MG))).astype(BF16)
        yield

    pending = [mlstm_stages(), s5_stages(), retention_stages(), xattn_stages()]
    while pending:
        pending = [stage for stage in pending if next(stage, _DONE) is not _DONE]

    y = x_ref[0] + _dot(mix_ref[...], wout_ref[...])
    if last_layer:
        y = _rms_norm(y, fnw_ref[...])
    y_ref[0] = y

    @pl.when(t == n_tblocks - 1)
    def _final():
        s_fin = s_ref[...]
        c_fin = c_ref[...].T
        n_fin = n_ref[...].T
        for h in range(NH):
            blk = slice(h * DH, (h + 1) * DH)
            rets_ref[0, h] = s_fin[blk, blk]
            mlc_ref[0, h] = c_fin[blk, blk]
        mln_ref[0] = jnp.concatenate([n_fin[h:h + 1, h * DH:(h + 1) * DH] for h in range(NH)],
                                     axis=0)
        mlm_ref[0] = m_ref[...]
        s5re_ref[0] = xre_ref[...]
        s5im_ref[0] = xim_ref[...]


def _prompt_layer(layer, x, mem, w, consts, last_layer, prev):
    bsz, seq, _ = x.shape
    depth = w["w_in"].shape[0]
    nt = seq // TB
    full = lambda shape: pl.BlockSpec(shape, lambda b, t: (0,) * len(shape),
                                      pipeline_mode=pl.Buffered(1))
    lyr = lambda shape: pl.BlockSpec((None,) + shape, lambda b, t: (layer,) + (0,) * len(shape),
                                     pipeline_mode=pl.Buffered(1))
    tok = lambda width: pl.BlockSpec((TB, width), lambda b, t: (t, 0))
    per_b = lambda r, c: pl.BlockSpec((1, r, c), lambda b, t: (b, 0, 0))
    in_specs = [
        pl.BlockSpec((1, TB, D), lambda b, t: (b, t, 0)),
        per_b(MEM, D),
        lyr((1, D)), lyr((DP, D)), lyr((NGATE, D)), lyr((D, D)), lyr((D, 2 * GW)),
        tok(128), tok(128), tok(128),
        full((L, NH * L)), full((L, GW)), full((L, GW)), full((1, GW)),
        lyr((1, GW)), lyr((1, GW)), lyr((8, TB)), lyr((8, TB)),
        lyr((6, LS, NS)), lyr((GW, 2 * NS)), lyr((2 * NS, GW)),
        lyr((1, GW)), lyr((GW, GW)), full((1, D)),
    ] + [pl.BlockSpec(memory_space=pl.ANY)] * len(prev)
    n_in = len(in_specs) - len(prev)
    state_shapes = [(NH, DH, DH),
                    (NH, DH, DH),
                    (NH, DH),
                    (8, 128),
                    (8, NS),
                    (8, NS),
                    (GW, MEM),
                    (GW, MEM)]
    assert len(state_shapes) == N_STATE_OUT and len(prev) in (0, N_STATE_OUT)
    out_shape = ((jax.ShapeDtypeStruct((bsz, seq, D), F32),)
                 + tuple(jax.ShapeDtypeStruct((depth, bsz) + s, F32) for s in state_shapes))
    state_spec = lambda s: pl.BlockSpec((None, 1) + s,
                                        lambda b, t: (layer, b) + (0,) * len(s))
    out_specs = ((pl.BlockSpec((1, TB, D), lambda b, t: (b, t, 0)),)
                 + tuple(state_spec(s) for s in state_shapes))
    scratch = [
        pltpu.VMEM((TB, DP), F32),
        pltpu.VMEM((TB, D), BF16),
        pltpu.VMEM((GW, GW), F32),
        pltpu.VMEM((GW, GW), F32),
        pltpu.VMEM((GW, 128), F32),
        pltpu.VMEM((8, 128), F32),
        pltpu.VMEM((8, NS), F32),
        pltpu.VMEM((8, NS), F32),
        pltpu.VMEM((NH * MEM, GW), BF16),
        pltpu.VMEM((NH * MEM, GW), BF16),
        pltpu.VMEM((TB, 2 * NS), F32),
        pltpu.VMEM((TB, 2 * NS), BF16),
        pltpu.VMEM((TB // LS, 2 * NS), F32),
    ]
    kern = functools.partial(_prompt_kernel, last_layer=last_layer, n_tblocks=nt, n_prev=len(prev))
    return pl.pallas_call(
        kern, grid=(bsz, nt), in_specs=in_specs, out_specs=out_specs, out_shape=out_shape,
        scratch_shapes=scratch, name="prompt_layer",
        input_output_aliases={n_in + i: 1 + i for i in range(len(prev))},
        compiler_params=pltpu.CompilerParams(
            dimension_semantics=("arbitrary", "arbitrary"), vmem_limit_bytes=VMEM_LIMIT),
    )(x, mem, w["norm_w"], w["w_in"], w["w_gate"], w["w_out"], w["w_mem_kv"],
      consts["cos_p"], consts["s1_p"], consts["s2_p"],
      consts["dec"], consts["qdec"], consts["kdec"], consts["cdec"],
      w["ret_gn"], w["ml_gn"], w["b_i8"], w["b_f8"],
      w["tab"], w["bt"], w["ct"], w["s5_d"], w["w_glu"], w["final_norm_w"], *prev)


SROWS = 16


def _sample_kernel_t(x_ref, normw_ref, win_ref, wg_ref, wout_ref,
                     cos_ref, s1_ref, s2_ref, gam_ref, gam8_ref,
                     retgn_ref, mlgn_ref, bi_ref, bf_ref,
                     abt_ref, bt_ref, ct_ref, dsk_ref, wglu_ref, fnw_ref,
                     m0_ref, n0_ref, x0re_ref, x0im_ref,
                     rets_ref, mlc_ref, ck_ref, cv_ref,
                     y_ref, retn_ref, mlcn_ref, mlnn_ref, mlmn_ref, s5re_ref, s5im_ref,
                     hs_ref, proj_ref, qgt_ref, rkt_ref, rvt_ref, mqt_ref, mkt_ref, mvt_ref,
                     wi_ref, ws_ref, ot_ref, cqt_ref, qa8_ref, xa_ref,
                     *, n_layers, n_blocks):
    layer = pl.program_id(0)
    g = pl.program_id(1)
    nsamp = x_ref.shape[0]
    ones_bd = _ones_matrix()
    avg = _avg_matrix()
    mask8 = (lax.broadcasted_iota(jnp.int32, (8, GW), 0)
             == lax.broadcasted_iota(jnp.int32, (8, GW), 1) // DH)

    def P(blk):
        return proj_ref[:, blk * GW:(blk + 1) * GW]

    @pl.when((layer == 0) & (g == 0))
    def _load_x():
        hs_ref[...] = x_ref[...]

    @pl.when(g == 0)
    def _pre():
        hn = _rms_norm(hs_ref[...], normw_ref[...]).astype(BF16)
        proj_ref[...] = _dot_nt(hn, win_ref[...])
        cosr, s1, s2 = cos_ref[...], s1_ref[...], s2_ref[...]
        qgt_ref[...] = (_rope(P(RQ), cosr, s1, s2) * gam_ref[...]).T
        rkt_ref[...] = _rope(P(RK), cosr, s1, s2).T
        rvt_ref[...] = P(RV).T
        mqt_ref[...] = P(MQ).T
        mkt_ref[...] = P(MK).T
        mvt_ref[...] = P(MV).T
        gt = _dot_nt(wg_ref[...], hn)
        ig = gt[0:8] + bi_ref[...]
        lf = _log_sigmoid(gt[8:16] + bf_ref[...])
        a = lf + m0_ref[...]
        mt = jnp.maximum(a, ig)
        wi_ref[...] = jnp.exp(ig - mt)
        ws_ref[...] = jnp.exp(a - mt)
        mlmn_ref[...] = mt
        ot_ref[...] = jnp.zeros_like(ot_ref)
        qa8_ref[...] = jnp.where(mask8[None], P(AQ)[:, None, :], 0.0).reshape(8 * nsamp, GW)
        but = _dot_tn(bt_ref[...], P(SU).T.astype(BF16))
        are, aim = abt_ref[0:NS, :], abt_ref[NS:2 * NS, :]
        x0r, x0i = x0re_ref[...], x0im_ref[...]
        s5re_ref[...] = are * x0r - aim * x0i + but[0:NS]
        s5im_ref[...] = are * x0i + aim * x0r + but[NS:2 * NS]

    head = g // (GW // SROWS // NH)
    hrow = pl.ds(pl.multiple_of(head * DH, DH), DH)
    gam_row = gam8_ref[pl.ds(head, 1), :]
    ws_row = ws_ref[pl.ds(head, 1), :]
    wi_row = wi_ref[pl.ds(head, 1), :]
    v_slab = rvt_ref[hrow, :]
    q_slab = mqt_ref[hrow, :]
    k_slab = mkt_ref[hrow, :]
    o_acc = jnp.zeros((DH, nsamp), F32)
    for i in range(SROWS):
        r = pl.ds(g * SROWS + i, 1)
        s_t = rets_ref[i]
        o_acc = o_acc + qgt_ref[r, :] * s_t
        retn_ref[i] = gam_row * s_t + rkt_ref[r, :] * v_slab
        c_t = mlc_ref[i]
        cqt_ref[r, :] = jnp.sum(c_t * q_slab, axis=0, keepdims=True)
        mlcn_ref[i] = ws_row * c_t + (wi_row * mvt_ref[r, :]) * k_slab
    ot_ref[hrow, :] = ot_ref[hrow, :] + o_acc

    tiles = [pl.ds(pl.multiple_of((g * SB + i) * 8, 8), 8) for i in range(SB)]
    q8 = [qa8_ref[tiles[i], :].astype(BF16) for i in range(SB)]
    sc = [_dot(q8[i], ck_ref[i].astype(BF16)) for i in range(SB)]
    ps = []
    for i in range(SB):
        e = jnp.exp(sc[i] - jnp.max(sc[i], axis=-1, keepdims=True))
        ps.append((e / jnp.sum(e, axis=-1, keepdims=True)).astype(BF16))
    ta = [_dot_nt(ps[i], cv_ref[i].astype(BF16)) for i in range(SB)]
    xa_ref[pl.ds(pl.multiple_of(g * SB, SB), SB), :] = jnp.concatenate(
        [jnp.sum(jnp.where(mask8, ta[i], 0.0), axis=0, keepdims=True) for i in range(SB)], axis=0)

    @pl.when(g == n_blocks - 1)
    def _post():
        cosr, s1, s2 = cos_ref[...], s1_ref[...], s2_ref[...]
        rq = _rope(P(RQ), cosr, s1, s2)
        rk = _rope(P(RK), cosr, s1, s2)
        ro = _dot_x2(rq * rk, ones_bd) * P(RV) + ot_ref[...].T
        ret_out = _head_norm(ro, retgn_ref[...], avg) * _silu(P(RG))
        mht = []
        for h in range(NH):
            rows = slice(h * DH, (h + 1) * DH)
            wi = wi_ref[h:h + 1, :]
            ws = ws_ref[h:h + 1, :]
            emt = jnp.exp(-mlmn_ref[h:h + 1, :])
            qt, kt, n0 = mqt_ref[rows, :], mkt_ref[rows, :], n0_ref[rows, :]
            s = jnp.sum(qt * kt, axis=0, keepdims=True) * wi
            den = s + ws * jnp.sum(n0 * qt, axis=0, keepdims=True)
            mht.append((s * mvt_ref[rows, :] + ws * cqt_ref[rows, :])
                       / jnp.maximum(jnp.abs(den), emt))
            mlnn_ref[rows, :] = ws * n0 + wi * kt
        mh = jnp.concatenate(mht, axis=0).T * _sigmoid(P(MO))
        ml_out = _head_norm(mh, mlgn_ref[...], avg) * _silu(P(MG))
        su = P(SU)
        xcat = jnp.concatenate([s5re_ref[...], s5im_ref[...]], axis=0).astype(BF16)
        sy = _dot_tn(xcat, ct_ref[...]) + dsk_ref[...] * su
        sy = _gelu_tanh(sy)
        sy = sy * _sigmoid(_dot(sy.astype(BF16), wglu_ref[...]))
        s5_out = sy * _silu(P(SG))
        xa_out = xa_ref[...] * _silu(P(AG))
        mix = jnp.concatenate([ret_out, ml_out, s5_out, xa_out], axis=1).astype(BF16)
        y = hs_ref[...] + _dot(mix, wout_ref[...])
        hs_ref[...] = y

        @pl.when(layer == n_layers - 1)
        def _emit():
            y_ref[...] = _rms_norm(y, fnw_ref[...])


def _sample_layers_t(x, st, w, consts):
    nsamp = x.shape[0]
    depth = w["w_in"].shape[0]
    nb = GW // SROWS
    assert nsamp == nb * SB and nsamp == 128
    once = lambda shape: pl.BlockSpec(shape, lambda l, g: (0,) * len(shape),
                                      pipeline_mode=pl.Buffered(1))
    lyr = lambda shape: pl.BlockSpec((None,) + shape, lambda l, g: (l,) + (0,) * len(shape))
    lyr_out = lyr
    srows = pl.BlockSpec((None, SROWS, DH, nsamp), lambda l, g: (l, g, 0, 0))
    cache = pl.BlockSpec((None, SB, GW, MEM), lambda l, g: (l, g, 0, 0))
    in_specs = [
        once((nsamp, D)), lyr((1, D)), lyr((DP, D)), lyr((NGATE, D)), lyr((D, D)),
        once((1, 128)), once((1, 128)), once((1, 128)), once((1, GW)), once((8, nsamp)),
        lyr((1, GW)), lyr((1, GW)), lyr((8, nsamp)), lyr((8, nsamp)),
        lyr((2 * NS, nsamp)), lyr((GW, 2 * NS)), lyr((2 * NS, GW)),
        lyr((1, GW)), lyr((GW, GW)), once((1, D)),
        lyr((8, nsamp)), lyr((GW, nsamp)), lyr((NS, nsamp)), lyr((NS, nsamp)),
        srows, srows, cache, cache,
    ]
    out_shape = (
        jax.ShapeDtypeStruct((nsamp, D), F32),
        jax.ShapeDtypeStruct((depth, GW, DH, nsamp), F32),
        jax.ShapeDtypeStruct((depth, GW, DH, nsamp), F32),
        jax.ShapeDtypeStruct((depth, GW, nsamp), F32),
        jax.ShapeDtypeStruct((depth, 8, nsamp), F32),
        jax.ShapeDtypeStruct((depth, NS, nsamp), F32),
        jax.ShapeDtypeStruct((depth, NS, nsamp), F32),
    )
    out_specs = (
        pl.BlockSpec((nsamp, D), lambda l, g: (0, 0)), srows, srows,
        lyr_out((GW, nsamp)), lyr_out((8, nsamp)), lyr_out((NS, nsamp)), lyr_out((NS, nsamp)),
    )
    scratch = [
        pltpu.VMEM((nsamp, D), F32),
        pltpu.VMEM((nsamp, DP), F32),
        pltpu.VMEM((GW, nsamp), F32), pltpu.VMEM((GW, nsamp), F32), pltpu.VMEM((GW, nsamp), F32),
        pltpu.VMEM((GW, nsamp), F32), pltpu.VMEM((GW, nsamp), F32), pltpu.VMEM((GW, nsamp), F32),
        pltpu.VMEM((8, nsamp), F32), pltpu.VMEM((8, nsamp), F32),
        pltpu.VMEM((GW, nsamp), F32), pltpu.VMEM((GW, nsamp), F32),
        pltpu.VMEM((8 * nsamp, GW), F32), pltpu.VMEM((nsamp, GW), F32),
    ]
    kern = functools.partial(_sample_kernel_t, n_layers=depth, n_blocks=nb)
    return pl.pallas_call(
        kern, grid=(depth, nb), in_specs=in_specs, out_specs=out_specs, out_shape=out_shape,
        scratch_shapes=scratch, name="sample_layers",
        compiler_params=pltpu.CompilerParams(
            dimension_semantics=("arbitrary", "arbitrary"), vmem_limit_bytes=VMEM_LIMIT),
    )(x, w["norm_w"], w["w_in"], w["w_gate"], w["w_out"],
      consts["cos_s"], consts["s1_s"], consts["s2_s"], consts["gam"], consts["gam8"],
      w["ret_gn"], w["ml_gn"], w["b_i8"], w["b_f8"],
      w["abt"], w["bt"], w["ct"], w["s5_d"], w["w_glu"], w["final_norm_w"],
      st["m"], st["n"], st["s5_re"], st["s5_im"], st["ret"], st["c"], st["mem_k"], st["mem_v"])


def _rope_tables(pos):
    half = DH // 2
    inv = ROPE_BASE ** (-np.arange(half, dtype=np.float64) / half)
    ang = np.asarray(pos, np.float64)[:, None] * inv[None, :]
    cos, sin = np.cos(ang), np.sin(ang)
    zero = np.zeros_like(sin)
    c = np.tile(np.concatenate([cos, cos], axis=-1), (1, 2))
    s1 = np.tile(np.concatenate([zero, sin], axis=-1), (1, 2))
    s2 = np.tile(np.concatenate([-sin, zero], axis=-1), (1, 2))
    return tuple(jnp.asarray(t, F32) for t in (c, s1, s2))


def _constants(seq):
    lg = np.log1p(-np.power(2.0, -5.0 - np.arange(NH, dtype=np.float64)))[:, None]
    idx = np.arange(L, dtype=np.float64)
    diff = idx[:, None] - idx[None, :]
    decay = np.where(diff >= 0, np.exp(lg[:, :, None] * np.maximum(diff, 0.0)), 0.0)
    rep = lambda t: np.repeat(t, DH, axis=0).T
    consts = {
        "dec": np.transpose(decay, (1, 0, 2)).reshape(L, NH * L),
        "qdec": rep(np.exp(lg * (idx + 1.0))),
        "kdec": rep(np.exp(lg * (L - 1.0 - idx))),
        "cdec": rep(np.exp(lg * L)),
        "gam": rep(np.exp(lg * 1.0)),
        "gam8": np.pad(np.broadcast_to(np.exp(lg), (NH, 128)), ((0, 8 - NH), (0, 0))),
    }
    consts = {k: jnp.asarray(v, F32) for k, v in consts.items()}
    consts["cos_p"], consts["s1_p"], consts["s2_p"] = _rope_tables(np.arange(seq))
    consts["cos_s"], consts["s1_s"], consts["s2_s"] = _rope_tables(PAST_LEN + np.arange(1))
    return consts


def _pack_w_in(w_in):
    wt = jnp.swapaxes(w_in, 1, 2)
    sizes = (GW,) * 9 + (NH, NH) + (GW,) * 4
    offs = np.concatenate([[0], np.cumsum(sizes)])
    seg = [wt[:, int(offs[i]):int(offs[i + 1]), :] for i in range(len(sizes))]
    scale = DH ** -0.5
    pad8 = lambda t: jnp.pad(t, ((0, 0), (0, 8 - t.shape[1]), (0, 0)))
    blocks = [seg[11],
              seg[0], seg[1] * scale, seg[2], seg[3],
              seg[4], seg[5] * scale, seg[6], seg[7], seg[8],
              seg[12], seg[13] * scale, seg[14]]
    gates = jnp.concatenate([pad8(seg[9]), pad8(seg[10])], axis=1)
    return jnp.concatenate(blocks, axis=1).astype(BF16), gates.astype(BF16)


def kernel(x_prompt, x_sample, mem_prompt, state_ret, state_mlstm_c, state_mlstm_n, state_mlstm_m,
           state_s5_re, state_s5_im, cache_mem_k, cache_mem_v,
           norm_w, w_in, ret_gn, ml_b_i, ml_b_f, ml_gn,
           s5_a_re, s5_a_im, s5_log_dt, s5_b_re, s5_b_im, s5_c_re, s5_c_im, s5_d, s5_w_glu,
           w_mem_k, w_mem_v, w_out, final_norm_w):
    depth = norm_w.shape[0]
    bp, seq, _ = x_prompt.shape
    bs = x_sample.shape[0]
    consts = _constants(seq)
    abt, bt, ct, tab = _s5_prepare(s5_a_re, s5_a_im, s5_log_dt, s5_b_re, s5_b_im, s5_c_re, s5_c_im)
    rows8 = lambda t: jnp.pad(jnp.broadcast_to(t[:, :, None], (depth, NH, TB)),
                              ((0, 0), (0, 8 - NH), (0, 0)))
    w_main, w_gate = _pack_w_in(w_in)
    w = {
        "norm_w": norm_w[:, None], "w_in": w_main, "w_gate": w_gate, "w_out": w_out.astype(BF16),
        "w_mem_kv": jnp.concatenate([w_mem_k, w_mem_v], axis=-1).astype(BF16),
        "ret_gn": ret_gn[:, None], "ml_gn": ml_gn[:, None],
        "b_i8": rows8(ml_b_i), "b_f8": rows8(ml_b_f),
        "abt": abt, "tab": tab, "bt": bt, "ct": ct,
        "s5_d": s5_d[:, None], "w_glu": s5_w_glu.astype(BF16),
        "final_norm_w": final_norm_w[None],
    }

    hp = x_prompt
    states = ()
    for l in range(depth):
        hp, *states = _prompt_layer(l, hp, mem_prompt, w, consts, l == depth - 1, tuple(states))
    ret_p, mlc_p, mln_p, ms, xr, xi, mk, mv = states
    mlm_p = ms[:, :, :NH, 0]
    s5re_p = xr[:, :, 0].reshape(depth, bp, S5G, S5P)
    s5im_p = xi[:, :, 0].reshape(depth, bp, S5G, S5P)
    memk_p = jnp.transpose(mk.reshape(depth, bp, NH, DH, MEM), (0, 1, 4, 2, 3))
    memv_p = jnp.transpose(mv.reshape(depth, bp, NH, DH, MEM), (0, 1, 4, 2, 3))

    st = {
        "m": jnp.pad(jnp.swapaxes(state_mlstm_m, 1, 2), ((0, 0), (0, 8 - NH), (0, 0))),
        "n": jnp.transpose(state_mlstm_n, (0, 2, 3, 1)).reshape(depth, GW, bs),
        "s5_re": jnp.transpose(state_s5_re, (0, 2, 3, 1)).reshape(depth, NS, bs),
        "s5_im": jnp.transpose(state_s5_im, (0, 2, 3, 1)).reshape(depth, NS, bs),
        "ret": jnp.transpose(state_ret, (0, 2, 3, 4, 1)).reshape(depth, GW, DH, bs),
        "c": jnp.transpose(state_mlstm_c, (0, 2, 3, 4, 1)).reshape(depth, GW, DH, bs),
        "mem_k": jnp.transpose(cache_mem_k, (0, 1, 3, 4, 2)).reshape(depth, bs, GW, MEM),
        "mem_v": jnp.transpose(cache_mem_v, (0, 1, 3, 4, 2)).reshape(depth, bs, GW, MEM),
    }
    hs, rn, cn, nn, mn, sr, si = _sample_layers_t(x_sample.reshape(bs, D), st, w, consts)
    back5 = lambda t: jnp.transpose(t.reshape(depth, NH, DH, DH, bs), (0, 4, 1, 2, 3))
    back4 = lambda t, a, b: jnp.transpose(t.reshape(depth, a, b, bs), (0, 3, 1, 2))
    return (hp, hs.reshape(bs, 1, D),
            ret_p, back5(rn), mlc_p, back5(cn),
            mln_p, back4(nn, NH, DH), mlm_p, jnp.swapaxes(mn[:, :NH], 1, 2),
            s5re_p, back4(sr, S5G, S5P), s5im_p, back4(si, S5G, S5P),
            memk_p, memv_p)
```

```python
import functools
import math

import numpy as np
import jax
import jax.numpy as jnp
from jax import lax
from jax.experimental import pallas as pl
from jax.experimental.pallas import tpu as pltpu

F32 = jnp.float32
BF16 = jnp.bfloat16

D = 1024
GW = 256
NH = 4
DH = 64
L = 128
LS = 8
S5G = 16
S5P = 64
S5C = 16
NS = S5G * S5P
MEM = 256
EPS = 1e-6
NEG_INF = -1e30
ROPE_BASE = 10000.0
PAST_LEN = 16384

TB = 512
SB = 8
XQ = 512
NBLK = 13
DP = NBLK * GW
(SU, RQ, RK, RV, RG, MQ, MK, MV, MO, MG, SG, AQ, AG) = range(NBLK)
NGATE = 16

VMEM_LIMIT = 56 * 1024 * 1024


_DONE = object()


def _dot(a, b):
    return jnp.dot(a, b, preferred_element_type=F32)


def _dot_nt(a, b):
    return lax.dot_general(a, b, (((1,), (1,)), ((), ())), preferred_element_type=F32)


def _dot_tn(a, b):
    return lax.dot_general(a, b, (((0,), (0,)), ((), ())), preferred_element_type=F32)


def _split2(x):
    hi = x.astype(BF16)
    lo = (x - hi.astype(F32)).astype(BF16)
    return hi, lo


def _dot_x2(x, w):
    hi, lo = _split2(x)
    return _dot(hi, w) + _dot(lo, w)


def _dot_x3(x, w):
    hi = x.astype(BF16)
    r = x - hi.astype(F32)
    mid = r.astype(BF16)
    lo = (r - mid.astype(F32)).astype(BF16)
    return _dot(hi, w) + _dot(mid, w) + _dot(lo, w)


def _sigmoid(x):
    return 0.5 * (1.0 + jnp.tanh(0.5 * x))


def _silu(x):
    h = 0.5 * x
    return h + h * jnp.tanh(h)


def _log_sigmoid(x):
    return jnp.minimum(x, 0.0) - jnp.log1p(jnp.exp(-jnp.abs(x)))


def _gelu_tanh(x):
    c = math.sqrt(2.0 / math.pi)
    h = 0.5 * x
    return h + h * jnp.tanh(x * (c + (0.044715 * c) * (x * x)))


def _lane_head(n):
    return lax.broadcasted_iota(jnp.int32, (1, n), 1) // DH


def _head_masks():
    lh = _lane_head(GW)
    return [lh == h for h in range(NH)]


def _block_diag_mask():
    r = lax.broadcasted_iota(jnp.int32, (GW, GW), 0) // DH
    c = lax.broadcasted_iota(jnp.int32, (GW, GW), 1) // DH
    return r == c


def _avg_matrix():
    return jnp.where(_block_diag_mask(), 1.0 / DH, 0.0).astype(BF16)


def _ones_matrix():
    return jnp.where(_block_diag_mask(), 1.0, 0.0).astype(BF16)


def _rope(x, cos, s1, s2):
    outs = []
    for j in range(2):
        xs = x[:, j * 128:(j + 1) * 128]
        outs.append(xs * cos + pltpu.roll(xs, 32, 1) * s1 + pltpu.roll(xs, 96, 1) * s2)
    return jnp.concatenate(outs, axis=1)


def _head_norm(x, gain, avg):
    mu = _dot_x2(x, avg)
    d = x - mu
    var = _dot((d * d).astype(BF16), avg)
    return d * lax.rsqrt(var + EPS) * gain


def _rms_norm(x, w):
    ms = jnp.mean(x * x, axis=-1, keepdims=True)
    return x * lax.rsqrt(ms + EPS) * w


def _stack_heads(x, hm):
    zero = jnp.zeros((x.shape[0], 128), x.dtype)
    blocks = []
    for h in range(NH):
        half = slice((h * DH) // 128 * 128, (h * DH) // 128 * 128 + 128)
        kept = jnp.where(hm[h][:, half], x[:, half], 0.0)
        blocks.append(jnp.concatenate([kept, zero] if half.start == 0 else [zero, kept], axis=1))
    return jnp.concatenate(blocks, axis=0)


def _s5_prep_kernel(are_ref, aim_ref, ldt_ref, bre_ref, bim_ref, cre_ref, cim_ref,
                    abt_ref, bt_ref, ct_ref, tab_ref):
    a_re = are_ref[0]
    a_im = aim_ref[0]
    dt = jnp.exp(ldt_ref[0])
    lam_re = a_re * dt
    lam_im = a_im * dt
    mag = jnp.exp(lam_re)
    ab_re = mag * jnp.cos(lam_im)
    ab_im = mag * jnp.sin(lam_im)
    den = a_re * a_re + a_im * a_im
    nr = ab_re - 1.0
    ni = ab_im
    f_re = (nr * a_re + ni * a_im) / den
    f_im = (ni * a_re - nr * a_im) / den
    abt_ref[0, 0:NS, :] = jnp.broadcast_to(ab_re, (128, NS)).T
    abt_ref[0, NS:2 * NS, :] = jnp.broadcast_to(ab_im, (128, NS)).T
    b_re = bre_ref[0]
    b_im = bim_ref[0]
    bb_re = (f_re * b_re - f_im * b_im).astype(BF16)
    bb_im = (f_re * b_im + f_im * b_re).astype(BF16)
    rep_r = (lax.broadcasted_iota(jnp.int32, (GW, S5C), 0) % S5C
             == lax.broadcasted_iota(jnp.int32, (GW, S5C), 1)).astype(BF16)
    in_blk = (lax.broadcasted_iota(jnp.int32, (GW, NS), 0) // S5C
              == lax.broadcasted_iota(jnp.int32, (GW, NS), 1) // S5P)
    bt_ref[0, :, :NS] = jnp.where(in_blk, _dot(rep_r, bb_re), 0.0).astype(BF16)
    bt_ref[0, :, NS:] = jnp.where(in_blk, _dot(rep_r, bb_im), 0.0).astype(BF16)
    rep_c = (lax.broadcasted_iota(jnp.int32, (S5C, GW), 0)
             == lax.broadcasted_iota(jnp.int32, (S5C, GW), 1) % S5C).astype(BF16)
    out_blk = (lax.broadcasted_iota(jnp.int32, (NS, GW), 0) // S5P
               == lax.broadcasted_iota(jnp.int32, (NS, GW), 1) // S5C)
    ct_ref[0, :NS, :] = jnp.where(out_blk, _dot(cre_ref[0].astype(BF16), rep_c), 0.0).astype(BF16)
    ct_ref[0, NS:, :] = jnp.where(out_blk, -_dot(cim_ref[0].astype(BF16), rep_c), 0.0).astype(BF16)
    i = lax.broadcasted_iota(jnp.int32, (LS, NS), 0).astype(F32)
    for slot, k in ((0, -i), (2, i), (4, i + 1.0)):
        pmag = jnp.exp(k * lam_re)
        tab_ref[0, slot] = pmag * jnp.cos(k * lam_im)
        tab_ref[0, slot + 1] = pmag * jnp.sin(k * lam_im)


def _s5_prepare(a_re, a_im, log_dt, b_re, b_im, c_re, c_im):
    depth = a_re.shape[0]
    are = a_re.reshape(depth, 1, NS)
    aim = a_im.reshape(depth, 1, NS)
    ldt = jnp.repeat(log_dt, S5P, axis=-1).reshape(depth, 1, NS)
    bre = jnp.transpose(b_re, (0, 3, 1, 2)).reshape(depth, S5C, NS)
    bim = jnp.transpose(b_im, (0, 3, 1, 2)).reshape(depth, S5C, NS)
    cre = jnp.transpose(c_re, (0, 1, 3, 2)).reshape(depth, NS, S5C)
    cim = jnp.transpose(c_im, (0, 1, 3, 2)).reshape(depth, NS, S5C)
    per_layer = lambda *shape: pl.BlockSpec((1,) + shape, lambda l: (l,) + (0,) * len(shape))
    out_shape = (jax.ShapeDtypeStruct((depth, 2 * NS, 128), F32),
                 jax.ShapeDtypeStruct((depth, GW, 2 * NS), BF16),
                 jax.ShapeDtypeStruct((depth, 2 * NS, GW), BF16),
                 jax.ShapeDtypeStruct((depth, 6, LS, NS), F32))
    return pl.pallas_call(
        _s5_prep_kernel, grid=(depth,),
        in_specs=[per_layer(1, NS), per_layer(1, NS), per_layer(1, NS),
                  per_layer(S5C, NS), per_layer(S5C, NS), per_layer(NS, S5C), per_layer(NS, S5C)],
        out_specs=(per_layer(2 * NS, 128), per_layer(GW, 2 * NS), per_layer(2 * NS, GW),
                   per_layer(6, LS, NS)),
        out_shape=out_shape, name="s5_prepare",
        compiler_params=pltpu.CompilerParams(dimension_semantics=("arbitrary",)),
    )(are, aim, ldt, bre, bim, cre, cim)


N_STATE_OUT = 8


def _prompt_kernel(x_ref, mem_ref, normw_ref, win_ref, wg_ref, wout_ref, wmkv_ref,
                   cos_ref, s1_ref, s2_ref, dec_ref, qdec_ref, kdec_ref, cdec_ref,
                   retgn_ref, mlgn_ref, bi_ref, bf_ref,
                   tab_ref, bt_ref, ct_ref, dsk_ref, wglu_ref, fnw_ref,
                   *rest, last_layer, n_tblocks, n_prev):
    (y_ref, rets_ref, mlc_ref, mln_ref, mlm_ref, s5re_ref, s5im_ref, memk_ref, memv_ref,
     proj_ref, mix_ref, s_ref, c_ref, n_ref, m_ref, xre_ref, xim_ref,
     mk_ref, mv_ref, bu_ref, xcat_ref, car_ref) = rest[n_prev:]
    t = pl.program_id(1)
    hm = _head_masks()
    bd = _block_diag_mask()
    avg = _avg_matrix()
    lane128 = lax.broadcasted_iota(jnp.int32, (1, 128), 1)
    row_i = lax.broadcasted_iota(jnp.int32, (L, 128), 0)
    col_i = lax.broadcasted_iota(jnp.int32, (L, 128), 1)
    causal = row_i >= col_i
    tri_u = jnp.where(row_i <= col_i, 1.0, 0.0).astype(BF16)
    tri_sub = jnp.where(causal & (row_i // LS == col_i // LS), 1.0, 0.0)
    blk_sum = jnp.where(lax.broadcasted_iota(jnp.int32, (L // LS, 128), 0)
                        == lax.broadcasted_iota(jnp.int32, (L // LS, 128), 1) // LS, 1.0, 0.0)
    tri_ext = jnp.concatenate([tri_sub, blk_sum], axis=0).astype(BF16)

    @pl.when(t == 0)
    def _init():
        s_ref[...] = jnp.zeros_like(s_ref)
        c_ref[...] = jnp.zeros_like(c_ref)
        n_ref[...] = jnp.zeros_like(n_ref)
        m_ref[...] = jnp.zeros_like(m_ref)
        xre_ref[...] = jnp.zeros_like(xre_ref)
        xim_ref[...] = jnp.zeros_like(xim_ref)
        mkv = _dot(mem_ref[0].astype(BF16), wmkv_ref[...])
        mk = mkv[:, :GW]
        mv = mkv[:, GW:]
        memk_ref[0] = mk.T
        memv_ref[0] = mv.T
        mk_ref[...] = _stack_heads(mk.astype(BF16), hm)
        mv_ref[...] = _stack_heads(mv.astype(BF16), hm)

    x = x_ref[0]
    hn = _rms_norm(x, normw_ref[...]).astype(BF16)
    proj_ref[...] = _dot_nt(hn, win_ref[...])
    gates_t = _dot_nt(wg_ref[...], hn)

    nc = TB // L
    crow = [slice(c * L, (c + 1) * L) for c in range(nc)]

    def PB(blk, rows=slice(None)):
        return proj_ref[rows, blk * GW:(blk + 1) * GW]


    def xattn_stages():
        for piece in range(TB // XQ):
            rs = slice(piece * XQ, (piece + 1) * XQ)
            sc = _dot_nt(PB(AQ, rs).astype(BF16), mk_ref[...])
            yield
            ps = []
            for h in range(NH):
                seg = sc[:, h * MEM:(h + 1) * MEM]
                e = jnp.exp(seg - jnp.max(seg, axis=-1, keepdims=True))
                ps.append((e / jnp.sum(e, axis=-1, keepdims=True)).astype(BF16))
            p = jnp.concatenate(ps, axis=1)
            yield
            xa = _dot(p, mv_ref[...])
            yield
            mix_ref[rs, 3 * GW:4 * GW] = (xa * _silu(PB(AG, rs))).astype(BF16)
            yield

    def s5_stages():
        lanes = [(slice(j * 128, (j + 1) * 128), slice(NS + j * 128, NS + (j + 1) * 128))
                 for j in range(NS // 128)]
        bpc = L // LS
        nblk = TB // LS
        su = PB(SU)
        bu_ref[...] = _dot(su.astype(BF16), bt_ref[...])
        yield
        for c in range(nc):
            for lre, lim in lanes:
                br = bu_ref[crow[c], lre]
                bi = bu_ref[crow[c], lim]
                wr = jnp.tile(tab_ref[0, :, lre], (bpc, 1))
                wi = jnp.tile(tab_ref[1, :, lre], (bpc, 1))
                xcat_ref[crow[c], lre] = (wr * br - wi * bi).astype(BF16)
                xcat_ref[crow[c], lim] = (wr * bi + wi * br).astype(BF16)
            yield
        for c in range(nc):
            z = _dot(tri_ext, xcat_ref[crow[c], :])
            bu_ref[crow[c], :] = z[0:L]
            car_ref[c * bpc:(c + 1) * bpc, :] = z[L:L + bpc]
            yield
        rowb = lax.broadcasted_iota(jnp.int32, (nblk, 128), 0)
        ers, eis, prs, pis, c0s = [], [], [], [], []
        for lre, lim in lanes:
            zr = car_ref[:, lre]
            zi = car_ref[:, lim]
            pr = tab_ref[2, LS - 1:LS, lre]
            pi = tab_ref[3, LS - 1:LS, lre]
            er = pr * zr - pi * zi
            ei = pr * zi + pi * zr
            pr = tab_ref[4, LS - 1:LS, lre]
            pi = tab_ref[5, LS - 1:LS, lre]
            c0r = xre_ref[0:1, lre]
            c0i = xim_ref[0:1, lre]
            ers.append(er + jnp.where(rowb == 0, pr * c0r - pi * c0i, 0.0))
            eis.append(ei + jnp.where(rowb == 0, pr * c0i + pi * c0r, 0.0))
            prs.append(pr)
            pis.append(pi)
            c0s.append((c0r, c0i))
        yield
        for k in range(nblk.bit_length() - 1):
            d = 1 << k
            for j in range(len(lanes)):
                er, ei, pr, pi = ers[j], eis[j], prs[j], pis[j]
                sr = jnp.where(rowb >= d, pltpu.roll(er, d, 0), 0.0)
                si = jnp.where(rowb >= d, pltpu.roll(ei, d, 0), 0.0)
                ers[j], eis[j] = er + pr * sr - pi * si, ei + pr * si + pi * sr
                prs[j], pis[j] = pr * pr - pi * pi, 2.0 * (pr * pi)
            yield
        for j, (lre, lim) in enumerate(lanes):
            xre_ref[:, lre] = jnp.broadcast_to(ers[j][nblk - 1:nblk, :], (8, 128))
            xim_ref[:, lre] = jnp.broadcast_to(eis[j][nblk - 1:nblk, :], (8, 128))
            cr = jnp.where(rowb == 0, c0s[j][0], pltpu.roll(ers[j], 1, 0))
            ci_ = jnp.where(rowb == 0, c0s[j][1], pltpu.roll(eis[j], 1, 0))
            ar = tab_ref[4, 0:1, lre]
            ai = tab_ref[5, 0:1, lre]
            car_ref[:, lre] = ar * cr - ai * ci_
            car_ref[:, lim] = ar * ci_ + ai * cr
        yield
        for c in range(nc):
            for lre, lim in lanes:
                cr = jnp.concatenate(
                    [jnp.broadcast_to(car_ref[c * bpc + j:c * bpc + j + 1, lre], (LS, 128))
                     for j in range(bpc)], axis=0)
                ci_ = jnp.concatenate(
                    [jnp.broadcast_to(car_ref[c * bpc + j:c * bpc + j + 1, lim], (LS, 128))
                     for j in range(bpc)], axis=0)
                zr = bu_ref[crow[c], lre] + cr
                zi = bu_ref[crow[c], lim] + ci_
                pr = jnp.tile(tab_ref[2, :, lre], (bpc, 1))
                pi = jnp.tile(tab_ref[3, :, lre], (bpc, 1))
                xcat_ref[crow[c], lre] = (pr * zr - pi * zi).astype(BF16)
                xcat_ref[crow[c], lim] = (pr * zi + pi * zr).astype(BF16)
            yield
        sy = _dot(xcat_ref[...], ct_ref[...]) + dsk_ref[...] * su
        yield
        sy = _gelu_tanh(sy)
        gate = _dot(sy.astype(BF16), wglu_ref[...])
        yield
        mix_ref[:, 2 * GW:3 * GW] = (sy * _sigmoid(gate) * _silu(PB(SG))).astype(BF16)
        yield

    def retention_stages():
        rq = _rope(PB(RQ), cos_ref[...], s1_ref[...], s2_ref[...])
        rk = _rope(PB(RK), cos_ref[...], s1_ref[...], s2_ref[...])
        rvf = PB(RV)
        rv = rvf.astype(BF16)
        rqb = rq.astype(BF16)
        rkb = rk.astype(BF16)
        kst = [_stack_heads(rkb[crow[c]], hm) for c in range(nc)]
        vst = [_stack_heads(rv[crow[c]], hm) for c in range(nc)]
        yield
        inner = [_dot_nt(rqb[crow[c]], kst[c]) for c in range(nc)]
        yield
        pmat = [(inner[c] * dec_ref[...]).astype(BF16) for c in range(nc)]
        kd = [(rk[crow[c]] * kdec_ref[...]).astype(BF16) for c in range(nc)]
        yield
        rloc = [_dot(pmat[c], vst[c]) for c in range(nc)]
        upd = [_dot_tn(kd[c], rv[crow[c]]) for c in range(nc)]
        yield
        st = [s_ref[...]]
        for c in range(nc):
            st.append(st[c] * cdec_ref[...] + jnp.where(bd, upd[c], 0.0))
        s_ref[...] = st[nc]
        qd = [(rq[crow[c]] * qdec_ref[...]).astype(BF16) for c in range(nc)]
        yield
        ost = [_dot(qd[c], st[c].astype(BF16)) for c in range(nc)]
        yield
        ro = jnp.concatenate([rloc[c] + ost[c] for c in range(nc)], axis=0)
        mix_ref[:, 0:GW] = (_head_norm(ro, retgn_ref[...], avg) * _silu(PB(RG))).astype(BF16)
        yield

    def mlstm_stages():
        ig_r = gates_t[0:8] + bi_ref[...]
        lf_r = _log_sigmoid(gates_t[8:16] + bf_ref[...])
        yield
        b_r = [_dot_x3(lf_r[:, crow[c]], tri_u) for c in range(nc)]
        yield
        gd_r = [ig_r[:, crow[c]] - b_r[c] for c in range(nc)]
        lane_r = lax.broadcasted_iota(jnp.int32, (8, L), 1)
        cm_r = list(gd_r)
        for k in range(7):
            d = 1 << k
            cm_r = [jnp.maximum(v, jnp.where(lane_r >= d, pltpu.roll(v, d, 1), NEG_INF))
                    for v in cm_r]
            yield
        m_prev = [m_ref[...]]
        mt_r = []
        for c in range(nc):
            mt = jnp.maximum(b_r[c] + m_prev[c], b_r[c] + cm_r[c])
            mt_r.append(mt)
            m_prev.append(jnp.broadcast_to(mt[:, L - 1:L], (8, L)))
            yield
        m_ref[...] = m_prev[nc]
        cols = []
        for c in range(nc):
            bm = b_r[c] - mt_r[c]
            ws = jnp.exp(b_r[c] + m_prev[c] - mt_r[c])
            wl = jnp.exp(gd_r[c] + jnp.broadcast_to(bm[:, L - 1:L], (8, L)))
            emt = jnp.exp(-mt_r[c])
            cols.append(jnp.concatenate([bm, ws, wl, emt, jnp.zeros((L - 32, L), F32)], axis=0).T)
        mqf = PB(MQ)
        mq = mqf.astype(BF16)
        mkf = PB(MK)
        mvf = PB(MV)
        mv_ = mvf.astype(BF16)
        mkb = mkf.astype(BF16)
        kst = [_stack_heads(mkb[crow[c]], hm) for c in range(nc)]
        vst = [_stack_heads(mv_[crow[c]], hm) for c in range(nc)]
        yield
        sraw = [_dot_nt(mq[crow[c]], kst[c]) for c in range(nc)]
        yield
        smat, den_i = [], []
        for c in range(nc):
            parts, dens = [], []
            for h in range(NH):
                arg = jnp.where(causal, cols[c][:, h:h + 1] + gd_r[c][h:h + 1, :], NEG_INF)
                s_h = sraw[c][:, h * L:(h + 1) * L] * jnp.exp(arg)
                dens.append(jnp.sum(s_h, axis=-1, keepdims=True))
                parts.append(s_h.astype(BF16))
            den_i.append(dens)
            smat.append(jnp.concatenate(parts, axis=1))
            yield
        ones_blk = jnp.ones((L, 128), BF16)
        def per_head(columns):
            first = lane128 < DH
            return jnp.concatenate([jnp.where(first, columns[0], columns[1]),
                                    jnp.where(first, columns[2], columns[3])], axis=1)

        kws = [(mkf[crow[c]] * per_head([cols[c][:, 16 + h:17 + h] for h in range(NH)])
                ).astype(BF16) for c in range(nc)]
        yield
        rloc = [_dot(smat[c], vst[c]) for c in range(nc)]
        u = [_dot_tn(kws[c], jnp.concatenate([mv_[crow[c]], ones_blk], axis=1))
             for c in range(nc)]
        yield
        nmask = (lax.broadcasted_iota(jnp.int32, (GW, 128), 0) // DH
                 == lax.broadcasted_iota(jnp.int32, (GW, 128), 1))
        cst = [c_ref[...]]
        nst = [n_ref[...]]
        for c in range(nc):
            wsl256 = jnp.zeros((1, GW), F32)
            wsl128 = jnp.zeros((1, 128), F32)
            for h in range(NH):
                wsl = cols[c][L - 1:L, 8 + h:9 + h]
                wsl256 = wsl256 + jnp.where(hm[h], wsl, 0.0)
                wsl128 = wsl128 + jnp.where(lane128 == h, wsl, 0.0)
            cst.append(cst[c] * wsl256 + jnp.where(bd, u[c][:, :GW], 0.0))
            nst.append(nst[c] * wsl128 + jnp.where(nmask, u[c][:, GW:], 0.0))
        c_ref[...] = cst[nc]
        n_ref[...] = nst[nc]
        yield
        qc = [_dot(mq[crow[c]], cst[c].astype(BF16)) for c in range(nc)]
        qn = [_dot(mq[crow[c]], nst[c].astype(BF16)) for c in range(nc)]
        yield
        mhs = []
        for c in range(nc):
            wsc = [cols[c][:, 8 + h:9 + h] for h in range(NH)]
            rdn = []
            for h in range(NH):
                den = den_i[c][h] + wsc[h] * qn[c][:, h:h + 1]
                rdn.append(1.0 / jnp.maximum(jnp.abs(den), cols[c][:, 24 + h:25 + h]))
            mhs.append((rloc[c] + per_head(wsc) * qc[c]) * per_head(rdn))
            yield
        mh = jnp.concatenate(mhs, axis=0) * _sigmoid(PB(MO))
        mix_ref[:, GW:2 * GW] = (_head_norm(mh, mlgn_ref[...], avg) * _silu(PB(MG))).astype(BF16)
        yield

    pending = [mlstm_stages(), s5_stages(), retention_stages(), xattn_stages()]
    while pending:
        pending = [stage for stage in pending if next(stage, _DONE) is not _DONE]

    y = x_ref[0] + _dot(mix_ref[...], wout_ref[...])
    if last_layer:
        y = _rms_norm(y, fnw_ref[...])
    y_ref[0] = y

    @pl.when(t == n_tblocks - 1)
    def _final():
        s_fin = s_ref[...]
        c_fin = c_ref[...].T
        n_fin = n_ref[...].T
        for h in range(NH):
            blk = slice(h * DH, (h + 1) * DH)
            rets_ref[0, h] = s_fin[blk, blk]
            mlc_ref[0, h] = c_fin[blk, blk]
        mln_ref[0] = jnp.concatenate([n_fin[h:h + 1, h * DH:(h + 1) * DH] for h in range(NH)],
                                     axis=0)
        mlm_ref[0] = m_ref[...]
        s5re_ref[0] = xre_ref[...]
        s5im_ref[0] = xim_ref[...]


def _prompt_layer(layer, x, mem, w, consts, last_layer, prev):
    bsz, seq, _ = x.shape
    depth = w["w_in"].shape[0]
    nt = seq // TB
    full = lambda shape: pl.BlockSpec(shape, lambda b, t: (0,) * len(shape),
                                      pipeline_mode=pl.Buffered(1))
    lyr = lambda shape: pl.BlockSpec((None,) + shape, lambda b, t: (layer,) + (0,) * len(shape),
                                     pipeline_mode=pl.Buffered(1))
    tok = lambda width: pl.BlockSpec((TB, width), lambda b, t: (t, 0))
    per_b = lambda r, c: pl.BlockSpec((1, r, c), lambda b, t: (b, 0, 0))
    in_specs = [
        pl.BlockSpec((1, TB, D), lambda b, t: (b, t, 0)),
        per_b(MEM, D),
        lyr((1, D)), lyr((DP, D)), lyr((NGATE, D)), lyr((D, D)), lyr((D, 2 * GW)),
        tok(128), tok(128), tok(128),
        full((L, NH * L)), full((L, GW)), full((L, GW)), full((1, GW)),
        lyr((1, GW)), lyr((1, GW)), lyr((8, TB)), lyr((8, TB)),
        lyr((6, LS, NS)), lyr((GW, 2 * NS)), lyr((2 * NS, GW)),
        lyr((1, GW)), lyr((GW, GW)), full((1, D)),
    ] + [pl.BlockSpec(memory_space=pl.ANY)] * len(prev)
    n_in = len(in_specs) - len(prev)
    state_shapes = [(NH, DH, DH),
                    (NH, DH, DH),
                    (NH, DH),
                    (8, 128),
                    (8, NS),
                    (8, NS),
                    (GW, MEM),
                    (GW, MEM)]
    assert len(state_shapes) == N_STATE_OUT and len(prev) in (0, N_STATE_OUT)
    out_shape = ((jax.ShapeDtypeStruct((bsz, seq, D), F32),)
                 + tuple(jax.ShapeDtypeStruct((depth, bsz) + s, F32) for s in state_shapes))
    state_spec = lambda s: pl.BlockSpec((None, 1) + s,
                                        lambda b, t: (layer, b) + (0,) * len(s))
    out_specs = ((pl.BlockSpec((1, TB, D), lambda b, t: (b, t, 0)),)
                 + tuple(state_spec(s) for s in state_shapes))
    scratch = [
        pltpu.VMEM((TB, DP), F32),
        pltpu.VMEM((TB, D), BF16),
        pltpu.VMEM((GW, GW), F32),
        pltpu.VMEM((GW, GW), F32),
        pltpu.VMEM((GW, 128), F32),
        pltpu.VMEM((8, 128), F32),
        pltpu.VMEM((8, NS), F32),
        pltpu.VMEM((8, NS), F32),
        pltpu.VMEM((NH * MEM, GW), BF16),
        pltpu.VMEM((NH * MEM, GW), BF16),
        pltpu.VMEM((TB, 2 * NS), F32),
        pltpu.VMEM((TB, 2 * NS), BF16),
        pltpu.VMEM((TB // LS, 2 * NS), F32),
    ]
    kern = functools.partial(_prompt_kernel, last_layer=last_layer, n_tblocks=nt, n_prev=len(prev))
    return pl.pallas_call(
        kern, grid=(bsz, nt), in_specs=in_specs, out_specs=out_specs, out_shape=out_shape,
        scratch_shapes=scratch, name="prompt_layer",
        input_output_aliases={n_in + i: 1 + i for i in range(len(prev))},
        compiler_params=pltpu.CompilerParams(
            dimension_semantics=("arbitrary", "arbitrary"), vmem_limit_bytes=VMEM_LIMIT),
    )(x, mem, w["norm_w"], w["w_in"], w["w_gate"], w["w_out"], w["w_mem_kv"],
      consts["cos_p"], consts["s1_p"], consts["s2_p"],
      consts["dec"], consts["qdec"], consts["kdec"], consts["cdec"],
      w["ret_gn"], w["ml_gn"], w["b_i8"], w["b_f8"],
      w["tab"], w["bt"], w["ct"], w["s5_d"], w["w_glu"], w["final_norm_w"], *prev)


SROWS = 16


def _sample_kernel_t(x_ref, normw_ref, win_ref, wg_ref, wout_ref,
                     cos_ref, s1_ref, s2_ref, gam_ref, gam8_ref,
                     retgn_ref, mlgn_ref, bi_ref, bf_ref,
                     abt_ref, bt_ref, ct_ref, dsk_ref, wglu_ref, fnw_ref,
                     m0_ref, n0_ref, x0re_ref, x0im_ref,
                     rets_ref, mlc_ref, ck_ref, cv_ref,
                     y_ref, retn_ref, mlcn_ref, mlnn_ref, mlmn_ref, s5re_ref, s5im_ref,
                     hs_ref, proj_ref, qgt_ref, rkt_ref, rvt_ref, mqt_ref, mkt_ref, mvt_ref,
                     wi_ref, ws_ref, ot_ref, cqt_ref, qa8_ref, xa_ref,
                     *, n_layers, n_blocks):
    layer = pl.program_id(0)
    g = pl.program_id(1)
    nsamp = x_ref.shape[0]
    ones_bd = _ones_matrix()
    avg = _avg_matrix()
    mask8 = (lax.broadcasted_iota(jnp.int32, (8, GW), 0)
             == lax.broadcasted_iota(jnp.int32, (8, GW), 1) // DH)

    def P(blk):
        return proj_ref[:, blk * GW:(blk + 1) * GW]

    @pl.when((layer == 0) & (g == 0))
    def _load_x():
        hs_ref[...] = x_ref[...]

    @pl.when(g == 0)
    def _pre():
        hn = _rms_norm(hs_ref[...], normw_ref[...]).astype(BF16)
        proj_ref[...] = _dot_nt(hn, win_ref[...])
        cosr, s1, s2 = cos_ref[...], s1_ref[...], s2_ref[...]
        qgt_ref[...] = (_rope(P(RQ), cosr, s1, s2) * gam_ref[...]).T
        rkt_ref[...] = _rope(P(RK), cosr, s1, s2).T
        rvt_ref[...] = P(RV).T
        mqt_ref[...] = P(MQ).T
        mkt_ref[...] = P(MK).T
        mvt_ref[...] = P(MV).T
        gt = _dot_nt(wg_ref[...], hn)
        ig = gt[0:8] + bi_ref[...]
        lf = _log_sigmoid(gt[8:16] + bf_ref[...])
        a = lf + m0_ref[...]
        mt = jnp.maximum(a, ig)
        wi_ref[...] = jnp.exp(ig - mt)
        ws_ref[...] = jnp.exp(a - mt)
        mlmn_ref[...] = mt
        ot_ref[...] = jnp.zeros_like(ot_ref)
        qa8_ref[...] = jnp.where(mask8[None], P(AQ)[:, None, :], 0.0).reshape(8 * nsamp, GW)
        but = _dot_tn(bt_ref[...], P(SU).T.astype(BF16))
        are, aim = abt_ref[0:NS, :], abt_ref[NS:2 * NS, :]
        x0r, x0i = x0re_ref[...], x0im_ref[...]
        s5re_ref[...] = are * x0r - aim * x0i + but[0:NS]
        s5im_ref[...] = are * x0i + aim * x0r + but[NS:2 * NS]

    head = g // (GW // SROWS // NH)
    hrow = pl.ds(pl.multiple_of(head * DH, DH), DH)
    gam_row = gam8_ref[pl.ds(head, 1), :]
    ws_row = ws_ref[pl.ds(head, 1), :]
    wi_row = wi_ref[pl.ds(head, 1), :]
    v_slab = rvt_ref[hrow, :]
    q_slab = mqt_ref[hrow, :]
    k_slab = mkt_ref[hrow, :]
    o_acc = jnp.zeros((DH, nsamp), F32)
    for i in range(SROWS):
        r = pl.ds(g * SROWS + i, 1)
        s_t = rets_ref[i]
        o_acc = o_acc + qgt_ref[r, :] * s_t
        retn_ref[i] = gam_row * s_t + rkt_ref[r, :] * v_slab
        c_t = mlc_ref[i]
        cqt_ref[r, :] = jnp.sum(c_t * q_slab, axis=0, keepdims=True)
        mlcn_ref[i] = ws_row * c_t + (wi_row * mvt_ref[r, :]) * k_slab
    ot_ref[hrow, :] = ot_ref[hrow, :] + o_acc

    tiles = [pl.ds(pl.multiple_of((g * SB + i) * 8, 8), 8) for i in range(SB)]
    q8 = [qa8_ref[tiles[i], :].astype(BF16) for i in range(SB)]
    sc = [_dot(q8[i], ck_ref[i].astype(BF16)) for i in range(SB)]
    ps = []
    for i in range(SB):
        e = jnp.exp(sc[i] - jnp.max(sc[i], axis=-1, keepdims=True))
        ps.append((e / jnp.sum(e, axis=-1, keepdims=True)).astype(BF16))
    ta = [_dot_nt(ps[i], cv_ref[i].astype(BF16)) for i in range(SB)]
    xa_ref[pl.ds(pl.multiple_of(g * SB, SB), SB), :] = jnp.concatenate(
        [jnp.sum(jnp.where(mask8, ta[i], 0.0), axis=0, keepdims=True) for i in range(SB)], axis=0)

    @pl.when(g == n_blocks - 1)
    def _post():
        cosr, s1, s2 = cos_ref[...], s1_ref[...], s2_ref[...]
        rq = _rope(P(RQ), cosr, s1, s2)
        rk = _rope(P(RK), cosr, s1, s2)
        ro = _dot_x2(rq * rk, ones_bd) * P(RV) + ot_ref[...].T
        ret_out = _head_norm(ro, retgn_ref[...], avg) * _silu(P(RG))
        mht = []
        for h in range(NH):
            rows = slice(h * DH, (h + 1) * DH)
            wi = wi_ref[h:h + 1, :]
            ws = ws_ref[h:h + 1, :]
            emt = jnp.exp(-mlmn_ref[h:h + 1, :])
            qt, kt, n0 = mqt_ref[rows, :], mkt_ref[rows, :], n0_ref[rows, :]
            s = jnp.sum(qt * kt, axis=0, keepdims=True) * wi
            den = s + ws * jnp.sum(n0 * qt, axis=0, keepdims=True)
            mht.append((s * mvt_ref[rows, :] + ws * cqt_ref[rows, :])
                       / jnp.maximum(jnp.abs(den), emt))
            mlnn_ref[rows, :] = ws * n0 + wi * kt
        mh = jnp.concatenate(mht, axis=0).T * _sigmoid(P(MO))
        ml_out = _head_norm(mh, mlgn_ref[...], avg) * _silu(P(MG))
        su = P(SU)
        xcat = jnp.concatenate([s5re_ref[...], s5im_ref[...]], axis=0).astype(BF16)
        sy = _dot_tn(xcat, ct_ref[...]) + dsk_ref[...] * su
        sy = _gelu_tanh(sy)
        sy = sy * _sigmoid(_dot(sy.astype(BF16), wglu_ref[...]))
        s5_out = sy * _silu(P(SG))
        xa_out = xa_ref[...] * _silu(P(AG))
        mix = jnp.concatenate([ret_out, ml_out, s5_out, xa_out], axis=1).astype(BF16)
        y = hs_ref[...] + _dot(mix, wout_ref[...])
        hs_ref[...] = y

        @pl.when(layer == n_layers - 1)
        def _emit():
            y_ref[...] = _rms_norm(y, fnw_ref[...])


def _sample_layers_t(x, st, w, consts):
    nsamp = x.shape[0]
    depth = w["w_in"].shape[0]
    nb = GW // SROWS
    assert nsamp == nb * SB and nsamp == 128
    once = lambda shape: pl.BlockSpec(shape, lambda l, g: (0,) * len(shape),
                                      pipeline_mode=pl.Buffered(1))
    lyr = lambda shape: pl.BlockSpec((None,) + shape, lambda l, g: (l,) + (0,) * len(shape))
    lyr_out = lyr
    srows = pl.BlockSpec((None, SROWS, DH, nsamp), lambda l, g: (l, g, 0, 0))
    cache = pl.BlockSpec((None, SB, GW, MEM), lambda l, g: (l, g, 0, 0))
    in_specs = [
        once((nsamp, D)), lyr((1, D)), lyr((DP, D)), lyr((NGATE, D)), lyr((D, D)),
        once((1, 128)), once((1, 128)), once((1, 128)), once((1, GW)), once((8, nsamp)),
        lyr((1, GW)), lyr((1, GW)), lyr((8, nsamp)), lyr((8, nsamp)),
        lyr((2 * NS, nsamp)), lyr((GW, 2 * NS)), lyr((2 * NS, GW)),
        lyr((1, GW)), lyr((GW, GW)), once((1, D)),
        lyr((8, nsamp)), lyr((GW, nsamp)), lyr((NS, nsamp)), lyr((NS, nsamp)),
        srows, srows, cache, cache,
    ]
    out_shape = (
        jax.ShapeDtypeStruct((nsamp, D), F32),
        jax.ShapeDtypeStruct((depth, GW, DH, nsamp), F32),
        jax.ShapeDtypeStruct((depth, GW, DH, nsamp), F32),
        jax.ShapeDtypeStruct((depth, GW, nsamp), F32),
        jax.ShapeDtypeStruct((depth, 8, nsamp), F32),
        jax.ShapeDtypeStruct((depth, NS, nsamp), F32),
        jax.ShapeDtypeStruct((depth, NS, nsamp), F32),
    )
    out_specs = (
        pl.BlockSpec((nsamp, D), lambda l, g: (0, 0)), srows, srows,
        lyr_out((GW, nsamp)), lyr_out((8, nsamp)), lyr_out((NS, nsamp)), lyr_out((NS, nsamp)),
    )
    scratch = [
        pltpu.VMEM((nsamp, D), F32),
        pltpu.VMEM((nsamp, DP), F32),
        pltpu.VMEM((GW, nsamp), F32), pltpu.VMEM((GW, nsamp), F32), pltpu.VMEM((GW, nsamp), F32),
        pltpu.VMEM((GW, nsamp), F32), pltpu.VMEM((GW, nsamp), F32), pltpu.VMEM((GW, nsamp), F32),
        pltpu.VMEM((8, nsamp), F32), pltpu.VMEM((8, nsamp), F32),
        pltpu.VMEM((GW, nsamp), F32), pltpu.VMEM((GW, nsamp), F32),
        pltpu.VMEM((8 * nsamp, GW), F32), pltpu.VMEM((nsamp, GW), F32),
    ]
    kern = functools.partial(_sample_kernel_t, n_layers=depth, n_blocks=nb)
    return pl.pallas_call(
        kern, grid=(depth, nb), in_specs=in_specs, out_specs=out_specs, out_shape=out_shape,
        scratch_shapes=scratch, name="sample_layers",
        compiler_params=pltpu.CompilerParams(
            dimension_semantics=("arbitrary", "arbitrary"), vmem_limit_bytes=VMEM_LIMIT),
    )(x, w["norm_w"], w["w_in"], w["w_gate"], w["w_out"],
      consts["cos_s"], consts["s1_s"], consts["s2_s"], consts["gam"], consts["gam8"],
      w["ret_gn"], w["ml_gn"], w["b_i8"], w["b_f8"],
      w["abt"], w["bt"], w["ct"], w["s5_d"], w["w_glu"], w["final_norm_w"],
      st["m"], st["n"], st["s5_re"], st["s5_im"], st["ret"], st["c"], st["mem_k"], st["mem_v"])


def _rope_tables(pos):
    half = DH // 2
    inv = ROPE_BASE ** (-np.arange(half, dtype=np.float64) / half)
    ang = np.asarray(pos, np.float64)[:, None] * inv[None, :]
    cos, sin = np.cos(ang), np.sin(ang)
    zero = np.zeros_like(sin)
    c = np.tile(np.concatenate([cos, cos], axis=-1), (1, 2))
    s1 = np.tile(np.concatenate([zero, sin], axis=-1), (1, 2))
    s2 = np.tile(np.concatenate([-sin, zero], axis=-1), (1, 2))
    return tuple(jnp.asarray(t, F32) for t in (c, s1, s2))


def _constants(seq):
    lg = np.log1p(-np.power(2.0, -5.0 - np.arange(NH, dtype=np.float64)))[:, None]
    idx = np.arange(L, dtype=np.float64)
    diff = idx[:, None] - idx[None, :]
    decay = np.where(diff >= 0, np.exp(lg[:, :, None] * np.maximum(diff, 0.0)), 0.0)
    rep = lambda t: np.repeat(t, DH, axis=0).T
    consts = {
        "dec": np.transpose(decay, (1, 0, 2)).reshape(L, NH * L),
        "qdec": rep(np.exp(lg * (idx + 1.0))),
        "kdec": rep(np.exp(lg * (L - 1.0 - idx))),
        "cdec": rep(np.exp(lg * L)),
        "gam": rep(np.exp(lg * 1.0)),
        "gam8": np.pad(np.broadcast_to(np.exp(lg), (NH, 128)), ((0, 8 - NH), (0, 0))),
    }
    consts = {k: jnp.asarray(v, F32) for k, v in consts.items()}
    consts["cos_p"], consts["s1_p"], consts["s2_p"] = _rope_tables(np.arange(seq))
    consts["cos_s"], consts["s1_s"], consts["s2_s"] = _rope_tables(PAST_LEN + np.arange(1))
    return consts


def _pack_w_in(w_in):
    wt = jnp.swapaxes(w_in, 1, 2)
    sizes = (GW,) * 9 + (NH, NH) + (GW,) * 4
    offs = np.concatenate([[0], np.cumsum(sizes)])
    seg = [wt[:, int(offs[i]):int(offs[i + 1]), :] for i in range(len(sizes))]
    scale = DH ** -0.5
    pad8 = lambda t: jnp.pad(t, ((0, 0), (0, 8 - t.shape[1]), (0, 0)))
    blocks = [seg[11],
              seg[0], seg[1] * scale, seg[2], seg[3],
              seg[4], seg[5] * scale, seg[6], seg[7], seg[8],
              seg[12], seg[13] * scale, seg[14]]
    gates = jnp.concatenate([pad8(seg[9]), pad8(seg[10])], axis=1)
    return jnp.concatenate(blocks, axis=1).astype(BF16), gates.astype(BF16)


def kernel(x_prompt, x_sample, mem_prompt, state_ret, state_mlstm_c, state_mlstm_n, state_mlstm_m,
           state_s5_re, state_s5_im, cache_mem_k, cache_mem_v,
           norm_w, w_in, ret_gn, ml_b_i, ml_b_f, ml_gn,
           s5_a_re, s5_a_im, s5_log_dt, s5_b_re, s5_b_im, s5_c_re, s5_c_im, s5_d, s5_w_glu,
           w_mem_k, w_mem_v, w_out, final_norm_w):
    depth = norm_w.shape[0]
    bp, seq, _ = x_prompt.shape
    bs = x_sample.shape[0]
    consts = _constants(seq)
    abt, bt, ct, tab = _s5_prepare(s5_a_re, s5_a_im, s5_log_dt, s5_b_re, s5_b_im, s5_c_re, s5_c_im)
    rows8 = lambda t: jnp.pad(jnp.broadcast_to(t[:, :, None], (depth, NH, TB)),
                              ((0, 0), (0, 8 - NH), (0, 0)))
    w_main, w_gate = _pack_w_in(w_in)
    w = {
        "norm_w": norm_w[:, None], "w_in": w_main, "w_gate": w_gate, "w_out": w_out.astype(BF16),
        "w_mem_kv": jnp.concatenate([w_mem_k, w_mem_v], axis=-1).astype(BF16),
        "ret_gn": ret_gn[:, None], "ml_gn": ml_gn[:, None],
        "b_i8": rows8(ml_b_i), "b_f8": rows8(ml_b_f),
        "abt": abt, "tab": tab, "bt": bt, "ct": ct,
        "s5_d": s5_d[:, None], "w_glu": s5_w_glu.astype(BF16),
        "final_norm_w": final_norm_w[None],
    }

    hp = x_prompt
    states = ()
    for l in range(depth):
        hp, *states = _prompt_layer(l, hp, mem_prompt, w, consts, l == depth - 1, tuple(states))
    ret_p, mlc_p, mln_p, ms, xr, xi, mk, mv = states
    mlm_p = ms[:, :, :NH, 0]
    s5re_p = xr[:, :, 0].reshape(depth, bp, S5G, S5P)
    s5im_p = xi[:, :, 0].reshape(depth, bp, S5G, S5P)
    memk_p = jnp.transpose(mk.reshape(depth, bp, NH, DH, MEM), (0, 1, 4, 2, 3))
    memv_p = jnp.transpose(mv.reshape(depth, bp, NH, DH, MEM), (0, 1, 4, 2, 3))

    st = {
        "m": jnp.pad(jnp.swapaxes(state_mlstm_m, 1, 2), ((0, 0), (0, 8 - NH), (0, 0))),
        "n": jnp.transpose(state_mlstm_n, (0, 2, 3, 1)).reshape(depth, GW, bs),
        "s5_re": jnp.transpose(state_s5_re, (0, 2, 3, 1)).reshape(depth, NS, bs),
        "s5_im": jnp.transpose(state_s5_im, (0, 2, 3, 1)).reshape(depth, NS, bs),
        "ret": jnp.transpose(state_ret, (0, 2, 3, 4, 1)).reshape(depth, GW, DH, bs),
        "c": jnp.transpose(state_mlstm_c, (0, 2, 3, 4, 1)).reshape(depth, GW, DH, bs),
        "mem_k": jnp.transpose(cache_mem_k, (0, 1, 3, 4, 2)).reshape(depth, bs, GW, MEM),
        "mem_v": jnp.transpose(cache_mem_v, (0, 1, 3, 4, 2)).reshape(depth, bs, GW, MEM),
    }
    hs, rn, cn, nn, mn, sr, si = _sample_layers_t(x_sample.reshape(bs, D), st, w, consts)
    back5 = lambda t: jnp.transpose(t.reshape(depth, NH, DH, DH, bs), (0, 4, 1, 2, 3))
    back4 = lambda t, a, b: jnp.transpose(t.reshape(depth, a, b, bs), (0, 3, 1, 2))
    return (hp, hs.reshape(bs, 1, D),
            ret_p, back5(rn), mlc_p, back5(cn),
            mln_p, back4(nn, NH, DH), mlm_p, jnp.swapaxes(mn[:, :NH], 1, 2),
            s5re_p, back4(sr, S5G, S5P), s5im_p, back4(si, S5G, S5P),
            memk_p, memv_p)
```

```python
import functools
import math

import numpy as np
import jax
import jax.numpy as jnp
from jax import lax
from jax.experimental import pallas as pl
from jax.experimental.pallas import tpu as pltpu

F32 = jnp.float32
BF16 = jnp.bfloat16

D = 1024
GW = 256
NH = 4
DH = 64
L = 128
LS = 8
NTAB = 8
S5G = 16
S5P = 64
S5C = 16
NS = S5G * S5P
MEM = 256
EPS = 1e-6
NEG_INF = -1e30
ROPE_BASE = 10000.0
PAST_LEN = 16384

TB = 512
SB = 8
XQ = 512
NBLK = 13
DP = NBLK * GW
(SU, RQ, RK, RV, RG, MQ, MK, MV, MO, MG, SG, AQ, AG) = range(NBLK)
NGATE = 16

VMEM_LIMIT = 56 * 1024 * 1024


_DONE = object()


def _dot(a, b):
    return jnp.dot(a, b, preferred_element_type=F32)


def _dot_nt(a, b):
    return lax.dot_general(a, b, (((1,), (1,)), ((), ())), preferred_element_type=F32)


def _dot_tn(a, b):
    return lax.dot_general(a, b, (((0,), (0,)), ((), ())), preferred_element_type=F32)


def _split2(x):
    hi = x.astype(BF16)
    lo = (x - hi.astype(F32)).astype(BF16)
    return hi, lo


def _dot_x2(x, w):
    hi, lo = _split2(x)
    return _dot(hi, w) + _dot(lo, w)


def _dot_x3(x, w):
    hi = x.astype(BF16)
    r = x - hi.astype(F32)
    mid = r.astype(BF16)
    lo = (r - mid.astype(F32)).astype(BF16)
    return _dot(hi, w) + _dot(mid, w) + _dot(lo, w)


def _sigmoid(x):
    return 0.5 * (1.0 + jnp.tanh(0.5 * x))


def _silu(x):
    h = 0.5 * x
    return h + h * jnp.tanh(h)


def _log_sigmoid(x):
    return jnp.minimum(x, 0.0) - jnp.log1p(jnp.exp(-jnp.abs(x)))


def _gelu_tanh(x):
    c = math.sqrt(2.0 / math.pi)
    h = 0.5 * x
    return h + h * jnp.tanh(x * (c + (0.044715 * c) * (x * x)))


def _lane_head(n):
    return lax.broadcasted_iota(jnp.int32, (1, n), 1) // DH


def _head_masks():
    lh = _lane_head(GW)
    return [lh == h for h in range(NH)]


def _block_diag_mask():
    r = lax.broadcasted_iota(jnp.int32, (GW, GW), 0) // DH
    c = lax.broadcasted_iota(jnp.int32, (GW, GW), 1) // DH
    return r == c


def _avg_matrix():
    return jnp.where(_block_diag_mask(), 1.0 / DH, 0.0).astype(BF16)


def _ones_matrix():
    return jnp.where(_block_diag_mask(), 1.0, 0.0).astype(BF16)


def _rope(x, cos, s1, s2):
    outs = []
    for j in range(2):
        xs = x[:, j * 128:(j + 1) * 128]
        outs.append(xs * cos + pltpu.roll(xs, 32, 1) * s1 + pltpu.roll(xs, 96, 1) * s2)
    return jnp.concatenate(outs, axis=1)


def _head_norm(x, gain, avg):
    mu = _dot_x2(x, avg)
    d = x - mu
    var = _dot((d * d).astype(BF16), avg)
    return d * lax.rsqrt(var + EPS) * gain


def _rms_norm(x, w):
    ms = jnp.mean(x * x, axis=-1, keepdims=True)
    return x * lax.rsqrt(ms + EPS) * w


def _stack_heads(x, hm):
    zero = jnp.zeros((x.shape[0], 128), x.dtype)
    blocks = []
    for h in range(NH):
        half = slice((h * DH) // 128 * 128, (h * DH) // 128 * 128 + 128)
        kept = jnp.where(hm[h][:, half], x[:, half], 0.0)
        blocks.append(jnp.concatenate([kept, zero] if half.start == 0 else [zero, kept], axis=1))
    return jnp.concatenate(blocks, axis=0)


def _s5_prep_kernel(are_ref, aim_ref, ldt_ref, bre_ref, bim_ref, cre_ref, cim_ref,
                    abt_ref, bt_ref, ct_ref, tab_ref):
    a_re = are_ref[0]
    a_im = aim_ref[0]
    dt = jnp.exp(ldt_ref[0])
    lam_re = a_re * dt
    lam_im = a_im * dt
    mag = jnp.exp(lam_re)
    ab_re = mag * jnp.cos(lam_im)
    ab_im = mag * jnp.sin(lam_im)
    den = a_re * a_re + a_im * a_im
    nr = ab_re - 1.0
    ni = ab_im
    f_re = (nr * a_re + ni * a_im) / den
    f_im = (ni * a_re - nr * a_im) / den
    abt_ref[0, 0:NS, :] = jnp.broadcast_to(ab_re, (128, NS)).T
    abt_ref[0, NS:2 * NS, :] = jnp.broadcast_to(ab_im, (128, NS)).T
    b_re = bre_ref[0]
    b_im = bim_ref[0]
    bb_re = (f_re * b_re - f_im * b_im).astype(BF16)
    bb_im = (f_re * b_im + f_im * b_re).astype(BF16)
    rep_r = (lax.broadcasted_iota(jnp.int32, (GW, S5C), 0) % S5C
             == lax.broadcasted_iota(jnp.int32, (GW, S5C), 1)).astype(BF16)
    in_blk = (lax.broadcasted_iota(jnp.int32, (GW, NS), 0) // S5C
              == lax.broadcasted_iota(jnp.int32, (GW, NS), 1) // S5P)
    bt_ref[0, :, :NS] = jnp.where(in_blk, _dot(rep_r, bb_re), 0.0).astype(BF16)
    bt_ref[0, :, NS:] = jnp.where(in_blk, _dot(rep_r, bb_im), 0.0).astype(BF16)
    rep_c = (lax.broadcasted_iota(jnp.int32, (S5C, GW), 0)
             == lax.broadcasted_iota(jnp.int32, (S5C, GW), 1) % S5C).astype(BF16)
    out_blk = (lax.broadcasted_iota(jnp.int32, (NS, GW), 0) // S5P
               == lax.broadcasted_iota(jnp.int32, (NS, GW), 1) // S5C)
    ct_ref[0, :NS, :] = jnp.where(out_blk, _dot(cre_ref[0].astype(BF16), rep_c), 0.0).astype(BF16)
    ct_ref[0, NS:, :] = jnp.where(out_blk, -_dot(cim_ref[0].astype(BF16), rep_c), 0.0).astype(BF16)
    i = lax.broadcasted_iota(jnp.int32, (LS, NS), 0).astype(F32)
    for slot, k in ((0, -i), (2, i), (4, i + 1.0), (6, LS * (i + 1.0))):
        pmag = jnp.exp(k * lam_re)
        tab_ref[0, slot] = pmag * jnp.cos(k * lam_im)
        tab_ref[0, slot + 1] = pmag * jnp.sin(k * lam_im)


def _s5_prepare(a_re, a_im, log_dt, b_re, b_im, c_re, c_im):
    depth = a_re.shape[0]
    are = a_re.reshape(depth, 1, NS)
    aim = a_im.reshape(depth, 1, NS)
    ldt = jnp.repeat(log_dt, S5P, axis=-1).reshape(depth, 1, NS)
    bre = jnp.transpose(b_re, (0, 3, 1, 2)).reshape(depth, S5C, NS)
    bim = jnp.transpose(b_im, (0, 3, 1, 2)).reshape(depth, S5C, NS)
    cre = jnp.transpose(c_re, (0, 1, 3, 2)).reshape(depth, NS, S5C)
    cim = jnp.transpose(c_im, (0, 1, 3, 2)).reshape(depth, NS, S5C)
    per_layer = lambda *shape: pl.BlockSpec((1,) + shape, lambda l: (l,) + (0,) * len(shape))
    out_shape = (jax.ShapeDtypeStruct((depth, 2 * NS, 128), F32),
                 jax.ShapeDtypeStruct((depth, GW, 2 * NS), BF16),
                 jax.ShapeDtypeStruct((depth, 2 * NS, GW), BF16),
                 jax.ShapeDtypeStruct((depth, NTAB, LS, NS), F32))
    return pl.pallas_call(
        _s5_prep_kernel, grid=(depth,),
        in_specs=[per_layer(1, NS), per_layer(1, NS), per_layer(1, NS),
                  per_layer(S5C, NS), per_layer(S5C, NS), per_layer(NS, S5C), per_layer(NS, S5C)],
        out_specs=(per_layer(2 * NS, 128), per_layer(GW, 2 * NS), per_layer(2 * NS, GW),
                   per_layer(NTAB, LS, NS)),
        out_shape=out_shape, name="s5_prepare",
        compiler_params=pltpu.CompilerParams(dimension_semantics=("arbitrary",)),
    )(are, aim, ldt, bre, bim, cre, cim)


N_STATE_OUT = 8


def _prompt_kernel(x_ref, mem_ref, normw_ref, win_ref, wg_ref, wout_ref, wmkv_ref,
                   cos_ref, s1_ref, s2_ref, dec_ref, qdec_ref, kdec_ref, cdec_ref,
                   retgn_ref, mlgn_ref, bi_ref, bf_ref,
                   tab_ref, bt_ref, ct_ref, dsk_ref, wglu_ref, fnw_ref,
                   *rest, last_layer, n_tblocks, n_prev):
    (y_ref, rets_ref, mlc_ref, mln_ref, mlm_ref, s5re_ref, s5im_ref, memk_ref, memv_ref,
     proj_ref, mix_ref, s_ref, c_ref, n_ref, m_ref, xre_ref, xim_ref,
     mk_ref, mv_ref, bu_ref, xcat_ref, car_ref) = rest[n_prev:]
    t = pl.program_id(1)
    hm = _head_masks()
    bd = _block_diag_mask()
    avg = _avg_matrix()
    lane128 = lax.broadcasted_iota(jnp.int32, (1, 128), 1)
    row_i = lax.broadcasted_iota(jnp.int32, (L, 128), 0)
    col_i = lax.broadcasted_iota(jnp.int32, (L, 128), 1)
    causal = row_i >= col_i
    tri_u = jnp.where(row_i <= col_i, 1.0, 0.0).astype(BF16)
    tri_sub = jnp.where(causal & (row_i // LS == col_i // LS), 1.0, 0.0)
    blk_sum = jnp.where(lax.broadcasted_iota(jnp.int32, (L // LS, 128), 0)
                        == lax.broadcasted_iota(jnp.int32, (L // LS, 128), 1) // LS, 1.0, 0.0)
    tri_ext = jnp.concatenate([tri_sub, blk_sum], axis=0).astype(BF16)

    @pl.when(t == 0)
    def _init():
        s_ref[...] = jnp.zeros_like(s_ref)
        c_ref[...] = jnp.zeros_like(c_ref)
        n_ref[...] = jnp.zeros_like(n_ref)
        m_ref[...] = jnp.zeros_like(m_ref)
        xre_ref[...] = jnp.zeros_like(xre_ref)
        xim_ref[...] = jnp.zeros_like(xim_ref)
        mkv = _dot(mem_ref[0].astype(BF16), wmkv_ref[...])
        mk = mkv[:, :GW]
        mv = mkv[:, GW:]
        memk_ref[0] = mk.T
        memv_ref[0] = mv.T
        mk_ref[...] = _stack_heads(mk.astype(BF16), hm)
        mv_ref[...] = _stack_heads(mv.astype(BF16), hm)

    x = x_ref[0]
    hn = _rms_norm(x, normw_ref[...]).astype(BF16)
    proj_ref[...] = _dot_nt(hn, win_ref[...])
    gates_t = _dot_nt(wg_ref[...], hn)

    nc = TB // L
    crow = [slice(c * L, (c + 1) * L) for c in range(nc)]

    def PB(blk, rows=slice(None)):
        return proj_ref[rows, blk * GW:(blk + 1) * GW]


    def xattn_stages():
        for piece in range(TB // XQ):
            rs = slice(piece * XQ, (piece + 1) * XQ)
            sc = _dot_nt(PB(AQ, rs).astype(BF16), mk_ref[...])
            yield
            ps = []
            for h in range(NH):
                seg = sc[:, h * MEM:(h + 1) * MEM]
                e = jnp.exp(seg - jnp.max(seg, axis=-1, keepdims=True))
                ps.append((e / jnp.sum(e, axis=-1, keepdims=True)).astype(BF16))
            p = jnp.concatenate(ps, axis=1)
            yield
            xa = _dot(p, mv_ref[...])
            yield
            mix_ref[rs, 3 * GW:4 * GW] = (xa * _silu(PB(AG, rs))).astype(BF16)
            yield

    def s5_stages():
        lanes = [(slice(j * 128, (j + 1) * 128), slice(NS + j * 128, NS + (j + 1) * 128))
                 for j in range(NS // 128)]
        bpc = L // LS
        nblk = TB // LS
        su = PB(SU)
        bu_ref[...] = _dot(su.astype(BF16), bt_ref[...])
        yield
        for c in range(nc):
            for lre, lim in lanes:
                br = bu_ref[crow[c], lre]
                bi = bu_ref[crow[c], lim]
                wr = jnp.tile(tab_ref[0, :, lre], (bpc, 1))
                wi = jnp.tile(tab_ref[1, :, lre], (bpc, 1))
                xcat_ref[crow[c], lre] = (wr * br - wi * bi).astype(BF16)
                xcat_ref[crow[c], lim] = (wr * bi + wi * br).astype(BF16)
            yield
        for c in range(nc):
            z = _dot(tri_ext, xcat_ref[crow[c], :])
            bu_ref[crow[c], :] = z[0:L]
            car_ref[c * bpc:(c + 1) * bpc, :] = z[L:L + bpc]
            yield
        rowb = lax.broadcasted_iota(jnp.int32, (nblk, 128), 0)
        ers, eis, prs, pis, c0s = [], [], [], [], []
        for lre, lim in lanes:
            zr = car_ref[:, lre]
            zi = car_ref[:, lim]
            pr = tab_ref[2, LS - 1:LS, lre]
            pi = tab_ref[3, LS - 1:LS, lre]
            er = pr * zr - pi * zi
            ei = pr * zi + pi * zr
            pr = tab_ref[4, LS - 1:LS, lre]
            pi = tab_ref[5, LS - 1:LS, lre]
            c0r = xre_ref[0:1, lre]
            c0i = xim_ref[0:1, lre]
            ers.append(er + jnp.where(rowb == 0, pr * c0r - pi * c0i, 0.0))
            eis.append(ei + jnp.where(rowb == 0, pr * c0i + pi * c0r, 0.0))
            prs.append(pr)
            pis.append(pi)
            c0s.append((c0r, c0i))
        yield
        assert LS == 8 and nblk % 8 == 0
        ngrp = nblk // 8
        row8 = lax.broadcasted_iota(jnp.int32, (8, 128), 0)
        ers = [[v[g * 8:(g + 1) * 8] for g in range(ngrp)] for v in ers]
        eis = [[v[g * 8:(g + 1) * 8] for g in range(ngrp)] for v in eis]
        for k in range(3):
            d = 1 << k
            for j in range(len(lanes)):
                pr, pi = prs[j], pis[j]
                for g in range(ngrp):
                    er, ei = ers[j][g], eis[j][g]
                    sr = jnp.where(row8 >= d, pltpu.roll(er, d, 0), 0.0)
                    si = jnp.where(row8 >= d, pltpu.roll(ei, d, 0), 0.0)
                    ers[j][g], eis[j][g] = er + pr * sr - pi * si, ei + pr * si + pi * sr
                prs[j], pis[j] = pr * pr - pi * pi, 2.0 * (pr * pi)
            yield
        for g in range(1, ngrp):
            for j, (lre, lim) in enumerate(lanes):
                qr = tab_ref[6, :, lre]
                qi = tab_ref[7, :, lre]
                cr = jnp.broadcast_to(ers[j][g - 1][7:8, :], (8, 128))
                ci_ = jnp.broadcast_to(eis[j][g - 1][7:8, :], (8, 128))
                ers[j][g] = ers[j][g] + (qr * cr - qi * ci_)
                eis[j][g] = eis[j][g] + (qr * ci_ + qi * cr)
            if g % 3 == 0 or g == ngrp - 1:
                yield
        ers = [jnp.concatenate(v, axis=0) for v in ers]
        eis = [jnp.concatenate(v, axis=0) for v in eis]
        for j, (lre, lim) in enumerate(lanes):
            xre_ref[:, lre] = jnp.broadcast_to(ers[j][nblk - 1:nblk, :], (8, 128))
            xim_ref[:, lre] = jnp.broadcast_to(eis[j][nblk - 1:nblk, :], (8, 128))
            cr = jnp.where(rowb == 0, c0s[j][0], pltpu.roll(ers[j], 1, 0))
            ci_ = jnp.where(rowb == 0, c0s[j][1], pltpu.roll(eis[j], 1, 0))
            ar = tab_ref[4, 0:1, lre]
            ai = tab_ref[5, 0:1, lre]
            car_ref[:, lre] = ar * cr - ai * ci_
            car_ref[:, lim] = ar * ci_ + ai * cr
        yield
        for c in range(nc):
            for lre, lim in lanes:
                cr = jnp.concatenate(
                    [jnp.broadcast_to(car_ref[c * bpc + j:c * bpc + j + 1, lre], (LS, 128))
                     for j in range(bpc)], axis=0)
                ci_ = jnp.concatenate(
                    [jnp.broadcast_to(car_ref[c * bpc + j:c * bpc + j + 1, lim], (LS, 128))
                     for j in range(bpc)], axis=0)
                zr = bu_ref[crow[c], lre] + cr
                zi = bu_ref[crow[c], lim] + ci_
                pr = jnp.tile(tab_ref[2, :, lre], (bpc, 1))
                pi = jnp.tile(tab_ref[3, :, lre], (bpc, 1))
                xcat_ref[crow[c], lre] = (pr * zr - pi * zi).astype(BF16)
                xcat_ref[crow[c], lim] = (pr * zi + pi * zr).astype(BF16)
            yield
        sy = _dot(xcat_ref[...], ct_ref[...]) + dsk_ref[...] * su
        yield
        sy = _gelu_tanh(sy)
        gate = _dot(sy.astype(BF16), wglu_ref[...])
        yield
        mix_ref[:, 2 * GW:3 * GW] = (sy * _sigmoid(gate) * _silu(PB(SG))).astype(BF16)
        yield

    def retention_stages():
        rq = _rope(PB(RQ), cos_ref[...], s1_ref[...], s2_ref[...])
        rk = _rope(PB(RK), cos_ref[...], s1_ref[...], s2_ref[...])
        rvf = PB(RV)
        rv = rvf.astype(BF16)
        rqb = rq.astype(BF16)
        rkb = rk.astype(BF16)
        kst = [_stack_heads(rkb[crow[c]], hm) for c in range(nc)]
        vst = [_stack_heads(rv[crow[c]], hm) for c in range(nc)]
        yield
        inner = [_dot_nt(rqb[crow[c]], kst[c]) for c in range(nc)]
        yield
        pmat = [(inner[c] * dec_ref[...]).astype(BF16) for c in range(nc)]
        kd = [(rk[crow[c]] * kdec_ref[...]).astype(BF16) for c in range(nc)]
        yield
        rloc = [_dot(pmat[c], vst[c]) for c in range(nc)]
        upd = [_dot_tn(kd[c], rv[crow[c]]) for c in range(nc)]
        yield
        st = [s_ref[...]]
        for c in range(nc):
            st.append(st[c] * cdec_ref[...] + jnp.where(bd, upd[c], 0.0))
        s_ref[...] = st[nc]
        qd = [(rq[crow[c]] * qdec_ref[...]).astype(BF16) for c in range(nc)]
        yield
        ost = [_dot(qd[c], st[c].astype(BF16)) for c in range(nc)]
        yield
        ro = jnp.concatenate([rloc[c] + ost[c] for c in range(nc)], axis=0)
        mix_ref[:, 0:GW] = (_head_norm(ro, retgn_ref[...], avg) * _silu(PB(RG))).astype(BF16)
        yield

    def mlstm_stages():
        ig_r = gates_t[0:8] + bi_ref[...]
        lf_r = _log_sigmoid(gates_t[8:16] + bf_ref[...])
        yield
        b_r = [_dot_x3(lf_r[:, crow[c]], tri_u) for c in range(nc)]
        yield
        gd_r = [ig_r[:, crow[c]] - b_r[c] for c in range(nc)]
        lane_r = lax.broadcasted_iota(jnp.int32, (8, L), 1)
        cm_r = list(gd_r)
        for k in range(7):
            d = 1 << k
            cm_r = [jnp.maximum(v, jnp.where(lane_r >= d, pltpu.roll(v, d, 1), NEG_INF))
                    for v in cm_r]
            yield
        m_prev = [m_ref[...]]
        mt_r = []
        for c in range(nc):
            mt = jnp.maximum(b_r[c] + m_prev[c], b_r[c] + cm_r[c])
            mt_r.append(mt)
            m_prev.append(jnp.broadcast_to(mt[:, L - 1:L], (8, L)))
            yield
        m_ref[...] = m_prev[nc]
        cols = []
        for c in range(nc):
            bm = b_r[c] - mt_r[c]
            ws = jnp.exp(b_r[c] + m_prev[c] - mt_r[c])
            wl = jnp.exp(gd_r[c] + jnp.broadcast_to(bm[:, L - 1:L], (8, L)))
            emt = jnp.exp(-mt_r[c])
            cols.append(jnp.concatenate([bm, ws, wl, emt, jnp.zeros((L - 32, L), F32)], axis=0).T)
        mqf = PB(MQ)
        mq = mqf.astype(BF16)
        mkf = PB(MK)
        mvf = PB(MV)
        mv_ = mvf.astype(BF16)
        mkb = mkf.astype(BF16)
        kst = [_stack_heads(mkb[crow[c]], hm) for c in range(nc)]
        vst = [_stack_heads(mv_[crow[c]], hm) for c in range(nc)]
        yield
        sraw = [_dot_nt(mq[crow[c]], kst[c]) for c in range(nc)]
        yield
        smat, den_i = [], []
        for c in range(nc):
            parts, dens = [], []
            for h in range(NH):
                arg = jnp.where(causal, cols[c][:, h:h + 1] + gd_r[c][h:h + 1, :], NEG_INF)
                s_h = sraw[c][:, h * L:(h + 1) * L] * jnp.exp(arg)
                dens.append(jnp.sum(s_h, axis=-1, keepdims=True))
                parts.append(s_h.astype(BF16))
            den_i.append(dens)
            smat.append(jnp.concatenate(parts, axis=1))
            yield
        ones_blk = jnp.ones((L, 128), BF16)
        def per_head(columns):
            first = lane128 < DH
            return jnp.concatenate([jnp.where(first, columns[0], columns[1]),
                                    jnp.where(first, columns[2], columns[3])], axis=1)

        kws = [(mkf[crow[c]] * per_head([cols[c][:, 16 + h:17 + h] for h in range(NH)])
                ).astype(BF16) for c in range(nc)]
        yield
        rloc = [_dot(smat[c], vst[c]) for c in range(nc)]
        u = [_dot_tn(kws[c], jnp.concatenate([mv_[crow[c]], ones_blk], axis=1))
             for c in range(nc)]
        yield
        nmask = (lax.broadcasted_iota(jnp.int32, (GW, 128), 0) // DH
                 == lax.broadcasted_iota(jnp.int32, (GW, 128), 1))
        cst = [c_ref[...]]
        nst = [n_ref[...]]
        for c in range(nc):
            wsl256 = jnp.zeros((1, GW), F32)
            wsl128 = jnp.zeros((1, 128), F32)
            for h in range(NH):
                wsl = cols[c][L - 1:L, 8 + h:9 + h]
                wsl256 = wsl256 + jnp.where(hm[h], wsl, 0.0)
                wsl128 = wsl128 + jnp.where(lane128 == h, wsl, 0.0)
            cst.append(cst[c] * wsl256 + jnp.where(bd, u[c][:, :GW], 0.0))
            nst.append(nst[c] * wsl128 + jnp.where(nmask, u[c][:, GW:], 0.0))
        c_ref[...] = cst[nc]
        n_ref[...] = nst[nc]
        yield
        qc = [_dot(mq[crow[c]], cst[c].astype(BF16)) for c in range(nc)]
        qn = [_dot(mq[crow[c]], nst[c].astype(BF16)) for c in range(nc)]
        yield
        mhs = []
        for c in range(nc):
            wsc = [cols[c][:, 8 + h:9 + h] for h in range(NH)]
            rdn = []
            for h in range(NH):
                den = den_i[c][h] + wsc[h] * qn[c][:, h:h + 1]
                rdn.append(1.0 / jnp.maximum(jnp.abs(den), cols[c][:, 24 + h:25 + h]))
            mhs.append((rloc[c] + per_head(wsc) * qc[c]) * per_head(rdn))
            yield
        mh = jnp.concatenate(mhs, axis=0) * _sigmoid(PB(MO))
        mix_ref[:, GW:2 * GW] = (_head_norm(mh, mlgn_ref[...], avg) * _silu(PB(MG))).astype(BF16)
        yield

    pending = [mlstm_stages(), s5_stages(), retention_stages(), xattn_stages()]
    while pending:
        pending = [stage for stage in pending if next(stage, _DONE) is not _DONE]

    y = x_ref[0] + _dot(mix_ref[...], wout_ref[...])
    if last_layer:
        y = _rms_norm(y, fnw_ref[...])
    y_ref[0] = y

    @pl.when(t == n_tblocks - 1)
    def _final():
        s_fin = s_ref[...]
        c_fin = c_ref[...].T
        n_fin = n_ref[...].T
        for h in range(NH):
            blk = slice(h * DH, (h + 1) * DH)
            rets_ref[0, h] = s_fin[blk, blk]
            mlc_ref[0, h] = c_fin[blk, blk]
        mln_ref[0] = jnp.concatenate([n_fin[h:h + 1, h * DH:(h + 1) * DH] for h in range(NH)],
                                     axis=0)
        mlm_ref[0] = m_ref[...]
        s5re_ref[0] = xre_ref[...]
        s5im_ref[0] = xim_ref[...]


def _prompt_layer(layer, x, mem, w, consts, last_layer, prev):
    bsz, seq, _ = x.shape
    depth = w["w_in"].shape[0]
    nt = seq // TB
    full = lambda shape: pl.BlockSpec(shape, lambda b, t: (0,) * len(shape),
                                      pipeline_mode=pl.Buffered(1))
    lyr = lambda shape: pl.BlockSpec((None,) + shape, lambda b, t: (layer,) + (0,) * len(shape),
                                     pipeline_mode=pl.Buffered(1))
    tok = lambda width: pl.BlockSpec((TB, width), lambda b, t: (t, 0))
    per_b = lambda r, c: pl.BlockSpec((1, r, c), lambda b, t: (b, 0, 0))
    in_specs = [
        pl.BlockSpec((1, TB, D), lambda b, t: (b, t, 0)),
        per_b(MEM, D),
        lyr((1, D)), lyr((DP, D)), lyr((NGATE, D)), lyr((D, D)), lyr((D, 2 * GW)),
        tok(128), tok(128), tok(128),
        full((L, NH * L)), full((L, GW)), full((L, GW)), full((1, GW)),
        lyr((1, GW)), lyr((1, GW)), lyr((8, TB)), lyr((8, TB)),
        lyr((NTAB, LS, NS)), lyr((GW, 2 * NS)), lyr((2 * NS, GW)),
        lyr((1, GW)), lyr((GW, GW)), full((1, D)),
    ] + [pl.BlockSpec(memory_space=pl.ANY)] * len(prev)
    n_in = len(in_specs) - len(prev)
    state_shapes = [(NH, DH, DH),
                    (NH, DH, DH),
                    (NH, DH),
                    (8, 128),
                    (8, NS),
                    (8, NS),
                    (GW, MEM),
                    (GW, MEM)]
    assert len(state_shapes) == N_STATE_OUT and len(prev) in (0, N_STATE_OUT)
    out_shape = ((jax.ShapeDtypeStruct((bsz, seq, D), F32),)
                 + tuple(jax.ShapeDtypeStruct((depth, bsz) + s, F32) for s in state_shapes))
    state_spec = lambda s: pl.BlockSpec((None, 1) + s,
                                        lambda b, t: (layer, b) + (0,) * len(s))
    out_specs = ((pl.BlockSpec((1, TB, D), lambda b, t: (b, t, 0)),)
                 + tuple(state_spec(s) for s in state_shapes))
    scratch = [
        pltpu.VMEM((TB, DP), F32),
        pltpu.VMEM((TB, D), BF16),
        pltpu.VMEM((GW, GW), F32),
        pltpu.VMEM((GW, GW), F32),
        pltpu.VMEM((GW, 128), F32),
        pltpu.VMEM((8, 128), F32),
        pltpu.VMEM((8, NS), F32),
        pltpu.VMEM((8, NS), F32),
        pltpu.VMEM((NH * MEM, GW), BF16),
        pltpu.VMEM((NH * MEM, GW), BF16),
        pltpu.VMEM((TB, 2 * NS), F32),
        pltpu.VMEM((TB, 2 * NS), BF16),
        pltpu.VMEM((TB // LS, 2 * NS), F32),
    ]
    kern = functools.partial(_prompt_kernel, last_layer=last_layer, n_tblocks=nt, n_prev=len(prev))
    return pl.pallas_call(
        kern, grid=(bsz, nt), in_specs=in_specs, out_specs=out_specs, out_shape=out_shape,
        scratch_shapes=scratch, name="prompt_layer",
        input_output_aliases={n_in + i: 1 + i for i in range(len(prev))},
        compiler_params=pltpu.CompilerParams(
            dimension_semantics=("arbitrary", "arbitrary"), vmem_limit_bytes=VMEM_LIMIT),
    )(x, mem, w["norm_w"], w["w_in"], w["w_gate"], w["w_out"], w["w_mem_kv"],
      consts["cos_p"], consts["s1_p"], consts["s2_p"],
      consts["dec"], consts["qdec"], consts["kdec"], consts["cdec"],
      w["ret_gn"], w["ml_gn"], w["b_i8"], w["b_f8"],
      w["tab"], w["bt"], w["ct"], w["s5_d"], w["w_glu"], w["final_norm_w"], *prev)


SROWS = 16


def _sample_kernel_t(x_ref, normw_ref, win_ref, wg_ref, wout_ref,
                     cos_ref, s1_ref, s2_ref, gam_ref, gam8_ref,
                     retgn_ref, mlgn_ref, bi_ref, bf_ref,
                     abt_ref, bt_ref, ct_ref, dsk_ref, wglu_ref, fnw_ref,
                     m0_ref, n0_ref, x0re_ref, x0im_ref,
                     rets_ref, mlc_ref, ck_ref, cv_ref,
                     y_ref, retn_ref, mlcn_ref, mlnn_ref, mlmn_ref, s5re_ref, s5im_ref,
                     hs_ref, proj_ref, qgt_ref, rkt_ref, rvt_ref, mqt_ref, mkt_ref, mvt_ref,
                     wi_ref, ws_ref, ot_ref, cqt_ref, qa8_ref, xa_ref,
                     *, n_layers, n_blocks):
    layer = pl.program_id(0)
    g = pl.program_id(1)
    nsamp = x_ref.shape[0]
    ones_bd = _ones_matrix()
    avg = _avg_matrix()
    mask8 = (lax.broadcasted_iota(jnp.int32, (8, GW), 0)
             == lax.broadcasted_iota(jnp.int32, (8, GW), 1) // DH)

    def P(blk):
        return proj_ref[:, blk * GW:(blk + 1) * GW]

    @pl.when((layer == 0) & (g == 0))
    def _load_x():
        hs_ref[...] = x_ref[...]

    @pl.when(g == 0)
    def _pre():
        hn = _rms_norm(hs_ref[...], normw_ref[...]).astype(BF16)
        proj_ref[...] = _dot_nt(hn, win_ref[...])
        cosr, s1, s2 = cos_ref[...], s1_ref[...], s2_ref[...]
        qgt_ref[...] = (_rope(P(RQ), cosr, s1, s2) * gam_ref[...]).T
        rkt_ref[...] = _rope(P(RK), cosr, s1, s2).T
        rvt_ref[...] = P(RV).T
        mqt_ref[...] = P(MQ).T
        mkt_ref[...] = P(MK).T
        mvt_ref[...] = P(MV).T
        gt = _dot_nt(wg_ref[...], hn)
        ig = gt[0:8] + bi_ref[...]
        lf = _log_sigmoid(gt[8:16] + bf_ref[...])
        a = lf + m0_ref[...]
        mt = jnp.maximum(a, ig)
        wi_ref[...] = jnp.exp(ig - mt)
        ws_ref[...] = jnp.exp(a - mt)
        mlmn_ref[...] = mt
        ot_ref[...] = jnp.zeros_like(ot_ref)
        qa8_ref[...] = jnp.where(mask8[None], P(AQ)[:, None, :], 0.0).reshape(8 * nsamp, GW)
        but = _dot_tn(bt_ref[...], P(SU).T.astype(BF16))
        are, aim = abt_ref[0:NS, :], abt_ref[NS:2 * NS, :]
        x0r, x0i = x0re_ref[...], x0im_ref[...]
        s5re_ref[...] = are * x0r - aim * x0i + but[0:NS]
        s5im_ref[...] = are * x0i + aim * x0r + but[NS:2 * NS]

    head = g // (GW // SROWS // NH)
    hrow = pl.ds(pl.multiple_of(head * DH, DH), DH)
    gam_row = gam8_ref[pl.ds(head, 1), :]
    ws_row = ws_ref[pl.ds(head, 1), :]
    wi_row = wi_ref[pl.ds(head, 1), :]
    v_slab = rvt_ref[hrow, :]
    q_slab = mqt_ref[hrow, :]
    k_slab = mkt_ref[hrow, :]
    o_acc = jnp.zeros((DH, nsamp), F32)
    for i in range(SROWS):
        r = pl.ds(g * SROWS + i, 1)
        s_t = rets_ref[i]
        o_acc = o_acc + qgt_ref[r, :] * s_t
        retn_ref[i] = gam_row * s_t + rkt_ref[r, :] * v_slab
        c_t = mlc_ref[i]
        cqt_ref[r, :] = jnp.sum(c_t * q_slab, axis=0, keepdims=True)
        mlcn_ref[i] = ws_row * c_t + (wi_row * mvt_ref[r, :]) * k_slab
    ot_ref[hrow, :] = ot_ref[hrow, :] + o_acc

    tiles = [pl.ds(pl.multiple_of((g * SB + i) * 8, 8), 8) for i in range(SB)]
    q8 = [qa8_ref[tiles[i], :].astype(BF16) for i in range(SB)]
    sc = [_dot(q8[i], ck_ref[i].astype(BF16)) for i in range(SB)]
    ps = []
    for i in range(SB):
        e = jnp.exp(sc[i] - jnp.max(sc[i], axis=-1, keepdims=True))
        ps.append((e / jnp.sum(e, axis=-1, keepdims=True)).astype(BF16))
    ta = [_dot_nt(ps[i], cv_ref[i].astype(BF16)) for i in range(SB)]
    xa_ref[pl.ds(pl.multiple_of(g * SB, SB), SB), :] = jnp.concatenate(
        [jnp.sum(jnp.where(mask8, ta[i], 0.0), axis=0, keepdims=True) for i in range(SB)], axis=0)

    @pl.when(g == n_blocks - 1)
    def _post():
        cosr, s1, s2 = cos_ref[...], s1_ref[...], s2_ref[...]
        rq = _rope(P(RQ), cosr, s1, s2)
        rk = _rope(P(RK), cosr, s1, s2)
        ro = _dot_x2(rq * rk, ones_bd) * P(RV) + ot_ref[...].T
        ret_out = _head_norm(ro, retgn_ref[...], avg) * _silu(P(RG))
        mht = []
        for h in range(NH):
            rows = slice(h * DH, (h + 1) * DH)
            wi = wi_ref[h:h + 1, :]
            ws = ws_ref[h:h + 1, :]
            emt = jnp.exp(-mlmn_ref[h:h + 1, :])
            qt, kt, n0 = mqt_ref[rows, :], mkt_ref[rows, :], n0_ref[rows, :]
            s = jnp.sum(qt * kt, axis=0, keepdims=True) * wi
            den = s + ws * jnp.sum(n0 * qt, axis=0, keepdims=True)
            mht.append((s * mvt_ref[rows, :] + ws * cqt_ref[rows, :])
                       / jnp.maximum(jnp.abs(den), emt))
            mlnn_ref[rows, :] = ws * n0 + wi * kt
        mh = jnp.concatenate(mht, axis=0).T * _sigmoid(P(MO))
        ml_out = _head_norm(mh, mlgn_ref[...], avg) * _silu(P(MG))
        su = P(SU)
        xcat = jnp.concatenate([s5re_ref[...], s5im_ref[...]], axis=0).astype(BF16)
        sy = _dot_tn(xcat, ct_ref[...]) + dsk_ref[...] * su
        sy = _gelu_tanh(sy)
        sy = sy * _sigmoid(_dot(sy.astype(BF16), wglu_ref[...]))
        s5_out = sy * _silu(P(SG))
        xa_out = xa_ref[...] * _silu(P(AG))
        mix = jnp.concatenate([ret_out, ml_out, s5_out, xa_out], axis=1).astype(BF16)
        y = hs_ref[...] + _dot(mix, wout_ref[...])
        hs_ref[...] = y

        @pl.when(layer == n_layers - 1)
        def _emit():
            y_ref[...] = _rms_norm(y, fnw_ref[...])


def _sample_layers_t(x, st, w, consts):
    nsamp = x.shape[0]
    depth = w["w_in"].shape[0]
    nb = GW // SROWS
    assert nsamp == nb * SB and nsamp == 128
    once = lambda shape: pl.BlockSpec(shape, lambda l, g: (0,) * len(shape),
                                      pipeline_mode=pl.Buffered(1))
    lyr = lambda shape: pl.BlockSpec((None,) + shape, lambda l, g: (l,) + (0,) * len(shape))
    lyr_out = lyr
    srows = pl.BlockSpec((None, SROWS, DH, nsamp), lambda l, g: (l, g, 0, 0))
    cache = pl.BlockSpec((None, SB, GW, MEM), lambda l, g: (l, g, 0, 0))
    in_specs = [
        once((nsamp, D)), lyr((1, D)), lyr((DP, D)), lyr((NGATE, D)), lyr((D, D)),
        once((1, 128)), once((1, 128)), once((1, 128)), once((1, GW)), once((8, nsamp)),
        lyr((1, GW)), lyr((1, GW)), lyr((8, nsamp)), lyr((8, nsamp)),
        lyr((2 * NS, nsamp)), lyr((GW, 2 * NS)), lyr((2 * NS, GW)),
        lyr((1, GW)), lyr((GW, GW)), once((1, D)),
        lyr((8, nsamp)), lyr((GW, nsamp)), lyr((NS, nsamp)), lyr((NS, nsamp)),
        srows, srows, cache, cache,
    ]
    out_shape = (
        jax.ShapeDtypeStruct((nsamp, D), F32),
        jax.ShapeDtypeStruct((depth, GW, DH, nsamp), F32),
        jax.ShapeDtypeStruct((depth, GW, DH, nsamp), F32),
        jax.ShapeDtypeStruct((depth, GW, nsamp), F32),
        jax.ShapeDtypeStruct((depth, 8, nsamp), F32),
        jax.ShapeDtypeStruct((depth, NS, nsamp), F32),
        jax.ShapeDtypeStruct((depth, NS, nsamp), F32),
    )
    out_specs = (
        pl.BlockSpec((nsamp, D), lambda l, g: (0, 0)), srows, srows,
        lyr_out((GW, nsamp)), lyr_out((8, nsamp)), lyr_out((NS, nsamp)), lyr_out((NS, nsamp)),
    )
    scratch = [
        pltpu.VMEM((nsamp, D), F32),
        pltpu.VMEM((nsamp, DP), F32),
        pltpu.VMEM((GW, nsamp), F32), pltpu.VMEM((GW, nsamp), F32), pltpu.VMEM((GW, nsamp), F32),
        pltpu.VMEM((GW, nsamp), F32), pltpu.VMEM((GW, nsamp), F32), pltpu.VMEM((GW, nsamp), F32),
        pltpu.VMEM((8, nsamp), F32), pltpu.VMEM((8, nsamp), F32),
        pltpu.VMEM((GW, nsamp), F32), pltpu.VMEM((GW, nsamp), F32),
        pltpu.VMEM((8 * nsamp, GW), F32), pltpu.VMEM((nsamp, GW), F32),
    ]
    kern = functools.partial(_sample_kernel_t, n_layers=depth, n_blocks=nb)
    return pl.pallas_call(
        kern, grid=(depth, nb), in_specs=in_specs, out_specs=out_specs, out_shape=out_shape,
        scratch_shapes=scratch, name="sample_layers",
        compiler_params=pltpu.CompilerParams(
            dimension_semantics=("arbitrary", "arbitrary"), vmem_limit_bytes=VMEM_LIMIT),
    )(x, w["norm_w"], w["w_in"], w["w_gate"], w["w_out"],
      consts["cos_s"], consts["s1_s"], consts["s2_s"], consts["gam"], consts["gam8"],
      w["ret_gn"], w["ml_gn"], w["b_i8"], w["b_f8"],
      w["abt"], w["bt"], w["ct"], w["s5_d"], w["w_glu"], w["final_norm_w"],
      st["m"], st["n"], st["s5_re"], st["s5_im"], st["ret"], st["c"], st["mem_k"], st["mem_v"])


def _rope_tables(pos):
    half = DH // 2
    inv = ROPE_BASE ** (-np.arange(half, dtype=np.float64) / half)
    ang = np.asarray(pos, np.float64)[:, None] * inv[None, :]
    cos, sin = np.cos(ang), np.sin(ang)
    zero = np.zeros_like(sin)
    c = np.tile(np.concatenate([cos, cos], axis=-1), (1, 2))
    s1 = np.tile(np.concatenate([zero, sin], axis=-1), (1, 2))
    s2 = np.tile(np.concatenate([-sin, zero], axis=-1), (1, 2))
    return tuple(jnp.asarray(t, F32) for t in (c, s1, s2))


def _constants(seq):
    lg = np.log1p(-np.power(2.0, -5.0 - np.arange(NH, dtype=np.float64)))[:, None]
    idx = np.arange(L, dtype=np.float64)
    diff = idx[:, None] - idx[None, :]
    decay = np.where(diff >= 0, np.exp(lg[:, :, None] * np.maximum(diff, 0.0)), 0.0)
    rep = lambda t: np.repeat(t, DH, axis=0).T
    consts = {
        "dec": np.transpose(decay, (1, 0, 2)).reshape(L, NH * L),
        "qdec": rep(np.exp(lg * (idx + 1.0))),
        "kdec": rep(np.exp(lg * (L - 1.0 - idx))),
        "cdec": rep(np.exp(lg * L)),
        "gam": rep(np.exp(lg * 1.0)),
        "gam8": np.pad(np.broadcast_to(np.exp(lg), (NH, 128)), ((0, 8 - NH), (0, 0))),
    }
    consts = {k: jnp.asarray(v, F32) for k, v in consts.items()}
    consts["cos_p"], consts["s1_p"], consts["s2_p"] = _rope_tables(np.arange(seq))
    consts["cos_s"], consts["s1_s"], consts["s2_s"] = _rope_tables(PAST_LEN + np.arange(1))
    return consts


def _pack_w_in(w_in):
    wt = jnp.swapaxes(w_in, 1, 2)
    sizes = (GW,) * 9 + (NH, NH) + (GW,) * 4
    offs = np.concatenate([[0], np.cumsum(sizes)])
    seg = [wt[:, int(offs[i]):int(offs[i + 1]), :] for i in range(len(sizes))]
    scale = DH ** -0.5
    pad8 = lambda t: jnp.pad(t, ((0, 0), (0, 8 - t.shape[1]), (0, 0)))
    blocks = [seg[11],
              seg[0], seg[1] * scale, seg[2], seg[3],
              seg[4], seg[5] * scale, seg[6], seg[7], seg[8],
              seg[12], seg[13] * scale, seg[14]]
    gates = jnp.concatenate([pad8(seg[9]), pad8(seg[10])], axis=1)
    return jnp.concatenate(blocks, axis=1).astype(BF16), gates.astype(BF16)


def kernel(x_prompt, x_sample, mem_prompt, state_ret, state_mlstm_c, state_mlstm_n, state_mlstm_m,
           state_s5_re, state_s5_im, cache_mem_k, cache_mem_v,
           norm_w, w_in, ret_gn, ml_b_i, ml_b_f, ml_gn,
           s5_a_re, s5_a_im, s5_log_dt, s5_b_re, s5_b_im, s5_c_re, s5_c_im, s5_d, s5_w_glu,
           w_mem_k, w_mem_v, w_out, final_norm_w):
    depth = norm_w.shape[0]
    bp, seq, _ = x_prompt.shape
    bs = x_sample.shape[0]
    consts = _constants(seq)
    abt, bt, ct, tab = _s5_prepare(s5_a_re, s5_a_im, s5_log_dt, s5_b_re, s5_b_im, s5_c_re, s5_c_im)
    rows8 = lambda t: jnp.pad(jnp.broadcast_to(t[:, :, None], (depth, NH, TB)),
                              ((0, 0), (0, 8 - NH), (0, 0)))
    w_main, w_gate = _pack_w_in(w_in)
    w = {
        "norm_w": norm_w[:, None], "w_in": w_main, "w_gate": w_gate, "w_out": w_out.astype(BF16),
        "w_mem_kv": jnp.concatenate([w_mem_k, w_mem_v], axis=-1).astype(BF16),
        "ret_gn": ret_gn[:, None], "ml_gn": ml_gn[:, None],
        "b_i8": rows8(ml_b_i), "b_f8": rows8(ml_b_f),
        "abt": abt, "tab": tab, "bt": bt, "ct": ct,
        "s5_d": s5_d[:, None], "w_glu": s5_w_glu.astype(BF16),
        "final_norm_w": final_norm_w[None],
    }

    hp = x_prompt
    states = ()
    for l in range(depth):
        hp, *states = _prompt_layer(l, hp, mem_prompt, w, consts, l == depth - 1, tuple(states))
    ret_p, mlc_p, mln_p, ms, xr, xi, mk, mv = states
    mlm_p = ms[:, :, :NH, 0]
    s5re_p = xr[:, :, 0].reshape(depth, bp, S5G, S5P)
    s5im_p = xi[:, :, 0].reshape(depth, bp, S5G, S5P)
    memk_p = jnp.transpose(mk.reshape(depth, bp, NH, DH, MEM), (0, 1, 4, 2, 3))
    memv_p = jnp.transpose(mv.reshape(depth, bp, NH, DH, MEM), (0, 1, 4, 2, 3))

    st = {
        "m": jnp.pad(jnp.swapaxes(state_mlstm_m, 1, 2), ((0, 0), (0, 8 - NH), (0, 0))),
        "n": jnp.transpose(state_mlstm_n, (0, 2, 3, 1)).reshape(depth, GW, bs),
        "s5_re": jnp.transpose(state_s5_re, (0, 2, 3, 1)).reshape(depth, NS, bs),
        "s5_im": jnp.transpose(state_s5_im, (0, 2, 3, 1)).reshape(depth, NS, bs),
        "ret": jnp.transpose(state_ret, (0, 2, 3, 4, 1)).reshape(depth, GW, DH, bs),
        "c": jnp.transpose(state_mlstm_c, (0, 2, 3, 4, 1)).reshape(depth, GW, DH, bs),
        "mem_k": jnp.transpose(cache_mem_k, (0, 1, 3, 4, 2)).reshape(depth, bs, GW, MEM),
        "mem_v": jnp.transpose(cache_mem_v, (0, 1, 3, 4, 2)).reshape(depth, bs, GW, MEM),
    }
    hs, rn, cn, nn, mn, sr, si = _sample_layers_t(x_sample.reshape(bs, D), st, w, consts)
    back5 = lambda t: jnp.transpose(t.reshape(depth, NH, DH, DH, bs), (0, 4, 1, 2, 3))
    back4 = lambda t, a, b: jnp.transpose(t.reshape(depth, a, b, bs), (0, 3, 1, 2))
    return (hp, hs.reshape(bs, 1, D),
            ret_p, back5(rn), mlc_p, back5(cn),
            mln_p, back4(nn, NH, DH), mlm_p, jnp.swapaxes(mn[:, :NH], 1, 2),
            s5re_p, back4(sr, S5G, S5P), s5im_p, back4(si, S5G, S5P),
            memk_p, memv_p)
```

```python
import functools
import math

import numpy as np
import jax
import jax.numpy as jnp
from jax import lax
from jax.experimental import pallas as pl
from jax.experimental.pallas import tpu as pltpu

F32 = jnp.float32
BF16 = jnp.bfloat16

D = 1024
GW = 256
NH = 4
DH = 64
L = 128
LS = 8
NTAB = 8
S5G = 16
S5P = 64
S5C = 16
NS = S5G * S5P
MEM = 256
EPS = 1e-6
NEG_INF = -1e30
ROPE_BASE = 10000.0
PAST_LEN = 16384

TB = 512
SB = 8
XQ = 512
NBLK = 13
DP = NBLK * GW
(SU, RQ, RK, RV, RG, MQ, MK, MV, MO, MG, SG, AQ, AG) = range(NBLK)
NGATE = 16

VMEM_LIMIT = 56 * 1024 * 1024


_DONE = object()


def _dot(a, b):
    return jnp.dot(a, b, preferred_element_type=F32)


def _dot_nt(a, b):
    return lax.dot_general(a, b, (((1,), (1,)), ((), ())), preferred_element_type=F32)


def _dot_tn(a, b):
    return lax.dot_general(a, b, (((0,), (0,)), ((), ())), preferred_element_type=F32)


def _split2(x):
    hi = x.astype(BF16)
    lo = (x - hi.astype(F32)).astype(BF16)
    return hi, lo


def _dot_x2(x, w):
    hi, lo = _split2(x)
    return _dot(hi, w) + _dot(lo, w)


def _dot_x3(x, w):
    hi = x.astype(BF16)
    r = x - hi.astype(F32)
    mid = r.astype(BF16)
    lo = (r - mid.astype(F32)).astype(BF16)
    return _dot(hi, w) + _dot(mid, w) + _dot(lo, w)


def _gated(xh, gh):
    return xh + xh * jnp.tanh(gh)


def _silu_half(gh):
    return _gated(gh, gh)


def _log_sigmoid(x):
    return jnp.minimum(x, 0.0) - jnp.log1p(jnp.exp(-jnp.abs(x)))


def _gelu_tanh(x):
    c = math.sqrt(2.0 / math.pi)
    h = 0.5 * x
    return h + h * jnp.tanh(x * (c + (0.044715 * c) * (x * x)))


def _lane_head(n):
    return lax.broadcasted_iota(jnp.int32, (1, n), 1) // DH


def _head_masks():
    lh = _lane_head(GW)
    return [lh == h for h in range(NH)]


def _block_diag_mask():
    r = lax.broadcasted_iota(jnp.int32, (GW, GW), 0) // DH
    c = lax.broadcasted_iota(jnp.int32, (GW, GW), 1) // DH
    return r == c


def _avg_matrix():
    return jnp.where(_block_diag_mask(), 1.0 / DH, 0.0).astype(BF16)


def _ones_matrix():
    return jnp.where(_block_diag_mask(), 1.0, 0.0).astype(BF16)


def _rope(x, cos, s1, s2):
    outs = []
    for j in range(2):
        xs = x[:, j * 128:(j + 1) * 128]
        outs.append(xs * cos + pltpu.roll(xs, 32, 1) * s1 + pltpu.roll(xs, 96, 1) * s2)
    return jnp.concatenate(outs, axis=1)


def _head_norm(x, gain, avg):
    mu = _dot_x2(x, avg)
    d = x - mu
    var = _dot((d * d).astype(BF16), avg)
    return d * lax.rsqrt(var + EPS) * gain


def _rms_norm(x, w):
    ms = jnp.mean(x * x, axis=-1, keepdims=True)
    return x * lax.rsqrt(ms + EPS) * w


def _stack_heads(x, hm):
    zero = jnp.zeros((x.shape[0], 128), x.dtype)
    blocks = []
    for h in range(NH):
        half = slice((h * DH) // 128 * 128, (h * DH) // 128 * 128 + 128)
        kept = jnp.where(hm[h][:, half], x[:, half], 0.0)
        blocks.append(jnp.concatenate([kept, zero] if half.start == 0 else [zero, kept], axis=1))
    return jnp.concatenate(blocks, axis=0)


def _s5_prep_kernel(are_ref, aim_ref, ldt_ref, bre_ref, bim_ref, cre_ref, cim_ref,
                    abt_ref, bt_ref, ct_ref, tab_ref):
    a_re = are_ref[0]
    a_im = aim_ref[0]
    dt = jnp.exp(ldt_ref[0])
    lam_re = a_re * dt
    lam_im = a_im * dt
    mag = jnp.exp(lam_re)
    ab_re = mag * jnp.cos(lam_im)
    ab_im = mag * jnp.sin(lam_im)
    den = a_re * a_re + a_im * a_im
    nr = ab_re - 1.0
    ni = ab_im
    f_re = (nr * a_re + ni * a_im) / den
    f_im = (ni * a_re - nr * a_im) / den
    abt_ref[0, 0:NS, :] = jnp.broadcast_to(ab_re, (128, NS)).T
    abt_ref[0, NS:2 * NS, :] = jnp.broadcast_to(ab_im, (128, NS)).T
    b_re = bre_ref[0]
    b_im = bim_ref[0]
    bb_re = (f_re * b_re - f_im * b_im).astype(BF16)
    bb_im = (f_re * b_im + f_im * b_re).astype(BF16)
    rep_r = (lax.broadcasted_iota(jnp.int32, (GW, S5C), 0) % S5C
             == lax.broadcasted_iota(jnp.int32, (GW, S5C), 1)).astype(BF16)
    in_blk = (lax.broadcasted_iota(jnp.int32, (GW, NS), 0) // S5C
              == lax.broadcasted_iota(jnp.int32, (GW, NS), 1) // S5P)
    bt_ref[0, :, :NS] = jnp.where(in_blk, _dot(rep_r, bb_re), 0.0).astype(BF16)
    bt_ref[0, :, NS:] = jnp.where(in_blk, _dot(rep_r, bb_im), 0.0).astype(BF16)
    rep_c = (lax.broadcasted_iota(jnp.int32, (S5C, GW), 0)
             == lax.broadcasted_iota(jnp.int32, (S5C, GW), 1) % S5C).astype(BF16)
    out_blk = (lax.broadcasted_iota(jnp.int32, (NS, GW), 0) // S5P
               == lax.broadcasted_iota(jnp.int32, (NS, GW), 1) // S5C)
    ct_ref[0, :NS, :] = jnp.where(out_blk, _dot(cre_ref[0].astype(BF16), rep_c), 0.0).astype(BF16)
    ct_ref[0, NS:, :] = jnp.where(out_blk, -_dot(cim_ref[0].astype(BF16), rep_c), 0.0).astype(BF16)
    i = lax.broadcasted_iota(jnp.int32, (LS, NS), 0).astype(F32)
    for slot, k in ((0, -i), (2, i), (4, i + 1.0), (6, LS * (i + 1.0))):
        pmag = jnp.exp(k * lam_re)
        tab_ref[0, slot] = pmag * jnp.cos(k * lam_im)
        tab_ref[0, slot + 1] = pmag * jnp.sin(k * lam_im)


def _s5_prepare(a_re, a_im, log_dt, b_re, b_im, c_re, c_im):
    depth = a_re.shape[0]
    are = a_re.reshape(depth, 1, NS)
    aim = a_im.reshape(depth, 1, NS)
    ldt = jnp.repeat(log_dt, S5P, axis=-1).reshape(depth, 1, NS)
    bre = jnp.transpose(b_re, (0, 3, 1, 2)).reshape(depth, S5C, NS)
    bim = jnp.transpose(b_im, (0, 3, 1, 2)).reshape(depth, S5C, NS)
    cre = jnp.transpose(c_re, (0, 1, 3, 2)).reshape(depth, NS, S5C)
    cim = jnp.transpose(c_im, (0, 1, 3, 2)).reshape(depth, NS, S5C)
    per_layer = lambda *shape: pl.BlockSpec((1,) + shape, lambda l: (l,) + (0,) * len(shape))
    out_shape = (jax.ShapeDtypeStruct((depth, 2 * NS, 128), F32),
                 jax.ShapeDtypeStruct((depth, GW, 2 * NS), BF16),
                 jax.ShapeDtypeStruct((depth, 2 * NS, GW), BF16),
                 jax.ShapeDtypeStruct((depth, NTAB, LS, NS), F32))
    return pl.pallas_call(
        _s5_prep_kernel, grid=(depth,),
        in_specs=[per_layer(1, NS), per_layer(1, NS), per_layer(1, NS),
                  per_layer(S5C, NS), per_layer(S5C, NS), per_layer(NS, S5C), per_layer(NS, S5C)],
        out_specs=(per_layer(2 * NS, 128), per_layer(GW, 2 * NS), per_layer(2 * NS, GW),
                   per_layer(NTAB, LS, NS)),
        out_shape=out_shape, name="s5_prepare",
        compiler_params=pltpu.CompilerParams(dimension_semantics=("arbitrary",)),
    )(are, aim, ldt, bre, bim, cre, cim)


N_STATE_OUT = 8


def _prompt_kernel(x_ref, mem_ref, normw_ref, win_ref, wg_ref, wout_ref, wmkv_ref,
                   cos_ref, s1_ref, s2_ref, dec_ref, qdec_ref, kdec_ref, cdec_ref,
                   retgn_ref, mlgn_ref, bi_ref, bf_ref,
                   tab_ref, bt_ref, ct_ref, dsk_ref, wglu_ref, fnw_ref,
                   *rest, last_layer, n_tblocks, n_prev):
    (y_ref, rets_ref, mlc_ref, mln_ref, mlm_ref, s5re_ref, s5im_ref, memk_ref, memv_ref,
     proj_ref, mix_ref, s_ref, c_ref, n_ref, m_ref, xre_ref, xim_ref,
     mk_ref, mv_ref, bu_ref, xcat_ref, car_ref) = rest[n_prev:]
    t = pl.program_id(1)
    hm = _head_masks()
    bd = _block_diag_mask()
    avg = _avg_matrix()
    lane128 = lax.broadcasted_iota(jnp.int32, (1, 128), 1)
    row_i = lax.broadcasted_iota(jnp.int32, (L, 128), 0)
    col_i = lax.broadcasted_iota(jnp.int32, (L, 128), 1)
    causal = row_i >= col_i
    tri_u = jnp.where(row_i <= col_i, 1.0, 0.0).astype(BF16)
    tri_sub = jnp.where(causal & (row_i // LS == col_i // LS), 1.0, 0.0)
    blk_sum = jnp.where(lax.broadcasted_iota(jnp.int32, (L // LS, 128), 0)
                        == lax.broadcasted_iota(jnp.int32, (L // LS, 128), 1) // LS, 1.0, 0.0)
    tri_ext = jnp.concatenate([tri_sub, blk_sum], axis=0).astype(BF16)

    @pl.when(t == 0)
    def _init():
        s_ref[...] = jnp.zeros_like(s_ref)
        c_ref[...] = jnp.zeros_like(c_ref)
        n_ref[...] = jnp.zeros_like(n_ref)
        m_ref[...] = jnp.zeros_like(m_ref)
        xre_ref[...] = jnp.zeros_like(xre_ref)
        xim_ref[...] = jnp.zeros_like(xim_ref)
        mkv = _dot(mem_ref[0].astype(BF16), wmkv_ref[...])
        mk = mkv[:, :GW]
        mv = mkv[:, GW:]
        memk_ref[0] = mk.T
        memv_ref[0] = mv.T
        mk_ref[...] = _stack_heads(mk.astype(BF16), hm)
        mv_ref[...] = _stack_heads(mv.astype(BF16), hm)

    x = x_ref[0]
    hn = _rms_norm(x, normw_ref[...]).astype(BF16)
    proj_ref[...] = _dot_nt(hn, win_ref[...])
    gates_t = _dot_nt(wg_ref[...], hn)

    nc = TB // L
    crow = [slice(c * L, (c + 1) * L) for c in range(nc)]

    def PB(blk, rows=slice(None)):
        return proj_ref[rows, blk * GW:(blk + 1) * GW]


    def xattn_stages():
        for piece in range(TB // XQ):
            rs = slice(piece * XQ, (piece + 1) * XQ)
            sc = _dot_nt(PB(AQ, rs).astype(BF16), mk_ref[...])
            yield
            ps = []
            for h in range(NH):
                seg = sc[:, h * MEM:(h + 1) * MEM]
                e = jnp.exp(seg - jnp.max(seg, axis=-1, keepdims=True))
                ps.append((e / jnp.sum(e, axis=-1, keepdims=True)).astype(BF16))
            p = jnp.concatenate(ps, axis=1)
            yield
            xa = _dot(p, mv_ref[...])
            yield
            mix_ref[rs, 3 * GW:4 * GW] = (xa * _silu_half(PB(AG, rs))).astype(BF16)
            yield

    def s5_stages():
        lanes = [(slice(j * 128, (j + 1) * 128), slice(NS + j * 128, NS + (j + 1) * 128))
                 for j in range(NS // 128)]
        bpc = L // LS
        nblk = TB // LS
        su = PB(SU)
        bu_ref[...] = _dot(su.astype(BF16), bt_ref[...])
        yield
        for c in range(nc):
            for lre, lim in lanes:
                br = bu_ref[crow[c], lre]
                bi = bu_ref[crow[c], lim]
                wr = jnp.tile(tab_ref[0, :, lre], (bpc, 1))
                wi = jnp.tile(tab_ref[1, :, lre], (bpc, 1))
                xcat_ref[crow[c], lre] = (wr * br - wi * bi).astype(BF16)
                xcat_ref[crow[c], lim] = (wr * bi + wi * br).astype(BF16)
            yield
        for c in range(nc):
            z = _dot(tri_ext, xcat_ref[crow[c], :])
            bu_ref[crow[c], :] = z[0:L]
            car_ref[c * bpc:(c + 1) * bpc, :] = z[L:L + bpc]
            yield
        rowb = lax.broadcasted_iota(jnp.int32, (nblk, 128), 0)
        ers, eis, prs, pis, c0s = [], [], [], [], []
        for lre, lim in lanes:
            zr = car_ref[:, lre]
            zi = car_ref[:, lim]
            pr = tab_ref[2, LS - 1:LS, lre]
            pi = tab_ref[3, LS - 1:LS, lre]
            er = pr * zr - pi * zi
            ei = pr * zi + pi * zr
            pr = tab_ref[4, LS - 1:LS, lre]
            pi = tab_ref[5, LS - 1:LS, lre]
            c0r = xre_ref[0:1, lre]
            c0i = xim_ref[0:1, lre]
            ers.append(er + jnp.where(rowb == 0, pr * c0r - pi * c0i, 0.0))
            eis.append(ei + jnp.where(rowb == 0, pr * c0i + pi * c0r, 0.0))
            prs.append(pr)
            pis.append(pi)
            c0s.append((c0r, c0i))
        yield
        assert LS == 8 and nblk % 8 == 0
        ngrp = nblk // 8
        row8 = lax.broadcasted_iota(jnp.int32, (8, 128), 0)
        ers = [[v[g * 8:(g + 1) * 8] for g in range(ngrp)] for v in ers]
        eis = [[v[g * 8:(g + 1) * 8] for g in range(ngrp)] for v in eis]
        for k in range(3):
            d = 1 << k
            for j in range(len(lanes)):
                pr, pi = prs[j], pis[j]
                for g in range(ngrp):
                    er, ei = ers[j][g], eis[j][g]
                    sr = jnp.where(row8 >= d, pltpu.roll(er, d, 0), 0.0)
                    si = jnp.where(row8 >= d, pltpu.roll(ei, d, 0), 0.0)
                    ers[j][g], eis[j][g] = er + pr * sr - pi * si, ei + pr * si + pi * sr
                prs[j], pis[j] = pr * pr - pi * pi, 2.0 * (pr * pi)
            yield
        for g in range(1, ngrp):
            for j, (lre, lim) in enumerate(lanes):
                qr = tab_ref[6, :, lre]
                qi = tab_ref[7, :, lre]
                cr = jnp.broadcast_to(ers[j][g - 1][7:8, :], (8, 128))
                ci_ = jnp.broadcast_to(eis[j][g - 1][7:8, :], (8, 128))
                ers[j][g] = ers[j][g] + (qr * cr - qi * ci_)
                eis[j][g] = eis[j][g] + (qr * ci_ + qi * cr)
            if g % 3 == 0 or g == ngrp - 1:
                yield
        ers = [jnp.concatenate(v, axis=0) for v in ers]
        eis = [jnp.concatenate(v, axis=0) for v in eis]
        for j, (lre, lim) in enumerate(lanes):
            xre_ref[:, lre] = jnp.broadcast_to(ers[j][nblk - 1:nblk, :], (8, 128))
            xim_ref[:, lre] = jnp.broadcast_to(eis[j][nblk - 1:nblk, :], (8, 128))
            cr = jnp.where(rowb == 0, c0s[j][0], pltpu.roll(ers[j], 1, 0))
            ci_ = jnp.where(rowb == 0, c0s[j][1], pltpu.roll(eis[j], 1, 0))
            ar = tab_ref[4, 0:1, lre]
            ai = tab_ref[5, 0:1, lre]
            car_ref[:, lre] = ar * cr - ai * ci_
            car_ref[:, lim] = ar * ci_ + ai * cr
        yield
        for c in range(nc):
            for lre, lim in lanes:
                cr = jnp.concatenate(
                    [jnp.broadcast_to(car_ref[c * bpc + j:c * bpc + j + 1, lre], (LS, 128))
                     for j in range(bpc)], axis=0)
                ci_ = jnp.concatenate(
                    [jnp.broadcast_to(car_ref[c * bpc + j:c * bpc + j + 1, lim], (LS, 128))
                     for j in range(bpc)], axis=0)
                zr = bu_ref[crow[c], lre] + cr
                zi = bu_ref[crow[c], lim] + ci_
                pr = jnp.tile(tab_ref[2, :, lre], (bpc, 1))
                pi = jnp.tile(tab_ref[3, :, lre], (bpc, 1))
                xcat_ref[crow[c], lre] = (pr * zr - pi * zi).astype(BF16)
                xcat_ref[crow[c], lim] = (pr * zi + pi * zr).astype(BF16)
            yield
        sy = _dot(xcat_ref[...], ct_ref[...]) + dsk_ref[...] * su
        yield
        sy = _gelu_tanh(sy)
        gate = _dot(sy.astype(BF16), wglu_ref[...])
        yield
        mix_ref[:, 2 * GW:3 * GW] = (_gated(0.5 * sy, gate) * _silu_half(PB(SG))).astype(BF16)
        yield

    def retention_stages():
        rq = _rope(PB(RQ), cos_ref[...], s1_ref[...], s2_ref[...])
        rk = _rope(PB(RK), cos_ref[...], s1_ref[...], s2_ref[...])
        rvf = PB(RV)
        rv = rvf.astype(BF16)
        rqb = rq.astype(BF16)
        rkb = rk.astype(BF16)
        kst = [_stack_heads(rkb[crow[c]], hm) for c in range(nc)]
        vst = [_stack_heads(rv[crow[c]], hm) for c in range(nc)]
        yield
        inner = [_dot_nt(rqb[crow[c]], kst[c]) for c in range(nc)]
        yield
        pmat = [(inner[c] * dec_ref[...]).astype(BF16) for c in range(nc)]
        kd = [(rk[crow[c]] * kdec_ref[...]).astype(BF16) for c in range(nc)]
        yield
        rloc = [_dot(pmat[c], vst[c]) for c in range(nc)]
        upd = [_dot_tn(kd[c], rv[crow[c]]) for c in range(nc)]
        yield
        st = [s_ref[...]]
        for c in range(nc):
            st.append(st[c] * cdec_ref[...] + jnp.where(bd, upd[c], 0.0))
        s_ref[...] = st[nc]
        qd = [(rq[crow[c]] * qdec_ref[...]).astype(BF16) for c in range(nc)]
        yield
        ost = [_dot(qd[c], st[c].astype(BF16)) for c in range(nc)]
        yield
        ro = jnp.concatenate([rloc[c] + ost[c] for c in range(nc)], axis=0)
        mix_ref[:, 0:GW] = (_head_norm(ro, retgn_ref[...], avg) * _silu_half(PB(RG))).astype(BF16)
        yield

    def mlstm_stages():
        ig_r = gates_t[0:8] + bi_ref[...]
        lf_r = _log_sigmoid(gates_t[8:16] + bf_ref[...])
        yield
        b_r = [_dot_x3(lf_r[:, crow[c]], tri_u) for c in range(nc)]
        yield
        gd_r = [ig_r[:, crow[c]] - b_r[c] for c in range(nc)]
        lane_r = lax.broadcasted_iota(jnp.int32, (8, L), 1)
        cm_r = list(gd_r)
        for k in range(7):
            d = 1 << k
            cm_r = [jnp.maximum(v, jnp.where(lane_r >= d, pltpu.roll(v, d, 1), NEG_INF))
                    for v in cm_r]
            yield
        m_prev = [m_ref[...]]
        mt_r = []
        for c in range(nc):
            mt = jnp.maximum(b_r[c] + m_prev[c], b_r[c] + cm_r[c])
            mt_r.append(mt)
            m_prev.append(jnp.broadcast_to(mt[:, L - 1:L], (8, L)))
            yield
        m_ref[...] = m_prev[nc]
        cols = []
        for c in range(nc):
            bm = b_r[c] - mt_r[c]
            ws = jnp.exp(b_r[c] + m_prev[c] - mt_r[c])
            wl = jnp.exp(gd_r[c] + jnp.broadcast_to(bm[:, L - 1:L], (8, L)))
            emt = jnp.exp(-mt_r[c])
            cols.append(jnp.concatenate([bm, ws, wl, emt, jnp.zeros((L - 32, L), F32)], axis=0).T)
        mqf = PB(MQ)
        mq = mqf.astype(BF16)
        mkf = PB(MK)
        mvf = PB(MV)
        mv_ = mvf.astype(BF16)
        mkb = mkf.astype(BF16)
        kst = [_stack_heads(mkb[crow[c]], hm) for c in range(nc)]
        vst = [_stack_heads(mv_[crow[c]], hm) for c in range(nc)]
        yield
        sraw = [_dot_nt(mq[crow[c]], kst[c]) for c in range(nc)]
        yield
        smat, den_i = [], []
        for c in range(nc):
            parts, dens = [], []
            for h in range(NH):
                arg = jnp.where(causal, cols[c][:, h:h + 1] + gd_r[c][h:h + 1, :], NEG_INF)
                s_h = sraw[c][:, h * L:(h + 1) * L] * jnp.exp(arg)
                dens.append(jnp.sum(s_h, axis=-1, keepdims=True))
                parts.append(s_h.astype(BF16))
            den_i.append(dens)
            smat.append(jnp.concatenate(parts, axis=1))
            yield
        ones_blk = jnp.ones((L, 128), BF16)
        def per_head(columns):
            first = lane128 < DH
            return jnp.concatenate([jnp.where(first, columns[0], columns[1]),
                                    jnp.where(first, columns[2], columns[3])], axis=1)

        kws = [(mkf[crow[c]] * per_head([cols[c][:, 16 + h:17 + h] for h in range(NH)])
                ).astype(BF16) for c in range(nc)]
        yield
        rloc = [_dot(smat[c], vst[c]) for c in range(nc)]
        u = [_dot_tn(kws[c], jnp.concatenate([mv_[crow[c]], ones_blk], axis=1))
             for c in range(nc)]
        yield
        nmask = (lax.broadcasted_iota(jnp.int32, (GW, 128), 0) // DH
                 == lax.broadcasted_iota(jnp.int32, (GW, 128), 1))
        cst = [c_ref[...]]
        nst = [n_ref[...]]
        for c in range(nc):
            wsl256 = jnp.zeros((1, GW), F32)
            wsl128 = jnp.zeros((1, 128), F32)
            for h in range(NH):
                wsl = cols[c][L - 1:L, 8 + h:9 + h]
                wsl256 = wsl256 + jnp.where(hm[h], wsl, 0.0)
                wsl128 = wsl128 + jnp.where(lane128 == h, wsl, 0.0)
            cst.append(cst[c] * wsl256 + jnp.where(bd, u[c][:, :GW], 0.0))
            nst.append(nst[c] * wsl128 + jnp.where(nmask, u[c][:, GW:], 0.0))
        c_ref[...] = cst[nc]
        n_ref[...] = nst[nc]
        yield
        qc = [_dot(mq[crow[c]], cst[c].astype(BF16)) for c in range(nc)]
        qn = [_dot(mq[crow[c]], nst[c].astype(BF16)) for c in range(nc)]
        yield
        mhs = []
        for c in range(nc):
            wsc = [cols[c][:, 8 + h:9 + h] for h in range(NH)]
            rdn = []
            for h in range(NH):
                den = den_i[c][h] + wsc[h] * qn[c][:, h:h + 1]
                rdn.append(0.5 / jnp.maximum(jnp.abs(den), cols[c][:, 24 + h:25 + h]))
            mhs.append((rloc[c] + per_head(wsc) * qc[c]) * per_head(rdn))
            yield
        mh = _gated(jnp.concatenate(mhs, axis=0), PB(MO))
        mix_ref[:, GW:2 * GW] = (_head_norm(mh, mlgn_ref[...], avg)
                                 * _silu_half(PB(MG))).astype(BF16)
        yield

    pending = [mlstm_stages(), s5_stages(), retention_stages(), xattn_stages()]
    while pending:
        pending = [stage for stage in pending if next(stage, _DONE) is not _DONE]

    y = x_ref[0] + _dot(mix_ref[...], wout_ref[...])
    if last_layer:
        y = _rms_norm(y, fnw_ref[...])
    y_ref[0] = y

    @pl.when(t == n_tblocks - 1)
    def _final():
        s_fin = s_ref[...]
        c_fin = c_ref[...].T
        n_fin = n_ref[...].T
        for h in range(NH):
            blk = slice(h * DH, (h + 1) * DH)
            rets_ref[0, h] = s_fin[blk, blk]
            mlc_ref[0, h] = c_fin[blk, blk]
        mln_ref[0] = jnp.concatenate([n_fin[h:h + 1, h * DH:(h + 1) * DH] for h in range(NH)],
                                     axis=0)
        mlm_ref[0] = m_ref[...]
        s5re_ref[0] = xre_ref[...]
        s5im_ref[0] = xim_ref[...]


def _prompt_layer(layer, x, mem, w, consts, last_layer, prev):
    bsz, seq, _ = x.shape
    depth = w["w_in"].shape[0]
    nt = seq // TB
    full = lambda shape: pl.BlockSpec(shape, lambda b, t: (0,) * len(shape),
                                      pipeline_mode=pl.Buffered(1))
    lyr = lambda shape: pl.BlockSpec((None,) + shape, lambda b, t: (layer,) + (0,) * len(shape),
                                     pipeline_mode=pl.Buffered(1))
    tok = lambda width: pl.BlockSpec((TB, width), lambda b, t: (t, 0))
    per_b = lambda r, c: pl.BlockSpec((1, r, c), lambda b, t: (b, 0, 0))
    in_specs = [
        pl.BlockSpec((1, TB, D), lambda b, t: (b, t, 0)),
        per_b(MEM, D),
        lyr((1, D)), lyr((DP, D)), lyr((NGATE, D)), lyr((D, D)), lyr((D, 2 * GW)),
        tok(128), tok(128), tok(128),
        full((L, NH * L)), full((L, GW)), full((L, GW)), full((1, GW)),
        lyr((1, GW)), lyr((1, GW)), lyr((8, TB)), lyr((8, TB)),
        lyr((NTAB, LS, NS)), lyr((GW, 2 * NS)), lyr((2 * NS, GW)),
        lyr((1, GW)), lyr((GW, GW)), full((1, D)),
    ] + [pl.BlockSpec(memory_space=pl.ANY)] * len(prev)
    n_in = len(in_specs) - len(prev)
    state_shapes = [(NH, DH, DH),
                    (NH, DH, DH),
                    (NH, DH),
                    (8, 128),
                    (8, NS),
                    (8, NS),
                    (GW, MEM),
                    (GW, MEM)]
    assert len(state_shapes) == N_STATE_OUT and len(prev) in (0, N_STATE_OUT)
    out_shape = ((jax.ShapeDtypeStruct((bsz, seq, D), F32),)
                 + tuple(jax.ShapeDtypeStruct((depth, bsz) + s, F32) for s in state_shapes))
    state_spec = lambda s: pl.BlockSpec((None, 1) + s,
                                        lambda b, t: (layer, b) + (0,) * len(s))
    out_specs = ((pl.BlockSpec((1, TB, D), lambda b, t: (b, t, 0)),)
                 + tuple(state_spec(s) for s in state_shapes))
    scratch = [
        pltpu.VMEM((TB, DP), F32),
        pltpu.VMEM((TB, D), BF16),
        pltpu.VMEM((GW, GW), F32),
        pltpu.VMEM((GW, GW), F32),
        pltpu.VMEM((GW, 128), F32),
        pltpu.VMEM((8, 128), F32),
        pltpu.VMEM((8, NS), F32),
        pltpu.VMEM((8, NS), F32),
        pltpu.VMEM((NH * MEM, GW), BF16),
        pltpu.VMEM((NH * MEM, GW), BF16),
        pltpu.VMEM((TB, 2 * NS), F32),
        pltpu.VMEM((TB, 2 * NS), BF16),
        pltpu.VMEM((TB // LS, 2 * NS), F32),
    ]
    kern = functools.partial(_prompt_kernel, last_layer=last_layer, n_tblocks=nt, n_prev=len(prev))
    return pl.pallas_call(
        kern, grid=(bsz, nt), in_specs=in_specs, out_specs=out_specs, out_shape=out_shape,
        scratch_shapes=scratch, name="prompt_layer",
        input_output_aliases={n_in + i: 1 + i for i in range(len(prev))},
        compiler_params=pltpu.CompilerParams(
            dimension_semantics=("arbitrary", "arbitrary"), vmem_limit_bytes=VMEM_LIMIT),
    )(x, mem, w["norm_w"], w["w_in"], w["w_gate"], w["w_out"], w["w_mem_kv"],
      consts["cos_p"], consts["s1_p"], consts["s2_p"],
      consts["dec"], consts["qdec"], consts["kdec"], consts["cdec"],
      w["ret_gn"], w["ml_gn"], w["b_i8"], w["b_f8"],
      w["tab"], w["bt"], w["ct"], w["s5_d"], w["w_glu"], w["final_norm_w"], *prev)


SROWS = 16


def _sample_kernel_t(x_ref, normw_ref, win_ref, wg_ref, wout_ref,
                     cos_ref, s1_ref, s2_ref, gam_ref, gam8_ref,
                     retgn_ref, mlgn_ref, bi_ref, bf_ref,
                     abt_ref, bt_ref, ct_ref, dsk_ref, wglu_ref, fnw_ref,
                     m0_ref, n0_ref, x0re_ref, x0im_ref,
                     rets_ref, mlc_ref, ck_ref, cv_ref,
                     y_ref, retn_ref, mlcn_ref, mlnn_ref, mlmn_ref, s5re_ref, s5im_ref,
                     hs_ref, proj_ref, qgt_ref, rkt_ref, rvt_ref, mqt_ref, mkt_ref, mvt_ref,
                     wi_ref, ws_ref, ot_ref, cqt_ref, qa8_ref, xa_ref,
                     *, n_layers, n_blocks):
    layer = pl.program_id(0)
    g = pl.program_id(1)
    nsamp = x_ref.shape[0]
    ones_bd = _ones_matrix()
    avg = _avg_matrix()
    mask8 = (lax.broadcasted_iota(jnp.int32, (8, GW), 0)
             == lax.broadcasted_iota(jnp.int32, (8, GW), 1) // DH)

    def P(blk):
        return proj_ref[:, blk * GW:(blk + 1) * GW]

    @pl.when((layer == 0) & (g == 0))
    def _load_x():
        hs_ref[...] = x_ref[...]

    @pl.when(g == 0)
    def _pre():
        hn = _rms_norm(hs_ref[...], normw_ref[...]).astype(BF16)
        proj_ref[...] = _dot_nt(hn, win_ref[...])
        cosr, s1, s2 = cos_ref[...], s1_ref[...], s2_ref[...]
        qgt_ref[...] = (_rope(P(RQ), cosr, s1, s2) * gam_ref[...]).T
        rkt_ref[...] = _rope(P(RK), cosr, s1, s2).T
        rvt_ref[...] = P(RV).T
        mqt_ref[...] = P(MQ).T
        mkt_ref[...] = P(MK).T
        mvt_ref[...] = P(MV).T
        gt = _dot_nt(wg_ref[...], hn)
        ig = gt[0:8] + bi_ref[...]
        lf = _log_sigmoid(gt[8:16] + bf_ref[...])
        a = lf + m0_ref[...]
        mt = jnp.maximum(a, ig)
        wi_ref[...] = jnp.exp(ig - mt)
        ws_ref[...] = jnp.exp(a - mt)
        mlmn_ref[...] = mt
        ot_ref[...] = jnp.zeros_like(ot_ref)
        qa8_ref[...] = jnp.where(mask8[None], P(AQ)[:, None, :], 0.0).reshape(8 * nsamp, GW)
        but = _dot_tn(bt_ref[...], P(SU).T.astype(BF16))
        are, aim = abt_ref[0:NS, :], abt_ref[NS:2 * NS, :]
        x0r, x0i = x0re_ref[...], x0im_ref[...]
        s5re_ref[...] = are * x0r - aim * x0i + but[0:NS]
        s5im_ref[...] = are * x0i + aim * x0r + but[NS:2 * NS]

    head = g // (GW // SROWS // NH)
    hrow = pl.ds(pl.multiple_of(head * DH, DH), DH)
    gam_row = gam8_ref[pl.ds(head, 1), :]
    ws_row = ws_ref[pl.ds(head, 1), :]
    wi_row = wi_ref[pl.ds(head, 1), :]
    v_slab = rvt_ref[hrow, :]
    q_slab = mqt_ref[hrow, :]
    k_slab = mkt_ref[hrow, :]
    o_acc = jnp.zeros((DH, nsamp), F32)
    for i in range(SROWS):
        r = pl.ds(g * SROWS + i, 1)
        s_t = rets_ref[i]
        o_acc = o_acc + qgt_ref[r, :] * s_t
        retn_ref[i] = gam_row * s_t + rkt_ref[r, :] * v_slab
        c_t = mlc_ref[i]
        cqt_ref[r, :] = jnp.sum(c_t * q_slab, axis=0, keepdims=True)
        mlcn_ref[i] = ws_row * c_t + (wi_row * mvt_ref[r, :]) * k_slab
    ot_ref[hrow, :] = ot_ref[hrow, :] + o_acc

    tiles = [pl.ds(pl.multiple_of((g * SB + i) * 8, 8), 8) for i in range(SB)]
    q8 = [qa8_ref[tiles[i], :].astype(BF16) for i in range(SB)]
    sc = [_dot(q8[i], ck_ref[i].astype(BF16)) for i in range(SB)]
    ps = []
    for i in range(SB):
        e = jnp.exp(sc[i] - jnp.max(sc[i], axis=-1, keepdims=True))
        ps.append((e / jnp.sum(e, axis=-1, keepdims=True)).astype(BF16))
    ta = [_dot_nt(ps[i], cv_ref[i].astype(BF16)) for i in range(SB)]
    xa_ref[pl.ds(pl.multiple_of(g * SB, SB), SB), :] = jnp.concatenate(
        [jnp.sum(jnp.where(mask8, ta[i], 0.0), axis=0, keepdims=True) for i in range(SB)], axis=0)

    @pl.when(g == n_blocks - 1)
    def _post():
        cosr, s1, s2 = cos_ref[...], s1_ref[...], s2_ref[...]
        rq = _rope(P(RQ), cosr, s1, s2)
        rk = _rope(P(RK), cosr, s1, s2)
        ro = _dot_x2(rq * rk, ones_bd) * P(RV) + ot_ref[...].T
        ret_out = _head_norm(ro, retgn_ref[...], avg) * _silu_half(P(RG))
        mht = []
        for h in range(NH):
            rows = slice(h * DH, (h + 1) * DH)
            wi = wi_ref[h:h + 1, :]
            ws = ws_ref[h:h + 1, :]
            emt = jnp.exp(-mlmn_ref[h:h + 1, :])
            qt, kt, n0 = mqt_ref[rows, :], mkt_ref[rows, :], n0_ref[rows, :]
            s = jnp.sum(qt * kt, axis=0, keepdims=True) * wi
            den = s + ws * jnp.sum(n0 * qt, axis=0, keepdims=True)
            mht.append((s * mvt_ref[rows, :] + ws * cqt_ref[rows, :])
                       / jnp.maximum(jnp.abs(den), emt))
            mlnn_ref[rows, :] = ws * n0 + wi * kt
        mh = _gated(0.5 * jnp.concatenate(mht, axis=0).T, P(MO))
        ml_out = _head_norm(mh, mlgn_ref[...], avg) * _silu_half(P(MG))
        su = P(SU)
        xcat = jnp.concatenate([s5re_ref[...], s5im_ref[...]], axis=0).astype(BF16)
        sy = _dot_tn(xcat, ct_ref[...]) + dsk_ref[...] * su
        sy = _gelu_tanh(sy)
        sy = _gated(0.5 * sy, _dot(sy.astype(BF16), wglu_ref[...]))
        s5_out = sy * _silu_half(P(SG))
        xa_out = xa_ref[...] * _silu_half(P(AG))
        mix = jnp.concatenate([ret_out, ml_out, s5_out, xa_out], axis=1).astype(BF16)
        y = hs_ref[...] + _dot(mix, wout_ref[...])
        hs_ref[...] = y

        @pl.when(layer == n_layers - 1)
        def _emit():
            y_ref[...] = _rms_norm(y, fnw_ref[...])


def _sample_layers_t(x, st, w, consts):
    nsamp = x.shape[0]
    depth = w["w_in"].shape[0]
    nb = GW // SROWS
    assert nsamp == nb * SB and nsamp == 128
    once = lambda shape: pl.BlockSpec(shape, lambda l, g: (0,) * len(shape),
                                      pipeline_mode=pl.Buffered(1))
    lyr = lambda shape: pl.BlockSpec((None,) + shape, lambda l, g: (l,) + (0,) * len(shape))
    lyr_out = lyr
    srows = pl.BlockSpec((None, SROWS, DH, nsamp), lambda l, g: (l, g, 0, 0))
    cache = pl.BlockSpec((None, SB, GW, MEM), lambda l, g: (l, g, 0, 0))
    in_specs = [
        once((nsamp, D)), lyr((1, D)), lyr((DP, D)), lyr((NGATE, D)), lyr((D, D)),
        once((1, 128)), once((1, 128)), once((1, 128)), once((1, GW)), once((8, nsamp)),
        lyr((1, GW)), lyr((1, GW)), lyr((8, nsamp)), lyr((8, nsamp)),
        lyr((2 * NS, nsamp)), lyr((GW, 2 * NS)), lyr((2 * NS, GW)),
        lyr((1, GW)), lyr((GW, GW)), once((1, D)),
        lyr((8, nsamp)), lyr((GW, nsamp)), lyr((NS, nsamp)), lyr((NS, nsamp)),
        srows, srows, cache, cache,
    ]
    out_shape = (
        jax.ShapeDtypeStruct((nsamp, D), F32),
        jax.ShapeDtypeStruct((depth, GW, DH, nsamp), F32),
        jax.ShapeDtypeStruct((depth, GW, DH, nsamp), F32),
        jax.ShapeDtypeStruct((depth, GW, nsamp), F32),
        jax.ShapeDtypeStruct((depth, 8, nsamp), F32),
        jax.ShapeDtypeStruct((depth, NS, nsamp), F32),
        jax.ShapeDtypeStruct((depth, NS, nsamp), F32),
    )
    out_specs = (
        pl.BlockSpec((nsamp, D), lambda l, g: (0, 0)), srows, srows,
        lyr_out((GW, nsamp)), lyr_out((8, nsamp)), lyr_out((NS, nsamp)), lyr_out((NS, nsamp)),
    )
    scratch = [
        pltpu.VMEM((nsamp, D), F32),
        pltpu.VMEM((nsamp, DP), F32),
        pltpu.VMEM((GW, nsamp), F32), pltpu.VMEM((GW, nsamp), F32), pltpu.VMEM((GW, nsamp), F32),
        pltpu.VMEM((GW, nsamp), F32), pltpu.VMEM((GW, nsamp), F32), pltpu.VMEM((GW, nsamp), F32),
        pltpu.VMEM((8, nsamp), F32), pltpu.VMEM((8, nsamp), F32),
        pltpu.VMEM((GW, nsamp), F32), pltpu.VMEM((GW, nsamp), F32),
        pltpu.VMEM((8 * nsamp, GW), F32), pltpu.VMEM((nsamp, GW), F32),
    ]
    kern = functools.partial(_sample_kernel_t, n_layers=depth, n_blocks=nb)
    return pl.pallas_call(
        kern, grid=(depth, nb), in_specs=in_specs, out_specs=out_specs, out_shape=out_shape,
        scratch_shapes=scratch, name="sample_layers",
        compiler_params=pltpu.CompilerParams(
            dimension_semantics=("arbitrary", "arbitrary"), vmem_limit_bytes=VMEM_LIMIT),
    )(x, w["norm_w"], w["w_in"], w["w_gate"], w["w_out"],
      consts["cos_s"], consts["s1_s"], consts["s2_s"], consts["gam"], consts["gam8"],
      w["ret_gn"], w["ml_gn"], w["b_i8"], w["b_f8"],
      w["abt"], w["bt"], w["ct"], w["s5_d"], w["w_glu"], w["final_norm_w"],
      st["m"], st["n"], st["s5_re"], st["s5_im"], st["ret"], st["c"], st["mem_k"], st["mem_v"])


def _rope_tables(pos):
    half = DH // 2
    inv = ROPE_BASE ** (-np.arange(half, dtype=np.float64) / half)
    ang = np.asarray(pos, np.float64)[:, None] * inv[None, :]
    cos, sin = np.cos(ang), np.sin(ang)
    zero = np.zeros_like(sin)
    c = np.tile(np.concatenate([cos, cos], axis=-1), (1, 2))
    s1 = np.tile(np.concatenate([zero, sin], axis=-1), (1, 2))
    s2 = np.tile(np.concatenate([-sin, zero], axis=-1), (1, 2))
    return tuple(jnp.asarray(t, F32) for t in (c, s1, s2))


def _constants(seq):
    lg = np.log1p(-np.power(2.0, -5.0 - np.arange(NH, dtype=np.float64)))[:, None]
    idx = np.arange(L, dtype=np.float64)
    diff = idx[:, None] - idx[None, :]
    decay = np.where(diff >= 0, np.exp(lg[:, :, None] * np.maximum(diff, 0.0)), 0.0)
    rep = lambda t: np.repeat(t, DH, axis=0).T
    consts = {
        "dec": np.transpose(decay, (1, 0, 2)).reshape(L, NH * L),
        "qdec": rep(np.exp(lg * (idx + 1.0))),
        "kdec": rep(np.exp(lg * (L - 1.0 - idx))),
        "cdec": rep(np.exp(lg * L)),
        "gam": rep(np.exp(lg * 1.0)),
        "gam8": np.pad(np.broadcast_to(np.exp(lg), (NH, 128)), ((0, 8 - NH), (0, 0))),
    }
    consts = {k: jnp.asarray(v, F32) for k, v in consts.items()}
    consts["cos_p"], consts["s1_p"], consts["s2_p"] = _rope_tables(np.arange(seq))
    consts["cos_s"], consts["s1_s"], consts["s2_s"] = _rope_tables(PAST_LEN + np.arange(1))
    return consts


def _pack_w_in(w_in):
    wt = jnp.swapaxes(w_in, 1, 2)
    sizes = (GW,) * 9 + (NH, NH) + (GW,) * 4
    offs = np.concatenate([[0], np.cumsum(sizes)])
    seg = [wt[:, int(offs[i]):int(offs[i + 1]), :] for i in range(len(sizes))]
    scale = DH ** -0.5
    pad8 = lambda t: jnp.pad(t, ((0, 0), (0, 8 - t.shape[1]), (0, 0)))
    blocks = [seg[11],
              seg[0], seg[1] * scale, seg[2], seg[3] * 0.5,
              seg[4], seg[5] * scale, seg[6], seg[7] * 0.5, seg[8] * 0.5,
              seg[12] * 0.5, seg[13] * scale, seg[14] * 0.5]
    gates = jnp.concatenate([pad8(seg[9]), pad8(seg[10])], axis=1)
    return jnp.concatenate(blocks, axis=1).astype(BF16), gates.astype(BF16)


def kernel(x_prompt, x_sample, mem_prompt, state_ret, state_mlstm_c, state_mlstm_n, state_mlstm_m,
           state_s5_re, state_s5_im, cache_mem_k, cache_mem_v,
           norm_w, w_in, ret_gn, ml_b_i, ml_b_f, ml_gn,
           s5_a_re, s5_a_im, s5_log_dt, s5_b_re, s5_b_im, s5_c_re, s5_c_im, s5_d, s5_w_glu,
           w_mem_k, w_mem_v, w_out, final_norm_w):
    depth = norm_w.shape[0]
    bp, seq, _ = x_prompt.shape
    bs = x_sample.shape[0]
    consts = _constants(seq)
    abt, bt, ct, tab = _s5_prepare(s5_a_re, s5_a_im, s5_log_dt, s5_b_re, s5_b_im, s5_c_re, s5_c_im)
    rows8 = lambda t: jnp.pad(jnp.broadcast_to(t[:, :, None], (depth, NH, TB)),
                              ((0, 0), (0, 8 - NH), (0, 0)))
    w_main, w_gate = _pack_w_in(w_in)
    w = {
        "norm_w": norm_w[:, None], "w_in": w_main, "w_gate": w_gate, "w_out": w_out.astype(BF16),
        "w_mem_kv": jnp.concatenate([w_mem_k, w_mem_v], axis=-1).astype(BF16),
        "ret_gn": ret_gn[:, None], "ml_gn": ml_gn[:, None],
        "b_i8": rows8(ml_b_i), "b_f8": rows8(ml_b_f),
        "abt": abt, "tab": tab, "bt": bt, "ct": ct,
        "s5_d": s5_d[:, None], "w_glu": (0.5 * s5_w_glu).astype(BF16),
        "final_norm_w": final_norm_w[None],
    }

    hp = x_prompt
    states = ()
    for l in range(depth):
        hp, *states = _prompt_layer(l, hp, mem_prompt, w, consts, l == depth - 1, tuple(states))
    ret_p, mlc_p, mln_p, ms, xr, xi, mk, mv = states
    mlm_p = ms[:, :, :NH, 0]
    s5re_p = xr[:, :, 0].reshape(depth, bp, S5G, S5P)
    s5im_p = xi[:, :, 0].reshape(depth, bp, S5G, S5P)
    memk_p = jnp.transpose(mk.reshape(depth, bp, NH, DH, MEM), (0, 1, 4, 2, 3))
    memv_p = jnp.transpose(mv.reshape(depth, bp, NH, DH, MEM), (0, 1, 4, 2, 3))

    st = {
        "m": jnp.pad(jnp.swapaxes(state_mlstm_m, 1, 2), ((0, 0), (0, 8 - NH), (0, 0))),
        "n": jnp.transpose(state_mlstm_n, (0, 2, 3, 1)).reshape(depth, GW, bs),
        "s5_re": jnp.transpose(state_s5_re, (0, 2, 3, 1)).reshape(depth, NS, bs),
        "s5_im": jnp.transpose(state_s5_im, (0, 2, 3, 1)).reshape(depth, NS, bs),
        "ret": jnp.transpose(state_ret, (0, 2, 3, 4, 1)).reshape(depth, GW, DH, bs),
        "c": jnp.transpose(state_mlstm_c, (0, 2, 3, 4, 1)).reshape(depth, GW, DH, bs),
        "mem_k": jnp.transpose(cache_mem_k, (0, 1, 3, 4, 2)).reshape(depth, bs, GW, MEM),
        "mem_v": jnp.transpose(cache_mem_v, (0, 1, 3, 4, 2)).reshape(depth, bs, GW, MEM),
    }
    hs, rn, cn, nn, mn, sr, si = _sample_layers_t(x_sample.reshape(bs, D), st, w, consts)
    back5 = lambda t: jnp.transpose(t.reshape(depth, NH, DH, DH, bs), (0, 4, 1, 2, 3))
    back4 = lambda t, a, b: jnp.transpose(t.reshape(depth, a, b, bs), (0, 3, 1, 2))
    return (hp, hs.reshape(bs, 1, D),
            ret_p, back5(rn), mlc_p, back5(cn),
            mln_p, back4(nn, NH, DH), mlm_p, jnp.swapaxes(mn[:, :NH], 1, 2),
            s5re_p, back4(sr, S5G, S5P), s5im_p, back4(si, S5G, S5P),
            memk_p, memv_p)
```

```python
import functools
import math

import numpy as np
import jax
import jax.numpy as jnp
from jax import lax
from jax.experimental import pallas as pl
from jax.experimental.pallas import tpu as pltpu

F32 = jnp.float32
BF16 = jnp.bfloat16

D = 1024
GW = 256
NH = 4
DH = 64
L = 128
LS = 8
NTAB = 8
S5G = 16
S5P = 64
S5C = 16
NS = S5G * S5P
MEM = 256
EPS = 1e-6
NEG_INF = -1e30
ROPE_BASE = 10000.0
PAST_LEN = 16384

TB = 512
SB = 8
XQ = 512
NBLK = 13
DP = NBLK * GW
(RQ, RK, RV, RG, MQ, MK, MV, MO, MG, SU, SG, AQ, AG) = range(NBLK)
NBLK_HEAD = 9
NBLK_TAIL = NBLK - NBLK_HEAD
NGATE = 16

VMEM_LIMIT = 56 * 1024 * 1024


_DONE = object()


def _dot(a, b):
    return jnp.dot(a, b, preferred_element_type=F32)


def _dot_nt(a, b):
    return lax.dot_general(a, b, (((1,), (1,)), ((), ())), preferred_element_type=F32)


def _dot_tn(a, b):
    return lax.dot_general(a, b, (((0,), (0,)), ((), ())), preferred_element_type=F32)


def _split2(x):
    hi = x.astype(BF16)
    lo = (x - hi.astype(F32)).astype(BF16)
    return hi, lo


def _dot_x2(x, w):
    hi, lo = _split2(x)
    return _dot(hi, w) + _dot(lo, w)


def _dot_x3(x, w):
    hi = x.astype(BF16)
    r = x - hi.astype(F32)
    mid = r.astype(BF16)
    lo = (r - mid.astype(F32)).astype(BF16)
    return _dot(hi, w) + _dot(mid, w) + _dot(lo, w)


def _sigmoid(x):
    return 0.5 * (1.0 + jnp.tanh(0.5 * x))


def _silu(x):
    h = 0.5 * x
    return h + h * jnp.tanh(h)


def _log_sigmoid(x):
    return jnp.minimum(x, 0.0) - jnp.log1p(jnp.exp(-jnp.abs(x)))


def _gelu_tanh(x):
    c = math.sqrt(2.0 / math.pi)
    h = 0.5 * x
    return h + h * jnp.tanh(x * (c + (0.044715 * c) * (x * x)))


def _lane_head(n):
    return lax.broadcasted_iota(jnp.int32, (1, n), 1) // DH


def _head_masks():
    lh = _lane_head(GW)
    return [lh == h for h in range(NH)]


def _block_diag_mask():
    r = lax.broadcasted_iota(jnp.int32, (GW, GW), 0) // DH
    c = lax.broadcasted_iota(jnp.int32, (GW, GW), 1) // DH
    return r == c


def _avg_matrix():
    return jnp.where(_block_diag_mask(), 1.0 / DH, 0.0).astype(BF16)


def _ones_matrix():
    return jnp.where(_block_diag_mask(), 1.0, 0.0).astype(BF16)


def _rope(x, cos, s1, s2):
    outs = []
    for j in range(2):
        xs = x[:, j * 128:(j + 1) * 128]
        outs.append(xs * cos + pltpu.roll(xs, 32, 1) * s1 + pltpu.roll(xs, 96, 1) * s2)
    return jnp.concatenate(outs, axis=1)


def _head_norm(x, gain, avg):
    mu = _dot_x2(x, avg)
    d = x - mu
    var = _dot((d * d).astype(BF16), avg)
    return d * lax.rsqrt(var + EPS) * gain


def _rms_norm(x, w):
    ms = jnp.mean(x * x, axis=-1, keepdims=True)
    return x * lax.rsqrt(ms + EPS) * w


def _stack_heads(x, hm):
    zero = jnp.zeros((x.shape[0], 128), x.dtype)
    blocks = []
    for h in range(NH):
        half = slice((h * DH) // 128 * 128, (h * DH) // 128 * 128 + 128)
        kept = jnp.where(hm[h][:, half], x[:, half], 0.0)
        blocks.append(jnp.concatenate([kept, zero] if half.start == 0 else [zero, kept], axis=1))
    return jnp.concatenate(blocks, axis=0)


def _s5_prep_kernel(are_ref, aim_ref, ldt_ref, bre_ref, bim_ref, cre_ref, cim_ref,
                    abt_ref, bt_ref, ct_ref, tab_ref):
    a_re = are_ref[0]
    a_im = aim_ref[0]
    dt = jnp.exp(ldt_ref[0])
    lam_re = a_re * dt
    lam_im = a_im * dt
    mag = jnp.exp(lam_re)
    ab_re = mag * jnp.cos(lam_im)
    ab_im = mag * jnp.sin(lam_im)
    den = a_re * a_re + a_im * a_im
    nr = ab_re - 1.0
    ni = ab_im
    f_re = (nr * a_re + ni * a_im) / den
    f_im = (ni * a_re - nr * a_im) / den
    abt_ref[0, 0:NS, :] = jnp.broadcast_to(ab_re, (128, NS)).T
    abt_ref[0, NS:2 * NS, :] = jnp.broadcast_to(ab_im, (128, NS)).T
    b_re = bre_ref[0]
    b_im = bim_ref[0]
    bb_re = (f_re * b_re - f_im * b_im).astype(BF16)
    bb_im = (f_re * b_im + f_im * b_re).astype(BF16)
    rep_r = (lax.broadcasted_iota(jnp.int32, (GW, S5C), 0) % S5C
             == lax.broadcasted_iota(jnp.int32, (GW, S5C), 1)).astype(BF16)
    in_blk = (lax.broadcasted_iota(jnp.int32, (GW, NS), 0) // S5C
              == lax.broadcasted_iota(jnp.int32, (GW, NS), 1) // S5P)
    bt_ref[0, :, :NS] = jnp.where(in_blk, _dot(rep_r, bb_re), 0.0).astype(BF16)
    bt_ref[0, :, NS:] = jnp.where(in_blk, _dot(rep_r, bb_im), 0.0).astype(BF16)
    rep_c = (lax.broadcasted_iota(jnp.int32, (S5C, GW), 0)
             == lax.broadcasted_iota(jnp.int32, (S5C, GW), 1) % S5C).astype(BF16)
    out_blk = (lax.broadcasted_iota(jnp.int32, (NS, GW), 0) // S5P
               == lax.broadcasted_iota(jnp.int32, (NS, GW), 1) // S5C)
    ct_ref[0, :NS, :] = jnp.where(out_blk, _dot(cre_ref[0].astype(BF16), rep_c), 0.0).astype(BF16)
    ct_ref[0, NS:, :] = jnp.where(out_blk, -_dot(cim_ref[0].astype(BF16), rep_c), 0.0).astype(BF16)
    i = lax.broadcasted_iota(jnp.int32, (LS, NS), 0).astype(F32)
    for slot, k in ((0, -i), (2, i), (4, i + 1.0), (6, LS * (i + 1.0))):
        pmag = jnp.exp(k * lam_re)
        tab_ref[0, slot] = pmag * jnp.cos(k * lam_im)
        tab_ref[0, slot + 1] = pmag * jnp.sin(k * lam_im)


def _s5_prepare(a_re, a_im, log_dt, b_re, b_im, c_re, c_im):
    depth = a_re.shape[0]
    are = a_re.reshape(depth, 1, NS)
    aim = a_im.reshape(depth, 1, NS)
    ldt = jnp.repeat(log_dt, S5P, axis=-1).reshape(depth, 1, NS)
    bre = jnp.transpose(b_re, (0, 3, 1, 2)).reshape(depth, S5C, NS)
    bim = jnp.transpose(b_im, (0, 3, 1, 2)).reshape(depth, S5C, NS)
    cre = jnp.transpose(c_re, (0, 1, 3, 2)).reshape(depth, NS, S5C)
    cim = jnp.transpose(c_im, (0, 1, 3, 2)).reshape(depth, NS, S5C)
    per_layer = lambda *shape: pl.BlockSpec((1,) + shape, lambda l: (l,) + (0,) * len(shape))
    out_shape = (jax.ShapeDtypeStruct((depth, 2 * NS, 128), F32),
                 jax.ShapeDtypeStruct((depth, GW, 2 * NS), BF16),
                 jax.ShapeDtypeStruct((depth, 2 * NS, GW), BF16),
                 jax.ShapeDtypeStruct((depth, NTAB, LS, NS), F32))
    return pl.pallas_call(
        _s5_prep_kernel, grid=(depth,),
        in_specs=[per_layer(1, NS), per_layer(1, NS), per_layer(1, NS),
                  per_layer(S5C, NS), per_layer(S5C, NS), per_layer(NS, S5C), per_layer(NS, S5C)],
        out_specs=(per_layer(2 * NS, 128), per_layer(GW, 2 * NS), per_layer(2 * NS, GW),
                   per_layer(NTAB, LS, NS)),
        out_shape=out_shape, name="s5_prepare",
        compiler_params=pltpu.CompilerParams(dimension_semantics=("arbitrary",)),
    )(are, aim, ldt, bre, bim, cre, cim)


N_STATE_OUT = 8


def _prompt_kernel(x_ref, mem_ref, normw_ref, winh_ref, wint_ref, wg_ref, wout_ref, wmkv_ref,
                   cos_ref, s1_ref, s2_ref, dec_ref, qdec_ref, kdec_ref, cdec_ref,
                   retgn_ref, mlgn_ref, bi_ref, bf_ref,
                   tab_ref, bt_ref, ct_ref, dsk_ref, wglu_ref, fnw_ref,
                   *rest, last_layer, n_tblocks, n_prev):
    (y_ref, rets_ref, mlc_ref, mln_ref, mlm_ref, s5re_ref, s5im_ref, memk_ref, memv_ref,
     proj_ref, mix_ref, s_ref, c_ref, n_ref, m_ref, xre_ref, xim_ref,
     mk_ref, mv_ref, bu_ref, xcat_ref, car_ref) = rest[n_prev:]
    t = pl.program_id(1)
    hm = _head_masks()
    bd = _block_diag_mask()
    avg = _avg_matrix()
    lane128 = lax.broadcasted_iota(jnp.int32, (1, 128), 1)
    row_i = lax.broadcasted_iota(jnp.int32, (L, 128), 0)
    col_i = lax.broadcasted_iota(jnp.int32, (L, 128), 1)
    causal = row_i >= col_i
    tri_u = jnp.where(row_i <= col_i, 1.0, 0.0).astype(BF16)
    tri_sub = jnp.where(causal & (row_i // LS == col_i // LS), 1.0, 0.0)
    blk_sum = jnp.where(lax.broadcasted_iota(jnp.int32, (L // LS, 128), 0)
                        == lax.broadcasted_iota(jnp.int32, (L // LS, 128), 1) // LS, 1.0, 0.0)
    tri_ext = jnp.concatenate([tri_sub, blk_sum], axis=0).astype(BF16)

    @pl.when(t == 0)
    def _init():
        s_ref[...] = jnp.zeros_like(s_ref)
        c_ref[...] = jnp.zeros_like(c_ref)
        n_ref[...] = jnp.zeros_like(n_ref)
        m_ref[...] = jnp.zeros_like(m_ref)
        xre_ref[...] = jnp.zeros_like(xre_ref)
        xim_ref[...] = jnp.zeros_like(xim_ref)
        mkv = _dot(mem_ref[0].astype(BF16), wmkv_ref[...])
        mk = mkv[:, :GW]
        mv = mkv[:, GW:]
        memk_ref[0] = mk.T
        memv_ref[0] = mv.T
        mk_ref[...] = _stack_heads(mk.astype(BF16), hm)
        mv_ref[...] = _stack_heads(mv.astype(BF16), hm)

    x = x_ref[0]
    hn = _rms_norm(x, normw_ref[...]).astype(BF16)
    proj_ref[:, :NBLK_HEAD * GW] = _dot_nt(hn, winh_ref[...])
    proj_ref[:, NBLK_HEAD * GW:] = _dot_nt(hn, wint_ref[...])
    gates_t = _dot_nt(wg_ref[...], hn)

    nc = TB // L
    crow = [slice(c * L, (c + 1) * L) for c in range(nc)]

    def PB(blk, rows=slice(None)):
        return proj_ref[rows, blk * GW:(blk + 1) * GW]


    def xattn_stages():
        for piece in range(TB // XQ):
            rs = slice(piece * XQ, (piece + 1) * XQ)
            sc = _dot_nt(PB(AQ, rs).astype(BF16), mk_ref[...])
            yield
            ps = []
            for h in range(NH):
                seg = sc[:, h * MEM:(h + 1) * MEM]
                e = jnp.exp(seg - jnp.max(seg, axis=-1, keepdims=True))
                ps.append((e / jnp.sum(e, axis=-1, keepdims=True)).astype(BF16))
            p = jnp.concatenate(ps, axis=1)
            yield
            xa = _dot(p, mv_ref[...])
            yield
            mix_ref[rs, 3 * GW:4 * GW] = (xa * _silu(PB(AG, rs))).astype(BF16)
            yield

    def s5_stages():
        lanes = [(slice(j * 128, (j + 1) * 128), slice(NS + j * 128, NS + (j + 1) * 128))
                 for j in range(NS // 128)]
        bpc = L // LS
        nblk = TB // LS
        su = PB(SU)
        bu_ref[...] = _dot(su.astype(BF16), bt_ref[...])
        yield
        for c in range(nc):
            for lre, lim in lanes:
                br = bu_ref[crow[c], lre]
                bi = bu_ref[crow[c], lim]
                wr = jnp.tile(tab_ref[0, :, lre], (bpc, 1))
                wi = jnp.tile(tab_ref[1, :, lre], (bpc, 1))
                xcat_ref[crow[c], lre] = (wr * br - wi * bi).astype(BF16)
                xcat_ref[crow[c], lim] = (wr * bi + wi * br).astype(BF16)
            yield
        for c in range(nc):
            z = _dot(tri_ext, xcat_ref[crow[c], :])
            bu_ref[crow[c], :] = z[0:L]
            car_ref[c * bpc:(c + 1) * bpc, :] = z[L:L + bpc]
            yield
        rowb = lax.broadcasted_iota(jnp.int32, (nblk, 128), 0)
        ers, eis, prs, pis, c0s = [], [], [], [], []
        for lre, lim in lanes:
            zr = car_ref[:, lre]
            zi = car_ref[:, lim]
            pr = tab_ref[2, LS - 1:LS, lre]
            pi = tab_ref[3, LS - 1:LS, lre]
            er = pr * zr - pi * zi
            ei = pr * zi + pi * zr
            pr = tab_ref[4, LS - 1:LS, lre]
            pi = tab_ref[5, LS - 1:LS, lre]
            c0r = xre_ref[0:1, lre]
            c0i = xim_ref[0:1, lre]
            ers.append(er + jnp.where(rowb == 0, pr * c0r - pi * c0i, 0.0))
            eis.append(ei + jnp.where(rowb == 0, pr * c0i + pi * c0r, 0.0))
            prs.append(pr)
            pis.append(pi)
            c0s.append((c0r, c0i))
        yield
        assert LS == 8 and nblk % 8 == 0
        ngrp = nblk // 8
        row8 = lax.broadcasted_iota(jnp.int32, (8, 128), 0)
        ers = [[v[g * 8:(g + 1) * 8] for g in range(ngrp)] for v in ers]
        eis = [[v[g * 8:(g + 1) * 8] for g in range(ngrp)] for v in eis]
        for k in range(3):
            d = 1 << k
            for j in range(len(lanes)):
                pr, pi = prs[j], pis[j]
                for g in range(ngrp):
                    er, ei = ers[j][g], eis[j][g]
                    sr = jnp.where(row8 >= d, pltpu.roll(er, d, 0), 0.0)
                    si = jnp.where(row8 >= d, pltpu.roll(ei, d, 0), 0.0)
                    ers[j][g], eis[j][g] = er + pr * sr - pi * si, ei + pr * si + pi * sr
                prs[j], pis[j] = pr * pr - pi * pi, 2.0 * (pr * pi)
            yield
        for g in range(1, ngrp):
            for j, (lre, lim) in enumerate(lanes):
                qr = tab_ref[6, :, lre]
                qi = tab_ref[7, :, lre]
                cr = jnp.broadcast_to(ers[j][g - 1][7:8, :], (8, 128))
                ci_ = jnp.broadcast_to(eis[j][g - 1][7:8, :], (8, 128))
                ers[j][g] = ers[j][g] + (qr * cr - qi * ci_)
                eis[j][g] = eis[j][g] + (qr * ci_ + qi * cr)
            if g % 3 == 0 or g == ngrp - 1:
                yield
        ers = [jnp.concatenate(v, axis=0) for v in ers]
        eis = [jnp.concatenate(v, axis=0) for v in eis]
        for j, (lre, lim) in enumerate(lanes):
            xre_ref[:, lre] = jnp.broadcast_to(ers[j][nblk - 1:nblk, :], (8, 128))
            xim_ref[:, lre] = jnp.broadcast_to(eis[j][nblk - 1:nblk, :], (8, 128))
            cr = jnp.where(rowb == 0, c0s[j][0], pltpu.roll(ers[j], 1, 0))
            ci_ = jnp.where(rowb == 0, c0s[j][1], pltpu.roll(eis[j], 1, 0))
            ar = tab_ref[4, 0:1, lre]
            ai = tab_ref[5, 0:1, lre]
            car_ref[:, lre] = ar * cr - ai * ci_
            car_ref[:, lim] = ar * ci_ + ai * cr
        yield
        for c in range(nc):
            for lre, lim in lanes:
                cr = jnp.concatenate(
                    [jnp.broadcast_to(car_ref[c * bpc + j:c * bpc + j + 1, lre], (LS, 128))
                     for j in range(bpc)], axis=0)
                ci_ = jnp.concatenate(
                    [jnp.broadcast_to(car_ref[c * bpc + j:c * bpc + j + 1, lim], (LS, 128))
                     for j in range(bpc)], axis=0)
                zr = bu_ref[crow[c], lre] + cr
                zi = bu_ref[crow[c], lim] + ci_
                pr = jnp.tile(tab_ref[2, :, lre], (bpc, 1))
                pi = jnp.tile(tab_ref[3, :, lre], (bpc, 1))
                xcat_ref[crow[c], lre] = (pr * zr - pi * zi).astype(BF16)
                xcat_ref[crow[c], lim] = (pr * zi + pi * zr).astype(BF16)
            yield
        sy = _dot(xcat_ref[...], ct_ref[...]) + dsk_ref[...] * su
        yield
        sy = _gelu_tanh(sy)
        gate = _dot(sy.astype(BF16), wglu_ref[...])
        yield
        mix_ref[:, 2 * GW:3 * GW] = (sy * _sigmoid(gate) * _silu(PB(SG))).astype(BF16)
        yield

    def retention_stages():
        rq = _rope(PB(RQ), cos_ref[...], s1_ref[...], s2_ref[...])
        rk = _rope(PB(RK), cos_ref[...], s1_ref[...], s2_ref[...])
        rvf = PB(RV)
        rv = rvf.astype(BF16)
        rqb = rq.astype(BF16)
        rkb = rk.astype(BF16)
        kst = [_stack_heads(rkb[crow[c]], hm) for c in range(nc)]
        vst = [_stack_heads(rv[crow[c]], hm) for c in range(nc)]
        yield
        inner = [_dot_nt(rqb[crow[c]], kst[c]) for c in range(nc)]
        yield
        pmat = [(inner[c] * dec_ref[...]).astype(BF16) for c in range(nc)]
        kd = [(rk[crow[c]] * kdec_ref[...]).astype(BF16) for c in range(nc)]
        yield
        rloc = [_dot(pmat[c], vst[c]) for c in range(nc)]
        upd = [_dot_tn(kd[c], rv[crow[c]]) for c in range(nc)]
        yield
        st = [s_ref[...]]
        for c in range(nc):
            st.append(st[c] * cdec_ref[...] + jnp.where(bd, upd[c], 0.0))
        s_ref[...] = st[nc]
        qd = [(rq[crow[c]] * qdec_ref[...]).astype(BF16) for c in range(nc)]
        yield
        ost = [_dot(qd[c], st[c].astype(BF16)) for c in range(nc)]
        yield
        ro = jnp.concatenate([rloc[c] + ost[c] for c in range(nc)], axis=0)
        mix_ref[:, 0:GW] = (_head_norm(ro, retgn_ref[...], avg) * _silu(PB(RG))).astype(BF16)
        yield

    def mlstm_stages():
        ig_r = gates_t[0:8] + bi_ref[...]
        lf_r = _log_sigmoid(gates_t[8:16] + bf_ref[...])
        yield
        b_r = [_dot_x3(lf_r[:, crow[c]], tri_u) for c in range(nc)]
        yield
        gd_r = [ig_r[:, crow[c]] - b_r[c] for c in range(nc)]
        lane_r = lax.broadcasted_iota(jnp.int32, (8, L), 1)
        cm_r = list(gd_r)
        for k in range(7):
            d = 1 << k
            cm_r = [jnp.maximum(v, jnp.where(lane_r >= d, pltpu.roll(v, d, 1), NEG_INF))
                    for v in cm_r]
            yield
        m_prev = [m_ref[...]]
        mt_r = []
        for c in range(nc):
            mt = jnp.maximum(b_r[c] + m_prev[c], b_r[c] + cm_r[c])
            mt_r.append(mt)
            m_prev.append(jnp.broadcast_to(mt[:, L - 1:L], (8, L)))
            yield
        m_ref[...] = m_prev[nc]
        cols = []
        for c in range(nc):
            bm = b_r[c] - mt_r[c]
            ws = jnp.exp(b_r[c] + m_prev[c] - mt_r[c])
            wl = jnp.exp(gd_r[c] + jnp.broadcast_to(bm[:, L - 1:L], (8, L)))
            emt = jnp.exp(-mt_r[c])
            cols.append(jnp.concatenate([bm, ws, wl, emt, jnp.zeros((L - 32, L), F32)], axis=0).T)
        mqf = PB(MQ)
        mq = mqf.astype(BF16)
        mkf = PB(MK)
        mvf = PB(MV)
        mv_ = mvf.astype(BF16)
        mkb = mkf.astype(BF16)
        kst = [_stack_heads(mkb[crow[c]], hm) for c in range(nc)]
        vst = [_stack_heads(mv_[crow[c]], hm) for c in range(nc)]
        yield
        sraw = [_dot_nt(mq[crow[c]], kst[c]) for c in range(nc)]
        yield
        smat, den_i = [], []
        for c in range(nc):
            parts, dens = [], []
            for h in range(NH):
                arg = jnp.where(causal, cols[c][:, h:h + 1] + gd_r[c][h:h + 1, :], NEG_INF)
                s_h = sraw[c][:, h * L:(h + 1) * L] * jnp.exp(arg)
                dens.append(jnp.sum(s_h, axis=-1, keepdims=True))
                parts.append(s_h.astype(BF16))
            den_i.append(dens)
            smat.append(jnp.concatenate(parts, axis=1))
            yield
        ones_blk = jnp.ones((L, 128), BF16)
        def per_head(columns):
            first = lane128 < DH
            return jnp.concatenate([jnp.where(first, columns[0], columns[1]),
                                    jnp.where(first, columns[2], columns[3])], axis=1)

        kws = [(mkf[crow[c]] * per_head([cols[c][:, 16 + h:17 + h] for h in range(NH)])
                ).astype(BF16) for c in range(nc)]
        yield
        rloc = [_dot(smat[c], vst[c]) for c in range(nc)]
        u = [_dot_tn(kws[c], jnp.concatenate([mv_[crow[c]], ones_blk], axis=1))
             for c in range(nc)]
        yield
        nmask = (lax.broadcasted_iota(jnp.int32, (GW, 128), 0) // DH
                 == lax.broadcasted_iota(jnp.int32, (GW, 128), 1))
        cst = [c_ref[...]]
        nst = [n_ref[...]]
        for c in range(nc):
            wsl256 = jnp.zeros((1, GW), F32)
            wsl128 = jnp.zeros((1, 128), F32)
            for h in range(NH):
                wsl = cols[c][L - 1:L, 8 + h:9 + h]
                wsl256 = wsl256 + jnp.where(hm[h], wsl, 0.0)
                wsl128 = wsl128 + jnp.where(lane128 == h, wsl, 0.0)
            cst.append(cst[c] * wsl256 + jnp.where(bd, u[c][:, :GW], 0.0))
            nst.append(nst[c] * wsl128 + jnp.where(nmask, u[c][:, GW:], 0.0))
        c_ref[...] = cst[nc]
        n_ref[...] = nst[nc]
        yield
        qc = [_dot(mq[crow[c]], cst[c].astype(BF16)) for c in range(nc)]
        qn = [_dot(mq[crow[c]], nst[c].astype(BF16)) for c in range(nc)]
        yield
        mhs = []
        for c in range(nc):
            wsc = [cols[c][:, 8 + h:9 + h] for h in range(NH)]
            rdn = []
            for h in range(NH):
                den = den_i[c][h] + wsc[h] * qn[c][:, h:h + 1]
                rdn.append(1.0 / jnp.maximum(jnp.abs(den), cols[c][:, 24 + h:25 + h]))
            mhs.append((rloc[c] + per_head(wsc) * qc[c]) * per_head(rdn))
            yield
        mh = jnp.concatenate(mhs, axis=0) * _sigmoid(PB(MO))
        mix_ref[:, GW:2 * GW] = (_head_norm(mh, mlgn_ref[...], avg) * _silu(PB(MG))).astype(BF16)
        yield

    pending = [mlstm_stages(), s5_stages(), retention_stages(), xattn_stages()]
    while pending:
        pending = [stage for stage in pending if next(stage, _DONE) is not _DONE]

    y = x_ref[0] + _dot(mix_ref[...], wout_ref[...])
    if last_layer:
        y = _rms_norm(y, fnw_ref[...])
    y_ref[0] = y

    @pl.when(t == n_tblocks - 1)
    def _final():
        s_fin = s_ref[...]
        c_fin = c_ref[...].T
        n_fin = n_ref[...].T
        for h in range(NH):
            blk = slice(h * DH, (h + 1) * DH)
            rets_ref[0, h] = s_fin[blk, blk]
            mlc_ref[0, h] = c_fin[blk, blk]
        mln_ref[0] = jnp.concatenate([n_fin[h:h + 1, h * DH:(h + 1) * DH] for h in range(NH)],
                                     axis=0)
        mlm_ref[0] = m_ref[...]
        s5re_ref[0] = xre_ref[...]
        s5im_ref[0] = xim_ref[...]


def _prompt_layer(layer, x, mem, w, consts, last_layer, prev):
    bsz, seq, _ = x.shape
    depth = w["w_out"].shape[0]
    nt = seq // TB
    full = lambda shape: pl.BlockSpec(shape, lambda b, t: (0,) * len(shape),
                                      pipeline_mode=pl.Buffered(1))
    lyr = lambda shape: pl.BlockSpec((None,) + shape, lambda b, t: (layer,) + (0,) * len(shape),
                                     pipeline_mode=pl.Buffered(1))
    tok = lambda width: pl.BlockSpec((TB, width), lambda b, t: (t, 0))
    per_b = lambda r, c: pl.BlockSpec((1, r, c), lambda b, t: (b, 0, 0))
    in_specs = [
        pl.BlockSpec((1, TB, D), lambda b, t: (b, t, 0)),
        per_b(MEM, D),
        lyr((1, D)), lyr((NBLK_HEAD * GW, D)), lyr((NBLK_TAIL * GW, D)), lyr((NGATE, D)),
        lyr((D, D)), lyr((D, 2 * GW)),
        tok(128), tok(128), tok(128),
        full((L, NH * L)), full((L, GW)), full((L, GW)), full((1, GW)),
        lyr((1, GW)), lyr((1, GW)), lyr((8, TB)), lyr((8, TB)),
        lyr((NTAB, LS, NS)), lyr((GW, 2 * NS)), lyr((2 * NS, GW)),
        lyr((1, GW)), lyr((GW, GW)), full((1, D)),
    ] + [pl.BlockSpec(memory_space=pl.ANY)] * len(prev)
    n_in = len(in_specs) - len(prev)
    state_shapes = [(NH, DH, DH),
                    (NH, DH, DH),
                    (NH, DH),
                    (8, 128),
                    (8, NS),
                    (8, NS),
                    (GW, MEM),
                    (GW, MEM)]
    assert len(state_shapes) == N_STATE_OUT and len(prev) in (0, N_STATE_OUT)
    out_shape = ((jax.ShapeDtypeStruct((bsz, seq, D), F32),)
                 + tuple(jax.ShapeDtypeStruct((depth, bsz) + s, F32) for s in state_shapes))
    state_spec = lambda s: pl.BlockSpec((None, 1) + s,
                                        lambda b, t: (layer, b) + (0,) * len(s))
    out_specs = ((pl.BlockSpec((1, TB, D), lambda b, t: (b, t, 0)),)
                 + tuple(state_spec(s) for s in state_shapes))
    scratch = [
        pltpu.VMEM((TB, DP), F32),
        pltpu.VMEM((TB, D), BF16),
        pltpu.VMEM((GW, GW), F32),
        pltpu.VMEM((GW, GW), F32),
        pltpu.VMEM((GW, 128), F32),
        pltpu.VMEM((8, 128), F32),
        pltpu.VMEM((8, NS), F32),
        pltpu.VMEM((8, NS), F32),
        pltpu.VMEM((NH * MEM, GW), BF16),
        pltpu.VMEM((NH * MEM, GW), BF16),
        pltpu.VMEM((TB, 2 * NS), F32),
        pltpu.VMEM((TB, 2 * NS), BF16),
        pltpu.VMEM((TB // LS, 2 * NS), F32),
    ]
    kern = functools.partial(_prompt_kernel, last_layer=last_layer, n_tblocks=nt, n_prev=len(prev))
    return pl.pallas_call(
        kern, grid=(bsz, nt), in_specs=in_specs, out_specs=out_specs, out_shape=out_shape,
        scratch_shapes=scratch, name="prompt_layer",
        input_output_aliases={n_in + i: 1 + i for i in range(len(prev))},
        compiler_params=pltpu.CompilerParams(
            dimension_semantics=("arbitrary", "arbitrary"), vmem_limit_bytes=VMEM_LIMIT),
    )(x, mem, w["norm_w"], w["w_in_head"], w["w_in_tail"], w["w_gate"], w["w_out"], w["w_mem_kv"],
      consts["cos_p"], consts["s1_p"], consts["s2_p"],
      consts["dec"], consts["qdec"], consts["kdec"], consts["cdec"],
      w["ret_gn"], w["ml_gn"], w["b_i8"], w["b_f8"],
      w["tab"], w["bt"], w["ct"], w["s5_d"], w["w_glu"], w["final_norm_w"], *prev)


SROWS = 16


def _sample_kernel_t(x_ref, normw_ref, winh_ref, wint_ref, wg_ref, wout_ref,
                     cos_ref, s1_ref, s2_ref, gam_ref, gam8_ref,
                     retgn_ref, mlgn_ref, bi_ref, bf_ref,
                     abt_ref, bt_ref, ct_ref, dsk_ref, wglu_ref, fnw_ref,
                     m0_ref, n0_ref, x0re_ref, x0im_ref,
                     rets_ref, mlc_ref, ck_ref, cv_ref,
                     y_ref, retn_ref, mlcn_ref, mlnn_ref, mlmn_ref, s5re_ref, s5im_ref,
                     hs_ref, proj_ref, qgt_ref, rkt_ref, rvt_ref, mqt_ref, mkt_ref, mvt_ref,
                     wi_ref, ws_ref, ot_ref, cqt_ref, qa8_ref, xa_ref,
                     *, n_layers, n_blocks):
    layer = pl.program_id(0)
    g = pl.program_id(1)
    nsamp = x_ref.shape[0]
    ones_bd = _ones_matrix()
    avg = _avg_matrix()
    mask8 = (lax.broadcasted_iota(jnp.int32, (8, GW), 0)
             == lax.broadcasted_iota(jnp.int32, (8, GW), 1) // DH)

    def P(blk):
        return proj_ref[:, blk * GW:(blk + 1) * GW]

    @pl.when((layer == 0) & (g == 0))
    def _load_x():
        hs_ref[...] = x_ref[...]

    @pl.when(g == 0)
    def _pre():
        hn = _rms_norm(hs_ref[...], normw_ref[...]).astype(BF16)
        proj_ref[:, :NBLK_HEAD * GW] = _dot_nt(hn, winh_ref[...])
        proj_ref[:, NBLK_HEAD * GW:] = _dot_nt(hn, wint_ref[...])
        cosr, s1, s2 = cos_ref[...], s1_ref[...], s2_ref[...]
        qgt_ref[...] = (_rope(P(RQ), cosr, s1, s2) * gam_ref[...]).T
        rkt_ref[...] = _rope(P(RK), cosr, s1, s2).T
        rvt_ref[...] = P(RV).T
        mqt_ref[...] = P(MQ).T
        mkt_ref[...] = P(MK).T
        mvt_ref[...] = P(MV).T
        gt = _dot_nt(wg_ref[...], hn)
        ig = gt[0:8] + bi_ref[...]
        lf = _log_sigmoid(gt[8:16] + bf_ref[...])
        a = lf + m0_ref[...]
        mt = jnp.maximum(a, ig)
        wi_ref[...] = jnp.exp(ig - mt)
        ws_ref[...] = jnp.exp(a - mt)
        mlmn_ref[...] = mt
        ot_ref[...] = jnp.zeros_like(ot_ref)
        qa8_ref[...] = jnp.where(mask8[None], P(AQ)[:, None, :], 0.0).reshape(8 * nsamp, GW)
        but = _dot_tn(bt_ref[...], P(SU).T.astype(BF16))
        are, aim = abt_ref[0:NS, :], abt_ref[NS:2 * NS, :]
        x0r, x0i = x0re_ref[...], x0im_ref[...]
        s5re_ref[...] = are * x0r - aim * x0i + but[0:NS]
        s5im_ref[...] = are * x0i + aim * x0r + but[NS:2 * NS]

    head = g // (GW // SROWS // NH)
    hrow = pl.ds(pl.multiple_of(head * DH, DH), DH)
    gam_row = gam8_ref[pl.ds(head, 1), :]
    ws_row = ws_ref[pl.ds(head, 1), :]
    wi_row = wi_ref[pl.ds(head, 1), :]
    v_slab = rvt_ref[hrow, :]
    q_slab = mqt_ref[hrow, :]
    k_slab = mkt_ref[hrow, :]
    o_acc = jnp.zeros((DH, nsamp), F32)
    for i in range(SROWS):
        r = pl.ds(g * SROWS + i, 1)
        s_t = rets_ref[i]
        o_acc = o_acc + qgt_ref[r, :] * s_t
        retn_ref[i] = gam_row * s_t + rkt_ref[r, :] * v_slab
        c_t = mlc_ref[i]
        cqt_ref[r, :] = jnp.sum(c_t * q_slab, axis=0, keepdims=True)
        mlcn_ref[i] = ws_row * c_t + (wi_row * mvt_ref[r, :]) * k_slab
    ot_ref[hrow, :] = ot_ref[hrow, :] + o_acc

    tiles = [pl.ds(pl.multiple_of((g * SB + i) * 8, 8), 8) for i in range(SB)]
    q8 = [qa8_ref[tiles[i], :].astype(BF16) for i in range(SB)]
    sc = [_dot(q8[i], ck_ref[i].astype(BF16)) for i in range(SB)]
    ps = []
    for i in range(SB):
        e = jnp.exp(sc[i] - jnp.max(sc[i], axis=-1, keepdims=True))
        ps.append((e / jnp.sum(e, axis=-1, keepdims=True)).astype(BF16))
    ta = [_dot_nt(ps[i], cv_ref[i].astype(BF16)) for i in range(SB)]
    xa_ref[pl.ds(pl.multiple_of(g * SB, SB), SB), :] = jnp.concatenate(
        [jnp.sum(jnp.where(mask8, ta[i], 0.0), axis=0, keepdims=True) for i in range(SB)], axis=0)

    @pl.when(g == n_blocks - 1)
    def _post():
        cosr, s1, s2 = cos_ref[...], s1_ref[...], s2_ref[...]
        rq = _rope(P(RQ), cosr, s1, s2)
        rk = _rope(P(RK), cosr, s1, s2)
        ro = _dot_x2(rq * rk, ones_bd) * P(RV) + ot_ref[...].T
        ret_out = _head_norm(ro, retgn_ref[...], avg) * _silu(P(RG))
        mht = []
        for h in range(NH):
            rows = slice(h * DH, (h + 1) * DH)
            wi = wi_ref[h:h + 1, :]
            ws = ws_ref[h:h + 1, :]
            emt = jnp.exp(-mlmn_ref[h:h + 1, :])
            qt, kt, n0 = mqt_ref[rows, :], mkt_ref[rows, :], n0_ref[rows, :]
            s = jnp.sum(qt * kt, axis=0, keepdims=True) * wi
            den = s + ws * jnp.sum(n0 * qt, axis=0, keepdims=True)
            mht.append((s * mvt_ref[rows, :] + ws * cqt_ref[rows, :])
                       / jnp.maximum(jnp.abs(den), emt))
            mlnn_ref[rows, :] = ws * n0 + wi * kt
        mh = jnp.concatenate(mht, axis=0).T * _sigmoid(P(MO))
        ml_out = _head_norm(mh, mlgn_ref[...], avg) * _silu(P(MG))
        su = P(SU)
        xcat = jnp.concatenate([s5re_ref[...], s5im_ref[...]], axis=0).astype(BF16)
        sy = _dot_tn(xcat, ct_ref[...]) + dsk_ref[...] * su
        sy = _gelu_tanh(sy)
        sy = sy * _sigmoid(_dot(sy.astype(BF16), wglu_ref[...]))
        s5_out = sy * _silu(P(SG))
        xa_out = xa_ref[...] * _silu(P(AG))
        mix = jnp.concatenate([ret_out, ml_out, s5_out, xa_out], axis=1).astype(BF16)
        y = hs_ref[...] + _dot(mix, wout_ref[...])
        hs_ref[...] = y

        @pl.when(layer == n_layers - 1)
        def _emit():
            y_ref[...] = _rms_norm(y, fnw_ref[...])


def _sample_layers_t(x, st, w, consts):
    nsamp = x.shape[0]
    depth = w["w_out"].shape[0]
    nb = GW // SROWS
    assert nsamp == nb * SB and nsamp == 128
    once = lambda shape: pl.BlockSpec(shape, lambda l, g: (0,) * len(shape),
                                      pipeline_mode=pl.Buffered(1))
    lyr = lambda shape: pl.BlockSpec((None,) + shape, lambda l, g: (l,) + (0,) * len(shape))
    lyr_out = lyr
    srows = pl.BlockSpec((None, SROWS, DH, nsamp), lambda l, g: (l, g, 0, 0))
    cache = pl.BlockSpec((None, SB, GW, MEM), lambda l, g: (l, g, 0, 0))
    in_specs = [
        once((nsamp, D)), lyr((1, D)), lyr((NBLK_HEAD * GW, D)), lyr((NBLK_TAIL * GW, D)),
        lyr((NGATE, D)), lyr((D, D)),
        once((1, 128)), once((1, 128)), once((1, 128)), once((1, GW)), once((8, nsamp)),
        lyr((1, GW)), lyr((1, GW)), lyr((8, nsamp)), lyr((8, nsamp)),
        lyr((2 * NS, nsamp)), lyr((GW, 2 * NS)), lyr((2 * NS, GW)),
        lyr((1, GW)), lyr((GW, GW)), once((1, D)),
        lyr((8, nsamp)), lyr((GW, nsamp)), lyr((NS, nsamp)), lyr((NS, nsamp)),
        srows, srows, cache, cache,
    ]
    out_shape = (
        jax.ShapeDtypeStruct((nsamp, D), F32),
        jax.ShapeDtypeStruct((depth, GW, DH, nsamp), F32),
        jax.ShapeDtypeStruct((depth, GW, DH, nsamp), F32),
        jax.ShapeDtypeStruct((depth, GW, nsamp), F32),
        jax.ShapeDtypeStruct((depth, 8, nsamp), F32),
        jax.ShapeDtypeStruct((depth, NS, nsamp), F32),
        jax.ShapeDtypeStruct((depth, NS, nsamp), F32),
    )
    out_specs = (
        pl.BlockSpec((nsamp, D), lambda l, g: (0, 0)), srows, srows,
        lyr_out((GW, nsamp)), lyr_out((8, nsamp)), lyr_out((NS, nsamp)), lyr_out((NS, nsamp)),
    )
    scratch = [
        pltpu.VMEM((nsamp, D), F32),
        pltpu.VMEM((nsamp, DP), F32),
        pltpu.VMEM((GW, nsamp), F32), pltpu.VMEM((GW, nsamp), F32), pltpu.VMEM((GW, nsamp), F32),
        pltpu.VMEM((GW, nsamp), F32), pltpu.VMEM((GW, nsamp), F32), pltpu.VMEM((GW, nsamp), F32),
        pltpu.VMEM((8, nsamp), F32), pltpu.VMEM((8, nsamp), F32),
        pltpu.VMEM((GW, nsamp), F32), pltpu.VMEM((GW, nsamp), F32),
        pltpu.VMEM((8 * nsamp, GW), F32), pltpu.VMEM((nsamp, GW), F32),
    ]
    kern = functools.partial(_sample_kernel_t, n_layers=depth, n_blocks=nb)
    return pl.pallas_call(
        kern, grid=(depth, nb), in_specs=in_specs, out_specs=out_specs, out_shape=out_shape,
        scratch_shapes=scratch, name="sample_layers",
        compiler_params=pltpu.CompilerParams(
            dimension_semantics=("arbitrary", "arbitrary"), vmem_limit_bytes=VMEM_LIMIT),
    )(x, w["norm_w"], w["w_in_head"], w["w_in_tail"], w["w_gate"], w["w_out"],
      consts["cos_s"], consts["s1_s"], consts["s2_s"], consts["gam"], consts["gam8"],
      w["ret_gn"], w["ml_gn"], w["b_i8"], w["b_f8"],
      w["abt"], w["bt"], w["ct"], w["s5_d"], w["w_glu"], w["final_norm_w"],
      st["m"], st["n"], st["s5_re"], st["s5_im"], st["ret"], st["c"], st["mem_k"], st["mem_v"])


def _rope_tables(pos):
    half = DH // 2
    inv = ROPE_BASE ** (-np.arange(half, dtype=np.float64) / half)
    ang = np.asarray(pos, np.float64)[:, None] * inv[None, :]
    cos, sin = np.cos(ang), np.sin(ang)
    zero = np.zeros_like(sin)
    c = np.tile(np.concatenate([cos, cos], axis=-1), (1, 2))
    s1 = np.tile(np.concatenate([zero, sin], axis=-1), (1, 2))
    s2 = np.tile(np.concatenate([-sin, zero], axis=-1), (1, 2))
    return tuple(jnp.asarray(t, F32) for t in (c, s1, s2))


def _constants(seq):
    lg = np.log1p(-np.power(2.0, -5.0 - np.arange(NH, dtype=np.float64)))[:, None]
    idx = np.arange(L, dtype=np.float64)
    diff = idx[:, None] - idx[None, :]
    decay = np.where(diff >= 0, np.exp(lg[:, :, None] * np.maximum(diff, 0.0)), 0.0)
    rep = lambda t: np.repeat(t, DH, axis=0).T
    consts = {
        "dec": np.transpose(decay, (1, 0, 2)).reshape(L, NH * L),
        "qdec": rep(np.exp(lg * (idx + 1.0))),
        "kdec": rep(np.exp(lg * (L - 1.0 - idx))),
        "cdec": rep(np.exp(lg * L)),
        "gam": rep(np.exp(lg * 1.0)),
        "gam8": np.pad(np.broadcast_to(np.exp(lg), (NH, 128)), ((0, 8 - NH), (0, 0))),
    }
    consts = {k: jnp.asarray(v, F32) for k, v in consts.items()}
    consts["cos_p"], consts["s1_p"], consts["s2_p"] = _rope_tables(np.arange(seq))
    consts["cos_s"], consts["s1_s"], consts["s2_s"] = _rope_tables(PAST_LEN + np.arange(1))
    return consts


def _pack_w_in(w_in):
    wt = jnp.swapaxes(w_in, 1, 2)
    n_head, n_gate = NBLK_HEAD * GW, 2 * NH
    assert wt.shape[1] == DP + n_gate
    row_scale = np.ones((DP,), np.float32)
    for blk in (RK, MK, AQ):
        row_scale[blk * GW:(blk + 1) * GW] = DH ** -0.5
    row_scale = jnp.asarray(row_scale)[None, :, None]
    head = (wt[:, :n_head] * row_scale[:, :n_head]).astype(BF16)
    tail = (wt[:, n_head + n_gate:] * row_scale[:, n_head:]).astype(BF16)
    pad8 = lambda t: jnp.pad(t, ((0, 0), (0, 8 - t.shape[1]), (0, 0)))
    gates = jnp.concatenate([pad8(wt[:, n_head:n_head + NH]),
                             pad8(wt[:, n_head + NH:n_head + n_gate])], axis=1)
    return head, tail, gates.astype(BF16)


def kernel(x_prompt, x_sample, mem_prompt, state_ret, state_mlstm_c, state_mlstm_n, state_mlstm_m,
           state_s5_re, state_s5_im, cache_mem_k, cache_mem_v,
           norm_w, w_in, ret_gn, ml_b_i, ml_b_f, ml_gn,
           s5_a_re, s5_a_im, s5_log_dt, s5_b_re, s5_b_im, s5_c_re, s5_c_im, s5_d, s5_w_glu,
           w_mem_k, w_mem_v, w_out, final_norm_w):
    depth = norm_w.shape[0]
    bp, seq, _ = x_prompt.shape
    bs = x_sample.shape[0]
    consts = _constants(seq)
    abt, bt, ct, tab = _s5_prepare(s5_a_re, s5_a_im, s5_log_dt, s5_b_re, s5_b_im, s5_c_re, s5_c_im)
    rows8 = lambda t: jnp.pad(jnp.broadcast_to(t[:, :, None], (depth, NH, TB)),
                              ((0, 0), (0, 8 - NH), (0, 0)))
    w_head, w_tail, w_gate = _pack_w_in(w_in)
    w = {
        "norm_w": norm_w[:, None], "w_in_head": w_head, "w_in_tail": w_tail, "w_gate": w_gate,
        "w_out": w_out.astype(BF16),
        "w_mem_kv": jnp.concatenate([w_mem_k, w_mem_v], axis=-1).astype(BF16),
        "ret_gn": ret_gn[:, None], "ml_gn": ml_gn[:, None],
        "b_i8": rows8(ml_b_i), "b_f8": rows8(ml_b_f),
        "abt": abt, "tab": tab, "bt": bt, "ct": ct,
        "s5_d": s5_d[:, None], "w_glu": s5_w_glu.astype(BF16),
        "final_norm_w": final_norm_w[None],
    }

    hp = x_prompt
    states = ()
    for l in range(depth):
        hp, *states = _prompt_layer(l, hp, mem_prompt, w, consts, l == depth - 1, tuple(states))
    ret_p, mlc_p, mln_p, ms, xr, xi, mk, mv = states
    mlm_p = ms[:, :, :NH, 0]
    s5re_p = xr[:, :, 0].reshape(depth, bp, S5G, S5P)
    s5im_p = xi[:, :, 0].reshape(depth, bp, S5G, S5P)
    memk_p = jnp.transpose(mk.reshape(depth, bp, NH, DH, MEM), (0, 1, 4, 2, 3))
    memv_p = jnp.transpose(mv.reshape(depth, bp, NH, DH, MEM), (0, 1, 4, 2, 3))

    st = {
        "m": jnp.pad(jnp.swapaxes(state_mlstm_m, 1, 2), ((0, 0), (0, 8 - NH), (0, 0))),
        "n": jnp.transpose(state_mlstm_n, (0, 2, 3, 1)).reshape(depth, GW, bs),
        "s5_re": jnp.transpose(state_s5_re, (0, 2, 3, 1)).reshape(depth, NS, bs),
        "s5_im": jnp.transpose(state_s5_im, (0, 2, 3, 1)).reshape(depth, NS, bs),
        "ret": jnp.transpose(state_ret, (0, 2, 3, 4, 1)).reshape(depth, GW, DH, bs),
        "c": jnp.transpose(state_mlstm_c, (0, 2, 3, 4, 1)).reshape(depth, GW, DH, bs),
        "mem_k": jnp.transpose(cache_mem_k, (0, 1, 3, 4, 2)).reshape(depth, bs, GW, MEM),
        "mem_v": jnp.transpose(cache_mem_v, (0, 1, 3, 4, 2)).reshape(depth, bs, GW, MEM),
    }
    hs, rn, cn, nn, mn, sr, si = _sample_layers_t(x_sample.reshape(bs, D), st, w, consts)
    back5 = lambda t: jnp.transpose(t.reshape(depth, NH, DH, DH, bs), (0, 4, 1, 2, 3))
    back4 = lambda t, a, b: jnp.transpose(t.reshape(depth, a, b, bs), (0, 3, 1, 2))
    return (hp, hs.reshape(bs, 1, D),
            ret_p, back5(rn), mlc_p, back5(cn),
            mln_p, back4(nn, NH, DH), mlm_p, jnp.swapaxes(mn[:, :NH], 1, 2),
            s5re_p, back4(sr, S5G, S5P), s5im_p, back4(si, S5G, S5P),
            memk_p, memv_p)
```

```python
import functools
import math

import numpy as np
import jax
import jax.numpy as jnp
from jax import lax
from jax.experimental import pallas as pl
from jax.experimental.pallas import tpu as pltpu

F32 = jnp.float32
BF16 = jnp.bfloat16

D = 1024
GW = 256
NH = 4
DH = 64
L = 128
LS = 8
NTAB = 8
S5G = 16
S5P = 64
S5C = 16
NS = S5G * S5P
MEM = 256
EPS = 1e-6
NEG_INF = -1e30
ROPE_BASE = 10000.0
PAST_LEN = 16384

TB = 512
SB = 8
XQ = 512
NBLK = 13
DP = NBLK * GW
(SU, RQ, RK, RV, RG, MQ, MK, MV, MO, MG, SG, AQ, AG) = range(NBLK)
NGATE = 16

VMEM_LIMIT = 56 * 1024 * 1024


_DONE = object()


def _dot(a, b):
    return jnp.dot(a, b, preferred_element_type=F32)


def _dot_nt(a, b):
    return lax.dot_general(a, b, (((1,), (1,)), ((), ())), preferred_element_type=F32)


def _dot_tn(a, b):
    return lax.dot_general(a, b, (((0,), (0,)), ((), ())), preferred_element_type=F32)


def _split2(x):
    hi = x.astype(BF16)
    lo = (x - hi.astype(F32)).astype(BF16)
    return hi, lo


def _dot_x2(x, w):
    hi, lo = _split2(x)
    return _dot(hi, w) + _dot(lo, w)


def _dot_x3(x, w):
    hi = x.astype(BF16)
    r = x - hi.astype(F32)
    mid = r.astype(BF16)
    lo = (r - mid.astype(F32)).astype(BF16)
    return _dot(hi, w) + _dot(mid, w) + _dot(lo, w)


def _sigmoid(x):
    return 0.5 * (1.0 + jnp.tanh(0.5 * x))


def _silu(x):
    h = 0.5 * x
    return h + h * jnp.tanh(h)


def _log_sigmoid(x):
    return jnp.minimum(x, 0.0) - jnp.log1p(jnp.exp(-jnp.abs(x)))


def _gelu_tanh(x):
    c = math.sqrt(2.0 / math.pi)
    h = 0.5 * x
    return h + h * jnp.tanh(x * (c + (0.044715 * c) * (x * x)))


def _lane_head(n):
    return lax.broadcasted_iota(jnp.int32, (1, n), 1) // DH


def _head_masks():
    lh = _lane_head(GW)
    return [lh == h for h in range(NH)]


def _block_diag_mask():
    r = lax.broadcasted_iota(jnp.int32, (GW, GW), 0) // DH
    c = lax.broadcasted_iota(jnp.int32, (GW, GW), 1) // DH
    return r == c


def _avg_matrix():
    return jnp.where(_block_diag_mask(), 1.0 / DH, 0.0).astype(BF16)


def _ones_matrix():
    return jnp.where(_block_diag_mask(), 1.0, 0.0).astype(BF16)


def _rope(x, cos, s1, s2):
    outs = []
    for j in range(2):
        xs = x[:, j * 128:(j + 1) * 128]
        outs.append(xs * cos + pltpu.roll(xs, 32, 1) * s1 + pltpu.roll(xs, 96, 1) * s2)
    return jnp.concatenate(outs, axis=1)


def _head_norm(x, gain, avg):
    mu = _dot_x2(x, avg)
    d = x - mu
    var = _dot((d * d).astype(BF16), avg)
    return d * lax.rsqrt(var + EPS) * gain


def _rms_norm(x, w):
    ms = jnp.mean(x * x, axis=-1, keepdims=True)
    return x * lax.rsqrt(ms + EPS) * w


def _stack_heads(x, hm):
    zero = jnp.zeros((x.shape[0], 128), x.dtype)
    blocks = []
    for h in range(NH):
        half = slice((h * DH) // 128 * 128, (h * DH) // 128 * 128 + 128)
        kept = jnp.where(hm[h][:, half], x[:, half], 0.0)
        blocks.append(jnp.concatenate([kept, zero] if half.start == 0 else [zero, kept], axis=1))
    return jnp.concatenate(blocks, axis=0)


def _s5_prep_kernel(are_ref, aim_ref, ldt_ref, bre_ref, bim_ref, cre_ref, cim_ref,
                    abt_ref, bt_ref, ct_ref, tab_ref):
    a_re = are_ref[0]
    a_im = aim_ref[0]
    dt = jnp.exp(ldt_ref[0])
    lam_re = a_re * dt
    lam_im = a_im * dt
    mag = jnp.exp(lam_re)
    ab_re = mag * jnp.cos(lam_im)
    ab_im = mag * jnp.sin(lam_im)
    den = a_re * a_re + a_im * a_im
    nr = ab_re - 1.0
    ni = ab_im
    f_re = (nr * a_re + ni * a_im) / den
    f_im = (ni * a_re - nr * a_im) / den
    abt_ref[0, 0:NS, :] = jnp.broadcast_to(ab_re, (128, NS)).T
    abt_ref[0, NS:2 * NS, :] = jnp.broadcast_to(ab_im, (128, NS)).T
    b_re = bre_ref[0]
    b_im = bim_ref[0]
    bb_re = (f_re * b_re - f_im * b_im).astype(BF16)
    bb_im = (f_re * b_im + f_im * b_re).astype(BF16)
    rep_r = (lax.broadcasted_iota(jnp.int32, (GW, S5C), 0) % S5C
             == lax.broadcasted_iota(jnp.int32, (GW, S5C), 1)).astype(BF16)
    in_blk = (lax.broadcasted_iota(jnp.int32, (GW, NS), 0) // S5C
              == lax.broadcasted_iota(jnp.int32, (GW, NS), 1) // S5P)
    bt_ref[0, :, :NS] = jnp.where(in_blk, _dot(rep_r, bb_re), 0.0).astype(BF16)
    bt_ref[0, :, NS:] = jnp.where(in_blk, _dot(rep_r, bb_im), 0.0).astype(BF16)
    rep_c = (lax.broadcasted_iota(jnp.int32, (S5C, GW), 0)
             == lax.broadcasted_iota(jnp.int32, (S5C, GW), 1) % S5C).astype(BF16)
    out_blk = (lax.broadcasted_iota(jnp.int32, (NS, GW), 0) // S5P
               == lax.broadcasted_iota(jnp.int32, (NS, GW), 1) // S5C)
    ct_ref[0, :NS, :] = jnp.where(out_blk, _dot(cre_ref[0].astype(BF16), rep_c), 0.0).astype(BF16)
    ct_ref[0, NS:, :] = jnp.where(out_blk, -_dot(cim_ref[0].astype(BF16), rep_c), 0.0).astype(BF16)
    i = lax.broadcasted_iota(jnp.int32, (LS, NS), 0).astype(F32)
    for slot, k in ((0, -i), (2, i), (4, i + 1.0), (6, LS * (i + 1.0))):
        pmag = jnp.exp(k * lam_re)
        tab_ref[0, slot] = pmag * jnp.cos(k * lam_im)
        tab_ref[0, slot + 1] = pmag * jnp.sin(k * lam_im)


def _s5_prepare(a_re, a_im, log_dt, b_re, b_im, c_re, c_im):
    depth = a_re.shape[0]
    are = a_re.reshape(depth, 1, NS)
    aim = a_im.reshape(depth, 1, NS)
    ldt = jnp.repeat(log_dt, S5P, axis=-1).reshape(depth, 1, NS)
    bre = jnp.transpose(b_re, (0, 3, 1, 2)).reshape(depth, S5C, NS)
    bim = jnp.transpose(b_im, (0, 3, 1, 2)).reshape(depth, S5C, NS)
    cre = jnp.transpose(c_re, (0, 1, 3, 2)).reshape(depth, NS, S5C)
    cim = jnp.transpose(c_im, (0, 1, 3, 2)).reshape(depth, NS, S5C)
    per_layer = lambda *shape: pl.BlockSpec((1,) + shape, lambda l: (l,) + (0,) * len(shape))
    out_shape = (jax.ShapeDtypeStruct((depth, 2 * NS, 128), F32),
                 jax.ShapeDtypeStruct((depth, GW, 2 * NS), BF16),
                 jax.ShapeDtypeStruct((depth, 2 * NS, GW), BF16),
                 jax.ShapeDtypeStruct((depth, NTAB, LS, NS), F32))
    return pl.pallas_call(
        _s5_prep_kernel, grid=(depth,),
        in_specs=[per_layer(1, NS), per_layer(1, NS), per_layer(1, NS),
                  per_layer(S5C, NS), per_layer(S5C, NS), per_layer(NS, S5C), per_layer(NS, S5C)],
        out_specs=(per_layer(2 * NS, 128), per_layer(GW, 2 * NS), per_layer(2 * NS, GW),
                   per_layer(NTAB, LS, NS)),
        out_shape=out_shape, name="s5_prepare",
        compiler_params=pltpu.CompilerParams(dimension_semantics=("arbitrary",)),
    )(are, aim, ldt, bre, bim, cre, cim)


N_STATE_OUT = 8


def _prompt_kernel(x_ref, mem_ref, normw_ref, win_ref, wg_ref, wout_ref, wmkv_ref,
                   cos_ref, s1_ref, s2_ref, dec_ref, qdec_ref, kdec_ref, cdec_ref,
                   retgn_ref, mlgn_ref, bi_ref, bf_ref,
                   tab_ref, bt_ref, ct_ref, dsk_ref, wglu_ref, fnw_ref,
                   *rest, last_layer, n_tblocks, n_prev):
    (y_ref, rets_ref, mlc_ref, mln_ref, mlm_ref, s5re_ref, s5im_ref, memk_ref, memv_ref,
     proj_ref, mix_ref, s_ref, c_ref, n_ref, m_ref, xre_ref, xim_ref,
     mk_ref, mv_ref, bu_ref, xcat_ref, car_ref) = rest[n_prev:]
    t = pl.program_id(1)
    hm = _head_masks()
    bd = _block_diag_mask()
    avg = _avg_matrix()
    lane128 = lax.broadcasted_iota(jnp.int32, (1, 128), 1)
    row_i = lax.broadcasted_iota(jnp.int32, (L, 128), 0)
    col_i = lax.broadcasted_iota(jnp.int32, (L, 128), 1)
    causal = row_i >= col_i
    tri_u = jnp.where(row_i <= col_i, 1.0, 0.0).astype(BF16)
    tri_sub = jnp.where(causal & (row_i // LS == col_i // LS), 1.0, 0.0)
    blk_sum = jnp.where(lax.broadcasted_iota(jnp.int32, (L // LS, 128), 0)
                        == lax.broadcasted_iota(jnp.int32, (L // LS, 128), 1) // LS, 1.0, 0.0)
    tri_ext = jnp.concatenate([tri_sub, blk_sum], axis=0).astype(BF16)

    @pl.when(t == 0)
    def _init():
        s_ref[...] = jnp.zeros_like(s_ref)
        c_ref[...] = jnp.zeros_like(c_ref)
        n_ref[...] = jnp.zeros_like(n_ref)
        m_ref[...] = jnp.zeros_like(m_ref)
        xre_ref[...] = jnp.zeros_like(xre_ref)
        xim_ref[...] = jnp.zeros_like(xim_ref)
        mkv = _dot(mem_ref[0].astype(BF16), wmkv_ref[...])
        mk = mkv[:, :GW]
        mv = mkv[:, GW:]
        memk_ref[0] = mk.T
        memv_ref[0] = mv.T
        mk_ref[...] = _stack_heads(mk.astype(BF16), hm)
        mv_ref[...] = _stack_heads(mv.astype(BF16), hm)

    x = x_ref[0]
    hn = _rms_norm(x, normw_ref[...]).astype(BF16)
    proj_ref[...] = _dot_nt(hn, win_ref[...])
    gates_t = _dot_nt(wg_ref[...], hn)

    nc = TB // L
    crow = [slice(c * L, (c + 1) * L) for c in range(nc)]

    def PB(blk, rows=slice(None)):
        return proj_ref[rows, blk * GW:(blk + 1) * GW]


    def xattn_stages():
        for piece in range(TB // XQ):
            rs = slice(piece * XQ, (piece + 1) * XQ)
            sc = _dot_nt(PB(AQ, rs).astype(BF16), mk_ref[...])
            yield
            ps = []
            for h in range(NH):
                seg = sc[:, h * MEM:(h + 1) * MEM]
                e = jnp.exp(seg - jnp.max(seg, axis=-1, keepdims=True))
                ps.append((e / jnp.sum(e, axis=-1, keepdims=True)).astype(BF16))
            p = jnp.concatenate(ps, axis=1)
            yield
            xa = _dot(p, mv_ref[...])
            yield
            mix_ref[rs, 3 * GW:4 * GW] = (xa * _silu(PB(AG, rs))).astype(BF16)
            yield

    def s5_stages():
        lanes = [(slice(j * 128, (j + 1) * 128), slice(NS + j * 128, NS + (j + 1) * 128))
                 for j in range(NS // 128)]
        bpc = L // LS
        nblk = TB // LS
        su = PB(SU)
        bu_ref[...] = _dot(su.astype(BF16), bt_ref[...])
        yield
        for c in range(nc + 1):
            if c < nc:
                for lre, lim in lanes:
                    br = bu_ref[crow[c], lre]
                    bi = bu_ref[crow[c], lim]
                    wr = jnp.tile(tab_ref[0, :, lre], (bpc, 1))
                    wi = jnp.tile(tab_ref[1, :, lre], (bpc, 1))
                    xcat_ref[crow[c], lre] = (wr * br - wi * bi).astype(BF16)
                    xcat_ref[crow[c], lim] = (wr * bi + wi * br).astype(BF16)
            if c > 0:
                z = _dot(tri_ext, xcat_ref[crow[c - 1], :])
                bu_ref[crow[c - 1], :] = z[0:L]
                car_ref[(c - 1) * bpc:c * bpc, :] = z[L:L + bpc]
            yield
        for _ in range(nc - 1):
            yield
        rowb = lax.broadcasted_iota(jnp.int32, (nblk, 128), 0)
        ers, eis, prs, pis, c0s = [], [], [], [], []
        for lre, lim in lanes:
            zr = car_ref[:, lre]
            zi = car_ref[:, lim]
            pr = tab_ref[2, LS - 1:LS, lre]
            pi = tab_ref[3, LS - 1:LS, lre]
            er = pr * zr - pi * zi
            ei = pr * zi + pi * zr
            pr = tab_ref[4, LS - 1:LS, lre]
            pi = tab_ref[5, LS - 1:LS, lre]
            c0r = xre_ref[0:1, lre]
            c0i = xim_ref[0:1, lre]
            ers.append(er + jnp.where(rowb == 0, pr * c0r - pi * c0i, 0.0))
            eis.append(ei + jnp.where(rowb == 0, pr * c0i + pi * c0r, 0.0))
            prs.append(pr)
            pis.append(pi)
            c0s.append((c0r, c0i))
        yield
        assert LS == 8 and nblk % 8 == 0
        ngrp = nblk // 8
        row8 = lax.broadcasted_iota(jnp.int32, (8, 128), 0)
        ers = [[v[g * 8:(g + 1) * 8] for g in range(ngrp)] for v in ers]
        eis = [[v[g * 8:(g + 1) * 8] for g in range(ngrp)] for v in eis]
        for k in range(3):
            d = 1 << k
            for j in range(len(lanes)):
                pr, pi = prs[j], pis[j]
                for g in range(ngrp):
                    er, ei = ers[j][g], eis[j][g]
                    sr = jnp.where(row8 >= d, pltpu.roll(er, d, 0), 0.0)
                    si = jnp.where(row8 >= d, pltpu.roll(ei, d, 0), 0.0)
                    ers[j][g], eis[j][g] = er + pr * sr - pi * si, ei + pr * si + pi * sr
                prs[j], pis[j] = pr * pr - pi * pi, 2.0 * (pr * pi)
            yield
        for g in range(1, ngrp):
            for j, (lre, lim) in enumerate(lanes):
                qr = tab_ref[6, :, lre]
                qi = tab_ref[7, :, lre]
                cr = jnp.broadcast_to(ers[j][g - 1][7:8, :], (8, 128))
                ci_ = jnp.broadcast_to(eis[j][g - 1][7:8, :], (8, 128))
                ers[j][g] = ers[j][g] + (qr * cr - qi * ci_)
                eis[j][g] = eis[j][g] + (qr * ci_ + qi * cr)
            if g % 3 == 0 or g == ngrp - 1:
                yield
        ers = [jnp.concatenate(v, axis=0) for v in ers]
        eis = [jnp.concatenate(v, axis=0) for v in eis]
        for j, (lre, lim) in enumerate(lanes):
            xre_ref[:, lre] = jnp.broadcast_to(ers[j][nblk - 1:nblk, :], (8, 128))
            xim_ref[:, lre] = jnp.broadcast_to(eis[j][nblk - 1:nblk, :], (8, 128))
            cr = jnp.where(rowb == 0, c0s[j][0], pltpu.roll(ers[j], 1, 0))
            ci_ = jnp.where(rowb == 0, c0s[j][1], pltpu.roll(eis[j], 1, 0))
            ar = tab_ref[4, 0:1, lre]
            ai = tab_ref[5, 0:1, lre]
            car_ref[:, lre] = ar * cr - ai * ci_
            car_ref[:, lim] = ar * ci_ + ai * cr
        yield
        for c in range(nc):
            for lre, lim in lanes:
                cr = jnp.concatenate(
                    [jnp.broadcast_to(car_ref[c * bpc + j:c * bpc + j + 1, lre], (LS, 128))
                     for j in range(bpc)], axis=0)
                ci_ = jnp.concatenate(
                    [jnp.broadcast_to(car_ref[c * bpc + j:c * bpc + j + 1, lim], (LS, 128))
                     for j in range(bpc)], axis=0)
                zr = bu_ref[crow[c], lre] + cr
                zi = bu_ref[crow[c], lim] + ci_
                pr = jnp.tile(tab_ref[2, :, lre], (bpc, 1))
                pi = jnp.tile(tab_ref[3, :, lre], (bpc, 1))
                xcat_ref[crow[c], lre] = (pr * zr - pi * zi).astype(BF16)
                xcat_ref[crow[c], lim] = (pr * zi + pi * zr).astype(BF16)
            yield
        sy = _dot(xcat_ref[...], ct_ref[...]) + dsk_ref[...] * su
        yield
        sy = _gelu_tanh(sy)
        gate = _dot(sy.astype(BF16), wglu_ref[...])
        yield
        mix_ref[:, 2 * GW:3 * GW] = (sy * _sigmoid(gate) * _silu(PB(SG))).astype(BF16)
        yield

    def retention_stages():
        rq = _rope(PB(RQ), cos_ref[...], s1_ref[...], s2_ref[...])
        rk = _rope(PB(RK), cos_ref[...], s1_ref[...], s2_ref[...])
        rvf = PB(RV)
        rv = rvf.astype(BF16)
        rqb = rq.astype(BF16)
        rkb = rk.astype(BF16)
        kst = [_stack_heads(rkb[crow[c]], hm) for c in range(nc)]
        vst = [_stack_heads(rv[crow[c]], hm) for c in range(nc)]
        yield
        inner = [_dot_nt(rqb[crow[c]], kst[c]) for c in range(nc)]
        yield
        pmat = [(inner[c] * dec_ref[...]).astype(BF16) for c in range(nc)]
        kd = [(rk[crow[c]] * kdec_ref[...]).astype(BF16) for c in range(nc)]
        yield
        rloc = [_dot(pmat[c], vst[c]) for c in range(nc)]
        upd = [_dot_tn(kd[c], rv[crow[c]]) for c in range(nc)]
        yield
        st = [s_ref[...]]
        for c in range(nc):
            st.append(st[c] * cdec_ref[...] + jnp.where(bd, upd[c], 0.0))
        s_ref[...] = st[nc]
        qd = [(rq[crow[c]] * qdec_ref[...]).astype(BF16) for c in range(nc)]
        yield
        ost = [_dot(qd[c], st[c].astype(BF16)) for c in range(nc)]
        yield
        ro = jnp.concatenate([rloc[c] + ost[c] for c in range(nc)], axis=0)
        mix_ref[:, 0:GW] = (_head_norm(ro, retgn_ref[...], avg) * _silu(PB(RG))).astype(BF16)
        yield

    def mlstm_stages():
        ig_r = gates_t[0:8] + bi_ref[...]
        lf_r = _log_sigmoid(gates_t[8:16] + bf_ref[...])
        yield
        b_r = [_dot_x3(lf_r[:, crow[c]], tri_u) for c in range(nc)]
        yield
        gd_r = [ig_r[:, crow[c]] - b_r[c] for c in range(nc)]
        lane_r = lax.broadcasted_iota(jnp.int32, (8, L), 1)
        cm_r = list(gd_r)
        for k in range(7):
            d = 1 << k
            cm_r = [jnp.maximum(v, jnp.where(lane_r >= d, pltpu.roll(v, d, 1), NEG_INF))
                    for v in cm_r]
            yield
        m_prev = [m_ref[...]]
        mt_r = []
        for c in range(nc):
            mt = jnp.maximum(b_r[c] + m_prev[c], b_r[c] + cm_r[c])
            mt_r.append(mt)
            m_prev.append(jnp.broadcast_to(mt[:, L - 1:L], (8, L)))
            yield
        m_ref[...] = m_prev[nc]
        cols = []
        for c in range(nc):
            bm = b_r[c] - mt_r[c]
            ws = jnp.exp(b_r[c] + m_prev[c] - mt_r[c])
            wl = jnp.exp(gd_r[c] + jnp.broadcast_to(bm[:, L - 1:L], (8, L)))
            emt = jnp.exp(-mt_r[c])
            cols.append(jnp.concatenate([bm, ws, wl, emt, jnp.zeros((L - 32, L), F32)], axis=0).T)
        mqf = PB(MQ)
        mq = mqf.astype(BF16)
        mkf = PB(MK)
        mvf = PB(MV)
        mv_ = mvf.astype(BF16)
        mkb = mkf.astype(BF16)
        kst = [_stack_heads(mkb[crow[c]], hm) for c in range(nc)]
        vst = [_stack_heads(mv_[crow[c]], hm) for c in range(nc)]
        yield
        sraw = [_dot_nt(mq[crow[c]], kst[c]) for c in range(nc)]
        yield
        smat, den_i = [], []
        for c in range(nc):
            parts, dens = [], []
            for h in range(NH):
                arg = jnp.where(causal, cols[c][:, h:h + 1] + gd_r[c][h:h + 1, :], NEG_INF)
                s_h = sraw[c][:, h * L:(h + 1) * L] * jnp.exp(arg)
                dens.append(jnp.sum(s_h, axis=-1, keepdims=True))
                parts.append(s_h.astype(BF16))
            den_i.append(dens)
            smat.append(jnp.concatenate(parts, axis=1))
            yield
        ones_blk = jnp.ones((L, 128), BF16)
        def per_head(columns):
            first = lane128 < DH
            return jnp.concatenate([jnp.where(first, columns[0], columns[1]),
                                    jnp.where(first, columns[2], columns[3])], axis=1)

        kws = [(mkf[crow[c]] * per_head([cols[c][:, 16 + h:17 + h] for h in range(NH)])
                ).astype(BF16) for c in range(nc)]
        yield
        rloc = [_dot(smat[c], vst[c]) for c in range(nc)]
        u = [_dot_tn(kws[c], jnp.concatenate([mv_[crow[c]], ones_blk], axis=1))
             for c in range(nc)]
        yield
        nmask = (lax.broadcasted_iota(jnp.int32, (GW, 128), 0) // DH
                 == lax.broadcasted_iota(jnp.int32, (GW, 128), 1))
        cst = [c_ref[...]]
        nst = [n_ref[...]]
        for c in range(nc):
            wsl256 = jnp.zeros((1, GW), F32)
            wsl128 = jnp.zeros((1, 128), F32)
            for h in range(NH):
                wsl = cols[c][L - 1:L, 8 + h:9 + h]
                wsl256 = wsl256 + jnp.where(hm[h], wsl, 0.0)
                wsl128 = wsl128 + jnp.where(lane128 == h, wsl, 0.0)
            cst.append(cst[c] * wsl256 + jnp.where(bd, u[c][:, :GW], 0.0))
            nst.append(nst[c] * wsl128 + jnp.where(nmask, u[c][:, GW:], 0.0))
        c_ref[...] = cst[nc]
        n_ref[...] = nst[nc]
        yield
        qc = [_dot(mq[crow[c]], cst[c].astype(BF16)) for c in range(nc)]
        qn = [_dot(mq[crow[c]], nst[c].astype(BF16)) for c in range(nc)]
        yield
        mhs = []
        for c in range(nc):
            wsc = [cols[c][:, 8 + h:9 + h] for h in range(NH)]
            rdn = []
            for h in range(NH):
                den = den_i[c][h] + wsc[h] * qn[c][:, h:h + 1]
                rdn.append(1.0 / jnp.maximum(jnp.abs(den), cols[c][:, 24 + h:25 + h]))
            mhs.append((rloc[c] + per_head(wsc) * qc[c]) * per_head(rdn))
            yield
        mh = jnp.concatenate(mhs, axis=0) * _sigmoid(PB(MO))
        mix_ref[:, GW:2 * GW] = (_head_norm(mh, mlgn_ref[...], avg) * _silu(PB(MG))).astype(BF16)
        yield

    pending = [mlstm_stages(), s5_stages(), retention_stages(), xattn_stages()]
    while pending:
        pending = [stage for stage in pending if next(stage, _DONE) is not _DONE]

    y = x_ref[0] + _dot(mix_ref[...], wout_ref[...])
    if last_layer:
        y = _rms_norm(y, fnw_ref[...])
    y_ref[0] = y

    @pl.when(t == n_tblocks - 1)
    def _final():
        s_fin = s_ref[...]
        c_fin = c_ref[...].T
        n_fin = n_ref[...].T
        for h in range(NH):
            blk = slice(h * DH, (h + 1) * DH)
            rets_ref[0, h] = s_fin[blk, blk]
            mlc_ref[0, h] = c_fin[blk, blk]
        mln_ref[0] = jnp.concatenate([n_fin[h:h + 1, h * DH:(h + 1) * DH] for h in range(NH)],
                                     axis=0)
        mlm_ref[0] = m_ref[...]
        s5re_ref[0] = xre_ref[...]
        s5im_ref[0] = xim_ref[...]


def _prompt_layer(layer, x, mem, w, consts, last_layer, prev):
    bsz, seq, _ = x.shape
    depth = w["w_in"].shape[0]
    nt = seq // TB
    full = lambda shape: pl.BlockSpec(shape, lambda b, t: (0,) * len(shape),
                                      pipeline_mode=pl.Buffered(1))
    lyr = lambda shape: pl.BlockSpec((None,) + shape, lambda b, t: (layer,) + (0,) * len(shape),
                                     pipeline_mode=pl.Buffered(1))
    tok = lambda width: pl.BlockSpec((TB, width), lambda b, t: (t, 0))
    per_b = lambda r, c: pl.BlockSpec((1, r, c), lambda b, t: (b, 0, 0))
    in_specs = [
        pl.BlockSpec((1, TB, D), lambda b, t: (b, t, 0)),
        per_b(MEM, D),
        lyr((1, D)), lyr((DP, D)), lyr((NGATE, D)), lyr((D, D)), lyr((D, 2 * GW)),
        tok(128), tok(128), tok(128),
        full((L, NH * L)), full((L, GW)), full((L, GW)), full((1, GW)),
        lyr((1, GW)), lyr((1, GW)), lyr((8, TB)), lyr((8, TB)),
        lyr((NTAB, LS, NS)), lyr((GW, 2 * NS)), lyr((2 * NS, GW)),
        lyr((1, GW)), lyr((GW, GW)), full((1, D)),
    ] + [pl.BlockSpec(memory_space=pl.ANY)] * len(prev)
    n_in = len(in_specs) - len(prev)
    state_shapes = [(NH, DH, DH),
                    (NH, DH, DH),
                    (NH, DH),
                    (8, 128),
                    (8, NS),
                    (8, NS),
                    (GW, MEM),
                    (GW, MEM)]
    assert len(state_shapes) == N_STATE_OUT and len(prev) in (0, N_STATE_OUT)
    out_shape = ((jax.ShapeDtypeStruct((bsz, seq, D), F32),)
                 + tuple(jax.ShapeDtypeStruct((depth, bsz) + s, F32) for s in state_shapes))
    state_spec = lambda s: pl.BlockSpec((None, 1) + s,
                                        lambda b, t: (layer, b) + (0,) * len(s))
    out_specs = ((pl.BlockSpec((1, TB, D), lambda b, t: (b, t, 0)),)
                 + tuple(state_spec(s) for s in state_shapes))
    scratch = [
        pltpu.VMEM((TB, DP), F32),
        pltpu.VMEM((TB, D), BF16),
        pltpu.VMEM((GW, GW), F32),
        pltpu.VMEM((GW, GW), F32),
        pltpu.VMEM((GW, 128), F32),
        pltpu.VMEM((8, 128), F32),
        pltpu.VMEM((8, NS), F32),
        pltpu.VMEM((8, NS), F32),
        pltpu.VMEM((NH * MEM, GW), BF16),
        pltpu.VMEM((NH * MEM, GW), BF16),
        pltpu.VMEM((TB, 2 * NS), F32),
        pltpu.VMEM((TB, 2 * NS), BF16),
        pltpu.VMEM((TB // LS, 2 * NS), F32),
    ]
    kern = functools.partial(_prompt_kernel, last_layer=last_layer, n_tblocks=nt, n_prev=len(prev))
    return pl.pallas_call(
        kern, grid=(bsz, nt), in_specs=in_specs, out_specs=out_specs, out_shape=out_shape,
        scratch_shapes=scratch, name="prompt_layer",
        input_output_aliases={n_in + i: 1 + i for i in range(len(prev))},
        compiler_params=pltpu.CompilerParams(
            dimension_semantics=("arbitrary", "arbitrary"), vmem_limit_bytes=VMEM_LIMIT),
    )(x, mem, w["norm_w"], w["w_in"], w["w_gate"], w["w_out"], w["w_mem_kv"],
      consts["cos_p"], consts["s1_p"], consts["s2_p"],
      consts["dec"], consts["qdec"], consts["kdec"], consts["cdec"],
      w["ret_gn"], w["ml_gn"], w["b_i8"], w["b_f8"],
      w["tab"], w["bt"], w["ct"], w["s5_d"], w["w_glu"], w["final_norm_w"], *prev)


SROWS = 16


def _sample_kernel_t(x_ref, normw_ref, win_ref, wg_ref, wout_ref,
                     cos_ref, s1_ref, s2_ref, gam_ref, gam8_ref,
                     retgn_ref, mlgn_ref, bi_ref, bf_ref,
                     abt_ref, bt_ref, ct_ref, dsk_ref, wglu_ref, fnw_ref,
                     m0_ref, n0_ref, x0re_ref, x0im_ref,
                     rets_ref, mlc_ref, ck_ref, cv_ref,
                     y_ref, retn_ref, mlcn_ref, mlnn_ref, mlmn_ref, s5re_ref, s5im_ref,
                     hs_ref, proj_ref, qgt_ref, rkt_ref, rvt_ref, mqt_ref, mkt_ref, mvt_ref,
                     wi_ref, ws_ref, ot_ref, cqt_ref, qa8_ref, xa_ref,
                     *, n_layers, n_blocks):
    layer = pl.program_id(0)
    g = pl.program_id(1)
    nsamp = x_ref.shape[0]
    ones_bd = _ones_matrix()
    avg = _avg_matrix()
    mask8 = (lax.broadcasted_iota(jnp.int32, (8, GW), 0)
             == lax.broadcasted_iota(jnp.int32, (8, GW), 1) // DH)

    def P(blk):
        return proj_ref[:, blk * GW:(blk + 1) * GW]

    @pl.when((layer == 0) & (g == 0))
    def _load_x():
        hs_ref[...] = x_ref[...]

    @pl.when(g == 0)
    def _pre():
        hn = _rms_norm(hs_ref[...], normw_ref[...]).astype(BF16)
        proj_ref[...] = _dot_nt(hn, win_ref[...])
        cosr, s1, s2 = cos_ref[...], s1_ref[...], s2_ref[...]
        qgt_ref[...] = (_rope(P(RQ), cosr, s1, s2) * gam_ref[...]).T
        rkt_ref[...] = _rope(P(RK), cosr, s1, s2).T
        rvt_ref[...] = P(RV).T
        mqt_ref[...] = P(MQ).T
        mkt_ref[...] = P(MK).T
        mvt_ref[...] = P(MV).T
        gt = _dot_nt(wg_ref[...], hn)
        ig = gt[0:8] + bi_ref[...]
        lf = _log_sigmoid(gt[8:16] + bf_ref[...])
        a = lf + m0_ref[...]
        mt = jnp.maximum(a, ig)
        wi_ref[...] = jnp.exp(ig - mt)
        ws_ref[...] = jnp.exp(a - mt)
        mlmn_ref[...] = mt
        ot_ref[...] = jnp.zeros_like(ot_ref)
        qa8_ref[...] = jnp.where(mask8[None], P(AQ)[:, None, :], 0.0).reshape(8 * nsamp, GW)
        but = _dot_tn(bt_ref[...], P(SU).T.astype(BF16))
        are, aim = abt_ref[0:NS, :], abt_ref[NS:2 * NS, :]
        x0r, x0i = x0re_ref[...], x0im_ref[...]
        s5re_ref[...] = are * x0r - aim * x0i + but[0:NS]
        s5im_ref[...] = are * x0i + aim * x0r + but[NS:2 * NS]

    head = g // (GW // SROWS // NH)
    hrow = pl.ds(pl.multiple_of(head * DH, DH), DH)
    gam_row = gam8_ref[pl.ds(head, 1), :]
    ws_row = ws_ref[pl.ds(head, 1), :]
    wi_row = wi_ref[pl.ds(head, 1), :]
    v_slab = rvt_ref[hrow, :]
    q_slab = mqt_ref[hrow, :]
    k_slab = mkt_ref[hrow, :]
    o_acc = jnp.zeros((DH, nsamp), F32)
    for i in range(SROWS):
        r = pl.ds(g * SROWS + i, 1)
        s_t = rets_ref[i]
        o_acc = o_acc + qgt_ref[r, :] * s_t
        retn_ref[i] = gam_row * s_t + rkt_ref[r, :] * v_slab
        c_t = mlc_ref[i]
        cqt_ref[r, :] = jnp.sum(c_t * q_slab, axis=0, keepdims=True)
        mlcn_ref[i] = ws_row * c_t + (wi_row * mvt_ref[r, :]) * k_slab
    ot_ref[hrow, :] = ot_ref[hrow, :] + o_acc

    tiles = [pl.ds(pl.multiple_of((g * SB + i) * 8, 8), 8) for i in range(SB)]
    q8 = [qa8_ref[tiles[i], :].astype(BF16) for i in range(SB)]
    sc = [_dot(q8[i], ck_ref[i].astype(BF16)) for i in range(SB)]
    ps = []
    for i in range(SB):
        e = jnp.exp(sc[i] - jnp.max(sc[i], axis=-1, keepdims=True))
        ps.append((e / jnp.sum(e, axis=-1, keepdims=True)).astype(BF16))
    ta = [_dot_nt(ps[i], cv_ref[i].astype(BF16)) for i in range(SB)]
    xa_ref[pl.ds(pl.multiple_of(g * SB, SB), SB), :] = jnp.concatenate(
        [jnp.sum(jnp.where(mask8, ta[i], 0.0), axis=0, keepdims=True) for i in range(SB)], axis=0)

    @pl.when(g == n_blocks - 1)
    def _post():
        cosr, s1, s2 = cos_ref[...], s1_ref[...], s2_ref[...]
        rq = _rope(P(RQ), cosr, s1, s2)
        rk = _rope(P(RK), cosr, s1, s2)
        ro = _dot_x2(rq * rk, ones_bd) * P(RV) + ot_ref[...].T
        ret_out = _head_norm(ro, retgn_ref[...], avg) * _silu(P(RG))
        mht = []
        for h in range(NH):
            rows = slice(h * DH, (h + 1) * DH)
            wi = wi_ref[h:h + 1, :]
            ws = ws_ref[h:h + 1, :]
            emt = jnp.exp(-mlmn_ref[h:h + 1, :])
            qt, kt, n0 = mqt_ref[rows, :], mkt_ref[rows, :], n0_ref[rows, :]
            s = jnp.sum(qt * kt, axis=0, keepdims=True) * wi
            den = s + ws * jnp.sum(n0 * qt, axis=0, keepdims=True)
            mht.append((s * mvt_ref[rows, :] + ws * cqt_ref[rows, :])
                       / jnp.maximum(jnp.abs(den), emt))
            mlnn_ref[rows, :] = ws * n0 + wi * kt
        mh = jnp.concatenate(mht, axis=0).T * _sigmoid(P(MO))
        ml_out = _head_norm(mh, mlgn_ref[...], avg) * _silu(P(MG))
        su = P(SU)
        xcat = jnp.concatenate([s5re_ref[...], s5im_ref[...]], axis=0).astype(BF16)
        sy = _dot_tn(xcat, ct_ref[...]) + dsk_ref[...] * su
        sy = _gelu_tanh(sy)
        sy = sy * _sigmoid(_dot(sy.astype(BF16), wglu_ref[...]))
        s5_out = sy * _silu(P(SG))
        xa_out = xa_ref[...] * _silu(P(AG))
        mix = jnp.concatenate([ret_out, ml_out, s5_out, xa_out], axis=1).astype(BF16)
        y = hs_ref[...] + _dot(mix, wout_ref[...])
        hs_ref[...] = y

        @pl.when(layer == n_layers - 1)
        def _emit():
            y_ref[...] = _rms_norm(y, fnw_ref[...])


def _sample_layers_t(x, st, w, consts):
    nsamp = x.shape[0]
    depth = w["w_in"].shape[0]
    nb = GW // SROWS
    assert nsamp == nb * SB and nsamp == 128
    once = lambda shape: pl.BlockSpec(shape, lambda l, g: (0,) * len(shape),
                                      pipeline_mode=pl.Buffered(1))
    lyr = lambda shape: pl.BlockSpec((None,) + shape, lambda l, g: (l,) + (0,) * len(shape))
    lyr_out = lyr
    srows = pl.BlockSpec((None, SROWS, DH, nsamp), lambda l, g: (l, g, 0, 0))
    cache = pl.BlockSpec((None, SB, GW, MEM), lambda l, g: (l, g, 0, 0))
    in_specs = [
        once((nsamp, D)), lyr((1, D)), lyr((DP, D)), lyr((NGATE, D)), lyr((D, D)),
        once((1, 128)), once((1, 128)), once((1, 128)), once((1, GW)), once((8, nsamp)),
        lyr((1, GW)), lyr((1, GW)), lyr((8, nsamp)), lyr((8, nsamp)),
        lyr((2 * NS, nsamp)), lyr((GW, 2 * NS)), lyr((2 * NS, GW)),
        lyr((1, GW)), lyr((GW, GW)), once((1, D)),
        lyr((8, nsamp)), lyr((GW, nsamp)), lyr((NS, nsamp)), lyr((NS, nsamp)),
        srows, srows, cache, cache,
    ]
    out_shape = (
        jax.ShapeDtypeStruct((nsamp, D), F32),
        jax.ShapeDtypeStruct((depth, GW, DH, nsamp), F32),
        jax.ShapeDtypeStruct((depth, GW, DH, nsamp), F32),
        jax.ShapeDtypeStruct((depth, GW, nsamp), F32),
        jax.ShapeDtypeStruct((depth, 8, nsamp), F32),
        jax.ShapeDtypeStruct((depth, NS, nsamp), F32),
        jax.ShapeDtypeStruct((depth, NS, nsamp), F32),
    )
    out_specs = (
        pl.BlockSpec((nsamp, D), lambda l, g: (0, 0)), srows, srows,
        lyr_out((GW, nsamp)), lyr_out((8, nsamp)), lyr_out((NS, nsamp)), lyr_out((NS, nsamp)),
    )
    scratch = [
        pltpu.VMEM((nsamp, D), F32),
        pltpu.VMEM((nsamp, DP), F32),
        pltpu.VMEM((GW, nsamp), F32), pltpu.VMEM((GW, nsamp), F32), pltpu.VMEM((GW, nsamp), F32),
        pltpu.VMEM((GW, nsamp), F32), pltpu.VMEM((GW, nsamp), F32), pltpu.VMEM((GW, nsamp), F32),
        pltpu.VMEM((8, nsamp), F32), pltpu.VMEM((8, nsamp), F32),
        pltpu.VMEM((GW, nsamp), F32), pltpu.VMEM((GW, nsamp), F32),
        pltpu.VMEM((8 * nsamp, GW), F32), pltpu.VMEM((nsamp, GW), F32),
    ]
    kern = functools.partial(_sample_kernel_t, n_layers=depth, n_blocks=nb)
    return pl.pallas_call(
        kern, grid=(depth, nb), in_specs=in_specs, out_specs=out_specs, out_shape=out_shape,
        scratch_shapes=scratch, name="sample_layers",
        compiler_params=pltpu.CompilerParams(
            dimension_semantics=("arbitrary", "arbitrary"), vmem_limit_bytes=VMEM_LIMIT),
    )(x, w["norm_w"], w["w_in"], w["w_gate"], w["w_out"],
      consts["cos_s"], consts["s1_s"], consts["s2_s"], consts["gam"], consts["gam8"],
      w["ret_gn"], w["ml_gn"], w["b_i8"], w["b_f8"],
      w["abt"], w["bt"], w["ct"], w["s5_d"], w["w_glu"], w["final_norm_w"],
      st["m"], st["n"], st["s5_re"], st["s5_im"], st["ret"], st["c"], st["mem_k"], st["mem_v"])


def _rope_tables(pos):
    half = DH // 2
    inv = ROPE_BASE ** (-np.arange(half, dtype=np.float64) / half)
    ang = np.asarray(pos, np.float64)[:, None] * inv[None, :]
    cos, sin = np.cos(ang), np.sin(ang)
    zero = np.zeros_like(sin)
    c = np.tile(np.concatenate([cos, cos], axis=-1), (1, 2))
    s1 = np.tile(np.concatenate([zero, sin], axis=-1), (1, 2))
    s2 = np.tile(np.concatenate([-sin, zero], axis=-1), (1, 2))
    return tuple(jnp.asarray(t, F32) for t in (c, s1, s2))


def _constants(seq):
    lg = np.log1p(-np.power(2.0, -5.0 - np.arange(NH, dtype=np.float64)))[:, None]
    idx = np.arange(L, dtype=np.float64)
    diff = idx[:, None] - idx[None, :]
    decay = np.where(diff >= 0, np.exp(lg[:, :, None] * np.maximum(diff, 0.0)), 0.0)
    rep = lambda t: np.repeat(t, DH, axis=0).T
    consts = {
        "dec": np.transpose(decay, (1, 0, 2)).reshape(L, NH * L),
        "qdec": rep(np.exp(lg * (idx + 1.0))),
        "kdec": rep(np.exp(lg * (L - 1.0 - idx))),
        "cdec": rep(np.exp(lg * L)),
        "gam": rep(np.exp(lg * 1.0)),
        "gam8": np.pad(np.broadcast_to(np.exp(lg), (NH, 128)), ((0, 8 - NH), (0, 0))),
    }
    consts = {k: jnp.asarray(v, F32) for k, v in consts.items()}
    consts["cos_p"], consts["s1_p"], consts["s2_p"] = _rope_tables(np.arange(seq))
    consts["cos_s"], consts["s1_s"], consts["s2_s"] = _rope_tables(PAST_LEN + np.arange(1))
    return consts


def _pack_w_in(w_in):
    wt = jnp.swapaxes(w_in, 1, 2)
    sizes = (GW,) * 9 + (NH, NH) + (GW,) * 4
    offs = np.concatenate([[0], np.cumsum(sizes)])
    seg = [wt[:, int(offs[i]):int(offs[i + 1]), :] for i in range(len(sizes))]
    scale = DH ** -0.5
    pad8 = lambda t: jnp.pad(t, ((0, 0), (0, 8 - t.shape[1]), (0, 0)))
    blocks = [seg[11],
              seg[0], seg[1] * scale, seg[2], seg[3],
              seg[4], seg[5] * scale, seg[6], seg[7], seg[8],
              seg[12], seg[13] * scale, seg[14]]
    gates = jnp.concatenate([pad8(seg[9]), pad8(seg[10])], axis=1)
    return jnp.concatenate(blocks, axis=1).astype(BF16), gates.astype(BF16)


def kernel(x_prompt, x_sample, mem_prompt, state_ret, state_mlstm_c, state_mlstm_n, state_mlstm_m,
           state_s5_re, state_s5_im, cache_mem_k, cache_mem_v,
           norm_w, w_in, ret_gn, ml_b_i, ml_b_f, ml_gn,
           s5_a_re, s5_a_im, s5_log_dt, s5_b_re, s5_b_im, s5_c_re, s5_c_im, s5_d, s5_w_glu,
           w_mem_k, w_mem_v, w_out, final_norm_w):
    depth = norm_w.shape[0]
    bp, seq, _ = x_prompt.shape
    bs = x_sample.shape[0]
    consts = _constants(seq)
    abt, bt, ct, tab = _s5_prepare(s5_a_re, s5_a_im, s5_log_dt, s5_b_re, s5_b_im, s5_c_re, s5_c_im)
    rows8 = lambda t: jnp.pad(jnp.broadcast_to(t[:, :, None], (depth, NH, TB)),
                              ((0, 0), (0, 8 - NH), (0, 0)))
    w_main, w_gate = _pack_w_in(w_in)
    w = {
        "norm_w": norm_w[:, None], "w_in": w_main, "w_gate": w_gate, "w_out": w_out.astype(BF16),
        "w_mem_kv": jnp.concatenate([w_mem_k, w_mem_v], axis=-1).astype(BF16),
        "ret_gn": ret_gn[:, None], "ml_gn": ml_gn[:, None],
        "b_i8": rows8(ml_b_i), "b_f8": rows8(ml_b_f),
        "abt": abt, "tab": tab, "bt": bt, "ct": ct,
        "s5_d": s5_d[:, None], "w_glu": s5_w_glu.astype(BF16),
        "final_norm_w": final_norm_w[None],
    }

    hp = x_prompt
    states = ()
    for l in range(depth):
        hp, *states = _prompt_layer(l, hp, mem_prompt, w, consts, l == depth - 1, tuple(states))
    ret_p, mlc_p, mln_p, ms, xr, xi, mk, mv = states
    mlm_p = ms[:, :, :NH, 0]
    s5re_p = xr[:, :, 0].reshape(depth, bp, S5G, S5P)
    s5im_p = xi[:, :, 0].reshape(depth, bp, S5G, S5P)
    memk_p = jnp.transpose(mk.reshape(depth, bp, NH, DH, MEM), (0, 1, 4, 2, 3))
    memv_p = jnp.transpose(mv.reshape(depth, bp, NH, DH, MEM), (0, 1, 4, 2, 3))

    st = {
        "m": jnp.pad(jnp.swapaxes(state_mlstm_m, 1, 2), ((0, 0), (0, 8 - NH), (0, 0))),
        "n": jnp.transpose(state_mlstm_n, (0, 2, 3, 1)).reshape(depth, GW, bs),
        "s5_re": jnp.transpose(state_s5_re, (0, 2, 3, 1)).reshape(depth, NS, bs),
        "s5_im": jnp.transpose(state_s5_im, (0, 2, 3, 1)).reshape(depth, NS, bs),
        "ret": jnp.transpose(state_ret, (0, 2, 3, 4, 1)).reshape(depth, GW, DH, bs),
        "c": jnp.transpose(state_mlstm_c, (0, 2, 3, 4, 1)).reshape(depth, GW, DH, bs),
        "mem_k": jnp.transpose(cache_mem_k, (0, 1, 3, 4, 2)).reshape(depth, bs, GW, MEM),
        "mem_v": jnp.transpose(cache_mem_v, (0, 1, 3, 4, 2)).reshape(depth, bs, GW, MEM),
    }
    hs, rn, cn, nn, mn, sr, si = _sample_layers_t(x_sample.reshape(bs, D), st, w, consts)
    back5 = lambda t: jnp.transpose(t.reshape(depth, NH, DH, DH, bs), (0, 4, 1, 2, 3))
    back4 = lambda t, a, b: jnp.transpose(t.reshape(depth, a, b, bs), (0, 3, 1, 2))
    return (hp, hs.reshape(bs, 1, D),
            ret_p, back5(rn), mlc_p, back5(cn),
            mln_p, back4(nn, NH, DH), mlm_p, jnp.swapaxes(mn[:, :NH], 1, 2),
            s5re_p, back4(sr, S5G, S5P), s5im_p, back4(si, S5G, S5P),
            memk_p, memv_p)
```

```python
import functools
import math

import numpy as np
import jax
import jax.numpy as jnp
from jax import lax
from jax.experimental import pallas as pl
from jax.experimental.pallas import tpu as pltpu

F32 = jnp.float32
BF16 = jnp.bfloat16

D = 1024
GW = 256
NH = 4
DH = 64
L = 128
LS = 8
NTAB = 8
S5G = 16
S5P = 64
S5C = 16
NS = S5G * S5P
MEM = 256
EPS = 1e-6
NEG_INF = -1e30
ROPE_BASE = 10000.0
PAST_LEN = 16384

TB = 512
SB = 8
XQ = 512
NBLK = 13
DP = NBLK * GW
(SU, RQ, RK, RV, RG, MQ, MK, MV, MO, MG, SG, AQ, AG) = range(NBLK)
NGATE = 16

VMEM_LIMIT = 56 * 1024 * 1024


_DONE = object()


def _dot(a, b):
    return jnp.dot(a, b, preferred_element_type=F32)


def _dot_nt(a, b):
    return lax.dot_general(a, b, (((1,), (1,)), ((), ())), preferred_element_type=F32)


def _dot_tn(a, b):
    return lax.dot_general(a, b, (((0,), (0,)), ((), ())), preferred_element_type=F32)


def _split2(x):
    hi = x.astype(BF16)
    lo = (x - hi.astype(F32)).astype(BF16)
    return hi, lo


def _dot_x2(x, w):
    hi, lo = _split2(x)
    return _dot(hi, w) + _dot(lo, w)


def _dot_x3(x, w):
    hi = x.astype(BF16)
    r = x - hi.astype(F32)
    mid = r.astype(BF16)
    lo = (r - mid.astype(F32)).astype(BF16)
    return _dot(hi, w) + _dot(mid, w) + _dot(lo, w)


def _sigmoid(x):
    return 0.5 * (1.0 + jnp.tanh(0.5 * x))


def _silu(x):
    h = 0.5 * x
    return h + h * jnp.tanh(h)


def _log_sigmoid(x):
    return jnp.minimum(x, 0.0) - jnp.log1p(jnp.exp(-jnp.abs(x)))


def _gelu_tanh(x):
    c = math.sqrt(2.0 / math.pi)
    h = 0.5 * x
    return h + h * jnp.tanh(x * (c + (0.044715 * c) * (x * x)))


def _lane_head(n):
    return lax.broadcasted_iota(jnp.int32, (1, n), 1) // DH


def _head_masks():
    lh = _lane_head(GW)
    return [lh == h for h in range(NH)]


def _block_diag_mask():
    r = lax.broadcasted_iota(jnp.int32, (GW, GW), 0) // DH
    c = lax.broadcasted_iota(jnp.int32, (GW, GW), 1) // DH
    return r == c


def _avg_matrix():
    return jnp.where(_block_diag_mask(), 1.0 / DH, 0.0).astype(BF16)


def _ones_matrix():
    return jnp.where(_block_diag_mask(), 1.0, 0.0).astype(BF16)


def _rope(x, cos, s1, s2):
    outs = []
    for j in range(2):
        xs = x[:, j * 128:(j + 1) * 128]
        outs.append(xs * cos + pltpu.roll(xs, 32, 1) * s1 + pltpu.roll(xs, 96, 1) * s2)
    return jnp.concatenate(outs, axis=1)


def _head_norm(x, gain, avg):
    mu = _dot_x2(x, avg)
    d = x - mu
    var = _dot((d * d).astype(BF16), avg)
    return d * lax.rsqrt(var + EPS) * gain


def _rms_norm(x, w):
    ms = jnp.mean(x * x, axis=-1, keepdims=True)
    return x * lax.rsqrt(ms + EPS) * w


def _stack_heads(x, hm):
    zero = jnp.zeros((x.shape[0], 128), x.dtype)
    blocks = []
    for h in range(NH):
        half = slice((h * DH) // 128 * 128, (h * DH) // 128 * 128 + 128)
        kept = jnp.where(hm[h][:, half], x[:, half], 0.0)
        blocks.append(jnp.concatenate([kept, zero] if half.start == 0 else [zero, kept], axis=1))
    return jnp.concatenate(blocks, axis=0)


def _s5_prep_kernel(are_ref, aim_ref, ldt_ref, bre_ref, bim_ref, cre_ref, cim_ref,
                    abt_ref, bt_ref, ct_ref, tab_ref):
    a_re = are_ref[0]
    a_im = aim_ref[0]
    dt = jnp.exp(ldt_ref[0])
    lam_re = a_re * dt
    lam_im = a_im * dt
    mag = jnp.exp(lam_re)
    ab_re = mag * jnp.cos(lam_im)
    ab_im = mag * jnp.sin(lam_im)
    den = a_re * a_re + a_im * a_im
    nr = ab_re - 1.0
    ni = ab_im
    f_re = (nr * a_re + ni * a_im) / den
    f_im = (ni * a_re - nr * a_im) / den
    abt_ref[0, 0:NS, :] = jnp.broadcast_to(ab_re, (128, NS)).T
    abt_ref[0, NS:2 * NS, :] = jnp.broadcast_to(ab_im, (128, NS)).T
    b_re = bre_ref[0]
    b_im = bim_ref[0]
    bb_re = (f_re * b_re - f_im * b_im).astype(BF16)
    bb_im = (f_re * b_im + f_im * b_re).astype(BF16)
    rep_r = (lax.broadcasted_iota(jnp.int32, (GW, S5C), 0) % S5C
             == lax.broadcasted_iota(jnp.int32, (GW, S5C), 1)).astype(BF16)
    in_blk = (lax.broadcasted_iota(jnp.int32, (GW, NS), 0) // S5C
              == lax.broadcasted_iota(jnp.int32, (GW, NS), 1) // S5P)
    bt_ref[0, :, :NS] = jnp.where(in_blk, _dot(rep_r, bb_re), 0.0).astype(BF16)
    bt_ref[0, :, NS:] = jnp.where(in_blk, _dot(rep_r, bb_im), 0.0).astype(BF16)
    rep_c = (lax.broadcasted_iota(jnp.int32, (S5C, GW), 0)
             == lax.broadcasted_iota(jnp.int32, (S5C, GW), 1) % S5C).astype(BF16)
    out_blk = (lax.broadcasted_iota(jnp.int32, (NS, GW), 0) // S5P
               == lax.broadcasted_iota(jnp.int32, (NS, GW), 1) // S5C)
    ct_ref[0, :NS, :] = jnp.where(out_blk, _dot(cre_ref[0].astype(BF16), rep_c), 0.0).astype(BF16)
    ct_ref[0, NS:, :] = jnp.where(out_blk, -_dot(cim_ref[0].astype(BF16), rep_c), 0.0).astype(BF16)
    i = lax.broadcasted_iota(jnp.int32, (LS, NS), 0).astype(F32)
    for slot, k in ((0, -i), (2, i), (4, i + 1.0), (6, LS * (i + 1.0))):
        pmag = jnp.exp(k * lam_re)
        tab_ref[0, slot] = pmag * jnp.cos(k * lam_im)
        tab_ref[0, slot + 1] = pmag * jnp.sin(k * lam_im)


def _s5_prepare(a_re, a_im, log_dt, b_re, b_im, c_re, c_im):
    depth = a_re.shape[0]
    are = a_re.reshape(depth, 1, NS)
    aim = a_im.reshape(depth, 1, NS)
    ldt = jnp.repeat(log_dt, S5P, axis=-1).reshape(depth, 1, NS)
    bre = jnp.transpose(b_re, (0, 3, 1, 2)).reshape(depth, S5C, NS)
    bim = jnp.transpose(b_im, (0, 3, 1, 2)).reshape(depth, S5C, NS)
    cre = jnp.transpose(c_re, (0, 1, 3, 2)).reshape(depth, NS, S5C)
    cim = jnp.transpose(c_im, (0, 1, 3, 2)).reshape(depth, NS, S5C)
    per_layer = lambda *shape: pl.BlockSpec((1,) + shape, lambda l: (l,) + (0,) * len(shape))
    out_shape = (jax.ShapeDtypeStruct((depth, 2 * NS, 128), F32),
                 jax.ShapeDtypeStruct((depth, GW, 2 * NS), BF16),
                 jax.ShapeDtypeStruct((depth, 2 * NS, GW), BF16),
                 jax.ShapeDtypeStruct((depth, NTAB, LS, NS), F32))
    return pl.pallas_call(
        _s5_prep_kernel, grid=(depth,),
        in_specs=[per_layer(1, NS), per_layer(1, NS), per_layer(1, NS),
                  per_layer(S5C, NS), per_layer(S5C, NS), per_layer(NS, S5C), per_layer(NS, S5C)],
        out_specs=(per_layer(2 * NS, 128), per_layer(GW, 2 * NS), per_layer(2 * NS, GW),
                   per_layer(NTAB, LS, NS)),
        out_shape=out_shape, name="s5_prepare",
        compiler_params=pltpu.CompilerParams(dimension_semantics=("arbitrary",)),
    )(are, aim, ldt, bre, bim, cre, cim)


N_STATE_OUT = 8


def _prompt_kernel(x_ref, mem_ref, normw_ref, win_ref, wg_ref, wout_ref, wmkv_ref,
                   cos_ref, s1_ref, s2_ref, dec_ref, qdec_ref, kdec_ref, cdec_ref,
                   retgn_ref, mlgn_ref, bi_ref, bf_ref,
                   tab_ref, bt_ref, ct_ref, dsk_ref, wglu_ref, fnw_ref,
                   *rest, last_layer, n_tblocks, n_prev):
    (y_ref, rets_ref, mlc_ref, mln_ref, mlm_ref, s5re_ref, s5im_ref, memk_ref, memv_ref,
     proj_ref, mix_ref, s_ref, c_ref, n_ref, m_ref, xre_ref, xim_ref,
     mk_ref, mv_ref, bu_ref, xcat_ref, car_ref) = rest[n_prev:]
    t = pl.program_id(1)
    hm = _head_masks()
    bd = _block_diag_mask()
    avg = _avg_matrix()
    lane128 = lax.broadcasted_iota(jnp.int32, (1, 128), 1)
    row_i = lax.broadcasted_iota(jnp.int32, (L, 128), 0)
    col_i = lax.broadcasted_iota(jnp.int32, (L, 128), 1)
    causal = row_i >= col_i
    tri_u = jnp.where(row_i <= col_i, 1.0, 0.0).astype(BF16)
    tri_sub = jnp.where(causal & (row_i // LS == col_i // LS), 1.0, 0.0)
    blk_sum = jnp.where(lax.broadcasted_iota(jnp.int32, (L // LS, 128), 0)
                        == lax.broadcasted_iota(jnp.int32, (L // LS, 128), 1) // LS, 1.0, 0.0)
    tri_ext = jnp.concatenate([tri_sub, blk_sum], axis=0).astype(BF16)

    @pl.when(t == 0)
    def _init():
        s_ref[...] = jnp.zeros_like(s_ref)
        c_ref[...] = jnp.zeros_like(c_ref)
        n_ref[...] = jnp.zeros_like(n_ref)
        m_ref[...] = jnp.zeros_like(m_ref)
        xre_ref[...] = jnp.zeros_like(xre_ref)
        xim_ref[...] = jnp.zeros_like(xim_ref)
        mkv = _dot(mem_ref[0].astype(BF16), wmkv_ref[...])
        mk = mkv[:, :GW]
        mv = mkv[:, GW:]
        memk_ref[0] = mk.T
        memv_ref[0] = mv.T
        mk_ref[...] = _stack_heads(mk.astype(BF16), hm)
        mv_ref[...] = _stack_heads(mv.astype(BF16), hm)

    x = x_ref[0]
    hn = _rms_norm(x, normw_ref[...]).astype(BF16)
    proj_ref[...] = _dot_nt(hn, win_ref[...])
    gates_t = _dot_nt(wg_ref[...], hn)

    nc = TB // L
    crow = [slice(c * L, (c + 1) * L) for c in range(nc)]

    def PB(blk, rows=slice(None)):
        return proj_ref[rows, blk * GW:(blk + 1) * GW]


    def xattn_stages():
        for piece in range(TB // XQ):
            rs = slice(piece * XQ, (piece + 1) * XQ)
            sc = _dot_nt(PB(AQ, rs).astype(BF16), mk_ref[...])
            yield
            ps = []
            for h in range(NH):
                seg = sc[:, h * MEM:(h + 1) * MEM]
                e = jnp.exp(seg - jnp.max(seg, axis=-1, keepdims=True))
                ps.append((e / jnp.sum(e, axis=-1, keepdims=True)).astype(BF16))
            p = jnp.concatenate(ps, axis=1)
            yield
            xa = _dot(p, mv_ref[...])
            yield
            mix_ref[rs, 3 * GW:4 * GW] = (xa * _silu(PB(AG, rs))).astype(BF16)
            yield

    def s5_stages():
        lanes = [(slice(j * 128, (j + 1) * 128), slice(NS + j * 128, NS + (j + 1) * 128))
                 for j in range(NS // 128)]
        bpc = L // LS
        nblk = TB // LS
        su = PB(SU)
        bu_ref[...] = _dot(su.astype(BF16), bt_ref[...])
        yield
        for c in range(nc + 1):
            if c < nc:
                for lre, lim in lanes:
                    br = bu_ref[crow[c], lre]
                    bi = bu_ref[crow[c], lim]
                    wr = jnp.tile(tab_ref[0, :, lre], (bpc, 1))
                    wi = jnp.tile(tab_ref[1, :, lre], (bpc, 1))
                    xcat_ref[crow[c], lre] = (wr * br - wi * bi).astype(BF16)
                    xcat_ref[crow[c], lim] = (wr * bi + wi * br).astype(BF16)
            if c > 0:
                z = _dot(tri_ext, xcat_ref[crow[c - 1], :])
                bu_ref[crow[c - 1], :] = z[0:L]
                car_ref[(c - 1) * bpc:c * bpc, :] = z[L:L + bpc]
            yield
        for _ in range(nc - 1):
            yield
        rowb = lax.broadcasted_iota(jnp.int32, (nblk, 128), 0)
        ers, eis, prs, pis, c0s = [], [], [], [], []
        for lre, lim in lanes:
            zr = car_ref[:, lre]
            zi = car_ref[:, lim]
            pr = tab_ref[2, LS - 1:LS, lre]
            pi = tab_ref[3, LS - 1:LS, lre]
            er = pr * zr - pi * zi
            ei = pr * zi + pi * zr
            pr = tab_ref[4, LS - 1:LS, lre]
            pi = tab_ref[5, LS - 1:LS, lre]
            c0r = xre_ref[0:1, lre]
            c0i = xim_ref[0:1, lre]
            ers.append(er + jnp.where(rowb == 0, pr * c0r - pi * c0i, 0.0))
            eis.append(ei + jnp.where(rowb == 0, pr * c0i + pi * c0r, 0.0))
            prs.append(pr)
            pis.append(pi)
            c0s.append((c0r, c0i))
        yield
        assert LS == 8 and nblk % 8 == 0
        ngrp = nblk // 8
        row8 = lax.broadcasted_iota(jnp.int32, (8, 128), 0)
        ers = [[v[g * 8:(g + 1) * 8] for g in range(ngrp)] for v in ers]
        eis = [[v[g * 8:(g + 1) * 8] for g in range(ngrp)] for v in eis]
        for k in range(3):
            d = 1 << k
            for j in range(len(lanes)):
                pr, pi = prs[j], pis[j]
                for g in range(ngrp):
                    er, ei = ers[j][g], eis[j][g]
                    sr = jnp.where(row8 >= d, pltpu.roll(er, d, 0), 0.0)
                    si = jnp.where(row8 >= d, pltpu.roll(ei, d, 0), 0.0)
                    ers[j][g], eis[j][g] = er + pr * sr - pi * si, ei + pr * si + pi * sr
                prs[j], pis[j] = pr * pr - pi * pi, 2.0 * (pr * pi)
            yield
        for g in range(1, ngrp):
            for j, (lre, lim) in enumerate(lanes):
                qr = tab_ref[6, :, lre]
                qi = tab_ref[7, :, lre]
                cr = jnp.broadcast_to(ers[j][g - 1][7:8, :], (8, 128))
                ci_ = jnp.broadcast_to(eis[j][g - 1][7:8, :], (8, 128))
                ers[j][g] = ers[j][g] + (qr * cr - qi * ci_)
                eis[j][g] = eis[j][g] + (qr * ci_ + qi * cr)
            if g % 3 == 0 or g == ngrp - 1:
                yield
        ers = [jnp.concatenate(v, axis=0) for v in ers]
        eis = [jnp.concatenate(v, axis=0) for v in eis]
        for j, (lre, lim) in enumerate(lanes):
            xre_ref[:, lre] = jnp.broadcast_to(ers[j][nblk - 1:nblk, :], (8, 128))
            xim_ref[:, lre] = jnp.broadcast_to(eis[j][nblk - 1:nblk, :], (8, 128))
            cr = jnp.where(rowb == 0, c0s[j][0], pltpu.roll(ers[j], 1, 0))
            ci_ = jnp.where(rowb == 0, c0s[j][1], pltpu.roll(eis[j], 1, 0))
            ar = tab_ref[4, 0:1, lre]
            ai = tab_ref[5, 0:1, lre]
            car_ref[:, lre] = ar * cr - ai * ci_
            car_ref[:, lim] = ar * ci_ + ai * cr
        yield
        for c in range(nc):
            for lre, lim in lanes:
                cr = jnp.concatenate(
                    [jnp.broadcast_to(car_ref[c * bpc + j:c * bpc + j + 1, lre], (LS, 128))
                     for j in range(bpc)], axis=0)
                ci_ = jnp.concatenate(
                    [jnp.broadcast_to(car_ref[c * bpc + j:c * bpc + j + 1, lim], (LS, 128))
                     for j in range(bpc)], axis=0)
                zr = bu_ref[crow[c], lre] + cr
                zi = bu_ref[crow[c], lim] + ci_
                pr = jnp.tile(tab_ref[2, :, lre], (bpc, 1))
                pi = jnp.tile(tab_ref[3, :, lre], (bpc, 1))
                xcat_ref[crow[c], lre] = (pr * zr - pi * zi).astype(BF16)
                xcat_ref[crow[c], lim] = (pr * zi + pi * zr).astype(BF16)
            yield
        sy = _dot(xcat_ref[...], ct_ref[...]) + dsk_ref[...] * su
        yield
        sy = _gelu_tanh(sy)
        gate = _dot(sy.astype(BF16), wglu_ref[...])
        yield
        mix_ref[:, 2 * GW:3 * GW] = (sy * _sigmoid(gate) * _silu(PB(SG))).astype(BF16)
        yield

    def retention_stages():
        rq = _rope(PB(RQ), cos_ref[...], s1_ref[...], s2_ref[...])
        rk = _rope(PB(RK), cos_ref[...], s1_ref[...], s2_ref[...])
        rvf = PB(RV)
        rv = rvf.astype(BF16)
        rqb = rq.astype(BF16)
        rkb = rk.astype(BF16)
        kst = [_stack_heads(rkb[crow[c]], hm) for c in range(nc)]
        vst = [_stack_heads(rv[crow[c]], hm) for c in range(nc)]
        yield
        inner = [_dot_nt(rqb[crow[c]], kst[c]) for c in range(nc)]
        yield
        pmat = [(inner[c] * dec_ref[...]).astype(BF16) for c in range(nc)]
        kd = [(rk[crow[c]] * kdec_ref[...]).astype(BF16) for c in range(nc)]
        yield
        rloc = [_dot(pmat[c], vst[c]) for c in range(nc)]
        upd = [_dot_tn(kd[c], rv[crow[c]]) for c in range(nc)]
        yield
        st = [s_ref[...]]
        for c in range(nc):
            st.append(st[c] * cdec_ref[...] + jnp.where(bd, upd[c], 0.0))
        s_ref[...] = st[nc]
        qd = [(rq[crow[c]] * qdec_ref[...]).astype(BF16) for c in range(nc)]
        yield
        ost = [_dot(qd[c], st[c].astype(BF16)) for c in range(nc)]
        yield
        ro = jnp.concatenate([rloc[c] + ost[c] for c in range(nc)], axis=0)
        mix_ref[:, 0:GW] = (_head_norm(ro, retgn_ref[...], avg) * _silu(PB(RG))).astype(BF16)
        yield

    def mlstm_stages():
        ig_r = gates_t[0:8] + bi_ref[...]
        lf_r = _log_sigmoid(gates_t[8:16] + bf_ref[...])
        yield
        b_r = [_dot_x3(lf_r[:, crow[c]], tri_u) for c in range(nc)]
        yield
        gd_r = [ig_r[:, crow[c]] - b_r[c] for c in range(nc)]
        lane_r = lax.broadcasted_iota(jnp.int32, (8, L), 1)
        cm_r = list(gd_r)
        for k in range(7):
            d = 1 << k
            cm_r = [jnp.maximum(v, jnp.where(lane_r >= d, pltpu.roll(v, d, 1), NEG_INF))
                    for v in cm_r]
            yield
        m_prev = [m_ref[...]]
        mt_r = []
        for c in range(nc):
            mt = jnp.maximum(b_r[c] + m_prev[c], b_r[c] + cm_r[c])
            mt_r.append(mt)
            m_prev.append(jnp.broadcast_to(mt[:, L - 1:L], (8, L)))
            yield
        m_ref[...] = m_prev[nc]
        cols = []
        for c in range(nc):
            bm = b_r[c] - mt_r[c]
            ws = jnp.exp(b_r[c] + m_prev[c] - mt_r[c])
            wl = jnp.exp(gd_r[c] + jnp.broadcast_to(bm[:, L - 1:L], (8, L)))
            emt = jnp.exp(-mt_r[c])
            cols.append(jnp.concatenate([bm, ws, wl, emt, jnp.zeros((L - 32, L), F32)], axis=0).T)
        mqf = PB(MQ)
        mq = mqf.astype(BF16)
        mkf = PB(MK)
        mvf = PB(MV)
        mv_ = mvf.astype(BF16)
        mkb = mkf.astype(BF16)
        kst = [_stack_heads(mkb[crow[c]], hm) for c in range(nc)]
        vst = [_stack_heads(mv_[crow[c]], hm) for c in range(nc)]
        yield
        sraw = [_dot_nt(mq[crow[c]], kst[c]) for c in range(nc)]
        yield
        smat, den_i = [], []
        for c in range(nc):
            parts, dens = [], []
            for h in range(NH):
                arg = jnp.where(causal, cols[c][:, h:h + 1] + gd_r[c][h:h + 1, :], NEG_INF)
                s_h = sraw[c][:, h * L:(h + 1) * L] * jnp.exp(arg)
                dens.append(jnp.sum(s_h, axis=-1, keepdims=True))
                parts.append(s_h.astype(BF16))
            den_i.append(dens)
            smat.append(jnp.concatenate(parts, axis=1))
            yield
        ones_blk = jnp.ones((L, 128), BF16)
        def per_head(columns):
            first = lane128 < DH
            return jnp.concatenate([jnp.where(first, columns[0], columns[1]),
                                    jnp.where(first, columns[2], columns[3])], axis=1)

        kws = [(mkf[crow[c]] * per_head([cols[c][:, 16 + h:17 + h] for h in range(NH)])
                ).astype(BF16) for c in range(nc)]
        yield
        rloc = [_dot(smat[c], vst[c]) for c in range(nc)]
        u = [_dot_tn(kws[c], jnp.concatenate([mv_[crow[c]], ones_blk], axis=1))
             for c in range(nc)]
        yield
        nmask = (lax.broadcasted_iota(jnp.int32, (GW, 128), 0) // DH
                 == lax.broadcasted_iota(jnp.int32, (GW, 128), 1))
        cst = [c_ref[...]]
        nst = [n_ref[...]]
        for c in range(nc):
            wsl256 = jnp.zeros((1, GW), F32)
            wsl128 = jnp.zeros((1, 128), F32)
            for h in range(NH):
                wsl = cols[c][L - 1:L, 8 + h:9 + h]
                wsl256 = wsl256 + jnp.where(hm[h], wsl, 0.0)
                wsl128 = wsl128 + jnp.where(lane128 == h, wsl, 0.0)
            cst.append(cst[c] * wsl256 + jnp.where(bd, u[c][:, :GW], 0.0))
            nst.append(nst[c] * wsl128 + jnp.where(nmask, u[c][:, GW:], 0.0))
        c_ref[...] = cst[nc]
        n_ref[...] = nst[nc]
        yield
        qc = [_dot(mq[crow[c]], cst[c].astype(BF16)) for c in range(nc)]
        qn = [_dot(mq[crow[c]], nst[c].astype(BF16)) for c in range(nc)]
        yield
        mhs = []
        for c in range(nc):
            wsc = [cols[c][:, 8 + h:9 + h] for h in range(NH)]
            rdn = []
            for h in range(NH):
                den = den_i[c][h] + wsc[h] * qn[c][:, h:h + 1]
                rdn.append(1.0 / jnp.maximum(jnp.abs(den), cols[c][:, 24 + h:25 + h]))
            mhs.append((rloc[c] + per_head(wsc) * qc[c]) * per_head(rdn))
            yield
        mh = jnp.concatenate(mhs, axis=0) * _sigmoid(PB(MO))
        mix_ref[:, GW:2 * GW] = (_head_norm(mh, mlgn_ref[...], avg) * _silu(PB(MG))).astype(BF16)
        yield

    pending = [mlstm_stages(), s5_stages(), retention_stages(), xattn_stages()]
    while pending:
        pending = [stage for stage in pending if next(stage, _DONE) is not _DONE]

    y = x_ref[0] + _dot(mix_ref[...], wout_ref[...])
    if last_layer:
        y = _rms_norm(y, fnw_ref[...])
    y_ref[0] = y

    @pl.when(t == n_tblocks - 1)
    def _final():
        s_fin = s_ref[...]
        c_fin = c_ref[...].T
        n_fin = n_ref[...].T
        for h in range(NH):
            blk = slice(h * DH, (h + 1) * DH)
            rets_ref[0, h] = s_fin[blk, blk]
            mlc_ref[0, h] = c_fin[blk, blk]
        mln_ref[0] = jnp.concatenate([n_fin[h:h + 1, h * DH:(h + 1) * DH] for h in range(NH)],
                                     axis=0)
        mlm_ref[0] = m_ref[...]
        s5re_ref[0] = xre_ref[...]
        s5im_ref[0] = xim_ref[...]


def _prompt_layer(layer, x, mem, w, consts, last_layer, prev):
    bsz, seq, _ = x.shape
    depth = w["w_in"].shape[0]
    nt = seq // TB
    full = lambda shape: pl.BlockSpec(shape, lambda b, t: (0,) * len(shape),
                                      pipeline_mode=pl.Buffered(1))
    lyr = lambda shape: pl.BlockSpec((None,) + shape, lambda b, t: (layer,) + (0,) * len(shape),
                                     pipeline_mode=pl.Buffered(1))
    tok = lambda width: pl.BlockSpec((TB, width), lambda b, t: (t, 0))
    per_b = lambda r, c: pl.BlockSpec((1, r, c), lambda b, t: (b, 0, 0))
    in_specs = [
        pl.BlockSpec((1, TB, D), lambda b, t: (b, t, 0)),
        per_b(MEM, D),
        lyr((1, D)), lyr((DP, D)), lyr((NGATE, D)), lyr((D, D)), lyr((D, 2 * GW)),
        tok(128), tok(128), tok(128),
        full((L, NH * L)), full((L, GW)), full((L, GW)), full((1, GW)),
        lyr((1, GW)), lyr((1, GW)), lyr((8, TB)), lyr((8, TB)),
        lyr((NTAB, LS, NS)), lyr((GW, 2 * NS)), lyr((2 * NS, GW)),
        lyr((1, GW)), lyr((GW, GW)), full((1, D)),
    ] + [pl.BlockSpec(memory_space=pl.ANY)] * len(prev)
    n_in = len(in_specs) - len(prev)
    state_shapes = [(NH, DH, DH),
                    (NH, DH, DH),
                    (NH, DH),
                    (8, 128),
                    (8, NS),
                    (8, NS),
                    (GW, MEM),
                    (GW, MEM)]
    assert len(state_shapes) == N_STATE_OUT and len(prev) in (0, N_STATE_OUT)
    out_shape = ((jax.ShapeDtypeStruct((bsz, seq, D), F32),)
                 + tuple(jax.ShapeDtypeStruct((depth, bsz) + s, F32) for s in state_shapes))
    state_spec = lambda s: pl.BlockSpec((None, 1) + s,
                                        lambda b, t: (layer, b) + (0,) * len(s))
    out_specs = ((pl.BlockSpec((1, TB, D), lambda b, t: (b, t, 0)),)
                 + tuple(state_spec(s) for s in state_shapes))
    scratch = [
        pltpu.VMEM((TB, DP), F32),
        pltpu.VMEM((TB, D), BF16),
        pltpu.VMEM((GW, GW), F32),
        pltpu.VMEM((GW, GW), F32),
        pltpu.VMEM((GW, 128), F32),
        pltpu.VMEM((8, 128), F32),
        pltpu.VMEM((8, NS), F32),
        pltpu.VMEM((8, NS), F32),
        pltpu.VMEM((NH * MEM, GW), BF16),
        pltpu.VMEM((NH * MEM, GW), BF16),
        pltpu.VMEM((TB, 2 * NS), F32),
        pltpu.VMEM((TB, 2 * NS), BF16),
        pltpu.VMEM((TB // LS, 2 * NS), F32),
    ]
    kern = functools.partial(_prompt_kernel, last_layer=last_layer, n_tblocks=nt, n_prev=len(prev))
    return pl.pallas_call(
        kern, grid=(bsz, nt), in_specs=in_specs, out_specs=out_specs, out_shape=out_shape,
        scratch_shapes=scratch, name="prompt_layer",
        input_output_aliases={n_in + i: 1 + i for i in range(len(prev))},
        compiler_params=pltpu.CompilerParams(
            dimension_semantics=("arbitrary", "arbitrary"), vmem_limit_bytes=VMEM_LIMIT),
    )(x, mem, w["norm_w"], w["w_in"], w["w_gate"], w["w_out"], w["w_mem_kv"],
      consts["cos_p"], consts["s1_p"], consts["s2_p"],
      consts["dec"], consts["qdec"], consts["kdec"], consts["cdec"],
      w["ret_gn"], w["ml_gn"], w["b_i8"], w["b_f8"],
      w["tab"], w["bt"], w["ct"], w["s5_d"], w["w_glu"], w["final_norm_w"], *prev)


SROWS = 16
CACHE_BUFS = 3


def _sample_kernel_t(x_ref, normw_ref, win_ref, wg_ref, wout_ref,
                     cos_ref, s1_ref, s2_ref, gam_ref, gam8_ref,
                     retgn_ref, mlgn_ref, bi_ref, bf_ref,
                     abt_ref, bt_ref, ct_ref, dsk_ref, wglu_ref, fnw_ref,
                     m0_ref, n0_ref, x0re_ref, x0im_ref,
                     rets_ref, mlc_ref, ck_ref, cv_ref,
                     y_ref, retn_ref, mlcn_ref, mlnn_ref, mlmn_ref, s5re_ref, s5im_ref,
                     hs_ref, proj_ref, qgt_ref, rkt_ref, rvt_ref, mqt_ref, mkt_ref, mvt_ref,
                     wi_ref, ws_ref, ot_ref, cqt_ref, qa8_ref, xa_ref,
                     ckb_ref, cvb_ref, csem_ref,
                     *, n_layers, n_blocks):
    layer = pl.program_id(0)
    g = pl.program_id(1)
    nsamp = x_ref.shape[0]
    ones_bd = _ones_matrix()
    avg = _avg_matrix()
    mask8 = (lax.broadcasted_iota(jnp.int32, (8, GW), 0)
             == lax.broadcasted_iota(jnp.int32, (8, GW), 1) // DH)

    def P(blk):
        return proj_ref[:, blk * GW:(blk + 1) * GW]

    step = layer * n_blocks + g
    slot = step % CACHE_BUFS

    def cache_copies(at_step, at_slot):
        lyr_i = at_step // n_blocks
        first = (at_step % n_blocks) * SB
        rows = pl.ds(first if isinstance(first, int) else pl.multiple_of(first, SB), SB)
        return (pltpu.make_async_copy(ck_ref.at[lyr_i, rows], ckb_ref.at[at_slot],
                                      csem_ref.at[0, at_slot]),
                pltpu.make_async_copy(cv_ref.at[lyr_i, rows], cvb_ref.at[at_slot],
                                      csem_ref.at[1, at_slot]))

    @pl.when(step == 0)
    def _prefetch():
        for s in range(CACHE_BUFS):
            for cp in cache_copies(s, s):
                cp.start()

    @pl.when((layer == 0) & (g == 0))
    def _load_x():
        hs_ref[...] = x_ref[...]

    @pl.when(g == 0)
    def _pre():
        hn = _rms_norm(hs_ref[...], normw_ref[...]).astype(BF16)
        proj_ref[...] = _dot_nt(hn, win_ref[...])
        cosr, s1, s2 = cos_ref[...], s1_ref[...], s2_ref[...]
        qgt_ref[...] = (_rope(P(RQ), cosr, s1, s2) * gam_ref[...]).T
        rkt_ref[...] = _rope(P(RK), cosr, s1, s2).T
        rvt_ref[...] = P(RV).T
        mqt_ref[...] = P(MQ).T
        mkt_ref[...] = P(MK).T
        mvt_ref[...] = P(MV).T
        gt = _dot_nt(wg_ref[...], hn)
        ig = gt[0:8] + bi_ref[...]
        lf = _log_sigmoid(gt[8:16] + bf_ref[...])
        a = lf + m0_ref[...]
        mt = jnp.maximum(a, ig)
        wi_ref[...] = jnp.exp(ig - mt)
        ws_ref[...] = jnp.exp(a - mt)
        mlmn_ref[...] = mt
        ot_ref[...] = jnp.zeros_like(ot_ref)
        qa8_ref[...] = jnp.where(mask8[None], P(AQ)[:, None, :], 0.0).reshape(8 * nsamp, GW)
        but = _dot_tn(bt_ref[...], P(SU).T.astype(BF16))
        are, aim = abt_ref[0:NS, :], abt_ref[NS:2 * NS, :]
        x0r, x0i = x0re_ref[...], x0im_ref[...]
        s5re_ref[...] = are * x0r - aim * x0i + but[0:NS]
        s5im_ref[...] = are * x0i + aim * x0r + but[NS:2 * NS]

    head = g // (GW // SROWS // NH)
    hrow = pl.ds(pl.multiple_of(head * DH, DH), DH)
    gam_row = gam8_ref[pl.ds(head, 1), :]
    ws_row = ws_ref[pl.ds(head, 1), :]
    wi_row = wi_ref[pl.ds(head, 1), :]
    v_slab = rvt_ref[hrow, :]
    q_slab = mqt_ref[hrow, :]
    k_slab = mkt_ref[hrow, :]
    o_acc = jnp.zeros((DH, nsamp), F32)
    for i in range(SROWS):
        r = pl.ds(g * SROWS + i, 1)
        s_t = rets_ref[i]
        o_acc = o_acc + qgt_ref[r, :] * s_t
        retn_ref[i] = gam_row * s_t + rkt_ref[r, :] * v_slab
        c_t = mlc_ref[i]
        cqt_ref[r, :] = jnp.sum(c_t * q_slab, axis=0, keepdims=True)
        mlcn_ref[i] = ws_row * c_t + (wi_row * mvt_ref[r, :]) * k_slab
    ot_ref[hrow, :] = ot_ref[hrow, :] + o_acc

    tiles = [pl.ds(pl.multiple_of((g * SB + i) * 8, 8), 8) for i in range(SB)]
    q8 = [qa8_ref[tiles[i], :].astype(BF16) for i in range(SB)]
    for cp in cache_copies(step, slot):
        cp.wait()
    sc = [_dot(q8[i], ckb_ref[slot, i].astype(BF16)) for i in range(SB)]
    ps = []
    for i in range(SB):
        e = jnp.exp(sc[i] - jnp.max(sc[i], axis=-1, keepdims=True))
        ps.append((e / jnp.sum(e, axis=-1, keepdims=True)).astype(BF16))
    ta = [_dot_nt(ps[i], cvb_ref[slot, i].astype(BF16)) for i in range(SB)]
    xa_ref[pl.ds(pl.multiple_of(g * SB, SB), SB), :] = jnp.concatenate(
        [jnp.sum(jnp.where(mask8, ta[i], 0.0), axis=0, keepdims=True) for i in range(SB)], axis=0)

    @pl.when(step + CACHE_BUFS < n_layers * n_blocks)
    def _refill():
        for cp in cache_copies(step + CACHE_BUFS, slot):
            cp.start()

    @pl.when(g == n_blocks - 1)
    def _post():
        cosr, s1, s2 = cos_ref[...], s1_ref[...], s2_ref[...]
        rq = _rope(P(RQ), cosr, s1, s2)
        rk = _rope(P(RK), cosr, s1, s2)
        ro = _dot_x2(rq * rk, ones_bd) * P(RV) + ot_ref[...].T
        ret_out = _head_norm(ro, retgn_ref[...], avg) * _silu(P(RG))
        mht = []
        for h in range(NH):
            rows = slice(h * DH, (h + 1) * DH)
            wi = wi_ref[h:h + 1, :]
            ws = ws_ref[h:h + 1, :]
            emt = jnp.exp(-mlmn_ref[h:h + 1, :])
            qt, kt, n0 = mqt_ref[rows, :], mkt_ref[rows, :], n0_ref[rows, :]
            s = jnp.sum(qt * kt, axis=0, keepdims=True) * wi
            den = s + ws * jnp.sum(n0 * qt, axis=0, keepdims=True)
            mht.append((s * mvt_ref[rows, :] + ws * cqt_ref[rows, :])
                       / jnp.maximum(jnp.abs(den), emt))
            mlnn_ref[rows, :] = ws * n0 + wi * kt
        mh = jnp.concatenate(mht, axis=0).T * _sigmoid(P(MO))
        ml_out = _head_norm(mh, mlgn_ref[...], avg) * _silu(P(MG))
        su = P(SU)
        xcat = jnp.concatenate([s5re_ref[...], s5im_ref[...]], axis=0).astype(BF16)
        sy = _dot_tn(xcat, ct_ref[...]) + dsk_ref[...] * su
        sy = _gelu_tanh(sy)
        sy = sy * _sigmoid(_dot(sy.astype(BF16), wglu_ref[...]))
        s5_out = sy * _silu(P(SG))
        xa_out = xa_ref[...] * _silu(P(AG))
        mix = jnp.concatenate([ret_out, ml_out, s5_out, xa_out], axis=1).astype(BF16)
        y = hs_ref[...] + _dot(mix, wout_ref[...])
        hs_ref[...] = y

        @pl.when(layer == n_layers - 1)
        def _emit():
            y_ref[...] = _rms_norm(y, fnw_ref[...])


def _sample_layers_t(x, st, w, consts):
    nsamp = x.shape[0]
    depth = w["w_in"].shape[0]
    nb = GW // SROWS
    assert nsamp == nb * SB and nsamp == 128
    once = lambda shape: pl.BlockSpec(shape, lambda l, g: (0,) * len(shape),
                                      pipeline_mode=pl.Buffered(1))
    lyr = lambda shape: pl.BlockSpec((None,) + shape, lambda l, g: (l,) + (0,) * len(shape))
    lyr_out = lyr
    srows = pl.BlockSpec((None, SROWS, DH, nsamp), lambda l, g: (l, g, 0, 0))
    cache = pl.BlockSpec(memory_space=pl.ANY)
    in_specs = [
        once((nsamp, D)), lyr((1, D)), lyr((DP, D)), lyr((NGATE, D)), lyr((D, D)),
        once((1, 128)), once((1, 128)), once((1, 128)), once((1, GW)), once((8, nsamp)),
        lyr((1, GW)), lyr((1, GW)), lyr((8, nsamp)), lyr((8, nsamp)),
        lyr((2 * NS, nsamp)), lyr((GW, 2 * NS)), lyr((2 * NS, GW)),
        lyr((1, GW)), lyr((GW, GW)), once((1, D)),
        lyr((8, nsamp)), lyr((GW, nsamp)), lyr((NS, nsamp)), lyr((NS, nsamp)),
        srows, srows, cache, cache,
    ]
    out_shape = (
        jax.ShapeDtypeStruct((nsamp, D), F32),
        jax.ShapeDtypeStruct((depth, GW, DH, nsamp), F32),
        jax.ShapeDtypeStruct((depth, GW, DH, nsamp), F32),
        jax.ShapeDtypeStruct((depth, GW, nsamp), F32),
        jax.ShapeDtypeStruct((depth, 8, nsamp), F32),
        jax.ShapeDtypeStruct((depth, NS, nsamp), F32),
        jax.ShapeDtypeStruct((depth, NS, nsamp), F32),
    )
    out_specs = (
        pl.BlockSpec((nsamp, D), lambda l, g: (0, 0)), srows, srows,
        lyr_out((GW, nsamp)), lyr_out((8, nsamp)), lyr_out((NS, nsamp)), lyr_out((NS, nsamp)),
    )
    scratch = [
        pltpu.VMEM((nsamp, D), F32),
        pltpu.VMEM((nsamp, DP), F32),
        pltpu.VMEM((GW, nsamp), F32), pltpu.VMEM((GW, nsamp), F32), pltpu.VMEM((GW, nsamp), F32),
        pltpu.VMEM((GW, nsamp), F32), pltpu.VMEM((GW, nsamp), F32), pltpu.VMEM((GW, nsamp), F32),
        pltpu.VMEM((8, nsamp), F32), pltpu.VMEM((8, nsamp), F32),
        pltpu.VMEM((GW, nsamp), F32), pltpu.VMEM((GW, nsamp), F32),
        pltpu.VMEM((8 * nsamp, GW), F32), pltpu.VMEM((nsamp, GW), F32),
        pltpu.VMEM((CACHE_BUFS, SB, GW, MEM), F32),
        pltpu.VMEM((CACHE_BUFS, SB, GW, MEM), F32),
        pltpu.SemaphoreType.DMA((2, CACHE_BUFS)),
    ]
    kern = functools.partial(_sample_kernel_t, n_layers=depth, n_blocks=nb)
    return pl.pallas_call(
        kern, grid=(depth, nb), in_specs=in_specs, out_specs=out_specs, out_shape=out_shape,
        scratch_shapes=scratch, name="sample_layers",
        compiler_params=pltpu.CompilerParams(
            dimension_semantics=("arbitrary", "arbitrary"), vmem_limit_bytes=VMEM_LIMIT),
    )(x, w["norm_w"], w["w_in"], w["w_gate"], w["w_out"],
      consts["cos_s"], consts["s1_s"], consts["s2_s"], consts["gam"], consts["gam8"],
      w["ret_gn"], w["ml_gn"], w["b_i8"], w["b_f8"],
      w["abt"], w["bt"], w["ct"], w["s5_d"], w["w_glu"], w["final_norm_w"],
      st["m"], st["n"], st["s5_re"], st["s5_im"], st["ret"], st["c"], st["mem_k"], st["mem_v"])


def _rope_tables(pos):
    half = DH // 2
    inv = ROPE_BASE ** (-np.arange(half, dtype=np.float64) / half)
    ang = np.asarray(pos, np.float64)[:, None] * inv[None, :]
    cos, sin = np.cos(ang), np.sin(ang)
    zero = np.zeros_like(sin)
    c = np.tile(np.concatenate([cos, cos], axis=-1), (1, 2))
    s1 = np.tile(np.concatenate([zero, sin], axis=-1), (1, 2))
    s2 = np.tile(np.concatenate([-sin, zero], axis=-1), (1, 2))
    return tuple(jnp.asarray(t, F32) for t in (c, s1, s2))


def _constants(seq):
    lg = np.log1p(-np.power(2.0, -5.0 - np.arange(NH, dtype=np.float64)))[:, None]
    idx = np.arange(L, dtype=np.float64)
    diff = idx[:, None] - idx[None, :]
    decay = np.where(diff >= 0, np.exp(lg[:, :, None] * np.maximum(diff, 0.0)), 0.0)
    rep = lambda t: np.repeat(t, DH, axis=0).T
    consts = {
        "dec": np.transpose(decay, (1, 0, 2)).reshape(L, NH * L),
        "qdec": rep(np.exp(lg * (idx + 1.0))),
        "kdec": rep(np.exp(lg * (L - 1.0 - idx))),
        "cdec": rep(np.exp(lg * L)),
        "gam": rep(np.exp(lg * 1.0)),
        "gam8": np.pad(np.broadcast_to(np.exp(lg), (NH, 128)), ((0, 8 - NH), (0, 0))),
    }
    consts = {k: jnp.asarray(v, F32) for k, v in consts.items()}
    consts["cos_p"], consts["s1_p"], consts["s2_p"] = _rope_tables(np.arange(seq))
    consts["cos_s"], consts["s1_s"], consts["s2_s"] = _rope_tables(PAST_LEN + np.arange(1))
    return consts


def _pack_w_in(w_in):
    wt = jnp.swapaxes(w_in, 1, 2)
    sizes = (GW,) * 9 + (NH, NH) + (GW,) * 4
    offs = np.concatenate([[0], np.cumsum(sizes)])
    seg = [wt[:, int(offs[i]):int(offs[i + 1]), :] for i in range(len(sizes))]
    scale = DH ** -0.5
    pad8 = lambda t: jnp.pad(t, ((0, 0), (0, 8 - t.shape[1]), (0, 0)))
    blocks = [seg[11],
              seg[0], seg[1] * scale, seg[2], seg[3],
              seg[4], seg[5] * scale, seg[6], seg[7], seg[8],
              seg[12], seg[13] * scale, seg[14]]
    gates = jnp.concatenate([pad8(seg[9]), pad8(seg[10])], axis=1)
    return jnp.concatenate(blocks, axis=1).astype(BF16), gates.astype(BF16)


def kernel(x_prompt, x_sample, mem_prompt, state_ret, state_mlstm_c, state_mlstm_n, state_mlstm_m,
           state_s5_re, state_s5_im, cache_mem_k, cache_mem_v,
           norm_w, w_in, ret_gn, ml_b_i, ml_b_f, ml_gn,
           s5_a_re, s5_a_im, s5_log_dt, s5_b_re, s5_b_im, s5_c_re, s5_c_im, s5_d, s5_w_glu,
           w_mem_k, w_mem_v, w_out, final_norm_w):
    depth = norm_w.shape[0]
    bp, seq, _ = x_prompt.shape
    bs = x_sample.shape[0]
    consts = _constants(seq)
    abt, bt, ct, tab = _s5_prepare(s5_a_re, s5_a_im, s5_log_dt, s5_b_re, s5_b_im, s5_c_re, s5_c_im)
    rows8 = lambda t: jnp.pad(jnp.broadcast_to(t[:, :, None], (depth, NH, TB)),
                              ((0, 0), (0, 8 - NH), (0, 0)))
    w_main, w_gate = _pack_w_in(w_in)
    w = {
        "norm_w": norm_w[:, None], "w_in": w_main, "w_gate": w_gate, "w_out": w_out.astype(BF16),
        "w_mem_kv": jnp.concatenate([w_mem_k, w_mem_v], axis=-1).astype(BF16),
        "ret_gn": ret_gn[:, None], "ml_gn": ml_gn[:, None],
        "b_i8": rows8(ml_b_i), "b_f8": rows8(ml_b_f),
        "abt": abt, "tab": tab, "bt": bt, "ct": ct,
        "s5_d": s5_d[:, None], "w_glu": s5_w_glu.astype(BF16),
        "final_norm_w": final_norm_w[None],
    }

    hp = x_prompt
    states = ()
    for l in range(depth):
        hp, *states = _prompt_layer(l, hp, mem_prompt, w, consts, l == depth - 1, tuple(states))
    ret_p, mlc_p, mln_p, ms, xr, xi, mk, mv = states
    mlm_p = ms[:, :, :NH, 0]
    s5re_p = xr[:, :, 0].reshape(depth, bp, S5G, S5P)
    s5im_p = xi[:, :, 0].reshape(depth, bp, S5G, S5P)
    memk_p = jnp.transpose(mk.reshape(depth, bp, NH, DH, MEM), (0, 1, 4, 2, 3))
    memv_p = jnp.transpose(mv.reshape(depth, bp, NH, DH, MEM), (0, 1, 4, 2, 3))

    st = {
        "m": jnp.pad(jnp.swapaxes(state_mlstm_m, 1, 2), ((0, 0), (0, 8 - NH), (0, 0))),
        "n": jnp.transpose(state_mlstm_n, (0, 2, 3, 1)).reshape(depth, GW, bs),
        "s5_re": jnp.transpose(state_s5_re, (0, 2, 3, 1)).reshape(depth, NS, bs),
        "s5_im": jnp.transpose(state_s5_im, (0, 2, 3, 1)).reshape(depth, NS, bs),
        "ret": jnp.transpose(state_ret, (0, 2, 3, 4, 1)).reshape(depth, GW, DH, bs),
        "c": jnp.transpose(state_mlstm_c, (0, 2, 3, 4, 1)).reshape(depth, GW, DH, bs),
        "mem_k": jnp.transpose(cache_mem_k, (0, 1, 3, 4, 2)).reshape(depth, bs, GW, MEM),
        "mem_v": jnp.transpose(cache_mem_v, (0, 1, 3, 4, 2)).reshape(depth, bs, GW, MEM),
    }
    hs, rn, cn, nn, mn, sr, si = _sample_layers_t(x_sample.reshape(bs, D), st, w, consts)
    back5 = lambda t: jnp.transpose(t.reshape(depth, NH, DH, DH, bs), (0, 4, 1, 2, 3))
    back4 = lambda t, a, b: jnp.transpose(t.reshape(depth, a, b, bs), (0, 3, 1, 2))
    return (hp, hs.reshape(bs, 1, D),
            ret_p, back5(rn), mlc_p, back5(cn),
            mln_p, back4(nn, NH, DH), mlm_p, jnp.swapaxes(mn[:, :NH], 1, 2),
            s5re_p, back4(sr, S5G, S5P), s5im_p, back4(si, S5G, S5P),
            memk_p, memv_p)
```

```python
import functools
import math

import numpy as np
import jax
import jax.numpy as jnp
from jax import lax
from jax.experimental import pallas as pl
from jax.experimental.pallas import tpu as pltpu

F32 = jnp.float32
BF16 = jnp.bfloat16

D = 1024
GW = 256
NH = 4
DH = 64
L = 128
LS = 8
NTAB = 8
S5G = 16
S5P = 64
S5C = 16
NS = S5G * S5P
MEM = 256
EPS = 1e-6
NEG_INF = -1e30
ROPE_BASE = 10000.0
PAST_LEN = 16384

TB = 512
SB = 8
XQ = 512
NBLK = 13
DP = NBLK * GW
(SU, RQ, RK, RV, RG, MQ, MK, MV, MO, MG, SG, AQ, AG) = range(NBLK)
NGATE = 16

VMEM_LIMIT = 56 * 1024 * 1024


_DONE = object()


def _dot(a, b):
    return jnp.dot(a, b, preferred_element_type=F32)


def _dot_nt(a, b):
    return lax.dot_general(a, b, (((1,), (1,)), ((), ())), preferred_element_type=F32)


def _dot_tn(a, b):
    return lax.dot_general(a, b, (((0,), (0,)), ((), ())), preferred_element_type=F32)


def _split2(x):
    hi = x.astype(BF16)
    lo = (x - hi.astype(F32)).astype(BF16)
    return hi, lo


def _dot_x2(x, w):
    hi, lo = _split2(x)
    return _dot(hi, w) + _dot(lo, w)


def _dot_x3(x, w):
    hi = x.astype(BF16)
    r = x - hi.astype(F32)
    mid = r.astype(BF16)
    lo = (r - mid.astype(F32)).astype(BF16)
    return _dot(hi, w) + _dot(mid, w) + _dot(lo, w)


def _sigmoid(x):
    return 0.5 * (1.0 + jnp.tanh(0.5 * x))


def _silu(x):
    h = 0.5 * x
    return h + h * jnp.tanh(h)


def _log_sigmoid(x):
    return jnp.minimum(x, 0.0) - jnp.log1p(jnp.exp(-jnp.abs(x)))


def _gelu_tanh(x):
    c = math.sqrt(2.0 / math.pi)
    h = 0.5 * x
    return h + h * jnp.tanh(x * (c + (0.044715 * c) * (x * x)))


def _lane_head(n):
    return lax.broadcasted_iota(jnp.int32, (1, n), 1) // DH


def _head_masks():
    lh = _lane_head(GW)
    return [lh == h for h in range(NH)]


def _block_diag_mask():
    r = lax.broadcasted_iota(jnp.int32, (GW, GW), 0) // DH
    c = lax.broadcasted_iota(jnp.int32, (GW, GW), 1) // DH
    return r == c


def _avg_matrix():
    return jnp.where(_block_diag_mask(), 1.0 / DH, 0.0).astype(BF16)


def _ones_matrix():
    return jnp.where(_block_diag_mask(), 1.0, 0.0).astype(BF16)


def _rope(x, cos, s1, s2):
    outs = []
    for j in range(2):
        xs = x[:, j * 128:(j + 1) * 128]
        outs.append(xs * cos + pltpu.roll(xs, 32, 1) * s1 + pltpu.roll(xs, 96, 1) * s2)
    return jnp.concatenate(outs, axis=1)


def _head_norm(x, gain, avg):
    mu = _dot_x2(x, avg)
    d = x - mu
    var = _dot((d * d).astype(BF16), avg)
    return d * lax.rsqrt(var + EPS) * gain


def _rms_norm(x, w):
    ms = jnp.mean(x * x, axis=-1, keepdims=True)
    return x * lax.rsqrt(ms + EPS) * w


def _stack_heads(x, hm):
    zero = jnp.zeros((x.shape[0], 128), x.dtype)
    blocks = []
    for h in range(NH):
        half = slice((h * DH) // 128 * 128, (h * DH) // 128 * 128 + 128)
        kept = jnp.where(hm[h][:, half], x[:, half], 0.0)
        blocks.append(jnp.concatenate([kept, zero] if half.start == 0 else [zero, kept], axis=1))
    return jnp.concatenate(blocks, axis=0)


def _s5_prep_kernel(are_ref, aim_ref, ldt_ref, bre_ref, bim_ref, cre_ref, cim_ref,
                    abt_ref, bt_ref, ct_ref, tab_ref):
    a_re = are_ref[0]
    a_im = aim_ref[0]
    dt = jnp.exp(ldt_ref[0])
    lam_re = a_re * dt
    lam_im = a_im * dt
    mag = jnp.exp(lam_re)
    ab_re = mag * jnp.cos(lam_im)
    ab_im = mag * jnp.sin(lam_im)
    den = a_re * a_re + a_im * a_im
    nr = ab_re - 1.0
    ni = ab_im
    f_re = (nr * a_re + ni * a_im) / den
    f_im = (ni * a_re - nr * a_im) / den
    abt_ref[0, 0:NS, :] = jnp.broadcast_to(ab_re, (128, NS)).T
    abt_ref[0, NS:2 * NS, :] = jnp.broadcast_to(ab_im, (128, NS)).T
    b_re = bre_ref[0]
    b_im = bim_ref[0]
    bb_re = (f_re * b_re - f_im * b_im).astype(BF16)
    bb_im = (f_re * b_im + f_im * b_re).astype(BF16)
    rep_r = (lax.broadcasted_iota(jnp.int32, (GW, S5C), 0) % S5C
             == lax.broadcasted_iota(jnp.int32, (GW, S5C), 1)).astype(BF16)
    in_blk = (lax.broadcasted_iota(jnp.int32, (GW, NS), 0) // S5C
              == lax.broadcasted_iota(jnp.int32, (GW, NS), 1) // S5P)
    bt_ref[0, :, :NS] = jnp.where(in_blk, _dot(rep_r, bb_re), 0.0).astype(BF16)
    bt_ref[0, :, NS:] = jnp.where(in_blk, _dot(rep_r, bb_im), 0.0).astype(BF16)
    rep_c = (lax.broadcasted_iota(jnp.int32, (S5C, GW), 0)
             == lax.broadcasted_iota(jnp.int32, (S5C, GW), 1) % S5C).astype(BF16)
    out_blk = (lax.broadcasted_iota(jnp.int32, (NS, GW), 0) // S5P
               == lax.broadcasted_iota(jnp.int32, (NS, GW), 1) // S5C)
    ct_ref[0, :NS, :] = jnp.where(out_blk, _dot(cre_ref[0].astype(BF16), rep_c), 0.0).astype(BF16)
    ct_ref[0, NS:, :] = jnp.where(out_blk, -_dot(cim_ref[0].astype(BF16), rep_c), 0.0).astype(BF16)
    i = lax.broadcasted_iota(jnp.int32, (LS, NS), 0).astype(F32)
    for slot, k in ((0, -i), (2, i), (4, i + 1.0), (6, LS * (i + 1.0))):
        pmag = jnp.exp(k * lam_re)
        tab_ref[0, slot] = pmag * jnp.cos(k * lam_im)
        tab_ref[0, slot + 1] = pmag * jnp.sin(k * lam_im)


def _s5_prepare(a_re, a_im, log_dt, b_re, b_im, c_re, c_im):
    depth = a_re.shape[0]
    are = a_re.reshape(depth, 1, NS)
    aim = a_im.reshape(depth, 1, NS)
    ldt = jnp.repeat(log_dt, S5P, axis=-1).reshape(depth, 1, NS)
    bre = jnp.transpose(b_re, (0, 3, 1, 2)).reshape(depth, S5C, NS)
    bim = jnp.transpose(b_im, (0, 3, 1, 2)).reshape(depth, S5C, NS)
    cre = jnp.transpose(c_re, (0, 1, 3, 2)).reshape(depth, NS, S5C)
    cim = jnp.transpose(c_im, (0, 1, 3, 2)).reshape(depth, NS, S5C)
    per_layer = lambda *shape: pl.BlockSpec((1,) + shape, lambda l: (l,) + (0,) * len(shape))
    out_shape = (jax.ShapeDtypeStruct((depth, 2 * NS, 128), F32),
                 jax.ShapeDtypeStruct((depth, GW, 2 * NS), BF16),
                 jax.ShapeDtypeStruct((depth, 2 * NS, GW), BF16),
                 jax.ShapeDtypeStruct((depth, NTAB, LS, NS), F32))
    return pl.pallas_call(
        _s5_prep_kernel, grid=(depth,),
        in_specs=[per_layer(1, NS), per_layer(1, NS), per_layer(1, NS),
                  per_layer(S5C, NS), per_layer(S5C, NS), per_layer(NS, S5C), per_layer(NS, S5C)],
        out_specs=(per_layer(2 * NS, 128), per_layer(GW, 2 * NS), per_layer(2 * NS, GW),
                   per_layer(NTAB, LS, NS)),
        out_shape=out_shape, name="s5_prepare",
        compiler_params=pltpu.CompilerParams(dimension_semantics=("arbitrary",)),
    )(are, aim, ldt, bre, bim, cre, cim)


N_STATE_OUT = 8


def _prompt_kernel(x_ref, mem_ref, normw_ref, win_ref, wg_ref, wout_ref, wmkv_ref,
                   cos_ref, s1_ref, s2_ref, dec_ref, qdec_ref, kdec_ref, cdec_ref,
                   retgn_ref, mlgn_ref, bi_ref, bf_ref,
                   tab_ref, bt_ref, ct_ref, dsk_ref, wglu_ref, fnw_ref,
                   *rest, last_layer, n_tblocks, n_prev):
    (y_ref, rets_ref, mlc_ref, mln_ref, mlm_ref, s5re_ref, s5im_ref, memk_ref, memv_ref,
     proj_ref, mix_ref, s_ref, c_ref, n_ref, m_ref, xre_ref, xim_ref,
     mk_ref, mv_ref, bu_ref, xcat_ref, car_ref) = rest[n_prev:]
    t = pl.program_id(1)
    hm = _head_masks()
    bd = _block_diag_mask()
    avg = _avg_matrix()
    lane128 = lax.broadcasted_iota(jnp.int32, (1, 128), 1)
    row_i = lax.broadcasted_iota(jnp.int32, (L, 128), 0)
    col_i = lax.broadcasted_iota(jnp.int32, (L, 128), 1)
    causal = row_i >= col_i
    tri_u = jnp.where(row_i <= col_i, 1.0, 0.0).astype(BF16)
    tri_sub = jnp.where(causal & (row_i // LS == col_i // LS), 1.0, 0.0)
    blk_sum = jnp.where(lax.broadcasted_iota(jnp.int32, (L // LS, 128), 0)
                        == lax.broadcasted_iota(jnp.int32, (L // LS, 128), 1) // LS, 1.0, 0.0)
    tri_ext = jnp.concatenate([tri_sub, blk_sum], axis=0).astype(BF16)

    @pl.when(t == 0)
    def _init():
        s_ref[...] = jnp.zeros_like(s_ref)
        c_ref[...] = jnp.zeros_like(c_ref)
        n_ref[...] = jnp.zeros_like(n_ref)
        m_ref[...] = jnp.zeros_like(m_ref)
        xre_ref[...] = jnp.zeros_like(xre_ref)
        xim_ref[...] = jnp.zeros_like(xim_ref)
        mkv = _dot(mem_ref[0].astype(BF16), wmkv_ref[...])
        mk = mkv[:, :GW]
        mv = mkv[:, GW:]
        memk_ref[0] = mk.T
        memv_ref[0] = mv.T
        mk_ref[...] = _stack_heads(mk.astype(BF16), hm)
        mv_ref[...] = _stack_heads(mv.astype(BF16), hm)

    x = x_ref[0]
    hn = _rms_norm(x, normw_ref[...]).astype(BF16)
    proj_ref[...] = _dot_nt(hn, win_ref[...])
    gates_t = _dot_nt(wg_ref[...], hn)

    nc = TB // L
    crow = [slice(c * L, (c + 1) * L) for c in range(nc)]

    def PB(blk, rows=slice(None)):
        return proj_ref[rows, blk * GW:(blk + 1) * GW]


    def xattn_stages():
        for piece in range(TB // XQ):
            rs = slice(piece * XQ, (piece + 1) * XQ)
            sc = _dot_nt(PB(AQ, rs).astype(BF16), mk_ref[...])
            yield
            ps = []
            for h in range(NH):
                seg = sc[:, h * MEM:(h + 1) * MEM]
                e = jnp.exp(seg - jnp.max(seg, axis=-1, keepdims=True))
                ps.append((e / jnp.sum(e, axis=-1, keepdims=True)).astype(BF16))
            p = jnp.concatenate(ps, axis=1)
            yield
            xa = _dot(p, mv_ref[...])
            yield
            mix_ref[rs, 3 * GW:4 * GW] = (xa * _silu(PB(AG, rs))).astype(BF16)
            yield

    def s5_stages():
        lanes = [(slice(j * 128, (j + 1) * 128), slice(NS + j * 128, NS + (j + 1) * 128))
                 for j in range(NS // 128)]
        bpc = L // LS
        nblk = TB // LS
        su = PB(SU)
        bu_ref[...] = _dot(su.astype(BF16), bt_ref[...])
        yield
        for c in range(nc + 1):
            if c < nc:
                for lre, lim in lanes:
                    br = bu_ref[crow[c], lre]
                    bi = bu_ref[crow[c], lim]
                    wr = jnp.tile(tab_ref[0, :, lre], (bpc, 1))
                    wi = jnp.tile(tab_ref[1, :, lre], (bpc, 1))
                    xcat_ref[crow[c], lre] = (wr * br - wi * bi).astype(BF16)
                    xcat_ref[crow[c], lim] = (wr * bi + wi * br).astype(BF16)
            if c > 0:
                z = _dot(tri_ext, xcat_ref[crow[c - 1], :])
                bu_ref[crow[c - 1], :] = z[0:L]
                car_ref[(c - 1) * bpc:c * bpc, :] = z[L:L + bpc]
            yield
        for _ in range(nc - 1):
            yield
        rowb = lax.broadcasted_iota(jnp.int32, (nblk, 128), 0)
        ers, eis, prs, pis, c0s = [], [], [], [], []
        for lre, lim in lanes:
            zr = car_ref[:, lre]
            zi = car_ref[:, lim]
            pr = tab_ref[2, LS - 1:LS, lre]
            pi = tab_ref[3, LS - 1:LS, lre]
            er = pr * zr - pi * zi
            ei = pr * zi + pi * zr
            pr = tab_ref[4, LS - 1:LS, lre]
            pi = tab_ref[5, LS - 1:LS, lre]
            c0r = xre_ref[0:1, lre]
            c0i = xim_ref[0:1, lre]
            ers.append(er + jnp.where(rowb == 0, pr * c0r - pi * c0i, 0.0))
            eis.append(ei + jnp.where(rowb == 0, pr * c0i + pi * c0r, 0.0))
            prs.append(pr)
            pis.append(pi)
            c0s.append((c0r, c0i))
        yield
        assert LS == 8 and nblk % 8 == 0
        ngrp = nblk // 8
        row8 = lax.broadcasted_iota(jnp.int32, (8, 128), 0)
        ers = [[v[g * 8:(g + 1) * 8] for g in range(ngrp)] for v in ers]
        eis = [[v[g * 8:(g + 1) * 8] for g in range(ngrp)] for v in eis]
        for k in range(3):
            d = 1 << k
            for j in range(len(lanes)):
                pr, pi = prs[j], pis[j]
                for g in range(ngrp):
                    er, ei = ers[j][g], eis[j][g]
                    sr = jnp.where(row8 >= d, pltpu.roll(er, d, 0), 0.0)
                    si = jnp.where(row8 >= d, pltpu.roll(ei, d, 0), 0.0)
                    ers[j][g], eis[j][g] = er + pr * sr - pi * si, ei + pr * si + pi * sr
                prs[j], pis[j] = pr * pr - pi * pi, 2.0 * (pr * pi)
            yield
        for g in range(1, ngrp):
            for j, (lre, lim) in enumerate(lanes):
                qr = tab_ref[6, :, lre]
                qi = tab_ref[7, :, lre]
                cr = jnp.broadcast_to(ers[j][g - 1][7:8, :], (8, 128))
                ci_ = jnp.broadcast_to(eis[j][g - 1][7:8, :], (8, 128))
                ers[j][g] = ers[j][g] + (qr * cr - qi * ci_)
                eis[j][g] = eis[j][g] + (qr * ci_ + qi * cr)
            if g % 3 == 0 or g == ngrp - 1:
                yield
        ers = [jnp.concatenate(v, axis=0) for v in ers]
        eis = [jnp.concatenate(v, axis=0) for v in eis]
        for j, (lre, lim) in enumerate(lanes):
            xre_ref[:, lre] = jnp.broadcast_to(ers[j][nblk - 1:nblk, :], (8, 128))
            xim_ref[:, lre] = jnp.broadcast_to(eis[j][nblk - 1:nblk, :], (8, 128))
            cr = jnp.where(rowb == 0, c0s[j][0], pltpu.roll(ers[j], 1, 0))
            ci_ = jnp.where(rowb == 0, c0s[j][1], pltpu.roll(eis[j], 1, 0))
            ar = tab_ref[4, 0:1, lre]
            ai = tab_ref[5, 0:1, lre]
            car_ref[:, lre] = ar * cr - ai * ci_
            car_ref[:, lim] = ar * ci_ + ai * cr
        yield
        for c in range(nc):
            for lre, lim in lanes:
                cr = jnp.concatenate(
                    [jnp.broadcast_to(car_ref[c * bpc + j:c * bpc + j + 1, lre], (LS, 128))
                     for j in range(bpc)], axis=0)
                ci_ = jnp.concatenate(
                    [jnp.broadcast_to(car_ref[c * bpc + j:c * bpc + j + 1, lim], (LS, 128))
                     for j in range(bpc)], axis=0)
                zr = bu_ref[crow[c], lre] + cr
                zi = bu_ref[crow[c], lim] + ci_
                pr = jnp.tile(tab_ref[2, :, lre], (bpc, 1))
                pi = jnp.tile(tab_ref[3, :, lre], (bpc, 1))
                xcat_ref[crow[c], lre] = (pr * zr - pi * zi).astype(BF16)
                xcat_ref[crow[c], lim] = (pr * zi + pi * zr).astype(BF16)
            yield
        sy = _dot(xcat_ref[...], ct_ref[...]) + dsk_ref[...] * su
        yield
        sy = _gelu_tanh(sy)
        gate = _dot(sy.astype(BF16), wglu_ref[...])
        yield
        mix_ref[:, 2 * GW:3 * GW] = (sy * _sigmoid(gate) * _silu(PB(SG))).astype(BF16)
        yield

    def retention_stages():
        rq = _rope(PB(RQ), cos_ref[...], s1_ref[...], s2_ref[...])
        rk = _rope(PB(RK), cos_ref[...], s1_ref[...], s2_ref[...])
        rvf = PB(RV)
        rv = rvf.astype(BF16)
        rqb = rq.astype(BF16)
        rkb = rk.astype(BF16)
        kst = [_stack_heads(rkb[crow[c]], hm) for c in range(nc)]
        vst = [_stack_heads(rv[crow[c]], hm) for c in range(nc)]
        yield
        inner = [_dot_nt(rqb[crow[c]], kst[c]) for c in range(nc)]
        yield
        pmat = [(inner[c] * dec_ref[...]).astype(BF16) for c in range(nc)]
        kd = [(rk[crow[c]] * kdec_ref[...]).astype(BF16) for c in range(nc)]
        yield
        rloc = [_dot(pmat[c], vst[c]) for c in range(nc)]
        upd = [_dot_tn(kd[c], rv[crow[c]]) for c in range(nc)]
        yield
        st = [s_ref[...]]
        for c in range(nc):
            st.append(st[c] * cdec_ref[...] + jnp.where(bd, upd[c], 0.0))
        s_ref[...] = st[nc]
        qd = [(rq[crow[c]] * qdec_ref[...]).astype(BF16) for c in range(nc)]
        yield
        ost = [_dot(qd[c], st[c].astype(BF16)) for c in range(nc)]
        yield
        ro = jnp.concatenate([rloc[c] + ost[c] for c in range(nc)], axis=0)
        mix_ref[:, 0:GW] = (_head_norm(ro, retgn_ref[...], avg) * _silu(PB(RG))).astype(BF16)
        yield

    def mlstm_stages():
        ig_r = gates_t[0:8] + bi_ref[...]
        lf_r = _log_sigmoid(gates_t[8:16] + bf_ref[...])
        yield
        b_r = [_dot_x3(lf_r[:, crow[c]], tri_u) for c in range(nc)]
        yield
        gd_r = [ig_r[:, crow[c]] - b_r[c] for c in range(nc)]
        lane_r = lax.broadcasted_iota(jnp.int32, (8, L), 1)
        cm_r = list(gd_r)
        for k in range(7):
            d = 1 << k
            cm_r = [jnp.maximum(v, jnp.where(lane_r >= d, pltpu.roll(v, d, 1), NEG_INF))
                    for v in cm_r]
            yield
        m_prev = [m_ref[...]]
        mt_r = []
        for c in range(nc):
            mt = jnp.maximum(b_r[c] + m_prev[c], b_r[c] + cm_r[c])
            mt_r.append(mt)
            m_prev.append(jnp.broadcast_to(mt[:, L - 1:L], (8, L)))
            yield
        m_ref[...] = m_prev[nc]
        cols = []
        for c in range(nc):
            bm = b_r[c] - mt_r[c]
            ws = jnp.exp(b_r[c] + m_prev[c] - mt_r[c])
            wl = jnp.exp(gd_r[c] + jnp.broadcast_to(bm[:, L - 1:L], (8, L)))
            emt = jnp.exp(-mt_r[c])
            cols.append(jnp.concatenate([bm, ws, wl, emt, jnp.zeros((L - 32, L), F32)], axis=0).T)
        mqf = PB(MQ)
        mq = mqf.astype(BF16)
        mkf = PB(MK)
        mvf = PB(MV)
        mv_ = mvf.astype(BF16)
        mkb = mkf.astype(BF16)
        kst = [_stack_heads(mkb[crow[c]], hm) for c in range(nc)]
        vst = [_stack_heads(mv_[crow[c]], hm) for c in range(nc)]
        yield
        sraw = [_dot_nt(mq[crow[c]], kst[c]) for c in range(nc)]
        yield
        smat, den_i = [], []
        for c in range(nc):
            parts, dens = [], []
            for h in range(NH):
                arg = jnp.where(causal, cols[c][:, h:h + 1] + gd_r[c][h:h + 1, :], NEG_INF)
                s_h = sraw[c][:, h * L:(h + 1) * L] * jnp.exp(arg)
                dens.append(jnp.sum(s_h, axis=-1, keepdims=True))
                parts.append(s_h.astype(BF16))
            den_i.append(dens)
            smat.append(jnp.concatenate(parts, axis=1))
            yield
        ones_blk = jnp.ones((L, 128), BF16)
        def per_head(columns):
            first = lane128 < DH
            return jnp.concatenate([jnp.where(first, columns[0], columns[1]),
                                    jnp.where(first, columns[2], columns[3])], axis=1)

        kws = [(mkf[crow[c]] * per_head([cols[c][:, 16 + h:17 + h] for h in range(NH)])
                ).astype(BF16) for c in range(nc)]
        yield
        rloc = [_dot(smat[c], vst[c]) for c in range(nc)]
        u = [_dot_tn(kws[c], jnp.concatenate([mv_[crow[c]], ones_blk], axis=1))
             for c in range(nc)]
        yield
        nmask = (lax.broadcasted_iota(jnp.int32, (GW, 128), 0) // DH
                 == lax.broadcasted_iota(jnp.int32, (GW, 128), 1))
        cst = [c_ref[...]]
        nst = [n_ref[...]]
        for c in range(nc):
            wsl256 = jnp.zeros((1, GW), F32)
            wsl128 = jnp.zeros((1, 128), F32)
            for h in range(NH):
                wsl = cols[c][L - 1:L, 8 + h:9 + h]
                wsl256 = wsl256 + jnp.where(hm[h], wsl, 0.0)
                wsl128 = wsl128 + jnp.where(lane128 == h, wsl, 0.0)
            cst.append(cst[c] * wsl256 + jnp.where(bd, u[c][:, :GW], 0.0))
            nst.append(nst[c] * wsl128 + jnp.where(nmask, u[c][:, GW:], 0.0))
        c_ref[...] = cst[nc]
        n_ref[...] = nst[nc]
        yield
        qc = [_dot(mq[crow[c]], cst[c].astype(BF16)) for c in range(nc)]
        qn = [_dot(mq[crow[c]], nst[c].astype(BF16)) for c in range(nc)]
        yield
        mhs = []
        for c in range(nc):
            wsc = [cols[c][:, 8 + h:9 + h] for h in range(NH)]
            rdn = []
            for h in range(NH):
                den = den_i[c][h] + wsc[h] * qn[c][:, h:h + 1]
                rdn.append(1.0 / jnp.maximum(jnp.abs(den), cols[c][:, 24 + h:25 + h]))
            mhs.append((rloc[c] + per_head(wsc) * qc[c]) * per_head(rdn))
            yield
        mh = jnp.concatenate(mhs, axis=0) * _sigmoid(PB(MO))
        mix_ref[:, GW:2 * GW] = (_head_norm(mh, mlgn_ref[...], avg) * _silu(PB(MG))).astype(BF16)
        yield

    pending = [mlstm_stages(), s5_stages(), retention_stages(), xattn_stages()]
    while pending:
        pending = [stage for stage in pending if next(stage, _DONE) is not _DONE]

    y = x_ref[0] + _dot(mix_ref[...], wout_ref[...])
    if last_layer:
        y = _rms_norm(y, fnw_ref[...])
    y_ref[0] = y

    @pl.when(t == n_tblocks - 1)
    def _final():
        s_fin = s_ref[...]
        c_fin = c_ref[...].T
        n_fin = n_ref[...].T
        for h in range(NH):
            blk = slice(h * DH, (h + 1) * DH)
            rets_ref[0, h] = s_fin[blk, blk]
            mlc_ref[0, h] = c_fin[blk, blk]
        mln_ref[0] = jnp.concatenate([n_fin[h:h + 1, h * DH:(h + 1) * DH] for h in range(NH)],
                                     axis=0)
        mlm_ref[0] = m_ref[...]
        s5re_ref[0] = xre_ref[...]
        s5im_ref[0] = xim_ref[...]


def _prompt_layer(layer, x, mem, w, consts, last_layer, prev):
    bsz, seq, _ = x.shape
    depth = w["w_in"].shape[0]
    nt = seq // TB
    full = lambda shape: pl.BlockSpec(shape, lambda b, t: (0,) * len(shape),
                                      pipeline_mode=pl.Buffered(1))
    lyr = lambda shape: pl.BlockSpec((None,) + shape, lambda b, t: (layer,) + (0,) * len(shape),
                                     pipeline_mode=pl.Buffered(1))
    tok = lambda width: pl.BlockSpec((TB, width), lambda b, t: (t, 0))
    per_b = lambda r, c: pl.BlockSpec((1, r, c), lambda b, t: (b, 0, 0))
    in_specs = [
        pl.BlockSpec((1, TB, D), lambda b, t: (b, t, 0)),
        per_b(MEM, D),
        lyr((1, D)), lyr((DP, D)), lyr((NGATE, D)), lyr((D, D)), lyr((D, 2 * GW)),
        tok(128), tok(128), tok(128),
        full((L, NH * L)), full((L, GW)), full((L, GW)), full((1, GW)),
        lyr((1, GW)), lyr((1, GW)), lyr((8, TB)), lyr((8, TB)),
        lyr((NTAB, LS, NS)), lyr((GW, 2 * NS)), lyr((2 * NS, GW)),
        lyr((1, GW)), lyr((GW, GW)), full((1, D)),
    ] + [pl.BlockSpec(memory_space=pl.ANY)] * len(prev)
    n_in = len(in_specs) - len(prev)
    state_shapes = [(NH, DH, DH),
                    (NH, DH, DH),
                    (NH, DH),
                    (8, 128),
                    (8, NS),
                    (8, NS),
                    (GW, MEM),
                    (GW, MEM)]
    assert len(state_shapes) == N_STATE_OUT and len(prev) in (0, N_STATE_OUT)
    out_shape = ((jax.ShapeDtypeStruct((bsz, seq, D), F32),)
                 + tuple(jax.ShapeDtypeStruct((depth, bsz) + s, F32) for s in state_shapes))
    state_spec = lambda s: pl.BlockSpec((None, 1) + s,
                                        lambda b, t: (layer, b) + (0,) * len(s))
    out_specs = ((pl.BlockSpec((1, TB, D), lambda b, t: (b, t, 0)),)
                 + tuple(state_spec(s) for s in state_shapes))
    scratch = [
        pltpu.VMEM((TB, DP), F32),
        pltpu.VMEM((TB, D), BF16),
        pltpu.VMEM((GW, GW), F32),
        pltpu.VMEM((GW, GW), F32),
        pltpu.VMEM((GW, 128), F32),
        pltpu.VMEM((8, 128), F32),
        pltpu.VMEM((8, NS), F32),
        pltpu.VMEM((8, NS), F32),
        pltpu.VMEM((NH * MEM, GW), BF16),
        pltpu.VMEM((NH * MEM, GW), BF16),
        pltpu.VMEM((TB, 2 * NS), F32),
        pltpu.VMEM((TB, 2 * NS), BF16),
        pltpu.VMEM((TB // LS, 2 * NS), F32),
    ]
    kern = functools.partial(_prompt_kernel, last_layer=last_layer, n_tblocks=nt, n_prev=len(prev))
    return pl.pallas_call(
        kern, grid=(bsz, nt), in_specs=in_specs, out_specs=out_specs, out_shape=out_shape,
        scratch_shapes=scratch, name="prompt_layer",
        input_output_aliases={n_in + i: 1 + i for i in range(len(prev))},
        compiler_params=pltpu.CompilerParams(
            dimension_semantics=("arbitrary", "arbitrary"), vmem_limit_bytes=VMEM_LIMIT),
    )(x, mem, w["norm_w"], w["w_in"], w["w_gate"], w["w_out"], w["w_mem_kv"],
      consts["cos_p"], consts["s1_p"], consts["s2_p"],
      consts["dec"], consts["qdec"], consts["kdec"], consts["cdec"],
      w["ret_gn"], w["ml_gn"], w["b_i8"], w["b_f8"],
      w["tab"], w["bt"], w["ct"], w["s5_d"], w["w_glu"], w["final_norm_w"], *prev)


SROWS = 16
CACHE_BUFS = 4


def _sample_kernel_t(x_ref, normw_ref, win_ref, wg_ref, wout_ref,
                     cos_ref, s1_ref, s2_ref, gam_ref, gam8_ref,
                     retgn_ref, mlgn_ref, bi_ref, bf_ref,
                     abt_ref, bt_ref, ct_ref, dsk_ref, wglu_ref, fnw_ref,
                     m0_ref, n0_ref, x0re_ref, x0im_ref,
                     rets_ref, mlc_ref, ck_ref, cv_ref,
                     y_ref, retn_ref, mlcn_ref, mlnn_ref, mlmn_ref, s5re_ref, s5im_ref,
                     hs_ref, proj_ref, qgt_ref, rkt_ref, rvt_ref, mqt_ref, mkt_ref, mvt_ref,
                     wi_ref, ws_ref, ot_ref, cqt_ref, qa8_ref, xa_ref,
                     ckb_ref, cvb_ref, csem_ref,
                     *, n_layers, n_blocks):
    layer = pl.program_id(0)
    g = pl.program_id(1)
    nsamp = x_ref.shape[0]
    ones_bd = _ones_matrix()
    avg = _avg_matrix()
    mask8 = (lax.broadcasted_iota(jnp.int32, (8, GW), 0)
             == lax.broadcasted_iota(jnp.int32, (8, GW), 1) // DH)

    def P(blk):
        return proj_ref[:, blk * GW:(blk + 1) * GW]

    step = layer * n_blocks + g
    slot = step % CACHE_BUFS

    def cache_copies(at_step, at_slot):
        lyr_i = at_step // n_blocks
        first = (at_step % n_blocks) * SB
        rows = pl.ds(first if isinstance(first, int) else pl.multiple_of(first, SB), SB)
        return (pltpu.make_async_copy(ck_ref.at[lyr_i, rows], ckb_ref.at[at_slot],
                                      csem_ref.at[0, at_slot]),
                pltpu.make_async_copy(cv_ref.at[lyr_i, rows], cvb_ref.at[at_slot],
                                      csem_ref.at[1, at_slot]))

    @pl.when(step == 0)
    def _prefetch():
        for s in range(CACHE_BUFS):
            for cp in cache_copies(s, s):
                cp.start()

    @pl.when((layer == 0) & (g == 0))
    def _load_x():
        hs_ref[...] = x_ref[...]

    @pl.when(g == 0)
    def _pre():
        hn = _rms_norm(hs_ref[...], normw_ref[...]).astype(BF16)
        proj_ref[...] = _dot_nt(hn, win_ref[...])
        cosr, s1, s2 = cos_ref[...], s1_ref[...], s2_ref[...]
        qgt_ref[...] = (_rope(P(RQ), cosr, s1, s2) * gam_ref[...]).T
        rkt_ref[...] = _rope(P(RK), cosr, s1, s2).T
        rvt_ref[...] = P(RV).T
        mqt_ref[...] = P(MQ).T
        mkt_ref[...] = P(MK).T
        mvt_ref[...] = P(MV).T
        gt = _dot_nt(wg_ref[...], hn)
        ig = gt[0:8] + bi_ref[...]
        lf = _log_sigmoid(gt[8:16] + bf_ref[...])
        a = lf + m0_ref[...]
        mt = jnp.maximum(a, ig)
        wi_ref[...] = jnp.exp(ig - mt)
        ws_ref[...] = jnp.exp(a - mt)
        mlmn_ref[...] = mt
        ot_ref[...] = jnp.zeros_like(ot_ref)
        qa8_ref[...] = jnp.where(mask8[None], P(AQ)[:, None, :], 0.0).reshape(8 * nsamp, GW)
        but = _dot_tn(bt_ref[...], P(SU).T.astype(BF16))
        are, aim = abt_ref[0:NS, :], abt_ref[NS:2 * NS, :]
        x0r, x0i = x0re_ref[...], x0im_ref[...]
        s5re_ref[...] = are * x0r - aim * x0i + but[0:NS]
        s5im_ref[...] = are * x0i + aim * x0r + but[NS:2 * NS]

    head = g // (GW // SROWS // NH)
    hrow = pl.ds(pl.multiple_of(head * DH, DH), DH)
    gam_row = gam8_ref[pl.ds(head, 1), :]
    ws_row = ws_ref[pl.ds(head, 1), :]
    wi_row = wi_ref[pl.ds(head, 1), :]
    v_slab = rvt_ref[hrow, :]
    q_slab = mqt_ref[hrow, :]
    k_slab = mkt_ref[hrow, :]
    o_acc = jnp.zeros((DH, nsamp), F32)
    for i in range(SROWS):
        r = pl.ds(g * SROWS + i, 1)
        s_t = rets_ref[i]
        o_acc = o_acc + qgt_ref[r, :] * s_t
        retn_ref[i] = gam_row * s_t + rkt_ref[r, :] * v_slab
        c_t = mlc_ref[i]
        cqt_ref[r, :] = jnp.sum(c_t * q_slab, axis=0, keepdims=True)
        mlcn_ref[i] = ws_row * c_t + (wi_row * mvt_ref[r, :]) * k_slab
    ot_ref[hrow, :] = ot_ref[hrow, :] + o_acc

    tiles = [pl.ds(pl.multiple_of((g * SB + i) * 8, 8), 8) for i in range(SB)]
    q8 = [qa8_ref[tiles[i], :].astype(BF16) for i in range(SB)]
    for cp in cache_copies(step, slot):
        cp.wait()
    sc = [_dot(q8[i], ckb_ref[slot, i].astype(BF16)) for i in range(SB)]
    ps = []
    for i in range(SB):
        e = jnp.exp(sc[i] - jnp.max(sc[i], axis=-1, keepdims=True))
        ps.append((e / jnp.sum(e, axis=-1, keepdims=True)).astype(BF16))
    ta = [_dot_nt(ps[i], cvb_ref[slot, i].astype(BF16)) for i in range(SB)]
    xa_ref[pl.ds(pl.multiple_of(g * SB, SB), SB), :] = jnp.concatenate(
        [jnp.sum(jnp.where(mask8, ta[i], 0.0), axis=0, keepdims=True) for i in range(SB)], axis=0)

    @pl.when(step + CACHE_BUFS < n_layers * n_blocks)
    def _refill():
        for cp in cache_copies(step + CACHE_BUFS, slot):
            cp.start()

    @pl.when(g == n_blocks - 1)
    def _post():
        cosr, s1, s2 = cos_ref[...], s1_ref[...], s2_ref[...]
        rq = _rope(P(RQ), cosr, s1, s2)
        rk = _rope(P(RK), cosr, s1, s2)
        ro = _dot_x2(rq * rk, ones_bd) * P(RV) + ot_ref[...].T
        ret_out = _head_norm(ro, retgn_ref[...], avg) * _silu(P(RG))
        mht = []
        for h in range(NH):
            rows = slice(h * DH, (h + 1) * DH)
            wi = wi_ref[h:h + 1, :]
            ws = ws_ref[h:h + 1, :]
            emt = jnp.exp(-mlmn_ref[h:h + 1, :])
            qt, kt, n0 = mqt_ref[rows, :], mkt_ref[rows, :], n0_ref[rows, :]
            s = jnp.sum(qt * kt, axis=0, keepdims=True) * wi
            den = s + ws * jnp.sum(n0 * qt, axis=0, keepdims=True)
            mht.append((s * mvt_ref[rows, :] + ws * cqt_ref[rows, :])
                       / jnp.maximum(jnp.abs(den), emt))
            mlnn_ref[rows, :] = ws * n0 + wi * kt
        mh = jnp.concatenate(mht, axis=0).T * _sigmoid(P(MO))
        ml_out = _head_norm(mh, mlgn_ref[...], avg) * _silu(P(MG))
        su = P(SU)
        xcat = jnp.concatenate([s5re_ref[...], s5im_ref[...]], axis=0).astype(BF16)
        sy = _dot_tn(xcat, ct_ref[...]) + dsk_ref[...] * su
        sy = _gelu_tanh(sy)
        sy = sy * _sigmoid(_dot(sy.astype(BF16), wglu_ref[...]))
        s5_out = sy * _silu(P(SG))
        xa_out = xa_ref[...] * _silu(P(AG))
        mix = jnp.concatenate([ret_out, ml_out, s5_out, xa_out], axis=1).astype(BF16)
        y = hs_ref[...] + _dot(mix, wout_ref[...])
        hs_ref[...] = y

        @pl.when(layer == n_layers - 1)
        def _emit():
            y_ref[...] = _rms_norm(y, fnw_ref[...])


def _sample_layers_t(x, st, w, consts):
    nsamp = x.shape[0]
    depth = w["w_in"].shape[0]
    nb = GW // SROWS
    assert nsamp == nb * SB and nsamp == 128
    once = lambda shape: pl.BlockSpec(shape, lambda l, g: (0,) * len(shape),
                                      pipeline_mode=pl.Buffered(1))
    lyr = lambda shape: pl.BlockSpec((None,) + shape, lambda l, g: (l,) + (0,) * len(shape))
    lyr_out = lyr
    srows = pl.BlockSpec((None, SROWS, DH, nsamp), lambda l, g: (l, g, 0, 0))
    cache = pl.BlockSpec(memory_space=pl.ANY)
    in_specs = [
        once((nsamp, D)), lyr((1, D)), lyr((DP, D)), lyr((NGATE, D)), lyr((D, D)),
        once((1, 128)), once((1, 128)), once((1, 128)), once((1, GW)), once((8, nsamp)),
        lyr((1, GW)), lyr((1, GW)), lyr((8, nsamp)), lyr((8, nsamp)),
        lyr((2 * NS, nsamp)), lyr((GW, 2 * NS)), lyr((2 * NS, GW)),
        lyr((1, GW)), lyr((GW, GW)), once((1, D)),
        lyr((8, nsamp)), lyr((GW, nsamp)), lyr((NS, nsamp)), lyr((NS, nsamp)),
        srows, srows, cache, cache,
    ]
    out_shape = (
        jax.ShapeDtypeStruct((nsamp, D), F32),
        jax.ShapeDtypeStruct((depth, GW, DH, nsamp), F32),
        jax.ShapeDtypeStruct((depth, GW, DH, nsamp), F32),
        jax.ShapeDtypeStruct((depth, GW, nsamp), F32),
        jax.ShapeDtypeStruct((depth, 8, nsamp), F32),
        jax.ShapeDtypeStruct((depth, NS, nsamp), F32),
        jax.ShapeDtypeStruct((depth, NS, nsamp), F32),
    )
    out_specs = (
        pl.BlockSpec((nsamp, D), lambda l, g: (0, 0)), srows, srows,
        lyr_out((GW, nsamp)), lyr_out((8, nsamp)), lyr_out((NS, nsamp)), lyr_out((NS, nsamp)),
    )
    scratch = [
        pltpu.VMEM((nsamp, D), F32),
        pltpu.VMEM((nsamp, DP), F32),
        pltpu.VMEM((GW, nsamp), F32), pltpu.VMEM((GW, nsamp), F32), pltpu.VMEM((GW, nsamp), F32),
        pltpu.VMEM((GW, nsamp), F32), pltpu.VMEM((GW, nsamp), F32), pltpu.VMEM((GW, nsamp), F32),
        pltpu.VMEM((8, nsamp), F32), pltpu.VMEM((8, nsamp), F32),
        pltpu.VMEM((GW, nsamp), F32), pltpu.VMEM((GW, nsamp), F32),
        pltpu.VMEM((8 * nsamp, GW), F32), pltpu.VMEM((nsamp, GW), F32),
        pltpu.VMEM((CACHE_BUFS, SB, GW, MEM), F32),
        pltpu.VMEM((CACHE_BUFS, SB, GW, MEM), F32),
        pltpu.SemaphoreType.DMA((2, CACHE_BUFS)),
    ]
    kern = functools.partial(_sample_kernel_t, n_layers=depth, n_blocks=nb)
    return pl.pallas_call(
        kern, grid=(depth, nb), in_specs=in_specs, out_specs=out_specs, out_shape=out_shape,
        scratch_shapes=scratch, name="sample_layers",
        compiler_params=pltpu.CompilerParams(
            dimension_semantics=("arbitrary", "arbitrary"), vmem_limit_bytes=VMEM_LIMIT),
    )(x, w["norm_w"], w["w_in"], w["w_gate"], w["w_out"],
      consts["cos_s"], consts["s1_s"], consts["s2_s"], consts["gam"], consts["gam8"],
      w["ret_gn"], w["ml_gn"], w["b_i8"], w["b_f8"],
      w["abt"], w["bt"], w["ct"], w["s5_d"], w["w_glu"], w["final_norm_w"],
      st["m"], st["n"], st["s5_re"], st["s5_im"], st["ret"], st["c"], st["mem_k"], st["mem_v"])


def _rope_tables(pos):
    half = DH // 2
    inv = ROPE_BASE ** (-np.arange(half, dtype=np.float64) / half)
    ang = np.asarray(pos, np.float64)[:, None] * inv[None, :]
    cos, sin = np.cos(ang), np.sin(ang)
    zero = np.zeros_like(sin)
    c = np.tile(np.concatenate([cos, cos], axis=-1), (1, 2))
    s1 = np.tile(np.concatenate([zero, sin], axis=-1), (1, 2))
    s2 = np.tile(np.concatenate([-sin, zero], axis=-1), (1, 2))
    return tuple(jnp.asarray(t, F32) for t in (c, s1, s2))


def _constants(seq):
    lg = np.log1p(-np.power(2.0, -5.0 - np.arange(NH, dtype=np.float64)))[:, None]
    idx = np.arange(L, dtype=np.float64)
    diff = idx[:, None] - idx[None, :]
    decay = np.where(diff >= 0, np.exp(lg[:, :, None] * np.maximum(diff, 0.0)), 0.0)
    rep = lambda t: np.repeat(t, DH, axis=0).T
    consts = {
        "dec": np.transpose(decay, (1, 0, 2)).reshape(L, NH * L),
        "qdec": rep(np.exp(lg * (idx + 1.0))),
        "kdec": rep(np.exp(lg * (L - 1.0 - idx))),
        "cdec": rep(np.exp(lg * L)),
        "gam": rep(np.exp(lg * 1.0)),
        "gam8": np.pad(np.broadcast_to(np.exp(lg), (NH, 128)), ((0, 8 - NH), (0, 0))),
    }
    consts = {k: jnp.asarray(v, F32) for k, v in consts.items()}
    consts["cos_p"], consts["s1_p"], consts["s2_p"] = _rope_tables(np.arange(seq))
    consts["cos_s"], consts["s1_s"], consts["s2_s"] = _rope_tables(PAST_LEN + np.arange(1))
    return consts


def _pack_w_in(w_in):
    wt = jnp.swapaxes(w_in, 1, 2)
    sizes = (GW,) * 9 + (NH, NH) + (GW,) * 4
    offs = np.concatenate([[0], np.cumsum(sizes)])
    seg = [wt[:, int(offs[i]):int(offs[i + 1]), :] for i in range(len(sizes))]
    scale = DH ** -0.5
    pad8 = lambda t: jnp.pad(t, ((0, 0), (0, 8 - t.shape[1]), (0, 0)))
    blocks = [seg[11],
              seg[0], seg[1] * scale, seg[2], seg[3],
              seg[4], seg[5] * scale, seg[6], seg[7], seg[8],
              seg[12], seg[13] * scale, seg[14]]
    gates = jnp.concatenate([pad8(seg[9]), pad8(seg[10])], axis=1)
    return jnp.concatenate(blocks, axis=1).astype(BF16), gates.astype(BF16)


def kernel(x_prompt, x_sample, mem_prompt, state_ret, state_mlstm_c, state_mlstm_n, state_mlstm_m,
           state_s5_re, state_s5_im, cache_mem_k, cache_mem_v,
           norm_w, w_in, ret_gn, ml_b_i, ml_b_f, ml_gn,
           s5_a_re, s5_a_im, s5_log_dt, s5_b_re, s5_b_im, s5_c_re, s5_c_im, s5_d, s5_w_glu,
           w_mem_k, w_mem_v, w_out, final_norm_w):
    depth = norm_w.shape[0]
    bp, seq, _ = x_prompt.shape
    bs = x_sample.shape[0]
    consts = _constants(seq)
    abt, bt, ct, tab = _s5_prepare(s5_a_re, s5_a_im, s5_log_dt, s5_b_re, s5_b_im, s5_c_re, s5_c_im)
    rows8 = lambda t: jnp.pad(jnp.broadcast_to(t[:, :, None], (depth, NH, TB)),
                              ((0, 0), (0, 8 - NH), (0, 0)))
    w_main, w_gate = _pack_w_in(w_in)
    w = {
        "norm_w": norm_w[:, None], "w_in": w_main, "w_gate": w_gate, "w_out": w_out.astype(BF16),
        "w_mem_kv": jnp.concatenate([w_mem_k, w_mem_v], axis=-1).astype(BF16),
        "ret_gn": ret_gn[:, None], "ml_gn": ml_gn[:, None],
        "b_i8": rows8(ml_b_i), "b_f8": rows8(ml_b_f),
        "abt": abt, "tab": tab, "bt": bt, "ct": ct,
        "s5_d": s5_d[:, None], "w_glu": s5_w_glu.astype(BF16),
        "final_norm_w": final_norm_w[None],
    }

    hp = x_prompt
    states = ()
    for l in range(depth):
        hp, *states = _prompt_layer(l, hp, mem_prompt, w, consts, l == depth - 1, tuple(states))
    ret_p, mlc_p, mln_p, ms, xr, xi, mk, mv = states
    mlm_p = ms[:, :, :NH, 0]
    s5re_p = xr[:, :, 0].reshape(depth, bp, S5G, S5P)
    s5im_p = xi[:, :, 0].reshape(depth, bp, S5G, S5P)
    memk_p = jnp.transpose(mk.reshape(depth, bp, NH, DH, MEM), (0, 1, 4, 2, 3))
    memv_p = jnp.transpose(mv.reshape(depth, bp, NH, DH, MEM), (0, 1, 4, 2, 3))

    st = {
        "m": jnp.pad(jnp.swapaxes(state_mlstm_m, 1, 2), ((0, 0), (0, 8 - NH), (0, 0))),
        "n": jnp.transpose(state_mlstm_n, (0, 2, 3, 1)).reshape(depth, GW, bs),
        "s5_re": jnp.transpose(state_s5_re, (0, 2, 3, 1)).reshape(depth, NS, bs),
        "s5_im": jnp.transpose(state_s5_im, (0, 2, 3, 1)).reshape(depth, NS, bs),
        "ret": jnp.transpose(state_ret, (0, 2, 3, 4, 1)).reshape(depth, GW, DH, bs),
        "c": jnp.transpose(state_mlstm_c, (0, 2, 3, 4, 1)).reshape(depth, GW, DH, bs),
        "mem_k": jnp.transpose(cache_mem_k, (0, 1, 3, 4, 2)).reshape(depth, bs, GW, MEM),
        "mem_v": jnp.transpose(cache_mem_v, (0, 1, 3, 4, 2)).reshape(depth, bs, GW, MEM),
    }
    hs, rn, cn, nn, mn, sr, si = _sample_layers_t(x_sample.reshape(bs, D), st, w, consts)
    back5 = lambda t: jnp.transpose(t.reshape(depth, NH, DH, DH, bs), (0, 4, 1, 2, 3))
    back4 = lambda t, a, b: jnp.transpose(t.reshape(depth, a, b, bs), (0, 3, 1, 2))
    return (hp, hs.reshape(bs, 1, D),
            ret_p, back5(rn), mlc_p, back5(cn),
            mln_p, back4(nn, NH, DH), mlm_p, jnp.swapaxes(mn[:, :NH], 1, 2),
            s5re_p, back4(sr, S5G, S5P), s5im_p, back4(si, S5G, S5P),
            memk_p, memv_p)
```
